```python
import math, functools
import jax, jax.numpy as jnp
from jax import lax
import numpy as np

D_MODEL = 2048
BATCH = 2
SEQ = 8192
DEPTH = 1
DEC_BATCH = 128
DEC_SEQ = 4
PAST_LEN = 16384
PAGE_SIZE = 128

N_BRANCH = 3
BR_W = 1024
GDN_H = 8
GDN_DK = 128
GDN_DV = 128
GDN_QK = GDN_H * GDN_DK
GDN_W = GDN_H * GDN_DV
GDN_CONV_CH = 2 * GDN_QK + GDN_W
CONV_W = 4
GDN_CHUNK = 64
SWA_H = 16
SWA_KV = 2
SWA_G = SWA_H // SWA_KV
SWA_DH = 64
SWA_QW = SWA_H * SWA_DH
SWA_KVW = SWA_KV * SWA_DH
WINDOW = 128
N_BUCKETS = 32
MAX_DISTANCE = 128
N_MEM = 256
MEM_H = 4
MEM_DH = 256
MEM_W = MEM_H * MEM_DH
NORM_EPS = 1e-6
IN_SIZES = (GDN_QK, GDN_QK, GDN_W, GDN_W, GDN_H, GDN_H,
            SWA_QW, SWA_KVW, SWA_KVW, SWA_QW,
            MEM_W, MEM_W,
            N_BRANCH * D_MODEL)
IN_COLS = sum(IN_SIZES)

kernel_name = 'gdn_swa_sink_memxattn_gated_hybrid_step'


def rmsnorm(x, w):
    xf = x.astype(jnp.float32)
    y = xf * lax.rsqrt(jnp.mean(xf * xf, axis=-1, keepdims=True) + NORM_EPS)
    return (y * w.astype(jnp.float32)).astype(x.dtype)


def l2norm(x):
    return x * lax.rsqrt(jnp.sum(x * x, axis=-1, keepdims=True) + NORM_EPS)


def t5_bucket(dist):
    n = jnp.maximum(dist, 0)
    max_exact = N_BUCKETS // 2
    nf = jnp.maximum(n, 1).astype(jnp.float32)
    large = max_exact + (jnp.log(nf / max_exact) / math.log(MAX_DISTANCE / max_exact)
                         * (N_BUCKETS - max_exact)).astype(jnp.int32)
    large = jnp.minimum(large, N_BUCKETS - 1)
    return jnp.where(n < max_exact, n, large)


def rel_bias_logits(dist, table):
    bias = table.astype(jnp.float32)[t5_bucket(dist)]
    return jnp.moveaxis(bias, -1, 0).reshape(SWA_KV, SWA_G, dist.shape[0], dist.shape[1])


def sink_softmax(logits, sink):
    m = jnp.maximum(jnp.max(logits, axis=-1, keepdims=True), sink)
    p = jnp.exp(logits - m)
    return p / (jnp.sum(p, axis=-1, keepdims=True) + jnp.exp(sink - m))


def causal_conv(x, prefix, w):
    L = x.shape[1]
    xp = jnp.concatenate([prefix.astype(x.dtype), x], axis=1)
    y = sum(xp[:, j:j + L] * w[j] for j in range(CONV_W))
    return y, xp[:, xp.shape[1] - (CONV_W - 1):]


def gated_delta_chunked(q, k, v, g, beta, s0):
    B, L, H, dk = q.shape
    dv = v.shape[-1]
    C = math.gcd(L, GDN_CHUNK)
    N = L // C

    def blocks(t):
        return jnp.moveaxis(t.reshape(B, N, C, H, -1), 3, 1)

    q, k, v = blocks(q), blocks(k), blocks(v)
    g = blocks(g[..., None])[..., 0]
    beta = blocks(beta[..., None])[..., 0]
    gc = jnp.cumsum(g, axis=-1)
    idx = jnp.arange(C)
    causal = idx[:, None] >= idx[None, :]
    decay = jnp.exp(jnp.where(causal, gc[..., :, None] - gc[..., None, :], -jnp.inf))
    k_beta = k * beta[..., None]
    a_strict = jnp.where(idx[:, None] > idx[None, :],
                         jnp.einsum('bhncd,bhnsd->bhncs', k_beta, k) * decay, 0.0)
    rhs = jnp.concatenate([v * beta[..., None], k_beta * jnp.exp(gc)[..., None]], axis=-1)
    sol = lax.linalg.triangular_solve(a_strict + jnp.eye(C, dtype=a_strict.dtype), rhs,
                                      left_side=True, lower=True, unit_diagonal=True)
    u, w = sol[..., :dv], sol[..., dv:]
    att = jnp.einsum('bhncd,bhnsd->bhncs', q, k) * decay
    q_dec = q * jnp.exp(gc)[..., None]
    k_tail = k * jnp.exp(gc[..., -1:] - gc)[..., None]
    g_tot = jnp.exp(gc[..., -1])

    def step(S, inp):
        qd, kt, uu, ww, aa, gt = inp
        v_new = uu - jnp.einsum('bhcd,bhde->bhce', ww, S)
        o = jnp.einsum('bhcd,bhde->bhce', qd, S) + jnp.einsum('bhcs,bhse->bhce', aa, v_new)
        S = S * gt[..., None, None] + jnp.einsum('bhcd,bhce->bhde', kt, v_new)
        return S, o

    xs = tuple(jnp.moveaxis(t, 2, 0) for t in (q_dec, k_tail, u, w, att, g_tot))
    s_fin, o = lax.scan(step, s0, xs)
    o = jnp.moveaxis(jnp.moveaxis(o, 0, 2), 1, 3).reshape(B, L, H, dv)
    return o, s_fin


def gdn_branch(gq, gk, gv, gz, gb, ga, conv_prefix, s0, conv_w, a_log, dt_bias, norm_w):
    B, L, _ = gq.shape
    qkv = jnp.concatenate([gq, gk, gv], axis=-1)
    conv, conv_new = causal_conv(qkv, conv_prefix, conv_w)
    conv = jax.nn.silu(conv.astype(jnp.float32))
    q = conv[..., :GDN_QK].reshape(B, L, GDN_H, GDN_DK)
    k = conv[..., GDN_QK:2 * GDN_QK].reshape(B, L, GDN_H, GDN_DK)
    v = conv[..., 2 * GDN_QK:].reshape(B, L, GDN_H, GDN_DV)
    q = l2norm(q) * (GDN_DK ** -0.5)
    k = l2norm(k)
    beta = jax.nn.sigmoid(gb.astype(jnp.float32))
    g = -jnp.exp(a_log.astype(jnp.float32)) * jax.nn.softplus(
        ga.astype(jnp.float32) + dt_bias.astype(jnp.float32))
    o, s_new = gated_delta_chunked(q, k, v, g, beta, s0.astype(jnp.float32))
    o = o * lax.rsqrt(jnp.mean(o * o, axis=-1, keepdims=True) + NORM_EPS) * norm_w.astype(jnp.float32)
    o = o * jax.nn.silu(gz.astype(jnp.float32).reshape(B, L, GDN_H, GDN_DV))
    return o.reshape(B, L, GDN_W).astype(gq.dtype), s_new.astype(s0.dtype), conv_new


def swa_prompt(q, k, v, sinks, table):
    B, L, _, _ = q.shape
    nb = L // WINDOW
    qb = q.reshape(B, nb, WINDOW, SWA_KV, SWA_G, SWA_DH)

    def with_prev(t):
        tb = t.reshape(B, nb, WINDOW, SWA_KV, SWA_DH)
        prev = jnp.pad(tb, ((0, 0), (1, 0), (0, 0), (0, 0), (0, 0)))[:, :-1]
        return jnp.concatenate([prev, tb], axis=2)

    kk, vv = with_prev(k), with_prev(v)
    qi = jnp.arange(WINDOW)[:, None] + WINDOW
    sj = jnp.arange(2 * WINDOW)[None, :]
    dist = qi - sj
    valid = (dist >= 0) & (dist < WINDOW)
    mask = valid[None] & ((jnp.arange(nb)[:, None, None] > 0) | (sj[None] >= WINDOW))
    logits = jnp.einsum('bnqkgd,bnskd->bnkgqs', qb, kk, preferred_element_type=jnp.float32)
    logits = logits * (SWA_DH ** -0.5) + rel_bias_logits(dist, table)
    logits = jnp.where(mask[None, :, None, None], logits, -jnp.inf)
    probs = sink_softmax(logits, sinks.astype(jnp.float32).reshape(SWA_KV, SWA_G, 1, 1))
    o = jnp.einsum('bnkgqs,bnskd->bnqkgd', probs.astype(v.dtype), vv)
    return o.reshape(B, L, SWA_QW), k[:, L - WINDOW:], v[:, L - WINDOW:]


def swa_sample(q, k, v, sinks, table, cache_k, cache_v):
    Bd, S, _, _ = q.shape
    Wb = cache_k.shape[1]
    kk = jnp.concatenate([cache_k.astype(k.dtype), k], axis=1)
    vv = jnp.concatenate([cache_v.astype(v.dtype), v], axis=1)
    dist = (Wb + jnp.arange(S))[:, None] - jnp.arange(Wb + S)[None, :]
    valid = (dist >= 0) & (dist < WINDOW)
    qg = q.reshape(Bd, S, SWA_KV, SWA_G, SWA_DH)
    logits = jnp.einsum('bqkgd,bskd->bkgqs', qg, kk, preferred_element_type=jnp.float32)
    logits = logits * (SWA_DH ** -0.5) + rel_bias_logits(dist, table)
    logits = jnp.where(valid, logits, -jnp.inf)
    probs = sink_softmax(logits, sinks.astype(jnp.float32).reshape(SWA_KV, SWA_G, 1, 1))
    o = jnp.einsum('bkgqs,bskd->bqkgd', probs.astype(v.dtype), vv)
    return o.reshape(Bd, S, SWA_QW), kk[:, S:], vv[:, S:]


def mem_kv(mem, norm_w, w_kv):
    B, M, _ = mem.shape
    kv = jnp.einsum('bmd,de->bme', rmsnorm(mem, norm_w), w_kv)
    return (kv[..., :MEM_W].reshape(B, M, MEM_H, MEM_DH),
            kv[..., MEM_W:].reshape(B, M, MEM_H, MEM_DH))


def mem_attend(q, mk, mv):
    B, L, _ = q.shape
    qh = q.reshape(B, L, MEM_H, MEM_DH)
    logits = jnp.einsum('blhd,bmhd->bhlm', qh, mk.astype(q.dtype),
                        preferred_element_type=jnp.float32) * (MEM_DH ** -0.5)
    probs = jax.nn.softmax(logits, axis=-1)
    o = jnp.einsum('bhlm,bmhd->blhd', probs.astype(q.dtype), mv.astype(q.dtype))
    return o.reshape(B, L, MEM_W)


def trunk_layer(x, conv_prefix, s0, swa_fn, mem_k, mem_v, norm_in, w_in, conv_w, a_log,
                dt_bias, gdn_norm, sinks, rel_table, w_branch, w_out):
    B, L, _ = x.shape
    xn = rmsnorm(x, norm_in)
    proj = jnp.einsum('bld,de->ble', xn, w_in)
    cuts = np.cumsum(IN_SIZES)[:-1].tolist()
    (gq, gk, gv, gz, gb, ga, sq, sk, sv, sz, mq, mz, mg) = jnp.split(proj, cuts, axis=-1)
    o_gdn, s_new, conv_new = gdn_branch(gq, gk, gv, gz, gb, ga, conv_prefix, s0,
                                        conv_w, a_log, dt_bias, gdn_norm)
    o_swa, k_win, v_win = swa_fn(sq.reshape(B, L, SWA_H, SWA_DH),
                                 sk.reshape(B, L, SWA_KV, SWA_DH),
                                 sv.reshape(B, L, SWA_KV, SWA_DH), sinks, rel_table)
    o_swa = o_swa * jax.nn.silu(sz)
    o_mem = mem_attend(mq, mem_k, mem_v) * jax.nn.silu(mz)
    gates = jax.nn.sigmoid(mg.astype(jnp.float32)).astype(x.dtype).reshape(B, L, N_BRANCH, D_MODEL)
    merged = sum(gates[:, :, n] * jnp.einsum('blc,cd->bld', o_n, w_branch[n])
                 for n, o_n in enumerate((o_gdn, o_swa, o_mem)))
    y = x + jnp.einsum('bld,de->ble', merged, w_out)
    return y, s_new, conv_new, k_win, v_win


def setup_inputs(seed: int = 0) -> dict:
    key = jax.random.key(seed)
    ks = jax.random.split(key, 24)
    f32 = jnp.float32

    def nrm(k, shape, scale):
        return jax.random.normal(k, shape, f32) * scale

    swa_buf = min(WINDOW, PAST_LEN)
    a_init = jax.random.uniform(ks[12], (DEPTH, GDN_H), f32, 1.0, 16.0)
    dt = jnp.exp(jax.random.uniform(ks[13], (DEPTH, GDN_H), f32, math.log(1e-3), math.log(1e-1)))
    return {
        'x_prompt': nrm(ks[0], (BATCH, SEQ, D_MODEL), 1.0),
        'x_sample': nrm(ks[1], (DEC_BATCH, DEC_SEQ, D_MODEL), 1.0),
        'state_gdn': nrm(ks[2], (DEPTH, DEC_BATCH, GDN_H, GDN_DK, GDN_DV), 0.3),
        'state_gdn_conv': nrm(ks[3], (DEPTH, DEC_BATCH, CONV_W - 1, GDN_CONV_CH), 1.0),
        'cache_swa_k': nrm(ks[4], (DEPTH, DEC_BATCH, swa_buf, SWA_KV, SWA_DH), 1.0),
        'cache_swa_v': nrm(ks[5], (DEPTH, DEC_BATCH, swa_buf, SWA_KV, SWA_DH), 1.0),
        'cache_mem_k': nrm(ks[6], (DEPTH, DEC_BATCH, N_MEM, MEM_H, MEM_DH), 1.0),
        'cache_mem_v': nrm(ks[7], (DEPTH, DEC_BATCH, N_MEM, MEM_H, MEM_DH), 1.0),
        'mem_prompt': nrm(ks[8], (BATCH, N_MEM, D_MODEL), 1.0),
        'norm_in': 1.0 + nrm(ks[9], (DEPTH, D_MODEL), 0.02),
        'w_in': nrm(ks[10], (DEPTH, D_MODEL, IN_COLS), D_MODEL ** -0.5),
        'gdn_conv_w': nrm(ks[11], (DEPTH, CONV_W, GDN_CONV_CH), CONV_W ** -0.5),
        'gdn_a_log': jnp.log(a_init),
        'gdn_dt_bias': dt + jnp.log(-jnp.expm1(-dt)),
        'gdn_norm': 1.0 + nrm(ks[14], (DEPTH, GDN_DV), 0.02),
        'swa_sinks': nrm(ks[15], (DEPTH, SWA_H), 0.5),
        'rel_bias': nrm(ks[16], (N_BUCKETS, SWA_H), 0.5),
        'norm_mem': 1.0 + nrm(ks[17], (DEPTH, D_MODEL), 0.02),
        'w_mem_kv': nrm(ks[18], (DEPTH, D_MODEL, 2 * MEM_W), D_MODEL ** -0.5),
        'w_branch': nrm(ks[19], (DEPTH, N_BRANCH, BR_W, D_MODEL), BR_W ** -0.5),
        'w_out': nrm(ks[20], (DEPTH, D_MODEL, D_MODEL), D_MODEL ** -0.5),
        'norm_f': 1.0 + nrm(ks[21], (D_MODEL,), 0.02),
    }


def reference(x_prompt, x_sample, state_gdn, state_gdn_conv, cache_swa_k, cache_swa_v,
              cache_mem_k, cache_mem_v, mem_prompt, norm_in, w_in, gdn_conv_w, gdn_a_log,
              gdn_dt_bias, gdn_norm, swa_sinks, rel_bias, norm_mem, w_mem_kv, w_branch,
              w_out, norm_f):
    hp, hs = x_prompt, x_sample
    B = x_prompt.shape[0]
    gdn_p, conv_p, swk_p, swv_p, mk_p, mv_p = [], [], [], [], [], []
    gdn_s, conv_s, swk_s, swv_s = [], [], [], []
    for l in range(DEPTH):
        weights = (norm_in[l], w_in[l], gdn_conv_w[l], gdn_a_log[l], gdn_dt_bias[l], gdn_norm[l],
                   swa_sinks[l], rel_bias, w_branch[l], w_out[l])
        mk, mv = mem_kv(mem_prompt, norm_mem[l], w_mem_kv[l])
        prefix0 = jnp.zeros((B, CONV_W - 1, GDN_CONV_CH), hp.dtype)
        s00 = jnp.zeros((B, GDN_H, GDN_DK, GDN_DV), hp.dtype)
        hp, s_new, c_new, k_new, v_new = trunk_layer(hp, prefix0, s00, swa_prompt, mk, mv, *weights)
        gdn_p.append(s_new); conv_p.append(c_new); swk_p.append(k_new); swv_p.append(v_new)
        mk_p.append(mk); mv_p.append(mv)
        swa_fn = functools.partial(swa_sample, cache_k=cache_swa_k[l], cache_v=cache_swa_v[l])
        hs, s_new, c_new, k_new, v_new = trunk_layer(hs, state_gdn_conv[l], state_gdn[l], swa_fn,
                                                     cache_mem_k[l], cache_mem_v[l], *weights)
        gdn_s.append(s_new); conv_s.append(c_new); swk_s.append(k_new); swv_s.append(v_new)
    y_prompt = rmsnorm(hp, norm_f)
    y_sample = rmsnorm(hs, norm_f)
    return (y_prompt, y_sample,
            jnp.stack(gdn_p), jnp.stack(conv_p), jnp.stack(swk_p), jnp.stack(swv_p),
            jnp.stack(mk_p), jnp.stack(mv_p),
            jnp.stack(gdn_s), jnp.stack(conv_s), jnp.stack(swk_s), jnp.stack(swv_s))
```

```python
import functools
import math

import numpy as np
import jax
import jax.numpy as jnp
from jax import lax
from jax.experimental import pallas as pl
from jax.experimental.pallas import tpu as pltpu

F32 = jnp.float32
BF16 = jnp.bfloat16

D_MODEL = 2048
N_BRANCH = 3
BR_W = 1024
GDN_H = 8
GDN_DK = 128
GDN_DV = 128
GDN_QK = GDN_H * GDN_DK
GDN_W = GDN_H * GDN_DV
GDN_CONV_CH = 2 * GDN_QK + GDN_W
CONV_W = 4
GDN_CHUNK = 64
SWA_H = 16
SWA_KV = 2
SWA_G = SWA_H // SWA_KV
SWA_DH = 64
SWA_QW = SWA_H * SWA_DH
SWA_KVW = SWA_KV * SWA_DH
WINDOW = 128
N_BUCKETS = 32
MAX_DISTANCE = 128
N_MEM = 256
MEM_H = 4
MEM_DH = 256
MEM_W = MEM_H * MEM_DH
NORM_EPS = 1e-6

LANE = 128
SUBLANE = 8
VMEM_LIMIT = 52 * 1024 * 1024


def _cparams(sem):
    return pltpu.CompilerParams(dimension_semantics=sem, vmem_limit_bytes=VMEM_LIMIT)


def _sigmoid(x):
    return 1.0 / (1.0 + jnp.exp(-x))


def _silu(x):
    return x * _sigmoid(x)


def _softplus(x):
    return jnp.maximum(x, 0.0) + jnp.log(1.0 + jnp.exp(-jnp.abs(x)))


def _dot(a, b):
    return jnp.dot(a.astype(BF16), b.astype(BF16), preferred_element_type=F32)


def _dot_nt(a, b):
    return lax.dot_general(a.astype(BF16), b.astype(BF16), (((1,), (1,)), ((), ())),
                           preferred_element_type=F32)


def _dot_tn(a, b):
    return lax.dot_general(a.astype(BF16), b.astype(BF16), (((0,), (0,)), ((), ())),
                           preferred_element_type=F32)


def _dot_f32(a, b):
    return jnp.dot(a, b, preferred_element_type=F32, precision=lax.Precision.HIGHEST)


def _norm_proj_kernel(x_ref, nw_ref, w_ref, o_ref, xn_ref, *, act):
    @pl.when(pl.program_id(1) == 0)
    def _():
        x = x_ref[...].astype(F32)
        ms = jnp.mean(x * x, axis=-1, keepdims=True)
        xn_ref[...] = (x * lax.rsqrt(ms + NORM_EPS) * nw_ref[...]).astype(BF16)

    y = jnp.dot(xn_ref[...], w_ref[...], preferred_element_type=F32)
    if act == "sigmoid":
        y = _sigmoid(y)
    o_ref[...] = y.astype(o_ref.dtype)


def _pick_tile(n, pref):
    t = min(n, pref)
    while n % t:
        t -= LANE
    return t


def _norm_proj(x, norm_w, w, out_dtype, act=None, tm_pref=1024, tn_pref=1024):
    m, d = x.shape
    n = w.shape[1]
    tm = min(m, tm_pref)
    assert m % tm == 0
    tn = _pick_tile(n, tn_pref)
    return pl.pallas_call(
        functools.partial(_norm_proj_kernel, act=act),
        out_shape=jax.ShapeDtypeStruct((m, n), out_dtype),
        grid=(m // tm, n // tn),
        in_specs=[pl.BlockSpec((tm, d), lambda i, j: (i, 0)),
                  pl.BlockSpec((1, d), lambda i, j: (0, 0)),
                  pl.BlockSpec((d, tn), lambda i, j: (0, j))],
        out_specs=pl.BlockSpec((tm, tn), lambda i, j: (i, j)),
        scratch_shapes=[pltpu.VMEM((tm, d), BF16)],
        compiler_params=_cparams(("parallel", "arbitrary")),
        name="norm_proj",
    )(x, norm_w, w)


HDR = SUBLANE


def _tri_inv(a, c):
    row = lax.broadcasted_iota(jnp.int32, (c, c), 0)
    col = lax.broadcasted_iota(jnp.int32, (c, c), 1)
    x = jnp.where(row == col, 1.0, 0.0).astype(F32) - a
    b = _dot_f32(a, a)
    n = 2
    while n < c:
        x = x + _dot_f32(x, b)
        n *= 2
        if n < c:
            b = _dot_f32(b, b)
    return x


def _gdn_kernel(qkv_ref, z_ref, ba_ref, hdr_ref, s0_ref, cw_ref, arow_ref, dtrow_ref, nw_ref,
                o_ref, s_ref, buf_ref, cv_ref, gb_ref, oacc_ref,
                *, chunk, n_chunks, per_chunk_state, valid_lo, valid_hi):
    c = chunk
    tb = c * n_chunks

    if per_chunk_state:
        buf_ref[0:HDR, :] = jnp.zeros((HDR, GDN_CONV_CH), F32)
    else:
        @pl.when(pl.program_id(1) == 0)
        def _():
            buf_ref[0:HDR, :] = hdr_ref[0]
            s_ref[...] = s0_ref[...]

    buf_ref[HDR:HDR + tb, :] = qkv_ref[...].astype(F32)
    acc = None
    for j in range(CONV_W):
        off = HDR - (CONV_W - 1) + j
        term = buf_ref[off:off + tb, :] * cw_ref[j:j + 1, :]
        acc = term if acc is None else acc + term
    cv_ref[...] = _silu(acc)
    if not per_chunk_state:
        buf_ref[0:HDR, :] = buf_ref[tb:tb + HDR, :]

    ba = ba_ref[...].astype(F32)
    beta_all = _sigmoid(ba)
    g_all = -jnp.exp(arow_ref[...]) * _softplus(ba + dtrow_ref[...])
    if per_chunk_state:
        r = lax.broadcasted_iota(jnp.int32, (tb, LANE), 0) & (c - 1)
        valid = (r >= valid_lo) & (r < valid_hi)
        beta_all = jnp.where(valid, beta_all, 0.0)
        g_all = jnp.where(valid, g_all, 0.0)
    gb_ref[0] = beta_all
    gb_ref[1] = g_all

    row = lax.broadcasted_iota(jnp.int32, (c, c), 0)
    col = lax.broadcasted_iota(jnp.int32, (c, c), 1)
    causal = row >= col
    strict = row > col
    tril = jnp.where(causal, 1.0, 0.0).astype(F32)
    scale_q = GDN_DK ** -0.5

    def chunk_body(ci, carry):
        r0 = pl.multiple_of(ci * c, c)
        rows = pl.ds(r0, c)
        beta_c = gb_ref[0, rows, :]
        g_c = gb_ref[1, rows, :]
        gc_all = _dot_f32(tril, g_c)
        gc_t = gc_all.T
        si = ci if per_chunk_state else 0
        if per_chunk_state:
            rv = lax.broadcasted_iota(jnp.int32, (c, 1), 0)
            rvalid = (rv >= valid_lo) & (rv < valid_hi)
        for h in range(GDN_H):
            q = cv_ref[rows, h * GDN_DK:(h + 1) * GDN_DK]
            k = cv_ref[rows, GDN_QK + h * GDN_DK:GDN_QK + (h + 1) * GDN_DK]
            v = cv_ref[rows, 2 * GDN_QK + h * GDN_DV:2 * GDN_QK + (h + 1) * GDN_DV]
            q = q * lax.rsqrt(jnp.sum(q * q, axis=-1, keepdims=True) + NORM_EPS) * scale_q
            k = k * lax.rsqrt(jnp.sum(k * k, axis=-1, keepdims=True) + NORM_EPS)
            if per_chunk_state:
                q = jnp.where(rvalid, q, 0.0)
                k = jnp.where(rvalid, k, 0.0)
                v = jnp.where(rvalid, v, 0.0)
            beta = beta_c[:, h:h + 1]
            gc_col = gc_all[:, GDN_H + h:GDN_H + h + 1]
            gc_row = gc_t[GDN_H + h:GDN_H + h + 1, :]
            gc_last = gc_col[c - 1:c, :]
            diff = jnp.where(causal, gc_col - gc_row, 0.0)
            decay = jnp.where(causal, jnp.exp(diff), 0.0)
            e_gc = jnp.exp(gc_col)
            kb = k * beta
            a = jnp.where(strict, _dot_nt(kb, k) * decay, 0.0)
            t_inv = _tri_inv(a, c)
            rhs = jnp.concatenate([v * beta, kb * e_gc], axis=1)
            sol = _dot(t_inv, rhs)
            u = sol[:, :GDN_DV]
            w = sol[:, GDN_DV:]
            att = _dot_nt(q, k) * decay
            s = s_ref[si, h]
            wq_s = _dot(jnp.concatenate([w, q * e_gc], axis=0), s)
            v_new = u - wq_s[:c]
            o = wq_s[c:] + _dot(att, v_new)
            k_tail = k * jnp.exp(gc_last - gc_col)
            s_ref[si, h] = s * jnp.exp(gc_last) + _dot_tn(k_tail, v_new)
            o = o * lax.rsqrt(jnp.mean(o * o, axis=-1, keepdims=True) + NORM_EPS) * nw_ref[...]
            oacc_ref[rows, h * GDN_DV:(h + 1) * GDN_DV] = o
        return carry

    if per_chunk_state:
        s_ref[...] = s0_ref[...]
    lax.fori_loop(0, n_chunks, chunk_body, 0)
    o_ref[...] = (oacc_ref[...] * _silu(z_ref[...].astype(F32))).astype(o_ref.dtype)


def _gdn(qkv, z, ba, hdr, s0, conv_w, a_row, dt_row, norm_w, *, n_seq, chunk, n_chunks,
         per_chunk_state, valid_lo, valid_hi, out_dtype, z_colblock=0):
    rows = qkv.shape[0]
    tb = chunk * n_chunks
    if per_chunk_state:
        grid = (rows // tb,)
        sem = ("arbitrary",)
        rmap = lambda i: (i, 0)
        zmap = lambda i: (i, z_colblock)
        hmap = lambda i: (0, 0, 0)
        smap = lambda i: (i, 0, 0, 0)
        cmap = lambda i: (0, 0)
        ns = n_chunks
    else:
        steps = rows // n_seq // tb
        grid = (n_seq, steps)
        sem = ("parallel", "arbitrary")
        rmap = lambda b, n: (b * steps + n, 0)
        zmap = lambda b, n: (b * steps + n, z_colblock)
        hmap = lambda b, n: (b, 0, 0)
        smap = lambda b, n: (b, 0, 0, 0)
        cmap = lambda b, n: (0, 0)
        ns = 1
    kern = functools.partial(_gdn_kernel, chunk=chunk, n_chunks=n_chunks,
                             per_chunk_state=per_chunk_state, valid_lo=valid_lo, valid_hi=valid_hi)
    return pl.pallas_call(
        kern,
        out_shape=(jax.ShapeDtypeStruct((rows, GDN_W), out_dtype),
                   jax.ShapeDtypeStruct(s0.shape, F32)),
        grid=grid,
        in_specs=[pl.BlockSpec((tb, GDN_CONV_CH), rmap),
                  pl.BlockSpec((tb, GDN_W), zmap),
                  pl.BlockSpec((tb, LANE), rmap),
                  pl.BlockSpec((1, HDR, GDN_CONV_CH), hmap),
                  pl.BlockSpec((ns, GDN_H, GDN_DK, GDN_DV), smap),
                  pl.BlockSpec((CONV_W, GDN_CONV_CH), cmap),
                  pl.BlockSpec((1, LANE), cmap),
                  pl.BlockSpec((1, LANE), cmap),
                  pl.BlockSpec((1, GDN_DV), cmap)],
        out_specs=(pl.BlockSpec((tb, GDN_W), rmap),
                   pl.BlockSpec((ns, GDN_H, GDN_DK, GDN_DV), smap)),
        scratch_shapes=[pltpu.VMEM((HDR + tb, GDN_CONV_CH), F32),
                        pltpu.VMEM((tb, GDN_CONV_CH), F32),
                        pltpu.VMEM((2, tb, LANE), F32),
                        pltpu.VMEM((tb, GDN_W), F32)],
        compiler_params=_cparams(sem),
        name="gdn",
    )(qkv, z, ba, hdr, s0, conv_w, a_row, dt_row, norm_w)


def _t5_bucket_np(dist):
    n = np.maximum(dist, 0)
    max_exact = N_BUCKETS // 2
    nf = np.maximum(n, 1).astype(np.float32)
    large = max_exact + (np.log(nf / np.float32(max_exact)) / np.float32(math.log(MAX_DISTANCE / max_exact))
                         * np.float32(N_BUCKETS - max_exact)).astype(np.int32)
    large = np.minimum(large, N_BUCKETS - 1)
    return np.where(n < max_exact, n, large).astype(np.int32)


def _bias_prompt_kernel(code_ref, tab_ref, o_ref):
    h = pl.program_id(1)
    code = code_ref[0]
    acc = jnp.full(code.shape, -jnp.inf, F32)
    for b in range(N_BUCKETS):
        acc = jnp.where(code == b, tab_ref[b, h], acc)
    o_ref[0, 0] = acc


def _bias_prompt(table):
    qi = np.arange(WINDOW)[:, None] + WINDOW
    sj = np.arange(2 * WINDOW)[None, :]
    dist = qi - sj
    valid = (dist >= 0) & (dist < WINDOW)
    bucket = _t5_bucket_np(dist)
    code_rest = np.where(valid, bucket, -1)
    code_first = np.where(valid & (sj >= WINDOW), bucket, -1)
    code = jnp.asarray(np.stack([code_first, code_rest]).astype(np.int32))
    return pl.pallas_call(
        _bias_prompt_kernel,
        out_shape=jax.ShapeDtypeStruct((2, SWA_H, WINDOW, 2 * WINDOW), F32),
        grid=(2, SWA_H),
        in_specs=[pl.BlockSpec((1, WINDOW, 2 * WINDOW), lambda v, h: (v, 0, 0)),
                  pl.BlockSpec(memory_space=pltpu.SMEM)],
        out_specs=pl.BlockSpec((1, 1, WINDOW, 2 * WINDOW), lambda v, h: (v, h, 0, 0)),
        compiler_params=_cparams(("arbitrary", "arbitrary")),
        name="swa_bias_prompt",
    )(code, table)


def _bias_sample_kernel(code_ref, tab_ref, o_ref):
    kv = pl.program_id(0)
    code = code_ref[...]
    acc = jnp.full(code.shape, -jnp.inf, F32)
    for g in range(SWA_G):
        for b in range(N_BUCKETS):
            acc = jnp.where(code == b + N_BUCKETS * g, tab_ref[b, kv * SWA_G + g], acc)
    o_ref[0] = acc


def _bias_sample(table, n_tok, n_cache, n_keys_pad):
    dist = (n_cache + np.arange(n_tok))[:, None] - np.arange(n_keys_pad)[None, :]
    valid = (dist >= 0) & (dist < WINDOW) & (np.arange(n_keys_pad)[None, :] < n_cache + n_tok)
    bucket = _t5_bucket_np(dist)
    code_t = np.where(valid, bucket, -1)
    g = np.arange(SWA_G)[:, None, None]
    code = np.where(code_t[None] >= 0, code_t[None] + N_BUCKETS * g, -1)
    code = jnp.asarray(code.reshape(SWA_G * n_tok, n_keys_pad).astype(np.int32))
    return pl.pallas_call(
        _bias_sample_kernel,
        out_shape=jax.ShapeDtypeStruct((SWA_KV, SWA_G * n_tok, n_keys_pad), F32),
        grid=(SWA_KV,),
        in_specs=[pl.BlockSpec((SWA_G * n_tok, n_keys_pad), lambda k: (0, 0)),
                  pl.BlockSpec(memory_space=pltpu.SMEM)],
        out_specs=pl.BlockSpec((1, SWA_G * n_tok, n_keys_pad), lambda k: (k, 0, 0)),
        compiler_params=_cparams(("arbitrary",)),
        name="swa_bias_sample",
    )(code, table)


def _sink_softmax_pv(logits, sink, v):
    m = jnp.maximum(jnp.max(logits, axis=-1, keepdims=True), sink)
    p = jnp.exp(logits - m)
    den = jnp.sum(p, axis=-1, keepdims=True) + jnp.exp(sink - m)
    return _dot(p, v) / den


def _swa_prompt_kernel(q_ref, z_ref, kc_ref, kp_ref, vc_ref, vp_ref, bias_ref, sink_ref, o_ref):
    kk = jnp.concatenate([kp_ref[...], kc_ref[...]], axis=0)
    vv = jnp.concatenate([vp_ref[...], vc_ref[...]], axis=0)
    scale = SWA_DH ** -0.5
    for h in range(SWA_H):
        kv = h // SWA_G
        cs = slice(h * SWA_DH, (h + 1) * SWA_DH)
        ks = slice(kv * SWA_DH, (kv + 1) * SWA_DH)
        logits = _dot_nt(q_ref[:, cs], kk[:, ks]) * scale + bias_ref[0, h]
        oh = _sink_softmax_pv(logits, sink_ref[h], vv[:, ks])
        o_ref[:, cs] = (oh * _silu(z_ref[:, cs].astype(F32))).astype(o_ref.dtype)


def _swa_prompt(proj, bias, sinks, n_seq, seq_len):
    nb = seq_len // WINDOW
    kcol = 2 * SWA_QW // SWA_KVW
    vcol = kcol + 1
    cur = lambda b, n: b * nb + n
    prev = lambda b, n: b * nb + jnp.maximum(n - 1, 0)
    return pl.pallas_call(
        _swa_prompt_kernel,
        out_shape=jax.ShapeDtypeStruct((n_seq * seq_len, SWA_QW), BF16),
        grid=(n_seq, nb),
        in_specs=[pl.BlockSpec((WINDOW, SWA_QW), lambda b, n: (cur(b, n), 0)),
                  pl.BlockSpec((WINDOW, SWA_QW), lambda b, n: (cur(b, n), 1)),
                  pl.BlockSpec((WINDOW, SWA_KVW), lambda b, n: (cur(b, n), kcol)),
                  pl.BlockSpec((WINDOW, SWA_KVW), lambda b, n: (prev(b, n), kcol)),
                  pl.BlockSpec((WINDOW, SWA_KVW), lambda b, n: (cur(b, n), vcol)),
                  pl.BlockSpec((WINDOW, SWA_KVW), lambda b, n: (prev(b, n), vcol)),
                  pl.BlockSpec((1, SWA_H, WINDOW, 2 * WINDOW), lambda b, n: (jnp.minimum(n, 1), 0, 0, 0)),
                  pl.BlockSpec(memory_space=pltpu.SMEM)],
        out_specs=pl.BlockSpec((WINDOW, SWA_QW), lambda b, n: (cur(b, n), 0)),
        compiler_params=_cparams(("parallel", "arbitrary")),
        name="swa_prompt",
    )(proj, proj, proj, proj, proj, proj, bias, sinks)


def _swa_sample_kernel(q_ref, z_ref, kn_ref, vn_ref, ck_ref, cv_ref, bias_ref, sink_ref, o_ref, *, n_seq_blk):
    scale = SWA_DH ** -0.5
    for s in range(n_seq_blk):
        kk = jnp.concatenate([ck_ref[s], kn_ref[s]], axis=0)
        vv = jnp.concatenate([cv_ref[s], vn_ref[s]], axis=0)
        for kv in range(SWA_KV):
            ks = slice(kv * SWA_DH, (kv + 1) * SWA_DH)
            logits = _dot_nt(q_ref[s, kv], kk[:, ks]) * scale + bias_ref[kv]
            oh = _sink_softmax_pv(logits, sink_ref[kv][:, 0:1], vv[:, ks])
            o_ref[s, kv] = (oh * _silu(z_ref[s, kv].astype(F32))).astype(o_ref.dtype)


def _swa_sample(q, z, k_new, v_new, cache_k, cache_v, bias, sink_rows, n_seq_blk=8):
    bd, _, rows, _ = q.shape
    wb = cache_k.shape[1]
    npad = k_new.shape[1]
    blk4 = lambda i: (i, 0, 0, 0)
    blk3 = lambda i: (i, 0, 0)
    return pl.pallas_call(
        functools.partial(_swa_sample_kernel, n_seq_blk=n_seq_blk),
        out_shape=jax.ShapeDtypeStruct(q.shape, BF16),
        grid=(bd // n_seq_blk,),
        in_specs=[pl.BlockSpec((n_seq_blk, SWA_KV, rows, SWA_DH), blk4),
                  pl.BlockSpec((n_seq_blk, SWA_KV, rows, SWA_DH), blk4),
                  pl.BlockSpec((n_seq_blk, npad, SWA_KVW), blk3),
                  pl.BlockSpec((n_seq_blk, npad, SWA_KVW), blk3),
                  pl.BlockSpec((n_seq_blk, wb, SWA_KVW), blk3),
                  pl.BlockSpec((n_seq_blk, wb, SWA_KVW), blk3),
                  pl.BlockSpec((SWA_KV, rows, wb + npad), lambda i: (0, 0, 0)),
                  pl.BlockSpec((SWA_KV, rows, LANE), lambda i: (0, 0, 0))],
        out_specs=pl.BlockSpec((n_seq_blk, SWA_KV, rows, SWA_DH), blk4),
        compiler_params=_cparams(("arbitrary",)),
        name="swa_sample",
    )(q, z, k_new, v_new, cache_k, cache_v, bias, sink_rows)


def _mem_heads(q_of, z_of, k, v, store):
    scale = MEM_DH ** -0.5
    for h in range(MEM_H):
        cs = slice(h * MEM_DH, (h + 1) * MEM_DH)
        logits = _dot_nt(q_of(cs), k[:, cs]) * scale
        m = jnp.max(logits, axis=-1, keepdims=True)
        p = jnp.exp(logits - m)
        den = jnp.sum(p, axis=-1, keepdims=True)
        oh = _dot(p, v[:, cs]) / den
        store(cs, oh * _silu(z_of(cs).astype(F32)))


def _mem_prompt_kernel(q_ref, z_ref, k_ref, v_ref, o_ref):
    def store(cs, val):
        o_ref[:, cs] = val.astype(o_ref.dtype)
    _mem_heads(lambda cs: q_ref[:, cs], lambda cs: z_ref[:, cs], k_ref[0], v_ref[0], store)


def _mem_prompt(proj, mk, mv, n_seq, seq_len, tq=512):
    steps = seq_len // tq
    return pl.pallas_call(
        _mem_prompt_kernel,
        out_shape=jax.ShapeDtypeStruct((n_seq * seq_len, MEM_W), BF16),
        grid=(n_seq, steps),
        in_specs=[pl.BlockSpec((tq, MEM_W), lambda b, n: (b * steps + n, 0)),
                  pl.BlockSpec((tq, MEM_W), lambda b, n: (b * steps + n, 1)),
                  pl.BlockSpec((1, N_MEM, MEM_W), lambda b, n: (b, 0, 0)),
                  pl.BlockSpec((1, N_MEM, MEM_W), lambda b, n: (b, 0, 0))],
        out_specs=pl.BlockSpec((tq, MEM_W), lambda b, n: (b * steps + n, 0)),
        compiler_params=_cparams(("parallel", "arbitrary")),
        name="mem_prompt",
    )(proj, proj, mk, mv)


def _mem_sample_kernel(q_ref, z_ref, k_ref, v_ref, o_ref, *, n_seq_blk):
    for s in range(n_seq_blk):
        def store(cs, val, s=s):
            o_ref[s, :, cs] = val.astype(o_ref.dtype)
        _mem_heads(lambda cs, s=s: q_ref[s, :, cs], lambda cs, s=s: z_ref[s, :, cs],
                   k_ref[s], v_ref[s], store)


def _mem_sample(q, z, cache_k, cache_v, n_seq_blk=4):
    bd, rows, _ = q.shape
    blk = lambda i: (i, 0, 0)
    return pl.pallas_call(
        functools.partial(_mem_sample_kernel, n_seq_blk=n_seq_blk),
        out_shape=jax.ShapeDtypeStruct(q.shape, BF16),
        grid=(bd // n_seq_blk,),
        in_specs=[pl.BlockSpec((n_seq_blk, rows, MEM_W), blk),
                  pl.BlockSpec((n_seq_blk, rows, MEM_W), blk),
                  pl.BlockSpec((n_seq_blk, N_MEM, MEM_W), blk),
                  pl.BlockSpec((n_seq_blk, N_MEM, MEM_W), blk)],
        out_specs=pl.BlockSpec((n_seq_blk, rows, MEM_W), blk),
        compiler_params=_cparams(("arbitrary",)),
        name="mem_sample",
    )(q, z, cache_k, cache_v)


def _merge_kernel(og_ref, os_ref, om_ref, gate_ref, x_ref, wb_ref, wo_ref, nf_ref, y_ref):
    merged = None
    for b, o_ref in enumerate((og_ref, os_ref, om_ref)):
        t = jnp.dot(o_ref[...], wb_ref[b], preferred_element_type=F32)
        t = t * gate_ref[:, b * D_MODEL:(b + 1) * D_MODEL].astype(F32)
        merged = t if merged is None else merged + t
    h = x_ref[...] + jnp.dot(merged.astype(BF16), wo_ref[...], preferred_element_type=F32)
    ms = jnp.mean(h * h, axis=-1, keepdims=True)
    y_ref[...] = h * lax.rsqrt(ms + NORM_EPS) * nf_ref[...]


def _merge(o_gdn, o_swa, o_mem, gates, x, w_branch, w_out, norm_f, tm=256):
    m = x.shape[0]
    tm = min(tm, m)
    row = lambda i: (i, 0)
    const2 = lambda i: (0, 0)
    return pl.pallas_call(
        _merge_kernel,
        out_shape=jax.ShapeDtypeStruct((m, D_MODEL), F32),
        grid=(m // tm,),
        in_specs=[pl.BlockSpec((tm, BR_W), row),
                  pl.BlockSpec((tm, BR_W), row),
                  pl.BlockSpec((tm, BR_W), row),
                  pl.BlockSpec((tm, N_BRANCH * D_MODEL), row),
                  pl.BlockSpec((tm, D_MODEL), row),
                  pl.BlockSpec((N_BRANCH, BR_W, D_MODEL), lambda i: (0, 0, 0),
                               pipeline_mode=pl.Buffered(1)),
                  pl.BlockSpec((D_MODEL, D_MODEL), const2, pipeline_mode=pl.Buffered(1)),
                  pl.BlockSpec((1, D_MODEL), const2)],
        out_specs=pl.BlockSpec((tm, D_MODEL), row),
        compiler_params=_cparams(("parallel",)),
        name="merge",
    )(o_gdn, o_swa, o_mem, gates, x, w_branch, w_out, norm_f)


_IN_SIZES = (GDN_QK, GDN_QK, GDN_W, GDN_W, GDN_H, GDN_H, SWA_QW, SWA_KVW, SWA_KVW, SWA_QW,
             MEM_W, MEM_W, N_BRANCH * D_MODEL)
_IN_NAMES = ("gq", "gk", "gv", "gz", "gb", "ga", "sq", "sk", "sv", "sz", "mq", "mz", "mg")
_IN_SPAN = {name: (int(off), int(off + size)) for name, off, size in
            zip(_IN_NAMES, np.cumsum((0,) + _IN_SIZES[:-1]), _IN_SIZES)}


def _cols(w, *names):
    return jnp.concatenate([w[:, _IN_SPAN[a][0]:_IN_SPAN[a][1]] for a in names], axis=1)


def kernel(x_prompt, x_sample, state_gdn, state_gdn_conv, cache_swa_k, cache_swa_v, cache_mem_k,
           cache_mem_v, mem_prompt, norm_in, w_in, gdn_conv_w, gdn_a_log, gdn_dt_bias, gdn_norm,
           swa_sinks, rel_bias, norm_mem, w_mem_kv, w_branch, w_out, norm_f):
    n_layers = norm_in.shape[0]
    assert n_layers == 1
    b, seq, _ = x_prompt.shape
    bd, ns, _ = x_sample.shape
    wb = cache_swa_k.shape[2]
    assert seq % WINDOW == 0 and seq % GDN_CHUNK == 0 and ns + CONV_W <= SUBLANE and wb == WINDOW
    lyr = 0

    w = w_in[lyr]
    w_gdn = _cols(w, "gq", "gk", "gv", "gz").astype(BF16)
    w_ba = jnp.pad(_cols(w, "gb", "ga"), ((0, 0), (0, LANE - 2 * GDN_H))).astype(BF16)
    w_swa = _cols(w, "sq", "sz", "sk", "sv").astype(BF16)
    w_mem = _cols(w, "mq", "mz").astype(BF16)
    w_gate = _cols(w, "mg").astype(BF16)
    w_tail = _cols(w, "gq", "gk", "gv", "sk", "sv").astype(BF16)
    w_mkv = w_mem_kv[lyr].astype(BF16)
    w_br = w_branch[lyr].astype(BF16)
    w_o = w_out[lyr].astype(BF16)
    nw_in = norm_in[lyr].reshape(1, D_MODEL)
    nw_mem = norm_mem[lyr].reshape(1, D_MODEL)
    nw_f = norm_f.reshape(1, D_MODEL)
    conv_w = gdn_conv_w[lyr]
    a_row = jnp.pad(gdn_a_log[lyr].reshape(1, GDN_H), ((0, 0), (GDN_H, LANE - 2 * GDN_H)))
    dt_row = jnp.pad(gdn_dt_bias[lyr].reshape(1, GDN_H), ((0, 0), (GDN_H, LANE - 2 * GDN_H)))
    gnw = gdn_norm[lyr].reshape(1, GDN_DV)
    sinks = swa_sinks[lyr]
    bias_p = _bias_prompt(rel_bias)
    npad = SUBLANE
    bias_s = _bias_sample(rel_bias, ns, wb, wb + npad)
    sink_rows = jnp.broadcast_to(jnp.repeat(sinks.reshape(SWA_KV, SWA_G), ns, axis=1)[:, :, None],
                                 (SWA_KV, SWA_G * ns, LANE))

    t = b * seq
    xp = x_prompt.reshape(t, D_MODEL)
    p_gdn = _norm_proj(xp, nw_in, w_gdn, BF16)
    p_ba = _norm_proj(xp, nw_in, w_ba, F32)
    p_swa = _norm_proj(xp, nw_in, w_swa, BF16)
    p_mem = _norm_proj(xp, nw_in, w_mem, BF16)
    g_p = _norm_proj(xp, nw_in, w_gate, BF16, act="sigmoid")
    x_tail = x_prompt[:, seq - WINDOW:, :].reshape(b * WINDOW, D_MODEL)
    p_tail = _norm_proj(x_tail, nw_in, w_tail, F32).reshape(b, WINDOW, GDN_CONV_CH + 2 * SWA_KVW)
    conv_p = p_tail[:, WINDOW - (CONV_W - 1):, :GDN_CONV_CH]
    swk_p = p_tail[:, :, GDN_CONV_CH:GDN_CONV_CH + SWA_KVW].reshape(b, WINDOW, SWA_KV, SWA_DH)
    swv_p = p_tail[:, :, GDN_CONV_CH + SWA_KVW:].reshape(b, WINDOW, SWA_KV, SWA_DH)

    mkv = _norm_proj(mem_prompt.reshape(b * N_MEM, D_MODEL), nw_mem, w_mkv, F32)
    mk_p = mkv[:, :MEM_W].reshape(b, N_MEM, MEM_W)
    mv_p = mkv[:, MEM_W:].reshape(b, N_MEM, MEM_W)

    o_gdn_p, s_p = _gdn(p_gdn, p_gdn, p_ba, jnp.zeros((b, HDR, GDN_CONV_CH), F32),
                        jnp.zeros((b, GDN_H, GDN_DK, GDN_DV), F32), conv_w, a_row, dt_row, gnw,
                        n_seq=b, chunk=GDN_CHUNK, n_chunks=4, per_chunk_state=False,
                        valid_lo=0, valid_hi=GDN_CHUNK, out_dtype=BF16,
                        z_colblock=GDN_CONV_CH // GDN_W)
    o_swa_p = _swa_prompt(p_swa, bias_p, sinks, b, seq)
    o_mem_p = _mem_prompt(p_mem, mk_p.astype(BF16), mv_p.astype(BF16), b, seq)
    y_p = _merge(o_gdn_p, o_swa_p, o_mem_p, g_p, xp, w_br, w_o, nw_f).reshape(b, seq, D_MODEL)

    ts = bd * ns
    xs = x_sample.reshape(ts, D_MODEL)
    s_gdn = _norm_proj(xs, nw_in, w_gdn, F32)
    s_ba = _norm_proj(xs, nw_in, w_ba, F32)
    s_swa = _norm_proj(xs, nw_in, w_swa, F32)
    s_mem = _norm_proj(xs, nw_in, w_mem, BF16)
    g_s = _norm_proj(xs, nw_in, w_gate, BF16, act="sigmoid")

    lo = CONV_W - 1
    hi = lo + ns
    pad_rows = ((0, 0), (lo, SUBLANE - hi), (0, 0))
    e_qkv = jnp.concatenate([state_gdn_conv[lyr], s_gdn[:, :GDN_CONV_CH].reshape(bd, ns, GDN_CONV_CH),
                             jnp.zeros((bd, SUBLANE - hi, GDN_CONV_CH), F32)], axis=1)
    e_z = jnp.pad(s_gdn[:, GDN_CONV_CH:].reshape(bd, ns, GDN_W), pad_rows)
    e_ba = jnp.pad(s_ba.reshape(bd, ns, LANE), pad_rows)
    seq_blk = 8
    o_gdn_s8, s_s = _gdn(e_qkv.reshape(bd * SUBLANE, GDN_CONV_CH), e_z.reshape(bd * SUBLANE, GDN_W),
                         e_ba.reshape(bd * SUBLANE, LANE), jnp.zeros((1, HDR, GDN_CONV_CH), F32),
                         state_gdn[lyr], conv_w, a_row, dt_row, gnw,
                         n_seq=bd, chunk=SUBLANE, n_chunks=seq_blk, per_chunk_state=True,
                         valid_lo=lo, valid_hi=hi, out_dtype=BF16)
    o_gdn_s = o_gdn_s8.reshape(bd, SUBLANE, GDN_W)[:, lo:hi].reshape(ts, GDN_W)
    conv_s = e_qkv[:, hi - (CONV_W - 1):hi]

    def to_heads(a):
        return a.reshape(bd, ns, SWA_KV, SWA_G, SWA_DH).transpose(0, 2, 3, 1, 4).reshape(
            bd, SWA_KV, SWA_G * ns, SWA_DH)

    k_new = s_swa[:, 2 * SWA_QW:2 * SWA_QW + SWA_KVW].reshape(bd, ns, SWA_KVW)
    v_new = s_swa[:, 2 * SWA_QW + SWA_KVW:].reshape(bd, ns, SWA_KVW)
    tok_pad = ((0, 0), (0, npad - ns), (0, 0))
    ck = cache_swa_k[lyr].reshape(bd, wb, SWA_KVW)
    cv = cache_swa_v[lyr].reshape(bd, wb, SWA_KVW)
    o_swa_h = _swa_sample(to_heads(s_swa[:, :SWA_QW]).astype(BF16), to_heads(s_swa[:, SWA_QW:2 * SWA_QW]),
                          jnp.pad(k_new, tok_pad), jnp.pad(v_new, tok_pad), ck, cv, bias_s, sink_rows)
    o_swa_s = o_swa_h.reshape(bd, SWA_KV, SWA_G, ns, SWA_DH).transpose(0, 3, 1, 2, 4).reshape(ts, SWA_QW)
    swk_s = jnp.concatenate([ck, k_new], axis=1)[:, ns:].reshape(bd, wb, SWA_KV, SWA_DH)
    swv_s = jnp.concatenate([cv, v_new], axis=1)[:, ns:].reshape(bd, wb, SWA_KV, SWA_DH)

    mq = jnp.pad(s_mem[:, :MEM_W].reshape(bd, ns, MEM_W), tok_pad)
    mz = jnp.pad(s_mem[:, MEM_W:].reshape(bd, ns, MEM_W), tok_pad)
    o_mem_s = _mem_sample(mq, mz, cache_mem_k[lyr].reshape(bd, N_MEM, MEM_W),
                          cache_mem_v[lyr].reshape(bd, N_MEM, MEM_W))[:, :ns].reshape(ts, MEM_W)
    y_s = _merge(o_gdn_s, o_swa_s, o_mem_s, g_s, xs, w_br, w_o, nw_f).reshape(bd, ns, D_MODEL)

    return (y_p, y_s,
            s_p[None], conv_p[None], swk_p[None], swv_p[None],
            mk_p.reshape(b, N_MEM, MEM_H, MEM_DH)[None], mv_p.reshape(b, N_MEM, MEM_H, MEM_DH)[None],
            s_s[None], conv_s[None], swk_s[None], swv_s[None])
```

```python
import functools
import math

import numpy as np
import jax
import jax.numpy as jnp
from jax import lax
from jax.experimental import pallas as pl
from jax.experimental.pallas import tpu as pltpu

F32 = jnp.float32
BF16 = jnp.bfloat16

D_MODEL = 2048
N_BRANCH = 3
BR_W = 1024
GDN_H = 8
GDN_DK = 128
GDN_DV = 128
GDN_QK = GDN_H * GDN_DK
GDN_W = GDN_H * GDN_DV
GDN_CONV_CH = 2 * GDN_QK + GDN_W
CONV_W = 4
GDN_CHUNK = 64
SWA_H = 16
SWA_KV = 2
SWA_G = SWA_H // SWA_KV
SWA_DH = 64
SWA_QW = SWA_H * SWA_DH
SWA_KVW = SWA_KV * SWA_DH
WINDOW = 128
N_BUCKETS = 32
MAX_DISTANCE = 128
N_MEM = 256
MEM_H = 4
MEM_DH = 256
MEM_W = MEM_H * MEM_DH
NORM_EPS = 1e-6

LANE = 128
SUBLANE = 8
VMEM_LIMIT = 52 * 1024 * 1024


def _cparams(sem):
    return pltpu.CompilerParams(dimension_semantics=sem, vmem_limit_bytes=VMEM_LIMIT)


def _sigmoid(x):
    return 1.0 / (1.0 + jnp.exp(-x))


def _silu(x):
    return x * _sigmoid(x)


def _softplus(x):
    return jnp.maximum(x, 0.0) + jnp.log(1.0 + jnp.exp(-jnp.abs(x)))


def _dot(a, b):
    return jnp.dot(a.astype(BF16), b.astype(BF16), preferred_element_type=F32)


def _dot_nt(a, b):
    return lax.dot_general(a.astype(BF16), b.astype(BF16), (((1,), (1,)), ((), ())),
                           preferred_element_type=F32)


def _dot_tn(a, b):
    return lax.dot_general(a.astype(BF16), b.astype(BF16), (((0,), (0,)), ((), ())),
                           preferred_element_type=F32)


def _dot_f32(a, b):
    return jnp.dot(a, b, preferred_element_type=F32, precision=lax.Precision.HIGHEST)


def _norm_proj_kernel(x_ref, nw_ref, w_ref, o_ref, xn_ref, *, act):
    @pl.when(pl.program_id(1) == 0)
    def _():
        x = x_ref[...].astype(F32)
        ms = jnp.mean(x * x, axis=-1, keepdims=True)
        xn_ref[...] = (x * lax.rsqrt(ms + NORM_EPS) * nw_ref[...]).astype(BF16)

    y = jnp.dot(xn_ref[...], w_ref[...], preferred_element_type=F32)
    if act == "sigmoid":
        y = _sigmoid(y)
    o_ref[...] = y.astype(o_ref.dtype)


def _pick_tile(n, pref):
    t = min(n, pref)
    while n % t:
        t -= LANE
    return t


def _norm_proj(x, norm_w, w, out_dtype, act=None, tm_pref=1024, tn_pref=1024):
    m, d = x.shape
    n = w.shape[1]
    tm = min(m, tm_pref)
    assert m % tm == 0
    tn = _pick_tile(n, tn_pref)
    return pl.pallas_call(
        functools.partial(_norm_proj_kernel, act=act),
        out_shape=jax.ShapeDtypeStruct((m, n), out_dtype),
        grid=(m // tm, n // tn),
        in_specs=[pl.BlockSpec((tm, d), lambda i, j: (i, 0)),
                  pl.BlockSpec((1, d), lambda i, j: (0, 0)),
                  pl.BlockSpec((d, tn), lambda i, j: (0, j))],
        out_specs=pl.BlockSpec((tm, tn), lambda i, j: (i, j)),
        scratch_shapes=[pltpu.VMEM((tm, d), BF16)],
        compiler_params=_cparams(("parallel", "arbitrary")),
        name="norm_proj",
    )(x, norm_w, w)


HDR = SUBLANE


def _tri_inv_many(a_list, c):
    row = lax.broadcasted_iota(jnp.int32, (c, c), 0)
    col = lax.broadcasted_iota(jnp.int32, (c, c), 1)
    eye = jnp.where(row == col, 1.0, 0.0).astype(F32)
    xs = [eye - a for a in a_list]
    bs = [_dot(a, a) for a in a_list]
    n = 2
    while n < c:
        xs = [x + _dot(x, b) for x, b in zip(xs, bs)]
        n *= 2
        if n < c:
            bs = [_dot(b, b) for b in bs]
    return xs


def _gdn_kernel(qkv_ref, z_ref, ba_ref, hdr_ref, s0_ref, cw_ref, arow_ref, dtrow_ref, nw_ref,
                o_ref, s_ref, buf_ref, cv_ref, gb_ref, oacc_ref,
                *, chunk, n_chunks, per_chunk_state, valid_lo, valid_hi):
    c = chunk
    tb = c * n_chunks

    if per_chunk_state:
        buf_ref[0:HDR, :] = jnp.zeros((HDR, GDN_CONV_CH), F32)
    else:
        @pl.when(pl.program_id(1) == 0)
        def _():
            buf_ref[0:HDR, :] = hdr_ref[0]
            s_ref[...] = s0_ref[...]

    buf_ref[HDR:HDR + tb, :] = qkv_ref[...].astype(F32)
    acc = None
    for j in range(CONV_W):
        off = HDR - (CONV_W - 1) + j
        term = buf_ref[off:off + tb, :] * cw_ref[j:j + 1, :]
        acc = term if acc is None else acc + term
    cv_ref[...] = _silu(acc)
    if not per_chunk_state:
        buf_ref[0:HDR, :] = buf_ref[tb:tb + HDR, :]

    ba = ba_ref[...].astype(F32)
    beta_all = _sigmoid(ba)
    g_all = -jnp.exp(arow_ref[...]) * _softplus(ba + dtrow_ref[...])
    if per_chunk_state:
        r = lax.broadcasted_iota(jnp.int32, (tb, LANE), 0) & (c - 1)
        valid = (r >= valid_lo) & (r < valid_hi)
        beta_all = jnp.where(valid, beta_all, 0.0)
        g_all = jnp.where(valid, g_all, 0.0)
    gb_ref[0] = beta_all
    gb_ref[1] = g_all

    row = lax.broadcasted_iota(jnp.int32, (c, c), 0)
    col = lax.broadcasted_iota(jnp.int32, (c, c), 1)
    causal = row >= col
    strict = row > col
    tril = jnp.where(causal, 1.0, 0.0).astype(F32)
    scale_q = GDN_DK ** -0.5

    def chunk_body(ci, carry):
        r0 = pl.multiple_of(ci * c, c)
        rows = pl.ds(r0, c)
        beta_c = gb_ref[0, rows, :]
        g_c = gb_ref[1, rows, :]
        gc_all = _dot_f32(tril, g_c)
        gc_t = gc_all.T
        si = ci if per_chunk_state else 0
        if per_chunk_state:
            rv = lax.broadcasted_iota(jnp.int32, (c, 1), 0)
            rvalid = (rv >= valid_lo) & (rv < valid_hi)
        heads = range(GDN_H)
        qs, ks, kbs, rhss, decays, e_gcs, gc_cols = [], [], [], [], [], [], []
        for h in heads:
            q = cv_ref[rows, h * GDN_DK:(h + 1) * GDN_DK]
            k = cv_ref[rows, GDN_QK + h * GDN_DK:GDN_QK + (h + 1) * GDN_DK]
            v = cv_ref[rows, 2 * GDN_QK + h * GDN_DV:2 * GDN_QK + (h + 1) * GDN_DV]
            q = q * lax.rsqrt(jnp.sum(q * q, axis=-1, keepdims=True) + NORM_EPS) * scale_q
            k = k * lax.rsqrt(jnp.sum(k * k, axis=-1, keepdims=True) + NORM_EPS)
            if per_chunk_state:
                q = jnp.where(rvalid, q, 0.0)
                k = jnp.where(rvalid, k, 0.0)
                v = jnp.where(rvalid, v, 0.0)
            beta = beta_c[:, h:h + 1]
            gc_col = gc_all[:, GDN_H + h:GDN_H + h + 1]
            gc_row = gc_t[GDN_H + h:GDN_H + h + 1, :]
            diff = jnp.where(causal, gc_col - gc_row, 0.0)
            decays.append(jnp.where(causal, jnp.exp(diff), 0.0))
            e_gc = jnp.exp(gc_col)
            kb = k * beta
            qs.append(q)
            ks.append(k)
            kbs.append(kb)
            e_gcs.append(e_gc)
            gc_cols.append(gc_col)
            rhss.append(jnp.concatenate([v * beta, kb * e_gc], axis=1))
        kq = [_dot_nt(jnp.concatenate([kbs[h], qs[h]], axis=0), ks[h]) for h in heads]
        a_list = [jnp.where(strict, kq[h][:c] * decays[h], 0.0) for h in heads]
        t_inv = _tri_inv_many(a_list, c)
        sols = [_dot(t_inv[h], rhss[h]) for h in heads]
        s_old = [s_ref[si, h] for h in heads]
        wq_s = [_dot(jnp.concatenate([sols[h][:, GDN_DV:], qs[h] * e_gcs[h]], axis=0), s_old[h])
                for h in heads]
        v_new = [sols[h][:, :GDN_DV] - wq_s[h][:c] for h in heads]
        o_att = [_dot(kq[h][c:] * decays[h], v_new[h]) for h in heads]
        for h in heads:
            gc_last = gc_cols[h][c - 1:c, :]
            k_tail = ks[h] * jnp.exp(gc_last - gc_cols[h])
            s_ref[si, h] = s_old[h] * jnp.exp(gc_last) + _dot_tn(k_tail, v_new[h])
        for h in heads:
            o = wq_s[h][c:] + o_att[h]
            o = o * lax.rsqrt(jnp.mean(o * o, axis=-1, keepdims=True) + NORM_EPS) * nw_ref[...]
            oacc_ref[rows, h * GDN_DV:(h + 1) * GDN_DV] = o
        return carry

    if per_chunk_state:
        s_ref[...] = s0_ref[...]
    lax.fori_loop(0, n_chunks, chunk_body, 0)
    o_ref[...] = (oacc_ref[...] * _silu(z_ref[...].astype(F32))).astype(o_ref.dtype)


def _gdn(qkv, z, ba, hdr, s0, conv_w, a_row, dt_row, norm_w, *, n_seq, chunk, n_chunks,
         per_chunk_state, valid_lo, valid_hi, out_dtype, z_colblock=0):
    rows = qkv.shape[0]
    tb = chunk * n_chunks
    if per_chunk_state:
        grid = (rows // tb,)
        sem = ("arbitrary",)
        rmap = lambda i: (i, 0)
        zmap = lambda i: (i, z_colblock)
        hmap = lambda i: (0, 0, 0)
        smap = lambda i: (i, 0, 0, 0)
        cmap = lambda i: (0, 0)
        ns = n_chunks
    else:
        steps = rows // n_seq // tb
        grid = (n_seq, steps)
        sem = ("parallel", "arbitrary")
        rmap = lambda b, n: (b * steps + n, 0)
        zmap = lambda b, n: (b * steps + n, z_colblock)
        hmap = lambda b, n: (b, 0, 0)
        smap = lambda b, n: (b, 0, 0, 0)
        cmap = lambda b, n: (0, 0)
        ns = 1
    kern = functools.partial(_gdn_kernel, chunk=chunk, n_chunks=n_chunks,
                             per_chunk_state=per_chunk_state, valid_lo=valid_lo, valid_hi=valid_hi)
    return pl.pallas_call(
        kern,
        out_shape=(jax.ShapeDtypeStruct((rows, GDN_W), out_dtype),
                   jax.ShapeDtypeStruct(s0.shape, F32)),
        grid=grid,
        in_specs=[pl.BlockSpec((tb, GDN_CONV_CH), rmap),
                  pl.BlockSpec((tb, GDN_W), zmap),
                  pl.BlockSpec((tb, LANE), rmap),
                  pl.BlockSpec((1, HDR, GDN_CONV_CH), hmap),
                  pl.BlockSpec((ns, GDN_H, GDN_DK, GDN_DV), smap),
                  pl.BlockSpec((CONV_W, GDN_CONV_CH), cmap),
                  pl.BlockSpec((1, LANE), cmap),
                  pl.BlockSpec((1, LANE), cmap),
                  pl.BlockSpec((1, GDN_DV), cmap)],
        out_specs=(pl.BlockSpec((tb, GDN_W), rmap),
                   pl.BlockSpec((ns, GDN_H, GDN_DK, GDN_DV), smap)),
        scratch_shapes=[pltpu.VMEM((HDR + tb, GDN_CONV_CH), F32),
                        pltpu.VMEM((tb, GDN_CONV_CH), F32),
                        pltpu.VMEM((2, tb, LANE), F32),
                        pltpu.VMEM((tb, GDN_W), F32)],
        compiler_params=_cparams(sem),
        name="gdn",
    )(qkv, z, ba, hdr, s0, conv_w, a_row, dt_row, norm_w)


def _t5_bucket_np(dist):
    n = np.maximum(dist, 0)
    max_exact = N_BUCKETS // 2
    nf = np.maximum(n, 1).astype(np.float32)
    large = max_exact + (np.log(nf / np.float32(max_exact)) / np.float32(math.log(MAX_DISTANCE / max_exact))
                         * np.float32(N_BUCKETS - max_exact)).astype(np.int32)
    large = np.minimum(large, N_BUCKETS - 1)
    return np.where(n < max_exact, n, large).astype(np.int32)


def _bias_prompt_kernel(code_ref, tab_ref, o_ref):
    h = pl.program_id(1)
    code = code_ref[0]
    acc = jnp.full(code.shape, -jnp.inf, F32)
    for b in range(N_BUCKETS):
        acc = jnp.where(code == b, tab_ref[b, h], acc)
    o_ref[0, 0] = acc


def _bias_prompt(table):
    qi = np.arange(WINDOW)[:, None] + WINDOW
    sj = np.arange(2 * WINDOW)[None, :]
    dist = qi - sj
    valid = (dist >= 0) & (dist < WINDOW)
    bucket = _t5_bucket_np(dist)
    code_rest = np.where(valid, bucket, -1)
    code_first = np.where(valid & (sj >= WINDOW), bucket, -1)
    code = jnp.asarray(np.stack([code_first, code_rest]).astype(np.int32))
    return pl.pallas_call(
        _bias_prompt_kernel,
        out_shape=jax.ShapeDtypeStruct((2, SWA_H, WINDOW, 2 * WINDOW), F32),
        grid=(2, SWA_H),
        in_specs=[pl.BlockSpec((1, WINDOW, 2 * WINDOW), lambda v, h: (v, 0, 0)),
                  pl.BlockSpec(memory_space=pltpu.SMEM)],
        out_specs=pl.BlockSpec((1, 1, WINDOW, 2 * WINDOW), lambda v, h: (v, h, 0, 0)),
        compiler_params=_cparams(("arbitrary", "arbitrary")),
        name="swa_bias_prompt",
    )(code, table)


def _bias_sample_kernel(code_ref, tab_ref, o_ref):
    kv = pl.program_id(0)
    code = code_ref[...]
    acc = jnp.full(code.shape, -jnp.inf, F32)
    for g in range(SWA_G):
        for b in range(N_BUCKETS):
            acc = jnp.where(code == b + N_BUCKETS * g, tab_ref[b, kv * SWA_G + g], acc)
    o_ref[0] = acc


def _bias_sample(table, n_tok, n_cache, n_keys_pad):
    dist = (n_cache + np.arange(n_tok))[:, None] - np.arange(n_keys_pad)[None, :]
    valid = (dist >= 0) & (dist < WINDOW) & (np.arange(n_keys_pad)[None, :] < n_cache + n_tok)
    bucket = _t5_bucket_np(dist)
    code_t = np.where(valid, bucket, -1)
    g = np.arange(SWA_G)[:, None, None]
    code = np.where(code_t[None] >= 0, code_t[None] + N_BUCKETS * g, -1)
    code = jnp.asarray(code.reshape(SWA_G * n_tok, n_keys_pad).astype(np.int32))
    return pl.pallas_call(
        _bias_sample_kernel,
        out_shape=jax.ShapeDtypeStruct((SWA_KV, SWA_G * n_tok, n_keys_pad), F32),
        grid=(SWA_KV,),
        in_specs=[pl.BlockSpec((SWA_G * n_tok, n_keys_pad), lambda k: (0, 0)),
                  pl.BlockSpec(memory_space=pltpu.SMEM)],
        out_specs=pl.BlockSpec((1, SWA_G * n_tok, n_keys_pad), lambda k: (k, 0, 0)),
        compiler_params=_cparams(("arbitrary",)),
        name="swa_bias_sample",
    )(code, table)


def _sink_softmax_pv(logits, sink, v):
    m = jnp.maximum(jnp.max(logits, axis=-1, keepdims=True), sink)
    p = jnp.exp(logits - m)
    den = jnp.sum(p, axis=-1, keepdims=True) + jnp.exp(sink - m)
    return _dot(p, v) / den


def _swa_prompt_kernel(q_ref, z_ref, kc_ref, kp_ref, vc_ref, vp_ref, bias_ref, sink_ref, o_ref):
    kk = jnp.concatenate([kp_ref[...], kc_ref[...]], axis=0)
    vv = jnp.concatenate([vp_ref[...], vc_ref[...]], axis=0)
    scale = SWA_DH ** -0.5
    for h in range(SWA_H):
        kv = h // SWA_G
        cs = slice(h * SWA_DH, (h + 1) * SWA_DH)
        ks = slice(kv * SWA_DH, (kv + 1) * SWA_DH)
        logits = _dot_nt(q_ref[:, cs], kk[:, ks]) * scale + bias_ref[0, h]
        oh = _sink_softmax_pv(logits, sink_ref[h], vv[:, ks])
        o_ref[:, cs] = (oh * _silu(z_ref[:, cs].astype(F32))).astype(o_ref.dtype)


def _swa_prompt(proj, bias, sinks, n_seq, seq_len):
    nb = seq_len // WINDOW
    kcol = 2 * SWA_QW // SWA_KVW
    vcol = kcol + 1
    cur = lambda b, n: b * nb + n
    prev = lambda b, n: b * nb + jnp.maximum(n - 1, 0)
    return pl.pallas_call(
        _swa_prompt_kernel,
        out_shape=jax.ShapeDtypeStruct((n_seq * seq_len, SWA_QW), BF16),
        grid=(n_seq, nb),
        in_specs=[pl.BlockSpec((WINDOW, SWA_QW), lambda b, n: (cur(b, n), 0)),
                  pl.BlockSpec((WINDOW, SWA_QW), lambda b, n: (cur(b, n), 1)),
                  pl.BlockSpec((WINDOW, SWA_KVW), lambda b, n: (cur(b, n), kcol)),
                  pl.BlockSpec((WINDOW, SWA_KVW), lambda b, n: (prev(b, n), kcol)),
                  pl.BlockSpec((WINDOW, SWA_KVW), lambda b, n: (cur(b, n), vcol)),
                  pl.BlockSpec((WINDOW, SWA_KVW), lambda b, n: (prev(b, n), vcol)),
                  pl.BlockSpec((1, SWA_H, WINDOW, 2 * WINDOW), lambda b, n: (jnp.minimum(n, 1), 0, 0, 0)),
                  pl.BlockSpec(memory_space=pltpu.SMEM)],
        out_specs=pl.BlockSpec((WINDOW, SWA_QW), lambda b, n: (cur(b, n), 0)),
        compiler_params=_cparams(("parallel", "arbitrary")),
        name="swa_prompt",
    )(proj, proj, proj, proj, proj, proj, bias, sinks)


def _swa_sample_kernel(q_ref, z_ref, kn_ref, vn_ref, ck_ref, cv_ref, bias_ref, sink_ref, o_ref, *, n_seq_blk):
    scale = SWA_DH ** -0.5
    for s in range(n_seq_blk):
        kk = jnp.concatenate([ck_ref[s], kn_ref[s]], axis=0)
        vv = jnp.concatenate([cv_ref[s], vn_ref[s]], axis=0)
        for kv in range(SWA_KV):
            ks = slice(kv * SWA_DH, (kv + 1) * SWA_DH)
            logits = _dot_nt(q_ref[s, kv], kk[:, ks]) * scale + bias_ref[kv]
            oh = _sink_softmax_pv(logits, sink_ref[kv][:, 0:1], vv[:, ks])
            o_ref[s, kv] = (oh * _silu(z_ref[s, kv].astype(F32))).astype(o_ref.dtype)


def _swa_sample(q, z, k_new, v_new, cache_k, cache_v, bias, sink_rows, n_seq_blk=8):
    bd, _, rows, _ = q.shape
    wb = cache_k.shape[1]
    npad = k_new.shape[1]
    blk4 = lambda i: (i, 0, 0, 0)
    blk3 = lambda i: (i, 0, 0)
    return pl.pallas_call(
        functools.partial(_swa_sample_kernel, n_seq_blk=n_seq_blk),
        out_shape=jax.ShapeDtypeStruct(q.shape, BF16),
        grid=(bd // n_seq_blk,),
        in_specs=[pl.BlockSpec((n_seq_blk, SWA_KV, rows, SWA_DH), blk4),
                  pl.BlockSpec((n_seq_blk, SWA_KV, rows, SWA_DH), blk4),
                  pl.BlockSpec((n_seq_blk, npad, SWA_KVW), blk3),
                  pl.BlockSpec((n_seq_blk, npad, SWA_KVW), blk3),
                  pl.BlockSpec((n_seq_blk, wb, SWA_KVW), blk3),
                  pl.BlockSpec((n_seq_blk, wb, SWA_KVW), blk3),
                  pl.BlockSpec((SWA_KV, rows, wb + npad), lambda i: (0, 0, 0)),
                  pl.BlockSpec((SWA_KV, rows, LANE), lambda i: (0, 0, 0))],
        out_specs=pl.BlockSpec((n_seq_blk, SWA_KV, rows, SWA_DH), blk4),
        compiler_params=_cparams(("arbitrary",)),
        name="swa_sample",
    )(q, z, k_new, v_new, cache_k, cache_v, bias, sink_rows)


def _mem_heads(q_of, z_of, k_of, v_of, store):
    scale = MEM_DH ** -0.5
    for h in range(MEM_H):
        logits = _dot_nt(q_of(h), k_of(h)) * scale
        m = jnp.max(logits, axis=-1, keepdims=True)
        p = jnp.exp(logits - m)
        den = jnp.sum(p, axis=-1, keepdims=True)
        oh = _dot(p, v_of(h)) / den
        store(h, oh * _silu(z_of(h).astype(F32)))


def _mem_cols(h):
    return slice(h * MEM_DH, (h + 1) * MEM_DH)


def _mem_prompt_kernel(q_ref, z_ref, k_ref, v_ref, o_ref):
    def store(h, val):
        o_ref[:, _mem_cols(h)] = val.astype(o_ref.dtype)
    _mem_heads(lambda h: q_ref[:, _mem_cols(h)], lambda h: z_ref[:, _mem_cols(h)],
               lambda h: k_ref[0, :, _mem_cols(h)], lambda h: v_ref[0, :, _mem_cols(h)], store)


def _mem_prompt(proj, mk, mv, n_seq, seq_len, tq=512):
    steps = seq_len // tq
    return pl.pallas_call(
        _mem_prompt_kernel,
        out_shape=jax.ShapeDtypeStruct((n_seq * seq_len, MEM_W), BF16),
        grid=(n_seq, steps),
        in_specs=[pl.BlockSpec((tq, MEM_W), lambda b, n: (b * steps + n, 0)),
                  pl.BlockSpec((tq, MEM_W), lambda b, n: (b * steps + n, 1)),
                  pl.BlockSpec((1, N_MEM, MEM_W), lambda b, n: (b, 0, 0)),
                  pl.BlockSpec((1, N_MEM, MEM_W), lambda b, n: (b, 0, 0))],
        out_specs=pl.BlockSpec((tq, MEM_W), lambda b, n: (b * steps + n, 0)),
        compiler_params=_cparams(("parallel", "arbitrary")),
        name="mem_prompt",
    )(proj, proj, mk, mv)


def _mem_sample_kernel(q_ref, z_ref, k_ref, v_ref, o_ref, *, n_seq_blk):
    for s in range(n_seq_blk):
        def store(h, val, s=s):
            o_ref[s, :, _mem_cols(h)] = val.astype(o_ref.dtype)
        _mem_heads(lambda h, s=s: q_ref[s, :, _mem_cols(h)], lambda h, s=s: z_ref[s, :, _mem_cols(h)],
                   lambda h, s=s: k_ref[s, :, _mem_cols(h)], lambda h, s=s: v_ref[s, :, _mem_cols(h)],
                   store)


def _mem_sample(q, z, cache_k, cache_v, n_seq_blk=8):
    bd, rows, _ = q.shape
    blk = lambda i: (i, 0, 0)
    return pl.pallas_call(
        functools.partial(_mem_sample_kernel, n_seq_blk=n_seq_blk),
        out_shape=jax.ShapeDtypeStruct(q.shape, BF16),
        grid=(bd // n_seq_blk,),
        in_specs=[pl.BlockSpec((n_seq_blk, rows, MEM_W), blk),
                  pl.BlockSpec((n_seq_blk, rows, MEM_W), blk),
                  pl.BlockSpec((n_seq_blk, N_MEM, MEM_W), blk),
                  pl.BlockSpec((n_seq_blk, N_MEM, MEM_W), blk)],
        out_specs=pl.BlockSpec((n_seq_blk, rows, MEM_W), blk),
        compiler_params=_cparams(("arbitrary",)),
        name="mem_sample",
    )(q, z, cache_k, cache_v)


def _merge_kernel(og_ref, os_ref, om_ref, gate_ref, x_ref, wb_ref, wo_ref, nf_ref, y_ref):
    merged = None
    for b, o_ref in enumerate((og_ref, os_ref, om_ref)):
        t = jnp.dot(o_ref[...], wb_ref[b], preferred_element_type=F32)
        t = t * gate_ref[:, b * D_MODEL:(b + 1) * D_MODEL].astype(F32)
        merged = t if merged is None else merged + t
    h = x_ref[...] + jnp.dot(merged.astype(BF16), wo_ref[...], preferred_element_type=F32)
    ms = jnp.mean(h * h, axis=-1, keepdims=True)
    y_ref[...] = h * lax.rsqrt(ms + NORM_EPS) * nf_ref[...]


def _merge(o_gdn, o_swa, o_mem, gates, x, w_branch, w_out, norm_f, tm=256):
    m = x.shape[0]
    tm = min(tm, m)
    row = lambda i: (i, 0)
    const2 = lambda i: (0, 0)
    return pl.pallas_call(
        _merge_kernel,
        out_shape=jax.ShapeDtypeStruct((m, D_MODEL), F32),
        grid=(m // tm,),
        in_specs=[pl.BlockSpec((tm, BR_W), row),
                  pl.BlockSpec((tm, BR_W), row),
                  pl.BlockSpec((tm, BR_W), row),
                  pl.BlockSpec((tm, N_BRANCH * D_MODEL), row),
                  pl.BlockSpec((tm, D_MODEL), row),
                  pl.BlockSpec((N_BRANCH, BR_W, D_MODEL), lambda i: (0, 0, 0),
                               pipeline_mode=pl.Buffered(1)),
                  pl.BlockSpec((D_MODEL, D_MODEL), const2, pipeline_mode=pl.Buffered(1)),
                  pl.BlockSpec((1, D_MODEL), const2)],
        out_specs=pl.BlockSpec((tm, D_MODEL), row),
        compiler_params=_cparams(("parallel",)),
        name="merge",
    )(o_gdn, o_swa, o_mem, gates, x, w_branch, w_out, norm_f)


_IN_SIZES = (GDN_QK, GDN_QK, GDN_W, GDN_W, GDN_H, GDN_H, SWA_QW, SWA_KVW, SWA_KVW, SWA_QW,
             MEM_W, MEM_W, N_BRANCH * D_MODEL)
_IN_NAMES = ("gq", "gk", "gv", "gz", "gb", "ga", "sq", "sk", "sv", "sz", "mq", "mz", "mg")
_IN_SPAN = {name: (int(off), int(off + size)) for name, off, size in
            zip(_IN_NAMES, np.cumsum((0,) + _IN_SIZES[:-1]), _IN_SIZES)}


def _cols(w, *names):
    return jnp.concatenate([w[:, _IN_SPAN[a][0]:_IN_SPAN[a][1]] for a in names], axis=1)


def kernel(x_prompt, x_sample, state_gdn, state_gdn_conv, cache_swa_k, cache_swa_v, cache_mem_k,
           cache_mem_v, mem_prompt, norm_in, w_in, gdn_conv_w, gdn_a_log, gdn_dt_bias, gdn_norm,
           swa_sinks, rel_bias, norm_mem, w_mem_kv, w_branch, w_out, norm_f):
    n_layers = norm_in.shape[0]
    assert n_layers == 1
    b, seq, _ = x_prompt.shape
    bd, ns, _ = x_sample.shape
    wb = cache_swa_k.shape[2]
    assert seq % WINDOW == 0 and seq % GDN_CHUNK == 0 and ns + CONV_W <= SUBLANE and wb == WINDOW
    lyr = 0

    w = w_in[lyr]
    w_gdn = _cols(w, "gq", "gk", "gv", "gz").astype(BF16)
    w_ba = jnp.pad(_cols(w, "gb", "ga"), ((0, 0), (0, LANE - 2 * GDN_H))).astype(BF16)
    w_swa = _cols(w, "sq", "sz", "sk", "sv").astype(BF16)
    w_mem = _cols(w, "mq", "mz").astype(BF16)
    w_gate = _cols(w, "mg").astype(BF16)
    w_tail = _cols(w, "gq", "gk", "gv", "sk", "sv").astype(BF16)
    w_mkv = w_mem_kv[lyr].astype(BF16)
    w_br = w_branch[lyr].astype(BF16)
    w_o = w_out[lyr].astype(BF16)
    nw_in = norm_in[lyr].reshape(1, D_MODEL)
    nw_mem = norm_mem[lyr].reshape(1, D_MODEL)
    nw_f = norm_f.reshape(1, D_MODEL)
    conv_w = gdn_conv_w[lyr]
    a_row = jnp.pad(gdn_a_log[lyr].reshape(1, GDN_H), ((0, 0), (GDN_H, LANE - 2 * GDN_H)))
    dt_row = jnp.pad(gdn_dt_bias[lyr].reshape(1, GDN_H), ((0, 0), (GDN_H, LANE - 2 * GDN_H)))
    gnw = gdn_norm[lyr].reshape(1, GDN_DV)
    sinks = swa_sinks[lyr]
    bias_p = _bias_prompt(rel_bias)
    npad = SUBLANE
    bias_s = _bias_sample(rel_bias, ns, wb, wb + npad)
    sink_rows = jnp.broadcast_to(jnp.repeat(sinks.reshape(SWA_KV, SWA_G), ns, axis=1)[:, :, None],
                                 (SWA_KV, SWA_G * ns, LANE))

    t = b * seq
    xp = x_prompt.reshape(t, D_MODEL)
    p_gdn = _norm_proj(xp, nw_in, w_gdn, BF16)
    p_ba = _norm_proj(xp, nw_in, w_ba, F32)
    p_swa = _norm_proj(xp, nw_in, w_swa, BF16)
    p_mem = _norm_proj(xp, nw_in, w_mem, BF16)
    g_p = _norm_proj(xp, nw_in, w_gate, BF16, act="sigmoid")
    x_tail = x_prompt[:, seq - WINDOW:, :].reshape(b * WINDOW, D_MODEL)
    p_tail = _norm_proj(x_tail, nw_in, w_tail, F32).reshape(b, WINDOW, GDN_CONV_CH + 2 * SWA_KVW)
    conv_p = p_tail[:, WINDOW - (CONV_W - 1):, :GDN_CONV_CH]
    swk_p = p_tail[:, :, GDN_CONV_CH:GDN_CONV_CH + SWA_KVW].reshape(b, WINDOW, SWA_KV, SWA_DH)
    swv_p = p_tail[:, :, GDN_CONV_CH + SWA_KVW:].reshape(b, WINDOW, SWA_KV, SWA_DH)

    mkv = _norm_proj(mem_prompt.reshape(b * N_MEM, D_MODEL), nw_mem, w_mkv, F32)
    mk_p = mkv[:, :MEM_W].reshape(b, N_MEM, MEM_W)
    mv_p = mkv[:, MEM_W:].reshape(b, N_MEM, MEM_W)

    o_gdn_p, s_p = _gdn(p_gdn, p_gdn, p_ba, jnp.zeros((b, HDR, GDN_CONV_CH), F32),
                        jnp.zeros((b, GDN_H, GDN_DK, GDN_DV), F32), conv_w, a_row, dt_row, gnw,
                        n_seq=b, chunk=GDN_CHUNK, n_chunks=4, per_chunk_state=False,
                        valid_lo=0, valid_hi=GDN_CHUNK, out_dtype=BF16,
                        z_colblock=GDN_CONV_CH // GDN_W)
    o_swa_p = _swa_prompt(p_swa, bias_p, sinks, b, seq)
    o_mem_p = _mem_prompt(p_mem, mk_p.astype(BF16), mv_p.astype(BF16), b, seq)
    y_p = _merge(o_gdn_p, o_swa_p, o_mem_p, g_p, xp, w_br, w_o, nw_f).reshape(b, seq, D_MODEL)

    ts = bd * ns
    xs = x_sample.reshape(ts, D_MODEL)
    s_gdn = _norm_proj(xs, nw_in, w_gdn, F32)
    s_ba = _norm_proj(xs, nw_in, w_ba, F32)
    s_swa = _norm_proj(xs, nw_in, w_swa, F32)
    s_mem = _norm_proj(xs, nw_in, w_mem, BF16)
    g_s = _norm_proj(xs, nw_in, w_gate, BF16, act="sigmoid")

    lo = CONV_W - 1
    hi = lo + ns
    pad_rows = ((0, 0), (lo, SUBLANE - hi), (0, 0))
    e_qkv = jnp.concatenate([state_gdn_conv[lyr], s_gdn[:, :GDN_CONV_CH].reshape(bd, ns, GDN_CONV_CH),
                             jnp.zeros((bd, SUBLANE - hi, GDN_CONV_CH), F32)], axis=1)
    e_z = jnp.pad(s_gdn[:, GDN_CONV_CH:].reshape(bd, ns, GDN_W), pad_rows)
    e_ba = jnp.pad(s_ba.reshape(bd, ns, LANE), pad_rows)
    seq_blk = 8
    o_gdn_s8, s_s = _gdn(e_qkv.reshape(bd * SUBLANE, GDN_CONV_CH), e_z.reshape(bd * SUBLANE, GDN_W),
                         e_ba.reshape(bd * SUBLANE, LANE), jnp.zeros((1, HDR, GDN_CONV_CH), F32),
                         state_gdn[lyr], conv_w, a_row, dt_row, gnw,
                         n_seq=bd, chunk=SUBLANE, n_chunks=seq_blk, per_chunk_state=True,
                         valid_lo=lo, valid_hi=hi, out_dtype=BF16)
    o_gdn_s = o_gdn_s8.reshape(bd, SUBLANE, GDN_W)[:, lo:hi].reshape(ts, GDN_W)
    conv_s = e_qkv[:, hi - (CONV_W - 1):hi]

    def to_heads(a):
        return a.reshape(bd, ns, SWA_KV, SWA_G, SWA_DH).transpose(0, 2, 3, 1, 4).reshape(
            bd, SWA_KV, SWA_G * ns, SWA_DH)

    k_new = s_swa[:, 2 * SWA_QW:2 * SWA_QW + SWA_KVW].reshape(bd, ns, SWA_KVW)
    v_new = s_swa[:, 2 * SWA_QW + SWA_KVW:].reshape(bd, ns, SWA_KVW)
    tok_pad = ((0, 0), (0, npad - ns), (0, 0))
    ck = cache_swa_k[lyr].reshape(bd, wb, SWA_KVW)
    cv = cache_swa_v[lyr].reshape(bd, wb, SWA_KVW)
    o_swa_h = _swa_sample(to_heads(s_swa[:, :SWA_QW]).astype(BF16), to_heads(s_swa[:, SWA_QW:2 * SWA_QW]),
                          jnp.pad(k_new, tok_pad), jnp.pad(v_new, tok_pad), ck, cv, bias_s, sink_rows)
    o_swa_s = o_swa_h.reshape(bd, SWA_KV, SWA_G, ns, SWA_DH).transpose(0, 3, 1, 2, 4).reshape(ts, SWA_QW)
    swk_s = jnp.concatenate([ck, k_new], axis=1)[:, ns:].reshape(bd, wb, SWA_KV, SWA_DH)
    swv_s = jnp.concatenate([cv, v_new], axis=1)[:, ns:].reshape(bd, wb, SWA_KV, SWA_DH)

    mq = jnp.pad(s_mem[:, :MEM_W].reshape(bd, ns, MEM_W), tok_pad)
    mz = jnp.pad(s_mem[:, MEM_W:].reshape(bd, ns, MEM_W), tok_pad)
    o_mem_s = _mem_sample(mq, mz, cache_mem_k[lyr].reshape(bd, N_MEM, MEM_W).astype(BF16),
                          cache_mem_v[lyr].reshape(bd, N_MEM, MEM_W).astype(BF16))[:, :ns].reshape(ts, MEM_W)
    y_s = _merge(o_gdn_s, o_swa_s, o_mem_s, g_s, xs, w_br, w_o, nw_f).reshape(bd, ns, D_MODEL)

    return (y_p, y_s,
            s_p[None], conv_p[None], swk_p[None], swv_p[None],
            mk_p.reshape(b, N_MEM, MEM_H, MEM_DH)[None], mv_p.reshape(b, N_MEM, MEM_H, MEM_DH)[None],
            s_s[None], conv_s[None], swk_s[None], swv_s[None])
```

```python
import functools
import math

import numpy as np
import jax
import jax.numpy as jnp
from jax import lax
from jax.experimental import pallas as pl
from jax.experimental.pallas import tpu as pltpu

F32 = jnp.float32
BF16 = jnp.bfloat16

D_MODEL = 2048
N_BRANCH = 3
BR_W = 1024
GDN_H = 8
GDN_DK = 128
GDN_DV = 128
GDN_QK = GDN_H * GDN_DK
GDN_W = GDN_H * GDN_DV
GDN_CONV_CH = 2 * GDN_QK + GDN_W
CONV_W = 4
GDN_CHUNK = 64
SWA_H = 16
SWA_KV = 2
SWA_G = SWA_H // SWA_KV
SWA_DH = 64
SWA_QW = SWA_H * SWA_DH
SWA_KVW = SWA_KV * SWA_DH
WINDOW = 128
N_BUCKETS = 32
MAX_DISTANCE = 128
N_MEM = 256
MEM_H = 4
MEM_DH = 256
MEM_W = MEM_H * MEM_DH
NORM_EPS = 1e-6

LANE = 128
SUBLANE = 8
VMEM_LIMIT = 52 * 1024 * 1024


def _cparams(sem):
    return pltpu.CompilerParams(dimension_semantics=sem, vmem_limit_bytes=VMEM_LIMIT)


def _sigmoid(x):
    return 1.0 / (1.0 + jnp.exp(-x))


def _silu(x):
    return x * _sigmoid(x)


def _softplus(x):
    return jnp.maximum(x, 0.0) + jnp.log(1.0 + jnp.exp(-jnp.abs(x)))


def _dot(a, b):
    return jnp.dot(a.astype(BF16), b.astype(BF16), preferred_element_type=F32)


def _dot_nt(a, b):
    return lax.dot_general(a.astype(BF16), b.astype(BF16), (((1,), (1,)), ((), ())),
                           preferred_element_type=F32)


def _dot_tn(a, b):
    return lax.dot_general(a.astype(BF16), b.astype(BF16), (((0,), (0,)), ((), ())),
                           preferred_element_type=F32)


def _dot_f32(a, b):
    return jnp.dot(a, b, preferred_element_type=F32, precision=lax.Precision.HIGHEST)


def _rmsnorm_kernel(x_ref, nw_ref, o_ref):
    x = x_ref[...].astype(F32)
    ms = jnp.mean(x * x, axis=-1, keepdims=True)
    o_ref[...] = (x * lax.rsqrt(ms + NORM_EPS) * nw_ref[...]).astype(o_ref.dtype)


def _rmsnorm(x, norm_w, tm_pref=512):
    m, d = x.shape
    tm = min(m, tm_pref)
    assert m % tm == 0
    return pl.pallas_call(
        _rmsnorm_kernel,
        out_shape=jax.ShapeDtypeStruct((m, d), BF16),
        grid=(m // tm,),
        in_specs=[pl.BlockSpec((tm, d), lambda i: (i, 0)),
                  pl.BlockSpec((1, d), lambda i: (0, 0))],
        out_specs=pl.BlockSpec((tm, d), lambda i: (i, 0)),
        compiler_params=_cparams(("parallel",)),
        name="rmsnorm",
    )(x, norm_w)


def _proj_kernel(x_ref, w_ref, o_ref, *, act):
    y = jnp.dot(x_ref[...], w_ref[...], preferred_element_type=F32)
    if act == "sigmoid":
        y = _sigmoid(y)
    o_ref[...] = y.astype(o_ref.dtype)


def _pick_tile(n, pref):
    t = min(n, pref)
    while n % t:
        t -= LANE
    return t


def _proj(xn, w, out_dtype, act=None, tm_pref=1024, tn_pref=2048):
    m, d = xn.shape
    n = w.shape[1]
    tm = min(m, tm_pref)
    assert m % tm == 0
    tn = _pick_tile(n, tn_pref if out_dtype == BF16 else tn_pref // 2)
    return pl.pallas_call(
        functools.partial(_proj_kernel, act=act),
        out_shape=jax.ShapeDtypeStruct((m, n), out_dtype),
        grid=(m // tm, n // tn),
        in_specs=[pl.BlockSpec((tm, d), lambda i, j: (i, 0)),
                  pl.BlockSpec((d, tn), lambda i, j: (0, j))],
        out_specs=pl.BlockSpec((tm, tn), lambda i, j: (i, j)),
        compiler_params=_cparams(("parallel", "arbitrary")),
        name="proj",
    )(xn, w)


HDR = SUBLANE


def _tri_inv_many(a_list, c):
    row = lax.broadcasted_iota(jnp.int32, (c, c), 0)
    col = lax.broadcasted_iota(jnp.int32, (c, c), 1)
    eye = jnp.where(row == col, 1.0, 0.0).astype(F32)
    xs = [eye - a for a in a_list]
    bs = [_dot(a, a) for a in a_list]
    n = 2
    while n < c:
        xs = [x + _dot(x, b) for x, b in zip(xs, bs)]
        n *= 2
        if n < c:
            bs = [_dot(b, b) for b in bs]
    return xs


def _gdn_kernel(qkv_ref, z_ref, ba_ref, hdr_ref, s0_ref, cw_ref, arow_ref, dtrow_ref, nw_ref,
                o_ref, s_ref, buf_ref, cv_ref, gb_ref, oacc_ref, sol_ref, att_ref, gl_ref,
                *, chunk, n_chunks, group, per_chunk_state, valid_lo, valid_hi):
    c = chunk
    tb = c * n_chunks

    if per_chunk_state:
        buf_ref[0:HDR, :] = jnp.zeros((HDR, GDN_CONV_CH), F32)
    else:
        @pl.when(pl.program_id(1) == 0)
        def _():
            buf_ref[0:HDR, :] = hdr_ref[0]
            s_ref[...] = s0_ref[...]

    buf_ref[HDR:HDR + tb, :] = qkv_ref[...].astype(F32)
    acc = None
    for j in range(CONV_W):
        off = HDR - (CONV_W - 1) + j
        term = buf_ref[off:off + tb, :] * cw_ref[j:j + 1, :]
        acc = term if acc is None else acc + term
    cv_ref[...] = _silu(acc)
    if not per_chunk_state:
        buf_ref[0:HDR, :] = buf_ref[tb:tb + HDR, :]

    ba = ba_ref[...].astype(F32)
    beta_all = _sigmoid(ba)
    g_all = -jnp.exp(arow_ref[...]) * _softplus(ba + dtrow_ref[...])
    if per_chunk_state:
        r = lax.broadcasted_iota(jnp.int32, (tb, LANE), 0) & (c - 1)
        valid = (r >= valid_lo) & (r < valid_hi)
        beta_all = jnp.where(valid, beta_all, 0.0)
        g_all = jnp.where(valid, g_all, 0.0)
    gb_ref[0] = beta_all
    gb_ref[1] = g_all

    row = lax.broadcasted_iota(jnp.int32, (c, c), 0)
    col = lax.broadcasted_iota(jnp.int32, (c, c), 1)
    causal = row >= col
    strict = row > col
    tril = jnp.where(causal, 1.0, 0.0).astype(F32)
    scale_q = GDN_DK ** -0.5

    heads = range(GDN_H)
    qcol = lambda h: slice(h * GDN_DK, (h + 1) * GDN_DK)
    kcol = lambda h: slice(GDN_QK + h * GDN_DK, GDN_QK + (h + 1) * GDN_DK)
    vcol = lambda h: slice(2 * GDN_QK + h * GDN_DV, 2 * GDN_QK + (h + 1) * GDN_DV)
    ucol = lambda h: slice(h * (GDN_DV + GDN_DK), h * (GDN_DV + GDN_DK) + GDN_DV)
    wcol = lambda h: slice(h * (GDN_DV + GDN_DK) + GDN_DV, (h + 1) * (GDN_DV + GDN_DK))
    if per_chunk_state:
        rv = lax.broadcasted_iota(jnp.int32, (c, 1), 0)
        rvalid = (rv >= valid_lo) & (rv < valid_hi)

    def prep_body(gi, carry):
        probs = []
        for j in range(group):
            r0 = pl.multiple_of((gi * group + j) * c, c)
            rows = pl.ds(r0, c)
            beta_c = gb_ref[0, rows, :]
            gc_all = _dot_f32(tril, gb_ref[1, rows, :])
            gc_t = gc_all.T
            gl_ref[gi * group + j] = gc_all[c - 1:c, :]
            for h in heads:
                q = cv_ref[rows, qcol(h)]
                k = cv_ref[rows, kcol(h)]
                v = cv_ref[rows, vcol(h)]
                q = q * lax.rsqrt(jnp.sum(q * q, axis=-1, keepdims=True) + NORM_EPS) * scale_q
                k = k * lax.rsqrt(jnp.sum(k * k, axis=-1, keepdims=True) + NORM_EPS)
                if per_chunk_state:
                    q = jnp.where(rvalid, q, 0.0)
                    k = jnp.where(rvalid, k, 0.0)
                    v = jnp.where(rvalid, v, 0.0)
                beta = beta_c[:, h:h + 1]
                gc_col = gc_all[:, GDN_H + h:GDN_H + h + 1]
                gc_row = gc_t[GDN_H + h:GDN_H + h + 1, :]
                diff = jnp.where(causal, gc_col - gc_row, 0.0)
                decay = jnp.where(causal, jnp.exp(diff), 0.0)
                e_gc = jnp.exp(gc_col)
                kb = k * beta
                rhs = jnp.concatenate([v * beta, kb * e_gc], axis=1)
                cv_ref[rows, qcol(h)] = q * e_gc
                cv_ref[rows, kcol(h)] = k * jnp.exp(gc_col[c - 1:c, :] - gc_col)
                probs.append((rows, h, q, k, kb, rhs, decay))
        kq = [_dot_nt(jnp.concatenate([kb, q], axis=0), k) for (_, _, q, k, kb, _, _) in probs]
        a_list = [jnp.where(strict, kq_i[:c] * p[6], 0.0) for kq_i, p in zip(kq, probs)]
        t_inv = _tri_inv_many(a_list, c)
        for t_i, kq_i, (rows, h, _, _, _, rhs, decay) in zip(t_inv, kq, probs):
            sol_ref[rows, h * (GDN_DV + GDN_DK):(h + 1) * (GDN_DV + GDN_DK)] = _dot(t_i, rhs)
            att_ref[h, rows, :] = kq_i[c:] * decay
        return carry

    def scan_body(ci, carry):
        r0 = pl.multiple_of(ci * c, c)
        rows = pl.ds(r0, c)
        si = ci if per_chunk_state else 0
        g_tot = jnp.exp(gl_ref[ci])
        s_old = [s_ref[si, h] for h in heads]
        wq_s = [_dot(jnp.concatenate([sol_ref[rows, wcol(h)], cv_ref[rows, qcol(h)]], axis=0), s_old[h])
                for h in heads]
        v_new = [sol_ref[rows, ucol(h)] - wq_s[h][:c] for h in heads]
        o_att = [_dot(att_ref[h, rows, :], v_new[h]) for h in heads]
        for h in heads:
            s_ref[si, h] = (s_old[h] * g_tot[:, GDN_H + h:GDN_H + h + 1]
                            + _dot_tn(cv_ref[rows, kcol(h)], v_new[h]))
        for h in heads:
            o = wq_s[h][c:] + o_att[h]
            o = o * lax.rsqrt(jnp.mean(o * o, axis=-1, keepdims=True) + NORM_EPS) * nw_ref[...]
            oacc_ref[rows, h * GDN_DV:(h + 1) * GDN_DV] = o
        return carry

    if per_chunk_state:
        s_ref[...] = s0_ref[...]
    lax.fori_loop(0, n_chunks // group, prep_body, 0)
    lax.fori_loop(0, n_chunks, scan_body, 0)
    o_ref[...] = (oacc_ref[...] * _silu(z_ref[...].astype(F32))).astype(o_ref.dtype)


def _gdn(qkv, z, ba, hdr, s0, conv_w, a_row, dt_row, norm_w, *, n_seq, chunk, n_chunks,
         per_chunk_state, valid_lo, valid_hi, out_dtype, z_colblock=0, ba_colblock=0):
    rows = qkv.shape[0]
    tb = chunk * n_chunks
    if per_chunk_state:
        grid = (rows // tb,)
        sem = ("arbitrary",)
        rmap = lambda i: (i, 0)
        zmap = lambda i: (i, z_colblock)
        bmap = lambda i: (i, ba_colblock)
        hmap = lambda i: (0, 0, 0)
        smap = lambda i: (i, 0, 0, 0)
        cmap = lambda i: (0, 0)
        ns = n_chunks
    else:
        steps = rows // n_seq // tb
        grid = (n_seq, steps)
        sem = ("parallel", "arbitrary")
        rmap = lambda b, n: (b * steps + n, 0)
        zmap = lambda b, n: (b * steps + n, z_colblock)
        bmap = lambda b, n: (b * steps + n, ba_colblock)
        hmap = lambda b, n: (b, 0, 0)
        smap = lambda b, n: (b, 0, 0, 0)
        cmap = lambda b, n: (0, 0)
        ns = 1
    group = 2 if n_chunks % 2 == 0 else 1
    kern = functools.partial(_gdn_kernel, chunk=chunk, n_chunks=n_chunks, group=group,
                             per_chunk_state=per_chunk_state, valid_lo=valid_lo, valid_hi=valid_hi)
    return pl.pallas_call(
        kern,
        out_shape=(jax.ShapeDtypeStruct((rows, GDN_W), out_dtype),
                   jax.ShapeDtypeStruct(s0.shape, F32)),
        grid=grid,
        in_specs=[pl.BlockSpec((tb, GDN_CONV_CH), rmap),
                  pl.BlockSpec((tb, GDN_W), zmap),
                  pl.BlockSpec((tb, LANE), bmap),
                  pl.BlockSpec((1, HDR, GDN_CONV_CH), hmap),
                  pl.BlockSpec((ns, GDN_H, GDN_DK, GDN_DV), smap),
                  pl.BlockSpec((CONV_W, GDN_CONV_CH), cmap),
                  pl.BlockSpec((1, LANE), cmap),
                  pl.BlockSpec((1, LANE), cmap),
                  pl.BlockSpec((1, GDN_DV), cmap)],
        out_specs=(pl.BlockSpec((tb, GDN_W), rmap),
                   pl.BlockSpec((ns, GDN_H, GDN_DK, GDN_DV), smap)),
        scratch_shapes=[pltpu.VMEM((HDR + tb, GDN_CONV_CH), F32),
                        pltpu.VMEM((tb, GDN_CONV_CH), F32),
                        pltpu.VMEM((2, tb, LANE), F32),
                        pltpu.VMEM((tb, GDN_W), F32),
                        pltpu.VMEM((tb, GDN_H * (GDN_DV + GDN_DK)), F32),
                        pltpu.VMEM((GDN_H, tb, chunk), F32),
                        pltpu.VMEM((n_chunks, 1, LANE), F32)],
        compiler_params=_cparams(sem),
        name="gdn",
    )(qkv, z, ba, hdr, s0, conv_w, a_row, dt_row, norm_w)


def _t5_bucket_np(dist):
    n = np.maximum(dist, 0)
    max_exact = N_BUCKETS // 2
    nf = np.maximum(n, 1).astype(np.float32)
    large = max_exact + (np.log(nf / np.float32(max_exact)) / np.float32(math.log(MAX_DISTANCE / max_exact))
                         * np.float32(N_BUCKETS - max_exact)).astype(np.int32)
    large = np.minimum(large, N_BUCKETS - 1)
    return np.where(n < max_exact, n, large).astype(np.int32)


def _bias_prompt_kernel(code_ref, tab_ref, o_ref):
    h = pl.program_id(1)
    code = code_ref[0]
    acc = jnp.full(code.shape, -jnp.inf, F32)
    for b in range(N_BUCKETS):
        acc = jnp.where(code == b, tab_ref[b, h], acc)
    o_ref[0, 0] = acc


def _bias_prompt(table):
    qi = np.arange(WINDOW)[:, None] + WINDOW
    sj = np.arange(2 * WINDOW)[None, :]
    dist = qi - sj
    valid = (dist >= 0) & (dist < WINDOW)
    bucket = _t5_bucket_np(dist)
    code_rest = np.where(valid, bucket, -1)
    code_first = np.where(valid & (sj >= WINDOW), bucket, -1)
    code = jnp.asarray(np.stack([code_first, code_rest]).astype(np.int32))
    return pl.pallas_call(
        _bias_prompt_kernel,
        out_shape=jax.ShapeDtypeStruct((2, SWA_H, WINDOW, 2 * WINDOW), F32),
        grid=(2, SWA_H),
        in_specs=[pl.BlockSpec((1, WINDOW, 2 * WINDOW), lambda v, h: (v, 0, 0)),
                  pl.BlockSpec(memory_space=pltpu.SMEM)],
        out_specs=pl.BlockSpec((1, 1, WINDOW, 2 * WINDOW), lambda v, h: (v, h, 0, 0)),
        compiler_params=_cparams(("arbitrary", "arbitrary")),
        name="swa_bias_prompt",
    )(code, table)


def _bias_sample_kernel(code_ref, tab_ref, o_ref):
    kv = pl.program_id(0)
    code = code_ref[...]
    acc = jnp.full(code.shape, -jnp.inf, F32)
    for g in range(SWA_G):
        for b in range(N_BUCKETS):
            acc = jnp.where(code == b + N_BUCKETS * g, tab_ref[b, kv * SWA_G + g], acc)
    o_ref[0] = acc


def _bias_sample(table, n_tok, n_cache, n_keys_pad):
    dist = (n_cache + np.arange(n_tok))[:, None] - np.arange(n_keys_pad)[None, :]
    valid = (dist >= 0) & (dist < WINDOW) & (np.arange(n_keys_pad)[None, :] < n_cache + n_tok)
    bucket = _t5_bucket_np(dist)
    code_t = np.where(valid, bucket, -1)
    g = np.arange(SWA_G)[:, None, None]
    code = np.where(code_t[None] >= 0, code_t[None] + N_BUCKETS * g, -1)
    code = jnp.asarray(code.reshape(SWA_G * n_tok, n_keys_pad).astype(np.int32))
    return pl.pallas_call(
        _bias_sample_kernel,
        out_shape=jax.ShapeDtypeStruct((SWA_KV, SWA_G * n_tok, n_keys_pad), F32),
        grid=(SWA_KV,),
        in_specs=[pl.BlockSpec((SWA_G * n_tok, n_keys_pad), lambda k: (0, 0)),
                  pl.BlockSpec(memory_space=pltpu.SMEM)],
        out_specs=pl.BlockSpec((1, SWA_G * n_tok, n_keys_pad), lambda k: (k, 0, 0)),
        compiler_params=_cparams(("arbitrary",)),
        name="swa_bias_sample",
    )(code, table)


def _sink_softmax_pv(logits, sink, v):
    m = jnp.maximum(jnp.max(logits, axis=-1, keepdims=True), sink)
    p = jnp.exp(logits - m)
    den = jnp.sum(p, axis=-1, keepdims=True) + jnp.exp(sink - m)
    return _dot(p, v) / den


def _swa_prompt_kernel(q_ref, z_ref, kc_ref, kp_ref, vc_ref, vp_ref, bias_ref, sink_ref, o_ref):
    kk = jnp.concatenate([kp_ref[...], kc_ref[...]], axis=0)
    vv = jnp.concatenate([vp_ref[...], vc_ref[...]], axis=0)
    scale = SWA_DH ** -0.5
    for h in range(SWA_H):
        kv = h // SWA_G
        cs = slice(h * SWA_DH, (h + 1) * SWA_DH)
        ks = slice(kv * SWA_DH, (kv + 1) * SWA_DH)
        logits = _dot_nt(q_ref[:, cs], kk[:, ks]) * scale + bias_ref[0, h]
        oh = _sink_softmax_pv(logits, sink_ref[h], vv[:, ks])
        o_ref[:, cs] = (oh * _silu(z_ref[:, cs].astype(F32))).astype(o_ref.dtype)


def _swa_prompt(proj, kv, bias, sinks, n_seq, seq_len, q_colblock, z_colblock, k_colblock, v_colblock):
    nb = seq_len // WINDOW
    cur = lambda b, n: b * nb + n
    prev = lambda b, n: b * nb + jnp.maximum(n - 1, 0)
    return pl.pallas_call(
        _swa_prompt_kernel,
        out_shape=jax.ShapeDtypeStruct((n_seq * seq_len, SWA_QW), BF16),
        grid=(n_seq, nb),
        in_specs=[pl.BlockSpec((WINDOW, SWA_QW), lambda b, n: (cur(b, n), q_colblock)),
                  pl.BlockSpec((WINDOW, SWA_QW), lambda b, n: (cur(b, n), z_colblock)),
                  pl.BlockSpec((WINDOW, SWA_KVW), lambda b, n: (cur(b, n), k_colblock)),
                  pl.BlockSpec((WINDOW, SWA_KVW), lambda b, n: (prev(b, n), k_colblock)),
                  pl.BlockSpec((WINDOW, SWA_KVW), lambda b, n: (cur(b, n), v_colblock)),
                  pl.BlockSpec((WINDOW, SWA_KVW), lambda b, n: (prev(b, n), v_colblock)),
                  pl.BlockSpec((1, SWA_H, WINDOW, 2 * WINDOW), lambda b, n: (jnp.minimum(n, 1), 0, 0, 0)),
                  pl.BlockSpec(memory_space=pltpu.SMEM)],
        out_specs=pl.BlockSpec((WINDOW, SWA_QW), lambda b, n: (cur(b, n), 0)),
        compiler_params=_cparams(("parallel", "arbitrary")),
        name="swa_prompt",
    )(proj, proj, kv, kv, kv, kv, bias, sinks)


def _swa_sample_kernel(q_ref, z_ref, kn_ref, vn_ref, ck_ref, cv_ref, bias_ref, sink_ref, o_ref, *, n_seq_blk):
    scale = SWA_DH ** -0.5
    for s in range(n_seq_blk):
        kk = jnp.concatenate([ck_ref[s], kn_ref[s]], axis=0)
        vv = jnp.concatenate([cv_ref[s], vn_ref[s]], axis=0)
        for kv in range(SWA_KV):
            ks = slice(kv * SWA_DH, (kv + 1) * SWA_DH)
            logits = _dot_nt(q_ref[s, kv], kk[:, ks]) * scale + bias_ref[kv]
            oh = _sink_softmax_pv(logits, sink_ref[kv][:, 0:1], vv[:, ks])
            o_ref[s, kv] = (oh * _silu(z_ref[s, kv].astype(F32))).astype(o_ref.dtype)


def _swa_sample(q, z, k_new, v_new, cache_k, cache_v, bias, sink_rows, n_seq_blk=8):
    bd, _, rows, _ = q.shape
    wb = cache_k.shape[1]
    npad = k_new.shape[1]
    blk4 = lambda i: (i, 0, 0, 0)
    blk3 = lambda i: (i, 0, 0)
    return pl.pallas_call(
        functools.partial(_swa_sample_kernel, n_seq_blk=n_seq_blk),
        out_shape=jax.ShapeDtypeStruct(q.shape, BF16),
        grid=(bd // n_seq_blk,),
        in_specs=[pl.BlockSpec((n_seq_blk, SWA_KV, rows, SWA_DH), blk4),
                  pl.BlockSpec((n_seq_blk, SWA_KV, rows, SWA_DH), blk4),
                  pl.BlockSpec((n_seq_blk, npad, SWA_KVW), blk3),
                  pl.BlockSpec((n_seq_blk, npad, SWA_KVW), blk3),
                  pl.BlockSpec((n_seq_blk, wb, SWA_KVW), blk3),
                  pl.BlockSpec((n_seq_blk, wb, SWA_KVW), blk3),
                  pl.BlockSpec((SWA_KV, rows, wb + npad), lambda i: (0, 0, 0)),
                  pl.BlockSpec((SWA_KV, rows, LANE), lambda i: (0, 0, 0))],
        out_specs=pl.BlockSpec((n_seq_blk, SWA_KV, rows, SWA_DH), blk4),
        compiler_params=_cparams(("arbitrary",)),
        name="swa_sample",
    )(q, z, k_new, v_new, cache_k, cache_v, bias, sink_rows)


def _mem_heads(q_of, z_of, k_of, v_of, store):
    scale = MEM_DH ** -0.5
    for h in range(MEM_H):
        logits = _dot_nt(q_of(h), k_of(h)) * scale
        m = jnp.max(logits, axis=-1, keepdims=True)
        p = jnp.exp(logits - m)
        den = jnp.sum(p, axis=-1, keepdims=True)
        oh = _dot(p, v_of(h)) / den
        store(h, oh * _silu(z_of(h).astype(F32)))


def _mem_cols(h):
    return slice(h * MEM_DH, (h + 1) * MEM_DH)


def _mem_prompt_kernel(q_ref, z_ref, k_ref, v_ref, o_ref):
    def store(h, val):
        o_ref[:, _mem_cols(h)] = val.astype(o_ref.dtype)
    _mem_heads(lambda h: q_ref[:, _mem_cols(h)], lambda h: z_ref[:, _mem_cols(h)],
               lambda h: k_ref[0, :, _mem_cols(h)], lambda h: v_ref[0, :, _mem_cols(h)], store)


def _mem_prompt(proj, mk, mv, n_seq, seq_len, q_colblock, z_colblock, tq=512):
    steps = seq_len // tq
    return pl.pallas_call(
        _mem_prompt_kernel,
        out_shape=jax.ShapeDtypeStruct((n_seq * seq_len, MEM_W), BF16),
        grid=(n_seq, steps),
        in_specs=[pl.BlockSpec((tq, MEM_W), lambda b, n: (b * steps + n, q_colblock)),
                  pl.BlockSpec((tq, MEM_W), lambda b, n: (b * steps + n, z_colblock)),
                  pl.BlockSpec((1, N_MEM, MEM_W), lambda b, n: (b, 0, 0)),
                  pl.BlockSpec((1, N_MEM, MEM_W), lambda b, n: (b, 0, 0))],
        out_specs=pl.BlockSpec((tq, MEM_W), lambda b, n: (b * steps + n, 0)),
        compiler_params=_cparams(("parallel", "arbitrary")),
        name="mem_prompt",
    )(proj, proj, mk, mv)


def _mem_sample_kernel(q_ref, z_ref, k_hbm, v_hbm, o_ref, kbuf, vbuf, sem, *, n_seq_blk):
    i = pl.program_id(0)
    n_steps = pl.num_programs(0)
    slot = i % 2

    def copies(step, slot_):
        seqs = pl.ds(step * n_seq_blk, n_seq_blk)
        out = []
        for h in range(MEM_H):
            out.append(pltpu.make_async_copy(k_hbm.at[seqs, :, h, :], kbuf.at[slot_, h], sem.at[0, slot_, h]))
            out.append(pltpu.make_async_copy(v_hbm.at[seqs, :, h, :], vbuf.at[slot_, h], sem.at[1, slot_, h]))
        return out

    @pl.when(i == 0)
    def _():
        for cp in copies(0, 0):
            cp.start()

    @pl.when(i + 1 < n_steps)
    def _():
        for cp in copies(i + 1, 1 - slot):
            cp.start()

    for cp in copies(i, slot):
        cp.wait()

    for s in range(n_seq_blk):
        def store(h, val, s=s):
            o_ref[s, :, _mem_cols(h)] = val.astype(o_ref.dtype)
        _mem_heads(lambda h, s=s: q_ref[s, :, _mem_cols(h)], lambda h, s=s: z_ref[s, :, _mem_cols(h)],
                   lambda h, s=s: kbuf[slot, h, s], lambda h, s=s: vbuf[slot, h, s], store)


def _mem_sample(q, z, cache_k, cache_v, n_seq_blk=4):
    bd, rows, _ = q.shape
    blk = lambda i: (i, 0, 0)
    buf = pltpu.VMEM((2, MEM_H, n_seq_blk, N_MEM, MEM_DH), cache_k.dtype)
    return pl.pallas_call(
        functools.partial(_mem_sample_kernel, n_seq_blk=n_seq_blk),
        out_shape=jax.ShapeDtypeStruct(q.shape, BF16),
        grid=(bd // n_seq_blk,),
        in_specs=[pl.BlockSpec((n_seq_blk, rows, MEM_W), blk),
                  pl.BlockSpec((n_seq_blk, rows, MEM_W), blk),
                  pl.BlockSpec(memory_space=pl.ANY),
                  pl.BlockSpec(memory_space=pl.ANY)],
        out_specs=pl.BlockSpec((n_seq_blk, rows, MEM_W), blk),
        scratch_shapes=[buf, buf, pltpu.SemaphoreType.DMA((2, 2, MEM_H))],
        compiler_params=_cparams(("arbitrary",)),
        name="mem_sample",
    )(q, z, cache_k, cache_v)


def _merge_kernel(og_ref, os_ref, om_ref, gate_ref, x_ref, wb_ref, wo_ref, nf_ref, y_ref):
    merged = None
    for b, o_ref in enumerate((og_ref, os_ref, om_ref)):
        t = jnp.dot(o_ref[...], wb_ref[b], preferred_element_type=F32)
        t = t * gate_ref[:, b * D_MODEL:(b + 1) * D_MODEL].astype(F32)
        merged = t if merged is None else merged + t
    h = x_ref[...] + jnp.dot(merged.astype(BF16), wo_ref[...], preferred_element_type=F32)
    ms = jnp.mean(h * h, axis=-1, keepdims=True)
    y_ref[...] = h * lax.rsqrt(ms + NORM_EPS) * nf_ref[...]


def _merge(o_gdn, o_swa, o_mem, gates, x, w_branch, w_out, norm_f, tm=256):
    m = x.shape[0]
    tm = min(tm, m)
    row = lambda i: (i, 0)
    const2 = lambda i: (0, 0)
    return pl.pallas_call(
        _merge_kernel,
        out_shape=jax.ShapeDtypeStruct((m, D_MODEL), F32),
        grid=(m // tm,),
        in_specs=[pl.BlockSpec((tm, BR_W), row),
                  pl.BlockSpec((tm, BR_W), row),
                  pl.BlockSpec((tm, BR_W), row),
                  pl.BlockSpec((tm, N_BRANCH * D_MODEL), row),
                  pl.BlockSpec((tm, D_MODEL), row),
                  pl.BlockSpec((N_BRANCH, BR_W, D_MODEL), lambda i: (0, 0, 0),
                               pipeline_mode=pl.Buffered(1)),
                  pl.BlockSpec((D_MODEL, D_MODEL), const2, pipeline_mode=pl.Buffered(1)),
                  pl.BlockSpec((1, D_MODEL), const2)],
        out_specs=pl.BlockSpec((tm, D_MODEL), row),
        compiler_params=_cparams(("parallel",)),
        name="merge",
    )(o_gdn, o_swa, o_mem, gates, x, w_branch, w_out, norm_f)


_IN_SIZES = (GDN_QK, GDN_QK, GDN_W, GDN_W, GDN_H, GDN_H, SWA_QW, SWA_KVW, SWA_KVW, SWA_QW,
             MEM_W, MEM_W, N_BRANCH * D_MODEL)
_IN_NAMES = ("gq", "gk", "gv", "gz", "gb", "ga", "sq", "sk", "sv", "sz", "mq", "mz", "mg")
_IN_SPAN = {name: (int(off), int(off + size)) for name, off, size in
            zip(_IN_NAMES, np.cumsum((0,) + _IN_SIZES[:-1]), _IN_SIZES)}


def _cols(w, *names):
    return jnp.concatenate([w[:, _IN_SPAN[a][0]:_IN_SPAN[a][1]] for a in names], axis=1)


def kernel(x_prompt, x_sample, state_gdn, state_gdn_conv, cache_swa_k, cache_swa_v, cache_mem_k,
           cache_mem_v, mem_prompt, norm_in, w_in, gdn_conv_w, gdn_a_log, gdn_dt_bias, gdn_norm,
           swa_sinks, rel_bias, norm_mem, w_mem_kv, w_branch, w_out, norm_f):
    n_layers = norm_in.shape[0]
    assert n_layers == 1
    b, seq, _ = x_prompt.shape
    bd, ns, _ = x_sample.shape
    wb = cache_swa_k.shape[2]
    assert seq % WINDOW == 0 and seq % GDN_CHUNK == 0 and ns + CONV_W <= SUBLANE and wb == WINDOW
    lyr = 0

    w = w_in[lyr]
    w_main = _cols(w, "gq", "gk", "gv", "gz", "sq", "sz", "mq", "mz").astype(BF16)
    cb_gz, cb_sq, cb_sz, cb_mq, cb_mz = 3, 4, 5, 6, 7
    w_small = jnp.pad(_cols(w, "sk", "sv", "gb", "ga"), ((0, 0), (0, LANE - 2 * GDN_H))).astype(BF16)
    cb_sk, cb_sv, cb_ba = 0, 1, 2
    w_gate = _cols(w, "mg").astype(BF16)
    w_qkv = w_main[:, :GDN_CONV_CH]
    w_mkv = w_mem_kv[lyr].astype(BF16)
    w_br = w_branch[lyr].astype(BF16)
    w_o = w_out[lyr].astype(BF16)
    nw_in = norm_in[lyr].reshape(1, D_MODEL)
    nw_mem = norm_mem[lyr].reshape(1, D_MODEL)
    nw_f = norm_f.reshape(1, D_MODEL)
    conv_w = gdn_conv_w[lyr]
    a_row = jnp.pad(gdn_a_log[lyr].reshape(1, GDN_H), ((0, 0), (GDN_H, LANE - 2 * GDN_H)))
    dt_row = jnp.pad(gdn_dt_bias[lyr].reshape(1, GDN_H), ((0, 0), (GDN_H, LANE - 2 * GDN_H)))
    gnw = gdn_norm[lyr].reshape(1, GDN_DV)
    sinks = swa_sinks[lyr]
    bias_p = _bias_prompt(rel_bias)
    npad = SUBLANE
    bias_s = _bias_sample(rel_bias, ns, wb, wb + npad)
    sink_rows = jnp.broadcast_to(jnp.repeat(sinks.reshape(SWA_KV, SWA_G), ns, axis=1)[:, :, None],
                                 (SWA_KV, SWA_G * ns, LANE))

    t = b * seq
    xp = x_prompt.reshape(t, D_MODEL)
    xn_p = _rmsnorm(xp, nw_in)
    p_main = _proj(xn_p, w_main, BF16)
    p_small = _proj(xn_p, w_small, F32)
    g_p = _proj(xn_p, w_gate, BF16, act="sigmoid")
    xn_tail = xn_p.reshape(b, seq, D_MODEL)[:, seq - SUBLANE:, :].reshape(b * SUBLANE, D_MODEL)
    conv_p = _proj(xn_tail, w_qkv, F32).reshape(b, SUBLANE, GDN_CONV_CH)[:, SUBLANE - (CONV_W - 1):]
    kv_tail = p_small.reshape(b, seq, 3 * LANE)[:, seq - WINDOW:]
    swk_p = kv_tail[:, :, cb_sk * LANE:(cb_sk + 1) * LANE].reshape(b, WINDOW, SWA_KV, SWA_DH)
    swv_p = kv_tail[:, :, cb_sv * LANE:(cb_sv + 1) * LANE].reshape(b, WINDOW, SWA_KV, SWA_DH)

    mkv = _proj(_rmsnorm(mem_prompt.reshape(b * N_MEM, D_MODEL), nw_mem), w_mkv, F32)
    mk_p = mkv[:, :MEM_W].reshape(b, N_MEM, MEM_W)
    mv_p = mkv[:, MEM_W:].reshape(b, N_MEM, MEM_W)

    o_gdn_p, s_p = _gdn(p_main, p_main, p_small, jnp.zeros((b, HDR, GDN_CONV_CH), F32),
                        jnp.zeros((b, GDN_H, GDN_DK, GDN_DV), F32), conv_w, a_row, dt_row, gnw,
                        n_seq=b, chunk=GDN_CHUNK, n_chunks=4, per_chunk_state=False,
                        valid_lo=0, valid_hi=GDN_CHUNK, out_dtype=BF16,
                        z_colblock=cb_gz, ba_colblock=cb_ba)
    o_swa_p = _swa_prompt(p_main, p_small, bias_p, sinks, b, seq, cb_sq, cb_sz, cb_sk, cb_sv)
    o_mem_p = _mem_prompt(p_main, mk_p.astype(BF16), mv_p.astype(BF16), b, seq, cb_mq, cb_mz)
    y_p = _merge(o_gdn_p, o_swa_p, o_mem_p, g_p, xp, w_br, w_o, nw_f).reshape(b, seq, D_MODEL)

    ts = bd * ns
    xs = x_sample.reshape(ts, D_MODEL)
    xn_s = _rmsnorm(xs, nw_in)
    s_main = _proj(xn_s, w_main, F32)
    s_small = _proj(xn_s, w_small, F32)
    g_s = _proj(xn_s, w_gate, BF16, act="sigmoid")
    s_gdn = s_main[:, :GDN_CONV_CH + GDN_W]
    s_ba = s_small[:, cb_ba * LANE:(cb_ba + 1) * LANE]
    s_swa = jnp.concatenate([s_main[:, cb_sq * BR_W:(cb_sz + 1) * BR_W], s_small[:, :2 * LANE]], axis=1)
    s_mem = s_main[:, cb_mq * BR_W:(cb_mz + 1) * BR_W].astype(BF16)

    lo = CONV_W - 1
    hi = lo + ns
    pad_rows = ((0, 0), (lo, SUBLANE - hi), (0, 0))
    e_qkv = jnp.concatenate([state_gdn_conv[lyr], s_gdn[:, :GDN_CONV_CH].reshape(bd, ns, GDN_CONV_CH),
                             jnp.zeros((bd, SUBLANE - hi, GDN_CONV_CH), F32)], axis=1)
    e_z = jnp.pad(s_gdn[:, GDN_CONV_CH:].reshape(bd, ns, GDN_W), pad_rows)
    e_ba = jnp.pad(s_ba.reshape(bd, ns, LANE), pad_rows)
    seq_blk = 8
    o_gdn_s8, s_s = _gdn(e_qkv.reshape(bd * SUBLANE, GDN_CONV_CH), e_z.reshape(bd * SUBLANE, GDN_W),
                         e_ba.reshape(bd * SUBLANE, LANE), jnp.zeros((1, HDR, GDN_CONV_CH), F32),
                         state_gdn[lyr], conv_w, a_row, dt_row, gnw,
                         n_seq=bd, chunk=SUBLANE, n_chunks=seq_blk, per_chunk_state=True,
                         valid_lo=lo, valid_hi=hi, out_dtype=BF16)
    o_gdn_s = o_gdn_s8.reshape(bd, SUBLANE, GDN_W)[:, lo:hi].reshape(ts, GDN_W)
    conv_s = e_qkv[:, hi - (CONV_W - 1):hi]

    def to_heads(a):
        return a.reshape(bd, ns, SWA_KV, SWA_G, SWA_DH).transpose(0, 2, 3, 1, 4).reshape(
            bd, SWA_KV, SWA_G * ns, SWA_DH)

    k_new = s_swa[:, 2 * SWA_QW:2 * SWA_QW + SWA_KVW].reshape(bd, ns, SWA_KVW)
    v_new = s_swa[:, 2 * SWA_QW + SWA_KVW:].reshape(bd, ns, SWA_KVW)
    tok_pad = ((0, 0), (0, npad - ns), (0, 0))
    ck = cache_swa_k[lyr].reshape(bd, wb, SWA_KVW)
    cv = cache_swa_v[lyr].reshape(bd, wb, SWA_KVW)
    o_swa_h = _swa_sample(to_heads(s_swa[:, :SWA_QW]).astype(BF16), to_heads(s_swa[:, SWA_QW:2 * SWA_QW]),
                          jnp.pad(k_new, tok_pad), jnp.pad(v_new, tok_pad), ck, cv, bias_s, sink_rows)
    o_swa_s = o_swa_h.reshape(bd, SWA_KV, SWA_G, ns, SWA_DH).transpose(0, 3, 1, 2, 4).reshape(ts, SWA_QW)
    swk_s = jnp.concatenate([ck, k_new], axis=1)[:, ns:].reshape(bd, wb, SWA_KV, SWA_DH)
    swv_s = jnp.concatenate([cv, v_new], axis=1)[:, ns:].reshape(bd, wb, SWA_KV, SWA_DH)

    mq = jnp.pad(s_mem[:, :MEM_W].reshape(bd, ns, MEM_W), tok_pad)
    mz = jnp.pad(s_mem[:, MEM_W:].reshape(bd, ns, MEM_W), tok_pad)
    o_mem_s = _mem_sample(mq, mz, cache_mem_k[lyr], cache_mem_v[lyr])[:, :ns].reshape(ts, MEM_W)
    y_s = _merge(o_gdn_s, o_swa_s, o_mem_s, g_s, xs, w_br, w_o, nw_f).reshape(bd, ns, D_MODEL)

    return (y_p, y_s,
            s_p[None], conv_p[None], swk_p[None], swv_p[None],
            mk_p.reshape(b, N_MEM, MEM_H, MEM_DH)[None], mv_p.reshape(b, N_MEM, MEM_H, MEM_DH)[None],
            s_s[None], conv_s[None], swk_s[None], swv_s[None])
```

```python
import functools
import math

import numpy as np
import jax
import jax.numpy as jnp
from jax import lax
from jax.experimental import pallas as pl
from jax.experimental.pallas import tpu as pltpu

F32 = jnp.float32
BF16 = jnp.bfloat16

D_MODEL = 2048
N_BRANCH = 3
BR_W = 1024
GDN_H = 8
GDN_DK = 128
GDN_DV = 128
GDN_QK = GDN_H * GDN_DK
GDN_W = GDN_H * GDN_DV
GDN_CONV_CH = 2 * GDN_QK + GDN_W
CONV_W = 4
GDN_CHUNK = 64
SWA_H = 16
SWA_KV = 2
SWA_G = SWA_H // SWA_KV
SWA_DH = 64
SWA_QW = SWA_H * SWA_DH
SWA_KVW = SWA_KV * SWA_DH
WINDOW = 128
N_BUCKETS = 32
MAX_DISTANCE = 128
N_MEM = 256
MEM_H = 4
MEM_DH = 256
MEM_W = MEM_H * MEM_DH
NORM_EPS = 1e-6

LANE = 128
SUBLANE = 8
VMEM_LIMIT = 52 * 1024 * 1024


def _cparams(sem):
    return pltpu.CompilerParams(dimension_semantics=sem, vmem_limit_bytes=VMEM_LIMIT)


def _sigmoid(x):
    return 1.0 / (1.0 + jnp.exp(-x))


def _silu(x):
    return x * _sigmoid(x)


def _softplus(x):
    return jnp.maximum(x, 0.0) + jnp.log(1.0 + jnp.exp(-jnp.abs(x)))


def _dot(a, b):
    return jnp.dot(a.astype(BF16), b.astype(BF16), preferred_element_type=F32)


def _dot_nt(a, b):
    return lax.dot_general(a.astype(BF16), b.astype(BF16), (((1,), (1,)), ((), ())),
                           preferred_element_type=F32)


def _dot_tn(a, b):
    return lax.dot_general(a.astype(BF16), b.astype(BF16), (((0,), (0,)), ((), ())),
                           preferred_element_type=F32)


def _dot_f32(a, b):
    return jnp.dot(a, b, preferred_element_type=F32, precision=lax.Precision.HIGHEST)


def _rmsnorm_kernel(x_ref, nw_ref, o_ref):
    x = x_ref[...].astype(F32)
    ms = jnp.mean(x * x, axis=-1, keepdims=True)
    o_ref[...] = (x * lax.rsqrt(ms + NORM_EPS) * nw_ref[...]).astype(o_ref.dtype)


def _rmsnorm(x, norm_w, tm_pref=512):
    m, d = x.shape
    tm = min(m, tm_pref)
    assert m % tm == 0
    return pl.pallas_call(
        _rmsnorm_kernel,
        out_shape=jax.ShapeDtypeStruct((m, d), BF16),
        grid=(m // tm,),
        in_specs=[pl.BlockSpec((tm, d), lambda i: (i, 0)),
                  pl.BlockSpec((1, d), lambda i: (0, 0))],
        out_specs=pl.BlockSpec((tm, d), lambda i: (i, 0)),
        compiler_params=_cparams(("parallel",)),
        name="rmsnorm",
    )(x, norm_w)


def _proj_kernel(x_ref, w_ref, o_ref, *, act):
    y = jnp.dot(x_ref[...], w_ref[...], preferred_element_type=F32)
    if act == "sigmoid":
        y = _sigmoid(y)
    o_ref[...] = y.astype(o_ref.dtype)


def _pick_tile(n, pref):
    t = min(n, pref)
    while n % t:
        t -= LANE
    return t


def _proj(xn, w, out_dtype, act=None, tm_pref=1024, tn_pref=2048):
    m, d = xn.shape
    n = w.shape[1]
    tm = min(m, tm_pref)
    assert m % tm == 0
    tn = _pick_tile(n, tn_pref if out_dtype == BF16 else tn_pref // 2)
    return pl.pallas_call(
        functools.partial(_proj_kernel, act=act),
        out_shape=jax.ShapeDtypeStruct((m, n), out_dtype),
        grid=(m // tm, n // tn),
        in_specs=[pl.BlockSpec((tm, d), lambda i, j: (i, 0)),
                  pl.BlockSpec((d, tn), lambda i, j: (0, j))],
        out_specs=pl.BlockSpec((tm, tn), lambda i, j: (i, j)),
        compiler_params=_cparams(("parallel", "arbitrary")),
        name="proj",
    )(xn, w)


HDR = SUBLANE


def _tri_inv_many(a_list, c):
    row = lax.broadcasted_iota(jnp.int32, (c, c), 0)
    col = lax.broadcasted_iota(jnp.int32, (c, c), 1)
    eye = jnp.where(row == col, 1.0, 0.0).astype(F32)
    xs = [eye - a for a in a_list]
    bs = [_dot(a, a) for a in a_list]
    n = 2
    while n < c:
        xs = [x + _dot(x, b) for x, b in zip(xs, bs)]
        n *= 2
        if n < c:
            bs = [_dot(b, b) for b in bs]
    return xs


def _gdn_kernel(qkv_ref, z_ref, ba_ref, hdr_ref, s0_ref, cw_ref, arow_ref, dtrow_ref, nw_ref,
                o_ref, s_ref, buf_ref, cv_ref, gb_ref, oacc_ref, sol_ref, att_ref, gl_ref,
                *, chunk, n_chunks, group, unroll, per_chunk_state, valid_lo, valid_hi):
    c = chunk
    tb = c * n_chunks

    if per_chunk_state:
        buf_ref[0:2 * HDR, :] = jnp.zeros((2 * HDR, GDN_CONV_CH), F32)
    else:
        @pl.when(pl.program_id(1) == 0)
        def _():
            buf_ref[0:HDR, :] = hdr_ref[0]
            buf_ref[HDR:2 * HDR, :] = jnp.zeros((HDR, GDN_CONV_CH), F32)
            s_ref[...] = s0_ref[...]

    def tap_from_history(j, n_rows):
        off = HDR - (CONV_W - 1) + j
        return buf_ref[off:off + n_rows, :] * cw_ref[j:j + 1, :]

    if qkv_ref.dtype == BF16:
        xb = qkv_ref[...]
        r = lax.broadcasted_iota(jnp.int32, (tb, tb), 0)
        cc = lax.broadcasted_iota(jnp.int32, (tb, tb), 1)
        acc = xb.astype(F32) * cw_ref[CONV_W - 1:CONV_W, :]
        for j in range(CONV_W - 1):
            shift = jnp.where(r - cc == CONV_W - 1 - j, 1.0, 0.0).astype(BF16)
            acc = acc + jnp.dot(shift, xb, preferred_element_type=F32) * cw_ref[j:j + 1, :]
        cv_ref[...] = _silu(acc)
        top = acc[0:HDR]
        for j in range(CONV_W - 1):
            top = top + tap_from_history(j, HDR)
        cv_ref[0:HDR, :] = _silu(top)
        buf_ref[0:HDR, :] = qkv_ref[tb - 2 * HDR:tb, :].astype(F32)[HDR:]
    else:
        buf_ref[HDR:HDR + tb, :] = qkv_ref[...].astype(F32)
        acc = None
        for j in range(CONV_W):
            term = tap_from_history(j, tb)
            acc = term if acc is None else acc + term
        cv_ref[...] = _silu(acc)
        if not per_chunk_state:
            buf_ref[0:HDR, :] = buf_ref[tb:tb + HDR, :]

    ba = ba_ref[...].astype(F32)
    beta_all = _sigmoid(ba)
    g_all = -jnp.exp(arow_ref[...]) * _softplus(ba + dtrow_ref[...])
    if per_chunk_state:
        r = lax.broadcasted_iota(jnp.int32, (tb, LANE), 0) & (c - 1)
        valid = (r >= valid_lo) & (r < valid_hi)
        beta_all = jnp.where(valid, beta_all, 0.0)
        g_all = jnp.where(valid, g_all, 0.0)
    gb_ref[0] = beta_all
    gb_ref[1] = g_all

    row = lax.broadcasted_iota(jnp.int32, (c, c), 0)
    col = lax.broadcasted_iota(jnp.int32, (c, c), 1)
    causal = row >= col
    strict = row > col
    tril = jnp.where(causal, 1.0, 0.0).astype(F32)
    scale_q = GDN_DK ** -0.5

    heads = range(GDN_H)
    qcol = lambda h: slice(h * GDN_DK, (h + 1) * GDN_DK)
    kcol = lambda h: slice(GDN_QK + h * GDN_DK, GDN_QK + (h + 1) * GDN_DK)
    vcol = lambda h: slice(2 * GDN_QK + h * GDN_DV, 2 * GDN_QK + (h + 1) * GDN_DV)
    ucol = lambda h: slice(h * (GDN_DV + GDN_DK), h * (GDN_DV + GDN_DK) + GDN_DV)
    wcol = lambda h: slice(h * (GDN_DV + GDN_DK) + GDN_DV, (h + 1) * (GDN_DV + GDN_DK))
    if per_chunk_state:
        rv = lax.broadcasted_iota(jnp.int32, (c, 1), 0)
        rvalid = (rv >= valid_lo) & (rv < valid_hi)

    def chunk_rows(ci):
        if isinstance(ci, int):
            return slice(ci * c, (ci + 1) * c)
        return pl.ds(pl.multiple_of(ci * c, c), c)

    def prep_body(gi, carry):
        probs = []
        for j in range(group):
            rows = chunk_rows(gi * group + j)
            beta_c = gb_ref[0, rows, :]
            gc_all = _dot_f32(tril, gb_ref[1, rows, :])
            gc_t = gc_all.T
            gl_ref[gi * group + j] = gc_all[c - 1:c, :]
            for h in heads:
                q = cv_ref[rows, qcol(h)]
                k = cv_ref[rows, kcol(h)]
                v = cv_ref[rows, vcol(h)]
                q = q * lax.rsqrt(jnp.sum(q * q, axis=-1, keepdims=True) + NORM_EPS) * scale_q
                k = k * lax.rsqrt(jnp.sum(k * k, axis=-1, keepdims=True) + NORM_EPS)
                if per_chunk_state:
                    q = jnp.where(rvalid, q, 0.0)
                    k = jnp.where(rvalid, k, 0.0)
                    v = jnp.where(rvalid, v, 0.0)
                beta = beta_c[:, h:h + 1]
                gc_col = gc_all[:, GDN_H + h:GDN_H + h + 1]
                gc_row = gc_t[GDN_H + h:GDN_H + h + 1, :]
                diff = jnp.where(causal, gc_col - gc_row, 0.0)
                decay = jnp.where(causal, jnp.exp(diff), 0.0)
                e_gc = jnp.exp(gc_col)
                kb = k * beta
                rhs = jnp.concatenate([v * beta, kb * e_gc], axis=1)
                cv_ref[rows, qcol(h)] = q * e_gc
                cv_ref[rows, kcol(h)] = k * jnp.exp(gc_col[c - 1:c, :] - gc_col)
                probs.append((rows, h, q, k, kb, rhs, decay))
        kq = [_dot_nt(jnp.concatenate([kb, q], axis=0), k) for (_, _, q, k, kb, _, _) in probs]
        a_list = [jnp.where(strict, kq_i[:c] * p[6], 0.0) for kq_i, p in zip(kq, probs)]
        t_inv = _tri_inv_many(a_list, c)
        for t_i, kq_i, (rows, h, _, _, _, rhs, decay) in zip(t_inv, kq, probs):
            sol_ref[rows, h * (GDN_DV + GDN_DK):(h + 1) * (GDN_DV + GDN_DK)] = _dot(t_i, rhs)
            att_ref[h, rows, :] = kq_i[c:] * decay
        return carry

    def scan_body(ci, carry):
        rows = chunk_rows(ci)
        si = ci if per_chunk_state else 0
        g_tot = jnp.exp(gl_ref[ci])
        s_old = [s_ref[si, h] for h in heads]
        wq_s = [_dot(jnp.concatenate([sol_ref[rows, wcol(h)], cv_ref[rows, qcol(h)]], axis=0), s_old[h])
                for h in heads]
        v_new = [sol_ref[rows, ucol(h)] - wq_s[h][:c] for h in heads]
        o_att = [_dot(att_ref[h, rows, :], v_new[h]) for h in heads]
        for h in heads:
            s_ref[si, h] = (s_old[h] * g_tot[:, GDN_H + h:GDN_H + h + 1]
                            + _dot_tn(cv_ref[rows, kcol(h)], v_new[h]))
        for h in heads:
            o = wq_s[h][c:] + o_att[h]
            o = o * lax.rsqrt(jnp.mean(o * o, axis=-1, keepdims=True) + NORM_EPS) * nw_ref[...]
            oacc_ref[rows, h * GDN_DV:(h + 1) * GDN_DV] = o
        return carry

    if per_chunk_state:
        s_ref[...] = s0_ref[...]
    if unroll:
        for gi in range(n_chunks // group):
            prep_body(gi, 0)
        for ci in range(n_chunks):
            scan_body(ci, 0)
    else:
        lax.fori_loop(0, n_chunks // group, prep_body, 0)
        lax.fori_loop(0, n_chunks, scan_body, 0)
    o_ref[...] = (oacc_ref[...] * _silu(z_ref[...].astype(F32))).astype(o_ref.dtype)


def _gdn(qkv, z, ba, hdr, s0, conv_w, a_row, dt_row, norm_w, *, n_seq, chunk, n_chunks,
         per_chunk_state, valid_lo, valid_hi, out_dtype, z_colblock=0, ba_colblock=0, group=2,
         unroll=False):
    rows = qkv.shape[0]
    tb = chunk * n_chunks
    if per_chunk_state:
        grid = (rows // tb,)
        sem = ("arbitrary",)
        rmap = lambda i: (i, 0)
        zmap = lambda i: (i, z_colblock)
        bmap = lambda i: (i, ba_colblock)
        hmap = lambda i: (0, 0, 0)
        smap = lambda i: (i, 0, 0, 0)
        cmap = lambda i: (0, 0)
        ns = n_chunks
    else:
        steps = rows // n_seq // tb
        grid = (n_seq, steps)
        sem = ("parallel", "arbitrary")
        rmap = lambda b, n: (b * steps + n, 0)
        zmap = lambda b, n: (b * steps + n, z_colblock)
        bmap = lambda b, n: (b * steps + n, ba_colblock)
        hmap = lambda b, n: (b, 0, 0)
        smap = lambda b, n: (b, 0, 0, 0)
        cmap = lambda b, n: (0, 0)
        ns = 1
    assert n_chunks % group == 0
    kern = functools.partial(_gdn_kernel, chunk=chunk, n_chunks=n_chunks, group=group, unroll=unroll,
                             per_chunk_state=per_chunk_state, valid_lo=valid_lo, valid_hi=valid_hi)
    return pl.pallas_call(
        kern,
        out_shape=(jax.ShapeDtypeStruct((rows, GDN_W), out_dtype),
                   jax.ShapeDtypeStruct(s0.shape, F32)),
        grid=grid,
        in_specs=[pl.BlockSpec((tb, GDN_CONV_CH), rmap),
                  pl.BlockSpec((tb, GDN_W), zmap),
                  pl.BlockSpec((tb, LANE), bmap),
                  pl.BlockSpec((1, HDR, GDN_CONV_CH), hmap),
                  pl.BlockSpec((ns, GDN_H, GDN_DK, GDN_DV), smap),
                  pl.BlockSpec((CONV_W, GDN_CONV_CH), cmap),
                  pl.BlockSpec((1, LANE), cmap),
                  pl.BlockSpec((1, LANE), cmap),
                  pl.BlockSpec((1, GDN_DV), cmap)],
        out_specs=(pl.BlockSpec((tb, GDN_W), rmap),
                   pl.BlockSpec((ns, GDN_H, GDN_DK, GDN_DV), smap)),
        scratch_shapes=[pltpu.VMEM((HDR + tb, GDN_CONV_CH), F32),
                        pltpu.VMEM((tb, GDN_CONV_CH), F32),
                        pltpu.VMEM((2, tb, LANE), F32),
                        pltpu.VMEM((tb, GDN_W), F32),
                        pltpu.VMEM((tb, GDN_H * (GDN_DV + GDN_DK)), F32),
                        pltpu.VMEM((GDN_H, tb, chunk), F32),
                        pltpu.VMEM((n_chunks, 1, LANE), F32)],
        compiler_params=_cparams(sem),
        name="gdn",
    )(qkv, z, ba, hdr, s0, conv_w, a_row, dt_row, norm_w)


def _t5_bucket_np(dist):
    n = np.maximum(dist, 0)
    max_exact = N_BUCKETS // 2
    nf = np.maximum(n, 1).astype(np.float32)
    large = max_exact + (np.log(nf / np.float32(max_exact)) / np.float32(math.log(MAX_DISTANCE / max_exact))
                         * np.float32(N_BUCKETS - max_exact)).astype(np.int32)
    large = np.minimum(large, N_BUCKETS - 1)
    return np.where(n < max_exact, n, large).astype(np.int32)


def _bias_prompt_kernel(code_ref, tab_ref, o_ref):
    h = pl.program_id(1)
    code = code_ref[0]
    acc = jnp.full(code.shape, -jnp.inf, F32)
    for b in range(N_BUCKETS):
        acc = jnp.where(code == b, tab_ref[b, h], acc)
    o_ref[0, 0] = acc


def _bias_prompt(table):
    qi = np.arange(WINDOW)[:, None] + WINDOW
    sj = np.arange(2 * WINDOW)[None, :]
    dist = qi - sj
    valid = (dist >= 0) & (dist < WINDOW)
    bucket = _t5_bucket_np(dist)
    code_rest = np.where(valid, bucket, -1)
    code_first = np.where(valid & (sj >= WINDOW), bucket, -1)
    code = jnp.asarray(np.stack([code_first, code_rest]).astype(np.int32))
    return pl.pallas_call(
        _bias_prompt_kernel,
        out_shape=jax.ShapeDtypeStruct((2, SWA_H, WINDOW, 2 * WINDOW), F32),
        grid=(2, SWA_H),
        in_specs=[pl.BlockSpec((1, WINDOW, 2 * WINDOW), lambda v, h: (v, 0, 0)),
                  pl.BlockSpec(memory_space=pltpu.SMEM)],
        out_specs=pl.BlockSpec((1, 1, WINDOW, 2 * WINDOW), lambda v, h: (v, h, 0, 0)),
        compiler_params=_cparams(("arbitrary", "arbitrary")),
        name="swa_bias_prompt",
    )(code, table)


def _bias_sample_kernel(code_ref, tab_ref, o_ref):
    kv = pl.program_id(0)
    code = code_ref[...]
    acc = jnp.full(code.shape, -jnp.inf, F32)
    for g in range(SWA_G):
        for b in range(N_BUCKETS):
            acc = jnp.where(code == b + N_BUCKETS * g, tab_ref[b, kv * SWA_G + g], acc)
    o_ref[0] = acc


def _bias_sample(table, n_tok, n_cache, n_keys_pad):
    dist = (n_cache + np.arange(n_tok))[:, None] - np.arange(n_keys_pad)[None, :]
    valid = (dist >= 0) & (dist < WINDOW) & (np.arange(n_keys_pad)[None, :] < n_cache + n_tok)
    bucket = _t5_bucket_np(dist)
    code_t = np.where(valid, bucket, -1)
    g = np.arange(SWA_G)[:, None, None]
    code = np.where(code_t[None] >= 0, code_t[None] + N_BUCKETS * g, -1)
    code = jnp.asarray(code.reshape(SWA_G * n_tok, n_keys_pad).astype(np.int32))
    return pl.pallas_call(
        _bias_sample_kernel,
        out_shape=jax.ShapeDtypeStruct((SWA_KV, SWA_G * n_tok, n_keys_pad), F32),
        grid=(SWA_KV,),
        in_specs=[pl.BlockSpec((SWA_G * n_tok, n_keys_pad), lambda k: (0, 0)),
                  pl.BlockSpec(memory_space=pltpu.SMEM)],
        out_specs=pl.BlockSpec((1, SWA_G * n_tok, n_keys_pad), lambda k: (k, 0, 0)),
        compiler_params=_cparams(("arbitrary",)),
        name="swa_bias_sample",
    )(code, table)


def _sink_softmax_pv(logits, sink, v):
    m = jnp.maximum(jnp.max(logits, axis=-1, keepdims=True), sink)
    p = jnp.exp(logits - m)
    den = jnp.sum(p, axis=-1, keepdims=True) + jnp.exp(sink - m)
    return _dot(p, v) / den


def _swa_prompt_kernel(q_ref, z_ref, kc_ref, kp_ref, vc_ref, vp_ref, bias_ref, sink_ref, o_ref):
    kk = jnp.concatenate([kp_ref[...], kc_ref[...]], axis=0)
    vv = jnp.concatenate([vp_ref[...], vc_ref[...]], axis=0)
    scale = SWA_DH ** -0.5
    for h in range(SWA_H):
        kv = h // SWA_G
        cs = slice(h * SWA_DH, (h + 1) * SWA_DH)
        ks = slice(kv * SWA_DH, (kv + 1) * SWA_DH)
        logits = _dot_nt(q_ref[:, cs], kk[:, ks]) * scale + bias_ref[0, h]
        oh = _sink_softmax_pv(logits, sink_ref[h], vv[:, ks])
        o_ref[:, cs] = (oh * _silu(z_ref[:, cs].astype(F32))).astype(o_ref.dtype)


def _swa_prompt(proj, kv, bias, sinks, n_seq, seq_len, q_colblock, z_colblock, k_colblock, v_colblock):
    nb = seq_len // WINDOW
    cur = lambda b, n: b * nb + n
    prev = lambda b, n: b * nb + jnp.maximum(n - 1, 0)
    return pl.pallas_call(
        _swa_prompt_kernel,
        out_shape=jax.ShapeDtypeStruct((n_seq * seq_len, SWA_QW), BF16),
        grid=(n_seq, nb),
        in_specs=[pl.BlockSpec((WINDOW, SWA_QW), lambda b, n: (cur(b, n), q_colblock)),
                  pl.BlockSpec((WINDOW, SWA_QW), lambda b, n: (cur(b, n), z_colblock)),
                  pl.BlockSpec((WINDOW, SWA_KVW), lambda b, n: (cur(b, n), k_colblock)),
                  pl.BlockSpec((WINDOW, SWA_KVW), lambda b, n: (prev(b, n), k_colblock)),
                  pl.BlockSpec((WINDOW, SWA_KVW), lambda b, n: (cur(b, n), v_colblock)),
                  pl.BlockSpec((WINDOW, SWA_KVW), lambda b, n: (prev(b, n), v_colblock)),
                  pl.BlockSpec((1, SWA_H, WINDOW, 2 * WINDOW), lambda b, n: (jnp.minimum(n, 1), 0, 0, 0)),
                  pl.BlockSpec(memory_space=pltpu.SMEM)],
        out_specs=pl.BlockSpec((WINDOW, SWA_QW), lambda b, n: (cur(b, n), 0)),
        compiler_params=_cparams(("parallel", "arbitrary")),
        name="swa_prompt",
    )(proj, proj, kv, kv, kv, kv, bias, sinks)


def _swa_sample_kernel(q_ref, z_ref, kn_ref, vn_ref, ck_ref, cv_ref, bias_ref, sink_ref, o_ref, *, n_seq_blk):
    scale = SWA_DH ** -0.5
    for s in range(n_seq_blk):
        kk = jnp.concatenate([ck_ref[s], kn_ref[s]], axis=0)
        vv = jnp.concatenate([cv_ref[s], vn_ref[s]], axis=0)
        for kv in range(SWA_KV):
            ks = slice(kv * SWA_DH, (kv + 1) * SWA_DH)
            logits = _dot_nt(q_ref[s, kv], kk[:, ks]) * scale + bias_ref[kv]
            oh = _sink_softmax_pv(logits, sink_ref[kv][:, 0:1], vv[:, ks])
            o_ref[s, kv] = (oh * _silu(z_ref[s, kv].astype(F32))).astype(o_ref.dtype)


def _swa_sample(q, z, k_new, v_new, cache_k, cache_v, bias, sink_rows, n_seq_blk=8):
    bd, _, rows, _ = q.shape
    wb = cache_k.shape[1]
    npad = k_new.shape[1]
    blk4 = lambda i: (i, 0, 0, 0)
    blk3 = lambda i: (i, 0, 0)
    return pl.pallas_call(
        functools.partial(_swa_sample_kernel, n_seq_blk=n_seq_blk),
        out_shape=jax.ShapeDtypeStruct(q.shape, BF16),
        grid=(bd // n_seq_blk,),
        in_specs=[pl.BlockSpec((n_seq_blk, SWA_KV, rows, SWA_DH), blk4),
                  pl.BlockSpec((n_seq_blk, SWA_KV, rows, SWA_DH), blk4),
                  pl.BlockSpec((n_seq_blk, npad, SWA_KVW), blk3),
                  pl.BlockSpec((n_seq_blk, npad, SWA_KVW), blk3),
                  pl.BlockSpec((n_seq_blk, wb, SWA_KVW), blk3),
                  pl.BlockSpec((n_seq_blk, wb, SWA_KVW), blk3),
                  pl.BlockSpec((SWA_KV, rows, wb + npad), lambda i: (0, 0, 0)),
                  pl.BlockSpec((SWA_KV, rows, LANE), lambda i: (0, 0, 0))],
        out_specs=pl.BlockSpec((n_seq_blk, SWA_KV, rows, SWA_DH), blk4),
        compiler_params=_cparams(("arbitrary",)),
        name="swa_sample",
    )(q, z, k_new, v_new, cache_k, cache_v, bias, sink_rows)


def _mem_heads(q_of, z_of, k_of, v_of, store):
    scale = MEM_DH ** -0.5
    for h in range(MEM_H):
        logits = _dot_nt(q_of(h), k_of(h)) * scale
        m = jnp.max(logits, axis=-1, keepdims=True)
        p = jnp.exp(logits - m)
        den = jnp.sum(p, axis=-1, keepdims=True)
        oh = _dot(p, v_of(h)) / den
        store(h, oh * _silu(z_of(h).astype(F32)))


def _mem_cols(h):
    return slice(h * MEM_DH, (h + 1) * MEM_DH)


def _mem_prompt_kernel(q_ref, z_ref, k_ref, v_ref, o_ref):
    def store(h, val):
        o_ref[:, _mem_cols(h)] = val.astype(o_ref.dtype)
    _mem_heads(lambda h: q_ref[:, _mem_cols(h)], lambda h: z_ref[:, _mem_cols(h)],
               lambda h: k_ref[0, :, _mem_cols(h)], lambda h: v_ref[0, :, _mem_cols(h)], store)


def _mem_prompt(proj, mk, mv, n_seq, seq_len, q_colblock, z_colblock, tq=512):
    steps = seq_len // tq
    return pl.pallas_call(
        _mem_prompt_kernel,
        out_shape=jax.ShapeDtypeStruct((n_seq * seq_len, MEM_W), BF16),
        grid=(n_seq, steps),
        in_specs=[pl.BlockSpec((tq, MEM_W), lambda b, n: (b * steps + n, q_colblock)),
                  pl.BlockSpec((tq, MEM_W), lambda b, n: (b * steps + n, z_colblock)),
                  pl.BlockSpec((1, N_MEM, MEM_W), lambda b, n: (b, 0, 0)),
                  pl.BlockSpec((1, N_MEM, MEM_W), lambda b, n: (b, 0, 0))],
        out_specs=pl.BlockSpec((tq, MEM_W), lambda b, n: (b * steps + n, 0)),
        compiler_params=_cparams(("parallel", "arbitrary")),
        name="mem_prompt",
    )(proj, proj, mk, mv)


def _mem_sample_kernel(q_ref, z_ref, k_hbm, v_hbm, o_ref, kbuf, vbuf, sem, *, n_seq_blk):
    i = pl.program_id(0)
    n_steps = pl.num_programs(0)
    slot = i % 2

    def copies(step, slot_):
        seqs = pl.ds(step * n_seq_blk, n_seq_blk)
        out = []
        for h in range(MEM_H):
            out.append(pltpu.make_async_copy(k_hbm.at[seqs, :, h, :], kbuf.at[slot_, h], sem.at[0, slot_, h]))
            out.append(pltpu.make_async_copy(v_hbm.at[seqs, :, h, :], vbuf.at[slot_, h], sem.at[1, slot_, h]))
        return out

    @pl.when(i == 0)
    def _():
        for cp in copies(0, 0):
            cp.start()

    @pl.when(i + 1 < n_steps)
    def _():
        for cp in copies(i + 1, 1 - slot):
            cp.start()

    for cp in copies(i, slot):
        cp.wait()

    for s in range(n_seq_blk):
        def store(h, val, s=s):
            o_ref[s, :, _mem_cols(h)] = val.astype(o_ref.dtype)
        _mem_heads(lambda h, s=s: q_ref[s, :, _mem_cols(h)], lambda h, s=s: z_ref[s, :, _mem_cols(h)],
                   lambda h, s=s: kbuf[slot, h, s], lambda h, s=s: vbuf[slot, h, s], store)


def _mem_sample(q, z, cache_k, cache_v, n_seq_blk=4):
    bd, rows, _ = q.shape
    blk = lambda i: (i, 0, 0)
    buf = pltpu.VMEM((2, MEM_H, n_seq_blk, N_MEM, MEM_DH), cache_k.dtype)
    return pl.pallas_call(
        functools.partial(_mem_sample_kernel, n_seq_blk=n_seq_blk),
        out_shape=jax.ShapeDtypeStruct(q.shape, BF16),
        grid=(bd // n_seq_blk,),
        in_specs=[pl.BlockSpec((n_seq_blk, rows, MEM_W), blk),
                  pl.BlockSpec((n_seq_blk, rows, MEM_W), blk),
                  pl.BlockSpec(memory_space=pl.ANY),
                  pl.BlockSpec(memory_space=pl.ANY)],
        out_specs=pl.BlockSpec((n_seq_blk, rows, MEM_W), blk),
        scratch_shapes=[buf, buf, pltpu.SemaphoreType.DMA((2, 2, MEM_H))],
        compiler_params=_cparams(("arbitrary",)),
        name="mem_sample",
    )(q, z, cache_k, cache_v)


def _merge_kernel(og_ref, os_ref, om_ref, gate_ref, x_ref, wb_ref, wo_ref, nf_ref, y_ref):
    merged = None
    for b, o_ref in enumerate((og_ref, os_ref, om_ref)):
        t = jnp.dot(o_ref[...], wb_ref[b], preferred_element_type=F32)
        t = t * gate_ref[:, b * D_MODEL:(b + 1) * D_MODEL].astype(F32)
        merged = t if merged is None else merged + t
    h = x_ref[...] + jnp.dot(merged.astype(BF16), wo_ref[...], preferred_element_type=F32)
    ms = jnp.mean(h * h, axis=-1, keepdims=True)
    y_ref[...] = h * lax.rsqrt(ms + NORM_EPS) * nf_ref[...]


def _merge(o_gdn, o_swa, o_mem, gates, x, w_branch, w_out, norm_f, tm=256):
    m = x.shape[0]
    tm = min(tm, m)
    row = lambda i: (i, 0)
    const2 = lambda i: (0, 0)
    return pl.pallas_call(
        _merge_kernel,
        out_shape=jax.ShapeDtypeStruct((m, D_MODEL), F32),
        grid=(m // tm,),
        in_specs=[pl.BlockSpec((tm, BR_W), row),
                  pl.BlockSpec((tm, BR_W), row),
                  pl.BlockSpec((tm, BR_W), row),
                  pl.BlockSpec((tm, N_BRANCH * D_MODEL), row),
                  pl.BlockSpec((tm, D_MODEL), row),
                  pl.BlockSpec((N_BRANCH, BR_W, D_MODEL), lambda i: (0, 0, 0),
                               pipeline_mode=pl.Buffered(1)),
                  pl.BlockSpec((D_MODEL, D_MODEL), const2, pipeline_mode=pl.Buffered(1)),
                  pl.BlockSpec((1, D_MODEL), const2)],
        out_specs=pl.BlockSpec((tm, D_MODEL), row),
        compiler_params=_cparams(("parallel",)),
        name="merge",
    )(o_gdn, o_swa, o_mem, gates, x, w_branch, w_out, norm_f)


_IN_SIZES = (GDN_QK, GDN_QK, GDN_W, GDN_W, GDN_H, GDN_H, SWA_QW, SWA_KVW, SWA_KVW, SWA_QW,
             MEM_W, MEM_W, N_BRANCH * D_MODEL)
_IN_NAMES = ("gq", "gk", "gv", "gz", "gb", "ga", "sq", "sk", "sv", "sz", "mq", "mz", "mg")
_IN_SPAN = {name: (int(off), int(off + size)) for name, off, size in
            zip(_IN_NAMES, np.cumsum((0,) + _IN_SIZES[:-1]), _IN_SIZES)}


def _cols(w, *names):
    return jnp.concatenate([w[:, _IN_SPAN[a][0]:_IN_SPAN[a][1]] for a in names], axis=1)


def kernel(x_prompt, x_sample, state_gdn, state_gdn_conv, cache_swa_k, cache_swa_v, cache_mem_k,
           cache_mem_v, mem_prompt, norm_in, w_in, gdn_conv_w, gdn_a_log, gdn_dt_bias, gdn_norm,
           swa_sinks, rel_bias, norm_mem, w_mem_kv, w_branch, w_out, norm_f):
    n_layers = norm_in.shape[0]
    assert n_layers == 1
    b, seq, _ = x_prompt.shape
    bd, ns, _ = x_sample.shape
    wb = cache_swa_k.shape[2]
    assert seq % WINDOW == 0 and seq % GDN_CHUNK == 0 and ns + CONV_W <= SUBLANE and wb == WINDOW
    lyr = 0

    w = w_in[lyr]
    w_main = _cols(w, "gq", "gk", "gv", "gz", "sq", "sz", "mq", "mz").astype(BF16)
    cb_gz, cb_sq, cb_sz, cb_mq, cb_mz = 3, 4, 5, 6, 7
    w_small = jnp.pad(_cols(w, "sk", "sv", "gb", "ga"), ((0, 0), (0, LANE - 2 * GDN_H))).astype(BF16)
    cb_sk, cb_sv, cb_ba = 0, 1, 2
    w_gate = _cols(w, "mg").astype(BF16)
    w_qkv = w_main[:, :GDN_CONV_CH]
    w_mkv = w_mem_kv[lyr].astype(BF16)
    w_br = w_branch[lyr].astype(BF16)
    w_o = w_out[lyr].astype(BF16)
    nw_in = norm_in[lyr].reshape(1, D_MODEL)
    nw_mem = norm_mem[lyr].reshape(1, D_MODEL)
    nw_f = norm_f.reshape(1, D_MODEL)
    conv_w = gdn_conv_w[lyr]
    a_row = jnp.pad(gdn_a_log[lyr].reshape(1, GDN_H), ((0, 0), (GDN_H, LANE - 2 * GDN_H)))
    dt_row = jnp.pad(gdn_dt_bias[lyr].reshape(1, GDN_H), ((0, 0), (GDN_H, LANE - 2 * GDN_H)))
    gnw = gdn_norm[lyr].reshape(1, GDN_DV)
    sinks = swa_sinks[lyr]
    bias_p = _bias_prompt(rel_bias)
    npad = SUBLANE
    bias_s = _bias_sample(rel_bias, ns, wb, wb + npad)
    sink_rows = jnp.broadcast_to(jnp.repeat(sinks.reshape(SWA_KV, SWA_G), ns, axis=1)[:, :, None],
                                 (SWA_KV, SWA_G * ns, LANE))

    t = b * seq
    xp = x_prompt.reshape(t, D_MODEL)
    xn_p = _rmsnorm(xp, nw_in)
    p_main = _proj(xn_p, w_main, BF16)
    p_small = _proj(xn_p, w_small, F32)
    g_p = _proj(xn_p, w_gate, BF16, act="sigmoid")
    xn_tail = xn_p.reshape(b, seq, D_MODEL)[:, seq - SUBLANE:, :].reshape(b * SUBLANE, D_MODEL)
    conv_p = _proj(xn_tail, w_qkv, F32).reshape(b, SUBLANE, GDN_CONV_CH)[:, SUBLANE - (CONV_W - 1):]
    kv_tail = p_small.reshape(b, seq, 3 * LANE)[:, seq - WINDOW:]
    swk_p = kv_tail[:, :, cb_sk * LANE:(cb_sk + 1) * LANE].reshape(b, WINDOW, SWA_KV, SWA_DH)
    swv_p = kv_tail[:, :, cb_sv * LANE:(cb_sv + 1) * LANE].reshape(b, WINDOW, SWA_KV, SWA_DH)

    mkv = _proj(_rmsnorm(mem_prompt.reshape(b * N_MEM, D_MODEL), nw_mem), w_mkv, F32)
    mk_p = mkv[:, :MEM_W].reshape(b, N_MEM, MEM_W)
    mv_p = mkv[:, MEM_W:].reshape(b, N_MEM, MEM_W)

    o_gdn_p, s_p = _gdn(p_main, p_main, p_small, jnp.zeros((b, HDR, GDN_CONV_CH), F32),
                        jnp.zeros((b, GDN_H, GDN_DK, GDN_DV), F32), conv_w, a_row, dt_row, gnw,
                        n_seq=b, chunk=GDN_CHUNK, n_chunks=4, per_chunk_state=False, group=4, unroll=True,
                        valid_lo=0, valid_hi=GDN_CHUNK, out_dtype=BF16,
                        z_colblock=cb_gz, ba_colblock=cb_ba)
    o_swa_p = _swa_prompt(p_main, p_small, bias_p, sinks, b, seq, cb_sq, cb_sz, cb_sk, cb_sv)
    o_mem_p = _mem_prompt(p_main, mk_p.astype(BF16), mv_p.astype(BF16), b, seq, cb_mq, cb_mz)
    y_p = _merge(o_gdn_p, o_swa_p, o_mem_p, g_p, xp, w_br, w_o, nw_f).reshape(b, seq, D_MODEL)

    ts = bd * ns
    xs = x_sample.reshape(ts, D_MODEL)
    xn_s = _rmsnorm(xs, nw_in)
    s_main = _proj(xn_s, w_main, F32)
    s_small = _proj(xn_s, w_small, F32)
    g_s = _proj(xn_s, w_gate, BF16, act="sigmoid")
    s_gdn = s_main[:, :GDN_CONV_CH + GDN_W]
    s_ba = s_small[:, cb_ba * LANE:(cb_ba + 1) * LANE]
    s_swa = jnp.concatenate([s_main[:, cb_sq * BR_W:(cb_sz + 1) * BR_W], s_small[:, :2 * LANE]], axis=1)
    s_mem = s_main[:, cb_mq * BR_W:(cb_mz + 1) * BR_W].astype(BF16)

    lo = CONV_W - 1
    hi = lo + ns
    pad_rows = ((0, 0), (lo, SUBLANE - hi), (0, 0))
    e_qkv = jnp.concatenate([state_gdn_conv[lyr], s_gdn[:, :GDN_CONV_CH].reshape(bd, ns, GDN_CONV_CH),
                             jnp.zeros((bd, SUBLANE - hi, GDN_CONV_CH), F32)], axis=1)
    e_z = jnp.pad(s_gdn[:, GDN_CONV_CH:].reshape(bd, ns, GDN_W), pad_rows)
    e_ba = jnp.pad(s_ba.reshape(bd, ns, LANE), pad_rows)
    seq_blk = 8
    o_gdn_s8, s_s = _gdn(e_qkv.reshape(bd * SUBLANE, GDN_CONV_CH), e_z.reshape(bd * SUBLANE, GDN_W),
                         e_ba.reshape(bd * SUBLANE, LANE), jnp.zeros((1, HDR, GDN_CONV_CH), F32),
                         state_gdn[lyr], conv_w, a_row, dt_row, gnw,
                         n_seq=bd, chunk=SUBLANE, n_chunks=seq_blk, per_chunk_state=True, group=8,
                         unroll=True,
                         valid_lo=lo, valid_hi=hi, out_dtype=BF16)
    o_gdn_s = o_gdn_s8.reshape(bd, SUBLANE, GDN_W)[:, lo:hi].reshape(ts, GDN_W)
    conv_s = e_qkv[:, hi - (CONV_W - 1):hi]

    def to_heads(a):
        return a.reshape(bd, ns, SWA_KV, SWA_G, SWA_DH).transpose(0, 2, 3, 1, 4).reshape(
            bd, SWA_KV, SWA_G * ns, SWA_DH)

    k_new = s_swa[:, 2 * SWA_QW:2 * SWA_QW + SWA_KVW].reshape(bd, ns, SWA_KVW)
    v_new = s_swa[:, 2 * SWA_QW + SWA_KVW:].reshape(bd, ns, SWA_KVW)
    tok_pad = ((0, 0), (0, npad - ns), (0, 0))
    ck = cache_swa_k[lyr].reshape(bd, wb, SWA_KVW)
    cv = cache_swa_v[lyr].reshape(bd, wb, SWA_KVW)
    o_swa_h = _swa_sample(to_heads(s_swa[:, :SWA_QW]).astype(BF16), to_heads(s_swa[:, SWA_QW:2 * SWA_QW]),
                          jnp.pad(k_new, tok_pad), jnp.pad(v_new, tok_pad), ck, cv, bias_s, sink_rows)
    o_swa_s = o_swa_h.reshape(bd, SWA_KV, SWA_G, ns, SWA_DH).transpose(0, 3, 1, 2, 4).reshape(ts, SWA_QW)
    swk_s = jnp.concatenate([ck, k_new], axis=1)[:, ns:].reshape(bd, wb, SWA_KV, SWA_DH)
    swv_s = jnp.concatenate([cv, v_new], axis=1)[:, ns:].reshape(bd, wb, SWA_KV, SWA_DH)

    mq = jnp.pad(s_mem[:, :MEM_W].reshape(bd, ns, MEM_W), tok_pad)
    mz = jnp.pad(s_mem[:, MEM_W:].reshape(bd, ns, MEM_W), tok_pad)
    o_mem_s = _mem_sample(mq, mz, cache_mem_k[lyr], cache_mem_v[lyr])[:, :ns].reshape(ts, MEM_W)
    y_s = _merge(o_gdn_s, o_swa_s, o_mem_s, g_s, xs, w_br, w_o, nw_f).reshape(bd, ns, D_MODEL)

    return (y_p, y_s,
            s_p[None], conv_p[None], swk_p[None], swv_p[None],
            mk_p.reshape(b, N_MEM, MEM_H, MEM_DH)[None], mv_p.reshape(b, N_MEM, MEM_H, MEM_DH)[None],
            s_s[None], conv_s[None], swk_s[None], swv_s[None])
```

```python
import functools
import math

import numpy as np
import jax
import jax.numpy as jnp
from jax import lax
from jax.experimental import pallas as pl
from jax.experimental.pallas import tpu as pltpu

F32 = jnp.float32
BF16 = jnp.bfloat16

D_MODEL = 2048
N_BRANCH = 3
BR_W = 1024
GDN_H = 8
GDN_DK = 128
GDN_DV = 128
GDN_QK = GDN_H * GDN_DK
GDN_W = GDN_H * GDN_DV
GDN_CONV_CH = 2 * GDN_QK + GDN_W
CONV_W = 4
GDN_CHUNK = 64
SWA_H = 16
SWA_KV = 2
SWA_G = SWA_H // SWA_KV
SWA_DH = 64
SWA_QW = SWA_H * SWA_DH
SWA_KVW = SWA_KV * SWA_DH
WINDOW = 128
N_BUCKETS = 32
MAX_DISTANCE = 128
N_MEM = 256
MEM_H = 4
MEM_DH = 256
MEM_W = MEM_H * MEM_DH
NORM_EPS = 1e-6

LANE = 128
SUBLANE = 8
VMEM_LIMIT = 52 * 1024 * 1024


def _cparams(sem):
    return pltpu.CompilerParams(dimension_semantics=sem, vmem_limit_bytes=VMEM_LIMIT)


def _sigmoid(x):
    return 1.0 / (1.0 + jnp.exp(-x))


def _silu(x):
    return x * _sigmoid(x)


def _softplus(x):
    return jnp.maximum(x, 0.0) + jnp.log(1.0 + jnp.exp(-jnp.abs(x)))


def _dot(a, b):
    return jnp.dot(a.astype(BF16), b.astype(BF16), preferred_element_type=F32)


def _dot_nt(a, b):
    return lax.dot_general(a.astype(BF16), b.astype(BF16), (((1,), (1,)), ((), ())),
                           preferred_element_type=F32)


def _dot_tn(a, b):
    return lax.dot_general(a.astype(BF16), b.astype(BF16), (((0,), (0,)), ((), ())),
                           preferred_element_type=F32)


def _dot_f32(a, b):
    return jnp.dot(a, b, preferred_element_type=F32, precision=lax.Precision.HIGHEST)


def _rmsnorm_kernel(x_ref, nw_ref, o_ref):
    x = x_ref[...].astype(F32)
    ms = jnp.mean(x * x, axis=-1, keepdims=True)
    o_ref[...] = (x * lax.rsqrt(ms + NORM_EPS) * nw_ref[...]).astype(o_ref.dtype)


def _rmsnorm(x, norm_w, tm_pref=512):
    m, d = x.shape
    tm = min(m, tm_pref)
    assert m % tm == 0
    return pl.pallas_call(
        _rmsnorm_kernel,
        out_shape=jax.ShapeDtypeStruct((m, d), BF16),
        grid=(m // tm,),
        in_specs=[pl.BlockSpec((tm, d), lambda i: (i, 0)),
                  pl.BlockSpec((1, d), lambda i: (0, 0))],
        out_specs=pl.BlockSpec((tm, d), lambda i: (i, 0)),
        compiler_params=_cparams(("parallel",)),
        name="rmsnorm",
    )(x, norm_w)


def _proj_kernel(x_ref, w_ref, o_ref, *, act):
    y = jnp.dot(x_ref[...], w_ref[...], preferred_element_type=F32)
    if act == "sigmoid":
        y = _sigmoid(y)
    o_ref[...] = y.astype(o_ref.dtype)


def _pick_tile(n, pref):
    t = min(n, pref)
    while n % t:
        t -= LANE
    return t


def _proj(xn, w, out_dtype, act=None, tm_pref=1024, tn_pref=2048):
    m, d = xn.shape
    n = w.shape[1]
    tm = min(m, tm_pref)
    assert m % tm == 0
    tn = _pick_tile(n, tn_pref if out_dtype == BF16 else tn_pref // 2)
    return pl.pallas_call(
        functools.partial(_proj_kernel, act=act),
        out_shape=jax.ShapeDtypeStruct((m, n), out_dtype),
        grid=(m // tm, n // tn),
        in_specs=[pl.BlockSpec((tm, d), lambda i, j: (i, 0)),
                  pl.BlockSpec((d, tn), lambda i, j: (0, j))],
        out_specs=pl.BlockSpec((tm, tn), lambda i, j: (i, j)),
        compiler_params=_cparams(("parallel", "arbitrary")),
        name="proj",
    )(xn, w)


HDR = SUBLANE


def _tri_inv_many(a_list, c):
    row = lax.broadcasted_iota(jnp.int32, (c, c), 0)
    col = lax.broadcasted_iota(jnp.int32, (c, c), 1)
    eye = jnp.where(row == col, 1.0, 0.0).astype(F32)
    xs = [eye - a for a in a_list]
    bs = [_dot(a, a) for a in a_list]
    n = 2
    while n < c:
        xs = [x + _dot(x, b) for x, b in zip(xs, bs)]
        n *= 2
        if n < c:
            bs = [_dot(b, b) for b in bs]
    return xs


def _gdn_kernel(qkv_ref, z_ref, ba_ref, hdr_ref, s0_ref, cw_ref, arow_ref, dtrow_ref, nw_ref,
                o_ref, s_ref, buf_ref, cv_ref, gb_ref, oacc_ref, sol_ref, att_ref, gl_ref,
                *, chunk, n_chunks, group, unroll, per_chunk_state, valid_lo, valid_hi):
    c = chunk
    tb = c * n_chunks

    if per_chunk_state:
        buf_ref[0:2 * HDR, :] = jnp.zeros((2 * HDR, GDN_CONV_CH), F32)
    else:
        @pl.when(pl.program_id(1) == 0)
        def _():
            buf_ref[0:HDR, :] = hdr_ref[0]
            buf_ref[HDR:2 * HDR, :] = jnp.zeros((HDR, GDN_CONV_CH), F32)
            s_ref[...] = s0_ref[...]

    def tap_from_history(j, n_rows):
        off = HDR - (CONV_W - 1) + j
        return buf_ref[off:off + n_rows, :] * cw_ref[j:j + 1, :]

    if qkv_ref.dtype == BF16:
        xb = qkv_ref[...]
        r = lax.broadcasted_iota(jnp.int32, (tb, tb), 0)
        cc = lax.broadcasted_iota(jnp.int32, (tb, tb), 1)
        acc = xb.astype(F32) * cw_ref[CONV_W - 1:CONV_W, :]
        for j in range(CONV_W - 1):
            shift = jnp.where(r - cc == CONV_W - 1 - j, 1.0, 0.0).astype(BF16)
            acc = acc + jnp.dot(shift, xb, preferred_element_type=F32) * cw_ref[j:j + 1, :]
        cv_ref[...] = _silu(acc)
        top = acc[0:HDR]
        for j in range(CONV_W - 1):
            top = top + tap_from_history(j, HDR)
        cv_ref[0:HDR, :] = _silu(top)
        buf_ref[0:HDR, :] = qkv_ref[tb - 2 * HDR:tb, :].astype(F32)[HDR:]
    else:
        buf_ref[HDR:HDR + tb, :] = qkv_ref[...].astype(F32)
        acc = None
        for j in range(CONV_W):
            term = tap_from_history(j, tb)
            acc = term if acc is None else acc + term
        cv_ref[...] = _silu(acc)
        if not per_chunk_state:
            buf_ref[0:HDR, :] = buf_ref[tb:tb + HDR, :]

    ba = ba_ref[...].astype(F32)
    beta_all = _sigmoid(ba)
    g_all = -jnp.exp(arow_ref[...]) * _softplus(ba + dtrow_ref[...])
    if per_chunk_state:
        r = lax.broadcasted_iota(jnp.int32, (tb, LANE), 0) & (c - 1)
        valid = (r >= valid_lo) & (r < valid_hi)
        beta_all = jnp.where(valid, beta_all, 0.0)
        g_all = jnp.where(valid, g_all, 0.0)
    gb_ref[0] = beta_all
    gb_ref[1] = g_all

    row = lax.broadcasted_iota(jnp.int32, (c, c), 0)
    col = lax.broadcasted_iota(jnp.int32, (c, c), 1)
    causal = row >= col
    strict = row > col
    tril = jnp.where(causal, 1.0, 0.0).astype(F32)
    scale_q = GDN_DK ** -0.5

    heads = range(GDN_H)
    qcol = lambda h: slice(h * GDN_DK, (h + 1) * GDN_DK)
    kcol = lambda h: slice(GDN_QK + h * GDN_DK, GDN_QK + (h + 1) * GDN_DK)
    vcol = lambda h: slice(2 * GDN_QK + h * GDN_DV, 2 * GDN_QK + (h + 1) * GDN_DV)
    ucol = lambda h: slice(h * (GDN_DV + GDN_DK), h * (GDN_DV + GDN_DK) + GDN_DV)
    wcol = lambda h: slice(h * (GDN_DV + GDN_DK) + GDN_DV, (h + 1) * (GDN_DV + GDN_DK))
    if per_chunk_state:
        rv = lax.broadcasted_iota(jnp.int32, (c, 1), 0)
        rvalid = (rv >= valid_lo) & (rv < valid_hi)

    def chunk_rows(ci):
        if isinstance(ci, int):
            return slice(ci * c, (ci + 1) * c)
        return pl.ds(pl.multiple_of(ci * c, c), c)

    def prep_body(gi, carry):
        probs = []
        for j in range(group):
            rows = chunk_rows(gi * group + j)
            beta_c = gb_ref[0, rows, :]
            gc_all = _dot_f32(tril, gb_ref[1, rows, :])
            gc_t = gc_all.T
            gl_ref[gi * group + j] = gc_all[c - 1:c, :]
            for h in heads:
                q = cv_ref[rows, qcol(h)]
                k = cv_ref[rows, kcol(h)]
                v = cv_ref[rows, vcol(h)]
                q = q * lax.rsqrt(jnp.sum(q * q, axis=-1, keepdims=True) + NORM_EPS) * scale_q
                k = k * lax.rsqrt(jnp.sum(k * k, axis=-1, keepdims=True) + NORM_EPS)
                if per_chunk_state:
                    q = jnp.where(rvalid, q, 0.0)
                    k = jnp.where(rvalid, k, 0.0)
                    v = jnp.where(rvalid, v, 0.0)
                beta = beta_c[:, h:h + 1]
                gc_col = gc_all[:, GDN_H + h:GDN_H + h + 1]
                gc_row = gc_t[GDN_H + h:GDN_H + h + 1, :]
                diff = jnp.where(causal, gc_col - gc_row, 0.0)
                decay = jnp.where(causal, jnp.exp(diff), 0.0)
                e_gc = jnp.exp(gc_col)
                kb = k * beta
                rhs = jnp.concatenate([v * beta, kb * e_gc], axis=1)
                cv_ref[rows, qcol(h)] = q * e_gc
                cv_ref[rows, kcol(h)] = k * jnp.exp(gc_col[c - 1:c, :] - gc_col)
                probs.append((rows, h, q, k, kb, rhs, decay))
        kq = [_dot_nt(jnp.concatenate([kb, q], axis=0), k) for (_, _, q, k, kb, _, _) in probs]
        a_list = [jnp.where(strict, kq_i[:c] * p[6], 0.0) for kq_i, p in zip(kq, probs)]
        t_inv = _tri_inv_many(a_list, c)
        for t_i, kq_i, (rows, h, _, _, _, rhs, decay) in zip(t_inv, kq, probs):
            sol_ref[rows, h * (GDN_DV + GDN_DK):(h + 1) * (GDN_DV + GDN_DK)] = _dot(t_i, rhs)
            att_ref[h, rows, :] = kq_i[c:] * decay
        return carry

    def scan_body(ci, carry):
        rows = chunk_rows(ci)
        si = ci if per_chunk_state else 0
        g_tot = jnp.exp(gl_ref[ci])
        s_old = [s_ref[si, h] for h in heads]
        wq_s = [_dot(jnp.concatenate([sol_ref[rows, wcol(h)], cv_ref[rows, qcol(h)]], axis=0), s_old[h])
                for h in heads]
        v_new = [sol_ref[rows, ucol(h)] - wq_s[h][:c] for h in heads]
        o_att = [_dot(att_ref[h, rows, :], v_new[h]) for h in heads]
        for h in heads:
            s_ref[si, h] = (s_old[h] * g_tot[:, GDN_H + h:GDN_H + h + 1]
                            + _dot_tn(cv_ref[rows, kcol(h)], v_new[h]))
        for h in heads:
            o = wq_s[h][c:] + o_att[h]
            o = o * lax.rsqrt(jnp.mean(o * o, axis=-1, keepdims=True) + NORM_EPS) * nw_ref[...]
            oacc_ref[rows, h * GDN_DV:(h + 1) * GDN_DV] = o
        return carry

    if per_chunk_state:
        s_ref[...] = s0_ref[...]
    if unroll:
        for gi in range(n_chunks // group):
            prep_body(gi, 0)
        for ci in range(n_chunks):
            scan_body(ci, 0)
    else:
        lax.fori_loop(0, n_chunks // group, prep_body, 0)
        lax.fori_loop(0, n_chunks, scan_body, 0)
    o_ref[...] = (oacc_ref[...] * _silu(z_ref[...].astype(F32))).astype(o_ref.dtype)


def _gdn(qkv, z, ba, hdr, s0, conv_w, a_row, dt_row, norm_w, *, n_seq, chunk, n_chunks,
         per_chunk_state, valid_lo, valid_hi, out_dtype, z_colblock=0, ba_colblock=0, group=2,
         unroll=False):
    rows = qkv.shape[0]
    tb = chunk * n_chunks
    if per_chunk_state:
        grid = (rows // tb,)
        sem = ("arbitrary",)
        rmap = lambda i: (i, 0)
        zmap = lambda i: (i, z_colblock)
        bmap = lambda i: (i, ba_colblock)
        hmap = lambda i: (0, 0, 0)
        smap = lambda i: (i, 0, 0, 0)
        cmap = lambda i: (0, 0)
        ns = n_chunks
    else:
        steps = rows // n_seq // tb
        grid = (n_seq, steps)
        sem = ("parallel", "arbitrary")
        rmap = lambda b, n: (b * steps + n, 0)
        zmap = lambda b, n: (b * steps + n, z_colblock)
        bmap = lambda b, n: (b * steps + n, ba_colblock)
        hmap = lambda b, n: (b, 0, 0)
        smap = lambda b, n: (b, 0, 0, 0)
        cmap = lambda b, n: (0, 0)
        ns = 1
    assert n_chunks % group == 0
    kern = functools.partial(_gdn_kernel, chunk=chunk, n_chunks=n_chunks, group=group, unroll=unroll,
                             per_chunk_state=per_chunk_state, valid_lo=valid_lo, valid_hi=valid_hi)
    return pl.pallas_call(
        kern,
        out_shape=(jax.ShapeDtypeStruct((rows, GDN_W), out_dtype),
                   jax.ShapeDtypeStruct(s0.shape, F32)),
        grid=grid,
        in_specs=[pl.BlockSpec((tb, GDN_CONV_CH), rmap),
                  pl.BlockSpec((tb, GDN_W), zmap),
                  pl.BlockSpec((tb, LANE), bmap),
                  pl.BlockSpec((1, HDR, GDN_CONV_CH), hmap),
                  pl.BlockSpec((ns, GDN_H, GDN_DK, GDN_DV), smap),
                  pl.BlockSpec((CONV_W, GDN_CONV_CH), cmap),
                  pl.BlockSpec((1, LANE), cmap),
                  pl.BlockSpec((1, LANE), cmap),
                  pl.BlockSpec((1, GDN_DV), cmap)],
        out_specs=(pl.BlockSpec((tb, GDN_W), rmap),
                   pl.BlockSpec((ns, GDN_H, GDN_DK, GDN_DV), smap)),
        scratch_shapes=[pltpu.VMEM((HDR + tb, GDN_CONV_CH), F32),
                        pltpu.VMEM((tb, GDN_CONV_CH), F32),
                        pltpu.VMEM((2, tb, LANE), F32),
                        pltpu.VMEM((tb, GDN_W), F32),
                        pltpu.VMEM((tb, GDN_H * (GDN_DV + GDN_DK)), F32),
                        pltpu.VMEM((GDN_H, tb, chunk), F32),
                        pltpu.VMEM((n_chunks, 1, LANE), F32)],
        compiler_params=_cparams(sem),
        name="gdn",
    )(qkv, z, ba, hdr, s0, conv_w, a_row, dt_row, norm_w)


def _t5_bucket_np(dist):
    n = np.maximum(dist, 0)
    max_exact = N_BUCKETS // 2
    nf = np.maximum(n, 1).astype(np.float32)
    large = max_exact + (np.log(nf / np.float32(max_exact)) / np.float32(math.log(MAX_DISTANCE / max_exact))
                         * np.float32(N_BUCKETS - max_exact)).astype(np.int32)
    large = np.minimum(large, N_BUCKETS - 1)
    return np.where(n < max_exact, n, large).astype(np.int32)


def _bias_prompt_kernel(code_ref, tab_ref, o_ref):
    h = pl.program_id(1)
    code = code_ref[0]
    acc = jnp.full(code.shape, -jnp.inf, F32)
    for b in range(N_BUCKETS):
        acc = jnp.where(code == b, tab_ref[b, h], acc)
    o_ref[0, 0] = acc


def _bias_prompt(table):
    qi = np.arange(WINDOW)[:, None]
    sj = np.arange(WINDOW)[None, :]
    own = sj <= qi
    bucket = _t5_bucket_np(np.where(own, qi - sj, qi + WINDOW - sj))
    code_first = np.where(own, bucket, -1)
    code = jnp.asarray(np.stack([code_first, bucket]).astype(np.int32))
    return pl.pallas_call(
        _bias_prompt_kernel,
        out_shape=jax.ShapeDtypeStruct((2, SWA_H, WINDOW, WINDOW), F32),
        grid=(2, SWA_H),
        in_specs=[pl.BlockSpec((1, WINDOW, WINDOW), lambda v, h: (v, 0, 0)),
                  pl.BlockSpec(memory_space=pltpu.SMEM)],
        out_specs=pl.BlockSpec((1, 1, WINDOW, WINDOW), lambda v, h: (v, h, 0, 0)),
        compiler_params=_cparams(("arbitrary", "arbitrary")),
        name="swa_bias_prompt",
    )(code, table)


def _bias_sample_kernel(code_ref, tab_ref, o_ref):
    kv = pl.program_id(0)
    code = code_ref[...]
    acc = jnp.full(code.shape, -jnp.inf, F32)
    for g in range(SWA_G):
        for b in range(N_BUCKETS):
            acc = jnp.where(code == b + N_BUCKETS * g, tab_ref[b, kv * SWA_G + g], acc)
    o_ref[0] = acc


def _bias_sample(table, n_tok, n_cache, n_keys_pad):
    dist = (n_cache + np.arange(n_tok))[:, None] - np.arange(n_keys_pad)[None, :]
    valid = (dist >= 0) & (dist < WINDOW) & (np.arange(n_keys_pad)[None, :] < n_cache + n_tok)
    bucket = _t5_bucket_np(dist)
    code_t = np.where(valid, bucket, -1)
    g = np.arange(SWA_G)[:, None, None]
    code = np.where(code_t[None] >= 0, code_t[None] + N_BUCKETS * g, -1)
    code = jnp.asarray(code.reshape(SWA_G * n_tok, n_keys_pad).astype(np.int32))
    return pl.pallas_call(
        _bias_sample_kernel,
        out_shape=jax.ShapeDtypeStruct((SWA_KV, SWA_G * n_tok, n_keys_pad), F32),
        grid=(SWA_KV,),
        in_specs=[pl.BlockSpec((SWA_G * n_tok, n_keys_pad), lambda k: (0, 0)),
                  pl.BlockSpec(memory_space=pltpu.SMEM)],
        out_specs=pl.BlockSpec((1, SWA_G * n_tok, n_keys_pad), lambda k: (k, 0, 0)),
        compiler_params=_cparams(("arbitrary",)),
        name="swa_bias_sample",
    )(code, table)


def _sink_softmax_pv(logits, sink, v):
    m = jnp.maximum(jnp.max(logits, axis=-1, keepdims=True), sink)
    p = jnp.exp(logits - m)
    den = jnp.sum(p, axis=-1, keepdims=True) + jnp.exp(sink - m)
    return _dot(p, v) / den


def _swa_prompt_kernel(q_ref, z_ref, kc_ref, kp_ref, vc_ref, vp_ref, bias_ref, sink_ref, o_ref):
    kk = jnp.concatenate([kp_ref[...], kc_ref[...]], axis=0).astype(BF16)
    vv = jnp.concatenate([vp_ref[...], vc_ref[...]], axis=0).astype(BF16)
    scale = SWA_DH ** -0.5
    row = lax.broadcasted_iota(jnp.int32, (WINDOW, WINDOW), 0)
    col = lax.broadcasted_iota(jnp.int32, (WINDOW, WINDOW), 1)
    from_prev = col > row
    for kv in range(SWA_KV):
        ks = slice(kv * SWA_DH, (kv + 1) * SWA_DH)
        heads = range(kv * SWA_G, (kv + 1) * SWA_G)
        cs = lambda h: slice(h * SWA_DH, (h + 1) * SWA_DH)
        lg = [_dot_nt(q_ref[:, cs(h)] * scale, kk[:, ks]) for h in heads]
        ps, dens = [], []
        for h, l in zip(heads, lg):
            l = jnp.where(from_prev, l[:, :WINDOW], l[:, WINDOW:]) + bias_ref[0, h]
            sink = sink_ref[h]
            m = jnp.maximum(jnp.max(l, axis=-1, keepdims=True), sink)
            p = jnp.exp(l - m)
            dens.append(jnp.sum(p, axis=-1, keepdims=True) + jnp.exp(sink - m))
            ps.append(jnp.concatenate([jnp.where(from_prev, p, 0.0), jnp.where(from_prev, 0.0, p)],
                                      axis=1))
        outs = [_dot(p, vv[:, ks]) / den for p, den in zip(ps, dens)]
        for j in range(0, SWA_G, 2):
            two = slice(heads[j] * SWA_DH, (heads[j] + 2) * SWA_DH)
            o2 = jnp.concatenate([outs[j], outs[j + 1]], axis=1)
            o_ref[:, two] = (o2 * _silu(z_ref[:, two].astype(F32))).astype(o_ref.dtype)


def _swa_prompt(proj, kv, bias, sinks, n_seq, seq_len, q_colblock, z_colblock, k_colblock, v_colblock):
    nb = seq_len // WINDOW
    cur = lambda b, n: b * nb + n
    prev = lambda b, n: b * nb + jnp.maximum(n - 1, 0)
    return pl.pallas_call(
        _swa_prompt_kernel,
        out_shape=jax.ShapeDtypeStruct((n_seq * seq_len, SWA_QW), BF16),
        grid=(n_seq, nb),
        in_specs=[pl.BlockSpec((WINDOW, SWA_QW), lambda b, n: (cur(b, n), q_colblock)),
                  pl.BlockSpec((WINDOW, SWA_QW), lambda b, n: (cur(b, n), z_colblock)),
                  pl.BlockSpec((WINDOW, SWA_KVW), lambda b, n: (cur(b, n), k_colblock)),
                  pl.BlockSpec((WINDOW, SWA_KVW), lambda b, n: (prev(b, n), k_colblock)),
                  pl.BlockSpec((WINDOW, SWA_KVW), lambda b, n: (cur(b, n), v_colblock)),
                  pl.BlockSpec((WINDOW, SWA_KVW), lambda b, n: (prev(b, n), v_colblock)),
                  pl.BlockSpec((1, SWA_H, WINDOW, WINDOW), lambda b, n: (jnp.minimum(n, 1), 0, 0, 0)),
                  pl.BlockSpec(memory_space=pltpu.SMEM)],
        out_specs=pl.BlockSpec((WINDOW, SWA_QW), lambda b, n: (cur(b, n), 0)),
        compiler_params=_cparams(("parallel", "arbitrary")),
        name="swa_prompt",
    )(proj, proj, kv, kv, kv, kv, bias, sinks)


def _swa_sample_kernel(q_ref, z_ref, kn_ref, vn_ref, ck_ref, cv_ref, bias_ref, sink_ref, o_ref, *, n_seq_blk):
    scale = SWA_DH ** -0.5
    probs = [(s, kv) for s in range(n_seq_blk) for kv in range(SWA_KV)]
    ks = lambda kv: slice(kv * SWA_DH, (kv + 1) * SWA_DH)
    kk = [jnp.concatenate([ck_ref[s], kn_ref[s]], axis=0).astype(BF16) for s in range(n_seq_blk)]
    vv = [jnp.concatenate([cv_ref[s], vn_ref[s]], axis=0).astype(BF16) for s in range(n_seq_blk)]
    logits = [_dot_nt(q_ref[s, kv], kk[s][:, ks(kv)]) * scale + bias_ref[kv] for s, kv in probs]
    outs = [_sink_softmax_pv(l, sink_ref[kv][:, 0:1], vv[s][:, ks(kv)]) for l, (s, kv) in zip(logits, probs)]
    for oh, (s, kv) in zip(outs, probs):
        o_ref[s, kv] = (oh * _silu(z_ref[s, kv].astype(F32))).astype(o_ref.dtype)


def _swa_sample(q, z, k_new, v_new, cache_k, cache_v, bias, sink_rows, n_seq_blk=8):
    bd, _, rows, _ = q.shape
    wb = cache_k.shape[1]
    npad = k_new.shape[1]
    blk4 = lambda i: (i, 0, 0, 0)
    blk3 = lambda i: (i, 0, 0)
    return pl.pallas_call(
        functools.partial(_swa_sample_kernel, n_seq_blk=n_seq_blk),
        out_shape=jax.ShapeDtypeStruct(q.shape, BF16),
        grid=(bd // n_seq_blk,),
        in_specs=[pl.BlockSpec((n_seq_blk, SWA_KV, rows, SWA_DH), blk4),
                  pl.BlockSpec((n_seq_blk, SWA_KV, rows, SWA_DH), blk4),
                  pl.BlockSpec((n_seq_blk, npad, SWA_KVW), blk3),
                  pl.BlockSpec((n_seq_blk, npad, SWA_KVW), blk3),
                  pl.BlockSpec((n_seq_blk, wb, SWA_KVW), blk3),
                  pl.BlockSpec((n_seq_blk, wb, SWA_KVW), blk3),
                  pl.BlockSpec((SWA_KV, rows, wb + npad), lambda i: (0, 0, 0)),
                  pl.BlockSpec((SWA_KV, rows, LANE), lambda i: (0, 0, 0))],
        out_specs=pl.BlockSpec((n_seq_blk, SWA_KV, rows, SWA_DH), blk4),
        compiler_params=_cparams(("arbitrary",)),
        name="swa_sample",
    )(q, z, k_new, v_new, cache_k, cache_v, bias, sink_rows)


def _mem_attend(probs, q_of, z_of, k_of, v_of, store):
    scale = MEM_DH ** -0.5
    logits = [_dot_nt(q_of(p) * scale, k_of(p)) for p in probs]
    ps, dens = [], []
    for l in logits:
        m = jnp.max(l, axis=-1, keepdims=True)
        e = jnp.exp(l - m)
        dens.append(jnp.sum(e, axis=-1, keepdims=True))
        ps.append(e)
    outs = [_dot(e, v_of(p)) / den for e, den, p in zip(ps, dens, probs)]
    for p, oh in zip(probs, outs):
        store(p, oh * _silu(z_of(p).astype(F32)))


def _mem_cols(h):
    return slice(h * MEM_DH, (h + 1) * MEM_DH)


def _mem_prompt_kernel(q_ref, z_ref, k_ref, v_ref, o_ref):
    def store(h, val):
        o_ref[:, _mem_cols(h)] = val.astype(o_ref.dtype)
    _mem_attend(range(MEM_H), lambda h: q_ref[:, _mem_cols(h)], lambda h: z_ref[:, _mem_cols(h)],
                lambda h: k_ref[0, :, _mem_cols(h)], lambda h: v_ref[0, :, _mem_cols(h)], store)


def _mem_prompt(proj, mk, mv, n_seq, seq_len, q_colblock, z_colblock, tq=512):
    steps = seq_len // tq
    return pl.pallas_call(
        _mem_prompt_kernel,
        out_shape=jax.ShapeDtypeStruct((n_seq * seq_len, MEM_W), BF16),
        grid=(n_seq, steps),
        in_specs=[pl.BlockSpec((tq, MEM_W), lambda b, n: (b * steps + n, q_colblock)),
                  pl.BlockSpec((tq, MEM_W), lambda b, n: (b * steps + n, z_colblock)),
                  pl.BlockSpec((1, N_MEM, MEM_W), lambda b, n: (b, 0, 0)),
                  pl.BlockSpec((1, N_MEM, MEM_W), lambda b, n: (b, 0, 0))],
        out_specs=pl.BlockSpec((tq, MEM_W), lambda b, n: (b * steps + n, 0)),
        compiler_params=_cparams(("parallel", "arbitrary")),
        name="mem_prompt",
    )(proj, proj, mk, mv)


def _mem_sample_kernel(q_ref, z_ref, k_hbm, v_hbm, o_ref, kbuf, vbuf, sem, *, n_seq_blk):
    i = pl.program_id(0)
    n_steps = pl.num_programs(0)
    slot = i % 2

    def copies(step, slot_):
        seqs = pl.ds(step * n_seq_blk, n_seq_blk)
        out = []
        for h in range(MEM_H):
            out.append(pltpu.make_async_copy(k_hbm.at[seqs, :, h, :], kbuf.at[slot_, h], sem.at[0, slot_, h]))
            out.append(pltpu.make_async_copy(v_hbm.at[seqs, :, h, :], vbuf.at[slot_, h], sem.at[1, slot_, h]))
        return out

    @pl.when(i == 0)
    def _():
        for cp in copies(0, 0):
            cp.start()

    @pl.when(i + 1 < n_steps)
    def _():
        for cp in copies(i + 1, 1 - slot):
            cp.start()

    for cp in copies(i, slot):
        cp.wait()

    def store(p, val):
        o_ref[p[0], :, _mem_cols(p[1])] = val.astype(o_ref.dtype)
    probs = [(s, h) for s in range(n_seq_blk) for h in range(MEM_H)]
    _mem_attend(probs, lambda p: q_ref[p[0], :, _mem_cols(p[1])], lambda p: z_ref[p[0], :, _mem_cols(p[1])],
                lambda p: kbuf[slot, p[1], p[0]], lambda p: vbuf[slot, p[1], p[0]], store)


def _mem_sample(q, z, cache_k, cache_v, n_seq_blk=4):
    bd, rows, _ = q.shape
    blk = lambda i: (i, 0, 0)
    buf = pltpu.VMEM((2, MEM_H, n_seq_blk, N_MEM, MEM_DH), cache_k.dtype)
    return pl.pallas_call(
        functools.partial(_mem_sample_kernel, n_seq_blk=n_seq_blk),
        out_shape=jax.ShapeDtypeStruct(q.shape, BF16),
        grid=(bd // n_seq_blk,),
        in_specs=[pl.BlockSpec((n_seq_blk, rows, MEM_W), blk),
                  pl.BlockSpec((n_seq_blk, rows, MEM_W), blk),
                  pl.BlockSpec(memory_space=pl.ANY),
                  pl.BlockSpec(memory_space=pl.ANY)],
        out_specs=pl.BlockSpec((n_seq_blk, rows, MEM_W), blk),
        scratch_shapes=[buf, buf, pltpu.SemaphoreType.DMA((2, 2, MEM_H))],
        compiler_params=_cparams(("arbitrary",)),
        name="mem_sample",
    )(q, z, cache_k, cache_v)


def _merge_kernel(og_ref, os_ref, om_ref, gate_ref, x_ref, wb_ref, wo_ref, nf_ref, y_ref):
    merged = None
    for b, o_ref in enumerate((og_ref, os_ref, om_ref)):
        t = jnp.dot(o_ref[...], wb_ref[b], preferred_element_type=F32)
        t = t * gate_ref[:, b * D_MODEL:(b + 1) * D_MODEL].astype(F32)
        merged = t if merged is None else merged + t
    h = x_ref[...] + jnp.dot(merged.astype(BF16), wo_ref[...], preferred_element_type=F32)
    ms = jnp.mean(h * h, axis=-1, keepdims=True)
    y_ref[...] = h * lax.rsqrt(ms + NORM_EPS) * nf_ref[...]


def _merge(o_gdn, o_swa, o_mem, gates, x, w_branch, w_out, norm_f, tm=256):
    m = x.shape[0]
    tm = min(tm, m)
    row = lambda i: (i, 0)
    const2 = lambda i: (0, 0)
    return pl.pallas_call(
        _merge_kernel,
        out_shape=jax.ShapeDtypeStruct((m, D_MODEL), F32),
        grid=(m // tm,),
        in_specs=[pl.BlockSpec((tm, BR_W), row),
                  pl.BlockSpec((tm, BR_W), row),
                  pl.BlockSpec((tm, BR_W), row),
                  pl.BlockSpec((tm, N_BRANCH * D_MODEL), row),
                  pl.BlockSpec((tm, D_MODEL), row),
                  pl.BlockSpec((N_BRANCH, BR_W, D_MODEL), lambda i: (0, 0, 0),
                               pipeline_mode=pl.Buffered(1)),
                  pl.BlockSpec((D_MODEL, D_MODEL), const2, pipeline_mode=pl.Buffered(1)),
                  pl.BlockSpec((1, D_MODEL), const2)],
        out_specs=pl.BlockSpec((tm, D_MODEL), row),
        compiler_params=_cparams(("parallel",)),
        name="merge",
    )(o_gdn, o_swa, o_mem, gates, x, w_branch, w_out, norm_f)


_IN_SIZES = (GDN_QK, GDN_QK, GDN_W, GDN_W, GDN_H, GDN_H, SWA_QW, SWA_KVW, SWA_KVW, SWA_QW,
             MEM_W, MEM_W, N_BRANCH * D_MODEL)
_IN_NAMES = ("gq", "gk", "gv", "gz", "gb", "ga", "sq", "sk", "sv", "sz", "mq", "mz", "mg")
_IN_SPAN = {name: (int(off), int(off + size)) for name, off, size in
            zip(_IN_NAMES, np.cumsum((0,) + _IN_SIZES[:-1]), _IN_SIZES)}


def _cols(w, *names):
    return jnp.concatenate([w[:, _IN_SPAN[a][0]:_IN_SPAN[a][1]] for a in names], axis=1)


def kernel(x_prompt, x_sample, state_gdn, state_gdn_conv, cache_swa_k, cache_swa_v, cache_mem_k,
           cache_mem_v, mem_prompt, norm_in, w_in, gdn_conv_w, gdn_a_log, gdn_dt_bias, gdn_norm,
           swa_sinks, rel_bias, norm_mem, w_mem_kv, w_branch, w_out, norm_f):
    n_layers = norm_in.shape[0]
    assert n_layers == 1
    b, seq, _ = x_prompt.shape
    bd, ns, _ = x_sample.shape
    wb = cache_swa_k.shape[2]
    assert seq % WINDOW == 0 and seq % GDN_CHUNK == 0 and ns + CONV_W <= SUBLANE and wb == WINDOW
    lyr = 0

    w = w_in[lyr]
    w_main = _cols(w, "gq", "gk", "gv", "gz", "sq", "sz", "mq", "mz").astype(BF16)
    cb_gz, cb_sq, cb_sz, cb_mq, cb_mz = 3, 4, 5, 6, 7
    w_small = jnp.pad(_cols(w, "sk", "sv", "gb", "ga"), ((0, 0), (0, LANE - 2 * GDN_H))).astype(BF16)
    cb_sk, cb_sv, cb_ba = 0, 1, 2
    w_gate = _cols(w, "mg").astype(BF16)
    w_qkv = w_main[:, :GDN_CONV_CH]
    w_mkv = w_mem_kv[lyr].astype(BF16)
    w_br = w_branch[lyr].astype(BF16)
    w_o = w_out[lyr].astype(BF16)
    nw_in = norm_in[lyr].reshape(1, D_MODEL)
    nw_mem = norm_mem[lyr].reshape(1, D_MODEL)
    nw_f = norm_f.reshape(1, D_MODEL)
    conv_w = gdn_conv_w[lyr]
    a_row = jnp.pad(gdn_a_log[lyr].reshape(1, GDN_H), ((0, 0), (GDN_H, LANE - 2 * GDN_H)))
    dt_row = jnp.pad(gdn_dt_bias[lyr].reshape(1, GDN_H), ((0, 0), (GDN_H, LANE - 2 * GDN_H)))
    gnw = gdn_norm[lyr].reshape(1, GDN_DV)
    sinks = swa_sinks[lyr]
    bias_p = _bias_prompt(rel_bias)
    npad = SUBLANE
    bias_s = _bias_sample(rel_bias, ns, wb, wb + npad)
    sink_rows = jnp.broadcast_to(jnp.repeat(sinks.reshape(SWA_KV, SWA_G), ns, axis=1)[:, :, None],
                                 (SWA_KV, SWA_G * ns, LANE))

    t = b * seq
    xp = x_prompt.reshape(t, D_MODEL)
    xn_p = _rmsnorm(xp, nw_in)
    p_main = _proj(xn_p, w_main, BF16)
    p_small = _proj(xn_p, w_small, F32)
    g_p = _proj(xn_p, w_gate, BF16, act="sigmoid")
    xn_tail = xn_p.reshape(b, seq, D_MODEL)[:, seq - SUBLANE:, :].reshape(b * SUBLANE, D_MODEL)
    conv_p = _proj(xn_tail, w_qkv, F32).reshape(b, SUBLANE, GDN_CONV_CH)[:, SUBLANE - (CONV_W - 1):]
    kv_tail = p_small.reshape(b, seq, 3 * LANE)[:, seq - WINDOW:]
    swk_p = kv_tail[:, :, cb_sk * LANE:(cb_sk + 1) * LANE].reshape(b, WINDOW, SWA_KV, SWA_DH)
    swv_p = kv_tail[:, :, cb_sv * LANE:(cb_sv + 1) * LANE].reshape(b, WINDOW, SWA_KV, SWA_DH)

    mkv = _proj(_rmsnorm(mem_prompt.reshape(b * N_MEM, D_MODEL), nw_mem), w_mkv, F32)
    mk_p = mkv[:, :MEM_W].reshape(b, N_MEM, MEM_W)
    mv_p = mkv[:, MEM_W:].reshape(b, N_MEM, MEM_W)

    o_gdn_p, s_p = _gdn(p_main, p_main, p_small, jnp.zeros((b, HDR, GDN_CONV_CH), F32),
                        jnp.zeros((b, GDN_H, GDN_DK, GDN_DV), F32), conv_w, a_row, dt_row, gnw,
                        n_seq=b, chunk=GDN_CHUNK, n_chunks=4, per_chunk_state=False, group=4, unroll=True,
                        valid_lo=0, valid_hi=GDN_CHUNK, out_dtype=BF16,
                        z_colblock=cb_gz, ba_colblock=cb_ba)
    o_swa_p = _swa_prompt(p_main, p_small, bias_p, sinks, b, seq, cb_sq, cb_sz, cb_sk, cb_sv)
    o_mem_p = _mem_prompt(p_main, mk_p.astype(BF16), mv_p.astype(BF16), b, seq, cb_mq, cb_mz)
    y_p = _merge(o_gdn_p, o_swa_p, o_mem_p, g_p, xp, w_br, w_o, nw_f).reshape(b, seq, D_MODEL)

    ts = bd * ns
    xs = x_sample.reshape(ts, D_MODEL)
    xn_s = _rmsnorm(xs, nw_in)
    s_main = _proj(xn_s, w_main, F32)
    s_small = _proj(xn_s, w_small, F32)
    g_s = _proj(xn_s, w_gate, BF16, act="sigmoid")
    s_gdn = s_main[:, :GDN_CONV_CH + GDN_W]
    s_ba = s_small[:, cb_ba * LANE:(cb_ba + 1) * LANE]
    s_swa = jnp.concatenate([s_main[:, cb_sq * BR_W:(cb_sz + 1) * BR_W], s_small[:, :2 * LANE]], axis=1)
    s_mem = s_main[:, cb_mq * BR_W:(cb_mz + 1) * BR_W].astype(BF16)

    lo = CONV_W - 1
    hi = lo + ns
    pad_rows = ((0, 0), (lo, SUBLANE - hi), (0, 0))
    e_qkv = jnp.concatenate([state_gdn_conv[lyr], s_gdn[:, :GDN_CONV_CH].reshape(bd, ns, GDN_CONV_CH),
                             jnp.zeros((bd, SUBLANE - hi, GDN_CONV_CH), F32)], axis=1)
    e_z = jnp.pad(s_gdn[:, GDN_CONV_CH:].reshape(bd, ns, GDN_W), pad_rows)
    e_ba = jnp.pad(s_ba.reshape(bd, ns, LANE), pad_rows)
    seq_blk = 8
    o_gdn_s8, s_s = _gdn(e_qkv.reshape(bd * SUBLANE, GDN_CONV_CH), e_z.reshape(bd * SUBLANE, GDN_W),
                         e_ba.reshape(bd * SUBLANE, LANE), jnp.zeros((1, HDR, GDN_CONV_CH), F32),
                         state_gdn[lyr], conv_w, a_row, dt_row, gnw,
                         n_seq=bd, chunk=SUBLANE, n_chunks=seq_blk, per_chunk_state=True, group=8,
                         unroll=True,
                         valid_lo=lo, valid_hi=hi, out_dtype=BF16)
    o_gdn_s = o_gdn_s8.reshape(bd, SUBLANE, GDN_W)[:, lo:hi].reshape(ts, GDN_W)
    conv_s = e_qkv[:, hi - (CONV_W - 1):hi]

    def to_heads(a):
        return a.reshape(bd, ns, SWA_KV, SWA_G, SWA_DH).transpose(0, 2, 3, 1, 4).reshape(
            bd, SWA_KV, SWA_G * ns, SWA_DH)

    k_new = s_swa[:, 2 * SWA_QW:2 * SWA_QW + SWA_KVW].reshape(bd, ns, SWA_KVW)
    v_new = s_swa[:, 2 * SWA_QW + SWA_KVW:].reshape(bd, ns, SWA_KVW)
    tok_pad = ((0, 0), (0, npad - ns), (0, 0))
    ck = cache_swa_k[lyr].reshape(bd, wb, SWA_KVW)
    cv = cache_swa_v[lyr].reshape(bd, wb, SWA_KVW)
    o_swa_h = _swa_sample(to_heads(s_swa[:, :SWA_QW]).astype(BF16), to_heads(s_swa[:, SWA_QW:2 * SWA_QW]),
                          jnp.pad(k_new, tok_pad), jnp.pad(v_new, tok_pad), ck, cv, bias_s, sink_rows)
    o_swa_s = o_swa_h.reshape(bd, SWA_KV, SWA_G, ns, SWA_DH).transpose(0, 3, 1, 2, 4).reshape(ts, SWA_QW)
    swk_s = jnp.concatenate([ck, k_new], axis=1)[:, ns:].reshape(bd, wb, SWA_KV, SWA_DH)
    swv_s = jnp.concatenate([cv, v_new], axis=1)[:, ns:].reshape(bd, wb, SWA_KV, SWA_DH)

    mq = jnp.pad(s_mem[:, :MEM_W].reshape(bd, ns, MEM_W), tok_pad)
    mz = jnp.pad(s_mem[:, MEM_W:].reshape(bd, ns, MEM_W), tok_pad)
    o_mem_s = _mem_sample(mq, mz, cache_mem_k[lyr], cache_mem_v[lyr])[:, :ns].reshape(ts, MEM_W)
    y_s = _merge(o_gdn_s, o_swa_s, o_mem_s, g_s, xs, w_br, w_o, nw_f).reshape(bd, ns, D_MODEL)

    return (y_p, y_s,
            s_p[None], conv_p[None], swk_p[None], swv_p[None],
            mk_p.reshape(b, N_MEM, MEM_H, MEM_DH)[None], mv_p.reshape(b, N_MEM, MEM_H, MEM_DH)[None],
            s_s[None], conv_s[None], swk_s[None], swv_s[None])
```

```python
import functools
import math

import numpy as np
import jax
import jax.numpy as jnp
from jax import lax
from jax.experimental import pallas as pl
from jax.experimental.pallas import tpu as pltpu

F32 = jnp.float32
BF16 = jnp.bfloat16

D_MODEL = 2048
N_BRANCH = 3
BR_W = 1024
GDN_H = 8
GDN_DK = 128
GDN_DV = 128
GDN_QK = GDN_H * GDN_DK
GDN_W = GDN_H * GDN_DV
GDN_CONV_CH = 2 * GDN_QK + GDN_W
CONV_W = 4
GDN_CHUNK = 64
SWA_H = 16
SWA_KV = 2
SWA_G = SWA_H // SWA_KV
SWA_DH = 64
SWA_QW = SWA_H * SWA_DH
SWA_KVW = SWA_KV * SWA_DH
WINDOW = 128
N_BUCKETS = 32
MAX_DISTANCE = 128
N_MEM = 256
MEM_H = 4
MEM_DH = 256
MEM_W = MEM_H * MEM_DH
NORM_EPS = 1e-6

LANE = 128
SUBLANE = 8
VMEM_LIMIT = 52 * 1024 * 1024


def _cparams(sem):
    return pltpu.CompilerParams(dimension_semantics=sem, vmem_limit_bytes=VMEM_LIMIT)


def _sigmoid(x):
    return 1.0 / (1.0 + jnp.exp(-x))


def _silu(x):
    return x * _sigmoid(x)


def _softplus(x):
    return jnp.maximum(x, 0.0) + jnp.log(1.0 + jnp.exp(-jnp.abs(x)))


def _dot(a, b):
    return jnp.dot(a.astype(BF16), b.astype(BF16), preferred_element_type=F32)


def _dot_nt(a, b):
    return lax.dot_general(a.astype(BF16), b.astype(BF16), (((1,), (1,)), ((), ())),
                           preferred_element_type=F32)


def _dot_tn(a, b):
    return lax.dot_general(a.astype(BF16), b.astype(BF16), (((0,), (0,)), ((), ())),
                           preferred_element_type=F32)


def _dot_f32(a, b):
    return jnp.dot(a, b, preferred_element_type=F32, precision=lax.Precision.HIGHEST)


def _rmsnorm_kernel(x_ref, nw_ref, o_ref):
    x = x_ref[...].astype(F32)
    ms = jnp.mean(x * x, axis=-1, keepdims=True)
    o_ref[...] = (x * lax.rsqrt(ms + NORM_EPS) * nw_ref[...]).astype(o_ref.dtype)


def _rmsnorm(x, norm_w, tm_pref=512):
    m, d = x.shape
    tm = min(m, tm_pref)
    assert m % tm == 0
    return pl.pallas_call(
        _rmsnorm_kernel,
        out_shape=jax.ShapeDtypeStruct((m, d), BF16),
        grid=(m // tm,),
        in_specs=[pl.BlockSpec((tm, d), lambda i: (i, 0)),
                  pl.BlockSpec((1, d), lambda i: (0, 0))],
        out_specs=pl.BlockSpec((tm, d), lambda i: (i, 0)),
        compiler_params=_cparams(("parallel",)),
        name="rmsnorm",
    )(x, norm_w)


def _proj_kernel(x_ref, w_ref, o_ref, *, act):
    y = jnp.dot(x_ref[...], w_ref[...].astype(BF16), preferred_element_type=F32)
    if act == "sigmoid":
        y = _sigmoid(y)
    o_ref[...] = y.astype(o_ref.dtype)


def _pick_tile(n, pref):
    t = min(n, pref)
    while n % t:
        t -= LANE
    return t


def _proj(xn, w, out_dtype, act=None, col0=0, n=None, tm_pref=1024, tn_pref=2048):
    m, d = xn.shape
    n = w.shape[1] if n is None else n
    tm = min(m, tm_pref)
    assert m % tm == 0
    tn = _pick_tile(n, tn_pref if out_dtype == BF16 else tn_pref // 2)
    assert col0 % tn == 0
    jb = col0 // tn
    return pl.pallas_call(
        functools.partial(_proj_kernel, act=act),
        out_shape=jax.ShapeDtypeStruct((m, n), out_dtype),
        grid=(m // tm, n // tn),
        in_specs=[pl.BlockSpec((tm, d), lambda i, j: (i, 0)),
                  pl.BlockSpec((d, tn), lambda i, j: (0, jb + j))],
        out_specs=pl.BlockSpec((tm, tn), lambda i, j: (i, j)),
        compiler_params=_cparams(("parallel", "arbitrary")),
        name="proj",
    )(xn, w)


W_BLK = 1024
W_WIN = W_BLK + LANE


def _wprep_kernel(w_hbm, edge_ref, o_ref, inbuf, sem, *, blocks):
    j = pl.program_id(0)

    def window(b):
        start, width, _ = blocks[b]
        return pltpu.make_async_copy(w_hbm.at[:, pl.ds(start, width)],
                                     inbuf.at[b % 2, :, pl.ds(0, width)], sem.at[b % 2])

    for b, (_, width, shift) in enumerate(blocks):
        @pl.when(j == b)
        def _(b=b, width=width, shift=shift):
            if b == 0:
                window(0).start()
            if b + 1 < len(blocks):
                window(b + 1).start()
            window(b).wait()
            if shift + W_BLK <= width:
                o_ref[...] = inbuf[b % 2, :, shift:shift + W_BLK].astype(BF16)
            else:
                have = width - shift
                o_ref[...] = jnp.concatenate([inbuf[b % 2, :, shift:width], edge_ref[:, 0:W_BLK - have]],
                                             axis=1).astype(BF16)


def _wprep(w, src_cols):
    d, ncol = w.shape
    full = (ncol // LANE) * LANE
    blocks = []
    for s in src_cols:
        start = (s // LANE) * LANE
        width = min(W_WIN, full - start)
        assert s + W_BLK <= ncol and s + W_BLK - (start + width) <= ncol - full
        blocks.append((start, width, s - start))
    return pl.pallas_call(
        functools.partial(_wprep_kernel, blocks=tuple(blocks)),
        out_shape=jax.ShapeDtypeStruct((d, W_BLK * len(blocks)), BF16),
        grid=(len(blocks),),
        in_specs=[pl.BlockSpec(memory_space=pl.ANY),
                  pl.BlockSpec((d, LANE), lambda j: (0, ncol // LANE))],
        out_specs=pl.BlockSpec((d, W_BLK), lambda j: (0, j)),
        scratch_shapes=[pltpu.VMEM((2, d, W_WIN), w.dtype), pltpu.SemaphoreType.DMA((2,))],
        compiler_params=_cparams(("arbitrary",)),
        name="wprep",
    )(w, w)


HDR = SUBLANE


def _tri_inv_many(a_list, c):
    row = lax.broadcasted_iota(jnp.int32, (c, c), 0)
    col = lax.broadcasted_iota(jnp.int32, (c, c), 1)
    eye = jnp.where(row == col, 1.0, 0.0).astype(F32)
    xs = [eye - a for a in a_list]
    bs = [_dot(a, a) for a in a_list]
    n = 2
    while n < c:
        xs = [x + _dot(x, b) for x, b in zip(xs, bs)]
        n *= 2
        if n < c:
            bs = [_dot(b, b) for b in bs]
    return xs


def _gdn_kernel(qkv_ref, z_ref, ba_ref, hdr_ref, s0_ref, cw_ref, arow_ref, dtrow_ref, nw_ref,
                o_ref, s_ref, buf_ref, cv_ref, gb_ref, oacc_ref, sol_ref, att_ref, gl_ref,
                *, chunk, n_chunks, group, unroll, per_chunk_state, valid_lo, valid_hi):
    c = chunk
    tb = c * n_chunks

    if per_chunk_state:
        buf_ref[0:2 * HDR, :] = jnp.zeros((2 * HDR, GDN_CONV_CH), F32)
    else:
        @pl.when(pl.program_id(1) == 0)
        def _():
            buf_ref[0:HDR, :] = hdr_ref[0]
            buf_ref[HDR:2 * HDR, :] = jnp.zeros((HDR, GDN_CONV_CH), F32)
            s_ref[...] = s0_ref[...]

    def tap_from_history(j, n_rows):
        off = HDR - (CONV_W - 1) + j
        return buf_ref[off:off + n_rows, :] * cw_ref[j:j + 1, :]

    if qkv_ref.dtype == BF16:
        xb = qkv_ref[...]
        r = lax.broadcasted_iota(jnp.int32, (tb, tb), 0)
        cc = lax.broadcasted_iota(jnp.int32, (tb, tb), 1)
        acc = xb.astype(F32) * cw_ref[CONV_W - 1:CONV_W, :]
        for j in range(CONV_W - 1):
            shift = jnp.where(r - cc == CONV_W - 1 - j, 1.0, 0.0).astype(BF16)
            acc = acc + jnp.dot(shift, xb, preferred_element_type=F32) * cw_ref[j:j + 1, :]
        cv_ref[...] = _silu(acc)
        top = acc[0:HDR]
        for j in range(CONV_W - 1):
            top = top + tap_from_history(j, HDR)
        cv_ref[0:HDR, :] = _silu(top)
        buf_ref[0:HDR, :] = qkv_ref[tb - 2 * HDR:tb, :].astype(F32)[HDR:]
    else:
        buf_ref[HDR:HDR + tb, :] = qkv_ref[...].astype(F32)
        acc = None
        for j in range(CONV_W):
            term = tap_from_history(j, tb)
            acc = term if acc is None else acc + term
        cv_ref[...] = _silu(acc)
        if not per_chunk_state:
            buf_ref[0:HDR, :] = buf_ref[tb:tb + HDR, :]

    ba = ba_ref[...].astype(F32)
    beta_all = _sigmoid(ba)
    g_all = -jnp.exp(arow_ref[...]) * _softplus(ba + dtrow_ref[...])
    if per_chunk_state:
        r = lax.broadcasted_iota(jnp.int32, (tb, LANE), 0) & (c - 1)
        valid = (r >= valid_lo) & (r < valid_hi)
        beta_all = jnp.where(valid, beta_all, 0.0)
        g_all = jnp.where(valid, g_all, 0.0)
    gb_ref[0] = beta_all
    gb_ref[1] = g_all

    row = lax.broadcasted_iota(jnp.int32, (c, c), 0)
    col = lax.broadcasted_iota(jnp.int32, (c, c), 1)
    causal = row >= col
    strict = row > col
    tril = jnp.where(causal, 1.0, 0.0).astype(F32)
    scale_q = GDN_DK ** -0.5

    heads = range(GDN_H)
    qcol = lambda h: slice(h * GDN_DK, (h + 1) * GDN_DK)
    kcol = lambda h: slice(GDN_QK + h * GDN_DK, GDN_QK + (h + 1) * GDN_DK)
    vcol = lambda h: slice(2 * GDN_QK + h * GDN_DV, 2 * GDN_QK + (h + 1) * GDN_DV)
    ucol = lambda h: slice(h * (GDN_DV + GDN_DK), h * (GDN_DV + GDN_DK) + GDN_DV)
    wcol = lambda h: slice(h * (GDN_DV + GDN_DK) + GDN_DV, (h + 1) * (GDN_DV + GDN_DK))
    if per_chunk_state:
        rv = lax.broadcasted_iota(jnp.int32, (c, 1), 0)
        rvalid = (rv >= valid_lo) & (rv < valid_hi)

    def chunk_rows(ci):
        if isinstance(ci, int):
            return slice(ci * c, (ci + 1) * c)
        return pl.ds(pl.multiple_of(ci * c, c), c)

    def prep_body(gi, carry):
        probs = []
        for j in range(group):
            rows = chunk_rows(gi * group + j)
            beta_c = gb_ref[0, rows, :]
            gc_all = _dot_f32(tril, gb_ref[1, rows, :])
            gc_t = gc_all.T
            gl_ref[gi * group + j] = gc_all[c - 1:c, :]
            for h in heads:
                q = cv_ref[rows, qcol(h)]
                k = cv_ref[rows, kcol(h)]
                v = cv_ref[rows, vcol(h)]
                q = q * lax.rsqrt(jnp.sum(q * q, axis=-1, keepdims=True) + NORM_EPS) * scale_q
                k = k * lax.rsqrt(jnp.sum(k * k, axis=-1, keepdims=True) + NORM_EPS)
                if per_chunk_state:
                    q = jnp.where(rvalid, q, 0.0)
                    k = jnp.where(rvalid, k, 0.0)
                    v = jnp.where(rvalid, v, 0.0)
                beta = beta_c[:, h:h + 1]
                gc_col = gc_all[:, GDN_H + h:GDN_H + h + 1]
                gc_row = gc_t[GDN_H + h:GDN_H + h + 1, :]
                diff = jnp.where(causal, gc_col - gc_row, 0.0)
                decay = jnp.where(causal, jnp.exp(diff), 0.0)
                e_gc = jnp.exp(gc_col)
                kb = k * beta
                rhs = jnp.concatenate([v * beta, kb * e_gc], axis=1)
                cv_ref[rows, qcol(h)] = q * e_gc
                cv_ref[rows, kcol(h)] = k * jnp.exp(gc_col[c - 1:c, :] - gc_col)
                probs.append((rows, h, q, k, kb, rhs, decay))
        kq = [_dot_nt(jnp.concatenate([kb, q], axis=0), k) for (_, _, q, k, kb, _, _) in probs]
        a_list = [jnp.where(strict, kq_i[:c] * p[6], 0.0) for kq_i, p in zip(kq, probs)]
        t_inv = _tri_inv_many(a_list, c)
        for t_i, kq_i, (rows, h, _, _, _, rhs, decay) in zip(t_inv, kq, probs):
            sol_ref[rows, h * (GDN_DV + GDN_DK):(h + 1) * (GDN_DV + GDN_DK)] = _dot(t_i, rhs)
            att_ref[h, rows, :] = kq_i[c:] * decay
        return carry

    def scan_body(ci, carry):
        rows = chunk_rows(ci)
        si = ci if per_chunk_state else 0
        g_tot = jnp.exp(gl_ref[ci])
        s_old = [s_ref[si, h] for h in heads]
        wq_s = [_dot(jnp.concatenate([sol_ref[rows, wcol(h)], cv_ref[rows, qcol(h)]], axis=0), s_old[h])
                for h in heads]
        v_new = [sol_ref[rows, ucol(h)] - wq_s[h][:c] for h in heads]
        o_att = [_dot(att_ref[h, rows, :], v_new[h]) for h in heads]
        for h in heads:
            s_ref[si, h] = (s_old[h] * g_tot[:, GDN_H + h:GDN_H + h + 1]
                            + _dot_tn(cv_ref[rows, kcol(h)], v_new[h]))
        for h in heads:
            o = wq_s[h][c:] + o_att[h]
            o = o * lax.rsqrt(jnp.mean(o * o, axis=-1, keepdims=True) + NORM_EPS) * nw_ref[...]
            oacc_ref[rows, h * GDN_DV:(h + 1) * GDN_DV] = o
        return carry

    if per_chunk_state:
        s_ref[...] = s0_ref[...]
    if unroll:
        for gi in range(n_chunks // group):
            prep_body(gi, 0)
        for ci in range(n_chunks):
            scan_body(ci, 0)
    else:
        lax.fori_loop(0, n_chunks // group, prep_body, 0)
        lax.fori_loop(0, n_chunks, scan_body, 0)
    o_ref[...] = (oacc_ref[...] * _silu(z_ref[...].astype(F32))).astype(o_ref.dtype)


def _gdn(qkv, z, ba, hdr, s0, conv_w, a_row, dt_row, norm_w, *, n_seq, chunk, n_chunks,
         per_chunk_state, valid_lo, valid_hi, out_dtype, z_colblock=0, ba_colblock=0, group=2,
         unroll=False):
    rows = qkv.shape[0]
    tb = chunk * n_chunks
    if per_chunk_state:
        grid = (rows // tb,)
        sem = ("arbitrary",)
        rmap = lambda i: (i, 0)
        zmap = lambda i: (i, z_colblock)
        bmap = lambda i: (i, ba_colblock)
        hmap = lambda i: (0, 0, 0)
        smap = lambda i: (i, 0, 0, 0)
        cmap = lambda i: (0, 0)
        ns = n_chunks
    else:
        steps = rows // n_seq // tb
        grid = (n_seq, steps)
        sem = ("parallel", "arbitrary")
        rmap = lambda b, n: (b * steps + n, 0)
        zmap = lambda b, n: (b * steps + n, z_colblock)
        bmap = lambda b, n: (b * steps + n, ba_colblock)
        hmap = lambda b, n: (b, 0, 0)
        smap = lambda b, n: (b, 0, 0, 0)
        cmap = lambda b, n: (0, 0)
        ns = 1
    assert n_chunks % group == 0
    kern = functools.partial(_gdn_kernel, chunk=chunk, n_chunks=n_chunks, group=group, unroll=unroll,
                             per_chunk_state=per_chunk_state, valid_lo=valid_lo, valid_hi=valid_hi)
    return pl.pallas_call(
        kern,
        out_shape=(jax.ShapeDtypeStruct((rows, GDN_W), out_dtype),
                   jax.ShapeDtypeStruct(s0.shape, F32)),
        grid=grid,
        in_specs=[pl.BlockSpec((tb, GDN_CONV_CH), rmap),
                  pl.BlockSpec((tb, GDN_W), zmap),
                  pl.BlockSpec((tb, LANE), bmap),
                  pl.BlockSpec((1, HDR, GDN_CONV_CH), hmap),
                  pl.BlockSpec((ns, GDN_H, GDN_DK, GDN_DV), smap),
                  pl.BlockSpec((CONV_W, GDN_CONV_CH), cmap),
                  pl.BlockSpec((1, LANE), cmap),
                  pl.BlockSpec((1, LANE), cmap),
                  pl.BlockSpec((1, GDN_DV), cmap)],
        out_specs=(pl.BlockSpec((tb, GDN_W), rmap),
                   pl.BlockSpec((ns, GDN_H, GDN_DK, GDN_DV), smap)),
        scratch_shapes=[pltpu.VMEM((HDR + tb, GDN_CONV_CH), F32),
                        pltpu.VMEM((tb, GDN_CONV_CH), F32),
                        pltpu.VMEM((2, tb, LANE), F32),
                        pltpu.VMEM((tb, GDN_W), F32),
                        pltpu.VMEM((tb, GDN_H * (GDN_DV + GDN_DK)), F32),
                        pltpu.VMEM((GDN_H, tb, chunk), F32),
                        pltpu.VMEM((n_chunks, 1, LANE), F32)],
        compiler_params=_cparams(sem),
        name="gdn",
    )(qkv, z, ba, hdr, s0, conv_w, a_row, dt_row, norm_w)


def _t5_bucket_np(dist):
    n = np.maximum(dist, 0)
    max_exact = N_BUCKETS // 2
    nf = np.maximum(n, 1).astype(np.float32)
    large = max_exact + (np.log(nf / np.float32(max_exact)) / np.float32(math.log(MAX_DISTANCE / max_exact))
                         * np.float32(N_BUCKETS - max_exact)).astype(np.int32)
    large = np.minimum(large, N_BUCKETS - 1)
    return np.where(n < max_exact, n, large).astype(np.int32)


def _bias_prompt_kernel(code_ref, tab_ref, o_ref):
    h = pl.program_id(1)
    code = code_ref[0]
    acc = jnp.full(code.shape, -jnp.inf, F32)
    for b in range(N_BUCKETS):
        acc = jnp.where(code == b, tab_ref[b, h], acc)
    o_ref[0, 0] = acc


def _bias_prompt(table):
    qi = np.arange(WINDOW)[:, None]
    sj = np.arange(WINDOW)[None, :]
    own = sj <= qi
    bucket = _t5_bucket_np(np.where(own, qi - sj, qi + WINDOW - sj))
    code_first = np.where(own, bucket, -1)
    code = jnp.asarray(np.stack([code_first, bucket]).astype(np.int32))
    return pl.pallas_call(
        _bias_prompt_kernel,
        out_shape=jax.ShapeDtypeStruct((2, SWA_H, WINDOW, WINDOW), F32),
        grid=(2, SWA_H),
        in_specs=[pl.BlockSpec((1, WINDOW, WINDOW), lambda v, h: (v, 0, 0)),
                  pl.BlockSpec(memory_space=pltpu.SMEM)],
        out_specs=pl.BlockSpec((1, 1, WINDOW, WINDOW), lambda v, h: (v, h, 0, 0)),
        compiler_params=_cparams(("arbitrary", "arbitrary")),
        name="swa_bias_prompt",
    )(code, table)


def _bias_sample_kernel(code_ref, tab_ref, o_ref):
    kv = pl.program_id(0)
    code = code_ref[...]
    acc = jnp.full(code.shape, -jnp.inf, F32)
    for g in range(SWA_G):
        for b in range(N_BUCKETS):
            acc = jnp.where(code == b + N_BUCKETS * g, tab_ref[b, kv * SWA_G + g], acc)
    o_ref[0] = acc


def _bias_sample(table, n_tok, n_cache, n_keys_pad):
    dist = (n_cache + np.arange(n_tok))[:, None] - np.arange(n_keys_pad)[None, :]
    valid = (dist >= 0) & (dist < WINDOW) & (np.arange(n_keys_pad)[None, :] < n_cache + n_tok)
    bucket = _t5_bucket_np(dist)
    code_t = np.where(valid, bucket, -1)
    g = np.arange(SWA_G)[:, None, None]
    code = np.where(code_t[None] >= 0, code_t[None] + N_BUCKETS * g, -1)
    code = jnp.asarray(code.reshape(SWA_G * n_tok, n_keys_pad).astype(np.int32))
    return pl.pallas_call(
        _bias_sample_kernel,
        out_shape=jax.ShapeDtypeStruct((SWA_KV, SWA_G * n_tok, n_keys_pad), F32),
        grid=(SWA_KV,),
        in_specs=[pl.BlockSpec((SWA_G * n_tok, n_keys_pad), lambda k: (0, 0)),
                  pl.BlockSpec(memory_space=pltpu.SMEM)],
        out_specs=pl.BlockSpec((1, SWA_G * n_tok, n_keys_pad), lambda k: (k, 0, 0)),
        compiler_params=_cparams(("arbitrary",)),
        name="swa_bias_sample",
    )(code, table)


def _sink_softmax_pv(logits, sink, v):
    m = jnp.maximum(jnp.max(logits, axis=-1, keepdims=True), sink)
    p = jnp.exp(logits - m)
    den = jnp.sum(p, axis=-1, keepdims=True) + jnp.exp(sink - m)
    return _dot(p, v) / den


def _swa_prompt_kernel(q_ref, z_ref, kc_ref, kp_ref, vc_ref, vp_ref, bias_ref, sink_ref, o_ref):
    kk = jnp.concatenate([kp_ref[...], kc_ref[...]], axis=0).astype(BF16)
    vv = jnp.concatenate([vp_ref[...], vc_ref[...]], axis=0).astype(BF16)
    scale = SWA_DH ** -0.5
    row = lax.broadcasted_iota(jnp.int32, (WINDOW, WINDOW), 0)
    col = lax.broadcasted_iota(jnp.int32, (WINDOW, WINDOW), 1)
    from_prev = col > row
    for kv in range(SWA_KV):
        ks = slice(kv * SWA_DH, (kv + 1) * SWA_DH)
        heads = range(kv * SWA_G, (kv + 1) * SWA_G)
        cs = lambda h: slice(h * SWA_DH, (h + 1) * SWA_DH)
        lg = [_dot_nt(q_ref[:, cs(h)] * scale, kk[:, ks]) for h in heads]
        ps, dens = [], []
        for h, l in zip(heads, lg):
            l = jnp.where(from_prev, l[:, :WINDOW], l[:, WINDOW:]) + bias_ref[0, h]
            sink = sink_ref[h]
            m = jnp.maximum(jnp.max(l, axis=-1, keepdims=True), sink)
            p = jnp.exp(l - m)
            dens.append(jnp.sum(p, axis=-1, keepdims=True) + jnp.exp(sink - m))
            ps.append(jnp.concatenate([jnp.where(from_prev, p, 0.0), jnp.where(from_prev, 0.0, p)],
                                      axis=1))
        outs = [_dot(p, vv[:, ks]) / den for p, den in zip(ps, dens)]
        for j in range(0, SWA_G, 2):
            two = slice(heads[j] * SWA_DH, (heads[j] + 2) * SWA_DH)
            o2 = jnp.concatenate([outs[j], outs[j + 1]], axis=1)
            o_ref[:, two] = (o2 * _silu(z_ref[:, two].astype(F32))).astype(o_ref.dtype)


def _swa_prompt(proj, kv, bias, sinks, n_seq, seq_len, q_colblock, z_colblock, k_colblock, v_colblock):
    nb = seq_len // WINDOW
    cur = lambda b, n: b * nb + n
    prev = lambda b, n: b * nb + jnp.maximum(n - 1, 0)
    return pl.pallas_call(
        _swa_prompt_kernel,
        out_shape=jax.ShapeDtypeStruct((n_seq * seq_len, SWA_QW), BF16),
        grid=(n_seq, nb),
        in_specs=[pl.BlockSpec((WINDOW, SWA_QW), lambda b, n: (cur(b, n), q_colblock)),
                  pl.BlockSpec((WINDOW, SWA_QW), lambda b, n: (cur(b, n), z_colblock)),
                  pl.BlockSpec((WINDOW, SWA_KVW), lambda b, n: (cur(b, n), k_colblock)),
                  pl.BlockSpec((WINDOW, SWA_KVW), lambda b, n: (prev(b, n), k_colblock)),
                  pl.BlockSpec((WINDOW, SWA_KVW), lambda b, n: (cur(b, n), v_colblock)),
                  pl.BlockSpec((WINDOW, SWA_KVW), lambda b, n: (prev(b, n), v_colblock)),
                  pl.BlockSpec((1, SWA_H, WINDOW, WINDOW), lambda b, n: (jnp.minimum(n, 1), 0, 0, 0)),
                  pl.BlockSpec(memory_space=pltpu.SMEM)],
        out_specs=pl.BlockSpec((WINDOW, SWA_QW), lambda b, n: (cur(b, n), 0)),
        compiler_params=_cparams(("parallel", "arbitrary")),
        name="swa_prompt",
    )(proj, proj, kv, kv, kv, kv, bias, sinks)


def _swa_sample_kernel(q_ref, z_ref, kn_ref, vn_ref, ck_ref, cv_ref, bias_ref, sink_ref, o_ref, *, n_seq_blk):
    scale = SWA_DH ** -0.5
    probs = [(s, kv) for s in range(n_seq_blk) for kv in range(SWA_KV)]
    ks = lambda kv: slice(kv * SWA_DH, (kv + 1) * SWA_DH)
    kk = [jnp.concatenate([ck_ref[s], kn_ref[s]], axis=0).astype(BF16) for s in range(n_seq_blk)]
    vv = [jnp.concatenate([cv_ref[s], vn_ref[s]], axis=0).astype(BF16) for s in range(n_seq_blk)]
    logits = [_dot_nt(q_ref[s, kv], kk[s][:, ks(kv)]) * scale + bias_ref[kv] for s, kv in probs]
    outs = [_sink_softmax_pv(l, sink_ref[kv][:, 0:1], vv[s][:, ks(kv)]) for l, (s, kv) in zip(logits, probs)]
    for oh, (s, kv) in zip(outs, probs):
        o_ref[s, kv] = (oh * _silu(z_ref[s, kv].astype(F32))).astype(o_ref.dtype)


def _swa_sample(q, z, k_new, v_new, cache_k, cache_v, bias, sink_rows, n_seq_blk=8):
    bd, _, rows, _ = q.shape
    wb = cache_k.shape[1]
    npad = k_new.shape[1]
    blk4 = lambda i: (i, 0, 0, 0)
    blk3 = lambda i: (i, 0, 0)
    return pl.pallas_call(
        functools.partial(_swa_sample_kernel, n_seq_blk=n_seq_blk),
        out_shape=jax.ShapeDtypeStruct(q.shape, BF16),
        grid=(bd // n_seq_blk,),
        in_specs=[pl.BlockSpec((n_seq_blk, SWA_KV, rows, SWA_DH), blk4),
                  pl.BlockSpec((n_seq_blk, SWA_KV, rows, SWA_DH), blk4),
                  pl.BlockSpec((n_seq_blk, npad, SWA_KVW), blk3),
                  pl.BlockSpec((n_seq_blk, npad, SWA_KVW), blk3),
                  pl.BlockSpec((n_seq_blk, wb, SWA_KVW), blk3),
                  pl.BlockSpec((n_seq_blk, wb, SWA_KVW), blk3),
                  pl.BlockSpec((SWA_KV, rows, wb + npad), lambda i: (0, 0, 0)),
                  pl.BlockSpec((SWA_KV, rows, LANE), lambda i: (0, 0, 0))],
        out_specs=pl.BlockSpec((n_seq_blk, SWA_KV, rows, SWA_DH), blk4),
        compiler_params=_cparams(("arbitrary",)),
        name="swa_sample",
    )(q, z, k_new, v_new, cache_k, cache_v, bias, sink_rows)


def _mem_attend(probs, q_of, z_of, k_of, v_of, store):
    scale = MEM_DH ** -0.5
    logits = [_dot_nt(q_of(p) * scale, k_of(p)) for p in probs]
    ps, dens = [], []
    for l in logits:
        m = jnp.max(l, axis=-1, keepdims=True)
        e = jnp.exp(l - m)
        dens.append(jnp.sum(e, axis=-1, keepdims=True))
        ps.append(e)
    outs = [_dot(e, v_of(p)) / den for e, den, p in zip(ps, dens, probs)]
    for p, oh in zip(probs, outs):
        store(p, oh * _silu(z_of(p).astype(F32)))


def _mem_cols(h):
    return slice(h * MEM_DH, (h + 1) * MEM_DH)


def _mem_prompt_kernel(q_ref, z_ref, k_ref, v_ref, o_ref):
    def store(h, val):
        o_ref[:, _mem_cols(h)] = val.astype(o_ref.dtype)
    _mem_attend(range(MEM_H), lambda h: q_ref[:, _mem_cols(h)], lambda h: z_ref[:, _mem_cols(h)],
                lambda h: k_ref[0, :, _mem_cols(h)], lambda h: v_ref[0, :, _mem_cols(h)], store)


def _mem_prompt(proj, mk, mv, n_seq, seq_len, q_colblock, z_colblock, tq=512):
    steps = seq_len // tq
    return pl.pallas_call(
        _mem_prompt_kernel,
        out_shape=jax.ShapeDtypeStruct((n_seq * seq_len, MEM_W), BF16),
        grid=(n_seq, steps),
        in_specs=[pl.BlockSpec((tq, MEM_W), lambda b, n: (b * steps + n, q_colblock)),
                  pl.BlockSpec((tq, MEM_W), lambda b, n: (b * steps + n, z_colblock)),
                  pl.BlockSpec((1, N_MEM, MEM_W), lambda b, n: (b, 0, 0)),
                  pl.BlockSpec((1, N_MEM, MEM_W), lambda b, n: (b, 0, 0))],
        out_specs=pl.BlockSpec((tq, MEM_W), lambda b, n: (b * steps + n, 0)),
        compiler_params=_cparams(("parallel", "arbitrary")),
        name="mem_prompt",
    )(proj, proj, mk, mv)


def _mem_sample_kernel(q_ref, z_ref, k_hbm, v_hbm, o_ref, kbuf, vbuf, sem, *, n_seq_blk):
    i = pl.program_id(0)
    n_steps = pl.num_programs(0)
    slot = i % 2

    def copies(step, slot_):
        seqs = pl.ds(step * n_seq_blk, n_seq_blk)
        out = []
        for h in range(MEM_H):
            out.append(pltpu.make_async_copy(k_hbm.at[seqs, :, h, :], kbuf.at[slot_, h], sem.at[0, slot_, h]))
            out.append(pltpu.make_async_copy(v_hbm.at[seqs, :, h, :], vbuf.at[slot_, h], sem.at[1, slot_, h]))
        return out

    @pl.when(i == 0)
    def _():
        for cp in copies(0, 0):
            cp.start()

    @pl.when(i + 1 < n_steps)
    def _():
        for cp in copies(i + 1, 1 - slot):
            cp.start()

    for cp in copies(i, slot):
        cp.wait()

    def store(p, val):
        o_ref[p[0], :, _mem_cols(p[1])] = val.astype(o_ref.dtype)
    probs = [(s, h) for s in range(n_seq_blk) for h in range(MEM_H)]
    _mem_attend(probs, lambda p: q_ref[p[0], :, _mem_cols(p[1])], lambda p: z_ref[p[0], :, _mem_cols(p[1])],
                lambda p: kbuf[slot, p[1], p[0]], lambda p: vbuf[slot, p[1], p[0]], store)


def _mem_sample(q, z, cache_k, cache_v, n_seq_blk=4):
    bd, rows, _ = q.shape
    blk = lambda i: (i, 0, 0)
    buf = pltpu.VMEM((2, MEM_H, n_seq_blk, N_MEM, MEM_DH), cache_k.dtype)
    return pl.pallas_call(
        functools.partial(_mem_sample_kernel, n_seq_blk=n_seq_blk),
        out_shape=jax.ShapeDtypeStruct(q.shape, BF16),
        grid=(bd // n_seq_blk,),
        in_specs=[pl.BlockSpec((n_seq_blk, rows, MEM_W), blk),
                  pl.BlockSpec((n_seq_blk, rows, MEM_W), blk),
                  pl.BlockSpec(memory_space=pl.ANY),
                  pl.BlockSpec(memory_space=pl.ANY)],
        out_specs=pl.BlockSpec((n_seq_blk, rows, MEM_W), blk),
        scratch_shapes=[buf, buf, pltpu.SemaphoreType.DMA((2, 2, MEM_H))],
        compiler_params=_cparams(("arbitrary",)),
        name="mem_sample",
    )(q, z, cache_k, cache_v)


def _merge_kernel(og_ref, os_ref, om_ref, gate_ref, x_ref, wb_ref, wo_ref, nf_ref, y_ref):
    merged = None
    for b, o_ref in enumerate((og_ref, os_ref, om_ref)):
        t = jnp.dot(o_ref[...], wb_ref[b], preferred_element_type=F32)
        t = t * gate_ref[:, b * D_MODEL:(b + 1) * D_MODEL].astype(F32)
        merged = t if merged is None else merged + t
    h = x_ref[...] + jnp.dot(merged.astype(BF16), wo_ref[...], preferred_element_type=F32)
    ms = jnp.mean(h * h, axis=-1, keepdims=True)
    y_ref[...] = h * lax.rsqrt(ms + NORM_EPS) * nf_ref[...]


def _merge(o_gdn, o_swa, o_mem, gates, x, w_branch, w_out, norm_f, tm=256):
    m = x.shape[0]
    tm = min(tm, m)
    row = lambda i: (i, 0)
    const2 = lambda i: (0, 0)
    return pl.pallas_call(
        _merge_kernel,
        out_shape=jax.ShapeDtypeStruct((m, D_MODEL), F32),
        grid=(m // tm,),
        in_specs=[pl.BlockSpec((tm, BR_W), row),
                  pl.BlockSpec((tm, BR_W), row),
                  pl.BlockSpec((tm, BR_W), row),
                  pl.BlockSpec((tm, N_BRANCH * D_MODEL), row),
                  pl.BlockSpec((tm, D_MODEL), row),
                  pl.BlockSpec((N_BRANCH, BR_W, D_MODEL), lambda i: (0, 0, 0),
                               pipeline_mode=pl.Buffered(1)),
                  pl.BlockSpec((D_MODEL, D_MODEL), const2, pipeline_mode=pl.Buffered(1)),
                  pl.BlockSpec((1, D_MODEL), const2)],
        out_specs=pl.BlockSpec((tm, D_MODEL), row),
        compiler_params=_cparams(("parallel",)),
        name="merge",
    )(o_gdn, o_swa, o_mem, gates, x, w_branch, w_out, norm_f)


_IN_SIZES = (GDN_QK, GDN_QK, GDN_W, GDN_W, GDN_H, GDN_H, SWA_QW, SWA_KVW, SWA_KVW, SWA_QW,
             MEM_W, MEM_W, N_BRANCH * D_MODEL)
_IN_NAMES = ("gq", "gk", "gv", "gz", "gb", "ga", "sq", "sk", "sv", "sz", "mq", "mz", "mg")
_IN_SPAN = {name: (int(off), int(off + size)) for name, off, size in
            zip(_IN_NAMES, np.cumsum((0,) + _IN_SIZES[:-1]), _IN_SIZES)}


def _cols(w, *names):
    return jnp.concatenate([w[:, _IN_SPAN[a][0]:_IN_SPAN[a][1]] for a in names], axis=1)


def kernel(x_prompt, x_sample, state_gdn, state_gdn_conv, cache_swa_k, cache_swa_v, cache_mem_k,
           cache_mem_v, mem_prompt, norm_in, w_in, gdn_conv_w, gdn_a_log, gdn_dt_bias, gdn_norm,
           swa_sinks, rel_bias, norm_mem, w_mem_kv, w_branch, w_out, norm_f):
    n_layers = norm_in.shape[0]
    assert n_layers == 1
    b, seq, _ = x_prompt.shape
    bd, ns, _ = x_sample.shape
    wb = cache_swa_k.shape[2]
    assert seq % WINDOW == 0 and seq % GDN_CHUNK == 0 and ns + CONV_W <= SUBLANE and wb == WINDOW
    lyr = 0

    w = w_in[lyr]
    main_names = ("gq", "gk", "gv", "gz", "sq", "sz", "mq", "mz")
    n_main = len(main_names) * W_BLK
    n_gate = N_BRANCH * D_MODEL
    w_all = _wprep(w, [_IN_SPAN[a][0] for a in main_names]
                   + [_IN_SPAN["mg"][0] + W_BLK * k for k in range(n_gate // W_BLK)])
    cb_gz, cb_sq, cb_sz, cb_mq, cb_mz = 3, 4, 5, 6, 7
    w_small = jnp.pad(_cols(w, "sk", "sv", "gb", "ga"), ((0, 0), (0, LANE - 2 * GDN_H)))
    cb_sk, cb_sv, cb_ba = 0, 1, 2
    w_mkv = w_mem_kv[lyr].astype(BF16)
    w_br = w_branch[lyr].astype(BF16)
    w_o = w_out[lyr].astype(BF16)
    nw_in = norm_in[lyr].reshape(1, D_MODEL)
    nw_mem = norm_mem[lyr].reshape(1, D_MODEL)
    nw_f = norm_f.reshape(1, D_MODEL)
    conv_w = gdn_conv_w[lyr]
    a_row = jnp.pad(gdn_a_log[lyr].reshape(1, GDN_H), ((0, 0), (GDN_H, LANE - 2 * GDN_H)))
    dt_row = jnp.pad(gdn_dt_bias[lyr].reshape(1, GDN_H), ((0, 0), (GDN_H, LANE - 2 * GDN_H)))
    gnw = gdn_norm[lyr].reshape(1, GDN_DV)
    sinks = swa_sinks[lyr]
    bias_p = _bias_prompt(rel_bias)
    npad = SUBLANE
    bias_s = _bias_sample(rel_bias, ns, wb, wb + npad)
    sink_rows = jnp.broadcast_to(jnp.repeat(sinks.reshape(SWA_KV, SWA_G), ns, axis=1)[:, :, None],
                                 (SWA_KV, SWA_G * ns, LANE))

    t = b * seq
    xp = x_prompt.reshape(t, D_MODEL)
    xn_p = _rmsnorm(xp, nw_in)
    p_main = _proj(xn_p, w_all, BF16, n=n_main)
    p_small = _proj(xn_p, w_small, F32)
    g_p = _proj(xn_p, w_all, BF16, act="sigmoid", col0=n_main, n=n_gate)
    xn_tail = xn_p.reshape(b, seq, D_MODEL)[:, seq - SUBLANE:, :].reshape(b * SUBLANE, D_MODEL)
    conv_p = _proj(xn_tail, w_all, F32, n=GDN_CONV_CH).reshape(b, SUBLANE, GDN_CONV_CH)[:, SUBLANE - (CONV_W - 1):]
    kv_tail = p_small.reshape(b, seq, 3 * LANE)[:, seq - WINDOW:]
    swk_p = kv_tail[:, :, cb_sk * LANE:(cb_sk + 1) * LANE].reshape(b, WINDOW, SWA_KV, SWA_DH)
    swv_p = kv_tail[:, :, cb_sv * LANE:(cb_sv + 1) * LANE].reshape(b, WINDOW, SWA_KV, SWA_DH)

    mkv = _proj(_rmsnorm(mem_prompt.reshape(b * N_MEM, D_MODEL), nw_mem), w_mkv, F32)
    mk_p = mkv[:, :MEM_W].reshape(b, N_MEM, MEM_W)
    mv_p = mkv[:, MEM_W:].reshape(b, N_MEM, MEM_W)

    o_gdn_p, s_p = _gdn(p_main, p_main, p_small, jnp.zeros((b, HDR, GDN_CONV_CH), F32),
                        jnp.zeros((b, GDN_H, GDN_DK, GDN_DV), F32), conv_w, a_row, dt_row, gnw,
                        n_seq=b, chunk=GDN_CHUNK, n_chunks=4, per_chunk_state=False, group=4, unroll=True,
                        valid_lo=0, valid_hi=GDN_CHUNK, out_dtype=BF16,
                        z_colblock=cb_gz, ba_colblock=cb_ba)
    o_swa_p = _swa_prompt(p_main, p_small, bias_p, sinks, b, seq, cb_sq, cb_sz, cb_sk, cb_sv)
    o_mem_p = _mem_prompt(p_main, mk_p.astype(BF16), mv_p.astype(BF16), b, seq, cb_mq, cb_mz)
    y_p = _merge(o_gdn_p, o_swa_p, o_mem_p, g_p, xp, w_br, w_o, nw_f).reshape(b, seq, D_MODEL)

    ts = bd * ns
    xs = x_sample.reshape(ts, D_MODEL)
    xn_s = _rmsnorm(xs, nw_in)
    s_main = _proj(xn_s, w_all, F32, n=n_main)
    s_small = _proj(xn_s, w_small, F32)
    g_s = _proj(xn_s, w_all, BF16, act="sigmoid", col0=n_main, n=n_gate)
    s_gdn = s_main[:, :GDN_CONV_CH + GDN_W]
    s_ba = s_small[:, cb_ba * LANE:(cb_ba + 1) * LANE]
    s_swa = jnp.concatenate([s_main[:, cb_sq * BR_W:(cb_sz + 1) * BR_W], s_small[:, :2 * LANE]], axis=1)
    s_mem = s_main[:, cb_mq * BR_W:(cb_mz + 1) * BR_W].astype(BF16)

    lo = CONV_W - 1
    hi = lo + ns
    pad_rows = ((0, 0), (lo, SUBLANE - hi), (0, 0))
    e_qkv = jnp.concatenate([state_gdn_conv[lyr], s_gdn[:, :GDN_CONV_CH].reshape(bd, ns, GDN_CONV_CH),
                             jnp.zeros((bd, SUBLANE - hi, GDN_CONV_CH), F32)], axis=1)
    e_z = jnp.pad(s_gdn[:, GDN_CONV_CH:].reshape(bd, ns, GDN_W), pad_rows)
    e_ba = jnp.pad(s_ba.reshape(bd, ns, LANE), pad_rows)
    seq_blk = 8
    o_gdn_s8, s_s = _gdn(e_qkv.reshape(bd * SUBLANE, GDN_CONV_CH), e_z.reshape(bd * SUBLANE, GDN_W),
                         e_ba.reshape(bd * SUBLANE, LANE), jnp.zeros((1, HDR, GDN_CONV_CH), F32),
                         state_gdn[lyr], conv_w, a_row, dt_row, gnw,
                         n_seq=bd, chunk=SUBLANE, n_chunks=seq_blk, per_chunk_state=True, group=8,
                         unroll=True,
                         valid_lo=lo, valid_hi=hi, out_dtype=BF16)
    o_gdn_s = o_gdn_s8.reshape(bd, SUBLANE, GDN_W)[:, lo:hi].reshape(ts, GDN_W)
    conv_s = e_qkv[:, hi - (CONV_W - 1):hi]

    def to_heads(a):
        return a.reshape(bd, ns, SWA_KV, SWA_G, SWA_DH).transpose(0, 2, 3, 1, 4).reshape(
            bd, SWA_KV, SWA_G * ns, SWA_DH)

    k_new = s_swa[:, 2 * SWA_QW:2 * SWA_QW + SWA_KVW].reshape(bd, ns, SWA_KVW)
    v_new = s_swa[:, 2 * SWA_QW + SWA_KVW:].reshape(bd, ns, SWA_KVW)
    tok_pad = ((0, 0), (0, npad - ns), (0, 0))
    ck = cache_swa_k[lyr].reshape(bd, wb, SWA_KVW)
    cv = cache_swa_v[lyr].reshape(bd, wb, SWA_KVW)
    o_swa_h = _swa_sample(to_heads(s_swa[:, :SWA_QW]).astype(BF16), to_heads(s_swa[:, SWA_QW:2 * SWA_QW]),
                          jnp.pad(k_new, tok_pad), jnp.pad(v_new, tok_pad), ck, cv, bias_s, sink_rows)
    o_swa_s = o_swa_h.reshape(bd, SWA_KV, SWA_G, ns, SWA_DH).transpose(0, 3, 1, 2, 4).reshape(ts, SWA_QW)
    swk_s = jnp.concatenate([ck, k_new], axis=1)[:, ns:].reshape(bd, wb, SWA_KV, SWA_DH)
    swv_s = jnp.concatenate([cv, v_new], axis=1)[:, ns:].reshape(bd, wb, SWA_KV, SWA_DH)

    mq = jnp.pad(s_mem[:, :MEM_W].reshape(bd, ns, MEM_W), tok_pad)
    mz = jnp.pad(s_mem[:, MEM_W:].reshape(bd, ns, MEM_W), tok_pad)
    o_mem_s = _mem_sample(mq, mz, cache_mem_k[lyr], cache_mem_v[lyr])[:, :ns].reshape(ts, MEM_W)
    y_s = _merge(o_gdn_s, o_swa_s, o_mem_s, g_s, xs, w_br, w_o, nw_f).reshape(bd, ns, D_MODEL)

    return (y_p, y_s,
            s_p[None], conv_p[None], swk_p[None], swv_p[None],
            mk_p.reshape(b, N_MEM, MEM_H, MEM_DH)[None], mv_p.reshape(b, N_MEM, MEM_H, MEM_DH)[None],
            s_s[None], conv_s[None], swk_s[None], swv_s[None])
```

```python
import functools
import math

import numpy as np
import jax
import jax.numpy as jnp
from jax import lax
from jax.experimental import pallas as pl
from jax.experimental.pallas import tpu as pltpu

F32 = jnp.float32
BF16 = jnp.bfloat16

D_MODEL = 2048
N_BRANCH = 3
BR_W = 1024
GDN_H = 8
GDN_DK = 128
GDN_DV = 128
GDN_QK = GDN_H * GDN_DK
GDN_W = GDN_H * GDN_DV
GDN_CONV_CH = 2 * GDN_QK + GDN_W
CONV_W = 4
GDN_CHUNK = 64
SWA_H = 16
SWA_KV = 2
SWA_G = SWA_H // SWA_KV
SWA_DH = 64
SWA_QW = SWA_H * SWA_DH
SWA_KVW = SWA_KV * SWA_DH
WINDOW = 128
N_BUCKETS = 32
MAX_DISTANCE = 128
N_MEM = 256
MEM_H = 4
MEM_DH = 256
MEM_W = MEM_H * MEM_DH
NORM_EPS = 1e-6

LANE = 128
SUBLANE = 8
VMEM_LIMIT = 52 * 1024 * 1024


def _cparams(sem):
    return pltpu.CompilerParams(dimension_semantics=sem, vmem_limit_bytes=VMEM_LIMIT)


def _sigmoid(x):
    return 1.0 / (1.0 + jnp.exp(-x))


def _silu(x):
    return x * _sigmoid(x)


def _softplus(x):
    return jnp.maximum(x, 0.0) + jnp.log(1.0 + jnp.exp(-jnp.abs(x)))


def _dot(a, b):
    return jnp.dot(a.astype(BF16), b.astype(BF16), preferred_element_type=F32)


def _dot_nt(a, b):
    return lax.dot_general(a.astype(BF16), b.astype(BF16), (((1,), (1,)), ((), ())),
                           preferred_element_type=F32)


def _dot_tn(a, b):
    return lax.dot_general(a.astype(BF16), b.astype(BF16), (((0,), (0,)), ((), ())),
                           preferred_element_type=F32)


def _dot_f32(a, b):
    return jnp.dot(a, b, preferred_element_type=F32, precision=lax.Precision.HIGHEST)


def _rmsnorm_kernel(x_ref, nw_ref, o_ref):
    x = x_ref[...].astype(F32)
    ms = jnp.mean(x * x, axis=-1, keepdims=True)
    o_ref[...] = (x * lax.rsqrt(ms + NORM_EPS) * nw_ref[...]).astype(o_ref.dtype)


def _rmsnorm(x, norm_w, tm_pref=512):
    m, d = x.shape
    tm = min(m, tm_pref)
    assert m % tm == 0
    return pl.pallas_call(
        _rmsnorm_kernel,
        out_shape=jax.ShapeDtypeStruct((m, d), BF16),
        grid=(m // tm,),
        in_specs=[pl.BlockSpec((tm, d), lambda i: (i, 0)),
                  pl.BlockSpec((1, d), lambda i: (0, 0))],
        out_specs=pl.BlockSpec((tm, d), lambda i: (i, 0)),
        compiler_params=_cparams(("parallel",)),
        name="rmsnorm",
    )(x, norm_w)


def _proj_kernel(x_ref, w_ref, o_ref, *, act):
    y = jnp.dot(x_ref[...], w_ref[...], preferred_element_type=F32)
    if act == "sigmoid":
        y = _sigmoid(y)
    o_ref[...] = y.astype(o_ref.dtype)


def _pick_tile(n, pref):
    t = min(n, pref)
    while n % t:
        t -= LANE
    return t


def _proj(xn, w, out_dtype, act=None, col0=0, n=None, tm_pref=1024, tn_pref=2048):
    m, d = xn.shape
    n = w.shape[1] if n is None else n
    tm = min(m, tm_pref)
    assert m % tm == 0
    tn = _pick_tile(n, tn_pref if out_dtype == BF16 else tn_pref // 2)
    while col0 % tn:
        tn = _pick_tile(n, tn - LANE)
    jb = col0 // tn
    return pl.pallas_call(
        functools.partial(_proj_kernel, act=act),
        out_shape=jax.ShapeDtypeStruct((m, n), out_dtype),
        grid=(m // tm, n // tn),
        in_specs=[pl.BlockSpec((tm, d), lambda i, j: (i, 0)),
                  pl.BlockSpec((d, tn), lambda i, j: (0, jb + j))],
        out_specs=pl.BlockSpec((tm, tn), lambda i, j: (i, j)),
        compiler_params=_cparams(("parallel", "arbitrary")),
        name="proj",
    )(xn, w)


W_BLK = 1024
W_PIECE_ROWS = (W_BLK, 2 * SUBLANE)


def _wprep_kernel(wt_hbm, o_ref, inbuf0, inbuf1, sem, *, blocks):
    j = pl.program_id(0)
    inbuf = (inbuf0, inbuf1)

    def pieces(b):
        return [pltpu.make_async_copy(wt_hbm.at[pl.ds(start, rows), :],
                                      inbuf[k].at[b % 2, pl.ds(0, rows), :], sem.at[b % 2, k])
                for k, (start, rows) in enumerate(blocks[b])]

    for b, ranges in enumerate(blocks):
        @pl.when(j == b)
        def _(b=b, ranges=ranges):
            if b == 0:
                for cp in pieces(0):
                    cp.start()
            if b + 1 < len(blocks):
                for cp in pieces(b + 1):
                    cp.start()
            for cp in pieces(b):
                cp.wait()
            parts = [inbuf[k][b % 2, 0:rows, :] for k, (_, rows) in enumerate(ranges)]
            n_have = sum(rows for _, rows in ranges)
            if n_have < W_BLK:
                parts.append(jnp.zeros((W_BLK - n_have, parts[0].shape[1]), parts[0].dtype))
            val = parts[0] if len(parts) == 1 else jnp.concatenate(parts, axis=0)
            o_ref[...] = val.T.astype(BF16)


def _wprep(wt, block_srcs):
    ncol, d = wt.shape
    for ranges in block_srcs:
        assert len(ranges) <= len(W_PIECE_ROWS)
        for (s, rows), cap in zip(ranges, W_PIECE_ROWS):
            assert s % SUBLANE == 0 and rows % SUBLANE == 0 and rows <= cap and s + rows <= ncol
    return pl.pallas_call(
        functools.partial(_wprep_kernel, blocks=tuple(tuple(r) for r in block_srcs)),
        out_shape=jax.ShapeDtypeStruct((d, W_BLK * len(block_srcs)), BF16),
        grid=(len(block_srcs),),
        in_specs=[pl.BlockSpec(memory_space=pl.ANY)],
        out_specs=pl.BlockSpec((d, W_BLK), lambda j: (0, j)),
        scratch_shapes=[pltpu.VMEM((2, cap, d), wt.dtype) for cap in W_PIECE_ROWS]
                       + [pltpu.SemaphoreType.DMA((2, len(W_PIECE_ROWS)))],
        compiler_params=_cparams(("arbitrary",)),
        name="wprep",
    )(wt)


HDR = SUBLANE


def _tri_inv_many(a_list, c):
    row = lax.broadcasted_iota(jnp.int32, (c, c), 0)
    col = lax.broadcasted_iota(jnp.int32, (c, c), 1)
    eye = jnp.where(row == col, 1.0, 0.0).astype(F32)
    xs = [eye - a for a in a_list]
    bs = [_dot(a, a) for a in a_list]
    n = 2
    while n < c:
        xs = [x + _dot(x, b) for x, b in zip(xs, bs)]
        n *= 2
        if n < c:
            bs = [_dot(b, b) for b in bs]
    return xs


def _gdn_kernel(qkv_ref, z_ref, ba_ref, hdr_ref, s0_ref, cw_ref, arow_ref, dtrow_ref, nw_ref,
                o_ref, s_ref, buf_ref, cv_ref, gb_ref, oacc_ref, sol_ref, att_ref, gl_ref,
                *, chunk, n_chunks, group, unroll, per_chunk_state, valid_lo, valid_hi):
    c = chunk
    tb = c * n_chunks

    if per_chunk_state:
        buf_ref[0:2 * HDR, :] = jnp.zeros((2 * HDR, GDN_CONV_CH), F32)
    else:
        @pl.when(pl.program_id(1) == 0)
        def _():
            buf_ref[0:HDR, :] = hdr_ref[0]
            buf_ref[HDR:2 * HDR, :] = jnp.zeros((HDR, GDN_CONV_CH), F32)
            s_ref[...] = s0_ref[...]

    def tap_from_history(j, n_rows):
        off = HDR - (CONV_W - 1) + j
        return buf_ref[off:off + n_rows, :] * cw_ref[j:j + 1, :]

    if qkv_ref.dtype == BF16:
        xb = qkv_ref[...]
        r = lax.broadcasted_iota(jnp.int32, (tb, tb), 0)
        cc = lax.broadcasted_iota(jnp.int32, (tb, tb), 1)
        acc = xb.astype(F32) * cw_ref[CONV_W - 1:CONV_W, :]
        for j in range(CONV_W - 1):
            shift = jnp.where(r - cc == CONV_W - 1 - j, 1.0, 0.0).astype(BF16)
            acc = acc + jnp.dot(shift, xb, preferred_element_type=F32) * cw_ref[j:j + 1, :]
        cv_ref[...] = _silu(acc)
        top = acc[0:HDR]
        for j in range(CONV_W - 1):
            top = top + tap_from_history(j, HDR)
        cv_ref[0:HDR, :] = _silu(top)
        buf_ref[0:HDR, :] = qkv_ref[tb - 2 * HDR:tb, :].astype(F32)[HDR:]
    else:
        buf_ref[HDR:HDR + tb, :] = qkv_ref[...].astype(F32)
        acc = None
        for j in range(CONV_W):
            term = tap_from_history(j, tb)
            acc = term if acc is None else acc + term
        cv_ref[...] = _silu(acc)
        if not per_chunk_state:
            buf_ref[0:HDR, :] = buf_ref[tb:tb + HDR, :]

    ba = ba_ref[...].astype(F32)
    beta_all = _sigmoid(ba)
    g_all = -jnp.exp(arow_ref[...]) * _softplus(ba + dtrow_ref[...])
    if per_chunk_state:
        r = lax.broadcasted_iota(jnp.int32, (tb, LANE), 0) & (c - 1)
        valid = (r >= valid_lo) & (r < valid_hi)
        beta_all = jnp.where(valid, beta_all, 0.0)
        g_all = jnp.where(valid, g_all, 0.0)
    gb_ref[0] = beta_all
    gb_ref[1] = g_all

    row = lax.broadcasted_iota(jnp.int32, (c, c), 0)
    col = lax.broadcasted_iota(jnp.int32, (c, c), 1)
    causal = row >= col
    strict = row > col
    tril = jnp.where(causal, 1.0, 0.0).astype(F32)
    scale_q = GDN_DK ** -0.5

    heads = range(GDN_H)
    qcol = lambda h: slice(h * GDN_DK, (h + 1) * GDN_DK)
    kcol = lambda h: slice(GDN_QK + h * GDN_DK, GDN_QK + (h + 1) * GDN_DK)
    vcol = lambda h: slice(2 * GDN_QK + h * GDN_DV, 2 * GDN_QK + (h + 1) * GDN_DV)
    ucol = lambda h: slice(h * (GDN_DV + GDN_DK), h * (GDN_DV + GDN_DK) + GDN_DV)
    wcol = lambda h: slice(h * (GDN_DV + GDN_DK) + GDN_DV, (h + 1) * (GDN_DV + GDN_DK))
    if per_chunk_state:
        rv = lax.broadcasted_iota(jnp.int32, (c, 1), 0)
        rvalid = (rv >= valid_lo) & (rv < valid_hi)

    def chunk_rows(ci):
        if isinstance(ci, int):
            return slice(ci * c, (ci + 1) * c)
        return pl.ds(pl.multiple_of(ci * c, c), c)

    def prep_body(gi, carry):
        probs = []
        for j in range(group):
            rows = chunk_rows(gi * group + j)
            beta_c = gb_ref[0, rows, :]
            gc_all = _dot_f32(tril, gb_ref[1, rows, :])
            gc_t = gc_all.T
            gl_ref[gi * group + j] = gc_all[c - 1:c, :]
            for h in heads:
                q = cv_ref[rows, qcol(h)]
                k = cv_ref[rows, kcol(h)]
                v = cv_ref[rows, vcol(h)]
                q = q * lax.rsqrt(jnp.sum(q * q, axis=-1, keepdims=True) + NORM_EPS) * scale_q
                k = k * lax.rsqrt(jnp.sum(k * k, axis=-1, keepdims=True) + NORM_EPS)
                if per_chunk_state:
                    q = jnp.where(rvalid, q, 0.0)
                    k = jnp.where(rvalid, k, 0.0)
                    v = jnp.where(rvalid, v, 0.0)
                beta = beta_c[:, h:h + 1]
                gc_col = gc_all[:, GDN_H + h:GDN_H + h + 1]
                gc_row = gc_t[GDN_H + h:GDN_H + h + 1, :]
                diff = jnp.where(causal, gc_col - gc_row, 0.0)
                decay = jnp.where(causal, jnp.exp(diff), 0.0)
                e_gc = jnp.exp(gc_col)
                kb = k * beta
                rhs = jnp.concatenate([v * beta, kb * e_gc], axis=1)
                cv_ref[rows, qcol(h)] = q * e_gc
                cv_ref[rows, kcol(h)] = k * jnp.exp(gc_col[c - 1:c, :] - gc_col)
                probs.append((rows, h, q, k, kb, rhs, decay))
        kq = [_dot_nt(jnp.concatenate([kb, q], axis=0), k) for (_, _, q, k, kb, _, _) in probs]
        a_list = [jnp.where(strict, kq_i[:c] * p[6], 0.0) for kq_i, p in zip(kq, probs)]
        t_inv = _tri_inv_many(a_list, c)
        for t_i, kq_i, (rows, h, _, _, _, rhs, decay) in zip(t_inv, kq, probs):
            sol_ref[rows, h * (GDN_DV + GDN_DK):(h + 1) * (GDN_DV + GDN_DK)] = _dot(t_i, rhs)
            att_ref[h, rows, :] = kq_i[c:] * decay
        return carry

    def scan_body(ci, carry):
        rows = chunk_rows(ci)
        si = ci if per_chunk_state else 0
        g_tot = jnp.exp(gl_ref[ci])
        s_old = [s_ref[si, h] for h in heads]
        wq_s = [_dot(jnp.concatenate([sol_ref[rows, wcol(h)], cv_ref[rows, qcol(h)]], axis=0), s_old[h])
                for h in heads]
        v_new = [sol_ref[rows, ucol(h)] - wq_s[h][:c] for h in heads]
        o_att = [_dot(att_ref[h, rows, :], v_new[h]) for h in heads]
        for h in heads:
            s_ref[si, h] = (s_old[h] * g_tot[:, GDN_H + h:GDN_H + h + 1]
                            + _dot_tn(cv_ref[rows, kcol(h)], v_new[h]))
        for h in heads:
            o = wq_s[h][c:] + o_att[h]
            o = o * lax.rsqrt(jnp.mean(o * o, axis=-1, keepdims=True) + NORM_EPS) * nw_ref[...]
            oacc_ref[rows, h * GDN_DV:(h + 1) * GDN_DV] = o
        return carry

    if per_chunk_state:
        s_ref[...] = s0_ref[...]
    if unroll:
        for gi in range(n_chunks // group):
            prep_body(gi, 0)
        for ci in range(n_chunks):
            scan_body(ci, 0)
    else:
        lax.fori_loop(0, n_chunks // group, prep_body, 0)
        lax.fori_loop(0, n_chunks, scan_body, 0)
    o_ref[...] = (oacc_ref[...] * _silu(z_ref[...].astype(F32))).astype(o_ref.dtype)


def _gdn(qkv, z, ba, hdr, s0, conv_w, a_row, dt_row, norm_w, *, n_seq, chunk, n_chunks,
         per_chunk_state, valid_lo, valid_hi, out_dtype, z_colblock=0, ba_colblock=0, group=2,
         unroll=False):
    rows = qkv.shape[0]
    tb = chunk * n_chunks
    if per_chunk_state:
        grid = (rows // tb,)
        sem = ("arbitrary",)
        rmap = lambda i: (i, 0)
        zmap = lambda i: (i, z_colblock)
        bmap = lambda i: (i, ba_colblock)
        hmap = lambda i: (0, 0, 0)
        smap = lambda i: (i, 0, 0, 0)
        cmap = lambda i: (0, 0)
        ns = n_chunks
    else:
        steps = rows // n_seq // tb
        grid = (n_seq, steps)
        sem = ("parallel", "arbitrary")
        rmap = lambda b, n: (b * steps + n, 0)
        zmap = lambda b, n: (b * steps + n, z_colblock)
        bmap = lambda b, n: (b * steps + n, ba_colblock)
        hmap = lambda b, n: (b, 0, 0)
        smap = lambda b, n: (b, 0, 0, 0)
        cmap = lambda b, n: (0, 0)
        ns = 1
    assert n_chunks % group == 0
    kern = functools.partial(_gdn_kernel, chunk=chunk, n_chunks=n_chunks, group=group, unroll=unroll,
                             per_chunk_state=per_chunk_state, valid_lo=valid_lo, valid_hi=valid_hi)
    return pl.pallas_call(
        kern,
        out_shape=(jax.ShapeDtypeStruct((rows, GDN_W), out_dtype),
                   jax.ShapeDtypeStruct(s0.shape, F32)),
        grid=grid,
        in_specs=[pl.BlockSpec((tb, GDN_CONV_CH), rmap),
                  pl.BlockSpec((tb, GDN_W), zmap),
                  pl.BlockSpec((tb, LANE), bmap),
                  pl.BlockSpec((1, HDR, GDN_CONV_CH), hmap),
                  pl.BlockSpec((ns, GDN_H, GDN_DK, GDN_DV), smap),
                  pl.BlockSpec((CONV_W, GDN_CONV_CH), cmap),
                  pl.BlockSpec((1, LANE), cmap),
                  pl.BlockSpec((1, LANE), cmap),
                  pl.BlockSpec((1, GDN_DV), cmap)],
        out_specs=(pl.BlockSpec((tb, GDN_W), rmap),
                   pl.BlockSpec((ns, GDN_H, GDN_DK, GDN_DV), smap)),
        scratch_shapes=[pltpu.VMEM((HDR + tb, GDN_CONV_CH), F32),
                        pltpu.VMEM((tb, GDN_CONV_CH), F32),
                        pltpu.VMEM((2, tb, LANE), F32),
                        pltpu.VMEM((tb, GDN_W), F32),
                        pltpu.VMEM((tb, GDN_H * (GDN_DV + GDN_DK)), F32),
                        pltpu.VMEM((GDN_H, tb, chunk), F32),
                        pltpu.VMEM((n_chunks, 1, LANE), F32)],
        compiler_params=_cparams(sem),
        name="gdn",
    )(qkv, z, ba, hdr, s0, conv_w, a_row, dt_row, norm_w)


def _t5_bucket_np(dist):
    n = np.maximum(dist, 0)
    max_exact = N_BUCKETS // 2
    nf = np.maximum(n, 1).astype(np.float32)
    large = max_exact + (np.log(nf / np.float32(max_exact)) / np.float32(math.log(MAX_DISTANCE / max_exact))
                         * np.float32(N_BUCKETS - max_exact)).astype(np.int32)
    large = np.minimum(large, N_BUCKETS - 1)
    return np.where(n < max_exact, n, large).astype(np.int32)


def _bias_prompt_kernel(code_ref, tab_ref, o_ref):
    h = pl.program_id(1)
    code = code_ref[0]
    acc = jnp.full(code.shape, -jnp.inf, F32)
    for b in range(N_BUCKETS):
        acc = jnp.where(code == b, tab_ref[b, h], acc)
    o_ref[0, 0] = acc


def _bias_prompt(table):
    qi = np.arange(WINDOW)[:, None]
    sj = np.arange(WINDOW)[None, :]
    own = sj <= qi
    bucket = _t5_bucket_np(np.where(own, qi - sj, qi + WINDOW - sj))
    code_first = np.where(own, bucket, -1)
    code = jnp.asarray(np.stack([code_first, bucket]).astype(np.int32))
    return pl.pallas_call(
        _bias_prompt_kernel,
        out_shape=jax.ShapeDtypeStruct((2, SWA_H, WINDOW, WINDOW), F32),
        grid=(2, SWA_H),
        in_specs=[pl.BlockSpec((1, WINDOW, WINDOW), lambda v, h: (v, 0, 0)),
                  pl.BlockSpec(memory_space=pltpu.SMEM)],
        out_specs=pl.BlockSpec((1, 1, WINDOW, WINDOW), lambda v, h: (v, h, 0, 0)),
        compiler_params=_cparams(("arbitrary", "arbitrary")),
        name="swa_bias_prompt",
    )(code, table)


def _bias_sample_kernel(code_ref, tab_ref, o_ref):
    kv = pl.program_id(0)
    code = code_ref[...]
    acc = jnp.full(code.shape, -jnp.inf, F32)
    for g in range(SWA_G):
        for b in range(N_BUCKETS):
            acc = jnp.where(code == b + N_BUCKETS * g, tab_ref[b, kv * SWA_G + g], acc)
    o_ref[0] = acc


def _bias_sample(table, n_tok, n_cache, n_keys_pad):
    dist = (n_cache + np.arange(n_tok))[:, None] - np.arange(n_keys_pad)[None, :]
    valid = (dist >= 0) & (dist < WINDOW) & (np.arange(n_keys_pad)[None, :] < n_cache + n_tok)
    bucket = _t5_bucket_np(dist)
    code_t = np.where(valid, bucket, -1)
    g = np.arange(SWA_G)[:, None, None]
    code = np.where(code_t[None] >= 0, code_t[None] + N_BUCKETS * g, -1)
    code = jnp.asarray(code.reshape(SWA_G * n_tok, n_keys_pad).astype(np.int32))
    return pl.pallas_call(
        _bias_sample_kernel,
        out_shape=jax.ShapeDtypeStruct((SWA_KV, SWA_G * n_tok, n_keys_pad), F32),
        grid=(SWA_KV,),
        in_specs=[pl.BlockSpec((SWA_G * n_tok, n_keys_pad), lambda k: (0, 0)),
                  pl.BlockSpec(memory_space=pltpu.SMEM)],
        out_specs=pl.BlockSpec((1, SWA_G * n_tok, n_keys_pad), lambda k: (k, 0, 0)),
        compiler_params=_cparams(("arbitrary",)),
        name="swa_bias_sample",
    )(code, table)


def _sink_softmax_pv(logits, sink, v):
    m = jnp.maximum(jnp.max(logits, axis=-1, keepdims=True), sink)
    p = jnp.exp(logits - m)
    den = jnp.sum(p, axis=-1, keepdims=True) + jnp.exp(sink - m)
    return _dot(p, v) / den


def _swa_prompt_kernel(q_ref, z_ref, kc_ref, kp_ref, vc_ref, vp_ref, bias_ref, sink_ref, o_ref):
    kk = jnp.concatenate([kp_ref[...], kc_ref[...]], axis=0).astype(BF16)
    vv = jnp.concatenate([vp_ref[...], vc_ref[...]], axis=0).astype(BF16)
    scale = SWA_DH ** -0.5
    row = lax.broadcasted_iota(jnp.int32, (WINDOW, WINDOW), 0)
    col = lax.broadcasted_iota(jnp.int32, (WINDOW, WINDOW), 1)
    from_prev = col > row
    for kv in range(SWA_KV):
        ks = slice(kv * SWA_DH, (kv + 1) * SWA_DH)
        heads = range(kv * SWA_G, (kv + 1) * SWA_G)
        cs = lambda h: slice(h * SWA_DH, (h + 1) * SWA_DH)
        lg = [_dot_nt(q_ref[:, cs(h)] * scale, kk[:, ks]) for h in heads]
        ps, dens = [], []
        for h, l in zip(heads, lg):
            l = jnp.where(from_prev, l[:, :WINDOW], l[:, WINDOW:]) + bias_ref[0, h]
            sink = sink_ref[h]
            m = jnp.maximum(jnp.max(l, axis=-1, keepdims=True), sink)
            p = jnp.exp(l - m)
            dens.append(jnp.sum(p, axis=-1, keepdims=True) + jnp.exp(sink - m))
            ps.append(jnp.concatenate([jnp.where(from_prev, p, 0.0), jnp.where(from_prev, 0.0, p)],
                                      axis=1))
        outs = [_dot(p, vv[:, ks]) / den for p, den in zip(ps, dens)]
        for j in range(0, SWA_G, 2):
            two = slice(heads[j] * SWA_DH, (heads[j] + 2) * SWA_DH)
            o2 = jnp.concatenate([outs[j], outs[j + 1]], axis=1)
            o_ref[:, two] = (o2 * _silu(z_ref[:, two].astype(F32))).astype(o_ref.dtype)


def _swa_prompt(proj, kv, bias, sinks, n_seq, seq_len, q_colblock, z_colblock, k_colblock, v_colblock):
    nb = seq_len // WINDOW
    cur = lambda b, n: b * nb + n
    prev = lambda b, n: b * nb + jnp.maximum(n - 1, 0)
    return pl.pallas_call(
        _swa_prompt_kernel,
        out_shape=jax.ShapeDtypeStruct((n_seq * seq_len, SWA_QW), BF16),
        grid=(n_seq, nb),
        in_specs=[pl.BlockSpec((WINDOW, SWA_QW), lambda b, n: (cur(b, n), q_colblock)),
                  pl.BlockSpec((WINDOW, SWA_QW), lambda b, n: (cur(b, n), z_colblock)),
                  pl.BlockSpec((WINDOW, SWA_KVW), lambda b, n: (cur(b, n), k_colblock)),
                  pl.BlockSpec((WINDOW, SWA_KVW), lambda b, n: (prev(b, n), k_colblock)),
                  pl.BlockSpec((WINDOW, SWA_KVW), lambda b, n: (cur(b, n), v_colblock)),
                  pl.BlockSpec((WINDOW, SWA_KVW), lambda b, n: (prev(b, n), v_colblock)),
                  pl.BlockSpec((1, SWA_H, WINDOW, WINDOW), lambda b, n: (jnp.minimum(n, 1), 0, 0, 0)),
                  pl.BlockSpec(memory_space=pltpu.SMEM)],
        out_specs=pl.BlockSpec((WINDOW, SWA_QW), lambda b, n: (cur(b, n), 0)),
        compiler_params=_cparams(("parallel", "arbitrary")),
        name="swa_prompt",
    )(proj, proj, kv, kv, kv, kv, bias, sinks)


def _swa_sample_kernel(q_ref, z_ref, kn_ref, vn_ref, ck_ref, cv_ref, bias_ref, sink_ref, o_ref, *, n_seq_blk):
    scale = SWA_DH ** -0.5
    probs = [(s, kv) for s in range(n_seq_blk) for kv in range(SWA_KV)]
    ks = lambda kv: slice(kv * SWA_DH, (kv + 1) * SWA_DH)
    kk = [jnp.concatenate([ck_ref[s], kn_ref[s]], axis=0).astype(BF16) for s in range(n_seq_blk)]
    vv = [jnp.concatenate([cv_ref[s], vn_ref[s]], axis=0).astype(BF16) for s in range(n_seq_blk)]
    logits = [_dot_nt(q_ref[s, kv], kk[s][:, ks(kv)]) * scale + bias_ref[kv] for s, kv in probs]
    outs = [_sink_softmax_pv(l, sink_ref[kv][:, 0:1], vv[s][:, ks(kv)]) for l, (s, kv) in zip(logits, probs)]
    for oh, (s, kv) in zip(outs, probs):
        o_ref[s, kv] = (oh * _silu(z_ref[s, kv].astype(F32))).astype(o_ref.dtype)


def _swa_sample(q, z, k_new, v_new, cache_k, cache_v, bias, sink_rows, n_seq_blk=8):
    bd, _, rows, _ = q.shape
    wb = cache_k.shape[1]
    npad = k_new.shape[1]
    blk4 = lambda i: (i, 0, 0, 0)
    blk3 = lambda i: (i, 0, 0)
    return pl.pallas_call(
        functools.partial(_swa_sample_kernel, n_seq_blk=n_seq_blk),
        out_shape=jax.ShapeDtypeStruct(q.shape, BF16),
        grid=(bd // n_seq_blk,),
        in_specs=[pl.BlockSpec((n_seq_blk, SWA_KV, rows, SWA_DH), blk4),
                  pl.BlockSpec((n_seq_blk, SWA_KV, rows, SWA_DH), blk4),
                  pl.BlockSpec((n_seq_blk, npad, SWA_KVW), blk3),
                  pl.BlockSpec((n_seq_blk, npad, SWA_KVW), blk3),
                  pl.BlockSpec((n_seq_blk, wb, SWA_KVW), blk3),
                  pl.BlockSpec((n_seq_blk, wb, SWA_KVW), blk3),
                  pl.BlockSpec((SWA_KV, rows, wb + npad), lambda i: (0, 0, 0)),
                  pl.BlockSpec((SWA_KV, rows, LANE), lambda i: (0, 0, 0))],
        out_specs=pl.BlockSpec((n_seq_blk, SWA_KV, rows, SWA_DH), blk4),
        compiler_params=_cparams(("arbitrary",)),
        name="swa_sample",
    )(q, z, k_new, v_new, cache_k, cache_v, bias, sink_rows)


def _mem_attend(probs, q_of, z_of, k_of, v_of, store):
    scale = MEM_DH ** -0.5
    logits = [_dot_nt(q_of(p) * scale, k_of(p)) for p in probs]
    ps, dens = [], []
    for l in logits:
        m = jnp.max(l, axis=-1, keepdims=True)
        e = jnp.exp(l - m)
        dens.append(jnp.sum(e, axis=-1, keepdims=True))
        ps.append(e)
    outs = [_dot(e, v_of(p)) / den for e, den, p in zip(ps, dens, probs)]
    for p, oh in zip(probs, outs):
        store(p, oh * _silu(z_of(p).astype(F32)))


def _mem_cols(h):
    return slice(h * MEM_DH, (h + 1) * MEM_DH)


def _mem_prompt_kernel(q_ref, z_ref, k_ref, v_ref, o_ref):
    def store(h, val):
        o_ref[:, _mem_cols(h)] = val.astype(o_ref.dtype)
    _mem_attend(range(MEM_H), lambda h: q_ref[:, _mem_cols(h)], lambda h: z_ref[:, _mem_cols(h)],
                lambda h: k_ref[0, :, _mem_cols(h)], lambda h: v_ref[0, :, _mem_cols(h)], store)


def _mem_prompt(proj, mk, mv, n_seq, seq_len, q_colblock, z_colblock, tq=512):
    steps = seq_len // tq
    return pl.pallas_call(
        _mem_prompt_kernel,
        out_shape=jax.ShapeDtypeStruct((n_seq * seq_len, MEM_W), BF16),
        grid=(n_seq, steps),
        in_specs=[pl.BlockSpec((tq, MEM_W), lambda b, n: (b * steps + n, q_colblock)),
                  pl.BlockSpec((tq, MEM_W), lambda b, n: (b * steps + n, z_colblock)),
                  pl.BlockSpec((1, N_MEM, MEM_W), lambda b, n: (b, 0, 0)),
                  pl.BlockSpec((1, N_MEM, MEM_W), lambda b, n: (b, 0, 0))],
        out_specs=pl.BlockSpec((tq, MEM_W), lambda b, n: (b * steps + n, 0)),
        compiler_params=_cparams(("parallel", "arbitrary")),
        name="mem_prompt",
    )(proj, proj, mk, mv)


def _mem_sample_kernel(q_ref, z_ref, k_hbm, v_hbm, o_ref, kbuf, vbuf, sem, *, n_seq_blk):
    i = pl.program_id(0)
    n_steps = pl.num_programs(0)
    slot = i % 2

    def copies(step, slot_):
        seqs = pl.ds(step * n_seq_blk, n_seq_blk)
        out = []
        for h in range(MEM_H):
            out.append(pltpu.make_async_copy(k_hbm.at[seqs, :, h, :], kbuf.at[slot_, h], sem.at[0, slot_, h]))
            out.append(pltpu.make_async_copy(v_hbm.at[seqs, :, h, :], vbuf.at[slot_, h], sem.at[1, slot_, h]))
        return out

    @pl.when(i == 0)
    def _():
        for cp in copies(0, 0):
            cp.start()

    @pl.when(i + 1 < n_steps)
    def _():
        for cp in copies(i + 1, 1 - slot):
            cp.start()

    for cp in copies(i, slot):
        cp.wait()

    def store(p, val):
        o_ref[p[0], :, _mem_cols(p[1])] = val.astype(o_ref.dtype)
    probs = [(s, h) for s in range(n_seq_blk) for h in range(MEM_H)]
    _mem_attend(probs, lambda p: q_ref[p[0], :, _mem_cols(p[1])], lambda p: z_ref[p[0], :, _mem_cols(p[1])],
                lambda p: kbuf[slot, p[1], p[0]], lambda p: vbuf[slot, p[1], p[0]], store)


def _mem_sample(q, z, cache_k, cache_v, n_seq_blk=4):
    bd, rows, _ = q.shape
    blk = lambda i: (i, 0, 0)
    buf = pltpu.VMEM((2, MEM_H, n_seq_blk, N_MEM, MEM_DH), cache_k.dtype)
    return pl.pallas_call(
        functools.partial(_mem_sample_kernel, n_seq_blk=n_seq_blk),
        out_shape=jax.ShapeDtypeStruct(q.shape, BF16),
        grid=(bd // n_seq_blk,),
        in_specs=[pl.BlockSpec((n_seq_blk, rows, MEM_W), blk),
                  pl.BlockSpec((n_seq_blk, rows, MEM_W), blk),
                  pl.BlockSpec(memory_space=pl.ANY),
                  pl.BlockSpec(memory_space=pl.ANY)],
        out_specs=pl.BlockSpec((n_seq_blk, rows, MEM_W), blk),
        scratch_shapes=[buf, buf, pltpu.SemaphoreType.DMA((2, 2, MEM_H))],
        compiler_params=_cparams(("arbitrary",)),
        name="mem_sample",
    )(q, z, cache_k, cache_v)


def _merge_kernel(og_ref, os_ref, om_ref, gate_ref, x_ref, wb_ref, wo_ref, nf_ref, y_ref):
    merged = None
    for b, o_ref in enumerate((og_ref, os_ref, om_ref)):
        t = jnp.dot(o_ref[...], wb_ref[b], preferred_element_type=F32)
        t = t * gate_ref[:, b * D_MODEL:(b + 1) * D_MODEL].astype(F32)
        merged = t if merged is None else merged + t
    h = x_ref[...] + jnp.dot(merged.astype(BF16), wo_ref[...], preferred_element_type=F32)
    ms = jnp.mean(h * h, axis=-1, keepdims=True)
    y_ref[...] = h * lax.rsqrt(ms + NORM_EPS) * nf_ref[...]


def _merge(o_gdn, o_swa, o_mem, gates, x, w_branch, w_out, norm_f, tm=256):
    m = x.shape[0]
    tm = min(tm, m)
    row = lambda i: (i, 0)
    const2 = lambda i: (0, 0)
    return pl.pallas_call(
        _merge_kernel,
        out_shape=jax.ShapeDtypeStruct((m, D_MODEL), F32),
        grid=(m // tm,),
        in_specs=[pl.BlockSpec((tm, BR_W), row),
                  pl.BlockSpec((tm, BR_W), row),
                  pl.BlockSpec((tm, BR_W), row),
                  pl.BlockSpec((tm, N_BRANCH * D_MODEL), row),
                  pl.BlockSpec((tm, D_MODEL), row),
                  pl.BlockSpec((N_BRANCH, BR_W, D_MODEL), lambda i: (0, 0, 0),
                               pipeline_mode=pl.Buffered(1)),
                  pl.BlockSpec((D_MODEL, D_MODEL), const2, pipeline_mode=pl.Buffered(1)),
                  pl.BlockSpec((1, D_MODEL), const2)],
        out_specs=pl.BlockSpec((tm, D_MODEL), row),
        compiler_params=_cparams(("parallel",)),
        name="merge",
    )(o_gdn, o_swa, o_mem, gates, x, w_branch, w_out, norm_f)


_IN_SIZES = (GDN_QK, GDN_QK, GDN_W, GDN_W, GDN_H, GDN_H, SWA_QW, SWA_KVW, SWA_KVW, SWA_QW,
             MEM_W, MEM_W, N_BRANCH * D_MODEL)
_IN_NAMES = ("gq", "gk", "gv", "gz", "gb", "ga", "sq", "sk", "sv", "sz", "mq", "mz", "mg")
_IN_SPAN = {name: (int(off), int(off + size)) for name, off, size in
            zip(_IN_NAMES, np.cumsum((0,) + _IN_SIZES[:-1]), _IN_SIZES)}


def kernel(x_prompt, x_sample, state_gdn, state_gdn_conv, cache_swa_k, cache_swa_v, cache_mem_k,
           cache_mem_v, mem_prompt, norm_in, w_in, gdn_conv_w, gdn_a_log, gdn_dt_bias, gdn_norm,
           swa_sinks, rel_bias, norm_mem, w_mem_kv, w_branch, w_out, norm_f):
    n_layers = norm_in.shape[0]
    assert n_layers == 1
    b, seq, _ = x_prompt.shape
    bd, ns, _ = x_sample.shape
    wb = cache_swa_k.shape[2]
    assert seq % WINDOW == 0 and seq % GDN_CHUNK == 0 and ns + CONV_W <= SUBLANE and wb == WINDOW
    lyr = 0

    w = w_in[lyr]
    main_names = ("gq", "gk", "gv", "gz", "sq", "sz", "mq", "mz")
    n_main = len(main_names) * W_BLK
    n_gate = N_BRANCH * D_MODEL
    n_small = 3 * LANE
    w_all = _wprep(w.T, [[(_IN_SPAN[a][0], W_BLK)] for a in main_names]
                   + [[(_IN_SPAN["mg"][0] + W_BLK * k, W_BLK)] for k in range(n_gate // W_BLK)]
                   + [[(_IN_SPAN["sk"][0], 2 * SWA_KVW), (_IN_SPAN["gb"][0], 2 * GDN_H)]])
    cb_gz, cb_sq, cb_sz, cb_mq, cb_mz = 3, 4, 5, 6, 7
    cb_sk, cb_sv, cb_ba = 0, 1, 2
    w_mkv = w_mem_kv[lyr].astype(BF16)
    w_br = w_branch[lyr].astype(BF16)
    w_o = w_out[lyr].astype(BF16)
    nw_in = norm_in[lyr].reshape(1, D_MODEL)
    nw_mem = norm_mem[lyr].reshape(1, D_MODEL)
    nw_f = norm_f.reshape(1, D_MODEL)
    conv_w = gdn_conv_w[lyr]
    a_row = jnp.pad(gdn_a_log[lyr].reshape(1, GDN_H), ((0, 0), (GDN_H, LANE - 2 * GDN_H)))
    dt_row = jnp.pad(gdn_dt_bias[lyr].reshape(1, GDN_H), ((0, 0), (GDN_H, LANE - 2 * GDN_H)))
    gnw = gdn_norm[lyr].reshape(1, GDN_DV)
    sinks = swa_sinks[lyr]
    bias_p = _bias_prompt(rel_bias)
    npad = SUBLANE
    bias_s = _bias_sample(rel_bias, ns, wb, wb + npad)
    sink_rows = jnp.broadcast_to(jnp.repeat(sinks.reshape(SWA_KV, SWA_G), ns, axis=1)[:, :, None],
                                 (SWA_KV, SWA_G * ns, LANE))

    t = b * seq
    xp = x_prompt.reshape(t, D_MODEL)
    xn_p = _rmsnorm(xp, nw_in)
    p_main = _proj(xn_p, w_all, BF16, n=n_main)
    p_small = _proj(xn_p, w_all, F32, col0=n_main + n_gate, n=n_small)
    g_p = _proj(xn_p, w_all, BF16, act="sigmoid", col0=n_main, n=n_gate)
    xn_tail = xn_p.reshape(b, seq, D_MODEL)[:, seq - SUBLANE:, :].reshape(b * SUBLANE, D_MODEL)
    conv_p = _proj(xn_tail, w_all, F32, n=GDN_CONV_CH).reshape(b, SUBLANE, GDN_CONV_CH)[:, SUBLANE - (CONV_W - 1):]
    kv_tail = p_small.reshape(b, seq, 3 * LANE)[:, seq - WINDOW:]
    swk_p = kv_tail[:, :, cb_sk * LANE:(cb_sk + 1) * LANE].reshape(b, WINDOW, SWA_KV, SWA_DH)
    swv_p = kv_tail[:, :, cb_sv * LANE:(cb_sv + 1) * LANE].reshape(b, WINDOW, SWA_KV, SWA_DH)

    mkv = _proj(_rmsnorm(mem_prompt.reshape(b * N_MEM, D_MODEL), nw_mem), w_mkv, F32)
    mk_p = mkv[:, :MEM_W].reshape(b, N_MEM, MEM_W)
    mv_p = mkv[:, MEM_W:].reshape(b, N_MEM, MEM_W)

    o_gdn_p, s_p = _gdn(p_main, p_main, p_small, jnp.zeros((b, HDR, GDN_CONV_CH), F32),
                        jnp.zeros((b, GDN_H, GDN_DK, GDN_DV), F32), conv_w, a_row, dt_row, gnw,
                        n_seq=b, chunk=GDN_CHUNK, n_chunks=4, per_chunk_state=False, group=4, unroll=True,
                        valid_lo=0, valid_hi=GDN_CHUNK, out_dtype=BF16,
                        z_colblock=cb_gz, ba_colblock=cb_ba)
    o_swa_p = _swa_prompt(p_main, p_small, bias_p, sinks, b, seq, cb_sq, cb_sz, cb_sk, cb_sv)
    o_mem_p = _mem_prompt(p_main, mk_p.astype(BF16), mv_p.astype(BF16), b, seq, cb_mq, cb_mz)
    y_p = _merge(o_gdn_p, o_swa_p, o_mem_p, g_p, xp, w_br, w_o, nw_f).reshape(b, seq, D_MODEL)

    ts = bd * ns
    xs = x_sample.reshape(ts, D_MODEL)
    xn_s = _rmsnorm(xs, nw_in)
    s_main = _proj(xn_s, w_all, F32, n=n_main)
    s_small = _proj(xn_s, w_all, F32, col0=n_main + n_gate, n=n_small)
    g_s = _proj(xn_s, w_all, BF16, act="sigmoid", col0=n_main, n=n_gate)
    s_gdn = s_main[:, :GDN_CONV_CH + GDN_W]
    s_ba = s_small[:, cb_ba * LANE:(cb_ba + 1) * LANE]
    s_swa = jnp.concatenate([s_main[:, cb_sq * BR_W:(cb_sz + 1) * BR_W], s_small[:, :2 * LANE]], axis=1)
    s_mem = s_main[:, cb_mq * BR_W:(cb_mz + 1) * BR_W].astype(BF16)

    lo = CONV_W - 1
    hi = lo + ns
    pad_rows = ((0, 0), (lo, SUBLANE - hi), (0, 0))
    e_qkv = jnp.concatenate([state_gdn_conv[lyr], s_gdn[:, :GDN_CONV_CH].reshape(bd, ns, GDN_CONV_CH),
                             jnp.zeros((bd, SUBLANE - hi, GDN_CONV_CH), F32)], axis=1)
    e_z = jnp.pad(s_gdn[:, GDN_CONV_CH:].reshape(bd, ns, GDN_W), pad_rows)
    e_ba = jnp.pad(s_ba.reshape(bd, ns, LANE), pad_rows)
    seq_blk = 8
    o_gdn_s8, s_s = _gdn(e_qkv.reshape(bd * SUBLANE, GDN_CONV_CH), e_z.reshape(bd * SUBLANE, GDN_W),
                         e_ba.reshape(bd * SUBLANE, LANE), jnp.zeros((1, HDR, GDN_CONV_CH), F32),
                         state_gdn[lyr], conv_w, a_row, dt_row, gnw,
                         n_seq=bd, chunk=SUBLANE, n_chunks=seq_blk, per_chunk_state=True, group=8,
                         unroll=True,
                         valid_lo=lo, valid_hi=hi, out_dtype=BF16)
    o_gdn_s = o_gdn_s8.reshape(bd, SUBLANE, GDN_W)[:, lo:hi].reshape(ts, GDN_W)
    conv_s = e_qkv[:, hi - (CONV_W - 1):hi]

    def to_heads(a):
        return a.reshape(bd, ns, SWA_KV, SWA_G, SWA_DH).transpose(0, 2, 3, 1, 4).reshape(
            bd, SWA_KV, SWA_G * ns, SWA_DH)

    k_new = s_swa[:, 2 * SWA_QW:2 * SWA_QW + SWA_KVW].reshape(bd, ns, SWA_KVW)
    v_new = s_swa[:, 2 * SWA_QW + SWA_KVW:].reshape(bd, ns, SWA_KVW)
    tok_pad = ((0, 0), (0, npad - ns), (0, 0))
    ck = cache_swa_k[lyr].reshape(bd, wb, SWA_KVW)
    cv = cache_swa_v[lyr].reshape(bd, wb, SWA_KVW)
    o_swa_h = _swa_sample(to_heads(s_swa[:, :SWA_QW]).astype(BF16), to_heads(s_swa[:, SWA_QW:2 * SWA_QW]),
                          jnp.pad(k_new, tok_pad), jnp.pad(v_new, tok_pad), ck, cv, bias_s, sink_rows)
    o_swa_s = o_swa_h.reshape(bd, SWA_KV, SWA_G, ns, SWA_DH).transpose(0, 3, 1, 2, 4).reshape(ts, SWA_QW)
    swk_s = jnp.concatenate([ck, k_new], axis=1)[:, ns:].reshape(bd, wb, SWA_KV, SWA_DH)
    swv_s = jnp.concatenate([cv, v_new], axis=1)[:, ns:].reshape(bd, wb, SWA_KV, SWA_DH)

    mq = jnp.pad(s_mem[:, :MEM_W].reshape(bd, ns, MEM_W), tok_pad)
    mz = jnp.pad(s_mem[:, MEM_W:].reshape(bd, ns, MEM_W), tok_pad)
    o_mem_s = _mem_sample(mq, mz, cache_mem_k[lyr], cache_mem_v[lyr])[:, :ns].reshape(ts, MEM_W)
    y_s = _merge(o_gdn_s, o_swa_s, o_mem_s, g_s, xs, w_br, w_o, nw_f).reshape(bd, ns, D_MODEL)

    return (y_p, y_s,
            s_p[None], conv_p[None], swk_p[None], swv_p[None],
            mk_p.reshape(b, N_MEM, MEM_H, MEM_DH)[None], mv_p.reshape(b, N_MEM, MEM_H, MEM_DH)[None],
            s_s[None], conv_s[None], swk_s[None], swv_s[None])
```

```python
import functools
import math

import numpy as np
import jax
import jax.numpy as jnp
from jax import lax
from jax.experimental import pallas as pl
from jax.experimental.pallas import tpu as pltpu

F32 = jnp.float32
BF16 = jnp.bfloat16

D_MODEL = 2048
N_BRANCH = 3
BR_W = 1024
GDN_H = 8
GDN_DK = 128
GDN_DV = 128
GDN_QK = GDN_H * GDN_DK
GDN_W = GDN_H * GDN_DV
GDN_CONV_CH = 2 * GDN_QK + GDN_W
CONV_W = 4
GDN_CHUNK = 64
SWA_H = 16
SWA_KV = 2
SWA_G = SWA_H // SWA_KV
SWA_DH = 64
SWA_QW = SWA_H * SWA_DH
SWA_KVW = SWA_KV * SWA_DH
WINDOW = 128
N_BUCKETS = 32
MAX_DISTANCE = 128
N_MEM = 256
MEM_H = 4
MEM_DH = 256
MEM_W = MEM_H * MEM_DH
NORM_EPS = 1e-6

LANE = 128
SUBLANE = 8
VMEM_LIMIT = 52 * 1024 * 1024


def _cparams(sem):
    return pltpu.CompilerParams(dimension_semantics=sem, vmem_limit_bytes=VMEM_LIMIT)


def _sigmoid(x):
    return 0.5 * jnp.tanh(0.5 * x) + 0.5


def _silu(x):
    h = 0.5 * x
    return h * jnp.tanh(h) + h


def _softplus(x):
    return jnp.maximum(x, 0.0) + jnp.log(1.0 + jnp.exp(-jnp.abs(x)))


def _dot(a, b):
    return jnp.dot(a.astype(BF16), b.astype(BF16), preferred_element_type=F32)


def _dot_nt(a, b):
    return lax.dot_general(a.astype(BF16), b.astype(BF16), (((1,), (1,)), ((), ())),
                           preferred_element_type=F32)


def _dot_tn(a, b):
    return lax.dot_general(a.astype(BF16), b.astype(BF16), (((0,), (0,)), ((), ())),
                           preferred_element_type=F32)


def _dot_f32(a, b):
    return jnp.dot(a, b, preferred_element_type=F32, precision=lax.Precision.HIGHEST)


def _rmsnorm_kernel(x_ref, nw_ref, o_ref):
    x = x_ref[...].astype(F32)
    ms = jnp.mean(x * x, axis=-1, keepdims=True)
    o_ref[...] = (x * lax.rsqrt(ms + NORM_EPS) * nw_ref[...]).astype(o_ref.dtype)


def _rmsnorm(x, norm_w, tm_pref=512):
    m, d = x.shape
    tm = min(m, tm_pref)
    assert m % tm == 0
    return pl.pallas_call(
        _rmsnorm_kernel,
        out_shape=jax.ShapeDtypeStruct((m, d), BF16),
        grid=(m // tm,),
        in_specs=[pl.BlockSpec((tm, d), lambda i: (i, 0)),
                  pl.BlockSpec((1, d), lambda i: (0, 0))],
        out_specs=pl.BlockSpec((tm, d), lambda i: (i, 0)),
        compiler_params=_cparams(("parallel",)),
        name="rmsnorm",
    )(x, norm_w)


def _proj_kernel(x_ref, w_ref, o_ref, *, act):
    y = jnp.dot(x_ref[...], w_ref[...], preferred_element_type=F32)
    if act == "sigmoid":
        y = _sigmoid(y)
    o_ref[...] = y.astype(o_ref.dtype)


def _pick_tile(n, pref):
    t = min(n, pref)
    while n % t:
        t -= LANE
    return t


def _proj(xn, w, out_dtype, act=None, col0=0, n=None, tm_pref=1024, tn_pref=2048):
    m, d = xn.shape
    n = w.shape[1] if n is None else n
    tm = min(m, tm_pref)
    assert m % tm == 0
    tn = _pick_tile(n, tn_pref if out_dtype == BF16 else tn_pref // 2)
    while col0 % tn:
        tn = _pick_tile(n, tn - LANE)
    jb = col0 // tn
    return pl.pallas_call(
        functools.partial(_proj_kernel, act=act),
        out_shape=jax.ShapeDtypeStruct((m, n), out_dtype),
        grid=(m // tm, n // tn),
        in_specs=[pl.BlockSpec((tm, d), lambda i, j: (i, 0)),
                  pl.BlockSpec((d, tn), lambda i, j: (0, jb + j))],
        out_specs=pl.BlockSpec((tm, tn), lambda i, j: (i, j)),
        compiler_params=_cparams(("parallel", "arbitrary")),
        name="proj",
    )(xn, w)


W_BLK = 1024
W_PIECE_ROWS = (W_BLK, 2 * SUBLANE)


def _wprep_kernel(wt_hbm, o_ref, inbuf0, inbuf1, sem, *, blocks):
    j = pl.program_id(0)
    inbuf = (inbuf0, inbuf1)

    def sources(b):
        return [(start, rows) for start, rows in blocks[b] if start is not None]

    def pieces(b):
        return [pltpu.make_async_copy(wt_hbm.at[pl.ds(start, rows), :],
                                      inbuf[k].at[b % 2, pl.ds(0, rows), :], sem.at[b % 2, k])
                for k, (start, rows) in enumerate(sources(b))]

    for b, ranges in enumerate(blocks):
        @pl.when(j == b)
        def _(b=b, ranges=ranges):
            if b == 0:
                for cp in pieces(0):
                    cp.start()
            if b + 1 < len(blocks):
                for cp in pieces(b + 1):
                    cp.start()
            for cp in pieces(b):
                cp.wait()
            n_have = sum(rows for _, rows in ranges)
            parts, k = [], 0
            for start, rows in tuple(ranges) + ((None, W_BLK - n_have),):
                if start is None:
                    if rows:
                        parts.append(jnp.zeros((rows, inbuf0.shape[2]), inbuf0.dtype))
                else:
                    parts.append(inbuf[k][b % 2, 0:rows, :])
                    k += 1
            val = parts[0] if len(parts) == 1 else jnp.concatenate(parts, axis=0)
            o_ref[...] = val.T.astype(BF16)


def _wprep(wt, block_srcs):
    ncol, d = wt.shape
    for ranges in block_srcs:
        srcs = [(s, rows) for s, rows in ranges if s is not None]
        assert len(srcs) <= len(W_PIECE_ROWS) and all(rows % SUBLANE == 0 for _, rows in ranges)
        for (s, rows), cap in zip(srcs, W_PIECE_ROWS):
            assert s % SUBLANE == 0 and rows <= cap and s + rows <= ncol
    return pl.pallas_call(
        functools.partial(_wprep_kernel, blocks=tuple(tuple(r) for r in block_srcs)),
        out_shape=jax.ShapeDtypeStruct((d, W_BLK * len(block_srcs)), BF16),
        grid=(len(block_srcs),),
        in_specs=[pl.BlockSpec(memory_space=pl.ANY)],
        out_specs=pl.BlockSpec((d, W_BLK), lambda j: (0, j)),
        scratch_shapes=[pltpu.VMEM((2, cap, d), wt.dtype) for cap in W_PIECE_ROWS]
                       + [pltpu.SemaphoreType.DMA((2, len(W_PIECE_ROWS)))],
        compiler_params=_cparams(("arbitrary",)),
        name="wprep",
    )(wt)


HDR = SUBLANE


def _tri_inv_many(a_list, c):
    row = lax.broadcasted_iota(jnp.int32, (c, c), 0)
    col = lax.broadcasted_iota(jnp.int32, (c, c), 1)
    eye = jnp.where(row == col, 1.0, 0.0).astype(F32)
    xs = [eye - a for a in a_list]
    bs = [_dot(a, a) for a in a_list]
    n = 2
    while n < c:
        xs = [x + _dot(x, b) for x, b in zip(xs, bs)]
        n *= 2
        if n < c:
            bs = [_dot(b, b) for b in bs]
    return xs


def _gdn_kernel(qkv_ref, z_ref, ba_ref, hdr_ref, s0_ref, cw_ref, arow_ref, dtrow_ref, nw_ref,
                o_ref, s_ref, buf_ref, cv_ref, gb_ref, oacc_ref, sol_ref, att_ref, gl_ref,
                *, chunk, n_chunks, group, unroll, per_chunk_state, valid_lo, valid_hi):
    c = chunk
    tb = c * n_chunks

    if per_chunk_state:
        buf_ref[0:2 * HDR, :] = jnp.zeros((2 * HDR, GDN_CONV_CH), F32)
    else:
        @pl.when(pl.program_id(1) == 0)
        def _():
            buf_ref[0:HDR, :] = hdr_ref[0]
            buf_ref[HDR:2 * HDR, :] = jnp.zeros((HDR, GDN_CONV_CH), F32)
            s_ref[...] = s0_ref[...]

    def tap_from_history(j, n_rows):
        off = HDR - (CONV_W - 1) + j
        return buf_ref[off:off + n_rows, :] * cw_ref[j:j + 1, :]

    if qkv_ref.dtype == BF16:
        xb = qkv_ref[...]
        r = lax.broadcasted_iota(jnp.int32, (tb, tb), 0)
        cc = lax.broadcasted_iota(jnp.int32, (tb, tb), 1)
        acc = xb.astype(F32) * cw_ref[CONV_W - 1:CONV_W, :]
        for j in range(CONV_W - 1):
            shift = jnp.where(r - cc == CONV_W - 1 - j, 1.0, 0.0).astype(BF16)
            acc = acc + jnp.dot(shift, xb, preferred_element_type=F32) * cw_ref[j:j + 1, :]
        cv_ref[...] = _silu(acc)
        top = acc[0:HDR]
        for j in range(CONV_W - 1):
            top = top + tap_from_history(j, HDR)
        cv_ref[0:HDR, :] = _silu(top)
        buf_ref[0:HDR, :] = qkv_ref[tb - 2 * HDR:tb, :].astype(F32)[HDR:]
    else:
        buf_ref[HDR:HDR + tb, :] = qkv_ref[...].astype(F32)
        acc = None
        for j in range(CONV_W):
            term = tap_from_history(j, tb)
            acc = term if acc is None else acc + term
        cv_ref[...] = _silu(acc)
        if not per_chunk_state:
            buf_ref[0:HDR, :] = buf_ref[tb:tb + HDR, :]

    ba = ba_ref[...].astype(F32)
    beta_all = _sigmoid(ba)
    g_all = -jnp.exp(arow_ref[...]) * _softplus(ba + dtrow_ref[...])
    if per_chunk_state:
        r = lax.broadcasted_iota(jnp.int32, (tb, LANE), 0) & (c - 1)
        valid = (r >= valid_lo) & (r < valid_hi)
        beta_all = jnp.where(valid, beta_all, 0.0)
        g_all = jnp.where(valid, g_all, 0.0)
    gb_ref[0] = beta_all
    gb_ref[1] = g_all

    row = lax.broadcasted_iota(jnp.int32, (c, c), 0)
    col = lax.broadcasted_iota(jnp.int32, (c, c), 1)
    causal = row >= col
    strict = row > col
    tril = jnp.where(causal, 1.0, 0.0).astype(F32)
    scale_q = GDN_DK ** -0.5

    heads = range(GDN_H)
    qcol = lambda h: slice(h * GDN_DK, (h + 1) * GDN_DK)
    kcol = lambda h: slice(GDN_QK + h * GDN_DK, GDN_QK + (h + 1) * GDN_DK)
    vcol = lambda h: slice(2 * GDN_QK + h * GDN_DV, 2 * GDN_QK + (h + 1) * GDN_DV)
    ucol = lambda h: slice(h * (GDN_DV + GDN_DK), h * (GDN_DV + GDN_DK) + GDN_DV)
    wcol = lambda h: slice(h * (GDN_DV + GDN_DK) + GDN_DV, (h + 1) * (GDN_DV + GDN_DK))
    if per_chunk_state:
        rv = lax.broadcasted_iota(jnp.int32, (c, 1), 0)
        rvalid = (rv >= valid_lo) & (rv < valid_hi)

    def chunk_rows(ci):
        if isinstance(ci, int):
            return slice(ci * c, (ci + 1) * c)
        return pl.ds(pl.multiple_of(ci * c, c), c)

    def prep_body(gi, carry):
        probs = []
        for j in range(group):
            rows = chunk_rows(gi * group + j)
            beta_c = gb_ref[0, rows, :]
            gc_all = _dot_f32(tril, gb_ref[1, rows, :])
            gc_t = gc_all.T
            gl_ref[gi * group + j] = gc_all[c - 1:c, :]
            for h in heads:
                q = cv_ref[rows, qcol(h)]
                k = cv_ref[rows, kcol(h)]
                v = cv_ref[rows, vcol(h)]
                q = q * lax.rsqrt(jnp.sum(q * q, axis=-1, keepdims=True) + NORM_EPS) * scale_q
                k = k * lax.rsqrt(jnp.sum(k * k, axis=-1, keepdims=True) + NORM_EPS)
                if per_chunk_state:
                    q = jnp.where(rvalid, q, 0.0)
                    k = jnp.where(rvalid, k, 0.0)
                    v = jnp.where(rvalid, v, 0.0)
                beta = beta_c[:, h:h + 1]
                gc_col = gc_all[:, GDN_H + h:GDN_H + h + 1]
                gc_row = gc_t[GDN_H + h:GDN_H + h + 1, :]
                diff = jnp.where(causal, gc_col - gc_row, 0.0)
                decay = jnp.where(causal, jnp.exp(diff), 0.0)
                e_gc = jnp.exp(gc_col)
                kb = k * beta
                rhs = jnp.concatenate([v * beta, kb * e_gc], axis=1)
                cv_ref[rows, qcol(h)] = q * e_gc
                cv_ref[rows, kcol(h)] = k * jnp.exp(gc_col[c - 1:c, :] - gc_col)
                probs.append((rows, h, q, k, kb, rhs, decay))
        kq = [_dot_nt(jnp.concatenate([kb, q], axis=0), k) for (_, _, q, k, kb, _, _) in probs]
        a_list = [jnp.where(strict, kq_i[:c] * p[6], 0.0) for kq_i, p in zip(kq, probs)]
        t_inv = _tri_inv_many(a_list, c)
        for t_i, kq_i, (rows, h, _, _, _, rhs, decay) in zip(t_inv, kq, probs):
            sol_ref[rows, h * (GDN_DV + GDN_DK):(h + 1) * (GDN_DV + GDN_DK)] = _dot(t_i, rhs)
            att_ref[h, rows, :] = kq_i[c:] * decay
        return carry

    def scan_body(ci, carry):
        rows = chunk_rows(ci)
        si = ci if per_chunk_state else 0
        g_tot = jnp.exp(gl_ref[ci])
        s_old = [s_ref[si, h] for h in heads]
        wq_s = [_dot(jnp.concatenate([sol_ref[rows, wcol(h)], cv_ref[rows, qcol(h)]], axis=0), s_old[h])
                for h in heads]
        v_new = [sol_ref[rows, ucol(h)] - wq_s[h][:c] for h in heads]
        o_att = [_dot(att_ref[h, rows, :], v_new[h]) for h in heads]
        for h in heads:
            s_ref[si, h] = (s_old[h] * g_tot[:, GDN_H + h:GDN_H + h + 1]
                            + _dot_tn(cv_ref[rows, kcol(h)], v_new[h]))
        for h in heads:
            o = wq_s[h][c:] + o_att[h]
            o = o * lax.rsqrt(jnp.mean(o * o, axis=-1, keepdims=True) + NORM_EPS) * nw_ref[...]
            oacc_ref[rows, h * GDN_DV:(h + 1) * GDN_DV] = o
        return carry

    if per_chunk_state:
        s_ref[...] = s0_ref[...]
    if unroll:
        for gi in range(n_chunks // group):
            prep_body(gi, 0)
        for ci in range(n_chunks):
            scan_body(ci, 0)
    else:
        lax.fori_loop(0, n_chunks // group, prep_body, 0)
        lax.fori_loop(0, n_chunks, scan_body, 0)
    o_ref[...] = (oacc_ref[...] * _silu(z_ref[...].astype(F32))).astype(o_ref.dtype)


def _gdn(qkv, z, ba, hdr, s0, conv_w, a_row, dt_row, norm_w, *, n_seq, chunk, n_chunks,
         per_chunk_state, valid_lo, valid_hi, out_dtype, z_colblock=0, ba_colblock=0, group=2,
         unroll=False):
    rows = qkv.shape[0]
    tb = chunk * n_chunks
    if per_chunk_state:
        grid = (rows // tb,)
        sem = ("arbitrary",)
        rmap = lambda i: (i, 0)
        zmap = lambda i: (i, z_colblock)
        bmap = lambda i: (i, ba_colblock)
        hmap = lambda i: (0, 0, 0)
        smap = lambda i: (i, 0, 0, 0)
        cmap = lambda i: (0, 0)
        ns = n_chunks
    else:
        steps = rows // n_seq // tb
        grid = (n_seq, steps)
        sem = ("parallel", "arbitrary")
        rmap = lambda b, n: (b * steps + n, 0)
        zmap = lambda b, n: (b * steps + n, z_colblock)
        bmap = lambda b, n: (b * steps + n, ba_colblock)
        hmap = lambda b, n: (b, 0, 0)
        smap = lambda b, n: (b, 0, 0, 0)
        cmap = lambda b, n: (0, 0)
        ns = 1
    assert n_chunks % group == 0
    kern = functools.partial(_gdn_kernel, chunk=chunk, n_chunks=n_chunks, group=group, unroll=unroll,
                             per_chunk_state=per_chunk_state, valid_lo=valid_lo, valid_hi=valid_hi)
    return pl.pallas_call(
        kern,
        out_shape=(jax.ShapeDtypeStruct((rows, GDN_W), out_dtype),
                   jax.ShapeDtypeStruct(s0.shape, F32)),
        grid=grid,
        in_specs=[pl.BlockSpec((tb, GDN_CONV_CH), rmap),
                  pl.BlockSpec((tb, GDN_W), zmap),
                  pl.BlockSpec((tb, LANE), bmap),
                  pl.BlockSpec((1, HDR, GDN_CONV_CH), hmap),
                  pl.BlockSpec((ns, GDN_H, GDN_DK, GDN_DV), smap),
                  pl.BlockSpec((CONV_W, GDN_CONV_CH), cmap),
                  pl.BlockSpec((1, LANE), cmap),
                  pl.BlockSpec((1, LANE), cmap),
                  pl.BlockSpec((1, GDN_DV), cmap)],
        out_specs=(pl.BlockSpec((tb, GDN_W), rmap),
                   pl.BlockSpec((ns, GDN_H, GDN_DK, GDN_DV), smap)),
        scratch_shapes=[pltpu.VMEM((HDR + tb, GDN_CONV_CH), F32),
                        pltpu.VMEM((tb, GDN_CONV_CH), F32),
                        pltpu.VMEM((2, tb, LANE), F32),
                        pltpu.VMEM((tb, GDN_W), F32),
                        pltpu.VMEM((tb, GDN_H * (GDN_DV + GDN_DK)), F32),
                        pltpu.VMEM((GDN_H, tb, chunk), F32),
                        pltpu.VMEM((n_chunks, 1, LANE), F32)],
        compiler_params=_cparams(sem),
        name="gdn",
    )(qkv, z, ba, hdr, s0, conv_w, a_row, dt_row, norm_w)


def _t5_bucket_np(dist):
    n = np.maximum(dist, 0)
    max_exact = N_BUCKETS // 2
    nf = np.maximum(n, 1).astype(np.float32)
    large = max_exact + (np.log(nf / np.float32(max_exact)) / np.float32(math.log(MAX_DISTANCE / max_exact))
                         * np.float32(N_BUCKETS - max_exact)).astype(np.int32)
    large = np.minimum(large, N_BUCKETS - 1)
    return np.where(n < max_exact, n, large).astype(np.int32)


def _bias_prompt_kernel(code_ref, tab_ref, o_ref):
    h = pl.program_id(1)
    code = code_ref[0]
    acc = jnp.full(code.shape, -jnp.inf, F32)
    for b in range(N_BUCKETS):
        acc = jnp.where(code == b, tab_ref[b, h], acc)
    o_ref[0, 0] = acc


def _bias_prompt(table):
    qi = np.arange(WINDOW)[:, None]
    sj = np.arange(WINDOW)[None, :]
    own = sj <= qi
    bucket = _t5_bucket_np(np.where(own, qi - sj, qi + WINDOW - sj))
    code_first = np.where(own, bucket, -1)
    code = jnp.asarray(np.stack([code_first, bucket]).astype(np.int32))
    return pl.pallas_call(
        _bias_prompt_kernel,
        out_shape=jax.ShapeDtypeStruct((2, SWA_H, WINDOW, WINDOW), F32),
        grid=(2, SWA_H),
        in_specs=[pl.BlockSpec((1, WINDOW, WINDOW), lambda v, h: (v, 0, 0)),
                  pl.BlockSpec(memory_space=pltpu.SMEM)],
        out_specs=pl.BlockSpec((1, 1, WINDOW, WINDOW), lambda v, h: (v, h, 0, 0)),
        compiler_params=_cparams(("arbitrary", "arbitrary")),
        name="swa_bias_prompt",
    )(code, table)


def _bias_sample_kernel(code_ref, tab_ref, o_ref):
    kv = pl.program_id(0)
    code = code_ref[...]
    acc = jnp.full(code.shape, -jnp.inf, F32)
    for g in range(SWA_G):
        for b in range(N_BUCKETS):
            acc = jnp.where(code == b + N_BUCKETS * g, tab_ref[b, kv * SWA_G + g], acc)
    o_ref[0] = acc


def _bias_sample(table, n_tok, n_cache, n_keys_pad):
    dist = (n_cache + np.arange(n_tok))[:, None] - np.arange(n_keys_pad)[None, :]
    valid = (dist >= 0) & (dist < WINDOW) & (np.arange(n_keys_pad)[None, :] < n_cache + n_tok)
    bucket = _t5_bucket_np(dist)
    code_t = np.where(valid, bucket, -1)
    g = np.arange(SWA_G)[:, None, None]
    code = np.where(code_t[None] >= 0, code_t[None] + N_BUCKETS * g, -1)
    code = jnp.asarray(code.reshape(SWA_G * n_tok, n_keys_pad).astype(np.int32))
    return pl.pallas_call(
        _bias_sample_kernel,
        out_shape=jax.ShapeDtypeStruct((SWA_KV, SWA_G * n_tok, n_keys_pad), F32),
        grid=(SWA_KV,),
        in_specs=[pl.BlockSpec((SWA_G * n_tok, n_keys_pad), lambda k: (0, 0)),
                  pl.BlockSpec(memory_space=pltpu.SMEM)],
        out_specs=pl.BlockSpec((1, SWA_G * n_tok, n_keys_pad), lambda k: (k, 0, 0)),
        compiler_params=_cparams(("arbitrary",)),
        name="swa_bias_sample",
    )(code, table)


def _sink_softmax_pv(logits, sink, v):
    m = jnp.maximum(jnp.max(logits, axis=-1, keepdims=True), sink)
    p = jnp.exp(logits - m)
    den = jnp.sum(p, axis=-1, keepdims=True) + jnp.exp(sink - m)
    return _dot(p, v) / den


def _swa_prompt_kernel(q_ref, z_ref, kc_ref, kp_ref, vc_ref, vp_ref, bias_ref, sink_ref, o_ref, *, n_qblk):
    kall = jnp.concatenate([kp_ref[...], kc_ref[...]], axis=0).astype(BF16)
    vall = jnp.concatenate([vp_ref[...], vc_ref[...]], axis=0).astype(BF16)
    scale = SWA_DH ** -0.5
    row = lax.broadcasted_iota(jnp.int32, (WINDOW, WINDOW), 0)
    col = lax.broadcasted_iota(jnp.int32, (WINDOW, WINDOW), 1)
    from_prev = col > row
    first_variant = jnp.where(pl.program_id(1) == 0, 0, 1)
    cs = lambda h: slice(h * SWA_DH, (h + 1) * SWA_DH)
    rows = lambda qb: slice(qb * WINDOW, (qb + 1) * WINDOW)
    keys = lambda qb: slice(qb * WINDOW, (qb + 2) * WINDOW)
    for kv in range(SWA_KV):
        ks = slice(kv * SWA_DH, (kv + 1) * SWA_DH)
        probs = [(qb, h) for qb in range(n_qblk) for h in range(kv * SWA_G, (kv + 1) * SWA_G)]
        lg = [_dot_nt(q_ref[rows(qb), cs(h)] * scale, kall[keys(qb), ks]) for qb, h in probs]
        ps, dens = [], []
        for (qb, h), l in zip(probs, lg):
            bias = bias_ref[first_variant if qb == 0 else 1, h]
            l = jnp.where(from_prev, l[:, :WINDOW], l[:, WINDOW:]) + bias
            sink = sink_ref[h]
            m = jnp.maximum(jnp.max(l, axis=-1, keepdims=True), sink)
            p = jnp.exp(l - m)
            dens.append(jnp.sum(p, axis=-1, keepdims=True) + jnp.exp(sink - m))
            ps.append(jnp.concatenate([jnp.where(from_prev, p, 0.0), jnp.where(from_prev, 0.0, p)],
                                      axis=1))
        outs = [_dot(p, vall[keys(qb), ks]) / den for p, den, (qb, _) in zip(ps, dens, probs)]
        for j in range(0, len(probs), 2):
            qb, h = probs[j]
            two = slice(h * SWA_DH, (h + 2) * SWA_DH)
            o2 = jnp.concatenate([outs[j], outs[j + 1]], axis=1)
            o_ref[rows(qb), two] = (o2 * _silu(z_ref[rows(qb), two].astype(F32))).astype(o_ref.dtype)


def _swa_prompt(proj, kv, bias, sinks, n_seq, seq_len, q_colblock, z_colblock, k_colblock, v_colblock,
                n_qblk=1):
    tq = n_qblk * WINDOW
    steps = seq_len // tq
    cur = lambda b, n: b * steps + n
    prev = lambda b, n: (b * steps + n) * n_qblk - jnp.where(n == 0, 0, 1)
    return pl.pallas_call(
        functools.partial(_swa_prompt_kernel, n_qblk=n_qblk),
        out_shape=jax.ShapeDtypeStruct((n_seq * seq_len, SWA_QW), BF16),
        grid=(n_seq, steps),
        in_specs=[pl.BlockSpec((tq, SWA_QW), lambda b, n: (cur(b, n), q_colblock)),
                  pl.BlockSpec((tq, SWA_QW), lambda b, n: (cur(b, n), z_colblock)),
                  pl.BlockSpec((tq, SWA_KVW), lambda b, n: (cur(b, n), k_colblock)),
                  pl.BlockSpec((WINDOW, SWA_KVW), lambda b, n: (prev(b, n), k_colblock)),
                  pl.BlockSpec((tq, SWA_KVW), lambda b, n: (cur(b, n), v_colblock)),
                  pl.BlockSpec((WINDOW, SWA_KVW), lambda b, n: (prev(b, n), v_colblock)),
                  pl.BlockSpec((2, SWA_H, WINDOW, WINDOW), lambda b, n: (0, 0, 0, 0)),
                  pl.BlockSpec(memory_space=pltpu.SMEM)],
        out_specs=pl.BlockSpec((tq, SWA_QW), lambda b, n: (cur(b, n), 0)),
        compiler_params=_cparams(("parallel", "arbitrary")),
        name="swa_prompt",
    )(proj, proj, kv, kv, kv, kv, bias, sinks)


def _swa_sample_kernel(q_ref, z_ref, kn_ref, vn_ref, ck_ref, cv_ref, bias_ref, sink_ref, o_ref, *, n_seq_blk):
    scale = SWA_DH ** -0.5
    probs = [(s, kv) for s in range(n_seq_blk) for kv in range(SWA_KV)]
    ks = lambda kv: slice(kv * SWA_DH, (kv + 1) * SWA_DH)
    kk = [jnp.concatenate([ck_ref[s], kn_ref[s]], axis=0).astype(BF16) for s in range(n_seq_blk)]
    vv = [jnp.concatenate([cv_ref[s], vn_ref[s]], axis=0).astype(BF16) for s in range(n_seq_blk)]
    logits = [_dot_nt(q_ref[s, kv], kk[s][:, ks(kv)]) * scale + bias_ref[kv] for s, kv in probs]
    outs = [_sink_softmax_pv(l, sink_ref[kv][:, 0:1], vv[s][:, ks(kv)]) for l, (s, kv) in zip(logits, probs)]
    for oh, (s, kv) in zip(outs, probs):
        o_ref[s, kv] = (oh * _silu(z_ref[s, kv].astype(F32))).astype(o_ref.dtype)


def _swa_sample(q, z, k_new, v_new, cache_k, cache_v, bias, sink_rows, n_seq_blk=8):
    bd, _, rows, _ = q.shape
    wb = cache_k.shape[1]
    npad = k_new.shape[1]
    blk4 = lambda i: (i, 0, 0, 0)
    blk3 = lambda i: (i, 0, 0)
    return pl.pallas_call(
        functools.partial(_swa_sample_kernel, n_seq_blk=n_seq_blk),
        out_shape=jax.ShapeDtypeStruct(q.shape, BF16),
        grid=(bd // n_seq_blk,),
        in_specs=[pl.BlockSpec((n_seq_blk, SWA_KV, rows, SWA_DH), blk4),
                  pl.BlockSpec((n_seq_blk, SWA_KV, rows, SWA_DH), blk4),
                  pl.BlockSpec((n_seq_blk, npad, SWA_KVW), blk3),
                  pl.BlockSpec((n_seq_blk, npad, SWA_KVW), blk3),
                  pl.BlockSpec((n_seq_blk, wb, SWA_KVW), blk3),
                  pl.BlockSpec((n_seq_blk, wb, SWA_KVW), blk3),
                  pl.BlockSpec((SWA_KV, rows, wb + npad), lambda i: (0, 0, 0)),
                  pl.BlockSpec((SWA_KV, rows, LANE), lambda i: (0, 0, 0))],
        out_specs=pl.BlockSpec((n_seq_blk, SWA_KV, rows, SWA_DH), blk4),
        compiler_params=_cparams(("arbitrary",)),
        name="swa_sample",
    )(q, z, k_new, v_new, cache_k, cache_v, bias, sink_rows)


def _mem_attend(probs, q_of, z_of, k_of, v_of, store):
    scale = MEM_DH ** -0.5
    logits = [_dot_nt(q_of(p) * scale, k_of(p)) for p in probs]
    ps, dens = [], []
    for l in logits:
        m = jnp.max(l, axis=-1, keepdims=True)
        e = jnp.exp(l - m)
        dens.append(jnp.sum(e, axis=-1, keepdims=True))
        ps.append(e)
    outs = [_dot(e, v_of(p)) / den for e, den, p in zip(ps, dens, probs)]
    for p, oh in zip(probs, outs):
        store(p, oh * _silu(z_of(p).astype(F32)))


def _mem_cols(h):
    return slice(h * MEM_DH, (h + 1) * MEM_DH)


def _mem_prompt_kernel(q_ref, z_ref, k_ref, v_ref, o_ref):
    def store(h, val):
        o_ref[:, _mem_cols(h)] = val.astype(o_ref.dtype)
    _mem_attend(range(MEM_H), lambda h: q_ref[:, _mem_cols(h)], lambda h: z_ref[:, _mem_cols(h)],
                lambda h: k_ref[0, :, _mem_cols(h)], lambda h: v_ref[0, :, _mem_cols(h)], store)


def _mem_prompt(proj, mk, mv, n_seq, seq_len, q_colblock, z_colblock, tq=512):
    steps = seq_len // tq
    return pl.pallas_call(
        _mem_prompt_kernel,
        out_shape=jax.ShapeDtypeStruct((n_seq * seq_len, MEM_W), BF16),
        grid=(n_seq, steps),
        in_specs=[pl.BlockSpec((tq, MEM_W), lambda b, n: (b * steps + n, q_colblock)),
                  pl.BlockSpec((tq, MEM_W), lambda b, n: (b * steps + n, z_colblock)),
                  pl.BlockSpec((1, N_MEM, MEM_W), lambda b, n: (b, 0, 0)),
                  pl.BlockSpec((1, N_MEM, MEM_W), lambda b, n: (b, 0, 0))],
        out_specs=pl.BlockSpec((tq, MEM_W), lambda b, n: (b * steps + n, 0)),
        compiler_params=_cparams(("parallel", "arbitrary")),
        name="mem_prompt",
    )(proj, proj, mk, mv)


def _mem_sample_kernel(q_ref, z_ref, k_hbm, v_hbm, o_ref, kbuf, vbuf, sem, *, n_seq_blk):
    i = pl.program_id(0)
    n_steps = pl.num_programs(0)
    slot = i % 2

    def copies(step, slot_):
        seqs = pl.ds(step * n_seq_blk, n_seq_blk)
        out = []
        for h in range(MEM_H):
            out.append(pltpu.make_async_copy(k_hbm.at[seqs, :, h, :], kbuf.at[slot_, h], sem.at[0, slot_, h]))
            out.append(pltpu.make_async_copy(v_hbm.at[seqs, :, h, :], vbuf.at[slot_, h], sem.at[1, slot_, h]))
        return out

    @pl.when(i == 0)
    def _():
        for cp in copies(0, 0):
            cp.start()

    @pl.when(i + 1 < n_steps)
    def _():
        for cp in copies(i + 1, 1 - slot):
            cp.start()

    for cp in copies(i, slot):
        cp.wait()

    def store(p, val):
        o_ref[p[0], :, _mem_cols(p[1])] = val.astype(o_ref.dtype)
    probs = [(s, h) for s in range(n_seq_blk) for h in range(MEM_H)]
    _mem_attend(probs, lambda p: q_ref[p[0], :, _mem_cols(p[1])], lambda p: z_ref[p[0], :, _mem_cols(p[1])],
                lambda p: kbuf[slot, p[1], p[0]], lambda p: vbuf[slot, p[1], p[0]], store)


def _mem_sample(q, z, cache_k, cache_v, n_seq_blk=4):
    bd, rows, _ = q.shape
    blk = lambda i: (i, 0, 0)
    buf = pltpu.VMEM((2, MEM_H, n_seq_blk, N_MEM, MEM_DH), cache_k.dtype)
    return pl.pallas_call(
        functools.partial(_mem_sample_kernel, n_seq_blk=n_seq_blk),
        out_shape=jax.ShapeDtypeStruct(q.shape, BF16),
        grid=(bd // n_seq_blk,),
        in_specs=[pl.BlockSpec((n_seq_blk, rows, MEM_W), blk),
                  pl.BlockSpec((n_seq_blk, rows, MEM_W), blk),
                  pl.BlockSpec(memory_space=pl.ANY),
                  pl.BlockSpec(memory_space=pl.ANY)],
        out_specs=pl.BlockSpec((n_seq_blk, rows, MEM_W), blk),
        scratch_shapes=[buf, buf, pltpu.SemaphoreType.DMA((2, 2, MEM_H))],
        compiler_params=_cparams(("arbitrary",)),
        name="mem_sample",
    )(q, z, cache_k, cache_v)


def _merge_kernel(og_ref, os_ref, om_ref, gate_ref, x_ref, wb_ref, wo_ref, nf_ref, y_ref, *, n_sub):
    tm = x_ref.shape[0]
    sub = tm // n_sub
    for s in range(n_sub):
        rows = slice(s * sub, (s + 1) * sub)
        merged = None
        for b, o_ref in enumerate((og_ref, os_ref, om_ref)):
            t = jnp.dot(o_ref[rows, :], wb_ref[b], preferred_element_type=F32)
            t = t * gate_ref[rows, b * D_MODEL:(b + 1) * D_MODEL].astype(F32)
            merged = t if merged is None else merged + t
        h = x_ref[rows, :] + jnp.dot(merged.astype(BF16), wo_ref[...], preferred_element_type=F32)
        ms = jnp.mean(h * h, axis=-1, keepdims=True)
        y_ref[rows, :] = h * lax.rsqrt(ms + NORM_EPS) * nf_ref[...]


def _merge(o_gdn, o_swa, o_mem, gates, x, w_branch, w_out, norm_f, tm=256):
    m = x.shape[0]
    tm = min(tm, m)
    row = lambda i: (i, 0)
    const2 = lambda i: (0, 0)
    return pl.pallas_call(
        functools.partial(_merge_kernel, n_sub=1),
        out_shape=jax.ShapeDtypeStruct((m, D_MODEL), F32),
        grid=(m // tm,),
        in_specs=[pl.BlockSpec((tm, BR_W), row),
                  pl.BlockSpec((tm, BR_W), row),
                  pl.BlockSpec((tm, BR_W), row),
                  pl.BlockSpec((tm, N_BRANCH * D_MODEL), row),
                  pl.BlockSpec((tm, D_MODEL), row),
                  pl.BlockSpec((N_BRANCH, BR_W, D_MODEL), lambda i: (0, 0, 0),
                               pipeline_mode=pl.Buffered(1)),
                  pl.BlockSpec((D_MODEL, D_MODEL), const2, pipeline_mode=pl.Buffered(1)),
                  pl.BlockSpec((1, D_MODEL), const2)],
        out_specs=pl.BlockSpec((tm, D_MODEL), row),
        compiler_params=_cparams(("parallel",)),
        name="merge",
    )(o_gdn, o_swa, o_mem, gates, x, w_branch, w_out, norm_f)


_IN_SIZES = (GDN_QK, GDN_QK, GDN_W, GDN_W, GDN_H, GDN_H, SWA_QW, SWA_KVW, SWA_KVW, SWA_QW,
             MEM_W, MEM_W, N_BRANCH * D_MODEL)
_IN_NAMES = ("gq", "gk", "gv", "gz", "gb", "ga", "sq", "sk", "sv", "sz", "mq", "mz", "mg")
_IN_SPAN = {name: (int(off), int(off + size)) for name, off, size in
            zip(_IN_NAMES, np.cumsum((0,) + _IN_SIZES[:-1]), _IN_SIZES)}


def kernel(x_prompt, x_sample, state_gdn, state_gdn_conv, cache_swa_k, cache_swa_v, cache_mem_k,
           cache_mem_v, mem_prompt, norm_in, w_in, gdn_conv_w, gdn_a_log, gdn_dt_bias, gdn_norm,
           swa_sinks, rel_bias, norm_mem, w_mem_kv, w_branch, w_out, norm_f):
    n_layers = norm_in.shape[0]
    assert n_layers == 1
    b, seq, _ = x_prompt.shape
    bd, ns, _ = x_sample.shape
    wb = cache_swa_k.shape[2]
    assert seq % WINDOW == 0 and seq % GDN_CHUNK == 0 and ns + CONV_W <= SUBLANE and wb == WINDOW
    lyr = 0

    w = w_in[lyr]
    main_names = ("gq", "gk", "gv", "gz", "sq", "sz", "mq", "mz")
    n_main = len(main_names) * W_BLK
    n_gate = N_BRANCH * D_MODEL
    n_small = 3 * LANE
    small_lead = -(n_main + n_gate) % n_small
    w_all = _wprep(w.T, [[(_IN_SPAN[a][0], W_BLK)] for a in main_names]
                   + [[(_IN_SPAN["mg"][0] + W_BLK * k, W_BLK)] for k in range(n_gate // W_BLK)]
                   + [[(None, small_lead), (_IN_SPAN["sk"][0], 2 * SWA_KVW), (_IN_SPAN["gb"][0], 2 * GDN_H)]])
    col_small = n_main + n_gate + small_lead
    cb_gz, cb_sq, cb_sz, cb_mq, cb_mz = 3, 4, 5, 6, 7
    cb_sk, cb_sv, cb_ba = 0, 1, 2
    w_mkv = w_mem_kv[lyr].astype(BF16)
    w_br = w_branch[lyr].astype(BF16)
    w_o = w_out[lyr].astype(BF16)
    nw_in = norm_in[lyr].reshape(1, D_MODEL)
    nw_mem = norm_mem[lyr].reshape(1, D_MODEL)
    nw_f = norm_f.reshape(1, D_MODEL)
    conv_w = gdn_conv_w[lyr]
    a_row = jnp.pad(gdn_a_log[lyr].reshape(1, GDN_H), ((0, 0), (GDN_H, LANE - 2 * GDN_H)))
    dt_row = jnp.pad(gdn_dt_bias[lyr].reshape(1, GDN_H), ((0, 0), (GDN_H, LANE - 2 * GDN_H)))
    gnw = gdn_norm[lyr].reshape(1, GDN_DV)
    sinks = swa_sinks[lyr]
    bias_p = _bias_prompt(rel_bias)
    npad = SUBLANE
    bias_s = _bias_sample(rel_bias, ns, wb, wb + npad)
    sink_rows = jnp.broadcast_to(jnp.repeat(sinks.reshape(SWA_KV, SWA_G), ns, axis=1)[:, :, None],
                                 (SWA_KV, SWA_G * ns, LANE))

    t = b * seq
    xp = x_prompt.reshape(t, D_MODEL)
    xn_p = _rmsnorm(xp, nw_in)
    p_main = _proj(xn_p, w_all, BF16, n=n_main)
    p_small = _proj(xn_p, w_all, F32, col0=col_small, n=n_small)
    g_p = _proj(xn_p, w_all, BF16, act="sigmoid", col0=n_main, n=n_gate)
    xn_tail = xn_p.reshape(b, seq, D_MODEL)[:, seq - SUBLANE:, :].reshape(b * SUBLANE, D_MODEL)
    conv_p = _proj(xn_tail, w_all, F32, n=GDN_CONV_CH).reshape(b, SUBLANE, GDN_CONV_CH)[:, SUBLANE - (CONV_W - 1):]
    kv_tail = p_small.reshape(b, seq, 3 * LANE)[:, seq - WINDOW:]
    swk_p = kv_tail[:, :, cb_sk * LANE:(cb_sk + 1) * LANE].reshape(b, WINDOW, SWA_KV, SWA_DH)
    swv_p = kv_tail[:, :, cb_sv * LANE:(cb_sv + 1) * LANE].reshape(b, WINDOW, SWA_KV, SWA_DH)

    mkv = _proj(_rmsnorm(mem_prompt.reshape(b * N_MEM, D_MODEL), nw_mem), w_mkv, F32)
    mk_p = mkv[:, :MEM_W].reshape(b, N_MEM, MEM_W)
    mv_p = mkv[:, MEM_W:].reshape(b, N_MEM, MEM_W)

    o_gdn_p, s_p = _gdn(p_main, p_main, p_small, jnp.zeros((b, HDR, GDN_CONV_CH), F32),
                        jnp.zeros((b, GDN_H, GDN_DK, GDN_DV), F32), conv_w, a_row, dt_row, gnw,
                        n_seq=b, chunk=GDN_CHUNK, n_chunks=4, per_chunk_state=False, group=4, unroll=True,
                        valid_lo=0, valid_hi=GDN_CHUNK, out_dtype=BF16,
                        z_colblock=cb_gz, ba_colblock=cb_ba)
    o_swa_p = _swa_prompt(p_main, p_small, bias_p, sinks, b, seq, cb_sq, cb_sz, cb_sk, cb_sv)
    o_mem_p = _mem_prompt(p_main, mk_p.astype(BF16), mv_p.astype(BF16), b, seq, cb_mq, cb_mz)
    y_p = _merge(o_gdn_p, o_swa_p, o_mem_p, g_p, xp, w_br, w_o, nw_f).reshape(b, seq, D_MODEL)

    ts = bd * ns
    xs = x_sample.reshape(ts, D_MODEL)
    xn_s = _rmsnorm(xs, nw_in)
    s_main = _proj(xn_s, w_all, F32, n=n_main)
    s_small = _proj(xn_s, w_all, F32, col0=col_small, n=n_small)
    g_s = _proj(xn_s, w_all, BF16, act="sigmoid", col0=n_main, n=n_gate)
    s_gdn = s_main[:, :GDN_CONV_CH + GDN_W]
    s_ba = s_small[:, cb_ba * LANE:(cb_ba + 1) * LANE]
    s_swa = jnp.concatenate([s_main[:, cb_sq * BR_W:(cb_sz + 1) * BR_W], s_small[:, :2 * LANE]], axis=1)
    s_mem = s_main[:, cb_mq * BR_W:(cb_mz + 1) * BR_W].astype(BF16)

    lo = CONV_W - 1
    hi = lo + ns
    pad_rows = ((0, 0), (lo, SUBLANE - hi), (0, 0))
    e_qkv = jnp.concatenate([state_gdn_conv[lyr], s_gdn[:, :GDN_CONV_CH].reshape(bd, ns, GDN_CONV_CH),
                             jnp.zeros((bd, SUBLANE - hi, GDN_CONV_CH), F32)], axis=1)
    e_z = jnp.pad(s_gdn[:, GDN_CONV_CH:].reshape(bd, ns, GDN_W), pad_rows)
    e_ba = jnp.pad(s_ba.reshape(bd, ns, LANE), pad_rows)
    seq_blk = 8
    o_gdn_s8, s_s = _gdn(e_qkv.reshape(bd * SUBLANE, GDN_CONV_CH), e_z.reshape(bd * SUBLANE, GDN_W),
                         e_ba.reshape(bd * SUBLANE, LANE), jnp.zeros((1, HDR, GDN_CONV_CH), F32),
                         state_gdn[lyr], conv_w, a_row, dt_row, gnw,
                         n_seq=bd, chunk=SUBLANE, n_chunks=seq_blk, per_chunk_state=True, group=8,
                         unroll=True,
                         valid_lo=lo, valid_hi=hi, out_dtype=BF16)
    o_gdn_s = o_gdn_s8.reshape(bd, SUBLANE, GDN_W)[:, lo:hi].reshape(ts, GDN_W)
    conv_s = e_qkv[:, hi - (CONV_W - 1):hi]

    def to_heads(a):
        return a.reshape(bd, ns, SWA_KV, SWA_G, SWA_DH).transpose(0, 2, 3, 1, 4).reshape(
            bd, SWA_KV, SWA_G * ns, SWA_DH)

    k_new = s_swa[:, 2 * SWA_QW:2 * SWA_QW + SWA_KVW].reshape(bd, ns, SWA_KVW)
    v_new = s_swa[:, 2 * SWA_QW + SWA_KVW:].reshape(bd, ns, SWA_KVW)
    tok_pad = ((0, 0), (0, npad - ns), (0, 0))
    ck = cache_swa_k[lyr].reshape(bd, wb, SWA_KVW)
    cv = cache_swa_v[lyr].reshape(bd, wb, SWA_KVW)
    o_swa_h = _swa_sample(to_heads(s_swa[:, :SWA_QW]).astype(BF16), to_heads(s_swa[:, SWA_QW:2 * SWA_QW]),
                          jnp.pad(k_new, tok_pad), jnp.pad(v_new, tok_pad), ck, cv, bias_s, sink_rows)
    o_swa_s = o_swa_h.reshape(bd, SWA_KV, SWA_G, ns, SWA_DH).transpose(0, 3, 1, 2, 4).reshape(ts, SWA_QW)
    swk_s = jnp.concatenate([ck, k_new], axis=1)[:, ns:].reshape(bd, wb, SWA_KV, SWA_DH)
    swv_s = jnp.concatenate([cv, v_new], axis=1)[:, ns:].reshape(bd, wb, SWA_KV, SWA_DH)

    mq = jnp.pad(s_mem[:, :MEM_W].reshape(bd, ns, MEM_W), tok_pad)
    mz = jnp.pad(s_mem[:, MEM_W:].reshape(bd, ns, MEM_W), tok_pad)
    o_mem_s = _mem_sample(mq, mz, cache_mem_k[lyr], cache_mem_v[lyr])[:, :ns].reshape(ts, MEM_W)
    y_s = _merge(o_gdn_s, o_swa_s, o_mem_s, g_s, xs, w_br, w_o, nw_f).reshape(bd, ns, D_MODEL)

    return (y_p, y_s,
            s_p[None], conv_p[None], swk_p[None], swv_p[None],
            mk_p.reshape(b, N_MEM, MEM_H, MEM_DH)[None], mv_p.reshape(b, N_MEM, MEM_H, MEM_DH)[None],
            s_s[None], conv_s[None], swk_s[None], swv_s[None])
```

```python
import functools
import math

import numpy as np
import jax
import jax.numpy as jnp
from jax import lax
from jax.experimental import pallas as pl
from jax.experimental.pallas import tpu as pltpu

F32 = jnp.float32
BF16 = jnp.bfloat16

D_MODEL = 2048
N_BRANCH = 3
BR_W = 1024
GDN_H = 8
GDN_DK = 128
GDN_DV = 128
GDN_QK = GDN_H * GDN_DK
GDN_W = GDN_H * GDN_DV
GDN_CONV_CH = 2 * GDN_QK + GDN_W
CONV_W = 4
GDN_CHUNK = 64
SWA_H = 16
SWA_KV = 2
SWA_G = SWA_H // SWA_KV
SWA_DH = 64
SWA_QW = SWA_H * SWA_DH
SWA_KVW = SWA_KV * SWA_DH
WINDOW = 128
N_BUCKETS = 32
MAX_DISTANCE = 128
N_MEM = 256
MEM_H = 4
MEM_DH = 256
MEM_W = MEM_H * MEM_DH
NORM_EPS = 1e-6

LANE = 128
SUBLANE = 8
VMEM_LIMIT = 52 * 1024 * 1024


def _cparams(sem):
    return pltpu.CompilerParams(dimension_semantics=sem, vmem_limit_bytes=VMEM_LIMIT)


def _sigmoid(x):
    return 0.5 * jnp.tanh(0.5 * x) + 0.5


def _silu(x):
    h = 0.5 * x
    return h * jnp.tanh(h) + h


def _softplus(x):
    return jnp.maximum(x, 0.0) + jnp.log(1.0 + jnp.exp(-jnp.abs(x)))


def _dot(a, b):
    return jnp.dot(a.astype(BF16), b.astype(BF16), preferred_element_type=F32)


def _dot_nt(a, b):
    return lax.dot_general(a.astype(BF16), b.astype(BF16), (((1,), (1,)), ((), ())),
                           preferred_element_type=F32)


def _dot_tn(a, b):
    return lax.dot_general(a.astype(BF16), b.astype(BF16), (((0,), (0,)), ((), ())),
                           preferred_element_type=F32)


def _dot_f32(a, b):
    return jnp.dot(a, b, preferred_element_type=F32, precision=lax.Precision.HIGHEST)


def _rmsnorm_kernel(x_ref, nw_ref, *rest):
    x = x_ref[...].astype(F32)
    ms = jnp.mean(x * x, axis=-1, keepdims=True)
    xn = (x * lax.rsqrt(ms + NORM_EPS) * nw_ref[...]).astype(BF16)
    if len(rest) == 1:
        rest[0][...] = xn
    else:
        w_ref, o_ref, p_ref = rest
        o_ref[...] = xn
        p_ref[...] = jnp.dot(xn, w_ref[...], preferred_element_type=F32)


def _rmsnorm(x, norm_w, w=None, col0=0, n=None, tm_pref=512):
    m, d = x.shape
    tm = min(m, tm_pref)
    assert m % tm == 0
    row = lambda i: (i, 0)
    in_specs = [pl.BlockSpec((tm, d), row), pl.BlockSpec((1, d), lambda i: (0, 0))]
    out_shape = jax.ShapeDtypeStruct((m, d), BF16)
    out_specs = pl.BlockSpec((tm, d), row)
    args = (x, norm_w)
    if w is not None:
        assert col0 % n == 0
        in_specs.append(pl.BlockSpec((d, n), lambda i: (0, col0 // n)))
        out_shape = (out_shape, jax.ShapeDtypeStruct((m, n), F32))
        out_specs = (out_specs, pl.BlockSpec((tm, n), row))
        args = args + (w,)
    return pl.pallas_call(
        _rmsnorm_kernel,
        out_shape=out_shape,
        grid=(m // tm,),
        in_specs=in_specs,
        out_specs=out_specs,
        compiler_params=_cparams(("parallel",)),
        name="rmsnorm",
    )(*args)


def _proj_kernel(x_ref, w_ref, o_ref, *, act):
    y = jnp.dot(x_ref[...], w_ref[...], preferred_element_type=F32)
    if act == "sigmoid":
        y = _sigmoid(y)
    o_ref[...] = y.astype(o_ref.dtype)


def _pick_tile(n, pref):
    t = min(n, pref)
    while n % t:
        t -= LANE
    return t


def _proj(xn, w, out_dtype, act=None, col0=0, n=None, tm_pref=1024, tn_pref=2048):
    m, d = xn.shape
    n = w.shape[1] if n is None else n
    tm = min(m, tm_pref)
    assert m % tm == 0
    tn = _pick_tile(n, tn_pref if out_dtype == BF16 else tn_pref // 2)
    while col0 % tn:
        tn = _pick_tile(n, tn - LANE)
    jb = col0 // tn
    return pl.pallas_call(
        functools.partial(_proj_kernel, act=act),
        out_shape=jax.ShapeDtypeStruct((m, n), out_dtype),
        grid=(m // tm, n // tn),
        in_specs=[pl.BlockSpec((tm, d), lambda i, j: (i, 0)),
                  pl.BlockSpec((d, tn), lambda i, j: (0, jb + j))],
        out_specs=pl.BlockSpec((tm, tn), lambda i, j: (i, j)),
        compiler_params=_cparams(("parallel", "arbitrary")),
        name="proj",
    )(xn, w)


W_BLK = 1024
W_PIECE_ROWS = (W_BLK, 2 * SUBLANE)


def _wprep_kernel(wt_hbm, o_ref, inbuf0, inbuf1, sem, *, blocks):
    j = pl.program_id(0)
    inbuf = (inbuf0, inbuf1)

    def sources(b):
        return [(start, rows) for start, rows in blocks[b] if start is not None]

    def pieces(b):
        return [pltpu.make_async_copy(wt_hbm.at[pl.ds(start, rows), :],
                                      inbuf[k].at[b % 2, pl.ds(0, rows), :], sem.at[b % 2, k])
                for k, (start, rows) in enumerate(sources(b))]

    for b, ranges in enumerate(blocks):
        @pl.when(j == b)
        def _(b=b, ranges=ranges):
            if b == 0:
                for cp in pieces(0):
                    cp.start()
            if b + 1 < len(blocks):
                for cp in pieces(b + 1):
                    cp.start()
            for cp in pieces(b):
                cp.wait()
            n_have = sum(rows for _, rows in ranges)
            parts, k = [], 0
            for start, rows in tuple(ranges) + ((None, W_BLK - n_have),):
                if start is None:
                    if rows:
                        parts.append(jnp.zeros((rows, inbuf0.shape[2]), inbuf0.dtype))
                else:
                    parts.append(inbuf[k][b % 2, 0:rows, :])
                    k += 1
            val = parts[0] if len(parts) == 1 else jnp.concatenate(parts, axis=0)
            o_ref[...] = val.T.astype(BF16)


def _wprep(wt, block_srcs):
    ncol, d = wt.shape
    for ranges in block_srcs:
        srcs = [(s, rows) for s, rows in ranges if s is not None]
        assert len(srcs) <= len(W_PIECE_ROWS) and all(rows % SUBLANE == 0 for _, rows in ranges)
        for (s, rows), cap in zip(srcs, W_PIECE_ROWS):
            assert s % SUBLANE == 0 and rows <= cap and s + rows <= ncol
    return pl.pallas_call(
        functools.partial(_wprep_kernel, blocks=tuple(tuple(r) for r in block_srcs)),
        out_shape=jax.ShapeDtypeStruct((d, W_BLK * len(block_srcs)), BF16),
        grid=(len(block_srcs),),
        in_specs=[pl.BlockSpec(memory_space=pl.ANY)],
        out_specs=pl.BlockSpec((d, W_BLK), lambda j: (0, j)),
        scratch_shapes=[pltpu.VMEM((2, cap, d), wt.dtype) for cap in W_PIECE_ROWS]
                       + [pltpu.SemaphoreType.DMA((2, len(W_PIECE_ROWS)))],
        compiler_params=_cparams(("arbitrary",)),
        name="wprep",
    )(wt)


HDR = SUBLANE


def _tri_inv_many(a_list, c):
    row = lax.broadcasted_iota(jnp.int32, (c, c), 0)
    col = lax.broadcasted_iota(jnp.int32, (c, c), 1)
    eye = jnp.where(row == col, 1.0, 0.0).astype(F32)
    xs = [eye - a for a in a_list]
    bs = [_dot(a, a) for a in a_list]
    n = 2
    while n < c:
        xs = [x + _dot(x, b) for x, b in zip(xs, bs)]
        n *= 2
        if n < c:
            bs = [_dot(b, b) for b in bs]
    return xs


def _gdn_kernel(qkv_ref, z_ref, ba_ref, hdr_ref, s0_ref, cw_ref, arow_ref, dtrow_ref, nw_ref,
                o_ref, s_ref, buf_ref, cv_ref, gb_ref, oacc_ref, sol_ref, att_ref, gl_ref,
                *, chunk, n_chunks, group, unroll, per_chunk_state, valid_lo, valid_hi):
    c = chunk
    tb = c * n_chunks

    if per_chunk_state:
        buf_ref[0:2 * HDR, :] = jnp.zeros((2 * HDR, GDN_CONV_CH), F32)
    else:
        @pl.when(pl.program_id(1) == 0)
        def _():
            buf_ref[0:HDR, :] = hdr_ref[0]
            buf_ref[HDR:2 * HDR, :] = jnp.zeros((HDR, GDN_CONV_CH), F32)
            s_ref[...] = s0_ref[...]

    def tap_from_history(j, n_rows):
        off = HDR - (CONV_W - 1) + j
        return buf_ref[off:off + n_rows, :] * cw_ref[j:j + 1, :]

    if qkv_ref.dtype == BF16:
        xb = qkv_ref[...]
        r = lax.broadcasted_iota(jnp.int32, (tb, tb), 0)
        cc = lax.broadcasted_iota(jnp.int32, (tb, tb), 1)
        acc = xb.astype(F32) * cw_ref[CONV_W - 1:CONV_W, :]
        for j in range(CONV_W - 1):
            shift = jnp.where(r - cc == CONV_W - 1 - j, 1.0, 0.0).astype(BF16)
            acc = acc + jnp.dot(shift, xb, preferred_element_type=F32) * cw_ref[j:j + 1, :]
        cv_ref[...] = _silu(acc)
        top = acc[0:HDR]
        for j in range(CONV_W - 1):
            top = top + tap_from_history(j, HDR)
        cv_ref[0:HDR, :] = _silu(top)
        buf_ref[0:HDR, :] = qkv_ref[tb - 2 * HDR:tb, :].astype(F32)[HDR:]
    else:
        buf_ref[HDR:HDR + tb, :] = qkv_ref[...].astype(F32)
        acc = None
        for j in range(CONV_W):
            term = tap_from_history(j, tb)
            acc = term if acc is None else acc + term
        cv_ref[...] = _silu(acc)
        if not per_chunk_state:
            buf_ref[0:HDR, :] = buf_ref[tb:tb + HDR, :]

    ba = ba_ref[...].astype(F32)
    beta_all = _sigmoid(ba)
    g_all = -jnp.exp(arow_ref[...]) * _softplus(ba + dtrow_ref[...])
    if per_chunk_state:
        r = lax.broadcasted_iota(jnp.int32, (tb, LANE), 0) & (c - 1)
        valid = (r >= valid_lo) & (r < valid_hi)
        beta_all = jnp.where(valid, beta_all, 0.0)
        g_all = jnp.where(valid, g_all, 0.0)
    gb_ref[0] = beta_all
    gb_ref[1] = g_all

    row = lax.broadcasted_iota(jnp.int32, (c, c), 0)
    col = lax.broadcasted_iota(jnp.int32, (c, c), 1)
    causal = row >= col
    strict = row > col
    tril = jnp.where(causal, 1.0, 0.0).astype(F32)
    scale_q = GDN_DK ** -0.5

    heads = range(GDN_H)
    qcol = lambda h: slice(h * GDN_DK, (h + 1) * GDN_DK)
    kcol = lambda h: slice(GDN_QK + h * GDN_DK, GDN_QK + (h + 1) * GDN_DK)
    vcol = lambda h: slice(2 * GDN_QK + h * GDN_DV, 2 * GDN_QK + (h + 1) * GDN_DV)
    ucol = lambda h: slice(h * (GDN_DV + GDN_DK), h * (GDN_DV + GDN_DK) + GDN_DV)
    wcol = lambda h: slice(h * (GDN_DV + GDN_DK) + GDN_DV, (h + 1) * (GDN_DV + GDN_DK))
    if per_chunk_state:
        rv = lax.broadcasted_iota(jnp.int32, (c, 1), 0)
        rvalid = (rv >= valid_lo) & (rv < valid_hi)

    def chunk_rows(ci):
        if isinstance(ci, int):
            return slice(ci * c, (ci + 1) * c)
        return pl.ds(pl.multiple_of(ci * c, c), c)

    def prep_body(gi, carry):
        probs = []
        for j in range(group):
            rows = chunk_rows(gi * group + j)
            beta_c = gb_ref[0, rows, :]
            gc_all = _dot_f32(tril, gb_ref[1, rows, :])
            gc_t = gc_all.T
            gl_ref[gi * group + j] = gc_all[c - 1:c, :]
            for h in heads:
                q = cv_ref[rows, qcol(h)]
                k = cv_ref[rows, kcol(h)]
                v = cv_ref[rows, vcol(h)]
                q = q * lax.rsqrt(jnp.sum(q * q, axis=-1, keepdims=True) + NORM_EPS) * scale_q
                k = k * lax.rsqrt(jnp.sum(k * k, axis=-1, keepdims=True) + NORM_EPS)
                if per_chunk_state:
                    q = jnp.where(rvalid, q, 0.0)
                    k = jnp.where(rvalid, k, 0.0)
                    v = jnp.where(rvalid, v, 0.0)
                beta = beta_c[:, h:h + 1]
                gc_col = gc_all[:, GDN_H + h:GDN_H + h + 1]
                gc_row = gc_t[GDN_H + h:GDN_H + h + 1, :]
                diff = jnp.where(causal, gc_col - gc_row, 0.0)
                decay = jnp.where(causal, jnp.exp(diff), 0.0)
                e_gc = jnp.exp(gc_col)
                kb = k * beta
                rhs = jnp.concatenate([v * beta, kb * e_gc], axis=1)
                cv_ref[rows, qcol(h)] = q * e_gc
                cv_ref[rows, kcol(h)] = k * jnp.exp(gc_col[c - 1:c, :] - gc_col)
                probs.append((rows, h, q, k, kb, rhs, decay))
        kq = [_dot_nt(jnp.concatenate([kb, q], axis=0), k) for (_, _, q, k, kb, _, _) in probs]
        a_list = [jnp.where(strict, kq_i[:c] * p[6], 0.0) for kq_i, p in zip(kq, probs)]
        t_inv = _tri_inv_many(a_list, c)
        for t_i, kq_i, (rows, h, _, _, _, rhs, decay) in zip(t_inv, kq, probs):
            sol_ref[rows, h * (GDN_DV + GDN_DK):(h + 1) * (GDN_DV + GDN_DK)] = _dot(t_i, rhs)
            att_ref[h, rows, :] = kq_i[c:] * decay
        return carry

    def scan_body(ci, carry):
        rows = chunk_rows(ci)
        si = ci if per_chunk_state else 0
        g_tot = jnp.exp(gl_ref[ci])
        s_old = [s_ref[si, h] for h in heads]
        wq_s = [_dot(jnp.concatenate([sol_ref[rows, wcol(h)], cv_ref[rows, qcol(h)]], axis=0), s_old[h])
                for h in heads]
        v_new = [sol_ref[rows, ucol(h)] - wq_s[h][:c] for h in heads]
        o_att = [_dot(att_ref[h, rows, :], v_new[h]) for h in heads]
        for h in heads:
            s_ref[si, h] = (s_old[h] * g_tot[:, GDN_H + h:GDN_H + h + 1]
                            + _dot_tn(cv_ref[rows, kcol(h)], v_new[h]))
        for h in heads:
            o = wq_s[h][c:] + o_att[h]
            o = o * lax.rsqrt(jnp.mean(o * o, axis=-1, keepdims=True) + NORM_EPS) * nw_ref[...]
            oacc_ref[rows, h * GDN_DV:(h + 1) * GDN_DV] = o
        return carry

    if per_chunk_state:
        s_ref[...] = s0_ref[...]
    if unroll:
        for gi in range(n_chunks // group):
            prep_body(gi, 0)
        for ci in range(n_chunks):
            scan_body(ci, 0)
    else:
        lax.fori_loop(0, n_chunks // group, prep_body, 0)
        lax.fori_loop(0, n_chunks, scan_body, 0)
    o_ref[...] = (oacc_ref[...] * _silu(z_ref[...].astype(F32))).astype(o_ref.dtype)


def _gdn(qkv, z, ba, hdr, s0, conv_w, a_row, dt_row, norm_w, *, n_seq, chunk, n_chunks,
         per_chunk_state, valid_lo, valid_hi, out_dtype, z_colblock=0, ba_colblock=0, group=2,
         unroll=False):
    rows = qkv.shape[0]
    tb = chunk * n_chunks
    if per_chunk_state:
        grid = (rows // tb,)
        sem = ("arbitrary",)
        rmap = lambda i: (i, 0)
        zmap = lambda i: (i, z_colblock)
        bmap = lambda i: (i, ba_colblock)
        hmap = lambda i: (0, 0, 0)
        smap = lambda i: (i, 0, 0, 0)
        cmap = lambda i: (0, 0)
        ns = n_chunks
    else:
        steps = rows // n_seq // tb
        grid = (n_seq, steps)
        sem = ("parallel", "arbitrary")
        rmap = lambda b, n: (b * steps + n, 0)
        zmap = lambda b, n: (b * steps + n, z_colblock)
        bmap = lambda b, n: (b * steps + n, ba_colblock)
        hmap = lambda b, n: (b, 0, 0)
        smap = lambda b, n: (b, 0, 0, 0)
        cmap = lambda b, n: (0, 0)
        ns = 1
    assert n_chunks % group == 0
    kern = functools.partial(_gdn_kernel, chunk=chunk, n_chunks=n_chunks, group=group, unroll=unroll,
                             per_chunk_state=per_chunk_state, valid_lo=valid_lo, valid_hi=valid_hi)
    return pl.pallas_call(
        kern,
        out_shape=(jax.ShapeDtypeStruct((rows, GDN_W), out_dtype),
                   jax.ShapeDtypeStruct(s0.shape, F32)),
        grid=grid,
        in_specs=[pl.BlockSpec((tb, GDN_CONV_CH), rmap),
                  pl.BlockSpec((tb, GDN_W), zmap),
                  pl.BlockSpec((tb, LANE), bmap),
                  pl.BlockSpec((1, HDR, GDN_CONV_CH), hmap),
                  pl.BlockSpec((ns, GDN_H, GDN_DK, GDN_DV), smap),
                  pl.BlockSpec((CONV_W, GDN_CONV_CH), cmap),
                  pl.BlockSpec((1, LANE), cmap),
                  pl.BlockSpec((1, LANE), cmap),
                  pl.BlockSpec((1, GDN_DV), cmap)],
        out_specs=(pl.BlockSpec((tb, GDN_W), rmap),
                   pl.BlockSpec((ns, GDN_H, GDN_DK, GDN_DV), smap)),
        scratch_shapes=[pltpu.VMEM((HDR + tb, GDN_CONV_CH), F32),
                        pltpu.VMEM((tb, GDN_CONV_CH), F32),
                        pltpu.VMEM((2, tb, LANE), F32),
                        pltpu.VMEM((tb, GDN_W), F32),
                        pltpu.VMEM((tb, GDN_H * (GDN_DV + GDN_DK)), F32),
                        pltpu.VMEM((GDN_H, tb, chunk), F32),
                        pltpu.VMEM((n_chunks, 1, LANE), F32)],
        compiler_params=_cparams(sem),
        name="gdn",
    )(qkv, z, ba, hdr, s0, conv_w, a_row, dt_row, norm_w)


def _t5_bucket_np(dist):
    n = np.maximum(dist, 0)
    max_exact = N_BUCKETS // 2
    nf = np.maximum(n, 1).astype(np.float32)
    large = max_exact + (np.log(nf / np.float32(max_exact)) / np.float32(math.log(MAX_DISTANCE / max_exact))
                         * np.float32(N_BUCKETS - max_exact)).astype(np.int32)
    large = np.minimum(large, N_BUCKETS - 1)
    return np.where(n < max_exact, n, large).astype(np.int32)


def _bias_prompt_kernel(code_ref, tab_ref, o_ref):
    h = pl.program_id(1)
    code = code_ref[0]
    acc = jnp.full(code.shape, -jnp.inf, F32)
    for b in range(N_BUCKETS):
        acc = jnp.where(code == b, tab_ref[b, h], acc)
    o_ref[0, 0] = acc


def _bias_prompt(table):
    qi = np.arange(WINDOW)[:, None]
    sj = np.arange(WINDOW)[None, :]
    own = sj <= qi
    bucket = _t5_bucket_np(np.where(own, qi - sj, qi + WINDOW - sj))
    code_first = np.where(own, bucket, -1)
    code = jnp.asarray(np.stack([code_first, bucket]).astype(np.int32))
    return pl.pallas_call(
        _bias_prompt_kernel,
        out_shape=jax.ShapeDtypeStruct((2, SWA_H, WINDOW, WINDOW), F32),
        grid=(2, SWA_H),
        in_specs=[pl.BlockSpec((1, WINDOW, WINDOW), lambda v, h: (v, 0, 0)),
                  pl.BlockSpec(memory_space=pltpu.SMEM)],
        out_specs=pl.BlockSpec((1, 1, WINDOW, WINDOW), lambda v, h: (v, h, 0, 0)),
        compiler_params=_cparams(("arbitrary", "arbitrary")),
        name="swa_bias_prompt",
    )(code, table)


def _bias_sample_kernel(code_ref, tab_ref, o_ref):
    kv = pl.program_id(0)
    code = code_ref[...]
    acc = jnp.full(code.shape, -jnp.inf, F32)
    for g in range(SWA_G):
        for b in range(N_BUCKETS):
            acc = jnp.where(code == b + N_BUCKETS * g, tab_ref[b, kv * SWA_G + g], acc)
    o_ref[0] = acc


def _bias_sample(table, n_tok, n_cache, n_keys_pad):
    dist = (n_cache + np.arange(n_tok))[:, None] - np.arange(n_keys_pad)[None, :]
    valid = (dist >= 0) & (dist < WINDOW) & (np.arange(n_keys_pad)[None, :] < n_cache + n_tok)
    bucket = _t5_bucket_np(dist)
    code_t = np.where(valid, bucket, -1)
    g = np.arange(SWA_G)[:, None, None]
    code = np.where(code_t[None] >= 0, code_t[None] + N_BUCKETS * g, -1)
    code = jnp.asarray(code.reshape(SWA_G * n_tok, n_keys_pad).astype(np.int32))
    return pl.pallas_call(
        _bias_sample_kernel,
        out_shape=jax.ShapeDtypeStruct((SWA_KV, SWA_G * n_tok, n_keys_pad), F32),
        grid=(SWA_KV,),
        in_specs=[pl.BlockSpec((SWA_G * n_tok, n_keys_pad), lambda k: (0, 0)),
                  pl.BlockSpec(memory_space=pltpu.SMEM)],
        out_specs=pl.BlockSpec((1, SWA_G * n_tok, n_keys_pad), lambda k: (k, 0, 0)),
        compiler_params=_cparams(("arbitrary",)),
        name="swa_bias_sample",
    )(code, table)


def _sink_softmax_pv(logits, sink, v):
    m = jnp.maximum(jnp.max(logits, axis=-1, keepdims=True), sink)
    p = jnp.exp(logits - m)
    den = jnp.sum(p, axis=-1, keepdims=True) + jnp.exp(sink - m)
    return _dot(p, v) / den


def _swa_prompt_kernel(q_ref, z_ref, kc_ref, kp_ref, vc_ref, vp_ref, bias_ref, sink_ref, o_ref, *, n_qblk):
    kall = jnp.concatenate([kp_ref[...], kc_ref[...]], axis=0).astype(BF16)
    vall = jnp.concatenate([vp_ref[...], vc_ref[...]], axis=0).astype(BF16)
    scale = SWA_DH ** -0.5
    row = lax.broadcasted_iota(jnp.int32, (WINDOW, WINDOW), 0)
    col = lax.broadcasted_iota(jnp.int32, (WINDOW, WINDOW), 1)
    from_prev = col > row
    first_variant = jnp.where(pl.program_id(1) == 0, 0, 1)
    cs = lambda h: slice(h * SWA_DH, (h + 1) * SWA_DH)
    rows = lambda qb: slice(qb * WINDOW, (qb + 1) * WINDOW)
    keys = lambda qb: slice(qb * WINDOW, (qb + 2) * WINDOW)
    for kv in range(SWA_KV):
        ks = slice(kv * SWA_DH, (kv + 1) * SWA_DH)
        probs = [(qb, h) for qb in range(n_qblk) for h in range(kv * SWA_G, (kv + 1) * SWA_G)]
        lg = [_dot_nt(q_ref[rows(qb), cs(h)] * scale, kall[keys(qb), ks]) for qb, h in probs]
        ps, dens = [], []
        for (qb, h), l in zip(probs, lg):
            bias = bias_ref[first_variant if qb == 0 else 1, h]
            l = jnp.where(from_prev, l[:, :WINDOW], l[:, WINDOW:]) + bias
            sink = sink_ref[h]
            m = jnp.maximum(jnp.max(l, axis=-1, keepdims=True), sink)
            p = jnp.exp(l - m)
            dens.append(jnp.sum(p, axis=-1, keepdims=True) + jnp.exp(sink - m))
            ps.append(jnp.concatenate([jnp.where(from_prev, p, 0.0), jnp.where(from_prev, 0.0, p)],
                                      axis=1))
        outs = [_dot(p, vall[keys(qb), ks]) / den for p, den, (qb, _) in zip(ps, dens, probs)]
        for j in range(0, len(probs), 2):
            qb, h = probs[j]
            two = slice(h * SWA_DH, (h + 2) * SWA_DH)
            o2 = jnp.concatenate([outs[j], outs[j + 1]], axis=1)
            o_ref[rows(qb), two] = (o2 * _silu(z_ref[rows(qb), two].astype(F32))).astype(o_ref.dtype)


def _swa_prompt(proj, kv, bias, sinks, n_seq, seq_len, q_colblock, z_colblock, k_colblock, v_colblock,
                n_qblk=1):
    tq = n_qblk * WINDOW
    steps = seq_len // tq
    cur = lambda b, n: b * steps + n
    prev = lambda b, n: (b * steps + n) * n_qblk - jnp.where(n == 0, 0, 1)
    return pl.pallas_call(
        functools.partial(_swa_prompt_kernel, n_qblk=n_qblk),
        out_shape=jax.ShapeDtypeStruct((n_seq * seq_len, SWA_QW), BF16),
        grid=(n_seq, steps),
        in_specs=[pl.BlockSpec((tq, SWA_QW), lambda b, n: (cur(b, n), q_colblock)),
                  pl.BlockSpec((tq, SWA_QW), lambda b, n: (cur(b, n), z_colblock)),
                  pl.BlockSpec((tq, SWA_KVW), lambda b, n: (cur(b, n), k_colblock)),
                  pl.BlockSpec((WINDOW, SWA_KVW), lambda b, n: (prev(b, n), k_colblock)),
                  pl.BlockSpec((tq, SWA_KVW), lambda b, n: (cur(b, n), v_colblock)),
                  pl.BlockSpec((WINDOW, SWA_KVW), lambda b, n: (prev(b, n), v_colblock)),
                  pl.BlockSpec((2, SWA_H, WINDOW, WINDOW), lambda b, n: (0, 0, 0, 0)),
                  pl.BlockSpec(memory_space=pltpu.SMEM)],
        out_specs=pl.BlockSpec((tq, SWA_QW), lambda b, n: (cur(b, n), 0)),
        compiler_params=_cparams(("parallel", "arbitrary")),
        name="swa_prompt",
    )(proj, proj, kv, kv, kv, kv, bias, sinks)


def _swa_sample_kernel(q_ref, z_ref, kn_ref, vn_ref, ck_ref, cv_ref, bias_ref, sink_ref, o_ref, *, n_seq_blk):
    scale = SWA_DH ** -0.5
    wb = ck_ref.shape[3]
    probs = [(s, kv) for s in range(n_seq_blk) for kv in range(SWA_KV)]
    ks = lambda kv: slice(kv * SWA_DH, (kv + 1) * SWA_DH)
    lc = [_dot(q_ref[s, kv] * scale, ck_ref[s, kv]) + bias_ref[kv, :, 0:wb] for s, kv in probs]
    ln = [_dot_nt(q_ref[s, kv] * scale, kn_ref[s, :, ks(kv)]) + bias_ref[kv, :, wb:] for s, kv in probs]
    pcs, pns, dens = [], [], []
    for (s, kv), c, n in zip(probs, lc, ln):
        sink = sink_ref[kv][:, 0:1]
        m = jnp.maximum(jnp.maximum(jnp.max(c, axis=-1, keepdims=True), jnp.max(n, axis=-1, keepdims=True)),
                        sink)
        pc = jnp.exp(c - m)
        pn = jnp.exp(n - m)
        pcs.append(pc)
        pns.append(pn)
        dens.append(jnp.sum(pc, axis=-1, keepdims=True) + jnp.sum(pn, axis=-1, keepdims=True)
                    + jnp.exp(sink - m))
    outs = [(_dot_nt(pc, cv_ref[s, kv]) + _dot(pn, vn_ref[s, :, ks(kv)])) / den
            for (s, kv), pc, pn, den in zip(probs, pcs, pns, dens)]
    for oh, (s, kv) in zip(outs, probs):
        o_ref[s, kv] = (oh * _silu(z_ref[s, kv].astype(F32))).astype(o_ref.dtype)


def _swa_sample(q, z, k_new, v_new, cache_kt, cache_vt, bias, sink_rows, n_seq_blk=8):
    bd, _, rows, _ = q.shape
    wb = cache_kt.shape[3]
    npad = k_new.shape[1]
    blk4 = lambda i: (i, 0, 0, 0)
    blk3 = lambda i: (i, 0, 0)
    return pl.pallas_call(
        functools.partial(_swa_sample_kernel, n_seq_blk=n_seq_blk),
        out_shape=jax.ShapeDtypeStruct(q.shape, BF16),
        grid=(bd // n_seq_blk,),
        in_specs=[pl.BlockSpec((n_seq_blk, SWA_KV, rows, SWA_DH), blk4),
                  pl.BlockSpec((n_seq_blk, SWA_KV, rows, SWA_DH), blk4),
                  pl.BlockSpec((n_seq_blk, npad, SWA_KVW), blk3),
                  pl.BlockSpec((n_seq_blk, npad, SWA_KVW), blk3),
                  pl.BlockSpec((n_seq_blk, SWA_KV, SWA_DH, wb), blk4),
                  pl.BlockSpec((n_seq_blk, SWA_KV, SWA_DH, wb), blk4),
                  pl.BlockSpec((SWA_KV, rows, wb + npad), lambda i: (0, 0, 0)),
                  pl.BlockSpec((SWA_KV, rows, LANE), lambda i: (0, 0, 0))],
        out_specs=pl.BlockSpec((n_seq_blk, SWA_KV, rows, SWA_DH), blk4),
        compiler_params=_cparams(("arbitrary",)),
        name="swa_sample",
    )(q, z, k_new, v_new, cache_kt, cache_vt, bias, sink_rows)


def _mem_attend(probs, q_of, z_of, k_of, v_of, store):
    scale = MEM_DH ** -0.5
    logits = [_dot_nt(q_of(p) * scale, k_of(p)) for p in probs]
    ps, dens = [], []
    for l in logits:
        m = jnp.max(l, axis=-1, keepdims=True)
        e = jnp.exp(l - m)
        dens.append(jnp.sum(e, axis=-1, keepdims=True))
        ps.append(e)
    outs = [_dot(e, v_of(p)) / den for e, den, p in zip(ps, dens, probs)]
    for p, oh in zip(probs, outs):
        store(p, oh * _silu(z_of(p).astype(F32)))


def _mem_cols(h):
    return slice(h * MEM_DH, (h + 1) * MEM_DH)


def _mem_prompt_kernel(q_ref, z_ref, k_ref, v_ref, o_ref):
    def store(h, val):
        o_ref[:, _mem_cols(h)] = val.astype(o_ref.dtype)
    _mem_attend(range(MEM_H), lambda h: q_ref[:, _mem_cols(h)], lambda h: z_ref[:, _mem_cols(h)],
                lambda h: k_ref[0, :, _mem_cols(h)], lambda h: v_ref[0, :, _mem_cols(h)], store)


def _mem_prompt(proj, mk, mv, n_seq, seq_len, q_colblock, z_colblock, tq=512):
    steps = seq_len // tq
    return pl.pallas_call(
        _mem_prompt_kernel,
        out_shape=jax.ShapeDtypeStruct((n_seq * seq_len, MEM_W), BF16),
        grid=(n_seq, steps),
        in_specs=[pl.BlockSpec((tq, MEM_W), lambda b, n: (b * steps + n, q_colblock)),
                  pl.BlockSpec((tq, MEM_W), lambda b, n: (b * steps + n, z_colblock)),
                  pl.BlockSpec((1, N_MEM, MEM_W), lambda b, n: (b, 0, 0)),
                  pl.BlockSpec((1, N_MEM, MEM_W), lambda b, n: (b, 0, 0))],
        out_specs=pl.BlockSpec((tq, MEM_W), lambda b, n: (b * steps + n, 0)),
        compiler_params=_cparams(("parallel", "arbitrary")),
        name="mem_prompt",
    )(proj, proj, mk, mv)


def _mem_sample_kernel(q_ref, z_ref, k_hbm, v_hbm, o_ref, kbuf, vbuf, sem, *, n_seq_blk):
    i = pl.program_id(0)
    n_steps = pl.num_programs(0)
    slot = i % 2

    def copies(step, slot_):
        seqs = pl.ds(step * n_seq_blk, n_seq_blk)
        out = []
        for h in range(MEM_H):
            out.append(pltpu.make_async_copy(k_hbm.at[seqs, :, h, :], kbuf.at[slot_, h], sem.at[0, slot_, h]))
            out.append(pltpu.make_async_copy(v_hbm.at[seqs, :, h, :], vbuf.at[slot_, h], sem.at[1, slot_, h]))
        return out

    @pl.when(i == 0)
    def _():
        for cp in copies(0, 0):
            cp.start()

    @pl.when(i + 1 < n_steps)
    def _():
        for cp in copies(i + 1, 1 - slot):
            cp.start()

    for cp in copies(i, slot):
        cp.wait()

    def store(p, val):
        o_ref[p[0], :, _mem_cols(p[1])] = val.astype(o_ref.dtype)
    probs = [(s, h) for s in range(n_seq_blk) for h in range(MEM_H)]
    _mem_attend(probs, lambda p: q_ref[p[0], :, _mem_cols(p[1])], lambda p: z_ref[p[0], :, _mem_cols(p[1])],
                lambda p: kbuf[slot, p[1], p[0]], lambda p: vbuf[slot, p[1], p[0]], store)


def _mem_sample(q, z, cache_k, cache_v, n_seq_blk=4):
    bd, rows, _ = q.shape
    blk = lambda i: (i, 0, 0)
    buf = pltpu.VMEM((2, MEM_H, n_seq_blk, N_MEM, MEM_DH), cache_k.dtype)
    return pl.pallas_call(
        functools.partial(_mem_sample_kernel, n_seq_blk=n_seq_blk),
        out_shape=jax.ShapeDtypeStruct(q.shape, BF16),
        grid=(bd // n_seq_blk,),
        in_specs=[pl.BlockSpec((n_seq_blk, rows, MEM_W), blk),
                  pl.BlockSpec((n_seq_blk, rows, MEM_W), blk),
                  pl.BlockSpec(memory_space=pl.ANY),
                  pl.BlockSpec(memory_space=pl.ANY)],
        out_specs=pl.BlockSpec((n_seq_blk, rows, MEM_W), blk),
        scratch_shapes=[buf, buf, pltpu.SemaphoreType.DMA((2, 2, MEM_H))],
        compiler_params=_cparams(("arbitrary",)),
        name="mem_sample",
    )(q, z, cache_k, cache_v)


def _merge_kernel(og_ref, os_ref, om_ref, gate_ref, x_ref, wb_ref, wo_ref, nf_ref, y_ref, *, n_sub):
    tm = x_ref.shape[0]
    sub = tm // n_sub
    for s in range(n_sub):
        rows = slice(s * sub, (s + 1) * sub)
        merged = None
        for b, o_ref in enumerate((og_ref, os_ref, om_ref)):
            t = jnp.dot(o_ref[rows, :], wb_ref[b], preferred_element_type=F32)
            t = t * gate_ref[rows, b * D_MODEL:(b + 1) * D_MODEL].astype(F32)
            merged = t if merged is None else merged + t
        h = x_ref[rows, :] + jnp.dot(merged.astype(BF16), wo_ref[...], preferred_element_type=F32)
        ms = jnp.mean(h * h, axis=-1, keepdims=True)
        y_ref[rows, :] = h * lax.rsqrt(ms + NORM_EPS) * nf_ref[...]


def _merge(o_gdn, o_swa, o_mem, gates, x, w_branch, w_out, norm_f, tm=256):
    m = x.shape[0]
    tm = min(tm, m)
    row = lambda i: (i, 0)
    const2 = lambda i: (0, 0)
    return pl.pallas_call(
        functools.partial(_merge_kernel, n_sub=1),
        out_shape=jax.ShapeDtypeStruct((m, D_MODEL), F32),
        grid=(m // tm,),
        in_specs=[pl.BlockSpec((tm, BR_W), row),
                  pl.BlockSpec((tm, BR_W), row),
                  pl.BlockSpec((tm, BR_W), row),
                  pl.BlockSpec((tm, N_BRANCH * D_MODEL), row),
                  pl.BlockSpec((tm, D_MODEL), row),
                  pl.BlockSpec((N_BRANCH, BR_W, D_MODEL), lambda i: (0, 0, 0),
                               pipeline_mode=pl.Buffered(1)),
                  pl.BlockSpec((D_MODEL, D_MODEL), const2, pipeline_mode=pl.Buffered(1)),
                  pl.BlockSpec((1, D_MODEL), const2)],
        out_specs=pl.BlockSpec((tm, D_MODEL), row),
        compiler_params=_cparams(("parallel",)),
        name="merge",
    )(o_gdn, o_swa, o_mem, gates, x, w_branch, w_out, norm_f)


_IN_SIZES = (GDN_QK, GDN_QK, GDN_W, GDN_W, GDN_H, GDN_H, SWA_QW, SWA_KVW, SWA_KVW, SWA_QW,
             MEM_W, MEM_W, N_BRANCH * D_MODEL)
_IN_NAMES = ("gq", "gk", "gv", "gz", "gb", "ga", "sq", "sk", "sv", "sz", "mq", "mz", "mg")
_IN_SPAN = {name: (int(off), int(off + size)) for name, off, size in
            zip(_IN_NAMES, np.cumsum((0,) + _IN_SIZES[:-1]), _IN_SIZES)}


def kernel(x_prompt, x_sample, state_gdn, state_gdn_conv, cache_swa_k, cache_swa_v, cache_mem_k,
           cache_mem_v, mem_prompt, norm_in, w_in, gdn_conv_w, gdn_a_log, gdn_dt_bias, gdn_norm,
           swa_sinks, rel_bias, norm_mem, w_mem_kv, w_branch, w_out, norm_f):
    n_layers = norm_in.shape[0]
    assert n_layers == 1
    b, seq, _ = x_prompt.shape
    bd, ns, _ = x_sample.shape
    wb = cache_swa_k.shape[2]
    assert seq % WINDOW == 0 and seq % GDN_CHUNK == 0 and ns + CONV_W <= SUBLANE and wb == WINDOW
    lyr = 0

    w = w_in[lyr]
    main_names = ("gq", "gk", "gv", "gz", "sq", "sz", "mq", "mz")
    n_main = len(main_names) * W_BLK
    n_gate = N_BRANCH * D_MODEL
    n_small = 3 * LANE
    small_lead = -(n_main + n_gate) % n_small
    w_all = _wprep(w.T, [[(_IN_SPAN[a][0], W_BLK)] for a in main_names]
                   + [[(_IN_SPAN["mg"][0] + W_BLK * k, W_BLK)] for k in range(n_gate // W_BLK)]
                   + [[(None, small_lead), (_IN_SPAN["sk"][0], 2 * SWA_KVW), (_IN_SPAN["gb"][0], 2 * GDN_H)]])
    col_small = n_main + n_gate + small_lead
    cb_gz, cb_sq, cb_sz, cb_mq, cb_mz = 3, 4, 5, 6, 7
    cb_sk, cb_sv, cb_ba = 0, 1, 2
    w_mkv = w_mem_kv[lyr].astype(BF16)
    w_br = w_branch[lyr].astype(BF16)
    w_o = w_out[lyr].astype(BF16)
    nw_in = norm_in[lyr].reshape(1, D_MODEL)
    nw_mem = norm_mem[lyr].reshape(1, D_MODEL)
    nw_f = norm_f.reshape(1, D_MODEL)
    conv_w = gdn_conv_w[lyr]
    a_row = jnp.pad(gdn_a_log[lyr].reshape(1, GDN_H), ((0, 0), (GDN_H, LANE - 2 * GDN_H)))
    dt_row = jnp.pad(gdn_dt_bias[lyr].reshape(1, GDN_H), ((0, 0), (GDN_H, LANE - 2 * GDN_H)))
    gnw = gdn_norm[lyr].reshape(1, GDN_DV)
    sinks = swa_sinks[lyr]
    bias_p = _bias_prompt(rel_bias)
    npad = SUBLANE
    bias_s = _bias_sample(rel_bias, ns, wb, wb + npad)
    sink_rows = jnp.broadcast_to(jnp.repeat(sinks.reshape(SWA_KV, SWA_G), ns, axis=1)[:, :, None],
                                 (SWA_KV, SWA_G * ns, LANE))

    t = b * seq
    xp = x_prompt.reshape(t, D_MODEL)
    xn_p, p_small = _rmsnorm(xp, nw_in, w_all, col_small, n_small)
    p_main = _proj(xn_p, w_all, BF16, n=n_main)
    g_p = _proj(xn_p, w_all, BF16, act="sigmoid", col0=n_main, n=n_gate)
    xn_tail = xn_p.reshape(b, seq, D_MODEL)[:, seq - SUBLANE:, :].reshape(b * SUBLANE, D_MODEL)
    conv_p = _proj(xn_tail, w_all, F32, n=GDN_CONV_CH).reshape(b, SUBLANE, GDN_CONV_CH)[:, SUBLANE - (CONV_W - 1):]
    kv_tail = p_small.reshape(b, seq, 3 * LANE)[:, seq - WINDOW:]
    swk_p = kv_tail[:, :, cb_sk * LANE:(cb_sk + 1) * LANE].reshape(b, WINDOW, SWA_KV, SWA_DH)
    swv_p = kv_tail[:, :, cb_sv * LANE:(cb_sv + 1) * LANE].reshape(b, WINDOW, SWA_KV, SWA_DH)

    mkv = _proj(_rmsnorm(mem_prompt.reshape(b * N_MEM, D_MODEL), nw_mem), w_mkv, F32)
    mk_p = mkv[:, :MEM_W].reshape(b, N_MEM, MEM_W)
    mv_p = mkv[:, MEM_W:].reshape(b, N_MEM, MEM_W)

    o_gdn_p, s_p = _gdn(p_main, p_main, p_small, jnp.zeros((b, HDR, GDN_CONV_CH), F32),
                        jnp.zeros((b, GDN_H, GDN_DK, GDN_DV), F32), conv_w, a_row, dt_row, gnw,
                        n_seq=b, chunk=GDN_CHUNK, n_chunks=4, per_chunk_state=False, group=4, unroll=True,
                        valid_lo=0, valid_hi=GDN_CHUNK, out_dtype=BF16,
                        z_colblock=cb_gz, ba_colblock=cb_ba)
    o_swa_p = _swa_prompt(p_main, p_small, bias_p, sinks, b, seq, cb_sq, cb_sz, cb_sk, cb_sv)
    o_mem_p = _mem_prompt(p_main, mk_p.astype(BF16), mv_p.astype(BF16), b, seq, cb_mq, cb_mz)
    y_p = _merge(o_gdn_p, o_swa_p, o_mem_p, g_p, xp, w_br, w_o, nw_f).reshape(b, seq, D_MODEL)

    ts = bd * ns
    xs = x_sample.reshape(ts, D_MODEL)
    xn_s, s_small = _rmsnorm(xs, nw_in, w_all, col_small, n_small)
    s_main = _proj(xn_s, w_all, F32, n=n_main)
    g_s = _proj(xn_s, w_all, BF16, act="sigmoid", col0=n_main, n=n_gate)
    s_gdn = s_main[:, :GDN_CONV_CH + GDN_W]
    s_ba = s_small[:, cb_ba * LANE:(cb_ba + 1) * LANE]
    s_swa = jnp.concatenate([s_main[:, cb_sq * BR_W:(cb_sz + 1) * BR_W], s_small[:, :2 * LANE]], axis=1)
    s_mem = s_main[:, cb_mq * BR_W:(cb_mz + 1) * BR_W].astype(BF16)

    lo = CONV_W - 1
    hi = lo + ns
    pad_rows = ((0, 0), (lo, SUBLANE - hi), (0, 0))
    e_qkv = jnp.concatenate([state_gdn_conv[lyr], s_gdn[:, :GDN_CONV_CH].reshape(bd, ns, GDN_CONV_CH),
                             jnp.zeros((bd, SUBLANE - hi, GDN_CONV_CH), F32)], axis=1)
    e_z = jnp.pad(s_gdn[:, GDN_CONV_CH:].reshape(bd, ns, GDN_W), pad_rows)
    e_ba = jnp.pad(s_ba.reshape(bd, ns, LANE), pad_rows)
    seq_blk = 8
    o_gdn_s8, s_s = _gdn(e_qkv.reshape(bd * SUBLANE, GDN_CONV_CH), e_z.reshape(bd * SUBLANE, GDN_W),
                         e_ba.reshape(bd * SUBLANE, LANE), jnp.zeros((1, HDR, GDN_CONV_CH), F32),
                         state_gdn[lyr], conv_w, a_row, dt_row, gnw,
                         n_seq=bd, chunk=SUBLANE, n_chunks=seq_blk, per_chunk_state=True, group=8,
                         unroll=True,
                         valid_lo=lo, valid_hi=hi, out_dtype=BF16)
    o_gdn_s = o_gdn_s8.reshape(bd, SUBLANE, GDN_W)[:, lo:hi].reshape(ts, GDN_W)
    conv_s = e_qkv[:, hi - (CONV_W - 1):hi]

    def to_heads(a):
        return a.reshape(bd, ns, SWA_KV, SWA_G, SWA_DH).transpose(0, 2, 3, 1, 4).reshape(
            bd, SWA_KV, SWA_G * ns, SWA_DH)

    k_new = s_swa[:, 2 * SWA_QW:2 * SWA_QW + SWA_KVW].reshape(bd, ns, SWA_KVW)
    v_new = s_swa[:, 2 * SWA_QW + SWA_KVW:].reshape(bd, ns, SWA_KVW)
    tok_pad = ((0, 0), (0, npad - ns), (0, 0))
    ck_t = cache_swa_k[lyr].transpose(0, 2, 3, 1)
    cv_t = cache_swa_v[lyr].transpose(0, 2, 3, 1)
    o_swa_h = _swa_sample(to_heads(s_swa[:, :SWA_QW]).astype(BF16), to_heads(s_swa[:, SWA_QW:2 * SWA_QW]),
                          jnp.pad(k_new, tok_pad), jnp.pad(v_new, tok_pad), ck_t, cv_t, bias_s, sink_rows)
    o_swa_s = o_swa_h.reshape(bd, SWA_KV, SWA_G, ns, SWA_DH).transpose(0, 3, 1, 2, 4).reshape(ts, SWA_QW)

    def next_window(cache_t, new):
        new_t = new.reshape(bd, ns, SWA_KV, SWA_DH).transpose(0, 2, 3, 1)
        return jnp.concatenate([cache_t[..., ns:], new_t], axis=-1).transpose(0, 3, 1, 2)

    swk_s = next_window(ck_t, k_new)
    swv_s = next_window(cv_t, v_new)

    mq = jnp.pad(s_mem[:, :MEM_W].reshape(bd, ns, MEM_W), tok_pad)
    mz = jnp.pad(s_mem[:, MEM_W:].reshape(bd, ns, MEM_W), tok_pad)
    o_mem_s = _mem_sample(mq, mz, cache_mem_k[lyr], cache_mem_v[lyr])[:, :ns].reshape(ts, MEM_W)
    y_s = _merge(o_gdn_s, o_swa_s, o_mem_s, g_s, xs, w_br, w_o, nw_f).reshape(bd, ns, D_MODEL)

    return (y_p, y_s,
            s_p[None], conv_p[None], swk_p[None], swv_p[None],
            mk_p.reshape(b, N_MEM, MEM_H, MEM_DH)[None], mv_p.reshape(b, N_MEM, MEM_H, MEM_DH)[None],
            s_s[None], conv_s[None], swk_s[None], swv_s[None])
```

```python
import functools
import math

import numpy as np
import jax
import jax.numpy as jnp
from jax import lax
from jax.experimental import pallas as pl
from jax.experimental.pallas import tpu as pltpu

F32 = jnp.float32
BF16 = jnp.bfloat16

D_MODEL = 2048
N_BRANCH = 3
BR_W = 1024
GDN_H = 8
GDN_DK = 128
GDN_DV = 128
GDN_QK = GDN_H * GDN_DK
GDN_W = GDN_H * GDN_DV
GDN_CONV_CH = 2 * GDN_QK + GDN_W
CONV_W = 4
GDN_CHUNK = 64
SWA_H = 16
SWA_KV = 2
SWA_G = SWA_H // SWA_KV
SWA_DH = 64
SWA_QW = SWA_H * SWA_DH
SWA_KVW = SWA_KV * SWA_DH
WINDOW = 128
N_BUCKETS = 32
MAX_DISTANCE = 128
N_MEM = 256
MEM_H = 4
MEM_DH = 256
MEM_W = MEM_H * MEM_DH
NORM_EPS = 1e-6

LANE = 128
SUBLANE = 8
VMEM_LIMIT = 52 * 1024 * 1024


def _cparams(sem):
    return pltpu.CompilerParams(dimension_semantics=sem, vmem_limit_bytes=VMEM_LIMIT)


def _sigmoid(x):
    return 0.5 * jnp.tanh(0.5 * x) + 0.5


def _silu(x):
    h = 0.5 * x
    return h * jnp.tanh(h) + h


def _softplus(x):
    return jnp.maximum(x, 0.0) + jnp.log(1.0 + jnp.exp(-jnp.abs(x)))


def _dot(a, b):
    return jnp.dot(a.astype(BF16), b.astype(BF16), preferred_element_type=F32)


def _dot_nt(a, b):
    return lax.dot_general(a.astype(BF16), b.astype(BF16), (((1,), (1,)), ((), ())),
                           preferred_element_type=F32)


def _dot_tn(a, b):
    return lax.dot_general(a.astype(BF16), b.astype(BF16), (((0,), (0,)), ((), ())),
                           preferred_element_type=F32)


def _dot_f32(a, b):
    return jnp.dot(a, b, preferred_element_type=F32, precision=lax.Precision.HIGHEST)


def _rmsnorm_kernel(x_ref, nw_ref, *rest):
    x = x_ref[...].astype(F32)
    ms = jnp.mean(x * x, axis=-1, keepdims=True)
    xn = (x * lax.rsqrt(ms + NORM_EPS) * nw_ref[...]).astype(BF16)
    if len(rest) == 1:
        rest[0][...] = xn
    else:
        w_ref, o_ref, p_ref = rest
        o_ref[...] = xn
        p_ref[...] = jnp.dot(xn, w_ref[...], preferred_element_type=F32)


def _rmsnorm(x, norm_w, w=None, col0=0, n=None, tm_pref=512):
    m, d = x.shape
    tm = min(m, tm_pref)
    assert m % tm == 0
    row = lambda i: (i, 0)
    in_specs = [pl.BlockSpec((tm, d), row), pl.BlockSpec((1, d), lambda i: (0, 0))]
    out_shape = jax.ShapeDtypeStruct((m, d), BF16)
    out_specs = pl.BlockSpec((tm, d), row)
    args = (x, norm_w)
    if w is not None:
        assert col0 % n == 0
        in_specs.append(pl.BlockSpec((d, n), lambda i: (0, col0 // n)))
        out_shape = (out_shape, jax.ShapeDtypeStruct((m, n), F32))
        out_specs = (out_specs, pl.BlockSpec((tm, n), row))
        args = args + (w,)
    return pl.pallas_call(
        _rmsnorm_kernel,
        out_shape=out_shape,
        grid=(m // tm,),
        in_specs=in_specs,
        out_specs=out_specs,
        compiler_params=_cparams(("parallel",)),
        name="rmsnorm",
    )(*args)


def _proj_kernel(x_ref, w_ref, o_ref, *, act):
    y = jnp.dot(x_ref[...], w_ref[...], preferred_element_type=F32)
    if act == "sigmoid":
        y = _sigmoid(y)
    o_ref[...] = y.astype(o_ref.dtype)


def _pick_tile(n, pref):
    t = min(n, pref)
    while n % t:
        t -= LANE
    return t


def _proj(xn, w, out_dtype, act=None, col0=0, n=None, tm_pref=1024, tn_pref=2048):
    m, d = xn.shape
    n = w.shape[1] if n is None else n
    tm = min(m, tm_pref)
    assert m % tm == 0
    tn = _pick_tile(n, tn_pref if out_dtype == BF16 else tn_pref // 2)
    while col0 % tn:
        tn = _pick_tile(n, tn - LANE)
    jb = col0 // tn
    return pl.pallas_call(
        functools.partial(_proj_kernel, act=act),
        out_shape=jax.ShapeDtypeStruct((m, n), out_dtype),
        grid=(m // tm, n // tn),
        in_specs=[pl.BlockSpec((tm, d), lambda i, j: (i, 0)),
                  pl.BlockSpec((d, tn), lambda i, j: (0, jb + j))],
        out_specs=pl.BlockSpec((tm, tn), lambda i, j: (i, j)),
        compiler_params=_cparams(("parallel", "arbitrary")),
        name="proj",
    )(xn, w)


W_BLK = 1024
W_PIECE_ROWS = (W_BLK, 2 * SUBLANE)


def _wprep_kernel(wt_hbm, o_ref, inbuf0, inbuf1, sem, *, blocks):
    j = pl.program_id(0)
    inbuf = (inbuf0, inbuf1)

    def sources(b):
        return [(start, rows) for start, rows in blocks[b] if start is not None]

    def pieces(b):
        return [pltpu.make_async_copy(wt_hbm.at[pl.ds(start, rows), :],
                                      inbuf[k].at[b % 2, pl.ds(0, rows), :], sem.at[b % 2, k])
                for k, (start, rows) in enumerate(sources(b))]

    for b, ranges in enumerate(blocks):
        @pl.when(j == b)
        def _(b=b, ranges=ranges):
            if b == 0:
                for cp in pieces(0):
                    cp.start()
            if b + 1 < len(blocks):
                for cp in pieces(b + 1):
                    cp.start()
            for cp in pieces(b):
                cp.wait()
            n_have = sum(rows for _, rows in ranges)
            parts, k = [], 0
            for start, rows in tuple(ranges) + ((None, W_BLK - n_have),):
                if start is None:
                    if rows:
                        parts.append(jnp.zeros((rows, inbuf0.shape[2]), inbuf0.dtype))
                else:
                    parts.append(inbuf[k][b % 2, 0:rows, :])
                    k += 1
            val = parts[0] if len(parts) == 1 else jnp.concatenate(parts, axis=0)
            o_ref[...] = val.T.astype(BF16)


def _wprep(wt, block_srcs):
    ncol, d = wt.shape
    for ranges in block_srcs:
        srcs = [(s, rows) for s, rows in ranges if s is not None]
        assert len(srcs) <= len(W_PIECE_ROWS) and all(rows % SUBLANE == 0 for _, rows in ranges)
        for (s, rows), cap in zip(srcs, W_PIECE_ROWS):
            assert s % SUBLANE == 0 and rows <= cap and s + rows <= ncol
    return pl.pallas_call(
        functools.partial(_wprep_kernel, blocks=tuple(tuple(r) for r in block_srcs)),
        out_shape=jax.ShapeDtypeStruct((d, W_BLK * len(block_srcs)), BF16),
        grid=(len(block_srcs),),
        in_specs=[pl.BlockSpec(memory_space=pl.ANY)],
        out_specs=pl.BlockSpec((d, W_BLK), lambda j: (0, j)),
        scratch_shapes=[pltpu.VMEM((2, cap, d), wt.dtype) for cap in W_PIECE_ROWS]
                       + [pltpu.SemaphoreType.DMA((2, len(W_PIECE_ROWS)))],
        compiler_params=_cparams(("arbitrary",)),
        name="wprep",
    )(wt)


HDR = SUBLANE


def _tri_inv_many(a_list, c):
    row = lax.broadcasted_iota(jnp.int32, (c, c), 0)
    col = lax.broadcasted_iota(jnp.int32, (c, c), 1)
    eye = jnp.where(row == col, 1.0, 0.0).astype(F32)
    xs = [eye - a for a in a_list]
    bs = [_dot(a, a) for a in a_list]
    n = 2
    while n < c:
        xs = [x + _dot(x, b) for x, b in zip(xs, bs)]
        n *= 2
        if n < c:
            bs = [_dot(b, b) for b in bs]
    return xs


def _gdn_kernel(qkv_ref, z_ref, ba_ref, hdr_ref, s0_ref, cw_ref, arow_ref, dtrow_ref, nw_ref,
                o_ref, s_ref, buf_ref, cv_ref, gb_ref, oacc_ref, sol_ref, att_ref, gl_ref,
                *, chunk, n_chunks, group, unroll, per_chunk_state, valid_lo, valid_hi):
    c = chunk
    tb = c * n_chunks

    if per_chunk_state:
        buf_ref[0:2 * HDR, :] = jnp.zeros((2 * HDR, GDN_CONV_CH), F32)
    else:
        @pl.when(pl.program_id(1) == 0)
        def _():
            buf_ref[0:HDR, :] = hdr_ref[0]
            buf_ref[HDR:2 * HDR, :] = jnp.zeros((HDR, GDN_CONV_CH), F32)
            s_ref[...] = s0_ref[...]

    def tap_from_history(j, n_rows):
        off = HDR - (CONV_W - 1) + j
        return buf_ref[off:off + n_rows, :] * cw_ref[j:j + 1, :]

    if qkv_ref.dtype == BF16:
        xb = qkv_ref[...]
        r = lax.broadcasted_iota(jnp.int32, (tb, tb), 0)
        cc = lax.broadcasted_iota(jnp.int32, (tb, tb), 1)
        acc = xb.astype(F32) * cw_ref[CONV_W - 1:CONV_W, :]
        for j in range(CONV_W - 1):
            shift = jnp.where(r - cc == CONV_W - 1 - j, 1.0, 0.0).astype(BF16)
            acc = acc + jnp.dot(shift, xb, preferred_element_type=F32) * cw_ref[j:j + 1, :]
        cv_ref[...] = _silu(acc)
        top = acc[0:HDR]
        for j in range(CONV_W - 1):
            top = top + tap_from_history(j, HDR)
        cv_ref[0:HDR, :] = _silu(top)
        buf_ref[0:HDR, :] = qkv_ref[tb - 2 * HDR:tb, :].astype(F32)[HDR:]
    else:
        buf_ref[HDR:HDR + tb, :] = qkv_ref[...].astype(F32)
        acc = None
        for j in range(CONV_W):
            term = tap_from_history(j, tb)
            acc = term if acc is None else acc + term
        cv_ref[...] = _silu(acc)
        if not per_chunk_state:
            buf_ref[0:HDR, :] = buf_ref[tb:tb + HDR, :]

    ba = ba_ref[...].astype(F32)
    beta_all = _sigmoid(ba)
    g_all = -jnp.exp(arow_ref[...]) * _softplus(ba + dtrow_ref[...])
    if per_chunk_state:
        r = lax.broadcasted_iota(jnp.int32, (tb, LANE), 0) & (c - 1)
        valid = (r >= valid_lo) & (r < valid_hi)
        beta_all = jnp.where(valid, beta_all, 0.0)
        g_all = jnp.where(valid, g_all, 0.0)
    gb_ref[0] = beta_all
    gb_ref[1] = g_all

    row = lax.broadcasted_iota(jnp.int32, (c, c), 0)
    col = lax.broadcasted_iota(jnp.int32, (c, c), 1)
    causal = row >= col
    strict = row > col
    tril = jnp.where(causal, 1.0, 0.0).astype(F32)
    scale_q = GDN_DK ** -0.5

    heads = range(GDN_H)
    qcol = lambda h: slice(h * GDN_DK, (h + 1) * GDN_DK)
    kcol = lambda h: slice(GDN_QK + h * GDN_DK, GDN_QK + (h + 1) * GDN_DK)
    vcol = lambda h: slice(2 * GDN_QK + h * GDN_DV, 2 * GDN_QK + (h + 1) * GDN_DV)
    ucol = lambda h: slice(h * (GDN_DV + GDN_DK), h * (GDN_DV + GDN_DK) + GDN_DV)
    wcol = lambda h: slice(h * (GDN_DV + GDN_DK) + GDN_DV, (h + 1) * (GDN_DV + GDN_DK))
    if per_chunk_state:
        rv = lax.broadcasted_iota(jnp.int32, (c, 1), 0)
        rvalid = (rv >= valid_lo) & (rv < valid_hi)

    def chunk_rows(ci):
        if isinstance(ci, int):
            return slice(ci * c, (ci + 1) * c)
        return pl.ds(pl.multiple_of(ci * c, c), c)

    def prep_body(gi, carry):
        probs = []
        for j in range(group):
            rows = chunk_rows(gi * group + j)
            beta_c = gb_ref[0, rows, :]
            gc_all = _dot_f32(tril, gb_ref[1, rows, :])
            gc_t = gc_all.T
            gl_ref[gi * group + j] = gc_all[c - 1:c, :]
            for h in heads:
                q = cv_ref[rows, qcol(h)]
                k = cv_ref[rows, kcol(h)]
                v = cv_ref[rows, vcol(h)]
                q = q * lax.rsqrt(jnp.sum(q * q, axis=-1, keepdims=True) + NORM_EPS) * scale_q
                k = k * lax.rsqrt(jnp.sum(k * k, axis=-1, keepdims=True) + NORM_EPS)
                if per_chunk_state:
                    q = jnp.where(rvalid, q, 0.0)
                    k = jnp.where(rvalid, k, 0.0)
                    v = jnp.where(rvalid, v, 0.0)
                beta = beta_c[:, h:h + 1]
                gc_col = gc_all[:, GDN_H + h:GDN_H + h + 1]
                gc_row = gc_t[GDN_H + h:GDN_H + h + 1, :]
                diff = jnp.where(causal, gc_col - gc_row, 0.0)
                decay = jnp.where(causal, jnp.exp(diff), 0.0)
                e_gc = jnp.exp(gc_col)
                kb = k * beta
                rhs = jnp.concatenate([v * beta, kb * e_gc], axis=1)
                cv_ref[rows, qcol(h)] = q * e_gc
                cv_ref[rows, kcol(h)] = k * jnp.exp(gc_col[c - 1:c, :] - gc_col)
                probs.append((rows, h, q, k, kb, rhs, decay))
        kq = [_dot_nt(jnp.concatenate([kb, q], axis=0), k) for (_, _, q, k, kb, _, _) in probs]
        a_list = [jnp.where(strict, kq_i[:c] * p[6], 0.0) for kq_i, p in zip(kq, probs)]
        t_inv = _tri_inv_many(a_list, c)
        for t_i, kq_i, (rows, h, _, _, _, rhs, decay) in zip(t_inv, kq, probs):
            sol_ref[rows, h * (GDN_DV + GDN_DK):(h + 1) * (GDN_DV + GDN_DK)] = _dot(t_i, rhs)
            att_ref[h, rows, :] = kq_i[c:] * decay
        return carry

    def scan_body(ci, carry):
        rows = chunk_rows(ci)
        si = ci if per_chunk_state else 0
        g_tot = jnp.exp(gl_ref[ci])
        s_old = [s_ref[si, h] for h in heads]
        wq_s = [_dot(jnp.concatenate([sol_ref[rows, wcol(h)], cv_ref[rows, qcol(h)]], axis=0), s_old[h])
                for h in heads]
        v_new = [sol_ref[rows, ucol(h)] - wq_s[h][:c] for h in heads]
        o_att = [_dot(att_ref[h, rows, :], v_new[h]) for h in heads]
        for h in heads:
            s_ref[si, h] = (s_old[h] * g_tot[:, GDN_H + h:GDN_H + h + 1]
                            + _dot_tn(cv_ref[rows, kcol(h)], v_new[h]))
        for h in heads:
            o = wq_s[h][c:] + o_att[h]
            o = o * lax.rsqrt(jnp.mean(o * o, axis=-1, keepdims=True) + NORM_EPS) * nw_ref[...]
            oacc_ref[rows, h * GDN_DV:(h + 1) * GDN_DV] = o
        return carry

    if per_chunk_state:
        s_ref[...] = s0_ref[...]
    if unroll:
        for gi in range(n_chunks // group):
            prep_body(gi, 0)
        for ci in range(n_chunks):
            scan_body(ci, 0)
    else:
        lax.fori_loop(0, n_chunks // group, prep_body, 0)
        lax.fori_loop(0, n_chunks, scan_body, 0)
    o_ref[...] = (oacc_ref[...] * _silu(z_ref[...].astype(F32))).astype(o_ref.dtype)


def _gdn(qkv, z, ba, hdr, s0, conv_w, a_row, dt_row, norm_w, *, n_seq, chunk, n_chunks,
         per_chunk_state, valid_lo, valid_hi, out_dtype, z_colblock=0, ba_colblock=0, group=2,
         unroll=False):
    rows = qkv.shape[0]
    tb = chunk * n_chunks
    if per_chunk_state:
        grid = (rows // tb,)
        sem = ("arbitrary",)
        rmap = lambda i: (i, 0)
        zmap = lambda i: (i, z_colblock)
        bmap = lambda i: (i, ba_colblock)
        hmap = lambda i: (0, 0, 0)
        smap = lambda i: (i, 0, 0, 0)
        cmap = lambda i: (0, 0)
        ns = n_chunks
    else:
        steps = rows // n_seq // tb
        grid = (n_seq, steps)
        sem = ("parallel", "arbitrary")
        rmap = lambda b, n: (b * steps + n, 0)
        zmap = lambda b, n: (b * steps + n, z_colblock)
        bmap = lambda b, n: (b * steps + n, ba_colblock)
        hmap = lambda b, n: (b, 0, 0)
        smap = lambda b, n: (b, 0, 0, 0)
        cmap = lambda b, n: (0, 0)
        ns = 1
    assert n_chunks % group == 0
    kern = functools.partial(_gdn_kernel, chunk=chunk, n_chunks=n_chunks, group=group, unroll=unroll,
                             per_chunk_state=per_chunk_state, valid_lo=valid_lo, valid_hi=valid_hi)
    return pl.pallas_call(
        kern,
        out_shape=(jax.ShapeDtypeStruct((rows, GDN_W), out_dtype),
                   jax.ShapeDtypeStruct(s0.shape, F32)),
        grid=grid,
        in_specs=[pl.BlockSpec((tb, GDN_CONV_CH), rmap),
                  pl.BlockSpec((tb, GDN_W), zmap),
                  pl.BlockSpec((tb, LANE), bmap),
                  pl.BlockSpec((1, HDR, GDN_CONV_CH), hmap),
                  pl.BlockSpec((ns, GDN_H, GDN_DK, GDN_DV), smap),
                  pl.BlockSpec((CONV_W, GDN_CONV_CH), cmap),
                  pl.BlockSpec((1, LANE), cmap),
                  pl.BlockSpec((1, LANE), cmap),
                  pl.BlockSpec((1, GDN_DV), cmap)],
        out_specs=(pl.BlockSpec((tb, GDN_W), rmap),
                   pl.BlockSpec((ns, GDN_H, GDN_DK, GDN_DV), smap)),
        scratch_shapes=[pltpu.VMEM((HDR + tb, GDN_CONV_CH), F32),
                        pltpu.VMEM((tb, GDN_CONV_CH), F32),
                        pltpu.VMEM((2, tb, LANE), F32),
                        pltpu.VMEM((tb, GDN_W), F32),
                        pltpu.VMEM((tb, GDN_H * (GDN_DV + GDN_DK)), F32),
                        pltpu.VMEM((GDN_H, tb, chunk), F32),
                        pltpu.VMEM((n_chunks, 1, LANE), F32)],
        compiler_params=_cparams(sem),
        name="gdn",
    )(qkv, z, ba, hdr, s0, conv_w, a_row, dt_row, norm_w)


def _t5_bucket_np(dist):
    n = np.maximum(dist, 0)
    max_exact = N_BUCKETS // 2
    nf = np.maximum(n, 1).astype(np.float32)
    large = max_exact + (np.log(nf / np.float32(max_exact)) / np.float32(math.log(MAX_DISTANCE / max_exact))
                         * np.float32(N_BUCKETS - max_exact)).astype(np.int32)
    large = np.minimum(large, N_BUCKETS - 1)
    return np.where(n < max_exact, n, large).astype(np.int32)


def _bias_prompt_kernel(code_ref, tab_ref, o_ref):
    h = pl.program_id(1)
    code = code_ref[0]
    acc = jnp.full(code.shape, -jnp.inf, F32)
    for b in range(N_BUCKETS):
        acc = jnp.where(code == b, tab_ref[b, h], acc)
    o_ref[0, 0] = acc


def _bias_prompt(table):
    qi = np.arange(WINDOW)[None, :]
    sj = np.arange(WINDOW)[:, None]
    own = sj <= qi
    bucket = _t5_bucket_np(np.where(own, qi - sj, qi + WINDOW - sj))
    code_first = np.where(own, bucket, -1)
    code = jnp.asarray(np.stack([code_first, bucket]).astype(np.int32))
    return pl.pallas_call(
        _bias_prompt_kernel,
        out_shape=jax.ShapeDtypeStruct((2, SWA_H, WINDOW, WINDOW), F32),
        grid=(2, SWA_H),
        in_specs=[pl.BlockSpec((1, WINDOW, WINDOW), lambda v, h: (v, 0, 0)),
                  pl.BlockSpec(memory_space=pltpu.SMEM)],
        out_specs=pl.BlockSpec((1, 1, WINDOW, WINDOW), lambda v, h: (v, h, 0, 0)),
        compiler_params=_cparams(("arbitrary", "arbitrary")),
        name="swa_bias_prompt",
    )(code, table)


def _bias_sample_kernel(code_ref, tab_ref, o_ref):
    kv = pl.program_id(0)
    code = code_ref[...]
    acc = jnp.full(code.shape, -jnp.inf, F32)
    for g in range(SWA_G):
        for b in range(N_BUCKETS):
            acc = jnp.where(code == b + N_BUCKETS * g, tab_ref[b, kv * SWA_G + g], acc)
    o_ref[0] = acc


def _bias_sample(table, n_tok, n_cache, n_keys_pad):
    dist = (n_cache + np.arange(n_tok))[:, None] - np.arange(n_keys_pad)[None, :]
    valid = (dist >= 0) & (dist < WINDOW) & (np.arange(n_keys_pad)[None, :] < n_cache + n_tok)
    bucket = _t5_bucket_np(dist)
    code_t = np.where(valid, bucket, -1)
    g = np.arange(SWA_G)[:, None, None]
    code = np.where(code_t[None] >= 0, code_t[None] + N_BUCKETS * g, -1)
    code = jnp.asarray(code.reshape(SWA_G * n_tok, n_keys_pad).astype(np.int32))
    return pl.pallas_call(
        _bias_sample_kernel,
        out_shape=jax.ShapeDtypeStruct((SWA_KV, SWA_G * n_tok, n_keys_pad), F32),
        grid=(SWA_KV,),
        in_specs=[pl.BlockSpec((SWA_G * n_tok, n_keys_pad), lambda k: (0, 0)),
                  pl.BlockSpec(memory_space=pltpu.SMEM)],
        out_specs=pl.BlockSpec((1, SWA_G * n_tok, n_keys_pad), lambda k: (k, 0, 0)),
        compiler_params=_cparams(("arbitrary",)),
        name="swa_bias_sample",
    )(code, table)


def _sink_softmax_pv(logits, sink, v):
    m = jnp.maximum(jnp.max(logits, axis=-1, keepdims=True), sink)
    p = jnp.exp(logits - m)
    den = jnp.sum(p, axis=-1, keepdims=True) + jnp.exp(sink - m)
    return _dot(p, v) / den


def _swa_prompt_kernel(q_ref, z_ref, kc_ref, kp_ref, vc_ref, vp_ref, bias_ref, sink_ref, o_ref, *, n_qblk):
    kall = jnp.concatenate([kp_ref[...], kc_ref[...]], axis=0)
    vall = jnp.concatenate([vp_ref[...], vc_ref[...]], axis=0)
    scale = SWA_DH ** -0.5
    key = lax.broadcasted_iota(jnp.int32, (WINDOW, WINDOW), 0)
    qry = lax.broadcasted_iota(jnp.int32, (WINDOW, WINDOW), 1)
    from_prev = key > qry
    first_variant = jnp.where(pl.program_id(1) == 0, 0, 1)
    cs = lambda h: slice(h * SWA_DH, (h + 1) * SWA_DH)
    rows = lambda qb: slice(qb * WINDOW, (qb + 1) * WINDOW)
    keys = lambda qb: slice(qb * WINDOW, (qb + 2) * WINDOW)
    for kv in range(SWA_KV):
        ks = slice(kv * SWA_DH, (kv + 1) * SWA_DH)
        k_kv = kall[:, ks].astype(BF16)
        v_t = vall[:, ks].astype(F32).T.astype(BF16)
        probs = [(qb, h) for qb in range(n_qblk) for h in range(kv * SWA_G, (kv + 1) * SWA_G)]
        lg = [_dot_nt(k_kv[keys(qb)], q_ref[rows(qb), cs(h)] * scale) for qb, h in probs]
        ps, dens = [], []
        for (qb, h), l in zip(probs, lg):
            bias = bias_ref[first_variant if qb == 0 else 1, h]
            l = jnp.where(from_prev, l[:WINDOW], l[WINDOW:]) + bias
            sink = sink_ref[h]
            m = jnp.maximum(jnp.max(l, axis=0, keepdims=True), sink)
            p = jnp.exp(l - m)
            dens.append(jnp.sum(p, axis=0, keepdims=True) + jnp.exp(sink - m))
            ps.append(jnp.concatenate([jnp.where(from_prev, p, 0.0), jnp.where(from_prev, 0.0, p)],
                                      axis=0))
        outs = [_dot(v_t[:, keys(qb)], p) / den for p, den, (qb, _) in zip(ps, dens, probs)]
        for j in range(0, len(probs), 2):
            qb, h = probs[j]
            two = slice(h * SWA_DH, (h + 2) * SWA_DH)
            o2 = jnp.concatenate([outs[j], outs[j + 1]], axis=0).T
            o_ref[rows(qb), two] = (o2 * _silu(z_ref[rows(qb), two].astype(F32))).astype(o_ref.dtype)


def _swa_prompt(proj, kv, bias, sinks, n_seq, seq_len, q_colblock, z_colblock, k_colblock, v_colblock,
                n_qblk=1):
    tq = n_qblk * WINDOW
    steps = seq_len // tq
    cur = lambda b, n: b * steps + n
    prev = lambda b, n: (b * steps + n) * n_qblk - jnp.where(n == 0, 0, 1)
    return pl.pallas_call(
        functools.partial(_swa_prompt_kernel, n_qblk=n_qblk),
        out_shape=jax.ShapeDtypeStruct((n_seq * seq_len, SWA_QW), BF16),
        grid=(n_seq, steps),
        in_specs=[pl.BlockSpec((tq, SWA_QW), lambda b, n: (cur(b, n), q_colblock)),
                  pl.BlockSpec((tq, SWA_QW), lambda b, n: (cur(b, n), z_colblock)),
                  pl.BlockSpec((tq, SWA_KVW), lambda b, n: (cur(b, n), k_colblock)),
                  pl.BlockSpec((WINDOW, SWA_KVW), lambda b, n: (prev(b, n), k_colblock)),
                  pl.BlockSpec((tq, SWA_KVW), lambda b, n: (cur(b, n), v_colblock)),
                  pl.BlockSpec((WINDOW, SWA_KVW), lambda b, n: (prev(b, n), v_colblock)),
                  pl.BlockSpec((2, SWA_H, WINDOW, WINDOW), lambda b, n: (0, 0, 0, 0)),
                  pl.BlockSpec(memory_space=pltpu.SMEM)],
        out_specs=pl.BlockSpec((tq, SWA_QW), lambda b, n: (cur(b, n), 0)),
        compiler_params=_cparams(("parallel", "arbitrary")),
        name="swa_prompt",
    )(proj, proj, kv, kv, kv, kv, bias, sinks)


def _swa_sample_kernel(q_ref, z_ref, kn_ref, vn_ref, ck_ref, cv_ref, bias_ref, sink_ref, o_ref, nk_ref, nv_ref,
                       *, n_seq_blk, n_new):
    scale = SWA_DH ** -0.5
    wb = ck_ref.shape[3]
    probs = [(s, kv) for s in range(n_seq_blk) for kv in range(SWA_KV)]
    ks = lambda kv: slice(kv * SWA_DH, (kv + 1) * SWA_DH)
    slot = lax.broadcasted_iota(jnp.int32, (SWA_DH, wb), 1)
    tok = lax.broadcasted_iota(jnp.int32, (kn_ref.shape[1], wb), 0)
    tok_slot = lax.broadcasted_iota(jnp.int32, (kn_ref.shape[1], wb), 1)
    place = jnp.where((tok_slot == tok + (wb - n_new)) & (tok < n_new), 1.0, 0.0).astype(F32)
    for s in range(n_seq_blk):
        for cache_ref, new_ref, out_ref in ((ck_ref, kn_ref, nk_ref), (cv_ref, vn_ref, nv_ref)):
            new_t = lax.dot_general(new_ref[s], place, (((0,), (0,)), ((), ())),
                                    preferred_element_type=F32, precision=lax.Precision.HIGHEST)
            for kv in range(SWA_KV):
                out_ref[s, kv] = jnp.where(slot >= wb - n_new, new_t[ks(kv), :],
                                           pltpu.roll(cache_ref[s, kv], wb - n_new, axis=1))
    lc = [_dot(q_ref[s, kv] * scale, ck_ref[s, kv]) + bias_ref[kv, :, 0:wb] for s, kv in probs]
    ln = [_dot_nt(q_ref[s, kv] * scale, kn_ref[s, :, ks(kv)]) + bias_ref[kv, :, wb:] for s, kv in probs]
    pcs, pns, dens = [], [], []
    for (s, kv), c, n in zip(probs, lc, ln):
        sink = sink_ref[kv][:, 0:1]
        m = jnp.maximum(jnp.maximum(jnp.max(c, axis=-1, keepdims=True), jnp.max(n, axis=-1, keepdims=True)),
                        sink)
        pc = jnp.exp(c - m)
        pn = jnp.exp(n - m)
        pcs.append(pc)
        pns.append(pn)
        dens.append(jnp.sum(pc, axis=-1, keepdims=True) + jnp.sum(pn, axis=-1, keepdims=True)
                    + jnp.exp(sink - m))
    outs = [(_dot_nt(pc, cv_ref[s, kv]) + _dot(pn, vn_ref[s, :, ks(kv)])) / den
            for (s, kv), pc, pn, den in zip(probs, pcs, pns, dens)]
    for oh, (s, kv) in zip(outs, probs):
        o_ref[s, kv] = (oh * _silu(z_ref[s, kv].astype(F32))).astype(o_ref.dtype)


def _swa_sample(q, z, k_new, v_new, cache_kt, cache_vt, bias, sink_rows, n_new, n_seq_blk=8):
    bd, _, rows, _ = q.shape
    wb = cache_kt.shape[3]
    npad = k_new.shape[1]
    assert wb == LANE
    blk4 = lambda i: (i, 0, 0, 0)
    blk3 = lambda i: (i, 0, 0)
    cache_spec = pl.BlockSpec((n_seq_blk, SWA_KV, SWA_DH, wb), blk4)
    return pl.pallas_call(
        functools.partial(_swa_sample_kernel, n_seq_blk=n_seq_blk, n_new=n_new),
        out_shape=(jax.ShapeDtypeStruct(q.shape, BF16),
                   jax.ShapeDtypeStruct(cache_kt.shape, F32), jax.ShapeDtypeStruct(cache_vt.shape, F32)),
        grid=(bd // n_seq_blk,),
        in_specs=[pl.BlockSpec((n_seq_blk, SWA_KV, rows, SWA_DH), blk4),
                  pl.BlockSpec((n_seq_blk, SWA_KV, rows, SWA_DH), blk4),
                  pl.BlockSpec((n_seq_blk, npad, SWA_KVW), blk3),
                  pl.BlockSpec((n_seq_blk, npad, SWA_KVW), blk3),
                  pl.BlockSpec((n_seq_blk, SWA_KV, SWA_DH, wb), blk4),
                  pl.BlockSpec((n_seq_blk, SWA_KV, SWA_DH, wb), blk4),
                  pl.BlockSpec((SWA_KV, rows, wb + npad), lambda i: (0, 0, 0)),
                  pl.BlockSpec((SWA_KV, rows, LANE), lambda i: (0, 0, 0))],
        out_specs=(pl.BlockSpec((n_seq_blk, SWA_KV, rows, SWA_DH), blk4), cache_spec, cache_spec),
        compiler_params=_cparams(("arbitrary",)),
        name="swa_sample",
    )(q, z, k_new, v_new, cache_kt, cache_vt, bias, sink_rows)


def _mem_attend(probs, q_of, z_of, k_of, v_of, store):
    scale = MEM_DH ** -0.5
    logits = [_dot_nt(q_of(p) * scale, k_of(p)) for p in probs]
    ps, dens = [], []
    for l in logits:
        m = jnp.max(l, axis=-1, keepdims=True)
        e = jnp.exp(l - m)
        dens.append(jnp.sum(e, axis=-1, keepdims=True))
        ps.append(e)
    outs = [_dot(e, v_of(p)) / den for e, den, p in zip(ps, dens, probs)]
    for p, oh in zip(probs, outs):
        store(p, oh * _silu(z_of(p).astype(F32)))


def _mem_cols(h):
    return slice(h * MEM_DH, (h + 1) * MEM_DH)


def _mem_prompt_kernel(q_ref, z_ref, k_ref, v_ref, o_ref):
    def store(h, val):
        o_ref[:, _mem_cols(h)] = val.astype(o_ref.dtype)
    _mem_attend(range(MEM_H), lambda h: q_ref[:, _mem_cols(h)], lambda h: z_ref[:, _mem_cols(h)],
                lambda h: k_ref[0, :, _mem_cols(h)], lambda h: v_ref[0, :, _mem_cols(h)], store)


def _mem_prompt(proj, mk, mv, n_seq, seq_len, q_colblock, z_colblock, tq=512):
    steps = seq_len // tq
    return pl.pallas_call(
        _mem_prompt_kernel,
        out_shape=jax.ShapeDtypeStruct((n_seq * seq_len, MEM_W), BF16),
        grid=(n_seq, steps),
        in_specs=[pl.BlockSpec((tq, MEM_W), lambda b, n: (b * steps + n, q_colblock)),
                  pl.BlockSpec((tq, MEM_W), lambda b, n: (b * steps + n, z_colblock)),
                  pl.BlockSpec((1, N_MEM, MEM_W), lambda b, n: (b, 0, 0)),
                  pl.BlockSpec((1, N_MEM, MEM_W), lambda b, n: (b, 0, 0))],
        out_specs=pl.BlockSpec((tq, MEM_W), lambda b, n: (b * steps + n, 0)),
        compiler_params=_cparams(("parallel", "arbitrary")),
        name="mem_prompt",
    )(proj, proj, mk, mv)


def _mem_sample_kernel(q_ref, z_ref, k_hbm, v_hbm, o_ref, kbuf, vbuf, sem, *, n_seq_blk):
    i = pl.program_id(0)
    n_steps = pl.num_programs(0)
    slot = i % 2

    def copies(step, slot_):
        seqs = pl.ds(step * n_seq_blk, n_seq_blk)
        out = []
        for h in range(MEM_H):
            out.append(pltpu.make_async_copy(k_hbm.at[seqs, :, h, :], kbuf.at[slot_, h], sem.at[0, slot_, h]))
            out.append(pltpu.make_async_copy(v_hbm.at[seqs, :, h, :], vbuf.at[slot_, h], sem.at[1, slot_, h]))
        return out

    @pl.when(i == 0)
    def _():
        for cp in copies(0, 0):
            cp.start()

    @pl.when(i + 1 < n_steps)
    def _():
        for cp in copies(i + 1, 1 - slot):
            cp.start()

    for cp in copies(i, slot):
        cp.wait()

    def store(p, val):
        o_ref[p[0], :, _mem_cols(p[1])] = val.astype(o_ref.dtype)
    probs = [(s, h) for s in range(n_seq_blk) for h in range(MEM_H)]
    _mem_attend(probs, lambda p: q_ref[p[0], :, _mem_cols(p[1])], lambda p: z_ref[p[0], :, _mem_cols(p[1])],
                lambda p: kbuf[slot, p[1], p[0]], lambda p: vbuf[slot, p[1], p[0]], store)


def _mem_sample(q, z, cache_k, cache_v, n_seq_blk=4):
    bd, rows, _ = q.shape
    blk = lambda i: (i, 0, 0)
    buf = pltpu.VMEM((2, MEM_H, n_seq_blk, N_MEM, MEM_DH), cache_k.dtype)
    return pl.pallas_call(
        functools.partial(_mem_sample_kernel, n_seq_blk=n_seq_blk),
        out_shape=jax.ShapeDtypeStruct(q.shape, BF16),
        grid=(bd // n_seq_blk,),
        in_specs=[pl.BlockSpec((n_seq_blk, rows, MEM_W), blk),
                  pl.BlockSpec((n_seq_blk, rows, MEM_W), blk),
                  pl.BlockSpec(memory_space=pl.ANY),
                  pl.BlockSpec(memory_space=pl.ANY)],
        out_specs=pl.BlockSpec((n_seq_blk, rows, MEM_W), blk),
        scratch_shapes=[buf, buf, pltpu.SemaphoreType.DMA((2, 2, MEM_H))],
        compiler_params=_cparams(("arbitrary",)),
        name="mem_sample",
    )(q, z, cache_k, cache_v)


def _merge_kernel(og_ref, os_ref, om_ref, gate_ref, x_ref, wb_ref, wo_ref, nf_ref, y_ref, *, n_sub):
    tm = x_ref.shape[0]
    sub = tm // n_sub
    for s in range(n_sub):
        rows = slice(s * sub, (s + 1) * sub)
        merged = None
        for b, o_ref in enumerate((og_ref, os_ref, om_ref)):
            t = jnp.dot(o_ref[rows, :], wb_ref[b], preferred_element_type=F32)
            t = t * gate_ref[rows, b * D_MODEL:(b + 1) * D_MODEL].astype(F32)
            merged = t if merged is None else merged + t
        h = x_ref[rows, :] + jnp.dot(merged.astype(BF16), wo_ref[...], preferred_element_type=F32)
        ms = jnp.mean(h * h, axis=-1, keepdims=True)
        y_ref[rows, :] = h * lax.rsqrt(ms + NORM_EPS) * nf_ref[...]


def _merge(o_gdn, o_swa, o_mem, gates, x, w_branch, w_out, norm_f, tm=256):
    m = x.shape[0]
    tm = min(tm, m)
    row = lambda i: (i, 0)
    const2 = lambda i: (0, 0)
    return pl.pallas_call(
        functools.partial(_merge_kernel, n_sub=1),
        out_shape=jax.ShapeDtypeStruct((m, D_MODEL), F32),
        grid=(m // tm,),
        in_specs=[pl.BlockSpec((tm, BR_W), row),
                  pl.BlockSpec((tm, BR_W), row),
                  pl.BlockSpec((tm, BR_W), row),
                  pl.BlockSpec((tm, N_BRANCH * D_MODEL), row),
                  pl.BlockSpec((tm, D_MODEL), row),
                  pl.BlockSpec((N_BRANCH, BR_W, D_MODEL), lambda i: (0, 0, 0),
                               pipeline_mode=pl.Buffered(1)),
                  pl.BlockSpec((D_MODEL, D_MODEL), const2, pipeline_mode=pl.Buffered(1)),
                  pl.BlockSpec((1, D_MODEL), const2)],
        out_specs=pl.BlockSpec((tm, D_MODEL), row),
        compiler_params=_cparams(("parallel",)),
        name="merge",
    )(o_gdn, o_swa, o_mem, gates, x, w_branch, w_out, norm_f)


_IN_SIZES = (GDN_QK, GDN_QK, GDN_W, GDN_W, GDN_H, GDN_H, SWA_QW, SWA_KVW, SWA_KVW, SWA_QW,
             MEM_W, MEM_W, N_BRANCH * D_MODEL)
_IN_NAMES = ("gq", "gk", "gv", "gz", "gb", "ga", "sq", "sk", "sv", "sz", "mq", "mz", "mg")
_IN_SPAN = {name: (int(off), int(off + size)) for name, off, size in
            zip(_IN_NAMES, np.cumsum((0,) + _IN_SIZES[:-1]), _IN_SIZES)}


def kernel(x_prompt, x_sample, state_gdn, state_gdn_conv, cache_swa_k, cache_swa_v, cache_mem_k,
           cache_mem_v, mem_prompt, norm_in, w_in, gdn_conv_w, gdn_a_log, gdn_dt_bias, gdn_norm,
           swa_sinks, rel_bias, norm_mem, w_mem_kv, w_branch, w_out, norm_f):
    n_layers = norm_in.shape[0]
    assert n_layers == 1
    b, seq, _ = x_prompt.shape
    bd, ns, _ = x_sample.shape
    wb = cache_swa_k.shape[2]
    assert seq % WINDOW == 0 and seq % GDN_CHUNK == 0 and ns + CONV_W <= SUBLANE and wb == WINDOW
    lyr = 0

    w = w_in[lyr]
    main_names = ("gq", "gk", "gv", "gz", "sq", "sz", "mq", "mz")
    n_main = len(main_names) * W_BLK
    n_gate = N_BRANCH * D_MODEL
    n_small = 3 * LANE
    small_lead = -(n_main + n_gate) % n_small
    w_all = _wprep(w.T, [[(_IN_SPAN[a][0], W_BLK)] for a in main_names]
                   + [[(_IN_SPAN["mg"][0] + W_BLK * k, W_BLK)] for k in range(n_gate // W_BLK)]
                   + [[(None, small_lead), (_IN_SPAN["sk"][0], 2 * SWA_KVW), (_IN_SPAN["gb"][0], 2 * GDN_H)]])
    col_small = n_main + n_gate + small_lead
    cb_gz, cb_sq, cb_sz, cb_mq, cb_mz = 3, 4, 5, 6, 7
    cb_sk, cb_sv, cb_ba = 0, 1, 2
    w_mkv = w_mem_kv[lyr].astype(BF16)
    w_br = w_branch[lyr].astype(BF16)
    w_o = w_out[lyr].astype(BF16)
    nw_in = norm_in[lyr].reshape(1, D_MODEL)
    nw_mem = norm_mem[lyr].reshape(1, D_MODEL)
    nw_f = norm_f.reshape(1, D_MODEL)
    conv_w = gdn_conv_w[lyr]
    a_row = jnp.pad(gdn_a_log[lyr].reshape(1, GDN_H), ((0, 0), (GDN_H, LANE - 2 * GDN_H)))
    dt_row = jnp.pad(gdn_dt_bias[lyr].reshape(1, GDN_H), ((0, 0), (GDN_H, LANE - 2 * GDN_H)))
    gnw = gdn_norm[lyr].reshape(1, GDN_DV)
    sinks = swa_sinks[lyr]
    bias_p = _bias_prompt(rel_bias)
    npad = SUBLANE
    bias_s = _bias_sample(rel_bias, ns, wb, wb + npad)
    sink_rows = jnp.broadcast_to(jnp.repeat(sinks.reshape(SWA_KV, SWA_G), ns, axis=1)[:, :, None],
                                 (SWA_KV, SWA_G * ns, LANE))

    t = b * seq
    xp = x_prompt.reshape(t, D_MODEL)
    xn_p, p_small = _rmsnorm(xp, nw_in, w_all, col_small, n_small)
    p_main = _proj(xn_p, w_all, BF16, n=n_main)
    g_p = _proj(xn_p, w_all, BF16, act="sigmoid", col0=n_main, n=n_gate)
    xn_tail = xn_p.reshape(b, seq, D_MODEL)[:, seq - SUBLANE:, :].reshape(b * SUBLANE, D_MODEL)
    conv_p = _proj(xn_tail, w_all, F32, n=GDN_CONV_CH).reshape(b, SUBLANE, GDN_CONV_CH)[:, SUBLANE - (CONV_W - 1):]
    kv_tail = p_small.reshape(b, seq, 3 * LANE)[:, seq - WINDOW:]
    swk_p = kv_tail[:, :, cb_sk * LANE:(cb_sk + 1) * LANE].reshape(b, WINDOW, SWA_KV, SWA_DH)
    swv_p = kv_tail[:, :, cb_sv * LANE:(cb_sv + 1) * LANE].reshape(b, WINDOW, SWA_KV, SWA_DH)

    mkv = _proj(_rmsnorm(mem_prompt.reshape(b * N_MEM, D_MODEL), nw_mem), w_mkv, F32)
    mk_p = mkv[:, :MEM_W].reshape(b, N_MEM, MEM_W)
    mv_p = mkv[:, MEM_W:].reshape(b, N_MEM, MEM_W)

    o_gdn_p, s_p = _gdn(p_main, p_main, p_small, jnp.zeros((b, HDR, GDN_CONV_CH), F32),
                        jnp.zeros((b, GDN_H, GDN_DK, GDN_DV), F32), conv_w, a_row, dt_row, gnw,
                        n_seq=b, chunk=GDN_CHUNK, n_chunks=4, per_chunk_state=False, group=4, unroll=True,
                        valid_lo=0, valid_hi=GDN_CHUNK, out_dtype=BF16,
                        z_colblock=cb_gz, ba_colblock=cb_ba)
    o_swa_p = _swa_prompt(p_main, p_small, bias_p, sinks, b, seq, cb_sq, cb_sz, cb_sk, cb_sv)
    o_mem_p = _mem_prompt(p_main, mk_p.astype(BF16), mv_p.astype(BF16), b, seq, cb_mq, cb_mz)
    y_p = _merge(o_gdn_p, o_swa_p, o_mem_p, g_p, xp, w_br, w_o, nw_f).reshape(b, seq, D_MODEL)

    ts = bd * ns
    xs = x_sample.reshape(ts, D_MODEL)
    xn_s, s_small = _rmsnorm(xs, nw_in, w_all, col_small, n_small)
    s_main = _proj(xn_s, w_all, F32, n=n_main)
    g_s = _proj(xn_s, w_all, BF16, act="sigmoid", col0=n_main, n=n_gate)
    s_gdn = s_main[:, :GDN_CONV_CH + GDN_W]
    s_ba = s_small[:, cb_ba * LANE:(cb_ba + 1) * LANE]
    s_swa = jnp.concatenate([s_main[:, cb_sq * BR_W:(cb_sz + 1) * BR_W], s_small[:, :2 * LANE]], axis=1)
    s_mem = s_main[:, cb_mq * BR_W:(cb_mz + 1) * BR_W].astype(BF16)

    lo = CONV_W - 1
    hi = lo + ns
    pad_rows = ((0, 0), (lo, SUBLANE - hi), (0, 0))
    e_qkv = jnp.concatenate([state_gdn_conv[lyr], s_gdn[:, :GDN_CONV_CH].reshape(bd, ns, GDN_CONV_CH),
                             jnp.zeros((bd, SUBLANE - hi, GDN_CONV_CH), F32)], axis=1)
    e_z = jnp.pad(s_gdn[:, GDN_CONV_CH:].reshape(bd, ns, GDN_W), pad_rows)
    e_ba = jnp.pad(s_ba.reshape(bd, ns, LANE), pad_rows)
    seq_blk = 8
    o_gdn_s8, s_s = _gdn(e_qkv.reshape(bd * SUBLANE, GDN_CONV_CH), e_z.reshape(bd * SUBLANE, GDN_W),
                         e_ba.reshape(bd * SUBLANE, LANE), jnp.zeros((1, HDR, GDN_CONV_CH), F32),
                         state_gdn[lyr], conv_w, a_row, dt_row, gnw,
                         n_seq=bd, chunk=SUBLANE, n_chunks=seq_blk, per_chunk_state=True, group=8,
                         unroll=True,
                         valid_lo=lo, valid_hi=hi, out_dtype=BF16)
    o_gdn_s = o_gdn_s8.reshape(bd, SUBLANE, GDN_W)[:, lo:hi].reshape(ts, GDN_W)
    conv_s = e_qkv[:, hi - (CONV_W - 1):hi]

    def to_heads(a):
        return a.reshape(bd, ns, SWA_KV, SWA_G, SWA_DH).transpose(0, 2, 3, 1, 4).reshape(
            bd, SWA_KV, SWA_G * ns, SWA_DH)

    k_new = s_swa[:, 2 * SWA_QW:2 * SWA_QW + SWA_KVW].reshape(bd, ns, SWA_KVW)
    v_new = s_swa[:, 2 * SWA_QW + SWA_KVW:].reshape(bd, ns, SWA_KVW)
    tok_pad = ((0, 0), (0, npad - ns), (0, 0))
    ck_t = cache_swa_k[lyr].transpose(0, 2, 3, 1)
    cv_t = cache_swa_v[lyr].transpose(0, 2, 3, 1)
    o_swa_h, nk_t, nv_t = _swa_sample(
        to_heads(s_swa[:, :SWA_QW]).astype(BF16), to_heads(s_swa[:, SWA_QW:2 * SWA_QW]),
        jnp.pad(k_new, tok_pad), jnp.pad(v_new, tok_pad), ck_t, cv_t, bias_s, sink_rows, n_new=ns)
    o_swa_s = o_swa_h.reshape(bd, SWA_KV, SWA_G, ns, SWA_DH).transpose(0, 3, 1, 2, 4).reshape(ts, SWA_QW)
    swk_s = nk_t.transpose(0, 3, 1, 2)
    swv_s = nv_t.transpose(0, 3, 1, 2)

    mq = jnp.pad(s_mem[:, :MEM_W].reshape(bd, ns, MEM_W), tok_pad)
    mz = jnp.pad(s_mem[:, MEM_W:].reshape(bd, ns, MEM_W), tok_pad)
    o_mem_s = _mem_sample(mq, mz, cache_mem_k[lyr], cache_mem_v[lyr])[:, :ns].reshape(ts, MEM_W)
    y_s = _merge(o_gdn_s, o_swa_s, o_mem_s, g_s, xs, w_br, w_o, nw_f).reshape(bd, ns, D_MODEL)

    return (y_p, y_s,
            s_p[None], conv_p[None], swk_p[None], swv_p[None],
            mk_p.reshape(b, N_MEM, MEM_H, MEM_DH)[None], mv_p.reshape(b, N_MEM, MEM_H, MEM_DH)[None],
            s_s[None], conv_s[None], swk_s[None], swv_s[None])
```

```python
import functools
import math

import numpy as np
import jax
import jax.numpy as jnp
from jax import lax
from jax.experimental import pallas as pl
from jax.experimental.pallas import tpu as pltpu

F32 = jnp.float32
BF16 = jnp.bfloat16

D_MODEL = 2048
N_BRANCH = 3
BR_W = 1024
GDN_H = 8
GDN_DK = 128
GDN_DV = 128
GDN_QK = GDN_H * GDN_DK
GDN_W = GDN_H * GDN_DV
GDN_CONV_CH = 2 * GDN_QK + GDN_W
CONV_W = 4
GDN_CHUNK = 64
SWA_H = 16
SWA_KV = 2
SWA_G = SWA_H // SWA_KV
SWA_DH = 64
SWA_QW = SWA_H * SWA_DH
SWA_KVW = SWA_KV * SWA_DH
WINDOW = 128
N_BUCKETS = 32
MAX_DISTANCE = 128
N_MEM = 256
MEM_H = 4
MEM_DH = 256
MEM_W = MEM_H * MEM_DH
NORM_EPS = 1e-6

LANE = 128
SUBLANE = 8
VMEM_LIMIT = 52 * 1024 * 1024


def _cparams(sem):
    return pltpu.CompilerParams(dimension_semantics=sem, vmem_limit_bytes=VMEM_LIMIT)


def _sigmoid(x):
    return 0.5 * jnp.tanh(0.5 * x) + 0.5


def _silu(x):
    h = 0.5 * x
    return h * jnp.tanh(h) + h


def _softplus(x):
    return jnp.maximum(x, 0.0) + jnp.log(1.0 + jnp.exp(-jnp.abs(x)))


def _dot(a, b):
    return jnp.dot(a.astype(BF16), b.astype(BF16), preferred_element_type=F32)


def _dot_nt(a, b):
    return lax.dot_general(a.astype(BF16), b.astype(BF16), (((1,), (1,)), ((), ())),
                           preferred_element_type=F32)


def _dot_tn(a, b):
    return lax.dot_general(a.astype(BF16), b.astype(BF16), (((0,), (0,)), ((), ())),
                           preferred_element_type=F32)


def _dot_f32(a, b):
    return jnp.dot(a, b, preferred_element_type=F32, precision=lax.Precision.HIGHEST)


def _rmsnorm_kernel(x_ref, nw_ref, *rest):
    x = x_ref[...].astype(F32)
    ms = jnp.mean(x * x, axis=-1, keepdims=True)
    xn = (x * lax.rsqrt(ms + NORM_EPS) * nw_ref[...]).astype(BF16)
    if len(rest) == 1:
        rest[0][...] = xn
    else:
        w_ref, o_ref, p_ref = rest
        o_ref[...] = xn
        p_ref[...] = jnp.dot(xn, w_ref[...], preferred_element_type=F32)


def _rmsnorm(x, norm_w, w=None, col0=0, n=None, tm_pref=512):
    m, d = x.shape
    tm = min(m, tm_pref)
    assert m % tm == 0
    row = lambda i: (i, 0)
    in_specs = [pl.BlockSpec((tm, d), row), pl.BlockSpec((1, d), lambda i: (0, 0))]
    out_shape = jax.ShapeDtypeStruct((m, d), BF16)
    out_specs = pl.BlockSpec((tm, d), row)
    args = (x, norm_w)
    if w is not None:
        assert col0 % n == 0
        in_specs.append(pl.BlockSpec((d, n), lambda i: (0, col0 // n)))
        out_shape = (out_shape, jax.ShapeDtypeStruct((m, n), F32))
        out_specs = (out_specs, pl.BlockSpec((tm, n), row))
        args = args + (w,)
    return pl.pallas_call(
        _rmsnorm_kernel,
        out_shape=out_shape,
        grid=(m // tm,),
        in_specs=in_specs,
        out_specs=out_specs,
        compiler_params=_cparams(("parallel",)),
        name="rmsnorm",
    )(*args)


def _proj_kernel(x_ref, w_ref, o_ref, *, act):
    y = jnp.dot(x_ref[...], w_ref[...], preferred_element_type=F32)
    if act == "sigmoid":
        y = _sigmoid(y)
    o_ref[...] = y.astype(o_ref.dtype)


def _pick_tile(n, pref):
    t = min(n, pref)
    while n % t:
        t -= LANE
    return t


def _proj(xn, w, out_dtype, act=None, col0=0, n=None, tm_pref=1024, tn_pref=2048):
    m, d = xn.shape
    n = w.shape[1] if n is None else n
    tm = min(m, tm_pref)
    assert m % tm == 0
    tn = _pick_tile(n, tn_pref if out_dtype == BF16 else tn_pref // 2)
    while col0 % tn:
        tn = _pick_tile(n, tn - LANE)
    jb = col0 // tn
    return pl.pallas_call(
        functools.partial(_proj_kernel, act=act),
        out_shape=jax.ShapeDtypeStruct((m, n), out_dtype),
        grid=(m // tm, n // tn),
        in_specs=[pl.BlockSpec((tm, d), lambda i, j: (i, 0)),
                  pl.BlockSpec((d, tn), lambda i, j: (0, jb + j))],
        out_specs=pl.BlockSpec((tm, tn), lambda i, j: (i, j)),
        compiler_params=_cparams(("parallel", "arbitrary")),
        name="proj",
    )(xn, w)


W_BLK = 1024
W_PIECE_ROWS = (W_BLK, 2 * SUBLANE)


def _wprep_kernel(wt_hbm, o_ref, inbuf0, inbuf1, sem, *, blocks):
    j = pl.program_id(0)
    inbuf = (inbuf0, inbuf1)

    def sources(b):
        return [(start, rows) for start, rows in blocks[b] if start is not None]

    def pieces(b):
        return [pltpu.make_async_copy(wt_hbm.at[pl.ds(start, rows), :],
                                      inbuf[k].at[b % 2, pl.ds(0, rows), :], sem.at[b % 2, k])
                for k, (start, rows) in enumerate(sources(b))]

    for b, ranges in enumerate(blocks):
        @pl.when(j == b)
        def _(b=b, ranges=ranges):
            if b == 0:
                for cp in pieces(0):
                    cp.start()
            if b + 1 < len(blocks):
                for cp in pieces(b + 1):
                    cp.start()
            for cp in pieces(b):
                cp.wait()
            n_have = sum(rows for _, rows in ranges)
            parts, k = [], 0
            for start, rows in tuple(ranges) + ((None, W_BLK - n_have),):
                if start is None:
                    if rows:
                        parts.append(jnp.zeros((rows, inbuf0.shape[2]), inbuf0.dtype))
                else:
                    parts.append(inbuf[k][b % 2, 0:rows, :])
                    k += 1
            val = parts[0] if len(parts) == 1 else jnp.concatenate(parts, axis=0)
            o_ref[...] = val.T.astype(BF16)


def _wprep(wt, block_srcs):
    ncol, d = wt.shape
    for ranges in block_srcs:
        srcs = [(s, rows) for s, rows in ranges if s is not None]
        assert len(srcs) <= len(W_PIECE_ROWS) and all(rows % SUBLANE == 0 for _, rows in ranges)
        for (s, rows), cap in zip(srcs, W_PIECE_ROWS):
            assert s % SUBLANE == 0 and rows <= cap and s + rows <= ncol
    return pl.pallas_call(
        functools.partial(_wprep_kernel, blocks=tuple(tuple(r) for r in block_srcs)),
        out_shape=jax.ShapeDtypeStruct((d, W_BLK * len(block_srcs)), BF16),
        grid=(len(block_srcs),),
        in_specs=[pl.BlockSpec(memory_space=pl.ANY)],
        out_specs=pl.BlockSpec((d, W_BLK), lambda j: (0, j)),
        scratch_shapes=[pltpu.VMEM((2, cap, d), wt.dtype) for cap in W_PIECE_ROWS]
                       + [pltpu.SemaphoreType.DMA((2, len(W_PIECE_ROWS)))],
        compiler_params=_cparams(("arbitrary",)),
        name="wprep",
    )(wt)


HDR = SUBLANE


def _tri_inv_many(a_list, c):
    row = lax.broadcasted_iota(jnp.int32, (c, c), 0)
    col = lax.broadcasted_iota(jnp.int32, (c, c), 1)
    eye = jnp.where(row == col, 1.0, 0.0).astype(F32)
    xs = [eye - a for a in a_list]
    bs = [_dot(a, a) for a in a_list]
    n = 2
    while n < c:
        xs = [x + _dot(x, b) for x, b in zip(xs, bs)]
        n *= 2
        if n < c:
            bs = [_dot(b, b) for b in bs]
    return xs


def _gdn_kernel(qkv_ref, z_ref, ba_ref, hdr_ref, s0_ref, cw_ref, arow_ref, dtrow_ref, nw_ref,
                o_ref, s_ref, buf_ref, cv_ref, gb_ref, oacc_ref, sol_ref, att_ref, gl_ref,
                *, chunk, n_chunks, group, unroll, per_chunk_state, valid_lo, valid_hi):
    c = chunk
    tb = c * n_chunks

    if per_chunk_state:
        buf_ref[0:2 * HDR, :] = jnp.zeros((2 * HDR, GDN_CONV_CH), F32)
    else:
        @pl.when(pl.program_id(1) == 0)
        def _():
            buf_ref[0:HDR, :] = hdr_ref[0]
            buf_ref[HDR:2 * HDR, :] = jnp.zeros((HDR, GDN_CONV_CH), F32)
            s_ref[...] = s0_ref[...]

    def tap_from_history(j, n_rows):
        off = HDR - (CONV_W - 1) + j
        return buf_ref[off:off + n_rows, :] * cw_ref[j:j + 1, :]

    if qkv_ref.dtype == BF16:
        xb = qkv_ref[...]
        r = lax.broadcasted_iota(jnp.int32, (tb, tb), 0)
        cc = lax.broadcasted_iota(jnp.int32, (tb, tb), 1)
        acc = xb.astype(F32) * cw_ref[CONV_W - 1:CONV_W, :]
        for j in range(CONV_W - 1):
            shift = jnp.where(r - cc == CONV_W - 1 - j, 1.0, 0.0).astype(BF16)
            acc = acc + jnp.dot(shift, xb, preferred_element_type=F32) * cw_ref[j:j + 1, :]
        cv_ref[...] = _silu(acc)
        top = acc[0:HDR]
        for j in range(CONV_W - 1):
            top = top + tap_from_history(j, HDR)
        cv_ref[0:HDR, :] = _silu(top)
        buf_ref[0:HDR, :] = qkv_ref[tb - 2 * HDR:tb, :].astype(F32)[HDR:]
    else:
        buf_ref[HDR:HDR + tb, :] = qkv_ref[...].astype(F32)
        acc = None
        for j in range(CONV_W):
            term = tap_from_history(j, tb)
            acc = term if acc is None else acc + term
        cv_ref[...] = _silu(acc)
        if not per_chunk_state:
            buf_ref[0:HDR, :] = buf_ref[tb:tb + HDR, :]

    ba = ba_ref[...].astype(F32)
    beta_all = _sigmoid(ba)
    g_all = -jnp.exp(arow_ref[...]) * _softplus(ba + dtrow_ref[...])
    if per_chunk_state:
        r = lax.broadcasted_iota(jnp.int32, (tb, LANE), 0) & (c - 1)
        valid = (r >= valid_lo) & (r < valid_hi)
        beta_all = jnp.where(valid, beta_all, 0.0)
        g_all = jnp.where(valid, g_all, 0.0)
    gb_ref[0] = beta_all
    gb_ref[1] = g_all

    row = lax.broadcasted_iota(jnp.int32, (c, c), 0)
    col = lax.broadcasted_iota(jnp.int32, (c, c), 1)
    causal = row >= col
    strict = row > col
    tril = jnp.where(causal, 1.0, 0.0).astype(F32)
    scale_q = GDN_DK ** -0.5

    heads = range(GDN_H)
    qcol = lambda h: slice(h * GDN_DK, (h + 1) * GDN_DK)
    kcol = lambda h: slice(GDN_QK + h * GDN_DK, GDN_QK + (h + 1) * GDN_DK)
    vcol = lambda h: slice(2 * GDN_QK + h * GDN_DV, 2 * GDN_QK + (h + 1) * GDN_DV)
    ucol = lambda h: slice(h * (GDN_DV + GDN_DK), h * (GDN_DV + GDN_DK) + GDN_DV)
    wcol = lambda h: slice(h * (GDN_DV + GDN_DK) + GDN_DV, (h + 1) * (GDN_DV + GDN_DK))
    if per_chunk_state:
        rv = lax.broadcasted_iota(jnp.int32, (c, 1), 0)
        rvalid = (rv >= valid_lo) & (rv < valid_hi)

    def chunk_rows(ci):
        if isinstance(ci, int):
            return slice(ci * c, (ci + 1) * c)
        return pl.ds(pl.multiple_of(ci * c, c), c)

    def prep_body(gi, carry):
        probs = []
        for j in range(group):
            rows = chunk_rows(gi * group + j)
            beta_c = gb_ref[0, rows, :]
            gc_all = _dot_f32(tril, gb_ref[1, rows, :])
            gc_t = gc_all.T
            gl_ref[gi * group + j] = gc_all[c - 1:c, :]
            for h in heads:
                q = cv_ref[rows, qcol(h)]
                k = cv_ref[rows, kcol(h)]
                v = cv_ref[rows, vcol(h)]
                q = q * lax.rsqrt(jnp.sum(q * q, axis=-1, keepdims=True) + NORM_EPS) * scale_q
                k = k * lax.rsqrt(jnp.sum(k * k, axis=-1, keepdims=True) + NORM_EPS)
                if per_chunk_state:
                    q = jnp.where(rvalid, q, 0.0)
                    k = jnp.where(rvalid, k, 0.0)
                    v = jnp.where(rvalid, v, 0.0)
                beta = beta_c[:, h:h + 1]
                gc_col = gc_all[:, GDN_H + h:GDN_H + h + 1]
                gc_row = gc_t[GDN_H + h:GDN_H + h + 1, :]
                diff = jnp.where(causal, gc_col - gc_row, 0.0)
                decay = jnp.where(causal, jnp.exp(diff), 0.0)
                e_gc = jnp.exp(gc_col)
                kb = k * beta
                rhs = jnp.concatenate([v * beta, kb * e_gc], axis=1)
                cv_ref[rows, qcol(h)] = q * e_gc
                cv_ref[rows, kcol(h)] = k * jnp.exp(gc_col[c - 1:c, :] - gc_col)
                probs.append((rows, h, q, k, kb, rhs, decay))
        kq = [_dot_nt(jnp.concatenate([kb, q], axis=0), k) for (_, _, q, k, kb, _, _) in probs]
        a_list = [jnp.where(strict, kq_i[:c] * p[6], 0.0) for kq_i, p in zip(kq, probs)]
        t_inv = _tri_inv_many(a_list, c)
        for t_i, kq_i, (rows, h, _, _, _, rhs, decay) in zip(t_inv, kq, probs):
            sol_ref[rows, h * (GDN_DV + GDN_DK):(h + 1) * (GDN_DV + GDN_DK)] = _dot(t_i, rhs)
            att_ref[h, rows, :] = kq_i[c:] * decay
        return carry

    def scan_body(ci, carry):
        rows = chunk_rows(ci)
        si = ci if per_chunk_state else 0
        g_tot = jnp.exp(gl_ref[ci])
        s_old = [s_ref[si, h] for h in heads]
        wq_s = [_dot(jnp.concatenate([sol_ref[rows, wcol(h)], cv_ref[rows, qcol(h)]], axis=0), s_old[h])
                for h in heads]
        v_new = [sol_ref[rows, ucol(h)] - wq_s[h][:c] for h in heads]
        o_att = [_dot(att_ref[h, rows, :], v_new[h]) for h in heads]
        for h in heads:
            s_ref[si, h] = (s_old[h] * g_tot[:, GDN_H + h:GDN_H + h + 1]
                            + _dot_tn(cv_ref[rows, kcol(h)], v_new[h]))
        for h in heads:
            o = wq_s[h][c:] + o_att[h]
            o = o * lax.rsqrt(jnp.mean(o * o, axis=-1, keepdims=True) + NORM_EPS) * nw_ref[...]
            oacc_ref[rows, h * GDN_DV:(h + 1) * GDN_DV] = o
        return carry

    if per_chunk_state:
        s_ref[...] = s0_ref[...]
    if unroll:
        for gi in range(n_chunks // group):
            prep_body(gi, 0)
        for ci in range(n_chunks):
            scan_body(ci, 0)
    else:
        lax.fori_loop(0, n_chunks // group, prep_body, 0)
        lax.fori_loop(0, n_chunks, scan_body, 0)
    o_ref[...] = (oacc_ref[...] * _silu(z_ref[...].astype(F32))).astype(o_ref.dtype)


def _gdn(qkv, z, ba, hdr, s0, conv_w, a_row, dt_row, norm_w, *, n_seq, chunk, n_chunks,
         per_chunk_state, valid_lo, valid_hi, out_dtype, z_colblock=0, ba_colblock=0, group=2,
         unroll=False):
    rows = qkv.shape[0]
    tb = chunk * n_chunks
    if per_chunk_state:
        grid = (rows // tb,)
        sem = ("arbitrary",)
        rmap = lambda i: (i, 0)
        zmap = lambda i: (i, z_colblock)
        bmap = lambda i: (i, ba_colblock)
        hmap = lambda i: (0, 0, 0)
        smap = lambda i: (i, 0, 0, 0)
        cmap = lambda i: (0, 0)
        ns = n_chunks
    else:
        steps = rows // n_seq // tb
        grid = (n_seq, steps)
        sem = ("parallel", "arbitrary")
        rmap = lambda b, n: (b * steps + n, 0)
        zmap = lambda b, n: (b * steps + n, z_colblock)
        bmap = lambda b, n: (b * steps + n, ba_colblock)
        hmap = lambda b, n: (b, 0, 0)
        smap = lambda b, n: (b, 0, 0, 0)
        cmap = lambda b, n: (0, 0)
        ns = 1
    assert n_chunks % group == 0
    kern = functools.partial(_gdn_kernel, chunk=chunk, n_chunks=n_chunks, group=group, unroll=unroll,
                             per_chunk_state=per_chunk_state, valid_lo=valid_lo, valid_hi=valid_hi)
    return pl.pallas_call(
        kern,
        out_shape=(jax.ShapeDtypeStruct((rows, GDN_W), out_dtype),
                   jax.ShapeDtypeStruct(s0.shape, F32)),
        grid=grid,
        in_specs=[pl.BlockSpec((tb, GDN_CONV_CH), rmap),
                  pl.BlockSpec((tb, GDN_W), zmap),
                  pl.BlockSpec((tb, LANE), bmap),
                  pl.BlockSpec((1, HDR, GDN_CONV_CH), hmap),
                  pl.BlockSpec((ns, GDN_H, GDN_DK, GDN_DV), smap),
                  pl.BlockSpec((CONV_W, GDN_CONV_CH), cmap),
                  pl.BlockSpec((1, LANE), cmap),
                  pl.BlockSpec((1, LANE), cmap),
                  pl.BlockSpec((1, GDN_DV), cmap)],
        out_specs=(pl.BlockSpec((tb, GDN_W), rmap),
                   pl.BlockSpec((ns, GDN_H, GDN_DK, GDN_DV), smap)),
        scratch_shapes=[pltpu.VMEM((HDR + tb, GDN_CONV_CH), F32),
                        pltpu.VMEM((tb, GDN_CONV_CH), F32),
                        pltpu.VMEM((2, tb, LANE), F32),
                        pltpu.VMEM((tb, GDN_W), F32),
                        pltpu.VMEM((tb, GDN_H * (GDN_DV + GDN_DK)), F32),
                        pltpu.VMEM((GDN_H, tb, chunk), F32),
                        pltpu.VMEM((n_chunks, 1, LANE), F32)],
        compiler_params=_cparams(sem),
        name="gdn",
    )(qkv, z, ba, hdr, s0, conv_w, a_row, dt_row, norm_w)


def _t5_bucket_np(dist):
    n = np.maximum(dist, 0)
    max_exact = N_BUCKETS // 2
    nf = np.maximum(n, 1).astype(np.float32)
    large = max_exact + (np.log(nf / np.float32(max_exact)) / np.float32(math.log(MAX_DISTANCE / max_exact))
                         * np.float32(N_BUCKETS - max_exact)).astype(np.int32)
    large = np.minimum(large, N_BUCKETS - 1)
    return np.where(n < max_exact, n, large).astype(np.int32)


def _bias_prompt_kernel(code_ref, tab_ref, o_ref):
    code = code_ref[0]
    for h in range(SWA_H):
        acc = jnp.full(code.shape, -jnp.inf, F32)
        for b in range(N_BUCKETS):
            acc = jnp.where(code == b, tab_ref[b, h], acc)
        o_ref[0, h] = acc


def _bias_prompt(table):
    qi = np.arange(WINDOW)[None, :]
    sj = np.arange(WINDOW)[:, None]
    own = sj <= qi
    bucket = _t5_bucket_np(np.where(own, qi - sj, qi + WINDOW - sj))
    code_first = np.where(own, bucket, -1)
    code = jnp.asarray(np.stack([code_first, bucket]).astype(np.int32))
    return pl.pallas_call(
        _bias_prompt_kernel,
        out_shape=jax.ShapeDtypeStruct((2, SWA_H, WINDOW, WINDOW), F32),
        grid=(2,),
        in_specs=[pl.BlockSpec((1, WINDOW, WINDOW), lambda v: (v, 0, 0)),
                  pl.BlockSpec(memory_space=pltpu.SMEM)],
        out_specs=pl.BlockSpec((1, SWA_H, WINDOW, WINDOW), lambda v: (v, 0, 0, 0)),
        compiler_params=_cparams(("arbitrary",)),
        name="swa_bias_prompt",
    )(code, table)


def _bias_sample_kernel(code_ref, tab_ref, o_ref):
    kv = pl.program_id(0)
    code = code_ref[...]
    acc = jnp.full(code.shape, -jnp.inf, F32)
    for g in range(SWA_G):
        for b in range(N_BUCKETS):
            acc = jnp.where(code == b + N_BUCKETS * g, tab_ref[b, kv * SWA_G + g], acc)
    o_ref[0] = acc


def _bias_sample(table, n_tok, n_cache, n_keys_pad):
    dist = (n_cache + np.arange(n_tok))[:, None] - np.arange(n_keys_pad)[None, :]
    valid = (dist >= 0) & (dist < WINDOW) & (np.arange(n_keys_pad)[None, :] < n_cache + n_tok)
    bucket = _t5_bucket_np(dist)
    code_t = np.where(valid, bucket, -1)
    g = np.arange(SWA_G)[:, None, None]
    code = np.where(code_t[None] >= 0, code_t[None] + N_BUCKETS * g, -1)
    code = jnp.asarray(code.reshape(SWA_G * n_tok, n_keys_pad).astype(np.int32))
    return pl.pallas_call(
        _bias_sample_kernel,
        out_shape=jax.ShapeDtypeStruct((SWA_KV, SWA_G * n_tok, n_keys_pad), F32),
        grid=(SWA_KV,),
        in_specs=[pl.BlockSpec((SWA_G * n_tok, n_keys_pad), lambda k: (0, 0)),
                  pl.BlockSpec(memory_space=pltpu.SMEM)],
        out_specs=pl.BlockSpec((1, SWA_G * n_tok, n_keys_pad), lambda k: (k, 0, 0)),
        compiler_params=_cparams(("arbitrary",)),
        name="swa_bias_sample",
    )(code, table)


def _sink_softmax_pv(logits, sink, v):
    m = jnp.maximum(jnp.max(logits, axis=-1, keepdims=True), sink)
    p = jnp.exp(logits - m)
    den = jnp.sum(p, axis=-1, keepdims=True) + jnp.exp(sink - m)
    return _dot(p, v) / den


def _swa_prompt_kernel(q_ref, z_ref, kc_ref, kp_ref, vc_ref, vp_ref, bias_ref, sink_ref, o_ref, *, n_qblk):
    kall = jnp.concatenate([kp_ref[...], kc_ref[...]], axis=0)
    vall = jnp.concatenate([vp_ref[...], vc_ref[...]], axis=0)
    scale = SWA_DH ** -0.5
    key = lax.broadcasted_iota(jnp.int32, (WINDOW, WINDOW), 0)
    qry = lax.broadcasted_iota(jnp.int32, (WINDOW, WINDOW), 1)
    from_prev = key > qry
    first_variant = jnp.where(pl.program_id(1) == 0, 0, 1)
    cs = lambda h: slice(h * SWA_DH, (h + 1) * SWA_DH)
    rows = lambda qb: slice(qb * WINDOW, (qb + 1) * WINDOW)
    keys = lambda qb: slice(qb * WINDOW, (qb + 2) * WINDOW)
    for kv in range(SWA_KV):
        ks = slice(kv * SWA_DH, (kv + 1) * SWA_DH)
        k_kv = kall[:, ks].astype(BF16)
        v_t = vall[:, ks].astype(F32).T.astype(BF16)
        probs = [(qb, h) for qb in range(n_qblk) for h in range(kv * SWA_G, (kv + 1) * SWA_G)]
        lg = [_dot_nt(k_kv[keys(qb)], q_ref[rows(qb), cs(h)] * scale) for qb, h in probs]
        ps, dens = [], []
        for (qb, h), l in zip(probs, lg):
            bias = bias_ref[first_variant if qb == 0 else 1, h]
            l = jnp.where(from_prev, l[:WINDOW], l[WINDOW:]) + bias
            sink = sink_ref[h]
            m = jnp.maximum(jnp.max(l, axis=0, keepdims=True), sink)
            p = jnp.exp(l - m)
            dens.append(jnp.sum(p, axis=0, keepdims=True) + jnp.exp(sink - m))
            ps.append(jnp.concatenate([jnp.where(from_prev, p, 0.0), jnp.where(from_prev, 0.0, p)],
                                      axis=0))
        outs = [_dot(v_t[:, keys(qb)], p) / den for p, den, (qb, _) in zip(ps, dens, probs)]
        for j in range(0, len(probs), 2):
            qb, h = probs[j]
            two = slice(h * SWA_DH, (h + 2) * SWA_DH)
            o2 = jnp.concatenate([outs[j], outs[j + 1]], axis=0).T
            o_ref[rows(qb), two] = (o2 * _silu(z_ref[rows(qb), two].astype(F32))).astype(o_ref.dtype)


def _swa_prompt(proj, kv, bias, sinks, n_seq, seq_len, q_colblock, z_colblock, k_colblock, v_colblock,
                n_qblk=1):
    tq = n_qblk * WINDOW
    steps = seq_len // tq
    cur = lambda b, n: b * steps + n
    prev = lambda b, n: (b * steps + n) * n_qblk - jnp.where(n == 0, 0, 1)
    return pl.pallas_call(
        functools.partial(_swa_prompt_kernel, n_qblk=n_qblk),
        out_shape=jax.ShapeDtypeStruct((n_seq * seq_len, SWA_QW), BF16),
        grid=(n_seq, steps),
        in_specs=[pl.BlockSpec((tq, SWA_QW), lambda b, n: (cur(b, n), q_colblock)),
                  pl.BlockSpec((tq, SWA_QW), lambda b, n: (cur(b, n), z_colblock)),
                  pl.BlockSpec((tq, SWA_KVW), lambda b, n: (cur(b, n), k_colblock)),
                  pl.BlockSpec((WINDOW, SWA_KVW), lambda b, n: (prev(b, n), k_colblock)),
                  pl.BlockSpec((tq, SWA_KVW), lambda b, n: (cur(b, n), v_colblock)),
                  pl.BlockSpec((WINDOW, SWA_KVW), lambda b, n: (prev(b, n), v_colblock)),
                  pl.BlockSpec((2, SWA_H, WINDOW, WINDOW), lambda b, n: (0, 0, 0, 0)),
                  pl.BlockSpec(memory_space=pltpu.SMEM)],
        out_specs=pl.BlockSpec((tq, SWA_QW), lambda b, n: (cur(b, n), 0)),
        compiler_params=_cparams(("parallel", "arbitrary")),
        name="swa_prompt",
    )(proj, proj, kv, kv, kv, kv, bias, sinks)


def _swa_sample_kernel(q_ref, z_ref, kn_ref, vn_ref, ck_ref, cv_ref, bias_ref, sink_ref, o_ref, nk_ref, nv_ref,
                       *, n_seq_blk, n_new):
    scale = SWA_DH ** -0.5
    wb = ck_ref.shape[3]
    probs = [(s, kv) for s in range(n_seq_blk) for kv in range(SWA_KV)]
    ks = lambda kv: slice(kv * SWA_DH, (kv + 1) * SWA_DH)
    slot = lax.broadcasted_iota(jnp.int32, (SWA_DH, wb), 1)
    tok = lax.broadcasted_iota(jnp.int32, (kn_ref.shape[1], wb), 0)
    tok_slot = lax.broadcasted_iota(jnp.int32, (kn_ref.shape[1], wb), 1)
    place = jnp.where((tok_slot == tok + (wb - n_new)) & (tok < n_new), 1.0, 0.0).astype(F32)
    for s in range(n_seq_blk):
        for cache_ref, new_ref, out_ref in ((ck_ref, kn_ref, nk_ref), (cv_ref, vn_ref, nv_ref)):
            new_t = lax.dot_general(new_ref[s], place, (((0,), (0,)), ((), ())),
                                    preferred_element_type=F32, precision=lax.Precision.HIGHEST)
            for kv in range(SWA_KV):
                out_ref[s, kv] = jnp.where(slot >= wb - n_new, new_t[ks(kv), :],
                                           pltpu.roll(cache_ref[s, kv], wb - n_new, axis=1))
    lc = [_dot(q_ref[s, kv] * scale, ck_ref[s, kv]) + bias_ref[kv, :, 0:wb] for s, kv in probs]
    ln = [_dot_nt(q_ref[s, kv] * scale, kn_ref[s, :, ks(kv)]) + bias_ref[kv, :, wb:] for s, kv in probs]
    pcs, pns, dens = [], [], []
    for (s, kv), c, n in zip(probs, lc, ln):
        sink = sink_ref[kv][:, 0:1]
        m = jnp.maximum(jnp.maximum(jnp.max(c, axis=-1, keepdims=True), jnp.max(n, axis=-1, keepdims=True)),
                        sink)
        pc = jnp.exp(c - m)
        pn = jnp.exp(n - m)
        pcs.append(pc)
        pns.append(pn)
        dens.append(jnp.sum(pc, axis=-1, keepdims=True) + jnp.sum(pn, axis=-1, keepdims=True)
                    + jnp.exp(sink - m))
    outs = [(_dot_nt(pc, cv_ref[s, kv]) + _dot(pn, vn_ref[s, :, ks(kv)])) / den
            for (s, kv), pc, pn, den in zip(probs, pcs, pns, dens)]
    for oh, (s, kv) in zip(outs, probs):
        o_ref[s, kv] = (oh * _silu(z_ref[s, kv].astype(F32))).astype(o_ref.dtype)


def _swa_sample(q, z, k_new, v_new, cache_kt, cache_vt, bias, sink_rows, n_new, n_seq_blk=8):
    bd, _, rows, _ = q.shape
    wb = cache_kt.shape[3]
    npad = k_new.shape[1]
    assert wb == LANE
    blk4 = lambda i: (i, 0, 0, 0)
    blk3 = lambda i: (i, 0, 0)
    cache_spec = pl.BlockSpec((n_seq_blk, SWA_KV, SWA_DH, wb), blk4)
    return pl.pallas_call(
        functools.partial(_swa_sample_kernel, n_seq_blk=n_seq_blk, n_new=n_new),
        out_shape=(jax.ShapeDtypeStruct(q.shape, BF16),
                   jax.ShapeDtypeStruct(cache_kt.shape, F32), jax.ShapeDtypeStruct(cache_vt.shape, F32)),
        grid=(bd // n_seq_blk,),
        in_specs=[pl.BlockSpec((n_seq_blk, SWA_KV, rows, SWA_DH), blk4),
                  pl.BlockSpec((n_seq_blk, SWA_KV, rows, SWA_DH), blk4),
                  pl.BlockSpec((n_seq_blk, npad, SWA_KVW), blk3),
                  pl.BlockSpec((n_seq_blk, npad, SWA_KVW), blk3),
                  pl.BlockSpec((n_seq_blk, SWA_KV, SWA_DH, wb), blk4),
                  pl.BlockSpec((n_seq_blk, SWA_KV, SWA_DH, wb), blk4),
                  pl.BlockSpec((SWA_KV, rows, wb + npad), lambda i: (0, 0, 0)),
                  pl.BlockSpec((SWA_KV, rows, LANE), lambda i: (0, 0, 0))],
        out_specs=(pl.BlockSpec((n_seq_blk, SWA_KV, rows, SWA_DH), blk4), cache_spec, cache_spec),
        compiler_params=_cparams(("arbitrary",)),
        name="swa_sample",
    )(q, z, k_new, v_new, cache_kt, cache_vt, bias, sink_rows)


def _mem_attend(probs, q_of, z_of, k_of, v_of, store):
    scale = MEM_DH ** -0.5
    logits = [_dot_nt(q_of(p) * scale, k_of(p)) for p in probs]
    ps, dens = [], []
    for l in logits:
        m = jnp.max(l, axis=-1, keepdims=True)
        e = jnp.exp(l - m)
        dens.append(jnp.sum(e, axis=-1, keepdims=True))
        ps.append(e)
    outs = [_dot(e, v_of(p)) / den for e, den, p in zip(ps, dens, probs)]
    for p, oh in zip(probs, outs):
        store(p, oh * _silu(z_of(p).astype(F32)))


def _mem_cols(h):
    return slice(h * MEM_DH, (h + 1) * MEM_DH)


def _mem_prompt_kernel(q_ref, z_ref, k_ref, v_ref, o_ref):
    def store(h, val):
        o_ref[:, _mem_cols(h)] = val.astype(o_ref.dtype)
    _mem_attend(range(MEM_H), lambda h: q_ref[:, _mem_cols(h)], lambda h: z_ref[:, _mem_cols(h)],
                lambda h: k_ref[:, _mem_cols(h)], lambda h: v_ref[:, _mem_cols(h)], store)


def _mem_prompt(proj, mkv, n_seq, seq_len, q_colblock, z_colblock, tq=512):
    steps = seq_len // tq
    return pl.pallas_call(
        _mem_prompt_kernel,
        out_shape=jax.ShapeDtypeStruct((n_seq * seq_len, MEM_W), BF16),
        grid=(n_seq, steps),
        in_specs=[pl.BlockSpec((tq, MEM_W), lambda b, n: (b * steps + n, q_colblock)),
                  pl.BlockSpec((tq, MEM_W), lambda b, n: (b * steps + n, z_colblock)),
                  pl.BlockSpec((N_MEM, MEM_W), lambda b, n: (b, 0)),
                  pl.BlockSpec((N_MEM, MEM_W), lambda b, n: (b, 1))],
        out_specs=pl.BlockSpec((tq, MEM_W), lambda b, n: (b * steps + n, 0)),
        compiler_params=_cparams(("parallel", "arbitrary")),
        name="mem_prompt",
    )(proj, proj, mkv, mkv)


def _mem_sample_kernel(q_ref, z_ref, k_hbm, v_hbm, o_ref, kbuf, vbuf, sem, *, n_seq_blk):
    i = pl.program_id(0)
    n_steps = pl.num_programs(0)
    slot = i % 2

    def copies(step, slot_):
        seqs = pl.ds(step * n_seq_blk, n_seq_blk)
        out = []
        for h in range(MEM_H):
            out.append(pltpu.make_async_copy(k_hbm.at[seqs, :, h, :], kbuf.at[slot_, h], sem.at[0, slot_, h]))
            out.append(pltpu.make_async_copy(v_hbm.at[seqs, :, h, :], vbuf.at[slot_, h], sem.at[1, slot_, h]))
        return out

    @pl.when(i == 0)
    def _():
        for cp in copies(0, 0):
            cp.start()

    @pl.when(i + 1 < n_steps)
    def _():
        for cp in copies(i + 1, 1 - slot):
            cp.start()

    for cp in copies(i, slot):
        cp.wait()

    def store(p, val):
        o_ref[p[0], :, _mem_cols(p[1])] = val.astype(o_ref.dtype)
    probs = [(s, h) for s in range(n_seq_blk) for h in range(MEM_H)]
    _mem_attend(probs, lambda p: q_ref[p[0], :, _mem_cols(p[1])], lambda p: z_ref[p[0], :, _mem_cols(p[1])],
                lambda p: kbuf[slot, p[1], p[0]], lambda p: vbuf[slot, p[1], p[0]], store)


def _mem_sample(q, z, cache_k, cache_v, n_seq_blk=4):
    bd, rows, _ = q.shape
    blk = lambda i: (i, 0, 0)
    buf = pltpu.VMEM((2, MEM_H, n_seq_blk, N_MEM, MEM_DH), cache_k.dtype)
    return pl.pallas_call(
        functools.partial(_mem_sample_kernel, n_seq_blk=n_seq_blk),
        out_shape=jax.ShapeDtypeStruct(q.shape, BF16),
        grid=(bd // n_seq_blk,),
        in_specs=[pl.BlockSpec((n_seq_blk, rows, MEM_W), blk),
                  pl.BlockSpec((n_seq_blk, rows, MEM_W), blk),
                  pl.BlockSpec(memory_space=pl.ANY),
                  pl.BlockSpec(memory_space=pl.ANY)],
        out_specs=pl.BlockSpec((n_seq_blk, rows, MEM_W), blk),
        scratch_shapes=[buf, buf, pltpu.SemaphoreType.DMA((2, 2, MEM_H))],
        compiler_params=_cparams(("arbitrary",)),
        name="mem_sample",
    )(q, z, cache_k, cache_v)


def _merge_kernel(og_ref, os_ref, om_ref, gate_ref, x_ref, wb_ref, wo_ref, nf_ref, y_ref, *, n_sub):
    tm = x_ref.shape[0]
    sub = tm // n_sub
    for s in range(n_sub):
        rows = slice(s * sub, (s + 1) * sub)
        merged = None
        for b, o_ref in enumerate((og_ref, os_ref, om_ref)):
            t = jnp.dot(o_ref[rows, :], wb_ref[b], preferred_element_type=F32)
            t = t * gate_ref[rows, b * D_MODEL:(b + 1) * D_MODEL].astype(F32)
            merged = t if merged is None else merged + t
        h = x_ref[rows, :] + jnp.dot(merged.astype(BF16), wo_ref[...], preferred_element_type=F32)
        ms = jnp.mean(h * h, axis=-1, keepdims=True)
        y_ref[rows, :] = h * lax.rsqrt(ms + NORM_EPS) * nf_ref[...]


def _merge(o_gdn, o_swa, o_mem, gates, x, w_branch, w_out, norm_f, tm=256):
    m = x.shape[0]
    tm = min(tm, m)
    row = lambda i: (i, 0)
    const2 = lambda i: (0, 0)
    return pl.pallas_call(
        functools.partial(_merge_kernel, n_sub=1),
        out_shape=jax.ShapeDtypeStruct((m, D_MODEL), F32),
        grid=(m // tm,),
        in_specs=[pl.BlockSpec((tm, BR_W), row),
                  pl.BlockSpec((tm, BR_W), row),
                  pl.BlockSpec((tm, BR_W), row),
                  pl.BlockSpec((tm, N_BRANCH * D_MODEL), row),
                  pl.BlockSpec((tm, D_MODEL), row),
                  pl.BlockSpec((N_BRANCH, BR_W, D_MODEL), lambda i: (0, 0, 0),
                               pipeline_mode=pl.Buffered(1)),
                  pl.BlockSpec((D_MODEL, D_MODEL), const2, pipeline_mode=pl.Buffered(1)),
                  pl.BlockSpec((1, D_MODEL), const2)],
        out_specs=pl.BlockSpec((tm, D_MODEL), row),
        compiler_params=_cparams(("parallel",)),
        name="merge",
    )(o_gdn, o_swa, o_mem, gates, x, w_branch, w_out, norm_f)


_IN_SIZES = (GDN_QK, GDN_QK, GDN_W, GDN_W, GDN_H, GDN_H, SWA_QW, SWA_KVW, SWA_KVW, SWA_QW,
             MEM_W, MEM_W, N_BRANCH * D_MODEL)
_IN_NAMES = ("gq", "gk", "gv", "gz", "gb", "ga", "sq", "sk", "sv", "sz", "mq", "mz", "mg")
_IN_SPAN = {name: (int(off), int(off + size)) for name, off, size in
            zip(_IN_NAMES, np.cumsum((0,) + _IN_SIZES[:-1]), _IN_SIZES)}


def kernel(x_prompt, x_sample, state_gdn, state_gdn_conv, cache_swa_k, cache_swa_v, cache_mem_k,
           cache_mem_v, mem_prompt, norm_in, w_in, gdn_conv_w, gdn_a_log, gdn_dt_bias, gdn_norm,
           swa_sinks, rel_bias, norm_mem, w_mem_kv, w_branch, w_out, norm_f):
    n_layers = norm_in.shape[0]
    assert n_layers == 1
    b, seq, _ = x_prompt.shape
    bd, ns, _ = x_sample.shape
    wb = cache_swa_k.shape[2]
    assert seq % WINDOW == 0 and seq % GDN_CHUNK == 0 and ns + CONV_W <= SUBLANE and wb == WINDOW
    lyr = 0

    w = w_in[lyr]
    main_names = ("gq", "gk", "gv", "gz", "sq", "sz", "mq", "mz")
    n_main = len(main_names) * W_BLK
    n_gate = N_BRANCH * D_MODEL
    n_small = 3 * LANE
    small_lead = -(n_main + n_gate) % n_small
    w_all = _wprep(w.T, [[(_IN_SPAN[a][0], W_BLK)] for a in main_names]
                   + [[(_IN_SPAN["mg"][0] + W_BLK * k, W_BLK)] for k in range(n_gate // W_BLK)]
                   + [[(None, small_lead), (_IN_SPAN["sk"][0], 2 * SWA_KVW), (_IN_SPAN["gb"][0], 2 * GDN_H)]])
    col_small = n_main + n_gate + small_lead
    cb_gz, cb_sq, cb_sz, cb_mq, cb_mz = 3, 4, 5, 6, 7
    cb_sk, cb_sv, cb_ba = 0, 1, 2
    w_mkv = w_mem_kv[lyr].astype(BF16)
    w_br = w_branch[lyr].astype(BF16)
    w_o = w_out[lyr].astype(BF16)
    nw_in = norm_in[lyr].reshape(1, D_MODEL)
    nw_mem = norm_mem[lyr].reshape(1, D_MODEL)
    nw_f = norm_f.reshape(1, D_MODEL)
    conv_w = gdn_conv_w[lyr]
    a_row = jnp.pad(gdn_a_log[lyr].reshape(1, GDN_H), ((0, 0), (GDN_H, LANE - 2 * GDN_H)))
    dt_row = jnp.pad(gdn_dt_bias[lyr].reshape(1, GDN_H), ((0, 0), (GDN_H, LANE - 2 * GDN_H)))
    gnw = gdn_norm[lyr].reshape(1, GDN_DV)
    sinks = swa_sinks[lyr]
    bias_p = _bias_prompt(rel_bias)
    npad = SUBLANE
    bias_s = _bias_sample(rel_bias, ns, wb, wb + npad)
    sink_rows = jnp.broadcast_to(jnp.repeat(sinks.reshape(SWA_KV, SWA_G), ns, axis=1)[:, :, None],
                                 (SWA_KV, SWA_G * ns, LANE))

    t = b * seq
    xp = x_prompt.reshape(t, D_MODEL)
    xn_p, p_small = _rmsnorm(xp, nw_in, w_all, col_small, n_small)
    p_main = _proj(xn_p, w_all, BF16, n=n_main)
    g_p = _proj(xn_p, w_all, BF16, act="sigmoid", col0=n_main, n=n_gate)
    xn_tail = xn_p.reshape(b, seq, D_MODEL)[:, seq - SUBLANE:, :].reshape(b * SUBLANE, D_MODEL)
    conv_p = _proj(xn_tail, w_all, F32, n=GDN_CONV_CH).reshape(b, SUBLANE, GDN_CONV_CH)[:, SUBLANE - (CONV_W - 1):]
    kv_tail = p_small.reshape(b, seq, 3 * LANE)[:, seq - WINDOW:]
    swk_p = kv_tail[:, :, cb_sk * LANE:(cb_sk + 1) * LANE].reshape(b, WINDOW, SWA_KV, SWA_DH)
    swv_p = kv_tail[:, :, cb_sv * LANE:(cb_sv + 1) * LANE].reshape(b, WINDOW, SWA_KV, SWA_DH)

    mkv = _proj(_rmsnorm(mem_prompt.reshape(b * N_MEM, D_MODEL), nw_mem), w_mkv, F32)
    mk_p = mkv[:, :MEM_W].reshape(b, N_MEM, MEM_W)
    mv_p = mkv[:, MEM_W:].reshape(b, N_MEM, MEM_W)

    o_gdn_p, s_p = _gdn(p_main, p_main, p_small, jnp.zeros((b, HDR, GDN_CONV_CH), F32),
                        jnp.zeros((b, GDN_H, GDN_DK, GDN_DV), F32), conv_w, a_row, dt_row, gnw,
                        n_seq=b, chunk=GDN_CHUNK, n_chunks=4, per_chunk_state=False, group=4, unroll=True,
                        valid_lo=0, valid_hi=GDN_CHUNK, out_dtype=BF16,
                        z_colblock=cb_gz, ba_colblock=cb_ba)
    o_swa_p = _swa_prompt(p_main, p_small, bias_p, sinks, b, seq, cb_sq, cb_sz, cb_sk, cb_sv)
    o_mem_p = _mem_prompt(p_main, mkv, b, seq, cb_mq, cb_mz)
    y_p = _merge(o_gdn_p, o_swa_p, o_mem_p, g_p, xp, w_br, w_o, nw_f).reshape(b, seq, D_MODEL)

    ts = bd * ns
    xs = x_sample.reshape(ts, D_MODEL)
    xn_s, s_small = _rmsnorm(xs, nw_in, w_all, col_small, n_small)
    s_main = _proj(xn_s, w_all, F32, n=n_main)
    g_s = _proj(xn_s, w_all, BF16, act="sigmoid", col0=n_main, n=n_gate)
    s_gdn = s_main[:, :GDN_CONV_CH + GDN_W]
    s_ba = s_small[:, cb_ba * LANE:(cb_ba + 1) * LANE]
    s_swa = jnp.concatenate([s_main[:, cb_sq * BR_W:(cb_sz + 1) * BR_W], s_small[:, :2 * LANE]], axis=1)
    s_mem = s_main[:, cb_mq * BR_W:(cb_mz + 1) * BR_W].astype(BF16)

    lo = CONV_W - 1
    hi = lo + ns
    pad_rows = ((0, 0), (lo, SUBLANE - hi), (0, 0))
    e_qkv = jnp.concatenate([state_gdn_conv[lyr], s_gdn[:, :GDN_CONV_CH].reshape(bd, ns, GDN_CONV_CH),
                             jnp.zeros((bd, SUBLANE - hi, GDN_CONV_CH), F32)], axis=1)
    e_z = jnp.pad(s_gdn[:, GDN_CONV_CH:].reshape(bd, ns, GDN_W), pad_rows)
    e_ba = jnp.pad(s_ba.reshape(bd, ns, LANE), pad_rows)
    seq_blk = 8
    o_gdn_s8, s_s = _gdn(e_qkv.reshape(bd * SUBLANE, GDN_CONV_CH), e_z.reshape(bd * SUBLANE, GDN_W),
                         e_ba.reshape(bd * SUBLANE, LANE), jnp.zeros((1, HDR, GDN_CONV_CH), F32),
                         state_gdn[lyr], conv_w, a_row, dt_row, gnw,
                         n_seq=bd, chunk=SUBLANE, n_chunks=seq_blk, per_chunk_state=True, group=8,
                         unroll=True,
                         valid_lo=lo, valid_hi=hi, out_dtype=BF16)
    o_gdn_s = o_gdn_s8.reshape(bd, SUBLANE, GDN_W)[:, lo:hi].reshape(ts, GDN_W)
    conv_s = e_qkv[:, hi - (CONV_W - 1):hi]

    def to_heads(a):
        return a.reshape(bd, ns, SWA_KV, SWA_G, SWA_DH).transpose(0, 2, 3, 1, 4).reshape(
            bd, SWA_KV, SWA_G * ns, SWA_DH)

    k_new = s_swa[:, 2 * SWA_QW:2 * SWA_QW + SWA_KVW].reshape(bd, ns, SWA_KVW)
    v_new = s_swa[:, 2 * SWA_QW + SWA_KVW:].reshape(bd, ns, SWA_KVW)
    tok_pad = ((0, 0), (0, npad - ns), (0, 0))
    ck_t = cache_swa_k[lyr].transpose(0, 2, 3, 1)
    cv_t = cache_swa_v[lyr].transpose(0, 2, 3, 1)
    o_swa_h, nk_t, nv_t = _swa_sample(
        to_heads(s_swa[:, :SWA_QW]).astype(BF16), to_heads(s_swa[:, SWA_QW:2 * SWA_QW]),
        jnp.pad(k_new, tok_pad), jnp.pad(v_new, tok_pad), ck_t, cv_t, bias_s, sink_rows, n_new=ns)
    o_swa_s = o_swa_h.reshape(bd, SWA_KV, SWA_G, ns, SWA_DH).transpose(0, 3, 1, 2, 4).reshape(ts, SWA_QW)
    swk_s = nk_t.transpose(0, 3, 1, 2)
    swv_s = nv_t.transpose(0, 3, 1, 2)

    mq = jnp.pad(s_mem[:, :MEM_W].reshape(bd, ns, MEM_W), tok_pad)
    mz = jnp.pad(s_mem[:, MEM_W:].reshape(bd, ns, MEM_W), tok_pad)
    o_mem_s = _mem_sample(mq, mz, cache_mem_k[lyr], cache_mem_v[lyr])[:, :ns].reshape(ts, MEM_W)
    y_s = _merge(o_gdn_s, o_swa_s, o_mem_s, g_s, xs, w_br, w_o, nw_f).reshape(bd, ns, D_MODEL)

    return (y_p, y_s,
            s_p[None], conv_p[None], swk_p[None], swv_p[None],
            mk_p.reshape(b, N_MEM, MEM_H, MEM_DH)[None], mv_p.reshape(b, N_MEM, MEM_H, MEM_DH)[None],
            s_s[None], conv_s[None], swk_s[None], swv_s[None])
```

```python
import functools
import math

import numpy as np
import jax
import jax.numpy as jnp
from jax import lax
from jax.experimental import pallas as pl
from jax.experimental.pallas import tpu as pltpu

F32 = jnp.float32
BF16 = jnp.bfloat16

D_MODEL = 2048
N_BRANCH = 3
BR_W = 1024
GDN_H = 8
GDN_DK = 128
GDN_DV = 128
GDN_QK = GDN_H * GDN_DK
GDN_W = GDN_H * GDN_DV
GDN_CONV_CH = 2 * GDN_QK + GDN_W
CONV_W = 4
GDN_CHUNK = 64
SWA_H = 16
SWA_KV = 2
SWA_G = SWA_H // SWA_KV
SWA_DH = 64
SWA_QW = SWA_H * SWA_DH
SWA_KVW = SWA_KV * SWA_DH
WINDOW = 128
N_BUCKETS = 32
MAX_DISTANCE = 128
N_MEM = 256
MEM_H = 4
MEM_DH = 256
MEM_W = MEM_H * MEM_DH
NORM_EPS = 1e-6

LANE = 128
SUBLANE = 8
VMEM_LIMIT = 52 * 1024 * 1024


def _cparams(sem):
    return pltpu.CompilerParams(dimension_semantics=sem, vmem_limit_bytes=VMEM_LIMIT)


def _sigmoid(x):
    return 0.5 * jnp.tanh(0.5 * x) + 0.5


def _silu(x):
    h = 0.5 * x
    return h * jnp.tanh(h) + h


def _softplus(x):
    return jnp.maximum(x, 0.0) + jnp.log(1.0 + jnp.exp(-jnp.abs(x)))


def _dot(a, b):
    return jnp.dot(a.astype(BF16), b.astype(BF16), preferred_element_type=F32)


def _dot_nt(a, b):
    return lax.dot_general(a.astype(BF16), b.astype(BF16), (((1,), (1,)), ((), ())),
                           preferred_element_type=F32)


def _dot_tn(a, b):
    return lax.dot_general(a.astype(BF16), b.astype(BF16), (((0,), (0,)), ((), ())),
                           preferred_element_type=F32)


def _dot_f32(a, b):
    return jnp.dot(a, b, preferred_element_type=F32, precision=lax.Precision.HIGHEST)


def _rmsnorm_kernel(x_ref, nw_ref, *rest):
    x = x_ref[...].astype(F32)
    ms = jnp.mean(x * x, axis=-1, keepdims=True)
    xn = (x * lax.rsqrt(ms + NORM_EPS) * nw_ref[...]).astype(BF16)
    if len(rest) == 1:
        rest[0][...] = xn
    else:
        w_ref, o_ref, p_ref = rest
        o_ref[...] = xn
        p_ref[...] = jnp.dot(xn, w_ref[...], preferred_element_type=F32)


def _rmsnorm(x, norm_w, w=None, col0=0, n=None, tm_pref=512):
    m, d = x.shape
    tm = min(m, tm_pref)
    assert m % tm == 0
    row = lambda i: (i, 0)
    in_specs = [pl.BlockSpec((tm, d), row), pl.BlockSpec((1, d), lambda i: (0, 0))]
    out_shape = jax.ShapeDtypeStruct((m, d), BF16)
    out_specs = pl.BlockSpec((tm, d), row)
    args = (x, norm_w)
    if w is not None:
        assert col0 % n == 0
        in_specs.append(pl.BlockSpec((d, n), lambda i: (0, col0 // n)))
        out_shape = (out_shape, jax.ShapeDtypeStruct((m, n), F32))
        out_specs = (out_specs, pl.BlockSpec((tm, n), row))
        args = args + (w,)
    return pl.pallas_call(
        _rmsnorm_kernel,
        out_shape=out_shape,
        grid=(m // tm,),
        in_specs=in_specs,
        out_specs=out_specs,
        compiler_params=_cparams(("parallel",)),
        name="rmsnorm",
    )(*args)


def _proj_kernel(x_ref, w_ref, o_ref, *, act):
    y = jnp.dot(x_ref[...], w_ref[...], preferred_element_type=F32)
    if act == "sigmoid":
        y = _sigmoid(y)
    o_ref[...] = y.astype(o_ref.dtype)


def _pick_tile(n, pref):
    t = min(n, pref)
    while n % t:
        t -= LANE
    return t


def _proj(xn, w, out_dtype, act=None, col0=0, n=None, tm_pref=1024, tn_pref=2048):
    m, d = xn.shape
    n = w.shape[1] if n is None else n
    tm = min(m, tm_pref)
    assert m % tm == 0
    tn = _pick_tile(n, tn_pref if out_dtype == BF16 else tn_pref // 2)
    while col0 % tn:
        tn = _pick_tile(n, tn - LANE)
    jb = col0 // tn
    return pl.pallas_call(
        functools.partial(_proj_kernel, act=act),
        out_shape=jax.ShapeDtypeStruct((m, n), out_dtype),
        grid=(m // tm, n // tn),
        in_specs=[pl.BlockSpec((tm, d), lambda i, j: (i, 0)),
                  pl.BlockSpec((d, tn), lambda i, j: (0, jb + j))],
        out_specs=pl.BlockSpec((tm, tn), lambda i, j: (i, j)),
        compiler_params=_cparams(("parallel", "arbitrary")),
        name="proj",
    )(xn, w)


W_BLK = 1024
W_PIECE_ROWS = (W_BLK, 2 * SUBLANE)


def _wprep_kernel(wt_hbm, o_ref, inbuf0, inbuf1, sem, *, blocks):
    j = pl.program_id(0)
    inbuf = (inbuf0, inbuf1)

    def sources(b):
        return [(start, rows) for start, rows in blocks[b] if start is not None]

    def pieces(b):
        return [pltpu.make_async_copy(wt_hbm.at[pl.ds(start, rows), :],
                                      inbuf[k].at[b % 2, pl.ds(0, rows), :], sem.at[b % 2, k])
                for k, (start, rows) in enumerate(sources(b))]

    for b, ranges in enumerate(blocks):
        @pl.when(j == b)
        def _(b=b, ranges=ranges):
            if b == 0:
                for cp in pieces(0):
                    cp.start()
            if b + 1 < len(blocks):
                for cp in pieces(b + 1):
                    cp.start()
            for cp in pieces(b):
                cp.wait()
            n_have = sum(rows for _, rows in ranges)
            parts, k = [], 0
            for start, rows in tuple(ranges) + ((None, W_BLK - n_have),):
                if start is None:
                    if rows:
                        parts.append(jnp.zeros((rows, inbuf0.shape[2]), inbuf0.dtype))
                else:
                    parts.append(inbuf[k][b % 2, 0:rows, :])
                    k += 1
            val = parts[0] if len(parts) == 1 else jnp.concatenate(parts, axis=0)
            o_ref[...] = val.T.astype(BF16)


def _wprep(wt, block_srcs):
    ncol, d = wt.shape
    for ranges in block_srcs:
        srcs = [(s, rows) for s, rows in ranges if s is not None]
        assert len(srcs) <= len(W_PIECE_ROWS) and all(rows % SUBLANE == 0 for _, rows in ranges)
        for (s, rows), cap in zip(srcs, W_PIECE_ROWS):
            assert s % SUBLANE == 0 and rows <= cap and s + rows <= ncol
    return pl.pallas_call(
        functools.partial(_wprep_kernel, blocks=tuple(tuple(r) for r in block_srcs)),
        out_shape=jax.ShapeDtypeStruct((d, W_BLK * len(block_srcs)), BF16),
        grid=(len(block_srcs),),
        in_specs=[pl.BlockSpec(memory_space=pl.ANY)],
        out_specs=pl.BlockSpec((d, W_BLK), lambda j: (0, j)),
        scratch_shapes=[pltpu.VMEM((2, cap, d), wt.dtype) for cap in W_PIECE_ROWS]
                       + [pltpu.SemaphoreType.DMA((2, len(W_PIECE_ROWS)))],
        compiler_params=_cparams(("arbitrary",)),
        name="wprep",
    )(wt)


HDR = SUBLANE


def _tri_inv_many(a_list, c):
    row = lax.broadcasted_iota(jnp.int32, (c, c), 0)
    col = lax.broadcasted_iota(jnp.int32, (c, c), 1)
    eye = jnp.where(row == col, 1.0, 0.0).astype(F32)
    xs = [eye - a for a in a_list]
    bs = [_dot(a, a) for a in a_list]
    n = 2
    while n < c:
        xs = [x + _dot(x, b) for x, b in zip(xs, bs)]
        n *= 2
        if n < c:
            bs = [_dot(b, b) for b in bs]
    return xs


def _gdn_kernel(qkv_ref, z_ref, ba_ref, hdr_ref, s0_ref, cw_ref, arow_ref, dtrow_ref, nw_ref,
                o_ref, s_ref, buf_ref, cv_ref, gb_ref, oacc_ref, sol_ref, att_ref, gl_ref,
                *, chunk, n_chunks, per_chunk_state, valid_lo, valid_hi):
    c = chunk
    tb = c * n_chunks

    if per_chunk_state:
        buf_ref[0:2 * HDR, :] = jnp.zeros((2 * HDR, GDN_CONV_CH), F32)
    else:
        @pl.when(pl.program_id(1) == 0)
        def _():
            buf_ref[0:HDR, :] = hdr_ref[0]
            buf_ref[HDR:2 * HDR, :] = jnp.zeros((HDR, GDN_CONV_CH), F32)
            s_ref[...] = s0_ref[...]

    def tap_from_history(j, n_rows):
        off = HDR - (CONV_W - 1) + j
        return buf_ref[off:off + n_rows, :] * cw_ref[j:j + 1, :]

    if qkv_ref.dtype == BF16:
        xb = qkv_ref[...]
        r = lax.broadcasted_iota(jnp.int32, (tb, tb), 0)
        cc = lax.broadcasted_iota(jnp.int32, (tb, tb), 1)
        acc = xb.astype(F32) * cw_ref[CONV_W - 1:CONV_W, :]
        for j in range(CONV_W - 1):
            shift = jnp.where(r - cc == CONV_W - 1 - j, 1.0, 0.0).astype(BF16)
            acc = acc + jnp.dot(shift, xb, preferred_element_type=F32) * cw_ref[j:j + 1, :]
        cv_ref[...] = _silu(acc)
        top = acc[0:HDR]
        for j in range(CONV_W - 1):
            top = top + tap_from_history(j, HDR)
        cv_ref[0:HDR, :] = _silu(top)
        buf_ref[0:HDR, :] = qkv_ref[tb - 2 * HDR:tb, :].astype(F32)[HDR:]
    else:
        buf_ref[HDR:HDR + tb, :] = qkv_ref[...].astype(F32)
        acc = None
        for j in range(CONV_W):
            term = tap_from_history(j, tb)
            acc = term if acc is None else acc + term
        cv_ref[...] = _silu(acc)
        if not per_chunk_state:
            buf_ref[0:HDR, :] = buf_ref[tb:tb + HDR, :]

    ba = ba_ref[...].astype(F32)
    beta_all = _sigmoid(ba)
    g_all = -jnp.exp(arow_ref[...]) * _softplus(ba + dtrow_ref[...])
    if per_chunk_state:
        r = lax.broadcasted_iota(jnp.int32, (tb, LANE), 0) & (c - 1)
        valid = (r >= valid_lo) & (r < valid_hi)
        beta_all = jnp.where(valid, beta_all, 0.0)
        g_all = jnp.where(valid, g_all, 0.0)
    gb_ref[0] = beta_all
    gb_ref[1] = g_all

    row = lax.broadcasted_iota(jnp.int32, (c, c), 0)
    col = lax.broadcasted_iota(jnp.int32, (c, c), 1)
    causal = row >= col
    strict = row > col
    tril = jnp.where(causal, 1.0, 0.0).astype(F32)
    scale_q = GDN_DK ** -0.5

    heads = range(GDN_H)
    qcol = lambda h: slice(h * GDN_DK, (h + 1) * GDN_DK)
    kcol = lambda h: slice(GDN_QK + h * GDN_DK, GDN_QK + (h + 1) * GDN_DK)
    vcol = lambda h: slice(2 * GDN_QK + h * GDN_DV, 2 * GDN_QK + (h + 1) * GDN_DV)
    ucol = lambda h: slice(h * (GDN_DV + GDN_DK), h * (GDN_DV + GDN_DK) + GDN_DV)
    wcol = lambda h: slice(h * (GDN_DV + GDN_DK) + GDN_DV, (h + 1) * (GDN_DV + GDN_DK))
    if per_chunk_state:
        rv = lax.broadcasted_iota(jnp.int32, (c, 1), 0)
        rvalid = (rv >= valid_lo) & (rv < valid_hi)

    def chunk_rows(ci):
        return slice(ci * c, (ci + 1) * c)

    def prep():
        probs = []
        for ci in range(n_chunks):
            rows = chunk_rows(ci)
            beta_c = gb_ref[0, rows, :]
            gc_all = _dot_f32(tril, gb_ref[1, rows, :])
            gc_t = gc_all.T
            gl_ref[ci] = gc_all[c - 1:c, :]
            for h in heads:
                q = cv_ref[rows, qcol(h)]
                k = cv_ref[rows, kcol(h)]
                v = cv_ref[rows, vcol(h)]
                q = q * lax.rsqrt(jnp.sum(q * q, axis=-1, keepdims=True) + NORM_EPS) * scale_q
                k = k * lax.rsqrt(jnp.sum(k * k, axis=-1, keepdims=True) + NORM_EPS)
                if per_chunk_state:
                    q = jnp.where(rvalid, q, 0.0)
                    k = jnp.where(rvalid, k, 0.0)
                    v = jnp.where(rvalid, v, 0.0)
                beta = beta_c[:, h:h + 1]
                gc_col = gc_all[:, GDN_H + h:GDN_H + h + 1]
                gc_row = gc_t[GDN_H + h:GDN_H + h + 1, :]
                diff = jnp.where(causal, gc_col - gc_row, 0.0)
                decay = jnp.where(causal, jnp.exp(diff), 0.0)
                e_gc = jnp.exp(gc_col)
                kb = k * beta
                rhs = jnp.concatenate([v * beta, kb * e_gc], axis=1)
                cv_ref[rows, qcol(h)] = q * e_gc
                cv_ref[rows, kcol(h)] = k * jnp.exp(gc_col[c - 1:c, :] - gc_col)
                probs.append((rows, h, q, k, kb, rhs, decay))
        kq = [_dot_nt(jnp.concatenate([kb, q], axis=0), k) for (_, _, q, k, kb, _, _) in probs]
        a_list = [jnp.where(strict, kq_i[:c] * p[6], 0.0) for kq_i, p in zip(kq, probs)]
        t_inv = _tri_inv_many(a_list, c)
        for t_i, kq_i, (rows, h, _, _, _, rhs, decay) in zip(t_inv, kq, probs):
            sol_ref[rows, h * (GDN_DV + GDN_DK):(h + 1) * (GDN_DV + GDN_DK)] = _dot(t_i, rhs)
            att_ref[h, rows, :] = kq_i[c:] * decay

    def scan(ci):
        rows = chunk_rows(ci)
        si = ci if per_chunk_state else 0
        g_tot = jnp.exp(gl_ref[ci])
        s_old = [s_ref[si, h] for h in heads]
        wq_s = [_dot(jnp.concatenate([sol_ref[rows, wcol(h)], cv_ref[rows, qcol(h)]], axis=0), s_old[h])
                for h in heads]
        v_new = [sol_ref[rows, ucol(h)] - wq_s[h][:c] for h in heads]
        o_att = [_dot(att_ref[h, rows, :], v_new[h]) for h in heads]
        for h in heads:
            s_ref[si, h] = (s_old[h] * g_tot[:, GDN_H + h:GDN_H + h + 1]
                            + _dot_tn(cv_ref[rows, kcol(h)], v_new[h]))
        for h in heads:
            o = wq_s[h][c:] + o_att[h]
            o = o * lax.rsqrt(jnp.mean(o * o, axis=-1, keepdims=True) + NORM_EPS) * nw_ref[...]
            oacc_ref[rows, h * GDN_DV:(h + 1) * GDN_DV] = o

    if per_chunk_state:
        s_ref[...] = s0_ref[...]
    prep()
    for ci in range(n_chunks):
        scan(ci)
    o_ref[...] = (oacc_ref[...] * _silu(z_ref[...].astype(F32))).astype(o_ref.dtype)


def _gdn(qkv, z, ba, hdr, s0, conv_w, a_row, dt_row, norm_w, *, n_seq, chunk, n_chunks,
         per_chunk_state, valid_lo, valid_hi, out_dtype, z_colblock=0, ba_colblock=0):
    rows = qkv.shape[0]
    tb = chunk * n_chunks
    if per_chunk_state:
        grid = (rows // tb,)
        sem = ("arbitrary",)
        rmap = lambda i: (i, 0)
        zmap = lambda i: (i, z_colblock)
        bmap = lambda i: (i, ba_colblock)
        hmap = lambda i: (0, 0, 0)
        smap = lambda i: (i, 0, 0, 0)
        cmap = lambda i: (0, 0)
        ns = n_chunks
    else:
        steps = rows // n_seq // tb
        grid = (n_seq, steps)
        sem = ("parallel", "arbitrary")
        rmap = lambda b, n: (b * steps + n, 0)
        zmap = lambda b, n: (b * steps + n, z_colblock)
        bmap = lambda b, n: (b * steps + n, ba_colblock)
        hmap = lambda b, n: (b, 0, 0)
        smap = lambda b, n: (b, 0, 0, 0)
        cmap = lambda b, n: (0, 0)
        ns = 1
    kern = functools.partial(_gdn_kernel, chunk=chunk, n_chunks=n_chunks,
                             per_chunk_state=per_chunk_state, valid_lo=valid_lo, valid_hi=valid_hi)
    return pl.pallas_call(
        kern,
        out_shape=(jax.ShapeDtypeStruct((rows, GDN_W), out_dtype),
                   jax.ShapeDtypeStruct(s0.shape, F32)),
        grid=grid,
        in_specs=[pl.BlockSpec((tb, GDN_CONV_CH), rmap),
                  pl.BlockSpec((tb, GDN_W), zmap),
                  pl.BlockSpec((tb, LANE), bmap),
                  pl.BlockSpec((1, HDR, GDN_CONV_CH), hmap),
                  pl.BlockSpec((ns, GDN_H, GDN_DK, GDN_DV), smap),
                  pl.BlockSpec((CONV_W, GDN_CONV_CH), cmap),
                  pl.BlockSpec((1, LANE), cmap),
                  pl.BlockSpec((1, LANE), cmap),
                  pl.BlockSpec((1, GDN_DV), cmap)],
        out_specs=(pl.BlockSpec((tb, GDN_W), rmap),
                   pl.BlockSpec((ns, GDN_H, GDN_DK, GDN_DV), smap)),
        scratch_shapes=[pltpu.VMEM((HDR + tb, GDN_CONV_CH), F32),
                        pltpu.VMEM((tb, GDN_CONV_CH), F32),
                        pltpu.VMEM((2, tb, LANE), F32),
                        pltpu.VMEM((tb, GDN_W), F32),
                        pltpu.VMEM((tb, GDN_H * (GDN_DV + GDN_DK)), F32),
                        pltpu.VMEM((GDN_H, tb, chunk), F32),
                        pltpu.VMEM((n_chunks, 1, LANE), F32)],
        compiler_params=_cparams(sem),
        name="gdn",
    )(qkv, z, ba, hdr, s0, conv_w, a_row, dt_row, norm_w)


def _t5_bucket_np(dist):
    n = np.maximum(dist, 0)
    max_exact = N_BUCKETS // 2
    nf = np.maximum(n, 1).astype(np.float32)
    large = max_exact + (np.log(nf / np.float32(max_exact)) / np.float32(math.log(MAX_DISTANCE / max_exact))
                         * np.float32(N_BUCKETS - max_exact)).astype(np.int32)
    large = np.minimum(large, N_BUCKETS - 1)
    return np.where(n < max_exact, n, large).astype(np.int32)


def _bias_prompt_kernel(code_ref, tab_ref, o_ref):
    code = code_ref[0]
    for h in range(SWA_H):
        acc = jnp.full(code.shape, -jnp.inf, F32)
        for b in range(N_BUCKETS):
            acc = jnp.where(code == b, tab_ref[b, h], acc)
        o_ref[0, h] = acc


def _bias_prompt(table):
    qi = np.arange(WINDOW)[None, :]
    sj = np.arange(WINDOW)[:, None]
    own = sj <= qi
    bucket = _t5_bucket_np(np.where(own, qi - sj, qi + WINDOW - sj))
    code_first = np.where(own, bucket, -1)
    code = jnp.asarray(np.stack([code_first, bucket]).astype(np.int32))
    return pl.pallas_call(
        _bias_prompt_kernel,
        out_shape=jax.ShapeDtypeStruct((2, SWA_H, WINDOW, WINDOW), F32),
        grid=(2,),
        in_specs=[pl.BlockSpec((1, WINDOW, WINDOW), lambda v: (v, 0, 0)),
                  pl.BlockSpec(memory_space=pltpu.SMEM)],
        out_specs=pl.BlockSpec((1, SWA_H, WINDOW, WINDOW), lambda v: (v, 0, 0, 0)),
        compiler_params=_cparams(("arbitrary",)),
        name="swa_bias_prompt",
    )(code, table)


def _bias_sample_kernel(code_ref, tab_ref, o_ref):
    kv = pl.program_id(0)
    code = code_ref[...]
    acc = jnp.full(code.shape, -jnp.inf, F32)
    for g in range(SWA_G):
        for b in range(N_BUCKETS):
            acc = jnp.where(code == b + N_BUCKETS * g, tab_ref[b, kv * SWA_G + g], acc)
    o_ref[0] = acc


def _bias_sample(table, n_tok, n_cache, n_keys_pad):
    dist = (n_cache + np.arange(n_tok))[:, None] - np.arange(n_keys_pad)[None, :]
    valid = (dist >= 0) & (dist < WINDOW) & (np.arange(n_keys_pad)[None, :] < n_cache + n_tok)
    bucket = _t5_bucket_np(dist)
    code_t = np.where(valid, bucket, -1)
    g = np.arange(SWA_G)[:, None, None]
    code = np.where(code_t[None] >= 0, code_t[None] + N_BUCKETS * g, -1)
    code = jnp.asarray(code.reshape(SWA_G * n_tok, n_keys_pad).astype(np.int32))
    return pl.pallas_call(
        _bias_sample_kernel,
        out_shape=jax.ShapeDtypeStruct((SWA_KV, SWA_G * n_tok, n_keys_pad), F32),
        grid=(SWA_KV,),
        in_specs=[pl.BlockSpec((SWA_G * n_tok, n_keys_pad), lambda k: (0, 0)),
                  pl.BlockSpec(memory_space=pltpu.SMEM)],
        out_specs=pl.BlockSpec((1, SWA_G * n_tok, n_keys_pad), lambda k: (k, 0, 0)),
        compiler_params=_cparams(("arbitrary",)),
        name="swa_bias_sample",
    )(code, table)


def _sink_softmax_pv(logits, sink, v):
    m = jnp.maximum(jnp.max(logits, axis=-1, keepdims=True), sink)
    p = jnp.exp(logits - m)
    den = jnp.sum(p, axis=-1, keepdims=True) + jnp.exp(sink - m)
    return _dot(p, v) / den


def _swa_prompt_kernel(q_ref, z_ref, kc_ref, kp_ref, vc_ref, vp_ref, bias_ref, sink_ref, o_ref, *, n_qblk):
    kall = jnp.concatenate([kp_ref[...], kc_ref[...]], axis=0)
    vall = jnp.concatenate([vp_ref[...], vc_ref[...]], axis=0)
    scale = SWA_DH ** -0.5
    key = lax.broadcasted_iota(jnp.int32, (WINDOW, WINDOW), 0)
    qry = lax.broadcasted_iota(jnp.int32, (WINDOW, WINDOW), 1)
    from_prev = key > qry
    first_variant = jnp.where(pl.program_id(1) == 0, 0, 1)
    cs = lambda h: slice(h * SWA_DH, (h + 1) * SWA_DH)
    rows = lambda qb: slice(qb * WINDOW, (qb + 1) * WINDOW)
    keys = lambda qb: slice(qb * WINDOW, (qb + 2) * WINDOW)
    for kv in range(SWA_KV):
        ks = slice(kv * SWA_DH, (kv + 1) * SWA_DH)
        k_kv = kall[:, ks].astype(BF16)
        v_t = vall[:, ks].astype(F32).T.astype(BF16)
        probs = [(qb, h) for qb in range(n_qblk) for h in range(kv * SWA_G, (kv + 1) * SWA_G)]
        lg = [_dot_nt(k_kv[keys(qb)], q_ref[rows(qb), cs(h)] * scale) for qb, h in probs]
        ps, dens = [], []
        for (qb, h), l in zip(probs, lg):
            bias = bias_ref[first_variant if qb == 0 else 1, h]
            l = jnp.where(from_prev, l[:WINDOW], l[WINDOW:]) + bias
            sink = sink_ref[h]
            m = jnp.maximum(jnp.max(l, axis=0, keepdims=True), sink)
            p = jnp.exp(l - m)
            dens.append(jnp.sum(p, axis=0, keepdims=True) + jnp.exp(sink - m))
            ps.append(jnp.concatenate([jnp.where(from_prev, p, 0.0), jnp.where(from_prev, 0.0, p)],
                                      axis=0))
        outs = [_dot(v_t[:, keys(qb)], p) / den for p, den, (qb, _) in zip(ps, dens, probs)]
        for j in range(0, len(probs), 2):
            qb, h = probs[j]
            two = slice(h * SWA_DH, (h + 2) * SWA_DH)
            o2 = jnp.concatenate([outs[j], outs[j + 1]], axis=0).T
            o_ref[rows(qb), two] = (o2 * _silu(z_ref[rows(qb), two].astype(F32))).astype(o_ref.dtype)


def _swa_prompt(proj, kv, bias, sinks, n_seq, seq_len, q_colblock, z_colblock, k_colblock, v_colblock,
                n_qblk=1):
    tq = n_qblk * WINDOW
    steps = seq_len // tq
    cur = lambda b, n: b * steps + n
    prev = lambda b, n: (b * steps + n) * n_qblk - jnp.where(n == 0, 0, 1)
    return pl.pallas_call(
        functools.partial(_swa_prompt_kernel, n_qblk=n_qblk),
        out_shape=jax.ShapeDtypeStruct((n_seq * seq_len, SWA_QW), BF16),
        grid=(n_seq, steps),
        in_specs=[pl.BlockSpec((tq, SWA_QW), lambda b, n: (cur(b, n), q_colblock)),
                  pl.BlockSpec((tq, SWA_QW), lambda b, n: (cur(b, n), z_colblock)),
                  pl.BlockSpec((tq, SWA_KVW), lambda b, n: (cur(b, n), k_colblock)),
                  pl.BlockSpec((WINDOW, SWA_KVW), lambda b, n: (prev(b, n), k_colblock)),
                  pl.BlockSpec((tq, SWA_KVW), lambda b, n: (cur(b, n), v_colblock)),
                  pl.BlockSpec((WINDOW, SWA_KVW), lambda b, n: (prev(b, n), v_colblock)),
                  pl.BlockSpec((2, SWA_H, WINDOW, WINDOW), lambda b, n: (0, 0, 0, 0)),
                  pl.BlockSpec(memory_space=pltpu.SMEM)],
        out_specs=pl.BlockSpec((tq, SWA_QW), lambda b, n: (cur(b, n), 0)),
        compiler_params=_cparams(("parallel", "arbitrary")),
        name="swa_prompt",
    )(proj, proj, kv, kv, kv, kv, bias, sinks)


def _swa_sample_kernel(q_ref, z_ref, kn_ref, vn_ref, ck_ref, cv_ref, bias_ref, sink_ref, o_ref, nk_ref, nv_ref,
                       *, n_seq_blk, n_new):
    scale = SWA_DH ** -0.5
    wb = ck_ref.shape[3]
    probs = [(s, kv) for s in range(n_seq_blk) for kv in range(SWA_KV)]
    ks = lambda kv: slice(kv * SWA_DH, (kv + 1) * SWA_DH)
    slot = lax.broadcasted_iota(jnp.int32, (SWA_DH, wb), 1)
    tok = lax.broadcasted_iota(jnp.int32, (kn_ref.shape[1], wb), 0)
    tok_slot = lax.broadcasted_iota(jnp.int32, (kn_ref.shape[1], wb), 1)
    place = jnp.where((tok_slot == tok + (wb - n_new)) & (tok < n_new), 1.0, 0.0).astype(F32)
    for s in range(n_seq_blk):
        for cache_ref, new_ref, out_ref in ((ck_ref, kn_ref, nk_ref), (cv_ref, vn_ref, nv_ref)):
            new_t = lax.dot_general(new_ref[s], place, (((0,), (0,)), ((), ())),
                                    preferred_element_type=F32, precision=lax.Precision.HIGHEST)
            for kv in range(SWA_KV):
                out_ref[s, kv] = jnp.where(slot >= wb - n_new, new_t[ks(kv), :],
                                           pltpu.roll(cache_ref[s, kv], wb - n_new, axis=1))
    lc = [_dot(q_ref[s, kv] * scale, ck_ref[s, kv]) + bias_ref[kv, :, 0:wb] for s, kv in probs]
    ln = [_dot_nt(q_ref[s, kv] * scale, kn_ref[s, :, ks(kv)]) + bias_ref[kv, :, wb:] for s, kv in probs]
    pcs, pns, dens = [], [], []
    for (s, kv), c, n in zip(probs, lc, ln):
        sink = sink_ref[kv][:, 0:1]
        m = jnp.maximum(jnp.maximum(jnp.max(c, axis=-1, keepdims=True), jnp.max(n, axis=-1, keepdims=True)),
                        sink)
        pc = jnp.exp(c - m)
        pn = jnp.exp(n - m)
        pcs.append(pc)
        pns.append(pn)
        dens.append(jnp.sum(pc, axis=-1, keepdims=True) + jnp.sum(pn, axis=-1, keepdims=True)
                    + jnp.exp(sink - m))
    outs = [(_dot_nt(pc, cv_ref[s, kv]) + _dot(pn, vn_ref[s, :, ks(kv)])) / den
            for (s, kv), pc, pn, den in zip(probs, pcs, pns, dens)]
    for oh, (s, kv) in zip(outs, probs):
        o_ref[s, kv] = (oh * _silu(z_ref[s, kv].astype(F32))).astype(o_ref.dtype)


def _swa_sample(q, z, k_new, v_new, cache_kt, cache_vt, bias, sink_rows, n_new, n_seq_blk=8):
    bd, _, rows, _ = q.shape
    wb = cache_kt.shape[3]
    npad = k_new.shape[1]
    assert wb == LANE
    blk4 = lambda i: (i, 0, 0, 0)
    blk3 = lambda i: (i, 0, 0)
    cache_spec = pl.BlockSpec((n_seq_blk, SWA_KV, SWA_DH, wb), blk4)
    return pl.pallas_call(
        functools.partial(_swa_sample_kernel, n_seq_blk=n_seq_blk, n_new=n_new),
        out_shape=(jax.ShapeDtypeStruct(q.shape, BF16),
                   jax.ShapeDtypeStruct(cache_kt.shape, F32), jax.ShapeDtypeStruct(cache_vt.shape, F32)),
        grid=(bd // n_seq_blk,),
        in_specs=[pl.BlockSpec((n_seq_blk, SWA_KV, rows, SWA_DH), blk4),
                  pl.BlockSpec((n_seq_blk, SWA_KV, rows, SWA_DH), blk4),
                  pl.BlockSpec((n_seq_blk, npad, SWA_KVW), blk3),
                  pl.BlockSpec((n_seq_blk, npad, SWA_KVW), blk3),
                  pl.BlockSpec((n_seq_blk, SWA_KV, SWA_DH, wb), blk4),
                  pl.BlockSpec((n_seq_blk, SWA_KV, SWA_DH, wb), blk4),
                  pl.BlockSpec((SWA_KV, rows, wb + npad), lambda i: (0, 0, 0)),
                  pl.BlockSpec((SWA_KV, rows, LANE), lambda i: (0, 0, 0))],
        out_specs=(pl.BlockSpec((n_seq_blk, SWA_KV, rows, SWA_DH), blk4), cache_spec, cache_spec),
        compiler_params=_cparams(("arbitrary",)),
        name="swa_sample",
    )(q, z, k_new, v_new, cache_kt, cache_vt, bias, sink_rows)


def _mem_attend(probs, q_of, z_of, k_of, v_of, store):
    scale = MEM_DH ** -0.5
    logits = [_dot_nt(q_of(p) * scale, k_of(p)) for p in probs]
    ps, dens = [], []
    for l in logits:
        m = jnp.max(l, axis=-1, keepdims=True)
        e = jnp.exp(l - m)
        dens.append(jnp.sum(e, axis=-1, keepdims=True))
        ps.append(e)
    outs = [_dot(e, v_of(p)) / den for e, den, p in zip(ps, dens, probs)]
    for p, oh in zip(probs, outs):
        store(p, oh * _silu(z_of(p).astype(F32)))


def _mem_cols(h):
    return slice(h * MEM_DH, (h + 1) * MEM_DH)


def _mem_prompt_kernel(q_ref, z_ref, k_ref, v_ref, o_ref):
    def store(h, val):
        o_ref[:, _mem_cols(h)] = val.astype(o_ref.dtype)
    _mem_attend(range(MEM_H), lambda h: q_ref[:, _mem_cols(h)], lambda h: z_ref[:, _mem_cols(h)],
                lambda h: k_ref[:, _mem_cols(h)], lambda h: v_ref[:, _mem_cols(h)], store)


def _mem_prompt(proj, mkv, n_seq, seq_len, q_colblock, z_colblock, tq=512):
    steps = seq_len // tq
    return pl.pallas_call(
        _mem_prompt_kernel,
        out_shape=jax.ShapeDtypeStruct((n_seq * seq_len, MEM_W), BF16),
        grid=(n_seq, steps),
        in_specs=[pl.BlockSpec((tq, MEM_W), lambda b, n: (b * steps + n, q_colblock)),
                  pl.BlockSpec((tq, MEM_W), lambda b, n: (b * steps + n, z_colblock)),
                  pl.BlockSpec((N_MEM, MEM_W), lambda b, n: (b, 0)),
                  pl.BlockSpec((N_MEM, MEM_W), lambda b, n: (b, 1))],
        out_specs=pl.BlockSpec((tq, MEM_W), lambda b, n: (b * steps + n, 0)),
        compiler_params=_cparams(("parallel", "arbitrary")),
        name="mem_prompt",
    )(proj, proj, mkv, mkv)


def _mem_sample_kernel(q_ref, z_ref, k_hbm, v_hbm, o_ref, kbuf, vbuf, sem, *, n_seq_blk):
    i = pl.program_id(0)
    n_steps = pl.num_programs(0)
    slot = i % 2

    def copies(step, slot_):
        seqs = pl.ds(step * n_seq_blk, n_seq_blk)
        out = []
        for h in range(MEM_H):
            out.append(pltpu.make_async_copy(k_hbm.at[seqs, :, h, :], kbuf.at[slot_, h], sem.at[0, slot_, h]))
            out.append(pltpu.make_async_copy(v_hbm.at[seqs, :, h, :], vbuf.at[slot_, h], sem.at[1, slot_, h]))
        return out

    @pl.when(i == 0)
    def _():
        for cp in copies(0, 0):
            cp.start()

    @pl.when(i + 1 < n_steps)
    def _():
        for cp in copies(i + 1, 1 - slot):
            cp.start()

    for cp in copies(i, slot):
        cp.wait()

    def store(p, val):
        o_ref[p[0], :, _mem_cols(p[1])] = val.astype(o_ref.dtype)
    probs = [(s, h) for s in range(n_seq_blk) for h in range(MEM_H)]
    _mem_attend(probs, lambda p: q_ref[p[0], :, _mem_cols(p[1])], lambda p: z_ref[p[0], :, _mem_cols(p[1])],
                lambda p: kbuf[slot, p[1], p[0]], lambda p: vbuf[slot, p[1], p[0]], store)


def _mem_sample(q, z, cache_k, cache_v, n_seq_blk=4):
    bd, rows, _ = q.shape
    blk = lambda i: (i, 0, 0)
    buf = pltpu.VMEM((2, MEM_H, n_seq_blk, N_MEM, MEM_DH), cache_k.dtype)
    return pl.pallas_call(
        functools.partial(_mem_sample_kernel, n_seq_blk=n_seq_blk),
        out_shape=jax.ShapeDtypeStruct(q.shape, BF16),
        grid=(bd // n_seq_blk,),
        in_specs=[pl.BlockSpec((n_seq_blk, rows, MEM_W), blk),
                  pl.BlockSpec((n_seq_blk, rows, MEM_W), blk),
                  pl.BlockSpec(memory_space=pl.ANY),
                  pl.BlockSpec(memory_space=pl.ANY)],
        out_specs=pl.BlockSpec((n_seq_blk, rows, MEM_W), blk),
        scratch_shapes=[buf, buf, pltpu.SemaphoreType.DMA((2, 2, MEM_H))],
        compiler_params=_cparams(("arbitrary",)),
        name="mem_sample",
    )(q, z, cache_k, cache_v)


def _merge_kernel(og_ref, os_ref, om_ref, gate_ref, x_ref, wb_ref, wo_ref, nf_ref, y_ref):
    merged = None
    for b, o_ref in enumerate((og_ref, os_ref, om_ref)):
        t = jnp.dot(o_ref[...], wb_ref[b], preferred_element_type=F32)
        t = t * gate_ref[:, b * D_MODEL:(b + 1) * D_MODEL].astype(F32)
        merged = t if merged is None else merged + t
    h = x_ref[...] + jnp.dot(merged.astype(BF16), wo_ref[...], preferred_element_type=F32)
    ms = jnp.mean(h * h, axis=-1, keepdims=True)
    y_ref[...] = h * lax.rsqrt(ms + NORM_EPS) * nf_ref[...]


def _merge(o_gdn, o_swa, o_mem, gates, x, w_branch, w_out, norm_f, tm=256):
    m = x.shape[0]
    tm = min(tm, m)
    row = lambda i: (i, 0)
    const2 = lambda i: (0, 0)
    return pl.pallas_call(
        _merge_kernel,
        out_shape=jax.ShapeDtypeStruct((m, D_MODEL), F32),
        grid=(m // tm,),
        in_specs=[pl.BlockSpec((tm, BR_W), row),
                  pl.BlockSpec((tm, BR_W), row),
                  pl.BlockSpec((tm, BR_W), row),
                  pl.BlockSpec((tm, N_BRANCH * D_MODEL), row),
                  pl.BlockSpec((tm, D_MODEL), row),
                  pl.BlockSpec((N_BRANCH, BR_W, D_MODEL), lambda i: (0, 0, 0),
                               pipeline_mode=pl.Buffered(1)),
                  pl.BlockSpec((D_MODEL, D_MODEL), const2, pipeline_mode=pl.Buffered(1)),
                  pl.BlockSpec((1, D_MODEL), const2)],
        out_specs=pl.BlockSpec((tm, D_MODEL), row),
        compiler_params=_cparams(("parallel",)),
        name="merge",
    )(o_gdn, o_swa, o_mem, gates, x, w_branch, w_out, norm_f)


_IN_SIZES = (GDN_QK, GDN_QK, GDN_W, GDN_W, GDN_H, GDN_H, SWA_QW, SWA_KVW, SWA_KVW, SWA_QW,
             MEM_W, MEM_W, N_BRANCH * D_MODEL)
_IN_NAMES = ("gq", "gk", "gv", "gz", "gb", "ga", "sq", "sk", "sv", "sz", "mq", "mz", "mg")
_IN_SPAN = {name: (int(off), int(off + size)) for name, off, size in
            zip(_IN_NAMES, np.cumsum((0,) + _IN_SIZES[:-1]), _IN_SIZES)}


def kernel(x_prompt, x_sample, state_gdn, state_gdn_conv, cache_swa_k, cache_swa_v, cache_mem_k,
           cache_mem_v, mem_prompt, norm_in, w_in, gdn_conv_w, gdn_a_log, gdn_dt_bias, gdn_norm,
           swa_sinks, rel_bias, norm_mem, w_mem_kv, w_branch, w_out, norm_f):
    n_layers = norm_in.shape[0]
    assert n_layers == 1
    b, seq, _ = x_prompt.shape
    bd, ns, _ = x_sample.shape
    wb = cache_swa_k.shape[2]
    assert seq % WINDOW == 0 and seq % GDN_CHUNK == 0 and ns + CONV_W <= SUBLANE and wb == WINDOW
    lyr = 0

    w = w_in[lyr]
    main_names = ("gq", "gk", "gv", "gz", "sq", "sz", "mq", "mz")
    n_main = len(main_names) * W_BLK
    n_gate = N_BRANCH * D_MODEL
    n_small = 3 * LANE
    small_lead = -(n_main + n_gate) % n_small
    w_all = _wprep(w.T, [[(_IN_SPAN[a][0], W_BLK)] for a in main_names]
                   + [[(_IN_SPAN["mg"][0] + W_BLK * k, W_BLK)] for k in range(n_gate // W_BLK)]
                   + [[(None, small_lead), (_IN_SPAN["sk"][0], 2 * SWA_KVW), (_IN_SPAN["gb"][0], 2 * GDN_H)]])
    col_small = n_main + n_gate + small_lead
    cb_gz, cb_sq, cb_sz, cb_mq, cb_mz = 3, 4, 5, 6, 7
    cb_sk, cb_sv, cb_ba = 0, 1, 2
    w_mkv = w_mem_kv[lyr].astype(BF16)
    w_br = w_branch[lyr].astype(BF16)
    w_o = w_out[lyr].astype(BF16)
    nw_in = norm_in[lyr].reshape(1, D_MODEL)
    nw_mem = norm_mem[lyr].reshape(1, D_MODEL)
    nw_f = norm_f.reshape(1, D_MODEL)
    conv_w = gdn_conv_w[lyr]
    a_row = jnp.pad(gdn_a_log[lyr].reshape(1, GDN_H), ((0, 0), (GDN_H, LANE - 2 * GDN_H)))
    dt_row = jnp.pad(gdn_dt_bias[lyr].reshape(1, GDN_H), ((0, 0), (GDN_H, LANE - 2 * GDN_H)))
    gnw = gdn_norm[lyr].reshape(1, GDN_DV)
    sinks = swa_sinks[lyr]
    bias_p = _bias_prompt(rel_bias)
    npad = SUBLANE
    bias_s = _bias_sample(rel_bias, ns, wb, wb + npad)
    sink_rows = jnp.broadcast_to(jnp.repeat(sinks.reshape(SWA_KV, SWA_G), ns, axis=1)[:, :, None],
                                 (SWA_KV, SWA_G * ns, LANE))

    t = b * seq
    xp = x_prompt.reshape(t, D_MODEL)
    xn_p, p_small = _rmsnorm(xp, nw_in, w_all, col_small, n_small)
    p_main = _proj(xn_p, w_all, BF16, n=n_main)
    g_p = _proj(xn_p, w_all, BF16, act="sigmoid", col0=n_main, n=n_gate)
    xn_tail = xn_p.reshape(b, seq, D_MODEL)[:, seq - SUBLANE:, :].reshape(b * SUBLANE, D_MODEL)
    conv_p = _proj(xn_tail, w_all, F32, n=GDN_CONV_CH).reshape(b, SUBLANE, GDN_CONV_CH)[:, SUBLANE - (CONV_W - 1):]
    kv_tail = p_small.reshape(b, seq, 3 * LANE)[:, seq - WINDOW:]
    swk_p = kv_tail[:, :, cb_sk * LANE:(cb_sk + 1) * LANE].reshape(b, WINDOW, SWA_KV, SWA_DH)
    swv_p = kv_tail[:, :, cb_sv * LANE:(cb_sv + 1) * LANE].reshape(b, WINDOW, SWA_KV, SWA_DH)

    mkv = _proj(_rmsnorm(mem_prompt.reshape(b * N_MEM, D_MODEL), nw_mem), w_mkv, F32)
    mk_p = mkv[:, :MEM_W].reshape(b, N_MEM, MEM_W)
    mv_p = mkv[:, MEM_W:].reshape(b, N_MEM, MEM_W)

    o_gdn_p, s_p = _gdn(p_main, p_main, p_small, jnp.zeros((b, HDR, GDN_CONV_CH), F32),
                        jnp.zeros((b, GDN_H, GDN_DK, GDN_DV), F32), conv_w, a_row, dt_row, gnw,
                        n_seq=b, chunk=GDN_CHUNK, n_chunks=4, per_chunk_state=False,
                        valid_lo=0, valid_hi=GDN_CHUNK, out_dtype=BF16,
                        z_colblock=cb_gz, ba_colblock=cb_ba)
    o_swa_p = _swa_prompt(p_main, p_small, bias_p, sinks, b, seq, cb_sq, cb_sz, cb_sk, cb_sv)
    o_mem_p = _mem_prompt(p_main, mkv, b, seq, cb_mq, cb_mz)
    y_p = _merge(o_gdn_p, o_swa_p, o_mem_p, g_p, xp, w_br, w_o, nw_f).reshape(b, seq, D_MODEL)

    ts = bd * ns
    xs = x_sample.reshape(ts, D_MODEL)
    xn_s, s_small = _rmsnorm(xs, nw_in, w_all, col_small, n_small)
    s_main = _proj(xn_s, w_all, F32, n=n_main)
    g_s = _proj(xn_s, w_all, BF16, act="sigmoid", col0=n_main, n=n_gate)
    s_gdn = s_main[:, :GDN_CONV_CH + GDN_W]
    s_ba = s_small[:, cb_ba * LANE:(cb_ba + 1) * LANE]
    s_swa = jnp.concatenate([s_main[:, cb_sq * BR_W:(cb_sz + 1) * BR_W], s_small[:, :2 * LANE]], axis=1)
    s_mem = s_main[:, cb_mq * BR_W:(cb_mz + 1) * BR_W].astype(BF16)

    lo = CONV_W - 1
    hi = lo + ns
    pad_rows = ((0, 0), (lo, SUBLANE - hi), (0, 0))
    e_qkv = jnp.concatenate([state_gdn_conv[lyr], s_gdn[:, :GDN_CONV_CH].reshape(bd, ns, GDN_CONV_CH),
                             jnp.zeros((bd, SUBLANE - hi, GDN_CONV_CH), F32)], axis=1)
    e_z = jnp.pad(s_gdn[:, GDN_CONV_CH:].reshape(bd, ns, GDN_W), pad_rows)
    e_ba = jnp.pad(s_ba.reshape(bd, ns, LANE), pad_rows)
    seq_blk = 8
    o_gdn_s8, s_s = _gdn(e_qkv.reshape(bd * SUBLANE, GDN_CONV_CH), e_z.reshape(bd * SUBLANE, GDN_W),
                         e_ba.reshape(bd * SUBLANE, LANE), jnp.zeros((1, HDR, GDN_CONV_CH), F32),
                         state_gdn[lyr], conv_w, a_row, dt_row, gnw,
                         n_seq=bd, chunk=SUBLANE, n_chunks=seq_blk, per_chunk_state=True,
                         valid_lo=lo, valid_hi=hi, out_dtype=BF16)
    o_gdn_s = o_gdn_s8.reshape(bd, SUBLANE, GDN_W)[:, lo:hi].reshape(ts, GDN_W)
    conv_s = e_qkv[:, hi - (CONV_W - 1):hi]

    def to_heads(a):
        return a.reshape(bd, ns, SWA_KV, SWA_G, SWA_DH).transpose(0, 2, 3, 1, 4).reshape(
            bd, SWA_KV, SWA_G * ns, SWA_DH)

    k_new = s_swa[:, 2 * SWA_QW:2 * SWA_QW + SWA_KVW].reshape(bd, ns, SWA_KVW)
    v_new = s_swa[:, 2 * SWA_QW + SWA_KVW:].reshape(bd, ns, SWA_KVW)
    tok_pad = ((0, 0), (0, npad - ns), (0, 0))
    ck_t = cache_swa_k[lyr].transpose(0, 2, 3, 1)
    cv_t = cache_swa_v[lyr].transpose(0, 2, 3, 1)
    o_swa_h, nk_t, nv_t = _swa_sample(
        to_heads(s_swa[:, :SWA_QW]).astype(BF16), to_heads(s_swa[:, SWA_QW:2 * SWA_QW]),
        jnp.pad(k_new, tok_pad), jnp.pad(v_new, tok_pad), ck_t, cv_t, bias_s, sink_rows, n_new=ns)
    o_swa_s = o_swa_h.reshape(bd, SWA_KV, SWA_G, ns, SWA_DH).transpose(0, 3, 1, 2, 4).reshape(ts, SWA_QW)
    swk_s = nk_t.transpose(0, 3, 1, 2)
    swv_s = nv_t.transpose(0, 3, 1, 2)

    mq = jnp.pad(s_mem[:, :MEM_W].reshape(bd, ns, MEM_W), tok_pad)
    mz = jnp.pad(s_mem[:, MEM_W:].reshape(bd, ns, MEM_W), tok_pad)
    o_mem_s = _mem_sample(mq, mz, cache_mem_k[lyr], cache_mem_v[lyr])[:, :ns].reshape(ts, MEM_W)
    y_s = _merge(o_gdn_s, o_swa_s, o_mem_s, g_s, xs, w_br, w_o, nw_f).reshape(bd, ns, D_MODEL)

    return (y_p, y_s,
            s_p[None], conv_p[None], swk_p[None], swv_p[None],
            mk_p.reshape(b, N_MEM, MEM_H, MEM_DH)[None], mv_p.reshape(b, N_MEM, MEM_H, MEM_DH)[None],
            s_s[None], conv_s[None], swk_s[None], swv_s[None])
```

```python
import functools
import math

import numpy as np
import jax
import jax.numpy as jnp
from jax import lax
from jax.experimental import pallas as pl
from jax.experimental.pallas import tpu as pltpu

F32 = jnp.float32
BF16 = jnp.bfloat16

D_MODEL = 2048
N_BRANCH = 3
BR_W = 1024
GDN_H = 8
GDN_DK = 128
GDN_DV = 128
GDN_QK = GDN_H * GDN_DK
GDN_W = GDN_H * GDN_DV
GDN_CONV_CH = 2 * GDN_QK + GDN_W
CONV_W = 4
GDN_CHUNK = 64
SWA_H = 16
SWA_KV = 2
SWA_G = SWA_H // SWA_KV
SWA_DH = 64
SWA_QW = SWA_H * SWA_DH
SWA_KVW = SWA_KV * SWA_DH
WINDOW = 128
N_BUCKETS = 32
MAX_DISTANCE = 128
N_MEM = 256
MEM_H = 4
MEM_DH = 256
MEM_W = MEM_H * MEM_DH
NORM_EPS = 1e-6

LANE = 128
SUBLANE = 8
VMEM_LIMIT = 52 * 1024 * 1024


def _cparams(sem):
    return pltpu.CompilerParams(dimension_semantics=sem, vmem_limit_bytes=VMEM_LIMIT)


def _sigmoid(x):
    return 0.5 * jnp.tanh(0.5 * x) + 0.5


def _silu(x):
    h = 0.5 * x
    return h * jnp.tanh(h) + h


def _softplus(x):
    return jnp.maximum(x, 0.0) + jnp.log(1.0 + jnp.exp(-jnp.abs(x)))


def _dot(a, b):
    return jnp.dot(a.astype(BF16), b.astype(BF16), preferred_element_type=F32)


def _dot_nt(a, b):
    return lax.dot_general(a.astype(BF16), b.astype(BF16), (((1,), (1,)), ((), ())),
                           preferred_element_type=F32)


def _dot_tn(a, b):
    return lax.dot_general(a.astype(BF16), b.astype(BF16), (((0,), (0,)), ((), ())),
                           preferred_element_type=F32)


def _dot_f32(a, b):
    return jnp.dot(a, b, preferred_element_type=F32, precision=lax.Precision.HIGHEST)


def _rmsnorm_kernel(x_ref, nw_ref, *rest):
    x = x_ref[...].astype(F32)
    ms = jnp.mean(x * x, axis=-1, keepdims=True)
    xn = (x * lax.rsqrt(ms + NORM_EPS) * nw_ref[...]).astype(BF16)
    if len(rest) == 1:
        rest[0][...] = xn
    else:
        w_ref, o_ref, p_ref = rest
        o_ref[...] = xn
        p_ref[...] = jnp.dot(xn, w_ref[...], preferred_element_type=F32)


def _rmsnorm(x, norm_w, w=None, col0=0, n=None, tm_pref=512):
    m, d = x.shape
    tm = min(m, tm_pref)
    assert m % tm == 0
    row = lambda i: (i, 0)
    in_specs = [pl.BlockSpec((tm, d), row), pl.BlockSpec((1, d), lambda i: (0, 0))]
    out_shape = jax.ShapeDtypeStruct((m, d), BF16)
    out_specs = pl.BlockSpec((tm, d), row)
    args = (x, norm_w)
    if w is not None:
        assert col0 % n == 0
        in_specs.append(pl.BlockSpec((d, n), lambda i: (0, col0 // n)))
        out_shape = (out_shape, jax.ShapeDtypeStruct((m, n), F32))
        out_specs = (out_specs, pl.BlockSpec((tm, n), row))
        args = args + (w,)
    return pl.pallas_call(
        _rmsnorm_kernel,
        out_shape=out_shape,
        grid=(m // tm,),
        in_specs=in_specs,
        out_specs=out_specs,
        compiler_params=_cparams(("parallel",)),
        name="rmsnorm",
    )(*args)


def _proj_kernel(x_ref, w_ref, o_ref, *, act):
    y = jnp.dot(x_ref[...], w_ref[...], preferred_element_type=F32)
    if act == "sigmoid":
        y = _sigmoid(y)
    o_ref[...] = y.astype(o_ref.dtype)


def _pick_tile(n, pref):
    t = min(n, pref)
    while n % t:
        t -= LANE
    return t


def _proj(xn, w, out_dtype, act=None, col0=0, n=None, tm_pref=1024, tn_pref=2048):
    m, d = xn.shape
    n = w.shape[1] if n is None else n
    tm = min(m, tm_pref)
    assert m % tm == 0
    tn = _pick_tile(n, tn_pref if out_dtype == BF16 else tn_pref // 2)
    while col0 % tn:
        tn = _pick_tile(n, tn - LANE)
    jb = col0 // tn
    return pl.pallas_call(
        functools.partial(_proj_kernel, act=act),
        out_shape=jax.ShapeDtypeStruct((m, n), out_dtype),
        grid=(m // tm, n // tn),
        in_specs=[pl.BlockSpec((tm, d), lambda i, j: (i, 0)),
                  pl.BlockSpec((d, tn), lambda i, j: (0, jb + j))],
        out_specs=pl.BlockSpec((tm, tn), lambda i, j: (i, j)),
        compiler_params=_cparams(("parallel", "arbitrary")),
        name="proj",
    )(xn, w)


W_BLK = 1024
W_PIECE_ROWS = (W_BLK, 2 * SUBLANE)


def _wprep_kernel(wt_hbm, o_ref, inbuf0, inbuf1, sem, *, blocks):
    j = pl.program_id(0)
    inbuf = (inbuf0, inbuf1)

    def sources(b):
        return [(start, rows) for start, rows in blocks[b] if start is not None]

    def pieces(b):
        return [pltpu.make_async_copy(wt_hbm.at[pl.ds(start, rows), :],
                                      inbuf[k].at[b % 2, pl.ds(0, rows), :], sem.at[b % 2, k])
                for k, (start, rows) in enumerate(sources(b))]

    for b, ranges in enumerate(blocks):
        @pl.when(j == b)
        def _(b=b, ranges=ranges):
            if b == 0:
                for cp in pieces(0):
                    cp.start()
            if b + 1 < len(blocks):
                for cp in pieces(b + 1):
                    cp.start()
            for cp in pieces(b):
                cp.wait()
            n_have = sum(rows for _, rows in ranges)
            parts, k = [], 0
            for start, rows in tuple(ranges) + ((None, W_BLK - n_have),):
                if start is None:
                    if rows:
                        parts.append(jnp.zeros((rows, inbuf0.shape[2]), inbuf0.dtype))
                else:
                    parts.append(inbuf[k][b % 2, 0:rows, :])
                    k += 1
            val = parts[0] if len(parts) == 1 else jnp.concatenate(parts, axis=0)
            o_ref[...] = val.T.astype(BF16)


def _wprep(wt, block_srcs):
    ncol, d = wt.shape
    for ranges in block_srcs:
        srcs = [(s, rows) for s, rows in ranges if s is not None]
        assert len(srcs) <= len(W_PIECE_ROWS) and all(rows % SUBLANE == 0 for _, rows in ranges)
        for (s, rows), cap in zip(srcs, W_PIECE_ROWS):
            assert s % SUBLANE == 0 and rows <= cap and s + rows <= ncol
    return pl.pallas_call(
        functools.partial(_wprep_kernel, blocks=tuple(tuple(r) for r in block_srcs)),
        out_shape=jax.ShapeDtypeStruct((d, W_BLK * len(block_srcs)), BF16),
        grid=(len(block_srcs),),
        in_specs=[pl.BlockSpec(memory_space=pl.ANY)],
        out_specs=pl.BlockSpec((d, W_BLK), lambda j: (0, j)),
        scratch_shapes=[pltpu.VMEM((2, cap, d), wt.dtype) for cap in W_PIECE_ROWS]
                       + [pltpu.SemaphoreType.DMA((2, len(W_PIECE_ROWS)))],
        compiler_params=_cparams(("arbitrary",)),
        name="wprep",
    )(wt)


HDR = SUBLANE


def _tri_inv_many(a_list, c):
    row = lax.broadcasted_iota(jnp.int32, (c, c), 0)
    col = lax.broadcasted_iota(jnp.int32, (c, c), 1)
    eye = jnp.where(row == col, 1.0, 0.0).astype(F32)
    xs = [eye - a for a in a_list]
    bs = [_dot(a, a) for a in a_list]
    n = 2
    while n < c:
        xs = [x + _dot(x, b) for x, b in zip(xs, bs)]
        n *= 2
        if n < c:
            bs = [_dot(b, b) for b in bs]
    return xs


def _gdn_kernel(qkv_ref, z_ref, ba_ref, hdr_ref, s0_ref, cw_ref, arow_ref, dtrow_ref, nw_ref,
                o_ref, s_ref, buf_ref, cv_ref, gb_ref, oacc_ref, sol_ref, att_ref, gl_ref,
                *, chunk, n_chunks, per_chunk_state, valid_lo, valid_hi):
    c = chunk
    tb = c * n_chunks

    if per_chunk_state:
        buf_ref[0:2 * HDR, :] = jnp.zeros((2 * HDR, GDN_CONV_CH), F32)
    else:
        @pl.when(pl.program_id(1) == 0)
        def _():
            buf_ref[0:HDR, :] = hdr_ref[0]
            buf_ref[HDR:2 * HDR, :] = jnp.zeros((HDR, GDN_CONV_CH), F32)
            s_ref[...] = s0_ref[...]

    def tap_from_history(j, n_rows):
        off = HDR - (CONV_W - 1) + j
        return buf_ref[off:off + n_rows, :] * cw_ref[j:j + 1, :]

    if qkv_ref.dtype == BF16:
        xb = qkv_ref[...]
        r = lax.broadcasted_iota(jnp.int32, (tb, tb), 0)
        cc = lax.broadcasted_iota(jnp.int32, (tb, tb), 1)
        acc = xb.astype(F32) * cw_ref[CONV_W - 1:CONV_W, :]
        for j in range(CONV_W - 1):
            shift = jnp.where(r - cc == CONV_W - 1 - j, 1.0, 0.0).astype(BF16)
            acc = acc + jnp.dot(shift, xb, preferred_element_type=F32) * cw_ref[j:j + 1, :]
        cv_ref[...] = _silu(acc)
        top = acc[0:HDR]
        for j in range(CONV_W - 1):
            top = top + tap_from_history(j, HDR)
        cv_ref[0:HDR, :] = _silu(top)
        buf_ref[0:HDR, :] = qkv_ref[tb - 2 * HDR:tb, :].astype(F32)[HDR:]
    else:
        buf_ref[HDR:HDR + tb, :] = qkv_ref[...].astype(F32)
        acc = None
        for j in range(CONV_W):
            term = tap_from_history(j, tb)
            acc = term if acc is None else acc + term
        cv_ref[...] = _silu(acc)
        if not per_chunk_state:
            buf_ref[0:HDR, :] = buf_ref[tb:tb + HDR, :]

    ba = ba_ref[...].astype(F32)
    beta_all = _sigmoid(ba)
    g_all = -jnp.exp(arow_ref[...]) * _softplus(ba + dtrow_ref[...])
    if per_chunk_state:
        r = lax.broadcasted_iota(jnp.int32, (tb, LANE), 0) & (c - 1)
        valid = (r >= valid_lo) & (r < valid_hi)
        beta_all = jnp.where(valid, beta_all, 0.0)
        g_all = jnp.where(valid, g_all, 0.0)
    gb_ref[0] = beta_all
    gb_ref[1] = g_all

    row = lax.broadcasted_iota(jnp.int32, (c, c), 0)
    col = lax.broadcasted_iota(jnp.int32, (c, c), 1)
    causal = row >= col
    strict = row > col
    tril = jnp.where(causal, 1.0, 0.0).astype(F32)
    scale_q = GDN_DK ** -0.5

    heads = range(GDN_H)
    qcol = lambda h: slice(h * GDN_DK, (h + 1) * GDN_DK)
    kcol = lambda h: slice(GDN_QK + h * GDN_DK, GDN_QK + (h + 1) * GDN_DK)
    vcol = lambda h: slice(2 * GDN_QK + h * GDN_DV, 2 * GDN_QK + (h + 1) * GDN_DV)
    ucol = lambda h: slice(h * (GDN_DV + GDN_DK), h * (GDN_DV + GDN_DK) + GDN_DV)
    wcol = lambda h: slice(h * (GDN_DV + GDN_DK) + GDN_DV, (h + 1) * (GDN_DV + GDN_DK))
    if per_chunk_state:
        rv = lax.broadcasted_iota(jnp.int32, (c, 1), 0)
        rvalid = (rv >= valid_lo) & (rv < valid_hi)

    def chunk_rows(ci):
        return slice(ci * c, (ci + 1) * c)

    def prep():
        probs = []
        for ci in range(n_chunks):
            rows = chunk_rows(ci)
            beta_c = gb_ref[0, rows, :]
            gc_all = _dot_f32(tril, gb_ref[1, rows, :])
            gc_t = gc_all.T
            gl_ref[ci] = gc_all[c - 1:c, :]
            for h in heads:
                q = cv_ref[rows, qcol(h)]
                k = cv_ref[rows, kcol(h)]
                v = cv_ref[rows, vcol(h)]
                q = q * lax.rsqrt(jnp.sum(q * q, axis=-1, keepdims=True) + NORM_EPS) * scale_q
                k = k * lax.rsqrt(jnp.sum(k * k, axis=-1, keepdims=True) + NORM_EPS)
                if per_chunk_state:
                    q = jnp.where(rvalid, q, 0.0)
                    k = jnp.where(rvalid, k, 0.0)
                    v = jnp.where(rvalid, v, 0.0)
                beta = beta_c[:, h:h + 1]
                gc_col = gc_all[:, GDN_H + h:GDN_H + h + 1]
                gc_row = gc_t[GDN_H + h:GDN_H + h + 1, :]
                diff = jnp.where(causal, gc_col - gc_row, 0.0)
                decay = jnp.where(causal, jnp.exp(diff), 0.0)
                e_gc = jnp.exp(gc_col)
                kb = k * beta
                rhs = jnp.concatenate([v * beta, kb * e_gc], axis=1)
                cv_ref[rows, qcol(h)] = q * e_gc
                cv_ref[rows, kcol(h)] = k * jnp.exp(gc_col[c - 1:c, :] - gc_col)
                probs.append((rows, h, q, k, kb, rhs, decay))
        kq = [_dot_nt(jnp.concatenate([kb, q], axis=0), k) for (_, _, q, k, kb, _, _) in probs]
        a_list = [jnp.where(strict, kq_i[:c] * p[6], 0.0) for kq_i, p in zip(kq, probs)]
        t_inv = _tri_inv_many(a_list, c)
        for t_i, kq_i, (rows, h, _, _, _, rhs, decay) in zip(t_inv, kq, probs):
            sol_ref[rows, h * (GDN_DV + GDN_DK):(h + 1) * (GDN_DV + GDN_DK)] = _dot(t_i, rhs)
            att_ref[h, rows, :] = kq_i[c:] * decay

    def scan(ci):
        rows = chunk_rows(ci)
        si = ci if per_chunk_state else 0
        g_tot = jnp.exp(gl_ref[ci])
        s_old = [s_ref[si, h] for h in heads]
        wq_s = [_dot(jnp.concatenate([sol_ref[rows, wcol(h)], cv_ref[rows, qcol(h)]], axis=0), s_old[h])
                for h in heads]
        v_new = [sol_ref[rows, ucol(h)] - wq_s[h][:c] for h in heads]
        o_att = [_dot(att_ref[h, rows, :], v_new[h]) for h in heads]
        for h in heads:
            s_ref[si, h] = (s_old[h] * g_tot[:, GDN_H + h:GDN_H + h + 1]
                            + _dot_tn(cv_ref[rows, kcol(h)], v_new[h]))
        for h in heads:
            o = wq_s[h][c:] + o_att[h]
            o = o * lax.rsqrt(jnp.mean(o * o, axis=-1, keepdims=True) + NORM_EPS) * nw_ref[...]
            oacc_ref[rows, h * GDN_DV:(h + 1) * GDN_DV] = o

    if per_chunk_state:
        s_ref[...] = s0_ref[...]
    prep()
    for ci in range(n_chunks):
        scan(ci)
    o_ref[...] = (oacc_ref[...] * _silu(z_ref[...].astype(F32))).astype(o_ref.dtype)


def _gdn(qkv, z, ba, hdr, s0, conv_w, a_row, dt_row, norm_w, *, n_seq, chunk, n_chunks,
         per_chunk_state, valid_lo, valid_hi, out_dtype, z_colblock=0, ba_colblock=0):
    rows = qkv.shape[0]
    tb = chunk * n_chunks
    if per_chunk_state:
        grid = (rows // tb,)
        sem = ("arbitrary",)
        rmap = lambda i: (i, 0)
        zmap = lambda i: (i, z_colblock)
        bmap = lambda i: (i, ba_colblock)
        hmap = lambda i: (0, 0, 0)
        smap = lambda i: (i, 0, 0, 0)
        cmap = lambda i: (0, 0)
        ns = n_chunks
    else:
        steps = rows // n_seq // tb
        grid = (n_seq, steps)
        sem = ("parallel", "arbitrary")
        rmap = lambda b, n: (b * steps + n, 0)
        zmap = lambda b, n: (b * steps + n, z_colblock)
        bmap = lambda b, n: (b * steps + n, ba_colblock)
        hmap = lambda b, n: (b, 0, 0)
        smap = lambda b, n: (b, 0, 0, 0)
        cmap = lambda b, n: (0, 0)
        ns = 1
    kern = functools.partial(_gdn_kernel, chunk=chunk, n_chunks=n_chunks,
                             per_chunk_state=per_chunk_state, valid_lo=valid_lo, valid_hi=valid_hi)
    return pl.pallas_call(
        kern,
        out_shape=(jax.ShapeDtypeStruct((rows, GDN_W), out_dtype),
                   jax.ShapeDtypeStruct(s0.shape, F32)),
        grid=grid,
        in_specs=[pl.BlockSpec((tb, GDN_CONV_CH), rmap),
                  pl.BlockSpec((tb, GDN_W), zmap),
                  pl.BlockSpec((tb, LANE), bmap),
                  pl.BlockSpec((1, HDR, GDN_CONV_CH), hmap),
                  pl.BlockSpec((ns, GDN_H, GDN_DK, GDN_DV), smap),
                  pl.BlockSpec((CONV_W, GDN_CONV_CH), cmap),
                  pl.BlockSpec((1, LANE), cmap),
                  pl.BlockSpec((1, LANE), cmap),
                  pl.BlockSpec((1, GDN_DV), cmap)],
        out_specs=(pl.BlockSpec((tb, GDN_W), rmap),
                   pl.BlockSpec((ns, GDN_H, GDN_DK, GDN_DV), smap)),
        scratch_shapes=[pltpu.VMEM((HDR + tb, GDN_CONV_CH), F32),
                        pltpu.VMEM((tb, GDN_CONV_CH), F32),
                        pltpu.VMEM((2, tb, LANE), F32),
                        pltpu.VMEM((tb, GDN_W), F32),
                        pltpu.VMEM((tb, GDN_H * (GDN_DV + GDN_DK)), F32),
                        pltpu.VMEM((GDN_H, tb, chunk), F32),
                        pltpu.VMEM((n_chunks, 1, LANE), F32)],
        compiler_params=_cparams(sem),
        name="gdn",
    )(qkv, z, ba, hdr, s0, conv_w, a_row, dt_row, norm_w)


def _t5_bucket_np(dist):
    n = np.maximum(dist, 0)
    max_exact = N_BUCKETS // 2
    nf = np.maximum(n, 1).astype(np.float32)
    large = max_exact + (np.log(nf / np.float32(max_exact)) / np.float32(math.log(MAX_DISTANCE / max_exact))
                         * np.float32(N_BUCKETS - max_exact)).astype(np.int32)
    large = np.minimum(large, N_BUCKETS - 1)
    return np.where(n < max_exact, n, large).astype(np.int32)


def _bias_prompt_kernel(code_ref, tab_ref, o_ref):
    code = code_ref[0]
    for h in range(SWA_H):
        acc = jnp.full(code.shape, -jnp.inf, F32)
        for b in range(N_BUCKETS):
            acc = jnp.where(code == b, tab_ref[b, h], acc)
        o_ref[0, h] = acc


def _bias_prompt(table):
    qi = np.arange(WINDOW)[None, :]
    sj = np.arange(WINDOW)[:, None]
    own = sj <= qi
    bucket = _t5_bucket_np(np.where(own, qi - sj, qi + WINDOW - sj))
    code_first = np.where(own, bucket, -1)
    code = jnp.asarray(np.stack([code_first, bucket]).astype(np.int32))
    return pl.pallas_call(
        _bias_prompt_kernel,
        out_shape=jax.ShapeDtypeStruct((2, SWA_H, WINDOW, WINDOW), F32),
        grid=(2,),
        in_specs=[pl.BlockSpec((1, WINDOW, WINDOW), lambda v: (v, 0, 0)),
                  pl.BlockSpec(memory_space=pltpu.SMEM)],
        out_specs=pl.BlockSpec((1, SWA_H, WINDOW, WINDOW), lambda v: (v, 0, 0, 0)),
        compiler_params=_cparams(("arbitrary",)),
        name="swa_bias_prompt",
    )(code, table)


def _bias_sample_kernel(code_ref, tab_ref, o_ref):
    kv = pl.program_id(0)
    code = code_ref[...]
    acc = jnp.full(code.shape, -jnp.inf, F32)
    for g in range(SWA_G):
        for b in range(N_BUCKETS):
            acc = jnp.where(code == b + N_BUCKETS * g, tab_ref[b, kv * SWA_G + g], acc)
    o_ref[0] = acc


def _bias_sample(table, n_tok, n_cache, n_keys_pad):
    dist = (n_cache + np.arange(n_tok))[:, None] - np.arange(n_keys_pad)[None, :]
    valid = (dist >= 0) & (dist < WINDOW) & (np.arange(n_keys_pad)[None, :] < n_cache + n_tok)
    bucket = _t5_bucket_np(dist)
    code_t = np.where(valid, bucket, -1)
    g = np.arange(SWA_G)[:, None, None]
    code = np.where(code_t[None] >= 0, code_t[None] + N_BUCKETS * g, -1)
    code = jnp.asarray(code.reshape(SWA_G * n_tok, n_keys_pad).astype(np.int32))
    return pl.pallas_call(
        _bias_sample_kernel,
        out_shape=jax.ShapeDtypeStruct((SWA_KV, SWA_G * n_tok, n_keys_pad), F32),
        grid=(SWA_KV,),
        in_specs=[pl.BlockSpec((SWA_G * n_tok, n_keys_pad), lambda k: (0, 0)),
                  pl.BlockSpec(memory_space=pltpu.SMEM)],
        out_specs=pl.BlockSpec((1, SWA_G * n_tok, n_keys_pad), lambda k: (k, 0, 0)),
        compiler_params=_cparams(("arbitrary",)),
        name="swa_bias_sample",
    )(code, table)


def _sink_softmax_pv(logits, sink, v):
    m = jnp.maximum(jnp.max(logits, axis=-1, keepdims=True), sink)
    p = jnp.exp(logits - m)
    den = jnp.sum(p, axis=-1, keepdims=True) + jnp.exp(sink - m)
    return _dot(p, v) / den


def _swa_prompt_kernel(q_ref, z_ref, kc_ref, kp_ref, vc_ref, vp_ref, bias_ref, sink_ref, o_ref, *, n_qblk):
    kall = jnp.concatenate([kp_ref[...], kc_ref[...]], axis=0)
    vall = jnp.concatenate([vp_ref[...], vc_ref[...]], axis=0)
    scale = SWA_DH ** -0.5
    key = lax.broadcasted_iota(jnp.int32, (WINDOW, WINDOW), 0)
    qry = lax.broadcasted_iota(jnp.int32, (WINDOW, WINDOW), 1)
    from_prev = key > qry
    first_variant = jnp.where(pl.program_id(1) == 0, 0, 1)
    cs = lambda h: slice(h * SWA_DH, (h + 1) * SWA_DH)
    rows = lambda qb: slice(qb * WINDOW, (qb + 1) * WINDOW)
    keys = lambda qb: slice(qb * WINDOW, (qb + 2) * WINDOW)
    for kv in range(SWA_KV):
        ks = slice(kv * SWA_DH, (kv + 1) * SWA_DH)
        k_kv = kall[:, ks].astype(BF16)
        v_t = vall[:, ks].astype(F32).T.astype(BF16)
        probs = [(qb, h) for qb in range(n_qblk) for h in range(kv * SWA_G, (kv + 1) * SWA_G)]
        lg = [_dot_nt(k_kv[keys(qb)], q_ref[rows(qb), cs(h)] * scale) for qb, h in probs]
        ps, dens = [], []
        for (qb, h), l in zip(probs, lg):
            bias = bias_ref[first_variant if qb == 0 else 1, h]
            l = jnp.where(from_prev, l[:WINDOW], l[WINDOW:]) + bias
            sink = sink_ref[h]
            m = jnp.maximum(jnp.max(l, axis=0, keepdims=True), sink)
            p = jnp.exp(l - m)
            dens.append(jnp.sum(p, axis=0, keepdims=True) + jnp.exp(sink - m))
            ps.append(jnp.concatenate([jnp.where(from_prev, p, 0.0), jnp.where(from_prev, 0.0, p)],
                                      axis=0))
        outs = [_dot(v_t[:, keys(qb)], p) / den for p, den, (qb, _) in zip(ps, dens, probs)]
        for j in range(0, len(probs), 2):
            qb, h = probs[j]
            two = slice(h * SWA_DH, (h + 2) * SWA_DH)
            o2 = jnp.concatenate([outs[j], outs[j + 1]], axis=0).T
            o_ref[rows(qb), two] = (o2 * _silu(z_ref[rows(qb), two].astype(F32))).astype(o_ref.dtype)


def _swa_prompt(proj, kv, bias, sinks, n_seq, seq_len, q_colblock, z_colblock, k_colblock, v_colblock,
                n_qblk=4):
    n_qblk = math.gcd(n_qblk, seq_len // WINDOW)
    tq = n_qblk * WINDOW
    steps = seq_len // tq
    cur = lambda b, n: b * steps + n
    prev = lambda b, n: (b * steps + n) * n_qblk - jnp.where(n == 0, 0, 1)
    return pl.pallas_call(
        functools.partial(_swa_prompt_kernel, n_qblk=n_qblk),
        out_shape=jax.ShapeDtypeStruct((n_seq * seq_len, SWA_QW), BF16),
        grid=(n_seq, steps),
        in_specs=[pl.BlockSpec((tq, SWA_QW), lambda b, n: (cur(b, n), q_colblock)),
                  pl.BlockSpec((tq, SWA_QW), lambda b, n: (cur(b, n), z_colblock)),
                  pl.BlockSpec((tq, SWA_KVW), lambda b, n: (cur(b, n), k_colblock)),
                  pl.BlockSpec((WINDOW, SWA_KVW), lambda b, n: (prev(b, n), k_colblock)),
                  pl.BlockSpec((tq, SWA_KVW), lambda b, n: (cur(b, n), v_colblock)),
                  pl.BlockSpec((WINDOW, SWA_KVW), lambda b, n: (prev(b, n), v_colblock)),
                  pl.BlockSpec((2, SWA_H, WINDOW, WINDOW), lambda b, n: (0, 0, 0, 0)),
                  pl.BlockSpec(memory_space=pltpu.SMEM)],
        out_specs=pl.BlockSpec((tq, SWA_QW), lambda b, n: (cur(b, n), 0)),
        compiler_params=_cparams(("parallel", "arbitrary")),
        name="swa_prompt",
    )(proj, proj, kv, kv, kv, kv, bias, sinks)


def _swa_sample_kernel(q_ref, z_ref, kn_ref, vn_ref, ck_ref, cv_ref, bias_ref, sink_ref, o_ref, nk_ref, nv_ref,
                       *, n_seq_blk, n_new):
    scale = SWA_DH ** -0.5
    wb = ck_ref.shape[3]
    probs = [(s, kv) for s in range(n_seq_blk) for kv in range(SWA_KV)]
    ks = lambda kv: slice(kv * SWA_DH, (kv + 1) * SWA_DH)
    slot = lax.broadcasted_iota(jnp.int32, (SWA_DH, wb), 1)
    tok = lax.broadcasted_iota(jnp.int32, (kn_ref.shape[1], wb), 0)
    tok_slot = lax.broadcasted_iota(jnp.int32, (kn_ref.shape[1], wb), 1)
    place = jnp.where((tok_slot == tok + (wb - n_new)) & (tok < n_new), 1.0, 0.0).astype(F32)
    for s in range(n_seq_blk):
        for cache_ref, new_ref, out_ref in ((ck_ref, kn_ref, nk_ref), (cv_ref, vn_ref, nv_ref)):
            new_t = lax.dot_general(new_ref[s], place, (((0,), (0,)), ((), ())),
                                    preferred_element_type=F32, precision=lax.Precision.HIGHEST)
            for kv in range(SWA_KV):
                out_ref[s, kv] = jnp.where(slot >= wb - n_new, new_t[ks(kv), :],
                                           pltpu.roll(cache_ref[s, kv], wb - n_new, axis=1))
    lc = [_dot(q_ref[s, kv] * scale, ck_ref[s, kv]) + bias_ref[kv, :, 0:wb] for s, kv in probs]
    ln = [_dot_nt(q_ref[s, kv] * scale, kn_ref[s, :, ks(kv)]) + bias_ref[kv, :, wb:] for s, kv in probs]
    pcs, pns, dens = [], [], []
    for (s, kv), c, n in zip(probs, lc, ln):
        sink = sink_ref[kv][:, 0:1]
        m = jnp.maximum(jnp.maximum(jnp.max(c, axis=-1, keepdims=True), jnp.max(n, axis=-1, keepdims=True)),
                        sink)
        pc = jnp.exp(c - m)
        pn = jnp.exp(n - m)
        pcs.append(pc)
        pns.append(pn)
        dens.append(jnp.sum(pc, axis=-1, keepdims=True) + jnp.sum(pn, axis=-1, keepdims=True)
                    + jnp.exp(sink - m))
    outs = [(_dot_nt(pc, cv_ref[s, kv]) + _dot(pn, vn_ref[s, :, ks(kv)])) / den
            for (s, kv), pc, pn, den in zip(probs, pcs, pns, dens)]
    for oh, (s, kv) in zip(outs, probs):
        o_ref[s, kv] = (oh * _silu(z_ref[s, kv].astype(F32))).astype(o_ref.dtype)


def _swa_sample(q, z, k_new, v_new, cache_kt, cache_vt, bias, sink_rows, n_new, n_seq_blk=8):
    bd, _, rows, _ = q.shape
    wb = cache_kt.shape[3]
    npad = k_new.shape[1]
    assert wb == LANE
    blk4 = lambda i: (i, 0, 0, 0)
    blk3 = lambda i: (i, 0, 0)
    cache_spec = pl.BlockSpec((n_seq_blk, SWA_KV, SWA_DH, wb), blk4)
    return pl.pallas_call(
        functools.partial(_swa_sample_kernel, n_seq_blk=n_seq_blk, n_new=n_new),
        out_shape=(jax.ShapeDtypeStruct(q.shape, BF16),
                   jax.ShapeDtypeStruct(cache_kt.shape, F32), jax.ShapeDtypeStruct(cache_vt.shape, F32)),
        grid=(bd // n_seq_blk,),
        in_specs=[pl.BlockSpec((n_seq_blk, SWA_KV, rows, SWA_DH), blk4),
                  pl.BlockSpec((n_seq_blk, SWA_KV, rows, SWA_DH), blk4),
                  pl.BlockSpec((n_seq_blk, npad, SWA_KVW), blk3),
                  pl.BlockSpec((n_seq_blk, npad, SWA_KVW), blk3),
                  pl.BlockSpec((n_seq_blk, SWA_KV, SWA_DH, wb), blk4),
                  pl.BlockSpec((n_seq_blk, SWA_KV, SWA_DH, wb), blk4),
                  pl.BlockSpec((SWA_KV, rows, wb + npad), lambda i: (0, 0, 0)),
                  pl.BlockSpec((SWA_KV, rows, LANE), lambda i: (0, 0, 0))],
        out_specs=(pl.BlockSpec((n_seq_blk, SWA_KV, rows, SWA_DH), blk4), cache_spec, cache_spec),
        compiler_params=_cparams(("arbitrary",)),
        name="swa_sample",
    )(q, z, k_new, v_new, cache_kt, cache_vt, bias, sink_rows)


def _mem_attend(probs, q_of, z_of, k_of, v_of, store):
    scale = MEM_DH ** -0.5
    logits = [_dot_nt(q_of(p) * scale, k_of(p)) for p in probs]
    ps, dens = [], []
    for l in logits:
        m = jnp.max(l, axis=-1, keepdims=True)
        e = jnp.exp(l - m)
        dens.append(jnp.sum(e, axis=-1, keepdims=True))
        ps.append(e)
    outs = [_dot(e, v_of(p)) / den for e, den, p in zip(ps, dens, probs)]
    for p, oh in zip(probs, outs):
        store(p, oh * _silu(z_of(p).astype(F32)))


def _mem_cols(h):
    return slice(h * MEM_DH, (h + 1) * MEM_DH)


def _mem_prompt_kernel(q_ref, z_ref, k_ref, v_ref, o_ref):
    def store(h, val):
        o_ref[:, _mem_cols(h)] = val.astype(o_ref.dtype)
    _mem_attend(range(MEM_H), lambda h: q_ref[:, _mem_cols(h)], lambda h: z_ref[:, _mem_cols(h)],
                lambda h: k_ref[:, _mem_cols(h)], lambda h: v_ref[:, _mem_cols(h)], store)


def _mem_prompt(proj, mkv, n_seq, seq_len, q_colblock, z_colblock, tq=1024):
    tq = math.gcd(tq, seq_len)
    steps = seq_len // tq
    return pl.pallas_call(
        _mem_prompt_kernel,
        out_shape=jax.ShapeDtypeStruct((n_seq * seq_len, MEM_W), BF16),
        grid=(n_seq, steps),
        in_specs=[pl.BlockSpec((tq, MEM_W), lambda b, n: (b * steps + n, q_colblock)),
                  pl.BlockSpec((tq, MEM_W), lambda b, n: (b * steps + n, z_colblock)),
                  pl.BlockSpec((N_MEM, MEM_W), lambda b, n: (b, 0)),
                  pl.BlockSpec((N_MEM, MEM_W), lambda b, n: (b, 1))],
        out_specs=pl.BlockSpec((tq, MEM_W), lambda b, n: (b * steps + n, 0)),
        compiler_params=_cparams(("parallel", "arbitrary")),
        name="mem_prompt",
    )(proj, proj, mkv, mkv)


def _mem_sample_kernel(q_ref, z_ref, k_hbm, v_hbm, o_ref, kbuf, vbuf, sem, *, n_seq_blk):
    i = pl.program_id(0)
    n_steps = pl.num_programs(0)
    slot = i % 2

    def copies(step, slot_):
        seqs = pl.ds(step * n_seq_blk, n_seq_blk)
        out = []
        for h in range(MEM_H):
            out.append(pltpu.make_async_copy(k_hbm.at[seqs, :, h, :], kbuf.at[slot_, h], sem.at[0, slot_, h]))
            out.append(pltpu.make_async_copy(v_hbm.at[seqs, :, h, :], vbuf.at[slot_, h], sem.at[1, slot_, h]))
        return out

    @pl.when(i == 0)
    def _():
        for cp in copies(0, 0):
            cp.start()

    @pl.when(i + 1 < n_steps)
    def _():
        for cp in copies(i + 1, 1 - slot):
            cp.start()

    for cp in copies(i, slot):
        cp.wait()

    def store(p, val):
        o_ref[p[0], :, _mem_cols(p[1])] = val.astype(o_ref.dtype)
    probs = [(s, h) for s in range(n_seq_blk) for h in range(MEM_H)]
    _mem_attend(probs, lambda p: q_ref[p[0], :, _mem_cols(p[1])], lambda p: z_ref[p[0], :, _mem_cols(p[1])],
                lambda p: kbuf[slot, p[1], p[0]], lambda p: vbuf[slot, p[1], p[0]], store)


def _mem_sample(q, z, cache_k, cache_v, n_seq_blk=4):
    bd, rows, _ = q.shape
    blk = lambda i: (i, 0, 0)
    buf = pltpu.VMEM((2, MEM_H, n_seq_blk, N_MEM, MEM_DH), cache_k.dtype)
    return pl.pallas_call(
        functools.partial(_mem_sample_kernel, n_seq_blk=n_seq_blk),
        out_shape=jax.ShapeDtypeStruct(q.shape, BF16),
        grid=(bd // n_seq_blk,),
        in_specs=[pl.BlockSpec((n_seq_blk, rows, MEM_W), blk),
                  pl.BlockSpec((n_seq_blk, rows, MEM_W), blk),
                  pl.BlockSpec(memory_space=pl.ANY),
                  pl.BlockSpec(memory_space=pl.ANY)],
        out_specs=pl.BlockSpec((n_seq_blk, rows, MEM_W), blk),
        scratch_shapes=[buf, buf, pltpu.SemaphoreType.DMA((2, 2, MEM_H))],
        compiler_params=_cparams(("arbitrary",)),
        name="mem_sample",
    )(q, z, cache_k, cache_v)


def _merge_kernel(og_ref, os_ref, om_ref, gate_ref, x_ref, wb_ref, wo_ref, nf_ref, y_ref):
    merged = None
    for b, o_ref in enumerate((og_ref, os_ref, om_ref)):
        t = jnp.dot(o_ref[...], wb_ref[b], preferred_element_type=F32)
        t = t * gate_ref[:, b * D_MODEL:(b + 1) * D_MODEL].astype(F32)
        merged = t if merged is None else merged + t
    h = x_ref[...] + jnp.dot(merged.astype(BF16), wo_ref[...], preferred_element_type=F32)
    ms = jnp.mean(h * h, axis=-1, keepdims=True)
    y_ref[...] = h * lax.rsqrt(ms + NORM_EPS) * nf_ref[...]


def _merge(o_gdn, o_swa, o_mem, gates, x, w_branch, w_out, norm_f, tm=256):
    m = x.shape[0]
    tm = min(tm, m)
    row = lambda i: (i, 0)
    const2 = lambda i: (0, 0)
    return pl.pallas_call(
        _merge_kernel,
        out_shape=jax.ShapeDtypeStruct((m, D_MODEL), F32),
        grid=(m // tm,),
        in_specs=[pl.BlockSpec((tm, BR_W), row),
                  pl.BlockSpec((tm, BR_W), row),
                  pl.BlockSpec((tm, BR_W), row),
                  pl.BlockSpec((tm, N_BRANCH * D_MODEL), row),
                  pl.BlockSpec((tm, D_MODEL), row),
                  pl.BlockSpec((N_BRANCH, BR_W, D_MODEL), lambda i: (0, 0, 0),
                               pipeline_mode=pl.Buffered(1)),
                  pl.BlockSpec((D_MODEL, D_MODEL), const2, pipeline_mode=pl.Buffered(1)),
                  pl.BlockSpec((1, D_MODEL), const2)],
        out_specs=pl.BlockSpec((tm, D_MODEL), row),
        compiler_params=_cparams(("parallel",)),
        name="merge",
    )(o_gdn, o_swa, o_mem, gates, x, w_branch, w_out, norm_f)


_IN_SIZES = (GDN_QK, GDN_QK, GDN_W, GDN_W, GDN_H, GDN_H, SWA_QW, SWA_KVW, SWA_KVW, SWA_QW,
             MEM_W, MEM_W, N_BRANCH * D_MODEL)
_IN_NAMES = ("gq", "gk", "gv", "gz", "gb", "ga", "sq", "sk", "sv", "sz", "mq", "mz", "mg")
_IN_SPAN = {name: (int(off), int(off + size)) for name, off, size in
            zip(_IN_NAMES, np.cumsum((0,) + _IN_SIZES[:-1]), _IN_SIZES)}


def kernel(x_prompt, x_sample, state_gdn, state_gdn_conv, cache_swa_k, cache_swa_v, cache_mem_k,
           cache_mem_v, mem_prompt, norm_in, w_in, gdn_conv_w, gdn_a_log, gdn_dt_bias, gdn_norm,
           swa_sinks, rel_bias, norm_mem, w_mem_kv, w_branch, w_out, norm_f):
    n_layers = norm_in.shape[0]
    assert n_layers == 1
    b, seq, _ = x_prompt.shape
    bd, ns, _ = x_sample.shape
    wb = cache_swa_k.shape[2]
    assert seq % WINDOW == 0 and seq % GDN_CHUNK == 0 and ns + CONV_W <= SUBLANE and wb == WINDOW
    lyr = 0

    w = w_in[lyr]
    main_names = ("gq", "gk", "gv", "gz", "sq", "sz", "mq", "mz")
    n_main = len(main_names) * W_BLK
    n_gate = N_BRANCH * D_MODEL
    n_small = 3 * LANE
    small_lead = -(n_main + n_gate) % n_small
    w_all = _wprep(w.T, [[(_IN_SPAN[a][0], W_BLK)] for a in main_names]
                   + [[(_IN_SPAN["mg"][0] + W_BLK * k, W_BLK)] for k in range(n_gate // W_BLK)]
                   + [[(None, small_lead), (_IN_SPAN["sk"][0], 2 * SWA_KVW), (_IN_SPAN["gb"][0], 2 * GDN_H)]])
    col_small = n_main + n_gate + small_lead
    cb_gz, cb_sq, cb_sz, cb_mq, cb_mz = 3, 4, 5, 6, 7
    cb_sk, cb_sv, cb_ba = 0, 1, 2
    w_mkv = w_mem_kv[lyr].astype(BF16)
    w_br = w_branch[lyr].astype(BF16)
    w_o = w_out[lyr].astype(BF16)
    nw_in = norm_in[lyr].reshape(1, D_MODEL)
    nw_mem = norm_mem[lyr].reshape(1, D_MODEL)
    nw_f = norm_f.reshape(1, D_MODEL)
    conv_w = gdn_conv_w[lyr]
    a_row = jnp.pad(gdn_a_log[lyr].reshape(1, GDN_H), ((0, 0), (GDN_H, LANE - 2 * GDN_H)))
    dt_row = jnp.pad(gdn_dt_bias[lyr].reshape(1, GDN_H), ((0, 0), (GDN_H, LANE - 2 * GDN_H)))
    gnw = gdn_norm[lyr].reshape(1, GDN_DV)
    sinks = swa_sinks[lyr]
    bias_p = _bias_prompt(rel_bias)
    npad = SUBLANE
    bias_s = _bias_sample(rel_bias, ns, wb, wb + npad)
    sink_rows = jnp.broadcast_to(jnp.repeat(sinks.reshape(SWA_KV, SWA_G), ns, axis=1)[:, :, None],
                                 (SWA_KV, SWA_G * ns, LANE))

    t = b * seq
    xp = x_prompt.reshape(t, D_MODEL)
    xn_p, p_small = _rmsnorm(xp, nw_in, w_all, col_small, n_small)
    p_main = _proj(xn_p, w_all, BF16, n=n_main)
    g_p = _proj(xn_p, w_all, BF16, act="sigmoid", col0=n_main, n=n_gate)
    xn_tail = xn_p.reshape(b, seq, D_MODEL)[:, seq - SUBLANE:, :].reshape(b * SUBLANE, D_MODEL)
    conv_p = _proj(xn_tail, w_all, F32, n=GDN_CONV_CH).reshape(b, SUBLANE, GDN_CONV_CH)[:, SUBLANE - (CONV_W - 1):]
    kv_tail = p_small.reshape(b, seq, 3 * LANE)[:, seq - WINDOW:]
    swk_p = kv_tail[:, :, cb_sk * LANE:(cb_sk + 1) * LANE].reshape(b, WINDOW, SWA_KV, SWA_DH)
    swv_p = kv_tail[:, :, cb_sv * LANE:(cb_sv + 1) * LANE].reshape(b, WINDOW, SWA_KV, SWA_DH)

    mkv = _proj(_rmsnorm(mem_prompt.reshape(b * N_MEM, D_MODEL), nw_mem), w_mkv, F32)
    mk_p = mkv[:, :MEM_W].reshape(b, N_MEM, MEM_W)
    mv_p = mkv[:, MEM_W:].reshape(b, N_MEM, MEM_W)

    o_gdn_p, s_p = _gdn(p_main, p_main, p_small, jnp.zeros((b, HDR, GDN_CONV_CH), F32),
                        jnp.zeros((b, GDN_H, GDN_DK, GDN_DV), F32), conv_w, a_row, dt_row, gnw,
                        n_seq=b, chunk=GDN_CHUNK, n_chunks=4, per_chunk_state=False,
                        valid_lo=0, valid_hi=GDN_CHUNK, out_dtype=BF16,
                        z_colblock=cb_gz, ba_colblock=cb_ba)
    o_swa_p = _swa_prompt(p_main, p_small, bias_p, sinks, b, seq, cb_sq, cb_sz, cb_sk, cb_sv)
    o_mem_p = _mem_prompt(p_main, mkv, b, seq, cb_mq, cb_mz)
    y_p = _merge(o_gdn_p, o_swa_p, o_mem_p, g_p, xp, w_br, w_o, nw_f).reshape(b, seq, D_MODEL)

    ts = bd * ns
    xs = x_sample.reshape(ts, D_MODEL)
    xn_s, s_small = _rmsnorm(xs, nw_in, w_all, col_small, n_small)
    s_main = _proj(xn_s, w_all, F32, n=n_main)
    g_s = _proj(xn_s, w_all, BF16, act="sigmoid", col0=n_main, n=n_gate)
    s_gdn = s_main[:, :GDN_CONV_CH + GDN_W]
    s_ba = s_small[:, cb_ba * LANE:(cb_ba + 1) * LANE]
    s_swa = jnp.concatenate([s_main[:, cb_sq * BR_W:(cb_sz + 1) * BR_W], s_small[:, :2 * LANE]], axis=1)
    s_mem = s_main[:, cb_mq * BR_W:(cb_mz + 1) * BR_W].astype(BF16)

    lo = CONV_W - 1
    hi = lo + ns
    pad_rows = ((0, 0), (lo, SUBLANE - hi), (0, 0))
    e_qkv = jnp.concatenate([state_gdn_conv[lyr], s_gdn[:, :GDN_CONV_CH].reshape(bd, ns, GDN_CONV_CH),
                             jnp.zeros((bd, SUBLANE - hi, GDN_CONV_CH), F32)], axis=1)
    e_z = jnp.pad(s_gdn[:, GDN_CONV_CH:].reshape(bd, ns, GDN_W), pad_rows)
    e_ba = jnp.pad(s_ba.reshape(bd, ns, LANE), pad_rows)
    seq_blk = 8
    o_gdn_s8, s_s = _gdn(e_qkv.reshape(bd * SUBLANE, GDN_CONV_CH), e_z.reshape(bd * SUBLANE, GDN_W),
                         e_ba.reshape(bd * SUBLANE, LANE), jnp.zeros((1, HDR, GDN_CONV_CH), F32),
                         state_gdn[lyr], conv_w, a_row, dt_row, gnw,
                         n_seq=bd, chunk=SUBLANE, n_chunks=seq_blk, per_chunk_state=True,
                         valid_lo=lo, valid_hi=hi, out_dtype=BF16)
    o_gdn_s = o_gdn_s8.reshape(bd, SUBLANE, GDN_W)[:, lo:hi].reshape(ts, GDN_W)
    conv_s = e_qkv[:, hi - (CONV_W - 1):hi]

    def to_heads(a):
        return a.reshape(bd, ns, SWA_KV, SWA_G, SWA_DH).transpose(0, 2, 3, 1, 4).reshape(
            bd, SWA_KV, SWA_G * ns, SWA_DH)

    k_new = s_swa[:, 2 * SWA_QW:2 * SWA_QW + SWA_KVW].reshape(bd, ns, SWA_KVW)
    v_new = s_swa[:, 2 * SWA_QW + SWA_KVW:].reshape(bd, ns, SWA_KVW)
    tok_pad = ((0, 0), (0, npad - ns), (0, 0))
    ck_t = cache_swa_k[lyr].transpose(0, 2, 3, 1)
    cv_t = cache_swa_v[lyr].transpose(0, 2, 3, 1)
    o_swa_h, nk_t, nv_t = _swa_sample(
        to_heads(s_swa[:, :SWA_QW]).astype(BF16), to_heads(s_swa[:, SWA_QW:2 * SWA_QW]),
        jnp.pad(k_new, tok_pad), jnp.pad(v_new, tok_pad), ck_t, cv_t, bias_s, sink_rows, n_new=ns)
    o_swa_s = o_swa_h.reshape(bd, SWA_KV, SWA_G, ns, SWA_DH).transpose(0, 3, 1, 2, 4).reshape(ts, SWA_QW)
    swk_s = nk_t.transpose(0, 3, 1, 2)
    swv_s = nv_t.transpose(0, 3, 1, 2)

    mq = jnp.pad(s_mem[:, :MEM_W].reshape(bd, ns, MEM_W), tok_pad)
    mz = jnp.pad(s_mem[:, MEM_W:].reshape(bd, ns, MEM_W), tok_pad)
    o_mem_s = _mem_sample(mq, mz, cache_mem_k[lyr], cache_mem_v[lyr])[:, :ns].reshape(ts, MEM_W)
    y_s = _merge(o_gdn_s, o_swa_s, o_mem_s, g_s, xs, w_br, w_o, nw_f).reshape(bd, ns, D_MODEL)

    return (y_p, y_s,
            s_p[None], conv_p[None], swk_p[None], swv_p[None],
            mk_p.reshape(b, N_MEM, MEM_H, MEM_DH)[None], mv_p.reshape(b, N_MEM, MEM_H, MEM_DH)[None],
            s_s[None], conv_s[None], swk_s[None], swv_s[None])
```

```python
import functools
import math

import numpy as np
import jax
import jax.numpy as jnp
from jax import lax
from jax.experimental import pallas as pl
from jax.experimental.pallas import tpu as pltpu

F32 = jnp.float32
BF16 = jnp.bfloat16

D_MODEL = 2048
N_BRANCH = 3
BR_W = 1024
GDN_H = 8
GDN_DK = 128
GDN_DV = 128
GDN_QK = GDN_H * GDN_DK
GDN_W = GDN_H * GDN_DV
GDN_CONV_CH = 2 * GDN_QK + GDN_W
CONV_W = 4
GDN_CHUNK = 64
SWA_H = 16
SWA_KV = 2
SWA_G = SWA_H // SWA_KV
SWA_DH = 64
SWA_QW = SWA_H * SWA_DH
SWA_KVW = SWA_KV * SWA_DH
WINDOW = 128
N_BUCKETS = 32
MAX_DISTANCE = 128
N_MEM = 256
MEM_H = 4
MEM_DH = 256
MEM_W = MEM_H * MEM_DH
NORM_EPS = 1e-6

LANE = 128
SUBLANE = 8
VMEM_LIMIT = 52 * 1024 * 1024


def _cparams(sem):
    return pltpu.CompilerParams(dimension_semantics=sem, vmem_limit_bytes=VMEM_LIMIT)


def _sigmoid(x):
    return 0.5 * jnp.tanh(0.5 * x) + 0.5


def _silu(x):
    h = 0.5 * x
    return h * jnp.tanh(h) + h


def _softplus(x):
    return jnp.maximum(x, 0.0) + jnp.log(1.0 + jnp.exp(-jnp.abs(x)))


def _dot(a, b):
    return jnp.dot(a.astype(BF16), b.astype(BF16), preferred_element_type=F32)


def _dot_nt(a, b):
    return lax.dot_general(a.astype(BF16), b.astype(BF16), (((1,), (1,)), ((), ())),
                           preferred_element_type=F32)


def _dot_tn(a, b):
    return lax.dot_general(a.astype(BF16), b.astype(BF16), (((0,), (0,)), ((), ())),
                           preferred_element_type=F32)


def _dot_f32(a, b):
    return jnp.dot(a, b, preferred_element_type=F32, precision=lax.Precision.HIGHEST)


def _rmsnorm_kernel(x_ref, nw_ref, *rest):
    x = x_ref[...].astype(F32)
    ms = jnp.mean(x * x, axis=-1, keepdims=True)
    xn = (x * lax.rsqrt(ms + NORM_EPS) * nw_ref[...]).astype(BF16)
    if len(rest) == 1:
        rest[0][...] = xn
    else:
        w_ref, o_ref, p_ref = rest
        o_ref[...] = xn
        p_ref[...] = jnp.dot(xn, w_ref[...], preferred_element_type=F32)


def _rmsnorm(x, norm_w, w=None, col0=0, n=None, tm_pref=1024):
    m, d = x.shape
    tm = min(m, tm_pref)
    assert m % tm == 0
    row = lambda i: (i, 0)
    in_specs = [pl.BlockSpec((tm, d), row), pl.BlockSpec((1, d), lambda i: (0, 0))]
    out_shape = jax.ShapeDtypeStruct((m, d), BF16)
    out_specs = pl.BlockSpec((tm, d), row)
    args = (x, norm_w)
    if w is not None:
        assert col0 % n == 0
        in_specs.append(pl.BlockSpec((d, n), lambda i: (0, col0 // n)))
        out_shape = (out_shape, jax.ShapeDtypeStruct((m, n), F32))
        out_specs = (out_specs, pl.BlockSpec((tm, n), row))
        args = args + (w,)
    return pl.pallas_call(
        _rmsnorm_kernel,
        out_shape=out_shape,
        grid=(m // tm,),
        in_specs=in_specs,
        out_specs=out_specs,
        compiler_params=_cparams(("parallel",)),
        name="rmsnorm",
    )(*args)


def _proj_kernel(x_ref, w_ref, o_ref, *, act):
    y = jnp.dot(x_ref[...], w_ref[...], preferred_element_type=F32)
    if act == "sigmoid":
        y = _sigmoid(y)
    o_ref[...] = y.astype(o_ref.dtype)


def _pick_tile(n, pref):
    t = min(n, pref)
    while n % t:
        t -= LANE
    return t


def _proj(xn, w, out_dtype, act=None, col0=0, n=None, tm_pref=1024, tn_pref=2048):
    m, d = xn.shape
    n = w.shape[1] if n is None else n
    tm = min(m, tm_pref)
    assert m % tm == 0
    tn = _pick_tile(n, tn_pref if out_dtype == BF16 else tn_pref // 2)
    while col0 % tn:
        tn = _pick_tile(n, tn - LANE)
    jb = col0 // tn
    return pl.pallas_call(
        functools.partial(_proj_kernel, act=act),
        out_shape=jax.ShapeDtypeStruct((m, n), out_dtype),
        grid=(m // tm, n // tn),
        in_specs=[pl.BlockSpec((tm, d), lambda i, j: (i, 0)),
                  pl.BlockSpec((d, tn), lambda i, j: (0, jb + j))],
        out_specs=pl.BlockSpec((tm, tn), lambda i, j: (i, j)),
        compiler_params=_cparams(("parallel", "arbitrary")),
        name="proj",
    )(xn, w)


W_BLK = 1024
W_PIECE_ROWS = (W_BLK, 2 * SUBLANE)


def _wprep_kernel(wt_hbm, o_ref, inbuf0, inbuf1, sem, *, blocks):
    j = pl.program_id(0)
    inbuf = (inbuf0, inbuf1)

    def sources(b):
        return [(start, rows) for start, rows in blocks[b] if start is not None]

    def pieces(b):
        return [pltpu.make_async_copy(wt_hbm.at[pl.ds(start, rows), :],
                                      inbuf[k].at[b % 2, pl.ds(0, rows), :], sem.at[b % 2, k])
                for k, (start, rows) in enumerate(sources(b))]

    for b, ranges in enumerate(blocks):
        @pl.when(j == b)
        def _(b=b, ranges=ranges):
            if b == 0:
                for cp in pieces(0):
                    cp.start()
            if b + 1 < len(blocks):
                for cp in pieces(b + 1):
                    cp.start()
            for cp in pieces(b):
                cp.wait()
            n_have = sum(rows for _, rows in ranges)
            parts, k = [], 0
            for start, rows in tuple(ranges) + ((None, W_BLK - n_have),):
                if start is None:
                    if rows:
                        parts.append(jnp.zeros((rows, inbuf0.shape[2]), inbuf0.dtype))
                else:
                    parts.append(inbuf[k][b % 2, 0:rows, :])
                    k += 1
            val = parts[0] if len(parts) == 1 else jnp.concatenate(parts, axis=0)
            o_ref[...] = val.T.astype(BF16)


def _wprep(wt, block_srcs):
    ncol, d = wt.shape
    for ranges in block_srcs:
        srcs = [(s, rows) for s, rows in ranges if s is not None]
        assert len(srcs) <= len(W_PIECE_ROWS) and all(rows % SUBLANE == 0 for _, rows in ranges)
        for (s, rows), cap in zip(srcs, W_PIECE_ROWS):
            assert s % SUBLANE == 0 and rows <= cap and s + rows <= ncol
    return pl.pallas_call(
        functools.partial(_wprep_kernel, blocks=tuple(tuple(r) for r in block_srcs)),
        out_shape=jax.ShapeDtypeStruct((d, W_BLK * len(block_srcs)), BF16),
        grid=(len(block_srcs),),
        in_specs=[pl.BlockSpec(memory_space=pl.ANY)],
        out_specs=pl.BlockSpec((d, W_BLK), lambda j: (0, j)),
        scratch_shapes=[pltpu.VMEM((2, cap, d), wt.dtype) for cap in W_PIECE_ROWS]
                       + [pltpu.SemaphoreType.DMA((2, len(W_PIECE_ROWS)))],
        compiler_params=_cparams(("arbitrary",)),
        name="wprep",
    )(wt)


HDR = SUBLANE


def _tri_inv_many(a_list, c):
    row = lax.broadcasted_iota(jnp.int32, (c, c), 0)
    col = lax.broadcasted_iota(jnp.int32, (c, c), 1)
    eye = jnp.where(row == col, 1.0, 0.0).astype(F32)
    xs = [eye - a for a in a_list]
    bs = [_dot(a, a) for a in a_list]
    n = 2
    while n < c:
        xs = [x + _dot(x, b) for x, b in zip(xs, bs)]
        n *= 2
        if n < c:
            bs = [_dot(b, b) for b in bs]
    return xs


def _gdn_kernel(qkv_ref, z_ref, ba_ref, hdr_ref, s0_ref, cw_ref, arow_ref, dtrow_ref, nw_ref,
                o_ref, s_ref, buf_ref, cv_ref, gb_ref, oacc_ref, sol_ref, att_ref, gl_ref,
                *, chunk, n_chunks, per_chunk_state, valid_lo, valid_hi):
    c = chunk
    tb = c * n_chunks

    if per_chunk_state:
        buf_ref[0:2 * HDR, :] = jnp.zeros((2 * HDR, GDN_CONV_CH), F32)
    else:
        @pl.when(pl.program_id(1) == 0)
        def _():
            buf_ref[0:HDR, :] = hdr_ref[0]
            buf_ref[HDR:2 * HDR, :] = jnp.zeros((HDR, GDN_CONV_CH), F32)
            s_ref[...] = s0_ref[...]

    def tap_from_history(j, n_rows):
        off = HDR - (CONV_W - 1) + j
        return buf_ref[off:off + n_rows, :] * cw_ref[j:j + 1, :]

    if qkv_ref.dtype == BF16:
        xb = qkv_ref[...]
        r = lax.broadcasted_iota(jnp.int32, (tb, tb), 0)
        cc = lax.broadcasted_iota(jnp.int32, (tb, tb), 1)
        acc = xb.astype(F32) * cw_ref[CONV_W - 1:CONV_W, :]
        for j in range(CONV_W - 1):
            shift = jnp.where(r - cc == CONV_W - 1 - j, 1.0, 0.0).astype(BF16)
            acc = acc + jnp.dot(shift, xb, preferred_element_type=F32) * cw_ref[j:j + 1, :]
        cv_ref[...] = _silu(acc)
        top = acc[0:HDR]
        for j in range(CONV_W - 1):
            top = top + tap_from_history(j, HDR)
        cv_ref[0:HDR, :] = _silu(top)
        buf_ref[0:HDR, :] = qkv_ref[tb - 2 * HDR:tb, :].astype(F32)[HDR:]
    else:
        buf_ref[HDR:HDR + tb, :] = qkv_ref[...].astype(F32)
        acc = None
        for j in range(CONV_W):
            term = tap_from_history(j, tb)
            acc = term if acc is None else acc + term
        cv_ref[...] = _silu(acc)
        if not per_chunk_state:
            buf_ref[0:HDR, :] = buf_ref[tb:tb + HDR, :]

    ba = ba_ref[...].astype(F32)
    beta_all = _sigmoid(ba)
    g_all = -jnp.exp(arow_ref[...]) * _softplus(ba + dtrow_ref[...])
    if per_chunk_state:
        r = lax.broadcasted_iota(jnp.int32, (tb, LANE), 0) & (c - 1)
        valid = (r >= valid_lo) & (r < valid_hi)
        beta_all = jnp.where(valid, beta_all, 0.0)
        g_all = jnp.where(valid, g_all, 0.0)
    gb_ref[0] = beta_all
    gb_ref[1] = g_all

    row = lax.broadcasted_iota(jnp.int32, (c, c), 0)
    col = lax.broadcasted_iota(jnp.int32, (c, c), 1)
    causal = row >= col
    strict = row > col
    tril = jnp.where(causal, 1.0, 0.0).astype(F32)
    scale_q = GDN_DK ** -0.5

    heads = range(GDN_H)
    qcol = lambda h: slice(h * GDN_DK, (h + 1) * GDN_DK)
    kcol = lambda h: slice(GDN_QK + h * GDN_DK, GDN_QK + (h + 1) * GDN_DK)
    vcol = lambda h: slice(2 * GDN_QK + h * GDN_DV, 2 * GDN_QK + (h + 1) * GDN_DV)
    ucol = lambda h: slice(h * (GDN_DV + GDN_DK), h * (GDN_DV + GDN_DK) + GDN_DV)
    wcol = lambda h: slice(h * (GDN_DV + GDN_DK) + GDN_DV, (h + 1) * (GDN_DV + GDN_DK))
    if per_chunk_state:
        rv = lax.broadcasted_iota(jnp.int32, (c, 1), 0)
        rvalid = (rv >= valid_lo) & (rv < valid_hi)

    def chunk_rows(ci):
        return slice(ci * c, (ci + 1) * c)

    def prep():
        probs = []
        for ci in range(n_chunks):
            rows = chunk_rows(ci)
            beta_c = gb_ref[0, rows, :]
            gc_all = _dot_f32(tril, gb_ref[1, rows, :])
            gc_t = gc_all.T
            gl_ref[ci] = gc_all[c - 1:c, :]
            for h in heads:
                q = cv_ref[rows, qcol(h)]
                k = cv_ref[rows, kcol(h)]
                v = cv_ref[rows, vcol(h)]
                q = q * lax.rsqrt(jnp.sum(q * q, axis=-1, keepdims=True) + NORM_EPS) * scale_q
                k = k * lax.rsqrt(jnp.sum(k * k, axis=-1, keepdims=True) + NORM_EPS)
                if per_chunk_state:
                    q = jnp.where(rvalid, q, 0.0)
                    k = jnp.where(rvalid, k, 0.0)
                    v = jnp.where(rvalid, v, 0.0)
                beta = beta_c[:, h:h + 1]
                gc_col = gc_all[:, GDN_H + h:GDN_H + h + 1]
                gc_row = gc_t[GDN_H + h:GDN_H + h + 1, :]
                diff = jnp.where(causal, gc_col - gc_row, 0.0)
                decay = jnp.where(causal, jnp.exp(diff), 0.0)
                e_gc = jnp.exp(gc_col)
                kb = k * beta
                rhs = jnp.concatenate([v * beta, kb * e_gc], axis=1)
                cv_ref[rows, qcol(h)] = q * e_gc
                cv_ref[rows, kcol(h)] = k * jnp.exp(gc_col[c - 1:c, :] - gc_col)
                probs.append((rows, h, q, k, kb, rhs, decay))
        kq = [_dot_nt(jnp.concatenate([kb, q], axis=0), k) for (_, _, q, k, kb, _, _) in probs]
        a_list = [jnp.where(strict, kq_i[:c] * p[6], 0.0) for kq_i, p in zip(kq, probs)]
        t_inv = _tri_inv_many(a_list, c)
        for t_i, kq_i, (rows, h, _, _, _, rhs, decay) in zip(t_inv, kq, probs):
            sol_ref[rows, h * (GDN_DV + GDN_DK):(h + 1) * (GDN_DV + GDN_DK)] = _dot(t_i, rhs)
            att_ref[h, rows, :] = kq_i[c:] * decay

    def scan(ci):
        rows = chunk_rows(ci)
        si = ci if per_chunk_state else 0
        g_tot = jnp.exp(gl_ref[ci])
        s_old = [s_ref[si, h] for h in heads]
        wq_s = [_dot(jnp.concatenate([sol_ref[rows, wcol(h)], cv_ref[rows, qcol(h)]], axis=0), s_old[h])
                for h in heads]
        v_new = [sol_ref[rows, ucol(h)] - wq_s[h][:c] for h in heads]
        o_att = [_dot(att_ref[h, rows, :], v_new[h]) for h in heads]
        for h in heads:
            s_ref[si, h] = (s_old[h] * g_tot[:, GDN_H + h:GDN_H + h + 1]
                            + _dot_tn(cv_ref[rows, kcol(h)], v_new[h]))
        for h in heads:
            o = wq_s[h][c:] + o_att[h]
            o = o * lax.rsqrt(jnp.mean(o * o, axis=-1, keepdims=True) + NORM_EPS) * nw_ref[...]
            oacc_ref[rows, h * GDN_DV:(h + 1) * GDN_DV] = o

    if per_chunk_state:
        s_ref[...] = s0_ref[...]
    prep()
    for ci in range(n_chunks):
        scan(ci)
    o_ref[...] = (oacc_ref[...] * _silu(z_ref[...].astype(F32))).astype(o_ref.dtype)


def _gdn(qkv, z, ba, hdr, s0, conv_w, a_row, dt_row, norm_w, *, n_seq, chunk, n_chunks,
         per_chunk_state, valid_lo, valid_hi, out_dtype, z_colblock=0, ba_colblock=0):
    rows = qkv.shape[0]
    tb = chunk * n_chunks
    if per_chunk_state:
        grid = (rows // tb,)
        sem = ("arbitrary",)
        rmap = lambda i: (i, 0)
        zmap = lambda i: (i, z_colblock)
        bmap = lambda i: (i, ba_colblock)
        hmap = lambda i: (0, 0, 0)
        smap = lambda i: (i, 0, 0, 0)
        cmap = lambda i: (0, 0)
        ns = n_chunks
    else:
        steps = rows // n_seq // tb
        grid = (n_seq, steps)
        sem = ("parallel", "arbitrary")
        rmap = lambda b, n: (b * steps + n, 0)
        zmap = lambda b, n: (b * steps + n, z_colblock)
        bmap = lambda b, n: (b * steps + n, ba_colblock)
        hmap = lambda b, n: (b, 0, 0)
        smap = lambda b, n: (b, 0, 0, 0)
        cmap = lambda b, n: (0, 0)
        ns = 1
    kern = functools.partial(_gdn_kernel, chunk=chunk, n_chunks=n_chunks,
                             per_chunk_state=per_chunk_state, valid_lo=valid_lo, valid_hi=valid_hi)
    return pl.pallas_call(
        kern,
        out_shape=(jax.ShapeDtypeStruct((rows, GDN_W), out_dtype),
                   jax.ShapeDtypeStruct(s0.shape, F32)),
        grid=grid,
        in_specs=[pl.BlockSpec((tb, GDN_CONV_CH), rmap),
                  pl.BlockSpec((tb, GDN_W), zmap),
                  pl.BlockSpec((tb, LANE), bmap),
                  pl.BlockSpec((1, HDR, GDN_CONV_CH), hmap),
                  pl.BlockSpec((ns, GDN_H, GDN_DK, GDN_DV), smap),
                  pl.BlockSpec((CONV_W, GDN_CONV_CH), cmap),
                  pl.BlockSpec((1, LANE), cmap),
                  pl.BlockSpec((1, LANE), cmap),
                  pl.BlockSpec((1, GDN_DV), cmap)],
        out_specs=(pl.BlockSpec((tb, GDN_W), rmap),
                   pl.BlockSpec((ns, GDN_H, GDN_DK, GDN_DV), smap)),
        scratch_shapes=[pltpu.VMEM((HDR + tb, GDN_CONV_CH), F32),
                        pltpu.VMEM((tb, GDN_CONV_CH), F32),
                        pltpu.VMEM((2, tb, LANE), F32),
                        pltpu.VMEM((tb, GDN_W), F32),
                        pltpu.VMEM((tb, GDN_H * (GDN_DV + GDN_DK)), F32),
                        pltpu.VMEM((GDN_H, tb, chunk), F32),
                        pltpu.VMEM((n_chunks, 1, LANE), F32)],
        compiler_params=_cparams(sem),
        name="gdn",
    )(qkv, z, ba, hdr, s0, conv_w, a_row, dt_row, norm_w)


def _t5_bucket_np(dist):
    n = np.maximum(dist, 0)
    max_exact = N_BUCKETS // 2
    nf = np.maximum(n, 1).astype(np.float32)
    large = max_exact + (np.log(nf / np.float32(max_exact)) / np.float32(math.log(MAX_DISTANCE / max_exact))
                         * np.float32(N_BUCKETS - max_exact)).astype(np.int32)
    large = np.minimum(large, N_BUCKETS - 1)
    return np.where(n < max_exact, n, large).astype(np.int32)


def _bias_prompt_kernel(code_ref, tab_ref, o_ref):
    code = code_ref[0]
    for h in range(SWA_H):
        acc = jnp.full(code.shape, -jnp.inf, F32)
        for b in range(N_BUCKETS):
            acc = jnp.where(code == b, tab_ref[b, h], acc)
        o_ref[0, h] = acc


def _bias_prompt(table):
    qi = np.arange(WINDOW)[None, :]
    sj = np.arange(WINDOW)[:, None]
    own = sj <= qi
    bucket = _t5_bucket_np(np.where(own, qi - sj, qi + WINDOW - sj))
    code_first = np.where(own, bucket, -1)
    code = jnp.asarray(np.stack([code_first, bucket]).astype(np.int32))
    return pl.pallas_call(
        _bias_prompt_kernel,
        out_shape=jax.ShapeDtypeStruct((2, SWA_H, WINDOW, WINDOW), F32),
        grid=(2,),
        in_specs=[pl.BlockSpec((1, WINDOW, WINDOW), lambda v: (v, 0, 0)),
                  pl.BlockSpec(memory_space=pltpu.SMEM)],
        out_specs=pl.BlockSpec((1, SWA_H, WINDOW, WINDOW), lambda v: (v, 0, 0, 0)),
        compiler_params=_cparams(("arbitrary",)),
        name="swa_bias_prompt",
    )(code, table)


def _bias_sample_kernel(code_ref, tab_ref, o_ref):
    kv = pl.program_id(0)
    code = code_ref[...]
    acc = jnp.full(code.shape, -jnp.inf, F32)
    for g in range(SWA_G):
        for b in range(N_BUCKETS):
            acc = jnp.where(code == b + N_BUCKETS * g, tab_ref[b, kv * SWA_G + g], acc)
    o_ref[0] = acc


def _bias_sample(table, n_tok, n_cache, n_keys_pad):
    dist = (n_cache + np.arange(n_tok))[:, None] - np.arange(n_keys_pad)[None, :]
    valid = (dist >= 0) & (dist < WINDOW) & (np.arange(n_keys_pad)[None, :] < n_cache + n_tok)
    bucket = _t5_bucket_np(dist)
    code_t = np.where(valid, bucket, -1)
    g = np.arange(SWA_G)[:, None, None]
    code = np.where(code_t[None] >= 0, code_t[None] + N_BUCKETS * g, -1)
    code = jnp.asarray(code.reshape(SWA_G * n_tok, n_keys_pad).astype(np.int32))
    return pl.pallas_call(
        _bias_sample_kernel,
        out_shape=jax.ShapeDtypeStruct((SWA_KV, SWA_G * n_tok, n_keys_pad), F32),
        grid=(SWA_KV,),
        in_specs=[pl.BlockSpec((SWA_G * n_tok, n_keys_pad), lambda k: (0, 0)),
                  pl.BlockSpec(memory_space=pltpu.SMEM)],
        out_specs=pl.BlockSpec((1, SWA_G * n_tok, n_keys_pad), lambda k: (k, 0, 0)),
        compiler_params=_cparams(("arbitrary",)),
        name="swa_bias_sample",
    )(code, table)


def _sink_softmax_pv(logits, sink, v):
    m = jnp.maximum(jnp.max(logits, axis=-1, keepdims=True), sink)
    p = jnp.exp(logits - m)
    den = jnp.sum(p, axis=-1, keepdims=True) + jnp.exp(sink - m)
    return _dot(p, v) / den


def _swa_prompt_kernel(q_ref, z_ref, kc_ref, kp_ref, vc_ref, vp_ref, bias_ref, sink_ref, o_ref, *, n_qblk):
    kall = jnp.concatenate([kp_ref[...], kc_ref[...]], axis=0)
    vall = jnp.concatenate([vp_ref[...], vc_ref[...]], axis=0)
    scale = SWA_DH ** -0.5
    key = lax.broadcasted_iota(jnp.int32, (WINDOW, WINDOW), 0)
    qry = lax.broadcasted_iota(jnp.int32, (WINDOW, WINDOW), 1)
    from_prev = key > qry
    first_variant = jnp.where(pl.program_id(1) == 0, 0, 1)
    cs = lambda h: slice(h * SWA_DH, (h + 1) * SWA_DH)
    rows = lambda qb: slice(qb * WINDOW, (qb + 1) * WINDOW)
    keys = lambda qb: slice(qb * WINDOW, (qb + 2) * WINDOW)
    for kv in range(SWA_KV):
        ks = slice(kv * SWA_DH, (kv + 1) * SWA_DH)
        k_kv = kall[:, ks].astype(BF16)
        v_t = vall[:, ks].astype(F32).T.astype(BF16)
        probs = [(qb, h) for qb in range(n_qblk) for h in range(kv * SWA_G, (kv + 1) * SWA_G)]
        lg = [_dot_nt(k_kv[keys(qb)], q_ref[rows(qb), cs(h)] * scale) for qb, h in probs]
        ps, dens = [], []
        for (qb, h), l in zip(probs, lg):
            bias = bias_ref[first_variant if qb == 0 else 1, h]
            l = jnp.where(from_prev, l[:WINDOW], l[WINDOW:]) + bias
            sink = sink_ref[h]
            m = jnp.maximum(jnp.max(l, axis=0, keepdims=True), sink)
            p = jnp.exp(l - m)
            dens.append(jnp.sum(p, axis=0, keepdims=True) + jnp.exp(sink - m))
            ps.append(jnp.concatenate([jnp.where(from_prev, p, 0.0), jnp.where(from_prev, 0.0, p)],
                                      axis=0))
        outs = [_dot(v_t[:, keys(qb)], p) / den for p, den, (qb, _) in zip(ps, dens, probs)]
        for j in range(0, len(probs), 2):
            qb, h = probs[j]
            two = slice(h * SWA_DH, (h + 2) * SWA_DH)
            o2 = jnp.concatenate([outs[j], outs[j + 1]], axis=0).T
            o_ref[rows(qb), two] = (o2 * _silu(z_ref[rows(qb), two].astype(F32))).astype(o_ref.dtype)


def _swa_prompt(proj, kv, bias, sinks, n_seq, seq_len, q_colblock, z_colblock, k_colblock, v_colblock,
                n_qblk=8):
    n_qblk = math.gcd(n_qblk, seq_len // WINDOW)
    tq = n_qblk * WINDOW
    steps = seq_len // tq
    cur = lambda b, n: b * steps + n
    prev = lambda b, n: (b * steps + n) * n_qblk - jnp.where(n == 0, 0, 1)
    return pl.pallas_call(
        functools.partial(_swa_prompt_kernel, n_qblk=n_qblk),
        out_shape=jax.ShapeDtypeStruct((n_seq * seq_len, SWA_QW), BF16),
        grid=(n_seq, steps),
        in_specs=[pl.BlockSpec((tq, SWA_QW), lambda b, n: (cur(b, n), q_colblock)),
                  pl.BlockSpec((tq, SWA_QW), lambda b, n: (cur(b, n), z_colblock)),
                  pl.BlockSpec((tq, SWA_KVW), lambda b, n: (cur(b, n), k_colblock)),
                  pl.BlockSpec((WINDOW, SWA_KVW), lambda b, n: (prev(b, n), k_colblock)),
                  pl.BlockSpec((tq, SWA_KVW), lambda b, n: (cur(b, n), v_colblock)),
                  pl.BlockSpec((WINDOW, SWA_KVW), lambda b, n: (prev(b, n), v_colblock)),
                  pl.BlockSpec((2, SWA_H, WINDOW, WINDOW), lambda b, n: (0, 0, 0, 0)),
                  pl.BlockSpec(memory_space=pltpu.SMEM)],
        out_specs=pl.BlockSpec((tq, SWA_QW), lambda b, n: (cur(b, n), 0)),
        compiler_params=_cparams(("parallel", "arbitrary")),
        name="swa_prompt",
    )(proj, proj, kv, kv, kv, kv, bias, sinks)


def _swa_sample_kernel(q_ref, z_ref, kn_ref, vn_ref, ck_ref, cv_ref, bias_ref, sink_ref, o_ref, nk_ref, nv_ref,
                       *, n_seq_blk, n_new):
    scale = SWA_DH ** -0.5
    wb = ck_ref.shape[3]
    probs = [(s, kv) for s in range(n_seq_blk) for kv in range(SWA_KV)]
    ks = lambda kv: slice(kv * SWA_DH, (kv + 1) * SWA_DH)
    slot = lax.broadcasted_iota(jnp.int32, (SWA_DH, wb), 1)
    tok = lax.broadcasted_iota(jnp.int32, (kn_ref.shape[1], wb), 0)
    tok_slot = lax.broadcasted_iota(jnp.int32, (kn_ref.shape[1], wb), 1)
    place = jnp.where((tok_slot == tok + (wb - n_new)) & (tok < n_new), 1.0, 0.0).astype(F32)
    for s in range(n_seq_blk):
        for cache_ref, new_ref, out_ref in ((ck_ref, kn_ref, nk_ref), (cv_ref, vn_ref, nv_ref)):
            new_t = lax.dot_general(new_ref[s], place, (((0,), (0,)), ((), ())),
                                    preferred_element_type=F32, precision=lax.Precision.HIGHEST)
            for kv in range(SWA_KV):
                out_ref[s, kv] = jnp.where(slot >= wb - n_new, new_t[ks(kv), :],
                                           pltpu.roll(cache_ref[s, kv], wb - n_new, axis=1))
    lc = [_dot(q_ref[s, kv] * scale, ck_ref[s, kv]) + bias_ref[kv, :, 0:wb] for s, kv in probs]
    ln = [_dot_nt(q_ref[s, kv] * scale, kn_ref[s, :, ks(kv)]) + bias_ref[kv, :, wb:] for s, kv in probs]
    pcs, pns, dens = [], [], []
    for (s, kv), c, n in zip(probs, lc, ln):
        sink = sink_ref[kv][:, 0:1]
        m = jnp.maximum(jnp.maximum(jnp.max(c, axis=-1, keepdims=True), jnp.max(n, axis=-1, keepdims=True)),
                        sink)
        pc = jnp.exp(c - m)
        pn = jnp.exp(n - m)
        pcs.append(pc)
        pns.append(pn)
        dens.append(jnp.sum(pc, axis=-1, keepdims=True) + jnp.sum(pn, axis=-1, keepdims=True)
                    + jnp.exp(sink - m))
    outs = [(_dot_nt(pc, cv_ref[s, kv]) + _dot(pn, vn_ref[s, :, ks(kv)])) / den
            for (s, kv), pc, pn, den in zip(probs, pcs, pns, dens)]
    for oh, (s, kv) in zip(outs, probs):
        o_ref[s, kv] = (oh * _silu(z_ref[s, kv].astype(F32))).astype(o_ref.dtype)


def _swa_sample(q, z, k_new, v_new, cache_kt, cache_vt, bias, sink_rows, n_new, n_seq_blk=8):
    bd, _, rows, _ = q.shape
    wb = cache_kt.shape[3]
    npad = k_new.shape[1]
    assert wb == LANE
    blk4 = lambda i: (i, 0, 0, 0)
    blk3 = lambda i: (i, 0, 0)
    cache_spec = pl.BlockSpec((n_seq_blk, SWA_KV, SWA_DH, wb), blk4)
    return pl.pallas_call(
        functools.partial(_swa_sample_kernel, n_seq_blk=n_seq_blk, n_new=n_new),
        out_shape=(jax.ShapeDtypeStruct(q.shape, BF16),
                   jax.ShapeDtypeStruct(cache_kt.shape, F32), jax.ShapeDtypeStruct(cache_vt.shape, F32)),
        grid=(bd // n_seq_blk,),
        in_specs=[pl.BlockSpec((n_seq_blk, SWA_KV, rows, SWA_DH), blk4),
                  pl.BlockSpec((n_seq_blk, SWA_KV, rows, SWA_DH), blk4),
                  pl.BlockSpec((n_seq_blk, npad, SWA_KVW), blk3),
                  pl.BlockSpec((n_seq_blk, npad, SWA_KVW), blk3),
                  pl.BlockSpec((n_seq_blk, SWA_KV, SWA_DH, wb), blk4),
                  pl.BlockSpec((n_seq_blk, SWA_KV, SWA_DH, wb), blk4),
                  pl.BlockSpec((SWA_KV, rows, wb + npad), lambda i: (0, 0, 0)),
                  pl.BlockSpec((SWA_KV, rows, LANE), lambda i: (0, 0, 0))],
        out_specs=(pl.BlockSpec((n_seq_blk, SWA_KV, rows, SWA_DH), blk4), cache_spec, cache_spec),
        compiler_params=_cparams(("arbitrary",)),
        name="swa_sample",
    )(q, z, k_new, v_new, cache_kt, cache_vt, bias, sink_rows)


def _mem_attend(probs, q_of, z_of, k_of, v_of, store):
    scale = MEM_DH ** -0.5
    logits = [_dot_nt(q_of(p) * scale, k_of(p)) for p in probs]
    ps, dens = [], []
    for l in logits:
        m = jnp.max(l, axis=-1, keepdims=True)
        e = jnp.exp(l - m)
        dens.append(jnp.sum(e, axis=-1, keepdims=True))
        ps.append(e)
    outs = [_dot(e, v_of(p)) / den for e, den, p in zip(ps, dens, probs)]
    for p, oh in zip(probs, outs):
        store(p, oh * _silu(z_of(p).astype(F32)))


def _mem_cols(h):
    return slice(h * MEM_DH, (h + 1) * MEM_DH)


def _mem_prompt_kernel(q_ref, z_ref, k_ref, v_ref, o_ref):
    def store(h, val):
        o_ref[:, _mem_cols(h)] = val.astype(o_ref.dtype)
    _mem_attend(range(MEM_H), lambda h: q_ref[:, _mem_cols(h)], lambda h: z_ref[:, _mem_cols(h)],
                lambda h: k_ref[:, _mem_cols(h)], lambda h: v_ref[:, _mem_cols(h)], store)


def _mem_prompt(proj, mkv, n_seq, seq_len, q_colblock, z_colblock, tq=2048):
    tq = math.gcd(tq, seq_len)
    steps = seq_len // tq
    return pl.pallas_call(
        _mem_prompt_kernel,
        out_shape=jax.ShapeDtypeStruct((n_seq * seq_len, MEM_W), BF16),
        grid=(n_seq, steps),
        in_specs=[pl.BlockSpec((tq, MEM_W), lambda b, n: (b * steps + n, q_colblock)),
                  pl.BlockSpec((tq, MEM_W), lambda b, n: (b * steps + n, z_colblock)),
                  pl.BlockSpec((N_MEM, MEM_W), lambda b, n: (b, 0)),
                  pl.BlockSpec((N_MEM, MEM_W), lambda b, n: (b, 1))],
        out_specs=pl.BlockSpec((tq, MEM_W), lambda b, n: (b * steps + n, 0)),
        compiler_params=_cparams(("parallel", "arbitrary")),
        name="mem_prompt",
    )(proj, proj, mkv, mkv)


def _mem_sample_kernel(q_ref, z_ref, k_hbm, v_hbm, o_ref, kbuf, vbuf, sem, *, n_seq_blk):
    i = pl.program_id(0)
    n_steps = pl.num_programs(0)
    slot = i % 2

    def copies(step, slot_):
        seqs = pl.ds(step * n_seq_blk, n_seq_blk)
        out = []
        for h in range(MEM_H):
            out.append(pltpu.make_async_copy(k_hbm.at[seqs, :, h, :], kbuf.at[slot_, h], sem.at[0, slot_, h]))
            out.append(pltpu.make_async_copy(v_hbm.at[seqs, :, h, :], vbuf.at[slot_, h], sem.at[1, slot_, h]))
        return out

    @pl.when(i == 0)
    def _():
        for cp in copies(0, 0):
            cp.start()

    @pl.when(i + 1 < n_steps)
    def _():
        for cp in copies(i + 1, 1 - slot):
            cp.start()

    for cp in copies(i, slot):
        cp.wait()

    def store(p, val):
        o_ref[p[0], :, _mem_cols(p[1])] = val.astype(o_ref.dtype)
    probs = [(s, h) for s in range(n_seq_blk) for h in range(MEM_H)]
    _mem_attend(probs, lambda p: q_ref[p[0], :, _mem_cols(p[1])], lambda p: z_ref[p[0], :, _mem_cols(p[1])],
                lambda p: kbuf[slot, p[1], p[0]], lambda p: vbuf[slot, p[1], p[0]], store)


def _mem_sample(q, z, cache_k, cache_v, n_seq_blk=8):
    bd, rows, _ = q.shape
    blk = lambda i: (i, 0, 0)
    buf = pltpu.VMEM((2, MEM_H, n_seq_blk, N_MEM, MEM_DH), cache_k.dtype)
    return pl.pallas_call(
        functools.partial(_mem_sample_kernel, n_seq_blk=n_seq_blk),
        out_shape=jax.ShapeDtypeStruct(q.shape, BF16),
        grid=(bd // n_seq_blk,),
        in_specs=[pl.BlockSpec((n_seq_blk, rows, MEM_W), blk),
                  pl.BlockSpec((n_seq_blk, rows, MEM_W), blk),
                  pl.BlockSpec(memory_space=pl.ANY),
                  pl.BlockSpec(memory_space=pl.ANY)],
        out_specs=pl.BlockSpec((n_seq_blk, rows, MEM_W), blk),
        scratch_shapes=[buf, buf, pltpu.SemaphoreType.DMA((2, 2, MEM_H))],
        compiler_params=_cparams(("arbitrary",)),
        name="mem_sample",
    )(q, z, cache_k, cache_v)


def _merge_kernel(og_ref, os_ref, om_ref, gate_ref, x_ref, wb_ref, wo_ref, nf_ref, y_ref):
    merged = None
    for b, o_ref in enumerate((og_ref, os_ref, om_ref)):
        t = jnp.dot(o_ref[...], wb_ref[b], preferred_element_type=F32)
        t = t * gate_ref[:, b * D_MODEL:(b + 1) * D_MODEL].astype(F32)
        merged = t if merged is None else merged + t
    h = x_ref[...] + jnp.dot(merged.astype(BF16), wo_ref[...], preferred_element_type=F32)
    ms = jnp.mean(h * h, axis=-1, keepdims=True)
    y_ref[...] = h * lax.rsqrt(ms + NORM_EPS) * nf_ref[...]


def _merge(o_gdn, o_swa, o_mem, gates, x, w_branch, w_out, norm_f, tm=256):
    m = x.shape[0]
    tm = min(tm, m)
    row = lambda i: (i, 0)
    const2 = lambda i: (0, 0)
    return pl.pallas_call(
        _merge_kernel,
        out_shape=jax.ShapeDtypeStruct((m, D_MODEL), F32),
        grid=(m // tm,),
        in_specs=[pl.BlockSpec((tm, BR_W), row),
                  pl.BlockSpec((tm, BR_W), row),
                  pl.BlockSpec((tm, BR_W), row),
                  pl.BlockSpec((tm, N_BRANCH * D_MODEL), row),
                  pl.BlockSpec((tm, D_MODEL), row),
                  pl.BlockSpec((N_BRANCH, BR_W, D_MODEL), lambda i: (0, 0, 0),
                               pipeline_mode=pl.Buffered(1)),
                  pl.BlockSpec((D_MODEL, D_MODEL), const2, pipeline_mode=pl.Buffered(1)),
                  pl.BlockSpec((1, D_MODEL), const2)],
        out_specs=pl.BlockSpec((tm, D_MODEL), row),
        compiler_params=_cparams(("parallel",)),
        name="merge",
    )(o_gdn, o_swa, o_mem, gates, x, w_branch, w_out, norm_f)


_IN_SIZES = (GDN_QK, GDN_QK, GDN_W, GDN_W, GDN_H, GDN_H, SWA_QW, SWA_KVW, SWA_KVW, SWA_QW,
             MEM_W, MEM_W, N_BRANCH * D_MODEL)
_IN_NAMES = ("gq", "gk", "gv", "gz", "gb", "ga", "sq", "sk", "sv", "sz", "mq", "mz", "mg")
_IN_SPAN = {name: (int(off), int(off + size)) for name, off, size in
            zip(_IN_NAMES, np.cumsum((0,) + _IN_SIZES[:-1]), _IN_SIZES)}


def kernel(x_prompt, x_sample, state_gdn, state_gdn_conv, cache_swa_k, cache_swa_v, cache_mem_k,
           cache_mem_v, mem_prompt, norm_in, w_in, gdn_conv_w, gdn_a_log, gdn_dt_bias, gdn_norm,
           swa_sinks, rel_bias, norm_mem, w_mem_kv, w_branch, w_out, norm_f):
    n_layers = norm_in.shape[0]
    assert n_layers == 1
    b, seq, _ = x_prompt.shape
    bd, ns, _ = x_sample.shape
    wb = cache_swa_k.shape[2]
    assert seq % WINDOW == 0 and seq % GDN_CHUNK == 0 and ns + CONV_W <= SUBLANE and wb == WINDOW
    lyr = 0

    w = w_in[lyr]
    main_names = ("gq", "gk", "gv", "gz", "sq", "sz", "mq", "mz")
    n_main = len(main_names) * W_BLK
    n_gate = N_BRANCH * D_MODEL
    n_small = 3 * LANE
    small_lead = -(n_main + n_gate) % n_small
    w_all = _wprep(w.T, [[(_IN_SPAN[a][0], W_BLK)] for a in main_names]
                   + [[(_IN_SPAN["mg"][0] + W_BLK * k, W_BLK)] for k in range(n_gate // W_BLK)]
                   + [[(None, small_lead), (_IN_SPAN["sk"][0], 2 * SWA_KVW), (_IN_SPAN["gb"][0], 2 * GDN_H)]])
    col_small = n_main + n_gate + small_lead
    cb_gz, cb_sq, cb_sz, cb_mq, cb_mz = 3, 4, 5, 6, 7
    cb_sk, cb_sv, cb_ba = 0, 1, 2
    w_mkv = w_mem_kv[lyr].astype(BF16)
    w_br = w_branch[lyr].astype(BF16)
    w_o = w_out[lyr].astype(BF16)
    nw_in = norm_in[lyr].reshape(1, D_MODEL)
    nw_mem = norm_mem[lyr].reshape(1, D_MODEL)
    nw_f = norm_f.reshape(1, D_MODEL)
    conv_w = gdn_conv_w[lyr]
    a_row = jnp.pad(gdn_a_log[lyr].reshape(1, GDN_H), ((0, 0), (GDN_H, LANE - 2 * GDN_H)))
    dt_row = jnp.pad(gdn_dt_bias[lyr].reshape(1, GDN_H), ((0, 0), (GDN_H, LANE - 2 * GDN_H)))
    gnw = gdn_norm[lyr].reshape(1, GDN_DV)
    sinks = swa_sinks[lyr]
    bias_p = _bias_prompt(rel_bias)
    npad = SUBLANE
    bias_s = _bias_sample(rel_bias, ns, wb, wb + npad)
    sink_rows = jnp.broadcast_to(jnp.repeat(sinks.reshape(SWA_KV, SWA_G), ns, axis=1)[:, :, None],
                                 (SWA_KV, SWA_G * ns, LANE))

    t = b * seq
    xp = x_prompt.reshape(t, D_MODEL)
    xn_p, p_small = _rmsnorm(xp, nw_in, w_all, col_small, n_small)
    p_main = _proj(xn_p, w_all, BF16, n=n_main)
    g_p = _proj(xn_p, w_all, BF16, act="sigmoid", col0=n_main, n=n_gate)
    xn_tail = xn_p.reshape(b, seq, D_MODEL)[:, seq - SUBLANE:, :].reshape(b * SUBLANE, D_MODEL)
    conv_p = _proj(xn_tail, w_all, F32, n=GDN_CONV_CH).reshape(b, SUBLANE, GDN_CONV_CH)[:, SUBLANE - (CONV_W - 1):]
    kv_tail = p_small.reshape(b, seq, 3 * LANE)[:, seq - WINDOW:]
    swk_p = kv_tail[:, :, cb_sk * LANE:(cb_sk + 1) * LANE].reshape(b, WINDOW, SWA_KV, SWA_DH)
    swv_p = kv_tail[:, :, cb_sv * LANE:(cb_sv + 1) * LANE].reshape(b, WINDOW, SWA_KV, SWA_DH)

    mkv = _proj(_rmsnorm(mem_prompt.reshape(b * N_MEM, D_MODEL), nw_mem), w_mkv, F32)
    mk_p = mkv[:, :MEM_W].reshape(b, N_MEM, MEM_W)
    mv_p = mkv[:, MEM_W:].reshape(b, N_MEM, MEM_W)

    o_gdn_p, s_p = _gdn(p_main, p_main, p_small, jnp.zeros((b, HDR, GDN_CONV_CH), F32),
                        jnp.zeros((b, GDN_H, GDN_DK, GDN_DV), F32), conv_w, a_row, dt_row, gnw,
                        n_seq=b, chunk=GDN_CHUNK, n_chunks=4, per_chunk_state=False,
                        valid_lo=0, valid_hi=GDN_CHUNK, out_dtype=BF16,
                        z_colblock=cb_gz, ba_colblock=cb_ba)
    o_swa_p = _swa_prompt(p_main, p_small, bias_p, sinks, b, seq, cb_sq, cb_sz, cb_sk, cb_sv)
    o_mem_p = _mem_prompt(p_main, mkv, b, seq, cb_mq, cb_mz)
    y_p = _merge(o_gdn_p, o_swa_p, o_mem_p, g_p, xp, w_br, w_o, nw_f).reshape(b, seq, D_MODEL)

    ts = bd * ns
    xs = x_sample.reshape(ts, D_MODEL)
    xn_s, s_small = _rmsnorm(xs, nw_in, w_all, col_small, n_small)
    s_main = _proj(xn_s, w_all, F32, n=n_main)
    g_s = _proj(xn_s, w_all, BF16, act="sigmoid", col0=n_main, n=n_gate)
    s_gdn = s_main[:, :GDN_CONV_CH + GDN_W]
    s_ba = s_small[:, cb_ba * LANE:(cb_ba + 1) * LANE]
    s_swa = jnp.concatenate([s_main[:, cb_sq * BR_W:(cb_sz + 1) * BR_W], s_small[:, :2 * LANE]], axis=1)
    s_mem = s_main[:, cb_mq * BR_W:(cb_mz + 1) * BR_W].astype(BF16)

    lo = CONV_W - 1
    hi = lo + ns
    pad_rows = ((0, 0), (lo, SUBLANE - hi), (0, 0))
    e_qkv = jnp.concatenate([state_gdn_conv[lyr], s_gdn[:, :GDN_CONV_CH].reshape(bd, ns, GDN_CONV_CH),
                             jnp.zeros((bd, SUBLANE - hi, GDN_CONV_CH), F32)], axis=1)
    e_z = jnp.pad(s_gdn[:, GDN_CONV_CH:].reshape(bd, ns, GDN_W), pad_rows)
    e_ba = jnp.pad(s_ba.reshape(bd, ns, LANE), pad_rows)
    seq_blk = 16
    o_gdn_s8, s_s = _gdn(e_qkv.reshape(bd * SUBLANE, GDN_CONV_CH), e_z.reshape(bd * SUBLANE, GDN_W),
                         e_ba.reshape(bd * SUBLANE, LANE), jnp.zeros((1, HDR, GDN_CONV_CH), F32),
                         state_gdn[lyr], conv_w, a_row, dt_row, gnw,
                         n_seq=bd, chunk=SUBLANE, n_chunks=seq_blk, per_chunk_state=True,
                         valid_lo=lo, valid_hi=hi, out_dtype=BF16)
    o_gdn_s = o_gdn_s8.reshape(bd, SUBLANE, GDN_W)[:, lo:hi].reshape(ts, GDN_W)
    conv_s = e_qkv[:, hi - (CONV_W - 1):hi]

    def to_heads(a):
        return a.reshape(bd, ns, SWA_KV, SWA_G, SWA_DH).transpose(0, 2, 3, 1, 4).reshape(
            bd, SWA_KV, SWA_G * ns, SWA_DH)

    k_new = s_swa[:, 2 * SWA_QW:2 * SWA_QW + SWA_KVW].reshape(bd, ns, SWA_KVW)
    v_new = s_swa[:, 2 * SWA_QW + SWA_KVW:].reshape(bd, ns, SWA_KVW)
    tok_pad = ((0, 0), (0, npad - ns), (0, 0))
    ck_t = cache_swa_k[lyr].transpose(0, 2, 3, 1)
    cv_t = cache_swa_v[lyr].transpose(0, 2, 3, 1)
    o_swa_h, nk_t, nv_t = _swa_sample(
        to_heads(s_swa[:, :SWA_QW]).astype(BF16), to_heads(s_swa[:, SWA_QW:2 * SWA_QW]),
        jnp.pad(k_new, tok_pad), jnp.pad(v_new, tok_pad), ck_t, cv_t, bias_s, sink_rows, n_new=ns)
    o_swa_s = o_swa_h.reshape(bd, SWA_KV, SWA_G, ns, SWA_DH).transpose(0, 3, 1, 2, 4).reshape(ts, SWA_QW)
    swk_s = nk_t.transpose(0, 3, 1, 2)
    swv_s = nv_t.transpose(0, 3, 1, 2)

    mq = jnp.pad(s_mem[:, :MEM_W].reshape(bd, ns, MEM_W), tok_pad)
    mz = jnp.pad(s_mem[:, MEM_W:].reshape(bd, ns, MEM_W), tok_pad)
    o_mem_s = _mem_sample(mq, mz, cache_mem_k[lyr], cache_mem_v[lyr])[:, :ns].reshape(ts, MEM_W)
    y_s = _merge(o_gdn_s, o_swa_s, o_mem_s, g_s, xs, w_br, w_o, nw_f).reshape(bd, ns, D_MODEL)

    return (y_p, y_s,
            s_p[None], conv_p[None], swk_p[None], swv_p[None],
            mk_p.reshape(b, N_MEM, MEM_H, MEM_DH)[None], mv_p.reshape(b, N_MEM, MEM_H, MEM_DH)[None],
            s_s[None], conv_s[None], swk_s[None], swv_s[None])
```

```python
import functools
import math

import numpy as np
import jax
import jax.numpy as jnp
from jax import lax
from jax.experimental import pallas as pl
from jax.experimental.pallas import tpu as pltpu

F32 = jnp.float32
BF16 = jnp.bfloat16

D_MODEL = 2048
N_BRANCH = 3
BR_W = 1024
GDN_H = 8
GDN_DK = 128
GDN_DV = 128
GDN_QK = GDN_H * GDN_DK
GDN_W = GDN_H * GDN_DV
GDN_CONV_CH = 2 * GDN_QK + GDN_W
CONV_W = 4
GDN_CHUNK = 64
SWA_H = 16
SWA_KV = 2
SWA_G = SWA_H // SWA_KV
SWA_DH = 64
SWA_QW = SWA_H * SWA_DH
SWA_KVW = SWA_KV * SWA_DH
WINDOW = 128
N_BUCKETS = 32
MAX_DISTANCE = 128
N_MEM = 256
MEM_H = 4
MEM_DH = 256
MEM_W = MEM_H * MEM_DH
NORM_EPS = 1e-6

LANE = 128
SUBLANE = 8
VMEM_LIMIT = 52 * 1024 * 1024


def _cparams(sem):
    return pltpu.CompilerParams(dimension_semantics=sem, vmem_limit_bytes=VMEM_LIMIT)


def _sigmoid(x):
    return 0.5 * jnp.tanh(0.5 * x) + 0.5


def _silu(x):
    h = 0.5 * x
    return h * jnp.tanh(h) + h


def _softplus(x):
    return jnp.maximum(x, 0.0) + jnp.log(1.0 + jnp.exp(-jnp.abs(x)))


def _dot(a, b):
    return jnp.dot(a.astype(BF16), b.astype(BF16), preferred_element_type=F32)


def _dot_nt(a, b):
    return lax.dot_general(a.astype(BF16), b.astype(BF16), (((1,), (1,)), ((), ())),
                           preferred_element_type=F32)


def _dot_tn(a, b):
    return lax.dot_general(a.astype(BF16), b.astype(BF16), (((0,), (0,)), ((), ())),
                           preferred_element_type=F32)


def _dot_f32(a, b):
    return jnp.dot(a, b, preferred_element_type=F32, precision=lax.Precision.HIGHEST)


def _rmsnorm_kernel(x_ref, nw_ref, *rest):
    x = x_ref[...].astype(F32)
    ms = jnp.mean(x * x, axis=-1, keepdims=True)
    xn = (x * lax.rsqrt(ms + NORM_EPS) * nw_ref[...]).astype(BF16)
    if len(rest) == 1:
        rest[0][...] = xn
    else:
        w_ref, o_ref, p_ref = rest
        o_ref[...] = xn
        p_ref[...] = jnp.dot(xn, w_ref[...], preferred_element_type=F32)


def _rmsnorm(x, norm_w, w=None, col0=0, n=None, tm_pref=1024):
    m, d = x.shape
    tm = min(m, tm_pref)
    assert m % tm == 0
    row = lambda i: (i, 0)
    in_specs = [pl.BlockSpec((tm, d), row), pl.BlockSpec((1, d), lambda i: (0, 0))]
    out_shape = jax.ShapeDtypeStruct((m, d), BF16)
    out_specs = pl.BlockSpec((tm, d), row)
    args = (x, norm_w)
    if w is not None:
        assert col0 % n == 0
        in_specs.append(pl.BlockSpec((d, n), lambda i: (0, col0 // n)))
        out_shape = (out_shape, jax.ShapeDtypeStruct((m, n), F32))
        out_specs = (out_specs, pl.BlockSpec((tm, n), row))
        args = args + (w,)
    return pl.pallas_call(
        _rmsnorm_kernel,
        out_shape=out_shape,
        grid=(m // tm,),
        in_specs=in_specs,
        out_specs=out_specs,
        compiler_params=_cparams(("parallel",)),
        name="rmsnorm",
    )(*args)


def _proj_kernel(x_ref, w_ref, o_ref, *, act):
    y = jnp.dot(x_ref[...], w_ref[...].astype(BF16), preferred_element_type=F32)
    if act == "sigmoid":
        y = _sigmoid(y)
    o_ref[...] = y.astype(o_ref.dtype)


def _pick_tile(n, pref):
    t = min(n, pref)
    while n % t:
        t -= LANE
    return t


def _proj(xn, w, out_dtype, act=None, col0=0, n=None, tm_pref=1024, tn_pref=2048):
    m, d = xn.shape
    n = w.shape[1] if n is None else n
    tm = min(m, tm_pref)
    assert m % tm == 0
    tn = _pick_tile(n, tn_pref if out_dtype == BF16 else tn_pref // 2)
    while col0 % tn:
        tn = _pick_tile(n, tn - LANE)
    jb = col0 // tn
    return pl.pallas_call(
        functools.partial(_proj_kernel, act=act),
        out_shape=jax.ShapeDtypeStruct((m, n), out_dtype),
        grid=(m // tm, n // tn),
        in_specs=[pl.BlockSpec((tm, d), lambda i, j: (i, 0)),
                  pl.BlockSpec((d, tn), lambda i, j: (0, jb + j))],
        out_specs=pl.BlockSpec((tm, tn), lambda i, j: (i, j)),
        compiler_params=_cparams(("parallel", "arbitrary")),
        name="proj",
    )(xn, w)


W_BLK = 1024
W_PIECE_ROWS = (W_BLK, 2 * SUBLANE)


def _wprep_kernel(wt_hbm, o_ref, inbuf0, inbuf1, sem, *, blocks):
    j = pl.program_id(0)
    inbuf = (inbuf0, inbuf1)

    def sources(b):
        return [(start, rows) for start, rows in blocks[b] if start is not None]

    def pieces(b):
        return [pltpu.make_async_copy(wt_hbm.at[pl.ds(start, rows), :],
                                      inbuf[k].at[b % 2, pl.ds(0, rows), :], sem.at[b % 2, k])
                for k, (start, rows) in enumerate(sources(b))]

    for b, ranges in enumerate(blocks):
        @pl.when(j == b)
        def _(b=b, ranges=ranges):
            if b == 0:
                for cp in pieces(0):
                    cp.start()
            if b + 1 < len(blocks):
                for cp in pieces(b + 1):
                    cp.start()
            for cp in pieces(b):
                cp.wait()
            n_have = sum(rows for _, rows in ranges)
            parts, k = [], 0
            for start, rows in tuple(ranges) + ((None, W_BLK - n_have),):
                if start is None:
                    if rows:
                        parts.append(jnp.zeros((rows, inbuf0.shape[2]), inbuf0.dtype))
                else:
                    parts.append(inbuf[k][b % 2, 0:rows, :])
                    k += 1
            val = parts[0] if len(parts) == 1 else jnp.concatenate(parts, axis=0)
            o_ref[...] = val.T.astype(BF16)


def _wprep(wt, block_srcs):
    ncol, d = wt.shape
    for ranges in block_srcs:
        srcs = [(s, rows) for s, rows in ranges if s is not None]
        assert len(srcs) <= len(W_PIECE_ROWS) and all(rows % SUBLANE == 0 for _, rows in ranges)
        for (s, rows), cap in zip(srcs, W_PIECE_ROWS):
            assert s % SUBLANE == 0 and rows <= cap and s + rows <= ncol
    return pl.pallas_call(
        functools.partial(_wprep_kernel, blocks=tuple(tuple(r) for r in block_srcs)),
        out_shape=jax.ShapeDtypeStruct((d, W_BLK * len(block_srcs)), BF16),
        grid=(len(block_srcs),),
        in_specs=[pl.BlockSpec(memory_space=pl.ANY)],
        out_specs=pl.BlockSpec((d, W_BLK), lambda j: (0, j)),
        scratch_shapes=[pltpu.VMEM((2, cap, d), wt.dtype) for cap in W_PIECE_ROWS]
                       + [pltpu.SemaphoreType.DMA((2, len(W_PIECE_ROWS)))],
        compiler_params=_cparams(("arbitrary",)),
        name="wprep",
    )(wt)


HDR = SUBLANE


def _tri_inv_many(a_list, c):
    row = lax.broadcasted_iota(jnp.int32, (c, c), 0)
    col = lax.broadcasted_iota(jnp.int32, (c, c), 1)
    eye = jnp.where(row == col, 1.0, 0.0).astype(F32)
    xs = [eye - a for a in a_list]
    bs = [_dot(a, a) for a in a_list]
    n = 2
    while n < c:
        xs = [x + _dot(x, b) for x, b in zip(xs, bs)]
        n *= 2
        if n < c:
            bs = [_dot(b, b) for b in bs]
    return xs


def _gdn_kernel(qkv_ref, z_ref, ba_ref, hdr_ref, s0_ref, cw_ref, arow_ref, dtrow_ref, nw_ref,
                o_ref, s_ref, buf_ref, cv_ref, gb_ref, oacc_ref, sol_ref, att_ref, gl_ref,
                *, chunk, n_chunks, per_chunk_state, valid_lo, valid_hi):
    c = chunk
    tb = c * n_chunks

    if per_chunk_state:
        buf_ref[0:2 * HDR, :] = jnp.zeros((2 * HDR, GDN_CONV_CH), F32)
    else:
        @pl.when(pl.program_id(1) == 0)
        def _():
            buf_ref[0:HDR, :] = hdr_ref[0]
            buf_ref[HDR:2 * HDR, :] = jnp.zeros((HDR, GDN_CONV_CH), F32)
            s_ref[...] = s0_ref[...]

    def tap_from_history(j, n_rows):
        off = HDR - (CONV_W - 1) + j
        return buf_ref[off:off + n_rows, :] * cw_ref[j:j + 1, :]

    if qkv_ref.dtype == BF16:
        xb = qkv_ref[...]
        r = lax.broadcasted_iota(jnp.int32, (tb, tb), 0)
        cc = lax.broadcasted_iota(jnp.int32, (tb, tb), 1)
        acc = xb.astype(F32) * cw_ref[CONV_W - 1:CONV_W, :]
        for j in range(CONV_W - 1):
            shift = jnp.where(r - cc == CONV_W - 1 - j, 1.0, 0.0).astype(BF16)
            acc = acc + jnp.dot(shift, xb, preferred_element_type=F32) * cw_ref[j:j + 1, :]
        cv_ref[...] = _silu(acc)
        top = acc[0:HDR]
        for j in range(CONV_W - 1):
            top = top + tap_from_history(j, HDR)
        cv_ref[0:HDR, :] = _silu(top)
        buf_ref[0:HDR, :] = qkv_ref[tb - 2 * HDR:tb, :].astype(F32)[HDR:]
    else:
        buf_ref[HDR:HDR + tb, :] = qkv_ref[...].astype(F32)
        acc = None
        for j in range(CONV_W):
            term = tap_from_history(j, tb)
            acc = term if acc is None else acc + term
        cv_ref[...] = _silu(acc)
        if not per_chunk_state:
            buf_ref[0:HDR, :] = buf_ref[tb:tb + HDR, :]

    ba = ba_ref[...].astype(F32)
    beta_all = _sigmoid(ba)
    g_all = -jnp.exp(arow_ref[...]) * _softplus(ba + dtrow_ref[...])
    if per_chunk_state:
        r = lax.broadcasted_iota(jnp.int32, (tb, LANE), 0) & (c - 1)
        valid = (r >= valid_lo) & (r < valid_hi)
        beta_all = jnp.where(valid, beta_all, 0.0)
        g_all = jnp.where(valid, g_all, 0.0)
    gb_ref[0] = beta_all
    gb_ref[1] = g_all

    row = lax.broadcasted_iota(jnp.int32, (c, c), 0)
    col = lax.broadcasted_iota(jnp.int32, (c, c), 1)
    causal = row >= col
    strict = row > col
    tril = jnp.where(causal, 1.0, 0.0).astype(F32)
    scale_q = GDN_DK ** -0.5

    heads = range(GDN_H)
    qcol = lambda h: slice(h * GDN_DK, (h + 1) * GDN_DK)
    kcol = lambda h: slice(GDN_QK + h * GDN_DK, GDN_QK + (h + 1) * GDN_DK)
    vcol = lambda h: slice(2 * GDN_QK + h * GDN_DV, 2 * GDN_QK + (h + 1) * GDN_DV)
    ucol = lambda h: slice(h * (GDN_DV + GDN_DK), h * (GDN_DV + GDN_DK) + GDN_DV)
    wcol = lambda h: slice(h * (GDN_DV + GDN_DK) + GDN_DV, (h + 1) * (GDN_DV + GDN_DK))
    if per_chunk_state:
        rv = lax.broadcasted_iota(jnp.int32, (c, 1), 0)
        rvalid = (rv >= valid_lo) & (rv < valid_hi)

    def chunk_rows(ci):
        return slice(ci * c, (ci + 1) * c)

    def prep():
        probs = []
        for ci in range(n_chunks):
            rows = chunk_rows(ci)
            beta_c = gb_ref[0, rows, :]
            gc_all = _dot_f32(tril, gb_ref[1, rows, :])
            gc_t = gc_all.T
            gl_ref[ci] = gc_all[c - 1:c, :]
            for h in heads:
                q = cv_ref[rows, qcol(h)]
                k = cv_ref[rows, kcol(h)]
                v = cv_ref[rows, vcol(h)]
                q = q * lax.rsqrt(jnp.sum(q * q, axis=-1, keepdims=True) + NORM_EPS) * scale_q
                k = k * lax.rsqrt(jnp.sum(k * k, axis=-1, keepdims=True) + NORM_EPS)
                if per_chunk_state:
                    q = jnp.where(rvalid, q, 0.0)
                    k = jnp.where(rvalid, k, 0.0)
                    v = jnp.where(rvalid, v, 0.0)
                beta = beta_c[:, h:h + 1]
                gc_col = gc_all[:, GDN_H + h:GDN_H + h + 1]
                gc_row = gc_t[GDN_H + h:GDN_H + h + 1, :]
                diff = jnp.where(causal, gc_col - gc_row, 0.0)
                decay = jnp.where(causal, jnp.exp(diff), 0.0)
                e_gc = jnp.exp(gc_col)
                kb = k * beta
                rhs = jnp.concatenate([v * beta, kb * e_gc], axis=1)
                cv_ref[rows, qcol(h)] = q * e_gc
                cv_ref[rows, kcol(h)] = k * jnp.exp(gc_col[c - 1:c, :] - gc_col)
                probs.append((rows, h, q, k, kb, rhs, decay))
        kq = [_dot_nt(jnp.concatenate([kb, q], axis=0), k) for (_, _, q, k, kb, _, _) in probs]
        a_list = [jnp.where(strict, kq_i[:c] * p[6], 0.0) for kq_i, p in zip(kq, probs)]
        t_inv = _tri_inv_many(a_list, c)
        for t_i, kq_i, (rows, h, _, _, _, rhs, decay) in zip(t_inv, kq, probs):
            sol_ref[rows, h * (GDN_DV + GDN_DK):(h + 1) * (GDN_DV + GDN_DK)] = _dot(t_i, rhs)
            att_ref[h, rows, :] = kq_i[c:] * decay

    def scan(chunks):
        probs = [(ci, h) for ci in chunks for h in heads]
        rows = chunk_rows
        si = lambda ci: ci if per_chunk_state else 0
        g_tot = {ci: jnp.exp(gl_ref[ci]) for ci in chunks}
        s_old = [s_ref[si(ci), h] for ci, h in probs]
        wq_s = [_dot(jnp.concatenate([sol_ref[rows(ci), wcol(h)], cv_ref[rows(ci), qcol(h)]], axis=0), s)
                for (ci, h), s in zip(probs, s_old)]
        v_new = [sol_ref[rows(ci), ucol(h)] - w[:c] for (ci, h), w in zip(probs, wq_s)]
        o_att = [_dot(att_ref[h, rows(ci), :], v) for (ci, h), v in zip(probs, v_new)]
        for (ci, h), s, v in zip(probs, s_old, v_new):
            s_ref[si(ci), h] = (s * g_tot[ci][:, GDN_H + h:GDN_H + h + 1]
                                + _dot_tn(cv_ref[rows(ci), kcol(h)], v))
        for (ci, h), w, oa in zip(probs, wq_s, o_att):
            o = w[c:] + oa
            o = o * lax.rsqrt(jnp.mean(o * o, axis=-1, keepdims=True) + NORM_EPS) * nw_ref[...]
            oacc_ref[rows(ci), h * GDN_DV:(h + 1) * GDN_DV] = o

    prep()
    if per_chunk_state:
        s_ref[...] = s0_ref[...]
        scan(range(n_chunks))
    else:
        for ci in range(n_chunks):
            scan([ci])
    o_ref[...] = (oacc_ref[...] * _silu(z_ref[...].astype(F32))).astype(o_ref.dtype)


def _gdn(qkv, z, ba, hdr, s0, conv_w, a_row, dt_row, norm_w, *, n_seq, chunk, n_chunks,
         per_chunk_state, valid_lo, valid_hi, out_dtype, z_colblock=0, ba_colblock=0):
    rows = qkv.shape[0]
    tb = chunk * n_chunks
    if per_chunk_state:
        grid = (rows // tb,)
        sem = ("arbitrary",)
        rmap = lambda i: (i, 0)
        zmap = lambda i: (i, z_colblock)
        bmap = lambda i: (i, ba_colblock)
        hmap = lambda i: (0, 0, 0)
        smap = lambda i: (i, 0, 0, 0)
        cmap = lambda i: (0, 0)
        ns = n_chunks
    else:
        steps = rows // n_seq // tb
        grid = (n_seq, steps)
        sem = ("parallel", "arbitrary")
        rmap = lambda b, n: (b * steps + n, 0)
        zmap = lambda b, n: (b * steps + n, z_colblock)
        bmap = lambda b, n: (b * steps + n, ba_colblock)
        hmap = lambda b, n: (b, 0, 0)
        smap = lambda b, n: (b, 0, 0, 0)
        cmap = lambda b, n: (0, 0)
        ns = 1
    kern = functools.partial(_gdn_kernel, chunk=chunk, n_chunks=n_chunks,
                             per_chunk_state=per_chunk_state, valid_lo=valid_lo, valid_hi=valid_hi)
    return pl.pallas_call(
        kern,
        out_shape=(jax.ShapeDtypeStruct((rows, GDN_W), out_dtype),
                   jax.ShapeDtypeStruct(s0.shape, F32)),
        grid=grid,
        in_specs=[pl.BlockSpec((tb, GDN_CONV_CH), rmap),
                  pl.BlockSpec((tb, GDN_W), zmap),
                  pl.BlockSpec((tb, LANE), bmap),
                  pl.BlockSpec((1, HDR, GDN_CONV_CH), hmap),
                  pl.BlockSpec((ns, GDN_H, GDN_DK, GDN_DV), smap),
                  pl.BlockSpec((CONV_W, GDN_CONV_CH), cmap),
                  pl.BlockSpec((1, LANE), cmap),
                  pl.BlockSpec((1, LANE), cmap),
                  pl.BlockSpec((1, GDN_DV), cmap)],
        out_specs=(pl.BlockSpec((tb, GDN_W), rmap),
                   pl.BlockSpec((ns, GDN_H, GDN_DK, GDN_DV), smap)),
        scratch_shapes=[pltpu.VMEM((HDR + tb, GDN_CONV_CH), F32),
                        pltpu.VMEM((tb, GDN_CONV_CH), F32),
                        pltpu.VMEM((2, tb, LANE), F32),
                        pltpu.VMEM((tb, GDN_W), F32),
                        pltpu.VMEM((tb, GDN_H * (GDN_DV + GDN_DK)), F32),
                        pltpu.VMEM((GDN_H, tb, chunk), F32),
                        pltpu.VMEM((n_chunks, 1, LANE), F32)],
        compiler_params=_cparams(sem),
        name="gdn",
    )(qkv, z, ba, hdr, s0, conv_w, a_row, dt_row, norm_w)


def _t5_bucket_np(dist):
    n = np.maximum(dist, 0)
    max_exact = N_BUCKETS // 2
    nf = np.maximum(n, 1).astype(np.float32)
    large = max_exact + (np.log(nf / np.float32(max_exact)) / np.float32(math.log(MAX_DISTANCE / max_exact))
                         * np.float32(N_BUCKETS - max_exact)).astype(np.int32)
    large = np.minimum(large, N_BUCKETS - 1)
    return np.where(n < max_exact, n, large).astype(np.int32)


def _bias_prompt_kernel(code_ref, tab_ref, o_ref):
    code = code_ref[0]
    for h in range(SWA_H):
        acc = jnp.full(code.shape, -jnp.inf, F32)
        for b in range(N_BUCKETS):
            acc = jnp.where(code == b, tab_ref[b, h], acc)
        o_ref[0, h] = acc


def _bias_prompt(table):
    qi = np.arange(WINDOW)[None, :]
    sj = np.arange(WINDOW)[:, None]
    own = sj <= qi
    bucket = _t5_bucket_np(np.where(own, qi - sj, qi + WINDOW - sj))
    code_first = np.where(own, bucket, -1)
    code = jnp.asarray(np.stack([code_first, bucket]).astype(np.int32))
    return pl.pallas_call(
        _bias_prompt_kernel,
        out_shape=jax.ShapeDtypeStruct((2, SWA_H, WINDOW, WINDOW), F32),
        grid=(2,),
        in_specs=[pl.BlockSpec((1, WINDOW, WINDOW), lambda v: (v, 0, 0)),
                  pl.BlockSpec(memory_space=pltpu.SMEM)],
        out_specs=pl.BlockSpec((1, SWA_H, WINDOW, WINDOW), lambda v: (v, 0, 0, 0)),
        compiler_params=_cparams(("arbitrary",)),
        name="swa_bias_prompt",
    )(code, table)


def _bias_sample_kernel(code_ref, tab_ref, o_ref):
    kv = pl.program_id(0)
    code = code_ref[...]
    acc = jnp.full(code.shape, -jnp.inf, F32)
    for g in range(SWA_G):
        for b in range(N_BUCKETS):
            acc = jnp.where(code == b + N_BUCKETS * g, tab_ref[b, kv * SWA_G + g], acc)
    o_ref[0] = acc


def _bias_sample(table, n_tok, n_cache, n_keys_pad):
    dist = (n_cache + np.arange(n_tok))[:, None] - np.arange(n_keys_pad)[None, :]
    valid = (dist >= 0) & (dist < WINDOW) & (np.arange(n_keys_pad)[None, :] < n_cache + n_tok)
    bucket = _t5_bucket_np(dist)
    code_t = np.where(valid, bucket, -1)
    g = np.arange(SWA_G)[:, None, None]
    code = np.where(code_t[None] >= 0, code_t[None] + N_BUCKETS * g, -1)
    code = jnp.asarray(code.reshape(SWA_G * n_tok, n_keys_pad).astype(np.int32))
    return pl.pallas_call(
        _bias_sample_kernel,
        out_shape=jax.ShapeDtypeStruct((SWA_KV, SWA_G * n_tok, n_keys_pad), F32),
        grid=(SWA_KV,),
        in_specs=[pl.BlockSpec((SWA_G * n_tok, n_keys_pad), lambda k: (0, 0)),
                  pl.BlockSpec(memory_space=pltpu.SMEM)],
        out_specs=pl.BlockSpec((1, SWA_G * n_tok, n_keys_pad), lambda k: (k, 0, 0)),
        compiler_params=_cparams(("arbitrary",)),
        name="swa_bias_sample",
    )(code, table)


def _sink_softmax_pv(logits, sink, v):
    m = jnp.maximum(jnp.max(logits, axis=-1, keepdims=True), sink)
    p = jnp.exp(logits - m)
    den = jnp.sum(p, axis=-1, keepdims=True) + jnp.exp(sink - m)
    return _dot(p, v) / den


def _swa_prompt_kernel(q_ref, z_ref, kc_ref, kp_ref, vc_ref, vp_ref, bias_ref, sink_ref, o_ref, *, n_qblk):
    kall = jnp.concatenate([kp_ref[...], kc_ref[...]], axis=0)
    vall = jnp.concatenate([vp_ref[...], vc_ref[...]], axis=0)
    scale = SWA_DH ** -0.5
    key = lax.broadcasted_iota(jnp.int32, (WINDOW, WINDOW), 0)
    qry = lax.broadcasted_iota(jnp.int32, (WINDOW, WINDOW), 1)
    from_prev = key > qry
    first_variant = jnp.where(pl.program_id(1) == 0, 0, 1)
    cs = lambda h: slice(h * SWA_DH, (h + 1) * SWA_DH)
    rows = lambda qb: slice(qb * WINDOW, (qb + 1) * WINDOW)
    keys = lambda qb: slice(qb * WINDOW, (qb + 2) * WINDOW)
    for kv in range(SWA_KV):
        ks = slice(kv * SWA_DH, (kv + 1) * SWA_DH)
        k_kv = kall[:, ks].astype(BF16)
        v_t = vall[:, ks].astype(F32).T.astype(BF16)
        probs = [(qb, h) for qb in range(n_qblk) for h in range(kv * SWA_G, (kv + 1) * SWA_G)]
        lg = [_dot_nt(k_kv[keys(qb)], q_ref[rows(qb), cs(h)] * scale) for qb, h in probs]
        ps, dens = [], []
        for (qb, h), l in zip(probs, lg):
            bias = bias_ref[first_variant if qb == 0 else 1, h]
            l = jnp.where(from_prev, l[:WINDOW], l[WINDOW:]) + bias
            sink = sink_ref[h]
            m = jnp.maximum(jnp.max(l, axis=0, keepdims=True), sink)
            p = jnp.exp(l - m)
            dens.append(jnp.sum(p, axis=0, keepdims=True) + jnp.exp(sink - m))
            ps.append(jnp.concatenate([jnp.where(from_prev, p, 0.0), jnp.where(from_prev, 0.0, p)],
                                      axis=0))
        outs = [_dot(v_t[:, keys(qb)], p) / den for p, den, (qb, _) in zip(ps, dens, probs)]
        for j in range(0, len(probs), 2):
            qb, h = probs[j]
            two = slice(h * SWA_DH, (h + 2) * SWA_DH)
            o2 = jnp.concatenate([outs[j], outs[j + 1]], axis=0).T
            o_ref[rows(qb), two] = (o2 * _silu(z_ref[rows(qb), two].astype(F32))).astype(o_ref.dtype)


def _swa_prompt(proj, kv, bias, sinks, n_seq, seq_len, q_colblock, z_colblock, k_colblock, v_colblock,
                n_qblk=8):
    n_qblk = math.gcd(n_qblk, seq_len // WINDOW)
    tq = n_qblk * WINDOW
    steps = seq_len // tq
    cur = lambda b, n: b * steps + n
    prev = lambda b, n: (b * steps + n) * n_qblk - jnp.where(n == 0, 0, 1)
    return pl.pallas_call(
        functools.partial(_swa_prompt_kernel, n_qblk=n_qblk),
        out_shape=jax.ShapeDtypeStruct((n_seq * seq_len, SWA_QW), BF16),
        grid=(n_seq, steps),
        in_specs=[pl.BlockSpec((tq, SWA_QW), lambda b, n: (cur(b, n), q_colblock)),
                  pl.BlockSpec((tq, SWA_QW), lambda b, n: (cur(b, n), z_colblock)),
                  pl.BlockSpec((tq, SWA_KVW), lambda b, n: (cur(b, n), k_colblock)),
                  pl.BlockSpec((WINDOW, SWA_KVW), lambda b, n: (prev(b, n), k_colblock)),
                  pl.BlockSpec((tq, SWA_KVW), lambda b, n: (cur(b, n), v_colblock)),
                  pl.BlockSpec((WINDOW, SWA_KVW), lambda b, n: (prev(b, n), v_colblock)),
                  pl.BlockSpec((2, SWA_H, WINDOW, WINDOW), lambda b, n: (0, 0, 0, 0)),
                  pl.BlockSpec(memory_space=pltpu.SMEM)],
        out_specs=pl.BlockSpec((tq, SWA_QW), lambda b, n: (cur(b, n), 0)),
        compiler_params=_cparams(("parallel", "arbitrary")),
        name="swa_prompt",
    )(proj, proj, kv, kv, kv, kv, bias, sinks)


def _swa_sample_kernel(q_ref, z_ref, kn_ref, vn_ref, ck_ref, cv_ref, bias_ref, sink_ref, o_ref, nk_ref, nv_ref,
                       *, n_seq_blk, n_new):
    scale = SWA_DH ** -0.5
    wb = ck_ref.shape[3]
    probs = [(s, kv) for s in range(n_seq_blk) for kv in range(SWA_KV)]
    ks = lambda kv: slice(kv * SWA_DH, (kv + 1) * SWA_DH)
    slot = lax.broadcasted_iota(jnp.int32, (SWA_DH, wb), 1)
    tok = lax.broadcasted_iota(jnp.int32, (kn_ref.shape[1], wb), 0)
    tok_slot = lax.broadcasted_iota(jnp.int32, (kn_ref.shape[1], wb), 1)
    place = jnp.where((tok_slot == tok + (wb - n_new)) & (tok < n_new), 1.0, 0.0).astype(F32)
    for s in range(n_seq_blk):
        for cache_ref, new_ref, out_ref in ((ck_ref, kn_ref, nk_ref), (cv_ref, vn_ref, nv_ref)):
            new_t = lax.dot_general(new_ref[s], place, (((0,), (0,)), ((), ())),
                                    preferred_element_type=F32, precision=lax.Precision.HIGHEST)
            for kv in range(SWA_KV):
                out_ref[s, kv] = jnp.where(slot >= wb - n_new, new_t[ks(kv), :],
                                           pltpu.roll(cache_ref[s, kv], wb - n_new, axis=1))
    lc = [_dot(q_ref[s, kv] * scale, ck_ref[s, kv]) + bias_ref[kv, :, 0:wb] for s, kv in probs]
    ln = [_dot_nt(q_ref[s, kv] * scale, kn_ref[s, :, ks(kv)]) + bias_ref[kv, :, wb:] for s, kv in probs]
    pcs, pns, dens = [], [], []
    for (s, kv), c, n in zip(probs, lc, ln):
        sink = sink_ref[kv][:, 0:1]
        m = jnp.maximum(jnp.maximum(jnp.max(c, axis=-1, keepdims=True), jnp.max(n, axis=-1, keepdims=True)),
                        sink)
        pc = jnp.exp(c - m)
        pn = jnp.exp(n - m)
        pcs.append(pc)
        pns.append(pn)
        dens.append(jnp.sum(pc, axis=-1, keepdims=True) + jnp.sum(pn, axis=-1, keepdims=True)
                    + jnp.exp(sink - m))
    outs = [(_dot_nt(pc, cv_ref[s, kv]) + _dot(pn, vn_ref[s, :, ks(kv)])) / den
            for (s, kv), pc, pn, den in zip(probs, pcs, pns, dens)]
    for oh, (s, kv) in zip(outs, probs):
        o_ref[s, kv] = (oh * _silu(z_ref[s, kv].astype(F32))).astype(o_ref.dtype)


def _swa_sample(q, z, k_new, v_new, cache_kt, cache_vt, bias, sink_rows, n_new, n_seq_blk=8):
    bd, _, rows, _ = q.shape
    wb = cache_kt.shape[3]
    npad = k_new.shape[1]
    assert wb == LANE
    blk4 = lambda i: (i, 0, 0, 0)
    blk3 = lambda i: (i, 0, 0)
    cache_spec = pl.BlockSpec((n_seq_blk, SWA_KV, SWA_DH, wb), blk4)
    return pl.pallas_call(
        functools.partial(_swa_sample_kernel, n_seq_blk=n_seq_blk, n_new=n_new),
        out_shape=(jax.ShapeDtypeStruct(q.shape, BF16),
                   jax.ShapeDtypeStruct(cache_kt.shape, F32), jax.ShapeDtypeStruct(cache_vt.shape, F32)),
        grid=(bd // n_seq_blk,),
        in_specs=[pl.BlockSpec((n_seq_blk, SWA_KV, rows, SWA_DH), blk4),
                  pl.BlockSpec((n_seq_blk, SWA_KV, rows, SWA_DH), blk4),
                  pl.BlockSpec((n_seq_blk, npad, SWA_KVW), blk3),
                  pl.BlockSpec((n_seq_blk, npad, SWA_KVW), blk3),
                  pl.BlockSpec((n_seq_blk, SWA_KV, SWA_DH, wb), blk4),
                  pl.BlockSpec((n_seq_blk, SWA_KV, SWA_DH, wb), blk4),
                  pl.BlockSpec((SWA_KV, rows, wb + npad), lambda i: (0, 0, 0)),
                  pl.BlockSpec((SWA_KV, rows, LANE), lambda i: (0, 0, 0))],
        out_specs=(pl.BlockSpec((n_seq_blk, SWA_KV, rows, SWA_DH), blk4), cache_spec, cache_spec),
        compiler_params=_cparams(("arbitrary",)),
        name="swa_sample",
    )(q, z, k_new, v_new, cache_kt, cache_vt, bias, sink_rows)


def _mem_attend(probs, q_of, z_of, k_of, v_of, store):
    scale = MEM_DH ** -0.5
    logits = [_dot_nt(q_of(p) * scale, k_of(p)) for p in probs]
    ps, dens = [], []
    for l in logits:
        m = jnp.max(l, axis=-1, keepdims=True)
        e = jnp.exp(l - m)
        dens.append(jnp.sum(e, axis=-1, keepdims=True))
        ps.append(e)
    outs = [_dot(e, v_of(p)) / den for e, den, p in zip(ps, dens, probs)]
    for p, oh in zip(probs, outs):
        store(p, oh * _silu(z_of(p).astype(F32)))


def _mem_cols(h):
    return slice(h * MEM_DH, (h + 1) * MEM_DH)


def _mem_prompt_kernel(q_ref, z_ref, k_ref, v_ref, o_ref):
    def store(h, val):
        o_ref[:, _mem_cols(h)] = val.astype(o_ref.dtype)
    _mem_attend(range(MEM_H), lambda h: q_ref[:, _mem_cols(h)], lambda h: z_ref[:, _mem_cols(h)],
                lambda h: k_ref[:, _mem_cols(h)], lambda h: v_ref[:, _mem_cols(h)], store)


def _mem_prompt(proj, mkv, n_seq, seq_len, q_colblock, z_colblock, tq=2048):
    tq = math.gcd(tq, seq_len)
    steps = seq_len // tq
    return pl.pallas_call(
        _mem_prompt_kernel,
        out_shape=jax.ShapeDtypeStruct((n_seq * seq_len, MEM_W), BF16),
        grid=(n_seq, steps),
        in_specs=[pl.BlockSpec((tq, MEM_W), lambda b, n: (b * steps + n, q_colblock)),
                  pl.BlockSpec((tq, MEM_W), lambda b, n: (b * steps + n, z_colblock)),
                  pl.BlockSpec((N_MEM, MEM_W), lambda b, n: (b, 0)),
                  pl.BlockSpec((N_MEM, MEM_W), lambda b, n: (b, 1))],
        out_specs=pl.BlockSpec((tq, MEM_W), lambda b, n: (b * steps + n, 0)),
        compiler_params=_cparams(("parallel", "arbitrary")),
        name="mem_prompt",
    )(proj, proj, mkv, mkv)


def _mem_sample_kernel(q_ref, z_ref, k_hbm, v_hbm, o_ref, kbuf, vbuf, sem, *, n_seq_blk):
    i = pl.program_id(0)
    n_steps = pl.num_programs(0)
    slot = i % 2

    def copies(step, slot_):
        seqs = pl.ds(step * n_seq_blk, n_seq_blk)
        out = []
        for h in range(MEM_H):
            out.append(pltpu.make_async_copy(k_hbm.at[seqs, :, h, :], kbuf.at[slot_, h], sem.at[0, slot_, h]))
            out.append(pltpu.make_async_copy(v_hbm.at[seqs, :, h, :], vbuf.at[slot_, h], sem.at[1, slot_, h]))
        return out

    @pl.when(i == 0)
    def _():
        for cp in copies(0, 0):
            cp.start()

    @pl.when(i + 1 < n_steps)
    def _():
        for cp in copies(i + 1, 1 - slot):
            cp.start()

    for cp in copies(i, slot):
        cp.wait()

    def store(p, val):
        o_ref[p[0], :, _mem_cols(p[1])] = val.astype(o_ref.dtype)
    probs = [(s, h) for s in range(n_seq_blk) for h in range(MEM_H)]
    _mem_attend(probs, lambda p: q_ref[p[0], :, _mem_cols(p[1])], lambda p: z_ref[p[0], :, _mem_cols(p[1])],
                lambda p: kbuf[slot, p[1], p[0]], lambda p: vbuf[slot, p[1], p[0]], store)


def _mem_sample(q, z, cache_k, cache_v, n_seq_blk=8):
    bd, rows, _ = q.shape
    blk = lambda i: (i, 0, 0)
    buf = pltpu.VMEM((2, MEM_H, n_seq_blk, N_MEM, MEM_DH), cache_k.dtype)
    return pl.pallas_call(
        functools.partial(_mem_sample_kernel, n_seq_blk=n_seq_blk),
        out_shape=jax.ShapeDtypeStruct(q.shape, BF16),
        grid=(bd // n_seq_blk,),
        in_specs=[pl.BlockSpec((n_seq_blk, rows, MEM_W), blk),
                  pl.BlockSpec((n_seq_blk, rows, MEM_W), blk),
                  pl.BlockSpec(memory_space=pl.ANY),
                  pl.BlockSpec(memory_space=pl.ANY)],
        out_specs=pl.BlockSpec((n_seq_blk, rows, MEM_W), blk),
        scratch_shapes=[buf, buf, pltpu.SemaphoreType.DMA((2, 2, MEM_H))],
        compiler_params=_cparams(("arbitrary",)),
        name="mem_sample",
    )(q, z, cache_k, cache_v)


def _merge_kernel(og_ref, os_ref, om_ref, gate_ref, x_ref, wb_ref, wo_ref, nf_ref, y_ref):
    merged = None
    for b, o_ref in enumerate((og_ref, os_ref, om_ref)):
        t = jnp.dot(o_ref[...], wb_ref[b], preferred_element_type=F32)
        t = t * gate_ref[:, b * D_MODEL:(b + 1) * D_MODEL].astype(F32)
        merged = t if merged is None else merged + t
    h = x_ref[...] + jnp.dot(merged.astype(BF16), wo_ref[...], preferred_element_type=F32)
    ms = jnp.mean(h * h, axis=-1, keepdims=True)
    y_ref[...] = h * lax.rsqrt(ms + NORM_EPS) * nf_ref[...]


def _merge(o_gdn, o_swa, o_mem, gates, x, w_branch, w_out, norm_f, tm=256):
    m = x.shape[0]
    tm = min(tm, m)
    row = lambda i: (i, 0)
    const2 = lambda i: (0, 0)
    return pl.pallas_call(
        _merge_kernel,
        out_shape=jax.ShapeDtypeStruct((m, D_MODEL), F32),
        grid=(m // tm,),
        in_specs=[pl.BlockSpec((tm, BR_W), row),
                  pl.BlockSpec((tm, BR_W), row),
                  pl.BlockSpec((tm, BR_W), row),
                  pl.BlockSpec((tm, N_BRANCH * D_MODEL), row),
                  pl.BlockSpec((tm, D_MODEL), row),
                  pl.BlockSpec((N_BRANCH, BR_W, D_MODEL), lambda i: (0, 0, 0),
                               pipeline_mode=pl.Buffered(1)),
                  pl.BlockSpec((D_MODEL, D_MODEL), const2, pipeline_mode=pl.Buffered(1)),
                  pl.BlockSpec((1, D_MODEL), const2)],
        out_specs=pl.BlockSpec((tm, D_MODEL), row),
        compiler_params=_cparams(("parallel",)),
        name="merge",
    )(o_gdn, o_swa, o_mem, gates, x, w_branch, w_out, norm_f)


_IN_SIZES = (GDN_QK, GDN_QK, GDN_W, GDN_W, GDN_H, GDN_H, SWA_QW, SWA_KVW, SWA_KVW, SWA_QW,
             MEM_W, MEM_W, N_BRANCH * D_MODEL)
_IN_NAMES = ("gq", "gk", "gv", "gz", "gb", "ga", "sq", "sk", "sv", "sz", "mq", "mz", "mg")
_IN_SPAN = {name: (int(off), int(off + size)) for name, off, size in
            zip(_IN_NAMES, np.cumsum((0,) + _IN_SIZES[:-1]), _IN_SIZES)}


def kernel(x_prompt, x_sample, state_gdn, state_gdn_conv, cache_swa_k, cache_swa_v, cache_mem_k,
           cache_mem_v, mem_prompt, norm_in, w_in, gdn_conv_w, gdn_a_log, gdn_dt_bias, gdn_norm,
           swa_sinks, rel_bias, norm_mem, w_mem_kv, w_branch, w_out, norm_f):
    n_layers = norm_in.shape[0]
    assert n_layers == 1
    b, seq, _ = x_prompt.shape
    bd, ns, _ = x_sample.shape
    wb = cache_swa_k.shape[2]
    assert seq % WINDOW == 0 and seq % GDN_CHUNK == 0 and ns + CONV_W <= SUBLANE and wb == WINDOW
    lyr = 0

    w = w_in[lyr]
    main_names = ("gq", "gk", "gv", "gz", "sq", "sz", "mq", "mz")
    n_main = len(main_names) * W_BLK
    n_gate = N_BRANCH * D_MODEL
    n_small = 3 * LANE
    small_lead = -(n_main + n_gate) % n_small
    w_all = _wprep(w.T, [[(_IN_SPAN[a][0], W_BLK)] for a in main_names]
                   + [[(_IN_SPAN["mg"][0] + W_BLK * k, W_BLK)] for k in range(n_gate // W_BLK)]
                   + [[(None, small_lead), (_IN_SPAN["sk"][0], 2 * SWA_KVW), (_IN_SPAN["gb"][0], 2 * GDN_H)]])
    col_small = n_main + n_gate + small_lead
    cb_gz, cb_sq, cb_sz, cb_mq, cb_mz = 3, 4, 5, 6, 7
    cb_sk, cb_sv, cb_ba = 0, 1, 2
    w_mkv = w_mem_kv[lyr]
    w_br = w_branch[lyr].astype(BF16)
    w_o = w_out[lyr].astype(BF16)
    nw_in = norm_in[lyr].reshape(1, D_MODEL)
    nw_mem = norm_mem[lyr].reshape(1, D_MODEL)
    nw_f = norm_f.reshape(1, D_MODEL)
    conv_w = gdn_conv_w[lyr]
    a_row = jnp.pad(gdn_a_log[lyr].reshape(1, GDN_H), ((0, 0), (GDN_H, LANE - 2 * GDN_H)))
    dt_row = jnp.pad(gdn_dt_bias[lyr].reshape(1, GDN_H), ((0, 0), (GDN_H, LANE - 2 * GDN_H)))
    gnw = gdn_norm[lyr].reshape(1, GDN_DV)
    sinks = swa_sinks[lyr]
    bias_p = _bias_prompt(rel_bias)
    npad = SUBLANE
    bias_s = _bias_sample(rel_bias, ns, wb, wb + npad)
    sink_rows = jnp.broadcast_to(jnp.repeat(sinks.reshape(SWA_KV, SWA_G), ns, axis=1)[:, :, None],
                                 (SWA_KV, SWA_G * ns, LANE))

    t = b * seq
    xp = x_prompt.reshape(t, D_MODEL)
    xn_p, p_small = _rmsnorm(xp, nw_in, w_all, col_small, n_small)
    p_main = _proj(xn_p, w_all, BF16, n=n_main)
    g_p = _proj(xn_p, w_all, BF16, act="sigmoid", col0=n_main, n=n_gate)
    xn_tail = xn_p.reshape(b, seq, D_MODEL)[:, seq - SUBLANE:, :].reshape(b * SUBLANE, D_MODEL)
    conv_p = _proj(xn_tail, w_all, F32, n=GDN_CONV_CH).reshape(b, SUBLANE, GDN_CONV_CH)[:, SUBLANE - (CONV_W - 1):]
    kv_tail = p_small.reshape(b, seq, 3 * LANE)[:, seq - WINDOW:]
    swk_p = kv_tail[:, :, cb_sk * LANE:(cb_sk + 1) * LANE].reshape(b, WINDOW, SWA_KV, SWA_DH)
    swv_p = kv_tail[:, :, cb_sv * LANE:(cb_sv + 1) * LANE].reshape(b, WINDOW, SWA_KV, SWA_DH)

    mkv = _proj(_rmsnorm(mem_prompt.reshape(b * N_MEM, D_MODEL), nw_mem), w_mkv, F32)
    mk_p = mkv[:, :MEM_W].reshape(b, N_MEM, MEM_W)
    mv_p = mkv[:, MEM_W:].reshape(b, N_MEM, MEM_W)

    o_gdn_p, s_p = _gdn(p_main, p_main, p_small, jnp.zeros((b, HDR, GDN_CONV_CH), F32),
                        jnp.zeros((b, GDN_H, GDN_DK, GDN_DV), F32), conv_w, a_row, dt_row, gnw,
                        n_seq=b, chunk=GDN_CHUNK, n_chunks=4, per_chunk_state=False,
                        valid_lo=0, valid_hi=GDN_CHUNK, out_dtype=BF16,
                        z_colblock=cb_gz, ba_colblock=cb_ba)
    o_swa_p = _swa_prompt(p_main, p_small, bias_p, sinks, b, seq, cb_sq, cb_sz, cb_sk, cb_sv)
    o_mem_p = _mem_prompt(p_main, mkv, b, seq, cb_mq, cb_mz)
    y_p = _merge(o_gdn_p, o_swa_p, o_mem_p, g_p, xp, w_br, w_o, nw_f).reshape(b, seq, D_MODEL)

    ts = bd * ns
    xs = x_sample.reshape(ts, D_MODEL)
    xn_s, s_small = _rmsnorm(xs, nw_in, w_all, col_small, n_small)
    s_main = _proj(xn_s, w_all, F32, n=n_main)
    g_s = _proj(xn_s, w_all, BF16, act="sigmoid", col0=n_main, n=n_gate)
    s_gdn = s_main[:, :GDN_CONV_CH + GDN_W]
    s_ba = s_small[:, cb_ba * LANE:(cb_ba + 1) * LANE]
    s_swa = jnp.concatenate([s_main[:, cb_sq * BR_W:(cb_sz + 1) * BR_W], s_small[:, :2 * LANE]], axis=1)
    s_mem = s_main[:, cb_mq * BR_W:(cb_mz + 1) * BR_W].astype(BF16)

    lo = CONV_W - 1
    hi = lo + ns
    pad_rows = ((0, 0), (lo, SUBLANE - hi), (0, 0))
    e_qkv = jnp.concatenate([state_gdn_conv[lyr], s_gdn[:, :GDN_CONV_CH].reshape(bd, ns, GDN_CONV_CH),
                             jnp.zeros((bd, SUBLANE - hi, GDN_CONV_CH), F32)], axis=1)
    e_z = jnp.pad(s_gdn[:, GDN_CONV_CH:].reshape(bd, ns, GDN_W), pad_rows)
    e_ba = jnp.pad(s_ba.reshape(bd, ns, LANE), pad_rows)
    seq_blk = 16
    o_gdn_s8, s_s = _gdn(e_qkv.reshape(bd * SUBLANE, GDN_CONV_CH), e_z.reshape(bd * SUBLANE, GDN_W),
                         e_ba.reshape(bd * SUBLANE, LANE), jnp.zeros((1, HDR, GDN_CONV_CH), F32),
                         state_gdn[lyr], conv_w, a_row, dt_row, gnw,
                         n_seq=bd, chunk=SUBLANE, n_chunks=seq_blk, per_chunk_state=True,
                         valid_lo=lo, valid_hi=hi, out_dtype=BF16)
    o_gdn_s = o_gdn_s8.reshape(bd, SUBLANE, GDN_W)[:, lo:hi].reshape(ts, GDN_W)
    conv_s = e_qkv[:, hi - (CONV_W - 1):hi]

    def to_heads(a):
        return a.reshape(bd, ns, SWA_KV, SWA_G, SWA_DH).transpose(0, 2, 3, 1, 4).reshape(
            bd, SWA_KV, SWA_G * ns, SWA_DH)

    k_new = s_swa[:, 2 * SWA_QW:2 * SWA_QW + SWA_KVW].reshape(bd, ns, SWA_KVW)
    v_new = s_swa[:, 2 * SWA_QW + SWA_KVW:].reshape(bd, ns, SWA_KVW)
    tok_pad = ((0, 0), (0, npad - ns), (0, 0))
    ck_t = cache_swa_k[lyr].transpose(0, 2, 3, 1)
    cv_t = cache_swa_v[lyr].transpose(0, 2, 3, 1)
    o_swa_h, nk_t, nv_t = _swa_sample(
        to_heads(s_swa[:, :SWA_QW]).astype(BF16), to_heads(s_swa[:, SWA_QW:2 * SWA_QW]),
        jnp.pad(k_new, tok_pad), jnp.pad(v_new, tok_pad), ck_t, cv_t, bias_s, sink_rows, n_new=ns)
    o_swa_s = o_swa_h.reshape(bd, SWA_KV, SWA_G, ns, SWA_DH).transpose(0, 3, 1, 2, 4).reshape(ts, SWA_QW)
    swk_s = nk_t.transpose(0, 3, 1, 2)
    swv_s = nv_t.transpose(0, 3, 1, 2)

    mq = jnp.pad(s_mem[:, :MEM_W].reshape(bd, ns, MEM_W), tok_pad)
    mz = jnp.pad(s_mem[:, MEM_W:].reshape(bd, ns, MEM_W), tok_pad)
    o_mem_s = _mem_sample(mq, mz, cache_mem_k[lyr], cache_mem_v[lyr])[:, :ns].reshape(ts, MEM_W)
    y_s = _merge(o_gdn_s, o_swa_s, o_mem_s, g_s, xs, w_br, w_o, nw_f).reshape(bd, ns, D_MODEL)

    return (y_p, y_s,
            s_p[None], conv_p[None], swk_p[None], swv_p[None],
            mk_p.reshape(b, N_MEM, MEM_H, MEM_DH)[None], mv_p.reshape(b, N_MEM, MEM_H, MEM_DH)[None],
            s_s[None], conv_s[None], swk_s[None], swv_s[None])
```

```python
import functools
import math

import numpy as np
import jax
import jax.numpy as jnp
from jax import lax
from jax.experimental import pallas as pl
from jax.experimental.pallas import tpu as pltpu

F32 = jnp.float32
BF16 = jnp.bfloat16

D_MODEL = 2048
N_BRANCH = 3
BR_W = 1024
GDN_H = 8
GDN_DK = 128
GDN_DV = 128
GDN_QK = GDN_H * GDN_DK
GDN_W = GDN_H * GDN_DV
GDN_CONV_CH = 2 * GDN_QK + GDN_W
CONV_W = 4
GDN_CHUNK = 64
SWA_H = 16
SWA_KV = 2
SWA_G = SWA_H // SWA_KV
SWA_DH = 64
SWA_QW = SWA_H * SWA_DH
SWA_KVW = SWA_KV * SWA_DH
WINDOW = 128
N_BUCKETS = 32
MAX_DISTANCE = 128
N_MEM = 256
MEM_H = 4
MEM_DH = 256
MEM_W = MEM_H * MEM_DH
NORM_EPS = 1e-6

LANE = 128
SUBLANE = 8
VMEM_LIMIT = 52 * 1024 * 1024


def _cparams(sem):
    return pltpu.CompilerParams(dimension_semantics=sem, vmem_limit_bytes=VMEM_LIMIT)


def _sigmoid(x):
    return 0.5 * jnp.tanh(0.5 * x) + 0.5


def _silu(x):
    h = 0.5 * x
    return h * jnp.tanh(h) + h


def _softplus(x):
    return jnp.maximum(x, 0.0) + jnp.log(1.0 + jnp.exp(-jnp.abs(x)))


def _dot(a, b):
    return jnp.dot(a.astype(BF16), b.astype(BF16), preferred_element_type=F32)


def _dot_nt(a, b):
    return lax.dot_general(a.astype(BF16), b.astype(BF16), (((1,), (1,)), ((), ())),
                           preferred_element_type=F32)


def _dot_tn(a, b):
    return lax.dot_general(a.astype(BF16), b.astype(BF16), (((0,), (0,)), ((), ())),
                           preferred_element_type=F32)


def _dot_f32(a, b):
    return jnp.dot(a, b, preferred_element_type=F32, precision=lax.Precision.HIGHEST)


def _rmsnorm_kernel(x_ref, nw_ref, *rest):
    x = x_ref[...].astype(F32)
    ms = jnp.mean(x * x, axis=-1, keepdims=True)
    xn = (x * lax.rsqrt(ms + NORM_EPS) * nw_ref[...]).astype(BF16)
    if len(rest) == 1:
        rest[0][...] = xn
    else:
        w_ref, o_ref, p_ref = rest
        o_ref[...] = xn
        p_ref[...] = jnp.dot(xn, w_ref[...], preferred_element_type=F32)


def _rmsnorm(x, norm_w, w=None, col0=0, n=None, tm_pref=1024):
    m, d = x.shape
    tm = min(m, tm_pref)
    assert m % tm == 0
    row = lambda i: (i, 0)
    in_specs = [pl.BlockSpec((tm, d), row), pl.BlockSpec((1, d), lambda i: (0, 0))]
    out_shape = jax.ShapeDtypeStruct((m, d), BF16)
    out_specs = pl.BlockSpec((tm, d), row)
    args = (x, norm_w)
    if w is not None:
        assert col0 % n == 0
        in_specs.append(pl.BlockSpec((d, n), lambda i: (0, col0 // n)))
        out_shape = (out_shape, jax.ShapeDtypeStruct((m, n), F32))
        out_specs = (out_specs, pl.BlockSpec((tm, n), row))
        args = args + (w,)
    return pl.pallas_call(
        _rmsnorm_kernel,
        out_shape=out_shape,
        grid=(m // tm,),
        in_specs=in_specs,
        out_specs=out_specs,
        compiler_params=_cparams(("parallel",)),
        name="rmsnorm",
    )(*args)


def _proj_kernel(x_ref, w_ref, o_ref, *, act):
    y = jnp.dot(x_ref[...], w_ref[...].astype(BF16), preferred_element_type=F32)
    if act == "sigmoid":
        y = _sigmoid(y)
    o_ref[...] = y.astype(o_ref.dtype)


def _pick_tile(n, pref):
    t = min(n, pref)
    while n % t:
        t -= LANE
    return t


def _proj(xn, w, out_dtype, act=None, col0=0, n=None, tm_pref=1024, tn_pref=2048):
    m, d = xn.shape
    n = w.shape[1] if n is None else n
    tm = min(m, tm_pref)
    assert m % tm == 0
    tn = _pick_tile(n, tn_pref if out_dtype == BF16 else tn_pref // 2)
    while col0 % tn:
        tn = _pick_tile(n, tn - LANE)
    jb = col0 // tn
    return pl.pallas_call(
        functools.partial(_proj_kernel, act=act),
        out_shape=jax.ShapeDtypeStruct((m, n), out_dtype),
        grid=(m // tm, n // tn),
        in_specs=[pl.BlockSpec((tm, d), lambda i, j: (i, 0)),
                  pl.BlockSpec((d, tn), lambda i, j: (0, jb + j))],
        out_specs=pl.BlockSpec((tm, tn), lambda i, j: (i, j)),
        compiler_params=_cparams(("parallel", "arbitrary")),
        name="proj",
    )(xn, w)


W_BLK = 1024
W_PIECE_ROWS = (W_BLK, 2 * SUBLANE)


def _wprep_kernel(wt_hbm, o_ref, inbuf0, inbuf1, sem, *, blocks):
    j = pl.program_id(0)
    inbuf = (inbuf0, inbuf1)

    def sources(b):
        return [(start, rows) for start, rows in blocks[b] if start is not None]

    def pieces(b):
        return [pltpu.make_async_copy(wt_hbm.at[pl.ds(start, rows), :],
                                      inbuf[k].at[b % 2, pl.ds(0, rows), :], sem.at[b % 2, k])
                for k, (start, rows) in enumerate(sources(b))]

    for b, ranges in enumerate(blocks):
        @pl.when(j == b)
        def _(b=b, ranges=ranges):
            if b == 0:
                for cp in pieces(0):
                    cp.start()
            if b + 1 < len(blocks):
                for cp in pieces(b + 1):
                    cp.start()
            for cp in pieces(b):
                cp.wait()
            n_have = sum(rows for _, rows in ranges)
            parts, k = [], 0
            for start, rows in tuple(ranges) + ((None, W_BLK - n_have),):
                if start is None:
                    if rows:
                        parts.append(jnp.zeros((rows, inbuf0.shape[2]), inbuf0.dtype))
                else:
                    parts.append(inbuf[k][b % 2, 0:rows, :])
                    k += 1
            val = parts[0] if len(parts) == 1 else jnp.concatenate(parts, axis=0)
            o_ref[...] = val.T.astype(BF16)


def _wprep(wt, block_srcs):
    ncol, d = wt.shape
    for ranges in block_srcs:
        srcs = [(s, rows) for s, rows in ranges if s is not None]
        assert len(srcs) <= len(W_PIECE_ROWS) and all(rows % SUBLANE == 0 for _, rows in ranges)
        for (s, rows), cap in zip(srcs, W_PIECE_ROWS):
            assert s % SUBLANE == 0 and rows <= cap and s + rows <= ncol
    return pl.pallas_call(
        functools.partial(_wprep_kernel, blocks=tuple(tuple(r) for r in block_srcs)),
        out_shape=jax.ShapeDtypeStruct((d, W_BLK * len(block_srcs)), BF16),
        grid=(len(block_srcs),),
        in_specs=[pl.BlockSpec(memory_space=pl.ANY)],
        out_specs=pl.BlockSpec((d, W_BLK), lambda j: (0, j)),
        scratch_shapes=[pltpu.VMEM((2, cap, d), wt.dtype) for cap in W_PIECE_ROWS]
                       + [pltpu.SemaphoreType.DMA((2, len(W_PIECE_ROWS)))],
        compiler_params=_cparams(("arbitrary",)),
        name="wprep",
    )(wt)


HDR = SUBLANE


def _tri_inv_many(a_list, c):
    row = lax.broadcasted_iota(jnp.int32, (c, c), 0)
    col = lax.broadcasted_iota(jnp.int32, (c, c), 1)
    eye = jnp.where(row == col, 1.0, 0.0).astype(F32)
    xs = [eye - a for a in a_list]
    bs = [_dot(a, a) for a in a_list]
    n = 2
    while n < c:
        xs = [x + _dot(x, b) for x, b in zip(xs, bs)]
        n *= 2
        if n < c:
            bs = [_dot(b, b) for b in bs]
    return xs


def _gdn_kernel(qkv_ref, z_ref, ba_ref, hdr_ref, s0_ref, cw_ref, arow_ref, dtrow_ref, nw_ref,
                o_ref, s_ref, buf_ref, cv_ref, gb_ref, oacc_ref, sol_ref, att_ref, gl_ref,
                *, chunk, n_chunks, chunks_per_seq, per_chunk_state, valid_lo, valid_hi):
    c = chunk
    tb = c * n_chunks
    cps = chunks_per_seq
    n_seqs = n_chunks // cps
    tbs = cps * c

    if per_chunk_state:
        buf_ref[0, 0:2 * HDR, :] = jnp.zeros((2 * HDR, GDN_CONV_CH), F32)
    else:
        @pl.when(pl.program_id(0) == 0)
        def _():
            for s in range(n_seqs):
                buf_ref[s, 0:HDR, :] = hdr_ref[s]
                buf_ref[s, HDR:2 * HDR, :] = jnp.zeros((HDR, GDN_CONV_CH), F32)
            s_ref[...] = s0_ref[...]

    def tap_from_history(s, j, n_rows):
        off = HDR - (CONV_W - 1) + j
        return buf_ref[s, off:off + n_rows, :] * cw_ref[j:j + 1, :]

    if qkv_ref.dtype == BF16:
        r = lax.broadcasted_iota(jnp.int32, (tbs, tbs), 0)
        cc = lax.broadcasted_iota(jnp.int32, (tbs, tbs), 1)
        for s in range(n_seqs):
            xb = qkv_ref[s]
            acc = xb.astype(F32) * cw_ref[CONV_W - 1:CONV_W, :]
            for j in range(CONV_W - 1):
                shift = jnp.where(r - cc == CONV_W - 1 - j, 1.0, 0.0).astype(BF16)
                acc = acc + jnp.dot(shift, xb, preferred_element_type=F32) * cw_ref[j:j + 1, :]
            cv_ref[s * tbs:(s + 1) * tbs, :] = _silu(acc)
            top = acc[0:HDR]
            for j in range(CONV_W - 1):
                top = top + tap_from_history(s, j, HDR)
            cv_ref[s * tbs:s * tbs + HDR, :] = _silu(top)
            buf_ref[s, 0:HDR, :] = qkv_ref[s, tbs - 2 * HDR:tbs, :].astype(F32)[HDR:]
    else:
        buf_ref[0, HDR:HDR + tb, :] = qkv_ref[...].astype(F32)
        acc = None
        for j in range(CONV_W):
            term = tap_from_history(0, j, tb)
            acc = term if acc is None else acc + term
        cv_ref[...] = _silu(acc)

    ba = ba_ref[...].astype(F32).reshape(tb, LANE)
    beta_all = _sigmoid(ba)
    g_all = -jnp.exp(arow_ref[...]) * _softplus(ba + dtrow_ref[...])
    if per_chunk_state:
        r = lax.broadcasted_iota(jnp.int32, (tb, LANE), 0) & (c - 1)
        valid = (r >= valid_lo) & (r < valid_hi)
        beta_all = jnp.where(valid, beta_all, 0.0)
        g_all = jnp.where(valid, g_all, 0.0)
    gb_ref[0] = beta_all
    gb_ref[1] = g_all

    row = lax.broadcasted_iota(jnp.int32, (c, c), 0)
    col = lax.broadcasted_iota(jnp.int32, (c, c), 1)
    causal = row >= col
    strict = row > col
    tril = jnp.where(causal, 1.0, 0.0).astype(F32)
    scale_q = GDN_DK ** -0.5

    heads = range(GDN_H)
    qcol = lambda h: slice(h * GDN_DK, (h + 1) * GDN_DK)
    kcol = lambda h: slice(GDN_QK + h * GDN_DK, GDN_QK + (h + 1) * GDN_DK)
    vcol = lambda h: slice(2 * GDN_QK + h * GDN_DV, 2 * GDN_QK + (h + 1) * GDN_DV)
    ucol = lambda h: slice(h * (GDN_DV + GDN_DK), h * (GDN_DV + GDN_DK) + GDN_DV)
    wcol = lambda h: slice(h * (GDN_DV + GDN_DK) + GDN_DV, (h + 1) * (GDN_DV + GDN_DK))
    if per_chunk_state:
        rv = lax.broadcasted_iota(jnp.int32, (c, 1), 0)
        rvalid = (rv >= valid_lo) & (rv < valid_hi)

    def chunk_rows(ci):
        return slice(ci * c, (ci + 1) * c)

    def prep(chunks):
        probs = []
        for ci in chunks:
            rows = chunk_rows(ci)
            beta_c = gb_ref[0, rows, :]
            gc_all = _dot_f32(tril, gb_ref[1, rows, :])
            gc_t = gc_all.T
            gl_ref[ci] = gc_all[c - 1:c, :]
            for h in heads:
                q = cv_ref[rows, qcol(h)]
                k = cv_ref[rows, kcol(h)]
                v = cv_ref[rows, vcol(h)]
                q = q * lax.rsqrt(jnp.sum(q * q, axis=-1, keepdims=True) + NORM_EPS) * scale_q
                k = k * lax.rsqrt(jnp.sum(k * k, axis=-1, keepdims=True) + NORM_EPS)
                if per_chunk_state:
                    q = jnp.where(rvalid, q, 0.0)
                    k = jnp.where(rvalid, k, 0.0)
                    v = jnp.where(rvalid, v, 0.0)
                beta = beta_c[:, h:h + 1]
                gc_col = gc_all[:, GDN_H + h:GDN_H + h + 1]
                gc_row = gc_t[GDN_H + h:GDN_H + h + 1, :]
                diff = jnp.where(causal, gc_col - gc_row, 0.0)
                decay = jnp.where(causal, jnp.exp(diff), 0.0)
                e_gc = jnp.exp(gc_col)
                kb = k * beta
                rhs = jnp.concatenate([v * beta, kb * e_gc], axis=1)
                cv_ref[rows, qcol(h)] = q * e_gc
                cv_ref[rows, kcol(h)] = k * jnp.exp(gc_col[c - 1:c, :] - gc_col)
                probs.append((rows, h, q, k, kb, rhs, decay))
        kq = [_dot_nt(jnp.concatenate([kb, q], axis=0), k) for (_, _, q, k, kb, _, _) in probs]
        a_list = [jnp.where(strict, kq_i[:c] * p[6], 0.0) for kq_i, p in zip(kq, probs)]
        t_inv = _tri_inv_many(a_list, c)
        for t_i, kq_i, (rows, h, _, _, _, rhs, decay) in zip(t_inv, kq, probs):
            sol_ref[rows, h * (GDN_DV + GDN_DK):(h + 1) * (GDN_DV + GDN_DK)] = _dot(t_i, rhs)
            att_ref[h, rows, :] = kq_i[c:] * decay

    def scan(chunks):
        probs = [(ci, h) for ci in chunks for h in heads]
        rows = chunk_rows
        si = lambda ci: ci // cps
        g_tot = {ci: jnp.exp(gl_ref[ci]) for ci in chunks}
        s_old = [s_ref[si(ci), h] for ci, h in probs]
        wq_s = [_dot(jnp.concatenate([sol_ref[rows(ci), wcol(h)], cv_ref[rows(ci), qcol(h)]], axis=0), s)
                for (ci, h), s in zip(probs, s_old)]
        v_new = [sol_ref[rows(ci), ucol(h)] - w[:c] for (ci, h), w in zip(probs, wq_s)]
        o_att = [_dot(att_ref[h, rows(ci), :], v) for (ci, h), v in zip(probs, v_new)]
        for (ci, h), s, v in zip(probs, s_old, v_new):
            s_ref[si(ci), h] = (s * g_tot[ci][:, GDN_H + h:GDN_H + h + 1]
                                + _dot_tn(cv_ref[rows(ci), kcol(h)], v))
        for (ci, h), w, oa in zip(probs, wq_s, o_att):
            o = w[c:] + oa
            o = o * lax.rsqrt(jnp.mean(o * o, axis=-1, keepdims=True) + NORM_EPS) * nw_ref[...]
            oacc_ref[rows(ci), h * GDN_DV:(h + 1) * GDN_DV] = o

    if per_chunk_state:
        s_ref[...] = s0_ref[...]
        prep(range(n_chunks))
        scan(range(n_chunks))
    else:
        for s in range(n_seqs):
            prep(range(s * cps, (s + 1) * cps))
        for j in range(cps):
            scan([s * cps + j for s in range(n_seqs)])
    gated = oacc_ref[...] * _silu(z_ref[...].astype(F32).reshape(tb, GDN_W))
    o_ref[...] = gated.astype(o_ref.dtype).reshape(o_ref.shape)


def _gdn(qkv, z, ba, hdr, s0, conv_w, a_row, dt_row, norm_w, *, chunk, n_chunks, chunks_per_seq,
         per_chunk_state, valid_lo, valid_hi, out_dtype, z_colblock=0, ba_colblock=0):
    tb = chunk * n_chunks
    if per_chunk_state:
        rows = qkv.shape[0]
        grid = (rows // tb,)
        row_blk = lambda width, cb: pl.BlockSpec((tb, width), lambda i: (i, cb))
        hdr_spec = pl.BlockSpec((1, HDR, GDN_CONV_CH), lambda i: (0, 0, 0))
        state_spec = pl.BlockSpec((n_chunks, GDN_H, GDN_DK, GDN_DV), lambda i: (i, 0, 0, 0))
        out_o = jax.ShapeDtypeStruct((rows, GDN_W), out_dtype)
        n_hist, hist_rows = 1, HDR + tb
    else:
        n_seq, seq_len = qkv.shape[0], qkv.shape[1]
        assert n_chunks == n_seq * chunks_per_seq and qkv.dtype == BF16
        tbs = chunk * chunks_per_seq
        grid = (seq_len // tbs,)
        row_blk = lambda width, cb: pl.BlockSpec((n_seq, tbs, width), lambda n: (0, n, cb))
        hdr_spec = pl.BlockSpec((n_seq, HDR, GDN_CONV_CH), lambda n: (0, 0, 0))
        state_spec = pl.BlockSpec((n_seq, GDN_H, GDN_DK, GDN_DV), lambda n: (0, 0, 0, 0))
        out_o = jax.ShapeDtypeStruct((n_seq, seq_len, GDN_W), out_dtype)
        n_hist, hist_rows = n_seq, 2 * HDR
    const2 = lambda *_: (0, 0)
    kern = functools.partial(_gdn_kernel, chunk=chunk, n_chunks=n_chunks, chunks_per_seq=chunks_per_seq,
                             per_chunk_state=per_chunk_state, valid_lo=valid_lo, valid_hi=valid_hi)
    return pl.pallas_call(
        kern,
        out_shape=(out_o, jax.ShapeDtypeStruct(s0.shape, F32)),
        grid=grid,
        in_specs=[row_blk(GDN_CONV_CH, 0),
                  row_blk(GDN_W, z_colblock),
                  row_blk(LANE, ba_colblock),
                  hdr_spec,
                  state_spec,
                  pl.BlockSpec((CONV_W, GDN_CONV_CH), const2),
                  pl.BlockSpec((1, LANE), const2),
                  pl.BlockSpec((1, LANE), const2),
                  pl.BlockSpec((1, GDN_DV), const2)],
        out_specs=(row_blk(GDN_W, 0), state_spec),
        scratch_shapes=[pltpu.VMEM((n_hist, hist_rows, GDN_CONV_CH), F32),
                        pltpu.VMEM((tb, GDN_CONV_CH), F32),
                        pltpu.VMEM((2, tb, LANE), F32),
                        pltpu.VMEM((tb, GDN_W), F32),
                        pltpu.VMEM((tb, GDN_H * (GDN_DV + GDN_DK)), F32),
                        pltpu.VMEM((GDN_H, tb, chunk), F32),
                        pltpu.VMEM((n_chunks, 1, LANE), F32)],
        compiler_params=_cparams(("arbitrary",)),
        name="gdn",
    )(qkv, z, ba, hdr, s0, conv_w, a_row, dt_row, norm_w)


def _t5_bucket_np(dist):
    n = np.maximum(dist, 0)
    max_exact = N_BUCKETS // 2
    nf = np.maximum(n, 1).astype(np.float32)
    large = max_exact + (np.log(nf / np.float32(max_exact)) / np.float32(math.log(MAX_DISTANCE / max_exact))
                         * np.float32(N_BUCKETS - max_exact)).astype(np.int32)
    large = np.minimum(large, N_BUCKETS - 1)
    return np.where(n < max_exact, n, large).astype(np.int32)


def _bias_prompt_kernel(code_ref, tab_ref, o_ref):
    code = code_ref[0]
    for h in range(SWA_H):
        acc = jnp.full(code.shape, -jnp.inf, F32)
        for b in range(N_BUCKETS):
            acc = jnp.where(code == b, tab_ref[b, h], acc)
        o_ref[0, h] = acc


def _bias_prompt(table):
    qi = np.arange(WINDOW)[None, :]
    sj = np.arange(WINDOW)[:, None]
    own = sj <= qi
    bucket = _t5_bucket_np(np.where(own, qi - sj, qi + WINDOW - sj))
    code_first = np.where(own, bucket, -1)
    code = jnp.asarray(np.stack([code_first, bucket]).astype(np.int32))
    return pl.pallas_call(
        _bias_prompt_kernel,
        out_shape=jax.ShapeDtypeStruct((2, SWA_H, WINDOW, WINDOW), F32),
        grid=(2,),
        in_specs=[pl.BlockSpec((1, WINDOW, WINDOW), lambda v: (v, 0, 0)),
                  pl.BlockSpec(memory_space=pltpu.SMEM)],
        out_specs=pl.BlockSpec((1, SWA_H, WINDOW, WINDOW), lambda v: (v, 0, 0, 0)),
        compiler_params=_cparams(("arbitrary",)),
        name="swa_bias_prompt",
    )(code, table)


def _bias_sample_kernel(code_ref, tab_ref, o_ref):
    kv = pl.program_id(0)
    code = code_ref[...]
    acc = jnp.full(code.shape, -jnp.inf, F32)
    for g in range(SWA_G):
        for b in range(N_BUCKETS):
            acc = jnp.where(code == b + N_BUCKETS * g, tab_ref[b, kv * SWA_G + g], acc)
    o_ref[0] = acc


def _bias_sample(table, n_tok, n_cache, n_keys_pad):
    dist = (n_cache + np.arange(n_tok))[:, None] - np.arange(n_keys_pad)[None, :]
    valid = (dist >= 0) & (dist < WINDOW) & (np.arange(n_keys_pad)[None, :] < n_cache + n_tok)
    bucket = _t5_bucket_np(dist)
    code_t = np.where(valid, bucket, -1)
    g = np.arange(SWA_G)[:, None, None]
    code = np.where(code_t[None] >= 0, code_t[None] + N_BUCKETS * g, -1)
    code = jnp.asarray(code.reshape(SWA_G * n_tok, n_keys_pad).astype(np.int32))
    return pl.pallas_call(
        _bias_sample_kernel,
        out_shape=jax.ShapeDtypeStruct((SWA_KV, SWA_G * n_tok, n_keys_pad), F32),
        grid=(SWA_KV,),
        in_specs=[pl.BlockSpec((SWA_G * n_tok, n_keys_pad), lambda k: (0, 0)),
                  pl.BlockSpec(memory_space=pltpu.SMEM)],
        out_specs=pl.BlockSpec((1, SWA_G * n_tok, n_keys_pad), lambda k: (k, 0, 0)),
        compiler_params=_cparams(("arbitrary",)),
        name="swa_bias_sample",
    )(code, table)


def _sink_softmax_pv(logits, sink, v):
    m = jnp.maximum(jnp.max(logits, axis=-1, keepdims=True), sink)
    p = jnp.exp(logits - m)
    den = jnp.sum(p, axis=-1, keepdims=True) + jnp.exp(sink - m)
    return _dot(p, v) / den


def _swa_prompt_kernel(q_ref, z_ref, kc_ref, kp_ref, vc_ref, vp_ref, bias_ref, sink_ref, o_ref, *, n_qblk):
    kall = jnp.concatenate([kp_ref[...], kc_ref[...]], axis=0)
    vall = jnp.concatenate([vp_ref[...], vc_ref[...]], axis=0)
    scale = SWA_DH ** -0.5
    key = lax.broadcasted_iota(jnp.int32, (WINDOW, WINDOW), 0)
    qry = lax.broadcasted_iota(jnp.int32, (WINDOW, WINDOW), 1)
    from_prev = key > qry
    first_variant = jnp.where(pl.program_id(1) == 0, 0, 1)
    cs = lambda h: slice(h * SWA_DH, (h + 1) * SWA_DH)
    rows = lambda qb: slice(qb * WINDOW, (qb + 1) * WINDOW)
    keys = lambda qb: slice(qb * WINDOW, (qb + 2) * WINDOW)
    for kv in range(SWA_KV):
        ks = slice(kv * SWA_DH, (kv + 1) * SWA_DH)
        k_kv = kall[:, ks].astype(BF16)
        v_t = vall[:, ks].astype(F32).T.astype(BF16)
        probs = [(qb, h) for qb in range(n_qblk) for h in range(kv * SWA_G, (kv + 1) * SWA_G)]
        lg = [_dot_nt(k_kv[keys(qb)], q_ref[rows(qb), cs(h)] * scale) for qb, h in probs]
        ps, dens = [], []
        for (qb, h), l in zip(probs, lg):
            bias = bias_ref[first_variant if qb == 0 else 1, h]
            l = jnp.where(from_prev, l[:WINDOW], l[WINDOW:]) + bias
            sink = sink_ref[h]
            m = jnp.maximum(jnp.max(l, axis=0, keepdims=True), sink)
            p = jnp.exp(l - m)
            dens.append(jnp.sum(p, axis=0, keepdims=True) + jnp.exp(sink - m))
            ps.append(jnp.concatenate([jnp.where(from_prev, p, 0.0), jnp.where(from_prev, 0.0, p)],
                                      axis=0))
        outs = [_dot(v_t[:, keys(qb)], p) / den for p, den, (qb, _) in zip(ps, dens, probs)]
        for j in range(0, len(probs), 2):
            qb, h = probs[j]
            two = slice(h * SWA_DH, (h + 2) * SWA_DH)
            o2 = jnp.concatenate([outs[j], outs[j + 1]], axis=0).T
            o_ref[rows(qb), two] = (o2 * _silu(z_ref[rows(qb), two].astype(F32))).astype(o_ref.dtype)


def _swa_prompt(proj, kv, bias, sinks, n_seq, seq_len, q_colblock, z_colblock, k_colblock, v_colblock,
                n_qblk=8):
    n_qblk = math.gcd(n_qblk, seq_len // WINDOW)
    tq = n_qblk * WINDOW
    steps = seq_len // tq
    cur = lambda b, n: b * steps + n
    prev = lambda b, n: (b * steps + n) * n_qblk - jnp.where(n == 0, 0, 1)
    return pl.pallas_call(
        functools.partial(_swa_prompt_kernel, n_qblk=n_qblk),
        out_shape=jax.ShapeDtypeStruct((n_seq * seq_len, SWA_QW), BF16),
        grid=(n_seq, steps),
        in_specs=[pl.BlockSpec((tq, SWA_QW), lambda b, n: (cur(b, n), q_colblock)),
                  pl.BlockSpec((tq, SWA_QW), lambda b, n: (cur(b, n), z_colblock)),
                  pl.BlockSpec((tq, SWA_KVW), lambda b, n: (cur(b, n), k_colblock)),
                  pl.BlockSpec((WINDOW, SWA_KVW), lambda b, n: (prev(b, n), k_colblock)),
                  pl.BlockSpec((tq, SWA_KVW), lambda b, n: (cur(b, n), v_colblock)),
                  pl.BlockSpec((WINDOW, SWA_KVW), lambda b, n: (prev(b, n), v_colblock)),
                  pl.BlockSpec((2, SWA_H, WINDOW, WINDOW), lambda b, n: (0, 0, 0, 0)),
                  pl.BlockSpec(memory_space=pltpu.SMEM)],
        out_specs=pl.BlockSpec((tq, SWA_QW), lambda b, n: (cur(b, n), 0)),
        compiler_params=_cparams(("parallel", "arbitrary")),
        name="swa_prompt",
    )(proj, proj, kv, kv, kv, kv, bias, sinks)


def _swa_sample_kernel(q_ref, z_ref, kn_ref, vn_ref, ck_ref, cv_ref, bias_ref, sink_ref, o_ref, nk_ref, nv_ref,
                       *, n_seq_blk, n_new):
    scale = SWA_DH ** -0.5
    wb = ck_ref.shape[3]
    probs = [(s, kv) for s in range(n_seq_blk) for kv in range(SWA_KV)]
    ks = lambda kv: slice(kv * SWA_DH, (kv + 1) * SWA_DH)
    slot = lax.broadcasted_iota(jnp.int32, (SWA_DH, wb), 1)
    tok = lax.broadcasted_iota(jnp.int32, (kn_ref.shape[1], wb), 0)
    tok_slot = lax.broadcasted_iota(jnp.int32, (kn_ref.shape[1], wb), 1)
    place = jnp.where((tok_slot == tok + (wb - n_new)) & (tok < n_new), 1.0, 0.0).astype(F32)
    for s in range(n_seq_blk):
        for cache_ref, new_ref, out_ref in ((ck_ref, kn_ref, nk_ref), (cv_ref, vn_ref, nv_ref)):
            new_t = lax.dot_general(new_ref[s], place, (((0,), (0,)), ((), ())),
                                    preferred_element_type=F32, precision=lax.Precision.HIGHEST)
            for kv in range(SWA_KV):
                out_ref[s, kv] = jnp.where(slot >= wb - n_new, new_t[ks(kv), :],
                                           pltpu.roll(cache_ref[s, kv], wb - n_new, axis=1))
    lc = [_dot(q_ref[s, kv] * scale, ck_ref[s, kv]) + bias_ref[kv, :, 0:wb] for s, kv in probs]
    ln = [_dot_nt(q_ref[s, kv] * scale, kn_ref[s, :, ks(kv)]) + bias_ref[kv, :, wb:] for s, kv in probs]
    pcs, pns, dens = [], [], []
    for (s, kv), c, n in zip(probs, lc, ln):
        sink = sink_ref[kv][:, 0:1]
        m = jnp.maximum(jnp.maximum(jnp.max(c, axis=-1, keepdims=True), jnp.max(n, axis=-1, keepdims=True)),
                        sink)
        pc = jnp.exp(c - m)
        pn = jnp.exp(n - m)
        pcs.append(pc)
        pns.append(pn)
        dens.append(jnp.sum(pc, axis=-1, keepdims=True) + jnp.sum(pn, axis=-1, keepdims=True)
                    + jnp.exp(sink - m))
    outs = [(_dot_nt(pc, cv_ref[s, kv]) + _dot(pn, vn_ref[s, :, ks(kv)])) / den
            for (s, kv), pc, pn, den in zip(probs, pcs, pns, dens)]
    for oh, (s, kv) in zip(outs, probs):
        o_ref[s, kv] = (oh * _silu(z_ref[s, kv].astype(F32))).astype(o_ref.dtype)


def _swa_sample(q, z, k_new, v_new, cache_kt, cache_vt, bias, sink_rows, n_new, n_seq_blk=8):
    bd, _, rows, _ = q.shape
    wb = cache_kt.shape[3]
    npad = k_new.shape[1]
    assert wb == LANE
    blk4 = lambda i: (i, 0, 0, 0)
    blk3 = lambda i: (i, 0, 0)
    cache_spec = pl.BlockSpec((n_seq_blk, SWA_KV, SWA_DH, wb), blk4)
    return pl.pallas_call(
        functools.partial(_swa_sample_kernel, n_seq_blk=n_seq_blk, n_new=n_new),
        out_shape=(jax.ShapeDtypeStruct(q.shape, BF16),
                   jax.ShapeDtypeStruct(cache_kt.shape, F32), jax.ShapeDtypeStruct(cache_vt.shape, F32)),
        grid=(bd // n_seq_blk,),
        in_specs=[pl.BlockSpec((n_seq_blk, SWA_KV, rows, SWA_DH), blk4),
                  pl.BlockSpec((n_seq_blk, SWA_KV, rows, SWA_DH), blk4),
                  pl.BlockSpec((n_seq_blk, npad, SWA_KVW), blk3),
                  pl.BlockSpec((n_seq_blk, npad, SWA_KVW), blk3),
                  pl.BlockSpec((n_seq_blk, SWA_KV, SWA_DH, wb), blk4),
                  pl.BlockSpec((n_seq_blk, SWA_KV, SWA_DH, wb), blk4),
                  pl.BlockSpec((SWA_KV, rows, wb + npad), lambda i: (0, 0, 0)),
                  pl.BlockSpec((SWA_KV, rows, LANE), lambda i: (0, 0, 0))],
        out_specs=(pl.BlockSpec((n_seq_blk, SWA_KV, rows, SWA_DH), blk4), cache_spec, cache_spec),
        compiler_params=_cparams(("arbitrary",)),
        name="swa_sample",
    )(q, z, k_new, v_new, cache_kt, cache_vt, bias, sink_rows)


def _mem_attend(probs, q_of, z_of, k_of, v_of, store):
    scale = MEM_DH ** -0.5
    logits = [_dot_nt(q_of(p) * scale, k_of(p)) for p in probs]
    ps, dens = [], []
    for l in logits:
        m = jnp.max(l, axis=-1, keepdims=True)
        e = jnp.exp(l - m)
        dens.append(jnp.sum(e, axis=-1, keepdims=True))
        ps.append(e)
    outs = [_dot(e, v_of(p)) / den for e, den, p in zip(ps, dens, probs)]
    for p, oh in zip(probs, outs):
        store(p, oh * _silu(z_of(p).astype(F32)))


def _mem_cols(h):
    return slice(h * MEM_DH, (h + 1) * MEM_DH)


def _mem_prompt_kernel(q_ref, z_ref, k_ref, v_ref, o_ref):
    def store(h, val):
        o_ref[:, _mem_cols(h)] = val.astype(o_ref.dtype)
    _mem_attend(range(MEM_H), lambda h: q_ref[:, _mem_cols(h)], lambda h: z_ref[:, _mem_cols(h)],
                lambda h: k_ref[:, _mem_cols(h)], lambda h: v_ref[:, _mem_cols(h)], store)


def _mem_prompt(proj, mkv, n_seq, seq_len, q_colblock, z_colblock, tq=2048):
    tq = math.gcd(tq, seq_len)
    steps = seq_len // tq
    return pl.pallas_call(
        _mem_prompt_kernel,
        out_shape=jax.ShapeDtypeStruct((n_seq * seq_len, MEM_W), BF16),
        grid=(n_seq, steps),
        in_specs=[pl.BlockSpec((tq, MEM_W), lambda b, n: (b * steps + n, q_colblock)),
                  pl.BlockSpec((tq, MEM_W), lambda b, n: (b * steps + n, z_colblock)),
                  pl.BlockSpec((N_MEM, MEM_W), lambda b, n: (b, 0)),
                  pl.BlockSpec((N_MEM, MEM_W), lambda b, n: (b, 1))],
        out_specs=pl.BlockSpec((tq, MEM_W), lambda b, n: (b * steps + n, 0)),
        compiler_params=_cparams(("parallel", "arbitrary")),
        name="mem_prompt",
    )(proj, proj, mkv, mkv)


def _mem_sample_kernel(q_ref, z_ref, k_hbm, v_hbm, o_ref, kbuf, vbuf, sem, *, n_seq_blk):
    i = pl.program_id(0)
    n_steps = pl.num_programs(0)
    slot = i % 2

    def copies(step, slot_):
        seqs = pl.ds(step * n_seq_blk, n_seq_blk)
        out = []
        for h in range(MEM_H):
            out.append(pltpu.make_async_copy(k_hbm.at[seqs, :, h, :], kbuf.at[slot_, h], sem.at[0, slot_, h]))
            out.append(pltpu.make_async_copy(v_hbm.at[seqs, :, h, :], vbuf.at[slot_, h], sem.at[1, slot_, h]))
        return out

    @pl.when(i == 0)
    def _():
        for cp in copies(0, 0):
            cp.start()

    @pl.when(i + 1 < n_steps)
    def _():
        for cp in copies(i + 1, 1 - slot):
            cp.start()

    for cp in copies(i, slot):
        cp.wait()

    def store(p, val):
        o_ref[p[0], :, _mem_cols(p[1])] = val.astype(o_ref.dtype)
    probs = [(s, h) for s in range(n_seq_blk) for h in range(MEM_H)]
    _mem_attend(probs, lambda p: q_ref[p[0], :, _mem_cols(p[1])], lambda p: z_ref[p[0], :, _mem_cols(p[1])],
                lambda p: kbuf[slot, p[1], p[0]], lambda p: vbuf[slot, p[1], p[0]], store)


def _mem_sample(q, z, cache_k, cache_v, n_seq_blk=8):
    bd, rows, _ = q.shape
    blk = lambda i: (i, 0, 0)
    buf = pltpu.VMEM((2, MEM_H, n_seq_blk, N_MEM, MEM_DH), cache_k.dtype)
    return pl.pallas_call(
        functools.partial(_mem_sample_kernel, n_seq_blk=n_seq_blk),
        out_shape=jax.ShapeDtypeStruct(q.shape, BF16),
        grid=(bd // n_seq_blk,),
        in_specs=[pl.BlockSpec((n_seq_blk, rows, MEM_W), blk),
                  pl.BlockSpec((n_seq_blk, rows, MEM_W), blk),
                  pl.BlockSpec(memory_space=pl.ANY),
                  pl.BlockSpec(memory_space=pl.ANY)],
        out_specs=pl.BlockSpec((n_seq_blk, rows, MEM_W), blk),
        scratch_shapes=[buf, buf, pltpu.SemaphoreType.DMA((2, 2, MEM_H))],
        compiler_params=_cparams(("arbitrary",)),
        name="mem_sample",
    )(q, z, cache_k, cache_v)


def _merge_kernel(og_ref, os_ref, om_ref, gate_ref, x_ref, wb_ref, wo_ref, nf_ref, y_ref):
    merged = None
    for b, o_ref in enumerate((og_ref, os_ref, om_ref)):
        t = jnp.dot(o_ref[...], wb_ref[b], preferred_element_type=F32)
        t = t * gate_ref[:, b * D_MODEL:(b + 1) * D_MODEL].astype(F32)
        merged = t if merged is None else merged + t
    h = x_ref[...] + jnp.dot(merged.astype(BF16), wo_ref[...], preferred_element_type=F32)
    ms = jnp.mean(h * h, axis=-1, keepdims=True)
    y_ref[...] = h * lax.rsqrt(ms + NORM_EPS) * nf_ref[...]


def _merge(o_gdn, o_swa, o_mem, gates, x, w_branch, w_out, norm_f, tm=256):
    m = x.shape[0]
    tm = min(tm, m)
    row = lambda i: (i, 0)
    const2 = lambda i: (0, 0)
    return pl.pallas_call(
        _merge_kernel,
        out_shape=jax.ShapeDtypeStruct((m, D_MODEL), F32),
        grid=(m // tm,),
        in_specs=[pl.BlockSpec((tm, BR_W), row),
                  pl.BlockSpec((tm, BR_W), row),
                  pl.BlockSpec((tm, BR_W), row),
                  pl.BlockSpec((tm, N_BRANCH * D_MODEL), row),
                  pl.BlockSpec((tm, D_MODEL), row),
                  pl.BlockSpec((N_BRANCH, BR_W, D_MODEL), lambda i: (0, 0, 0),
                               pipeline_mode=pl.Buffered(1)),
                  pl.BlockSpec((D_MODEL, D_MODEL), const2, pipeline_mode=pl.Buffered(1)),
                  pl.BlockSpec((1, D_MODEL), const2)],
        out_specs=pl.BlockSpec((tm, D_MODEL), row),
        compiler_params=_cparams(("parallel",)),
        name="merge",
    )(o_gdn, o_swa, o_mem, gates, x, w_branch, w_out, norm_f)


_IN_SIZES = (GDN_QK, GDN_QK, GDN_W, GDN_W, GDN_H, GDN_H, SWA_QW, SWA_KVW, SWA_KVW, SWA_QW,
             MEM_W, MEM_W, N_BRANCH * D_MODEL)
_IN_NAMES = ("gq", "gk", "gv", "gz", "gb", "ga", "sq", "sk", "sv", "sz", "mq", "mz", "mg")
_IN_SPAN = {name: (int(off), int(off + size)) for name, off, size in
            zip(_IN_NAMES, np.cumsum((0,) + _IN_SIZES[:-1]), _IN_SIZES)}


def kernel(x_prompt, x_sample, state_gdn, state_gdn_conv, cache_swa_k, cache_swa_v, cache_mem_k,
           cache_mem_v, mem_prompt, norm_in, w_in, gdn_conv_w, gdn_a_log, gdn_dt_bias, gdn_norm,
           swa_sinks, rel_bias, norm_mem, w_mem_kv, w_branch, w_out, norm_f):
    n_layers = norm_in.shape[0]
    assert n_layers == 1
    b, seq, _ = x_prompt.shape
    bd, ns, _ = x_sample.shape
    wb = cache_swa_k.shape[2]
    assert seq % WINDOW == 0 and seq % GDN_CHUNK == 0 and ns + CONV_W <= SUBLANE and wb == WINDOW
    lyr = 0

    w = w_in[lyr]
    main_names = ("gq", "gk", "gv", "gz", "sq", "sz", "mq", "mz")
    n_main = len(main_names) * W_BLK
    n_gate = N_BRANCH * D_MODEL
    n_small = 3 * LANE
    small_lead = -(n_main + n_gate) % n_small
    w_all = _wprep(w.T, [[(_IN_SPAN[a][0], W_BLK)] for a in main_names]
                   + [[(_IN_SPAN["mg"][0] + W_BLK * k, W_BLK)] for k in range(n_gate // W_BLK)]
                   + [[(None, small_lead), (_IN_SPAN["sk"][0], 2 * SWA_KVW), (_IN_SPAN["gb"][0], 2 * GDN_H)]])
    col_small = n_main + n_gate + small_lead
    cb_gz, cb_sq, cb_sz, cb_mq, cb_mz = 3, 4, 5, 6, 7
    cb_sk, cb_sv, cb_ba = 0, 1, 2
    w_mkv = w_mem_kv[lyr]
    w_br = w_branch[lyr].astype(BF16)
    w_o = w_out[lyr].astype(BF16)
    nw_in = norm_in[lyr].reshape(1, D_MODEL)
    nw_mem = norm_mem[lyr].reshape(1, D_MODEL)
    nw_f = norm_f.reshape(1, D_MODEL)
    conv_w = gdn_conv_w[lyr]
    a_row = jnp.pad(gdn_a_log[lyr].reshape(1, GDN_H), ((0, 0), (GDN_H, LANE - 2 * GDN_H)))
    dt_row = jnp.pad(gdn_dt_bias[lyr].reshape(1, GDN_H), ((0, 0), (GDN_H, LANE - 2 * GDN_H)))
    gnw = gdn_norm[lyr].reshape(1, GDN_DV)
    sinks = swa_sinks[lyr]
    bias_p = _bias_prompt(rel_bias)
    npad = SUBLANE
    bias_s = _bias_sample(rel_bias, ns, wb, wb + npad)
    sink_rows = jnp.broadcast_to(jnp.repeat(sinks.reshape(SWA_KV, SWA_G), ns, axis=1)[:, :, None],
                                 (SWA_KV, SWA_G * ns, LANE))

    t = b * seq
    xp = x_prompt.reshape(t, D_MODEL)
    xn_p, p_small = _rmsnorm(xp, nw_in, w_all, col_small, n_small)
    p_main = _proj(xn_p, w_all, BF16, n=n_main)
    g_p = _proj(xn_p, w_all, BF16, act="sigmoid", col0=n_main, n=n_gate)
    xn_tail = xn_p.reshape(b, seq, D_MODEL)[:, seq - SUBLANE:, :].reshape(b * SUBLANE, D_MODEL)
    conv_p = _proj(xn_tail, w_all, F32, n=GDN_CONV_CH).reshape(b, SUBLANE, GDN_CONV_CH)[:, SUBLANE - (CONV_W - 1):]
    kv_tail = p_small.reshape(b, seq, 3 * LANE)[:, seq - WINDOW:]
    swk_p = kv_tail[:, :, cb_sk * LANE:(cb_sk + 1) * LANE].reshape(b, WINDOW, SWA_KV, SWA_DH)
    swv_p = kv_tail[:, :, cb_sv * LANE:(cb_sv + 1) * LANE].reshape(b, WINDOW, SWA_KV, SWA_DH)

    mkv = _proj(_rmsnorm(mem_prompt.reshape(b * N_MEM, D_MODEL), nw_mem), w_mkv, F32)
    mk_p = mkv[:, :MEM_W].reshape(b, N_MEM, MEM_W)
    mv_p = mkv[:, MEM_W:].reshape(b, N_MEM, MEM_W)

    p_main3 = p_main.reshape(b, seq, n_main)
    gdn_cps = 4
    o_gdn_p, s_p = _gdn(p_main3, p_main3, p_small.reshape(b, seq, n_small),
                        jnp.zeros((b, HDR, GDN_CONV_CH), F32),
                        jnp.zeros((b, GDN_H, GDN_DK, GDN_DV), F32), conv_w, a_row, dt_row, gnw,
                        chunk=GDN_CHUNK, n_chunks=gdn_cps * b, chunks_per_seq=gdn_cps,
                        per_chunk_state=False, valid_lo=0, valid_hi=GDN_CHUNK, out_dtype=BF16,
                        z_colblock=cb_gz, ba_colblock=cb_ba)
    o_gdn_p = o_gdn_p.reshape(t, GDN_W)
    o_swa_p = _swa_prompt(p_main, p_small, bias_p, sinks, b, seq, cb_sq, cb_sz, cb_sk, cb_sv)
    o_mem_p = _mem_prompt(p_main, mkv, b, seq, cb_mq, cb_mz)
    y_p = _merge(o_gdn_p, o_swa_p, o_mem_p, g_p, xp, w_br, w_o, nw_f).reshape(b, seq, D_MODEL)

    ts = bd * ns
    xs = x_sample.reshape(ts, D_MODEL)
    xn_s, s_small = _rmsnorm(xs, nw_in, w_all, col_small, n_small)
    s_main = _proj(xn_s, w_all, F32, n=n_main)
    g_s = _proj(xn_s, w_all, BF16, act="sigmoid", col0=n_main, n=n_gate)
    s_gdn = s_main[:, :GDN_CONV_CH + GDN_W]
    s_ba = s_small[:, cb_ba * LANE:(cb_ba + 1) * LANE]
    s_swa = jnp.concatenate([s_main[:, cb_sq * BR_W:(cb_sz + 1) * BR_W], s_small[:, :2 * LANE]], axis=1)
    s_mem = s_main[:, cb_mq * BR_W:(cb_mz + 1) * BR_W].astype(BF16)

    lo = CONV_W - 1
    hi = lo + ns
    pad_rows = ((0, 0), (lo, SUBLANE - hi), (0, 0))
    e_qkv = jnp.concatenate([state_gdn_conv[lyr], s_gdn[:, :GDN_CONV_CH].reshape(bd, ns, GDN_CONV_CH),
                             jnp.zeros((bd, SUBLANE - hi, GDN_CONV_CH), F32)], axis=1)
    e_z = jnp.pad(s_gdn[:, GDN_CONV_CH:].reshape(bd, ns, GDN_W), pad_rows)
    e_ba = jnp.pad(s_ba.reshape(bd, ns, LANE), pad_rows)
    seq_blk = 16
    o_gdn_s8, s_s = _gdn(e_qkv.reshape(bd * SUBLANE, GDN_CONV_CH), e_z.reshape(bd * SUBLANE, GDN_W),
                         e_ba.reshape(bd * SUBLANE, LANE), jnp.zeros((1, HDR, GDN_CONV_CH), F32),
                         state_gdn[lyr], conv_w, a_row, dt_row, gnw,
                         chunk=SUBLANE, n_chunks=seq_blk, chunks_per_seq=1, per_chunk_state=True,
                         valid_lo=lo, valid_hi=hi, out_dtype=BF16)
    o_gdn_s = o_gdn_s8.reshape(bd, SUBLANE, GDN_W)[:, lo:hi].reshape(ts, GDN_W)
    conv_s = e_qkv[:, hi - (CONV_W - 1):hi]

    def to_heads(a):
        return a.reshape(bd, ns, SWA_KV, SWA_G, SWA_DH).transpose(0, 2, 3, 1, 4).reshape(
            bd, SWA_KV, SWA_G * ns, SWA_DH)

    k_new = s_swa[:, 2 * SWA_QW:2 * SWA_QW + SWA_KVW].reshape(bd, ns, SWA_KVW)
    v_new = s_swa[:, 2 * SWA_QW + SWA_KVW:].reshape(bd, ns, SWA_KVW)
    tok_pad = ((0, 0), (0, npad - ns), (0, 0))
    ck_t = cache_swa_k[lyr].transpose(0, 2, 3, 1)
    cv_t = cache_swa_v[lyr].transpose(0, 2, 3, 1)
    o_swa_h, nk_t, nv_t = _swa_sample(
        to_heads(s_swa[:, :SWA_QW]).astype(BF16), to_heads(s_swa[:, SWA_QW:2 * SWA_QW]),
        jnp.pad(k_new, tok_pad), jnp.pad(v_new, tok_pad), ck_t, cv_t, bias_s, sink_rows, n_new=ns)
    o_swa_s = o_swa_h.reshape(bd, SWA_KV, SWA_G, ns, SWA_DH).transpose(0, 3, 1, 2, 4).reshape(ts, SWA_QW)
    swk_s = nk_t.transpose(0, 3, 1, 2)
    swv_s = nv_t.transpose(0, 3, 1, 2)

    mq = jnp.pad(s_mem[:, :MEM_W].reshape(bd, ns, MEM_W), tok_pad)
    mz = jnp.pad(s_mem[:, MEM_W:].reshape(bd, ns, MEM_W), tok_pad)
    o_mem_s = _mem_sample(mq, mz, cache_mem_k[lyr], cache_mem_v[lyr])[:, :ns].reshape(ts, MEM_W)
    y_s = _merge(o_gdn_s, o_swa_s, o_mem_s, g_s, xs, w_br, w_o, nw_f).reshape(bd, ns, D_MODEL)

    return (y_p, y_s,
            s_p[None], conv_p[None], swk_p[None], swv_p[None],
            mk_p.reshape(b, N_MEM, MEM_H, MEM_DH)[None], mv_p.reshape(b, N_MEM, MEM_H, MEM_DH)[None],
            s_s[None], conv_s[None], swk_s[None], swv_s[None])
```

```python
import functools
import math

import numpy as np
import jax
import jax.numpy as jnp
from jax import lax
from jax.experimental import pallas as pl
from jax.experimental.pallas import tpu as pltpu

F32 = jnp.float32
BF16 = jnp.bfloat16

D_MODEL = 2048
N_BRANCH = 3
BR_W = 1024
GDN_H = 8
GDN_DK = 128
GDN_DV = 128
GDN_QK = GDN_H * GDN_DK
GDN_W = GDN_H * GDN_DV
GDN_CONV_CH = 2 * GDN_QK + GDN_W
CONV_W = 4
GDN_CHUNK = 64
SWA_H = 16
SWA_KV = 2
SWA_G = SWA_H // SWA_KV
SWA_DH = 64
SWA_QW = SWA_H * SWA_DH
SWA_KVW = SWA_KV * SWA_DH
WINDOW = 128
N_BUCKETS = 32
MAX_DISTANCE = 128
N_MEM = 256
MEM_H = 4
MEM_DH = 256
MEM_W = MEM_H * MEM_DH
NORM_EPS = 1e-6

LANE = 128
SUBLANE = 8
VMEM_LIMIT = 52 * 1024 * 1024


def _cparams(sem):
    return pltpu.CompilerParams(dimension_semantics=sem, vmem_limit_bytes=VMEM_LIMIT)


def _sigmoid(x):
    return 0.5 * jnp.tanh(0.5 * x) + 0.5


def _silu(x):
    h = 0.5 * x
    return h * jnp.tanh(h) + h


def _softplus(x):
    return jnp.maximum(x, 0.0) + jnp.log(1.0 + jnp.exp(-jnp.abs(x)))


def _dot(a, b):
    return jnp.dot(a.astype(BF16), b.astype(BF16), preferred_element_type=F32)


def _dot_nt(a, b):
    return lax.dot_general(a.astype(BF16), b.astype(BF16), (((1,), (1,)), ((), ())),
                           preferred_element_type=F32)


def _dot_tn(a, b):
    return lax.dot_general(a.astype(BF16), b.astype(BF16), (((0,), (0,)), ((), ())),
                           preferred_element_type=F32)


def _dot_f32(a, b):
    return jnp.dot(a, b, preferred_element_type=F32, precision=lax.Precision.HIGHEST)


def _rmsnorm_kernel(x_ref, nw_ref, *rest):
    x = x_ref[...].astype(F32)
    ms = jnp.mean(x * x, axis=-1, keepdims=True)
    xn = (x * lax.rsqrt(ms + NORM_EPS) * nw_ref[...]).astype(BF16)
    if len(rest) == 1:
        rest[0][...] = xn
    else:
        w_ref, o_ref, p_ref = rest
        o_ref[...] = xn
        p_ref[...] = jnp.dot(xn, w_ref[...], preferred_element_type=F32)


def _rmsnorm(x, norm_w, w=None, col0=0, n=None, tm_pref=1024):
    m, d = x.shape
    tm = min(m, tm_pref)
    assert m % tm == 0
    row = lambda i: (i, 0)
    in_specs = [pl.BlockSpec((tm, d), row), pl.BlockSpec((1, d), lambda i: (0, 0))]
    out_shape = jax.ShapeDtypeStruct((m, d), BF16)
    out_specs = pl.BlockSpec((tm, d), row)
    args = (x, norm_w)
    if w is not None:
        assert col0 % n == 0
        in_specs.append(pl.BlockSpec((d, n), lambda i: (0, col0 // n)))
        out_shape = (out_shape, jax.ShapeDtypeStruct((m, n), F32))
        out_specs = (out_specs, pl.BlockSpec((tm, n), row))
        args = args + (w,)
    return pl.pallas_call(
        _rmsnorm_kernel,
        out_shape=out_shape,
        grid=(m // tm,),
        in_specs=in_specs,
        out_specs=out_specs,
        compiler_params=_cparams(("parallel",)),
        name="rmsnorm",
    )(*args)


def _proj_kernel(x_ref, w_ref, o_ref, *, act, silu_cols, n_col_blocks):
    y = jnp.dot(x_ref[...], w_ref[...].astype(BF16), preferred_element_type=F32)
    if act == "sigmoid":
        y = _sigmoid(y)
    if not silu_cols:
        o_ref[...] = y.astype(o_ref.dtype)
        return
    tn = o_ref.shape[1]
    for jj in range(n_col_blocks):
        lo = jj * tn
        parts, pos = [], 0
        for a, b in silu_cols:
            a, b = max(a, lo) - lo, min(b, lo + tn) - lo
            if a < b:
                parts += [(pos, a, False), (a, b, True)]
                pos = b
        parts.append((pos, tn, False))

        @pl.when(pl.program_id(1) == jj)
        def _(parts=parts):
            for a, b, gate in parts:
                if a < b:
                    o_ref[:, a:b] = (_silu(y[:, a:b]) if gate else y[:, a:b]).astype(o_ref.dtype)


def _pick_tile(n, pref):
    t = min(n, pref)
    while n % t:
        t -= LANE
    return t


def _proj(xn, w, out_dtype, act=None, col0=0, n=None, silu_cols=(), tm_pref=1024, tn_pref=2048):
    m, d = xn.shape
    n = w.shape[1] if n is None else n
    tm = min(m, tm_pref)
    assert m % tm == 0
    tn = _pick_tile(n, tn_pref if out_dtype == BF16 else tn_pref // 2)
    while col0 % tn:
        tn = _pick_tile(n, tn - LANE)
    jb = col0 // tn
    return pl.pallas_call(
        functools.partial(_proj_kernel, act=act, silu_cols=tuple(silu_cols), n_col_blocks=n // tn),
        out_shape=jax.ShapeDtypeStruct((m, n), out_dtype),
        grid=(m // tm, n // tn),
        in_specs=[pl.BlockSpec((tm, d), lambda i, j: (i, 0)),
                  pl.BlockSpec((d, tn), lambda i, j: (0, jb + j))],
        out_specs=pl.BlockSpec((tm, tn), lambda i, j: (i, j)),
        compiler_params=_cparams(("parallel", "arbitrary")),
        name="proj",
    )(xn, w)


W_BLK = 1024
W_PIECE_ROWS = (W_BLK, 2 * SUBLANE)


def _wprep_kernel(wt_hbm, o_ref, inbuf0, inbuf1, sem, *, blocks):
    j = pl.program_id(0)
    inbuf = (inbuf0, inbuf1)

    def sources(b):
        return [(start, rows) for start, rows in blocks[b] if start is not None]

    def pieces(b):
        return [pltpu.make_async_copy(wt_hbm.at[pl.ds(start, rows), :],
                                      inbuf[k].at[b % 2, pl.ds(0, rows), :], sem.at[b % 2, k])
                for k, (start, rows) in enumerate(sources(b))]

    for b, ranges in enumerate(blocks):
        @pl.when(j == b)
        def _(b=b, ranges=ranges):
            if b == 0:
                for cp in pieces(0):
                    cp.start()
            if b + 1 < len(blocks):
                for cp in pieces(b + 1):
                    cp.start()
            for cp in pieces(b):
                cp.wait()
            n_have = sum(rows for _, rows in ranges)
            parts, k = [], 0
            for start, rows in tuple(ranges) + ((None, W_BLK - n_have),):
                if start is None:
                    if rows:
                        parts.append(jnp.zeros((rows, inbuf0.shape[2]), inbuf0.dtype))
                else:
                    parts.append(inbuf[k][b % 2, 0:rows, :])
                    k += 1
            val = parts[0] if len(parts) == 1 else jnp.concatenate(parts, axis=0)
            o_ref[...] = val.T.astype(BF16)


def _wprep(wt, block_srcs):
    ncol, d = wt.shape
    for ranges in block_srcs:
        srcs = [(s, rows) for s, rows in ranges if s is not None]
        assert len(srcs) <= len(W_PIECE_ROWS) and all(rows % SUBLANE == 0 for _, rows in ranges)
        for (s, rows), cap in zip(srcs, W_PIECE_ROWS):
            assert s % SUBLANE == 0 and rows <= cap and s + rows <= ncol
    return pl.pallas_call(
        functools.partial(_wprep_kernel, blocks=tuple(tuple(r) for r in block_srcs)),
        out_shape=jax.ShapeDtypeStruct((d, W_BLK * len(block_srcs)), BF16),
        grid=(len(block_srcs),),
        in_specs=[pl.BlockSpec(memory_space=pl.ANY)],
        out_specs=pl.BlockSpec((d, W_BLK), lambda j: (0, j)),
        scratch_shapes=[pltpu.VMEM((2, cap, d), wt.dtype) for cap in W_PIECE_ROWS]
                       + [pltpu.SemaphoreType.DMA((2, len(W_PIECE_ROWS)))],
        compiler_params=_cparams(("arbitrary",)),
        name="wprep",
    )(wt)


HDR = SUBLANE


def _tri_inv_many(a_list, c):
    row = lax.broadcasted_iota(jnp.int32, (c, c), 0)
    col = lax.broadcasted_iota(jnp.int32, (c, c), 1)
    eye = jnp.where(row == col, 1.0, 0.0).astype(F32)
    xs = [eye - a for a in a_list]
    bs = [_dot(a, a) for a in a_list]
    n = 2
    while n < c:
        xs = [x + _dot(x, b) for x, b in zip(xs, bs)]
        n *= 2
        if n < c:
            bs = [_dot(b, b) for b in bs]
    return xs


def _gdn_kernel(qkv_ref, z_ref, ba_ref, hdr_ref, s0_ref, cw_ref, arow_ref, dtrow_ref, nw_ref,
                o_ref, s_ref, buf_ref, cv_ref, gb_ref, oacc_ref, sol_ref, att_ref, gl_ref,
                *, chunk, n_chunks, chunks_per_seq, per_chunk_state, valid_lo, valid_hi):
    c = chunk
    tb = c * n_chunks
    cps = chunks_per_seq
    n_seqs = n_chunks // cps
    tbs = cps * c

    if per_chunk_state:
        buf_ref[0, 0:2 * HDR, :] = jnp.zeros((2 * HDR, GDN_CONV_CH), F32)
    else:
        @pl.when(pl.program_id(0) == 0)
        def _():
            for s in range(n_seqs):
                buf_ref[s, 0:HDR, :] = hdr_ref[s]
                buf_ref[s, HDR:2 * HDR, :] = jnp.zeros((HDR, GDN_CONV_CH), F32)
            s_ref[...] = s0_ref[...]

    def tap_from_history(s, j, n_rows):
        off = HDR - (CONV_W - 1) + j
        return buf_ref[s, off:off + n_rows, :] * cw_ref[j:j + 1, :]

    if qkv_ref.dtype == BF16:
        r = lax.broadcasted_iota(jnp.int32, (tbs, tbs), 0)
        cc = lax.broadcasted_iota(jnp.int32, (tbs, tbs), 1)
        for s in range(n_seqs):
            xb = qkv_ref[s]
            acc = xb.astype(F32) * cw_ref[CONV_W - 1:CONV_W, :]
            for j in range(CONV_W - 1):
                shift = jnp.where(r - cc == CONV_W - 1 - j, 1.0, 0.0).astype(BF16)
                acc = acc + jnp.dot(shift, xb, preferred_element_type=F32) * cw_ref[j:j + 1, :]
            cv_ref[s * tbs:(s + 1) * tbs, :] = _silu(acc)
            top = acc[0:HDR]
            for j in range(CONV_W - 1):
                top = top + tap_from_history(s, j, HDR)
            cv_ref[s * tbs:s * tbs + HDR, :] = _silu(top)
            buf_ref[s, 0:HDR, :] = qkv_ref[s, tbs - 2 * HDR:tbs, :].astype(F32)[HDR:]
    else:
        buf_ref[0, HDR:HDR + tb, :] = qkv_ref[...].astype(F32)
        acc = None
        for j in range(CONV_W):
            term = tap_from_history(0, j, tb)
            acc = term if acc is None else acc + term
        cv_ref[...] = _silu(acc)

    ba = ba_ref[...].astype(F32).reshape(tb, LANE)
    beta_all = _sigmoid(ba)
    g_all = -jnp.exp(arow_ref[...]) * _softplus(ba + dtrow_ref[...])
    if per_chunk_state:
        r = lax.broadcasted_iota(jnp.int32, (tb, LANE), 0) & (c - 1)
        valid = (r >= valid_lo) & (r < valid_hi)
        beta_all = jnp.where(valid, beta_all, 0.0)
        g_all = jnp.where(valid, g_all, 0.0)
    gb_ref[0] = beta_all
    gb_ref[1] = g_all

    row = lax.broadcasted_iota(jnp.int32, (c, c), 0)
    col = lax.broadcasted_iota(jnp.int32, (c, c), 1)
    causal = row >= col
    strict = row > col
    tril = jnp.where(causal, 1.0, 0.0).astype(F32)
    scale_q = GDN_DK ** -0.5

    heads = range(GDN_H)
    qcol = lambda h: slice(h * GDN_DK, (h + 1) * GDN_DK)
    kcol = lambda h: slice(GDN_QK + h * GDN_DK, GDN_QK + (h + 1) * GDN_DK)
    vcol = lambda h: slice(2 * GDN_QK + h * GDN_DV, 2 * GDN_QK + (h + 1) * GDN_DV)
    ucol = lambda h: slice(h * (GDN_DV + GDN_DK), h * (GDN_DV + GDN_DK) + GDN_DV)
    wcol = lambda h: slice(h * (GDN_DV + GDN_DK) + GDN_DV, (h + 1) * (GDN_DV + GDN_DK))
    if per_chunk_state:
        rv = lax.broadcasted_iota(jnp.int32, (c, 1), 0)
        rvalid = (rv >= valid_lo) & (rv < valid_hi)

    def chunk_rows(ci):
        return slice(ci * c, (ci + 1) * c)

    def prep(chunks):
        probs = []
        for ci in chunks:
            rows = chunk_rows(ci)
            beta_c = gb_ref[0, rows, :]
            gc_all = _dot_f32(tril, gb_ref[1, rows, :])
            gc_t = gc_all.T
            gl_ref[ci] = gc_all[c - 1:c, :]
            for h in heads:
                q = cv_ref[rows, qcol(h)]
                k = cv_ref[rows, kcol(h)]
                v = cv_ref[rows, vcol(h)]
                q = q * lax.rsqrt(jnp.sum(q * q, axis=-1, keepdims=True) + NORM_EPS) * scale_q
                k = k * lax.rsqrt(jnp.sum(k * k, axis=-1, keepdims=True) + NORM_EPS)
                if per_chunk_state:
                    q = jnp.where(rvalid, q, 0.0)
                    k = jnp.where(rvalid, k, 0.0)
                    v = jnp.where(rvalid, v, 0.0)
                beta = beta_c[:, h:h + 1]
                gc_col = gc_all[:, GDN_H + h:GDN_H + h + 1]
                gc_row = gc_t[GDN_H + h:GDN_H + h + 1, :]
                diff = jnp.where(causal, gc_col - gc_row, 0.0)
                decay = jnp.where(causal, jnp.exp(diff), 0.0)
                e_gc = jnp.exp(gc_col)
                kb = k * beta
                rhs = jnp.concatenate([v * beta, kb * e_gc], axis=1)
                cv_ref[rows, qcol(h)] = q * e_gc
                cv_ref[rows, kcol(h)] = k * jnp.exp(gc_col[c - 1:c, :] - gc_col)
                probs.append((rows, h, q, k, kb, rhs, decay))
        kq = [_dot_nt(jnp.concatenate([kb, q], axis=0), k) for (_, _, q, k, kb, _, _) in probs]
        a_list = [jnp.where(strict, kq_i[:c] * p[6], 0.0) for kq_i, p in zip(kq, probs)]
        t_inv = _tri_inv_many(a_list, c)
        for t_i, kq_i, (rows, h, _, _, _, rhs, decay) in zip(t_inv, kq, probs):
            sol_ref[rows, h * (GDN_DV + GDN_DK):(h + 1) * (GDN_DV + GDN_DK)] = _dot(t_i, rhs)
            att_ref[h, rows, :] = kq_i[c:] * decay

    def scan(chunks):
        probs = [(ci, h) for ci in chunks for h in heads]
        rows = chunk_rows
        si = lambda ci: ci // cps
        g_tot = {ci: jnp.exp(gl_ref[ci]) for ci in chunks}
        s_old = [s_ref[si(ci), h] for ci, h in probs]
        wq_s = [_dot(jnp.concatenate([sol_ref[rows(ci), wcol(h)], cv_ref[rows(ci), qcol(h)]], axis=0), s)
                for (ci, h), s in zip(probs, s_old)]
        v_new = [sol_ref[rows(ci), ucol(h)] - w[:c] for (ci, h), w in zip(probs, wq_s)]
        o_att = [_dot(att_ref[h, rows(ci), :], v) for (ci, h), v in zip(probs, v_new)]
        for (ci, h), s, v in zip(probs, s_old, v_new):
            s_ref[si(ci), h] = (s * g_tot[ci][:, GDN_H + h:GDN_H + h + 1]
                                + _dot_tn(cv_ref[rows(ci), kcol(h)], v))
        for (ci, h), w, oa in zip(probs, wq_s, o_att):
            o = w[c:] + oa
            o = o * lax.rsqrt(jnp.mean(o * o, axis=-1, keepdims=True) + NORM_EPS) * nw_ref[...]
            oacc_ref[rows(ci), h * GDN_DV:(h + 1) * GDN_DV] = o

    if per_chunk_state:
        s_ref[...] = s0_ref[...]
        prep(range(n_chunks))
        scan(range(n_chunks))
    else:
        for s in range(n_seqs):
            prep(range(s * cps, (s + 1) * cps))
        for j in range(cps):
            scan([s * cps + j for s in range(n_seqs)])
    gated = oacc_ref[...] * z_ref[...].astype(F32).reshape(tb, GDN_W)
    o_ref[...] = gated.astype(o_ref.dtype).reshape(o_ref.shape)


def _gdn(qkv, z, ba, hdr, s0, conv_w, a_row, dt_row, norm_w, *, chunk, n_chunks, chunks_per_seq,
         per_chunk_state, valid_lo, valid_hi, out_dtype, z_colblock=0, ba_colblock=0):
    tb = chunk * n_chunks
    if per_chunk_state:
        rows = qkv.shape[0]
        grid = (rows // tb,)
        row_blk = lambda width, cb: pl.BlockSpec((tb, width), lambda i: (i, cb))
        hdr_spec = pl.BlockSpec((1, HDR, GDN_CONV_CH), lambda i: (0, 0, 0))
        state_spec = pl.BlockSpec((n_chunks, GDN_H, GDN_DK, GDN_DV), lambda i: (i, 0, 0, 0))
        out_o = jax.ShapeDtypeStruct((rows, GDN_W), out_dtype)
        n_hist, hist_rows = 1, HDR + tb
    else:
        n_seq, seq_len = qkv.shape[0], qkv.shape[1]
        assert n_chunks == n_seq * chunks_per_seq and qkv.dtype == BF16
        tbs = chunk * chunks_per_seq
        grid = (seq_len // tbs,)
        row_blk = lambda width, cb: pl.BlockSpec((n_seq, tbs, width), lambda n: (0, n, cb))
        hdr_spec = pl.BlockSpec((n_seq, HDR, GDN_CONV_CH), lambda n: (0, 0, 0))
        state_spec = pl.BlockSpec((n_seq, GDN_H, GDN_DK, GDN_DV), lambda n: (0, 0, 0, 0))
        out_o = jax.ShapeDtypeStruct((n_seq, seq_len, GDN_W), out_dtype)
        n_hist, hist_rows = n_seq, 2 * HDR
    const2 = lambda *_: (0, 0)
    kern = functools.partial(_gdn_kernel, chunk=chunk, n_chunks=n_chunks, chunks_per_seq=chunks_per_seq,
                             per_chunk_state=per_chunk_state, valid_lo=valid_lo, valid_hi=valid_hi)
    return pl.pallas_call(
        kern,
        out_shape=(out_o, jax.ShapeDtypeStruct(s0.shape, F32)),
        grid=grid,
        in_specs=[row_blk(GDN_CONV_CH, 0),
                  row_blk(GDN_W, z_colblock),
                  row_blk(LANE, ba_colblock),
                  hdr_spec,
                  state_spec,
                  pl.BlockSpec((CONV_W, GDN_CONV_CH), const2),
                  pl.BlockSpec((1, LANE), const2),
                  pl.BlockSpec((1, LANE), const2),
                  pl.BlockSpec((1, GDN_DV), const2)],
        out_specs=(row_blk(GDN_W, 0), state_spec),
        scratch_shapes=[pltpu.VMEM((n_hist, hist_rows, GDN_CONV_CH), F32),
                        pltpu.VMEM((tb, GDN_CONV_CH), F32),
                        pltpu.VMEM((2, tb, LANE), F32),
                        pltpu.VMEM((tb, GDN_W), F32),
                        pltpu.VMEM((tb, GDN_H * (GDN_DV + GDN_DK)), F32),
                        pltpu.VMEM((GDN_H, tb, chunk), F32),
                        pltpu.VMEM((n_chunks, 1, LANE), F32)],
        compiler_params=_cparams(("arbitrary",)),
        name="gdn",
    )(qkv, z, ba, hdr, s0, conv_w, a_row, dt_row, norm_w)


def _t5_bucket_np(dist):
    n = np.maximum(dist, 0)
    max_exact = N_BUCKETS // 2
    nf = np.maximum(n, 1).astype(np.float32)
    large = max_exact + (np.log(nf / np.float32(max_exact)) / np.float32(math.log(MAX_DISTANCE / max_exact))
                         * np.float32(N_BUCKETS - max_exact)).astype(np.int32)
    large = np.minimum(large, N_BUCKETS - 1)
    return np.where(n < max_exact, n, large).astype(np.int32)


def _bias_prompt_kernel(code_ref, tab_ref, o_ref):
    code = code_ref[0]
    for h in range(SWA_H):
        acc = jnp.full(code.shape, -jnp.inf, F32)
        for b in range(N_BUCKETS):
            acc = jnp.where(code == b, tab_ref[b, h], acc)
        o_ref[0, h] = acc


def _bias_prompt(table):
    qi = np.arange(WINDOW)[None, :]
    sj = np.arange(WINDOW)[:, None]
    own = sj <= qi
    bucket = _t5_bucket_np(np.where(own, qi - sj, qi + WINDOW - sj))
    code_first = np.where(own, bucket, -1)
    code = jnp.asarray(np.stack([code_first, bucket]).astype(np.int32))
    return pl.pallas_call(
        _bias_prompt_kernel,
        out_shape=jax.ShapeDtypeStruct((2, SWA_H, WINDOW, WINDOW), F32),
        grid=(2,),
        in_specs=[pl.BlockSpec((1, WINDOW, WINDOW), lambda v: (v, 0, 0)),
                  pl.BlockSpec(memory_space=pltpu.SMEM)],
        out_specs=pl.BlockSpec((1, SWA_H, WINDOW, WINDOW), lambda v: (v, 0, 0, 0)),
        compiler_params=_cparams(("arbitrary",)),
        name="swa_bias_prompt",
    )(code, table)


def _bias_sample_kernel(code_ref, tab_ref, o_ref):
    kv = pl.program_id(0)
    code = code_ref[...]
    acc = jnp.full(code.shape, -jnp.inf, F32)
    for g in range(SWA_G):
        for b in range(N_BUCKETS):
            acc = jnp.where(code == b + N_BUCKETS * g, tab_ref[b, kv * SWA_G + g], acc)
    o_ref[0] = acc


def _bias_sample(table, n_tok, n_cache, n_keys_pad):
    dist = (n_cache + np.arange(n_tok))[:, None] - np.arange(n_keys_pad)[None, :]
    valid = (dist >= 0) & (dist < WINDOW) & (np.arange(n_keys_pad)[None, :] < n_cache + n_tok)
    bucket = _t5_bucket_np(dist)
    code_t = np.where(valid, bucket, -1)
    g = np.arange(SWA_G)[:, None, None]
    code = np.where(code_t[None] >= 0, code_t[None] + N_BUCKETS * g, -1)
    code = jnp.asarray(code.reshape(SWA_G * n_tok, n_keys_pad).astype(np.int32))
    return pl.pallas_call(
        _bias_sample_kernel,
        out_shape=jax.ShapeDtypeStruct((SWA_KV, SWA_G * n_tok, n_keys_pad), F32),
        grid=(SWA_KV,),
        in_specs=[pl.BlockSpec((SWA_G * n_tok, n_keys_pad), lambda k: (0, 0)),
                  pl.BlockSpec(memory_space=pltpu.SMEM)],
        out_specs=pl.BlockSpec((1, SWA_G * n_tok, n_keys_pad), lambda k: (k, 0, 0)),
        compiler_params=_cparams(("arbitrary",)),
        name="swa_bias_sample",
    )(code, table)


def _sink_softmax_pv(logits, sink, v):
    m = jnp.maximum(jnp.max(logits, axis=-1, keepdims=True), sink)
    p = jnp.exp(logits - m)
    den = jnp.sum(p, axis=-1, keepdims=True) + jnp.exp(sink - m)
    return _dot(p, v) / den


def _swa_prompt_kernel(q_ref, z_ref, kc_ref, kp_ref, vc_ref, vp_ref, bias_ref, sink_ref, o_ref, *, n_qblk):
    kall = jnp.concatenate([kp_ref[...], kc_ref[...]], axis=0)
    vall = jnp.concatenate([vp_ref[...], vc_ref[...]], axis=0)
    scale = SWA_DH ** -0.5
    key = lax.broadcasted_iota(jnp.int32, (WINDOW, WINDOW), 0)
    qry = lax.broadcasted_iota(jnp.int32, (WINDOW, WINDOW), 1)
    from_prev = key > qry
    first_variant = jnp.where(pl.program_id(1) == 0, 0, 1)
    cs = lambda h: slice(h * SWA_DH, (h + 1) * SWA_DH)
    rows = lambda qb: slice(qb * WINDOW, (qb + 1) * WINDOW)
    keys = lambda qb: slice(qb * WINDOW, (qb + 2) * WINDOW)
    for kv in range(SWA_KV):
        ks = slice(kv * SWA_DH, (kv + 1) * SWA_DH)
        k_kv = kall[:, ks].astype(BF16)
        v_t = vall[:, ks].astype(F32).T.astype(BF16)
        probs = [(qb, h) for qb in range(n_qblk) for h in range(kv * SWA_G, (kv + 1) * SWA_G)]
        lg = [_dot_nt(k_kv[keys(qb)], q_ref[rows(qb), cs(h)] * scale) for qb, h in probs]
        ps, dens = [], []
        for (qb, h), l in zip(probs, lg):
            bias = bias_ref[first_variant if qb == 0 else 1, h]
            l = jnp.where(from_prev, l[:WINDOW], l[WINDOW:]) + bias
            sink = sink_ref[h]
            m = jnp.maximum(jnp.max(l, axis=0, keepdims=True), sink)
            p = jnp.exp(l - m)
            dens.append(jnp.sum(p, axis=0, keepdims=True) + jnp.exp(sink - m))
            ps.append(jnp.concatenate([jnp.where(from_prev, p, 0.0), jnp.where(from_prev, 0.0, p)],
                                      axis=0))
        outs = [_dot(v_t[:, keys(qb)], p) / den for p, den, (qb, _) in zip(ps, dens, probs)]
        for j in range(0, len(probs), 2):
            qb, h = probs[j]
            two = slice(h * SWA_DH, (h + 2) * SWA_DH)
            o2 = jnp.concatenate([outs[j], outs[j + 1]], axis=0).T
            o_ref[rows(qb), two] = (o2 * z_ref[rows(qb), two].astype(F32)).astype(o_ref.dtype)


def _swa_prompt(proj, kv, bias, sinks, n_seq, seq_len, q_colblock, z_colblock, k_colblock, v_colblock,
                n_qblk=8):
    n_qblk = math.gcd(n_qblk, seq_len // WINDOW)
    tq = n_qblk * WINDOW
    steps = seq_len // tq
    cur = lambda b, n: b * steps + n
    prev = lambda b, n: (b * steps + n) * n_qblk - jnp.where(n == 0, 0, 1)
    return pl.pallas_call(
        functools.partial(_swa_prompt_kernel, n_qblk=n_qblk),
        out_shape=jax.ShapeDtypeStruct((n_seq * seq_len, SWA_QW), BF16),
        grid=(n_seq, steps),
        in_specs=[pl.BlockSpec((tq, SWA_QW), lambda b, n: (cur(b, n), q_colblock)),
                  pl.BlockSpec((tq, SWA_QW), lambda b, n: (cur(b, n), z_colblock)),
                  pl.BlockSpec((tq, SWA_KVW), lambda b, n: (cur(b, n), k_colblock)),
                  pl.BlockSpec((WINDOW, SWA_KVW), lambda b, n: (prev(b, n), k_colblock)),
                  pl.BlockSpec((tq, SWA_KVW), lambda b, n: (cur(b, n), v_colblock)),
                  pl.BlockSpec((WINDOW, SWA_KVW), lambda b, n: (prev(b, n), v_colblock)),
                  pl.BlockSpec((2, SWA_H, WINDOW, WINDOW), lambda b, n: (0, 0, 0, 0)),
                  pl.BlockSpec(memory_space=pltpu.SMEM)],
        out_specs=pl.BlockSpec((tq, SWA_QW), lambda b, n: (cur(b, n), 0)),
        compiler_params=_cparams(("parallel", "arbitrary")),
        name="swa_prompt",
    )(proj, proj, kv, kv, kv, kv, bias, sinks)


def _swa_sample_kernel(q_ref, z_ref, kn_ref, vn_ref, ck_ref, cv_ref, bias_ref, sink_ref, o_ref, nk_ref, nv_ref,
                       *, n_seq_blk, n_new):
    scale = SWA_DH ** -0.5
    wb = ck_ref.shape[3]
    probs = [(s, kv) for s in range(n_seq_blk) for kv in range(SWA_KV)]
    ks = lambda kv: slice(kv * SWA_DH, (kv + 1) * SWA_DH)
    slot = lax.broadcasted_iota(jnp.int32, (SWA_DH, wb), 1)
    tok = lax.broadcasted_iota(jnp.int32, (kn_ref.shape[1], wb), 0)
    tok_slot = lax.broadcasted_iota(jnp.int32, (kn_ref.shape[1], wb), 1)
    place = jnp.where((tok_slot == tok + (wb - n_new)) & (tok < n_new), 1.0, 0.0).astype(F32)
    for s in range(n_seq_blk):
        for cache_ref, new_ref, out_ref in ((ck_ref, kn_ref, nk_ref), (cv_ref, vn_ref, nv_ref)):
            new_t = lax.dot_general(new_ref[s], place, (((0,), (0,)), ((), ())),
                                    preferred_element_type=F32, precision=lax.Precision.HIGHEST)
            for kv in range(SWA_KV):
                out_ref[s, kv] = jnp.where(slot >= wb - n_new, new_t[ks(kv), :],
                                           pltpu.roll(cache_ref[s, kv], wb - n_new, axis=1))
    lc = [_dot(q_ref[s, kv] * scale, ck_ref[s, kv]) + bias_ref[kv, :, 0:wb] for s, kv in probs]
    ln = [_dot_nt(q_ref[s, kv] * scale, kn_ref[s, :, ks(kv)]) + bias_ref[kv, :, wb:] for s, kv in probs]
    pcs, pns, dens = [], [], []
    for (s, kv), c, n in zip(probs, lc, ln):
        sink = sink_ref[kv][:, 0:1]
        m = jnp.maximum(jnp.maximum(jnp.max(c, axis=-1, keepdims=True), jnp.max(n, axis=-1, keepdims=True)),
                        sink)
        pc = jnp.exp(c - m)
        pn = jnp.exp(n - m)
        pcs.append(pc)
        pns.append(pn)
        dens.append(jnp.sum(pc, axis=-1, keepdims=True) + jnp.sum(pn, axis=-1, keepdims=True)
                    + jnp.exp(sink - m))
    outs = [(_dot_nt(pc, cv_ref[s, kv]) + _dot(pn, vn_ref[s, :, ks(kv)])) / den
            for (s, kv), pc, pn, den in zip(probs, pcs, pns, dens)]
    for oh, (s, kv) in zip(outs, probs):
        o_ref[s, kv] = (oh * z_ref[s, kv].astype(F32)).astype(o_ref.dtype)


def _swa_sample(q, z, k_new, v_new, cache_kt, cache_vt, bias, sink_rows, n_new, n_seq_blk=8):
    bd, _, rows, _ = q.shape
    wb = cache_kt.shape[3]
    npad = k_new.shape[1]
    assert wb == LANE
    blk4 = lambda i: (i, 0, 0, 0)
    blk3 = lambda i: (i, 0, 0)
    cache_spec = pl.BlockSpec((n_seq_blk, SWA_KV, SWA_DH, wb), blk4)
    return pl.pallas_call(
        functools.partial(_swa_sample_kernel, n_seq_blk=n_seq_blk, n_new=n_new),
        out_shape=(jax.ShapeDtypeStruct(q.shape, BF16),
                   jax.ShapeDtypeStruct(cache_kt.shape, F32), jax.ShapeDtypeStruct(cache_vt.shape, F32)),
        grid=(bd // n_seq_blk,),
        in_specs=[pl.BlockSpec((n_seq_blk, SWA_KV, rows, SWA_DH), blk4),
                  pl.BlockSpec((n_seq_blk, SWA_KV, rows, SWA_DH), blk4),
                  pl.BlockSpec((n_seq_blk, npad, SWA_KVW), blk3),
                  pl.BlockSpec((n_seq_blk, npad, SWA_KVW), blk3),
                  pl.BlockSpec((n_seq_blk, SWA_KV, SWA_DH, wb), blk4),
                  pl.BlockSpec((n_seq_blk, SWA_KV, SWA_DH, wb), blk4),
                  pl.BlockSpec((SWA_KV, rows, wb + npad), lambda i: (0, 0, 0)),
                  pl.BlockSpec((SWA_KV, rows, LANE), lambda i: (0, 0, 0))],
        out_specs=(pl.BlockSpec((n_seq_blk, SWA_KV, rows, SWA_DH), blk4), cache_spec, cache_spec),
        compiler_params=_cparams(("arbitrary",)),
        name="swa_sample",
    )(q, z, k_new, v_new, cache_kt, cache_vt, bias, sink_rows)


def _mem_attend(probs, q_of, z_of, k_of, v_of, store):
    scale = MEM_DH ** -0.5
    logits = [_dot_nt(q_of(p) * scale, k_of(p)) for p in probs]
    ps, dens = [], []
    for l in logits:
        m = jnp.max(l, axis=-1, keepdims=True)
        e = jnp.exp(l - m)
        dens.append(jnp.sum(e, axis=-1, keepdims=True))
        ps.append(e)
    outs = [_dot(e, v_of(p)) / den for e, den, p in zip(ps, dens, probs)]
    for p, oh in zip(probs, outs):
        store(p, oh * z_of(p).astype(F32))


def _mem_cols(h):
    return slice(h * MEM_DH, (h + 1) * MEM_DH)


def _mem_prompt_kernel(q_ref, z_ref, k_ref, v_ref, o_ref):
    def store(h, val):
        o_ref[:, _mem_cols(h)] = val.astype(o_ref.dtype)
    _mem_attend(range(MEM_H), lambda h: q_ref[:, _mem_cols(h)], lambda h: z_ref[:, _mem_cols(h)],
                lambda h: k_ref[:, _mem_cols(h)], lambda h: v_ref[:, _mem_cols(h)], store)


def _mem_prompt(proj, mkv, n_seq, seq_len, q_colblock, z_colblock, tq=2048):
    tq = math.gcd(tq, seq_len)
    steps = seq_len // tq
    return pl.pallas_call(
        _mem_prompt_kernel,
        out_shape=jax.ShapeDtypeStruct((n_seq * seq_len, MEM_W), BF16),
        grid=(n_seq, steps),
        in_specs=[pl.BlockSpec((tq, MEM_W), lambda b, n: (b * steps + n, q_colblock)),
                  pl.BlockSpec((tq, MEM_W), lambda b, n: (b * steps + n, z_colblock)),
                  pl.BlockSpec((N_MEM, MEM_W), lambda b, n: (b, 0)),
                  pl.BlockSpec((N_MEM, MEM_W), lambda b, n: (b, 1))],
        out_specs=pl.BlockSpec((tq, MEM_W), lambda b, n: (b * steps + n, 0)),
        compiler_params=_cparams(("parallel", "arbitrary")),
        name="mem_prompt",
    )(proj, proj, mkv, mkv)


def _mem_sample_kernel(q_ref, z_ref, k_hbm, v_hbm, o_ref, kbuf, vbuf, sem, *, n_seq_blk):
    i = pl.program_id(0)
    n_steps = pl.num_programs(0)
    slot = i % 2

    def copies(step, slot_):
        seqs = pl.ds(step * n_seq_blk, n_seq_blk)
        out = []
        for h in range(MEM_H):
            out.append(pltpu.make_async_copy(k_hbm.at[seqs, :, h, :], kbuf.at[slot_, h], sem.at[0, slot_, h]))
            out.append(pltpu.make_async_copy(v_hbm.at[seqs, :, h, :], vbuf.at[slot_, h], sem.at[1, slot_, h]))
        return out

    @pl.when(i == 0)
    def _():
        for cp in copies(0, 0):
            cp.start()

    @pl.when(i + 1 < n_steps)
    def _():
        for cp in copies(i + 1, 1 - slot):
            cp.start()

    for cp in copies(i, slot):
        cp.wait()

    def store(p, val):
        o_ref[p[0], :, _mem_cols(p[1])] = val.astype(o_ref.dtype)
    probs = [(s, h) for s in range(n_seq_blk) for h in range(MEM_H)]
    _mem_attend(probs, lambda p: q_ref[p[0], :, _mem_cols(p[1])], lambda p: z_ref[p[0], :, _mem_cols(p[1])],
                lambda p: kbuf[slot, p[1], p[0]], lambda p: vbuf[slot, p[1], p[0]], store)


def _mem_sample(q, z, cache_k, cache_v, n_seq_blk=8):
    bd, rows, _ = q.shape
    blk = lambda i: (i, 0, 0)
    buf = pltpu.VMEM((2, MEM_H, n_seq_blk, N_MEM, MEM_DH), cache_k.dtype)
    return pl.pallas_call(
        functools.partial(_mem_sample_kernel, n_seq_blk=n_seq_blk),
        out_shape=jax.ShapeDtypeStruct(q.shape, BF16),
        grid=(bd // n_seq_blk,),
        in_specs=[pl.BlockSpec((n_seq_blk, rows, MEM_W), blk),
                  pl.BlockSpec((n_seq_blk, rows, MEM_W), blk),
                  pl.BlockSpec(memory_space=pl.ANY),
                  pl.BlockSpec(memory_space=pl.ANY)],
        out_specs=pl.BlockSpec((n_seq_blk, rows, MEM_W), blk),
        scratch_shapes=[buf, buf, pltpu.SemaphoreType.DMA((2, 2, MEM_H))],
        compiler_params=_cparams(("arbitrary",)),
        name="mem_sample",
    )(q, z, cache_k, cache_v)


def _merge_kernel(og_ref, os_ref, om_ref, gate_ref, x_ref, wb_ref, wo_ref, nf_ref, y_ref):
    merged = None
    for b, o_ref in enumerate((og_ref, os_ref, om_ref)):
        t = jnp.dot(o_ref[...], wb_ref[b], preferred_element_type=F32)
        t = t * gate_ref[:, b * D_MODEL:(b + 1) * D_MODEL].astype(F32)
        merged = t if merged is None else merged + t
    h = x_ref[...] + jnp.dot(merged.astype(BF16), wo_ref[...], preferred_element_type=F32)
    ms = jnp.mean(h * h, axis=-1, keepdims=True)
    y_ref[...] = h * lax.rsqrt(ms + NORM_EPS) * nf_ref[...]


def _merge(o_gdn, o_swa, o_mem, gates, x, w_branch, w_out, norm_f, tm=256):
    m = x.shape[0]
    tm = min(tm, m)
    row = lambda i: (i, 0)
    const2 = lambda i: (0, 0)
    return pl.pallas_call(
        _merge_kernel,
        out_shape=jax.ShapeDtypeStruct((m, D_MODEL), F32),
        grid=(m // tm,),
        in_specs=[pl.BlockSpec((tm, BR_W), row),
                  pl.BlockSpec((tm, BR_W), row),
                  pl.BlockSpec((tm, BR_W), row),
                  pl.BlockSpec((tm, N_BRANCH * D_MODEL), row),
                  pl.BlockSpec((tm, D_MODEL), row),
                  pl.BlockSpec((N_BRANCH, BR_W, D_MODEL), lambda i: (0, 0, 0),
                               pipeline_mode=pl.Buffered(1)),
                  pl.BlockSpec((D_MODEL, D_MODEL), const2, pipeline_mode=pl.Buffered(1)),
                  pl.BlockSpec((1, D_MODEL), const2)],
        out_specs=pl.BlockSpec((tm, D_MODEL), row),
        compiler_params=_cparams(("parallel",)),
        name="merge",
    )(o_gdn, o_swa, o_mem, gates, x, w_branch, w_out, norm_f)


_IN_SIZES = (GDN_QK, GDN_QK, GDN_W, GDN_W, GDN_H, GDN_H, SWA_QW, SWA_KVW, SWA_KVW, SWA_QW,
             MEM_W, MEM_W, N_BRANCH * D_MODEL)
_IN_NAMES = ("gq", "gk", "gv", "gz", "gb", "ga", "sq", "sk", "sv", "sz", "mq", "mz", "mg")
_IN_SPAN = {name: (int(off), int(off + size)) for name, off, size in
            zip(_IN_NAMES, np.cumsum((0,) + _IN_SIZES[:-1]), _IN_SIZES)}


def kernel(x_prompt, x_sample, state_gdn, state_gdn_conv, cache_swa_k, cache_swa_v, cache_mem_k,
           cache_mem_v, mem_prompt, norm_in, w_in, gdn_conv_w, gdn_a_log, gdn_dt_bias, gdn_norm,
           swa_sinks, rel_bias, norm_mem, w_mem_kv, w_branch, w_out, norm_f):
    n_layers = norm_in.shape[0]
    assert n_layers == 1
    b, seq, _ = x_prompt.shape
    bd, ns, _ = x_sample.shape
    wb = cache_swa_k.shape[2]
    assert seq % WINDOW == 0 and seq % GDN_CHUNK == 0 and ns + CONV_W <= SUBLANE and wb == WINDOW
    lyr = 0

    w = w_in[lyr]
    main_names = ("gq", "gk", "gv", "gz", "sq", "sz", "mq", "mz")
    n_main = len(main_names) * W_BLK
    n_gate = N_BRANCH * D_MODEL
    n_small = 3 * LANE
    small_lead = -(n_main + n_gate) % n_small
    w_all = _wprep(w.T, [[(_IN_SPAN[a][0], W_BLK)] for a in main_names]
                   + [[(_IN_SPAN["mg"][0] + W_BLK * k, W_BLK)] for k in range(n_gate // W_BLK)]
                   + [[(None, small_lead), (_IN_SPAN["sk"][0], 2 * SWA_KVW), (_IN_SPAN["gb"][0], 2 * GDN_H)]])
    col_small = n_main + n_gate + small_lead
    cb_gz, cb_sq, cb_sz, cb_mq, cb_mz = 3, 4, 5, 6, 7
    gate_cols = tuple((cb * W_BLK, (cb + 1) * W_BLK) for cb in (cb_gz, cb_sz, cb_mz))
    cb_sk, cb_sv, cb_ba = 0, 1, 2
    w_mkv = w_mem_kv[lyr]
    w_br = w_branch[lyr].astype(BF16)
    w_o = w_out[lyr].astype(BF16)
    nw_in = norm_in[lyr].reshape(1, D_MODEL)
    nw_mem = norm_mem[lyr].reshape(1, D_MODEL)
    nw_f = norm_f.reshape(1, D_MODEL)
    conv_w = gdn_conv_w[lyr]
    a_row = jnp.pad(gdn_a_log[lyr].reshape(1, GDN_H), ((0, 0), (GDN_H, LANE - 2 * GDN_H)))
    dt_row = jnp.pad(gdn_dt_bias[lyr].reshape(1, GDN_H), ((0, 0), (GDN_H, LANE - 2 * GDN_H)))
    gnw = gdn_norm[lyr].reshape(1, GDN_DV)
    sinks = swa_sinks[lyr]
    bias_p = _bias_prompt(rel_bias)
    npad = SUBLANE
    bias_s = _bias_sample(rel_bias, ns, wb, wb + npad)
    sink_rows = jnp.broadcast_to(jnp.repeat(sinks.reshape(SWA_KV, SWA_G), ns, axis=1)[:, :, None],
                                 (SWA_KV, SWA_G * ns, LANE))

    t = b * seq
    xp = x_prompt.reshape(t, D_MODEL)
    xn_p, p_small = _rmsnorm(xp, nw_in, w_all, col_small, n_small)
    p_main = _proj(xn_p, w_all, BF16, n=n_main, silu_cols=gate_cols)
    g_p = _proj(xn_p, w_all, BF16, act="sigmoid", col0=n_main, n=n_gate)
    xn_tail = xn_p.reshape(b, seq, D_MODEL)[:, seq - SUBLANE:, :].reshape(b * SUBLANE, D_MODEL)
    conv_p = _proj(xn_tail, w_all, F32, n=GDN_CONV_CH).reshape(b, SUBLANE, GDN_CONV_CH)[:, SUBLANE - (CONV_W - 1):]
    kv_tail = p_small.reshape(b, seq, 3 * LANE)[:, seq - WINDOW:]
    swk_p = kv_tail[:, :, cb_sk * LANE:(cb_sk + 1) * LANE].reshape(b, WINDOW, SWA_KV, SWA_DH)
    swv_p = kv_tail[:, :, cb_sv * LANE:(cb_sv + 1) * LANE].reshape(b, WINDOW, SWA_KV, SWA_DH)

    mkv = _proj(_rmsnorm(mem_prompt.reshape(b * N_MEM, D_MODEL), nw_mem), w_mkv, F32)
    mk_p = mkv[:, :MEM_W].reshape(b, N_MEM, MEM_W)
    mv_p = mkv[:, MEM_W:].reshape(b, N_MEM, MEM_W)

    p_main3 = p_main.reshape(b, seq, n_main)
    gdn_cps = 4
    o_gdn_p, s_p = _gdn(p_main3, p_main3, p_small.reshape(b, seq, n_small),
                        jnp.zeros((b, HDR, GDN_CONV_CH), F32),
                        jnp.zeros((b, GDN_H, GDN_DK, GDN_DV), F32), conv_w, a_row, dt_row, gnw,
                        chunk=GDN_CHUNK, n_chunks=gdn_cps * b, chunks_per_seq=gdn_cps,
                        per_chunk_state=False, valid_lo=0, valid_hi=GDN_CHUNK, out_dtype=BF16,
                        z_colblock=cb_gz, ba_colblock=cb_ba)
    o_gdn_p = o_gdn_p.reshape(t, GDN_W)
    o_swa_p = _swa_prompt(p_main, p_small, bias_p, sinks, b, seq, cb_sq, cb_sz, cb_sk, cb_sv)
    o_mem_p = _mem_prompt(p_main, mkv, b, seq, cb_mq, cb_mz)
    y_p = _merge(o_gdn_p, o_swa_p, o_mem_p, g_p, xp, w_br, w_o, nw_f).reshape(b, seq, D_MODEL)

    ts = bd * ns
    xs = x_sample.reshape(ts, D_MODEL)
    xn_s, s_small = _rmsnorm(xs, nw_in, w_all, col_small, n_small)
    s_main = _proj(xn_s, w_all, F32, n=n_main, silu_cols=gate_cols)
    g_s = _proj(xn_s, w_all, BF16, act="sigmoid", col0=n_main, n=n_gate)
    s_gdn = s_main[:, :GDN_CONV_CH + GDN_W]
    s_ba = s_small[:, cb_ba * LANE:(cb_ba + 1) * LANE]
    s_swa = jnp.concatenate([s_main[:, cb_sq * BR_W:(cb_sz + 1) * BR_W], s_small[:, :2 * LANE]], axis=1)
    s_mem = s_main[:, cb_mq * BR_W:(cb_mz + 1) * BR_W].astype(BF16)

    lo = CONV_W - 1
    hi = lo + ns
    pad_rows = ((0, 0), (lo, SUBLANE - hi), (0, 0))
    e_qkv = jnp.concatenate([state_gdn_conv[lyr], s_gdn[:, :GDN_CONV_CH].reshape(bd, ns, GDN_CONV_CH),
                             jnp.zeros((bd, SUBLANE - hi, GDN_CONV_CH), F32)], axis=1)
    e_z = jnp.pad(s_gdn[:, GDN_CONV_CH:].reshape(bd, ns, GDN_W), pad_rows)
    e_ba = jnp.pad(s_ba.reshape(bd, ns, LANE), pad_rows)
    seq_blk = 16
    o_gdn_s8, s_s = _gdn(e_qkv.reshape(bd * SUBLANE, GDN_CONV_CH), e_z.reshape(bd * SUBLANE, GDN_W),
                         e_ba.reshape(bd * SUBLANE, LANE), jnp.zeros((1, HDR, GDN_CONV_CH), F32),
                         state_gdn[lyr], conv_w, a_row, dt_row, gnw,
                         chunk=SUBLANE, n_chunks=seq_blk, chunks_per_seq=1, per_chunk_state=True,
                         valid_lo=lo, valid_hi=hi, out_dtype=BF16)
    o_gdn_s = o_gdn_s8.reshape(bd, SUBLANE, GDN_W)[:, lo:hi].reshape(ts, GDN_W)
    conv_s = e_qkv[:, hi - (CONV_W - 1):hi]

    def to_heads(a):
        return a.reshape(bd, ns, SWA_KV, SWA_G, SWA_DH).transpose(0, 2, 3, 1, 4).reshape(
            bd, SWA_KV, SWA_G * ns, SWA_DH)

    k_new = s_swa[:, 2 * SWA_QW:2 * SWA_QW + SWA_KVW].reshape(bd, ns, SWA_KVW)
    v_new = s_swa[:, 2 * SWA_QW + SWA_KVW:].reshape(bd, ns, SWA_KVW)
    tok_pad = ((0, 0), (0, npad - ns), (0, 0))
    ck_t = cache_swa_k[lyr].transpose(0, 2, 3, 1)
    cv_t = cache_swa_v[lyr].transpose(0, 2, 3, 1)
    o_swa_h, nk_t, nv_t = _swa_sample(
        to_heads(s_swa[:, :SWA_QW]).astype(BF16), to_heads(s_swa[:, SWA_QW:2 * SWA_QW]),
        jnp.pad(k_new, tok_pad), jnp.pad(v_new, tok_pad), ck_t, cv_t, bias_s, sink_rows, n_new=ns)
    o_swa_s = o_swa_h.reshape(bd, SWA_KV, SWA_G, ns, SWA_DH).transpose(0, 3, 1, 2, 4).reshape(ts, SWA_QW)
    swk_s = nk_t.transpose(0, 3, 1, 2)
    swv_s = nv_t.transpose(0, 3, 1, 2)

    mq = jnp.pad(s_mem[:, :MEM_W].reshape(bd, ns, MEM_W), tok_pad)
    mz = jnp.pad(s_mem[:, MEM_W:].reshape(bd, ns, MEM_W), tok_pad)
    o_mem_s = _mem_sample(mq, mz, cache_mem_k[lyr], cache_mem_v[lyr])[:, :ns].reshape(ts, MEM_W)
    y_s = _merge(o_gdn_s, o_swa_s, o_mem_s, g_s, xs, w_br, w_o, nw_f).reshape(bd, ns, D_MODEL)

    return (y_p, y_s,
            s_p[None], conv_p[None], swk_p[None], swv_p[None],
            mk_p.reshape(b, N_MEM, MEM_H, MEM_DH)[None], mv_p.reshape(b, N_MEM, MEM_H, MEM_DH)[None],
            s_s[None], conv_s[None], swk_s[None], swv_s[None])
```

```python
import functools
import math

import numpy as np
import jax
import jax.numpy as jnp
from jax import lax
from jax.experimental import pallas as pl
from jax.experimental.pallas import tpu as pltpu

F32 = jnp.float32
BF16 = jnp.bfloat16

D_MODEL = 2048
N_BRANCH = 3
BR_W = 1024
GDN_H = 8
GDN_DK = 128
GDN_DV = 128
GDN_QK = GDN_H * GDN_DK
GDN_W = GDN_H * GDN_DV
GDN_CONV_CH = 2 * GDN_QK + GDN_W
CONV_W = 4
GDN_CHUNK = 64
SWA_H = 16
SWA_KV = 2
SWA_G = SWA_H // SWA_KV
SWA_DH = 64
SWA_QW = SWA_H * SWA_DH
SWA_KVW = SWA_KV * SWA_DH
WINDOW = 128
N_BUCKETS = 32
MAX_DISTANCE = 128
N_MEM = 256
MEM_H = 4
MEM_DH = 256
MEM_W = MEM_H * MEM_DH
NORM_EPS = 1e-6

LANE = 128
SUBLANE = 8
VMEM_LIMIT = 52 * 1024 * 1024


def _cparams(sem):
    return pltpu.CompilerParams(dimension_semantics=sem, vmem_limit_bytes=VMEM_LIMIT)


def _sigmoid(x):
    return 0.5 * jnp.tanh(0.5 * x) + 0.5


def _silu(x):
    h = 0.5 * x
    return h * jnp.tanh(h) + h


def _softplus(x):
    return jnp.maximum(x, 0.0) + jnp.log(1.0 + jnp.exp(-jnp.abs(x)))


def _dot(a, b):
    return jnp.dot(a.astype(BF16), b.astype(BF16), preferred_element_type=F32)


def _dot_nt(a, b):
    return lax.dot_general(a.astype(BF16), b.astype(BF16), (((1,), (1,)), ((), ())),
                           preferred_element_type=F32)


def _dot_tn(a, b):
    return lax.dot_general(a.astype(BF16), b.astype(BF16), (((0,), (0,)), ((), ())),
                           preferred_element_type=F32)


def _dot_f32(a, b):
    return jnp.dot(a, b, preferred_element_type=F32, precision=lax.Precision.HIGHEST)


def _rmsnorm_kernel(x_ref, nw_ref, *rest):
    x = x_ref[...].astype(F32)
    ms = jnp.mean(x * x, axis=-1, keepdims=True)
    xn = (x * lax.rsqrt(ms + NORM_EPS) * nw_ref[...]).astype(BF16)
    if len(rest) == 1:
        rest[0][...] = xn
    else:
        w_ref, o_ref, p_ref = rest
        o_ref[...] = xn
        p_ref[...] = jnp.dot(xn, w_ref[...], preferred_element_type=F32)


def _rmsnorm(x, norm_w, w=None, col0=0, n=None, tm_pref=1024):
    m, d = x.shape
    tm = min(m, tm_pref)
    assert m % tm == 0
    row = lambda i: (i, 0)
    in_specs = [pl.BlockSpec((tm, d), row), pl.BlockSpec((1, d), lambda i: (0, 0))]
    out_shape = jax.ShapeDtypeStruct((m, d), BF16)
    out_specs = pl.BlockSpec((tm, d), row)
    args = (x, norm_w)
    if w is not None:
        assert col0 % n == 0
        in_specs.append(pl.BlockSpec((d, n), lambda i: (0, col0 // n)))
        out_shape = (out_shape, jax.ShapeDtypeStruct((m, n), F32))
        out_specs = (out_specs, pl.BlockSpec((tm, n), row))
        args = args + (w,)
    return pl.pallas_call(
        _rmsnorm_kernel,
        out_shape=out_shape,
        grid=(m // tm,),
        in_specs=in_specs,
        out_specs=out_specs,
        compiler_params=_cparams(("parallel",)),
        name="rmsnorm",
    )(*args)


def _proj_kernel(x_ref, w_ref, o_ref, *, act):
    y = jnp.dot(x_ref[...], w_ref[...].astype(BF16), preferred_element_type=F32)
    if act == "sigmoid":
        y = _sigmoid(y)
    o_ref[...] = y.astype(o_ref.dtype)


def _pick_tile(n, pref):
    t = min(n, pref)
    while n % t:
        t -= LANE
    return t


def _proj(xn, w, out_dtype, act=None, col0=0, n=None, tm_pref=1024, tn_pref=2048):
    m, d = xn.shape
    n = w.shape[1] if n is None else n
    tm = min(m, tm_pref)
    assert m % tm == 0
    tn = _pick_tile(n, tn_pref if out_dtype == BF16 else tn_pref // 2)
    while col0 % tn:
        tn = _pick_tile(n, tn - LANE)
    jb = col0 // tn
    return pl.pallas_call(
        functools.partial(_proj_kernel, act=act),
        out_shape=jax.ShapeDtypeStruct((m, n), out_dtype),
        grid=(m // tm, n // tn),
        in_specs=[pl.BlockSpec((tm, d), lambda i, j: (i, 0)),
                  pl.BlockSpec((d, tn), lambda i, j: (0, jb + j))],
        out_specs=pl.BlockSpec((tm, tn), lambda i, j: (i, j)),
        compiler_params=_cparams(("parallel", "arbitrary")),
        name="proj",
    )(xn, w)


W_BLK = 1024
W_PIECE_ROWS = (W_BLK, 2 * SUBLANE)


def _wprep_kernel(wt_hbm, o_ref, inbuf0, inbuf1, sem, *, blocks):
    j = pl.program_id(0)
    inbuf = (inbuf0, inbuf1)

    def sources(b):
        return [(start, rows) for start, rows in blocks[b] if start is not None]

    def pieces(b):
        return [pltpu.make_async_copy(wt_hbm.at[pl.ds(start, rows), :],
                                      inbuf[k].at[b % 2, pl.ds(0, rows), :], sem.at[b % 2, k])
                for k, (start, rows) in enumerate(sources(b))]

    for b, ranges in enumerate(blocks):
        @pl.when(j == b)
        def _(b=b, ranges=ranges):
            if b == 0:
                for cp in pieces(0):
                    cp.start()
            if b + 1 < len(blocks):
                for cp in pieces(b + 1):
                    cp.start()
            for cp in pieces(b):
                cp.wait()
            n_have = sum(rows for _, rows in ranges)
            parts, k = [], 0
            for start, rows in tuple(ranges) + ((None, W_BLK - n_have),):
                if start is None:
                    if rows:
                        parts.append(jnp.zeros((rows, inbuf0.shape[2]), inbuf0.dtype))
                else:
                    parts.append(inbuf[k][b % 2, 0:rows, :])
                    k += 1
            val = parts[0] if len(parts) == 1 else jnp.concatenate(parts, axis=0)
            o_ref[...] = val.T.astype(BF16)


def _wprep(wt, block_srcs):
    ncol, d = wt.shape
    for ranges in block_srcs:
        srcs = [(s, rows) for s, rows in ranges if s is not None]
        assert len(srcs) <= len(W_PIECE_ROWS) and all(rows % SUBLANE == 0 for _, rows in ranges)
        for (s, rows), cap in zip(srcs, W_PIECE_ROWS):
            assert s % SUBLANE == 0 and rows <= cap and s + rows <= ncol
    return pl.pallas_call(
        functools.partial(_wprep_kernel, blocks=tuple(tuple(r) for r in block_srcs)),
        out_shape=jax.ShapeDtypeStruct((d, W_BLK * len(block_srcs)), BF16),
        grid=(len(block_srcs),),
        in_specs=[pl.BlockSpec(memory_space=pl.ANY)],
        out_specs=pl.BlockSpec((d, W_BLK), lambda j: (0, j)),
        scratch_shapes=[pltpu.VMEM((2, cap, d), wt.dtype) for cap in W_PIECE_ROWS]
                       + [pltpu.SemaphoreType.DMA((2, len(W_PIECE_ROWS)))],
        compiler_params=_cparams(("arbitrary",)),
        name="wprep",
    )(wt)


HDR = SUBLANE


def _tri_inv_many(a_list, c):
    row = lax.broadcasted_iota(jnp.int32, (c, c), 0)
    col = lax.broadcasted_iota(jnp.int32, (c, c), 1)
    eye = jnp.where(row == col, 1.0, 0.0).astype(F32)
    xs = [eye - a for a in a_list]
    bs = [_dot(a, a) for a in a_list]
    n = 2
    while n < c:
        xs = [x + _dot(x, b) for x, b in zip(xs, bs)]
        n *= 2
        if n < c:
            bs = [_dot(b, b) for b in bs]
    return xs


def _gdn_kernel(qkv_ref, z_ref, ba_ref, hdr_ref, s0_ref, cw_ref, arow_ref, dtrow_ref, nw_ref,
                o_ref, s_ref, buf_ref, cv_ref, gb_ref, oacc_ref, sol_ref, att_ref, gl_ref,
                *, chunk, n_chunks, chunks_per_seq, per_chunk_state, valid_lo, valid_hi):
    c = chunk
    tb = c * n_chunks
    cps = chunks_per_seq
    n_seqs = n_chunks // cps
    tbs = cps * c

    if per_chunk_state:
        buf_ref[0, 0:2 * HDR, :] = jnp.zeros((2 * HDR, GDN_CONV_CH), F32)
    else:
        @pl.when(pl.program_id(0) == 0)
        def _():
            for s in range(n_seqs):
                buf_ref[s, 0:HDR, :] = hdr_ref[s]
                buf_ref[s, HDR:2 * HDR, :] = jnp.zeros((HDR, GDN_CONV_CH), F32)
            s_ref[...] = s0_ref[...]

    def tap_from_history(s, j, n_rows):
        off = HDR - (CONV_W - 1) + j
        return buf_ref[s, off:off + n_rows, :] * cw_ref[j:j + 1, :]

    if qkv_ref.dtype == BF16:
        r = lax.broadcasted_iota(jnp.int32, (tbs, tbs), 0)
        cc = lax.broadcasted_iota(jnp.int32, (tbs, tbs), 1)
        for s in range(n_seqs):
            xb = qkv_ref[s]
            acc = xb.astype(F32) * cw_ref[CONV_W - 1:CONV_W, :]
            for j in range(CONV_W - 1):
                shift = jnp.where(r - cc == CONV_W - 1 - j, 1.0, 0.0).astype(BF16)
                acc = acc + jnp.dot(shift, xb, preferred_element_type=F32) * cw_ref[j:j + 1, :]
            cv_ref[s * tbs:(s + 1) * tbs, :] = _silu(acc)
            top = acc[0:HDR]
            for j in range(CONV_W - 1):
                top = top + tap_from_history(s, j, HDR)
            cv_ref[s * tbs:s * tbs + HDR, :] = _silu(top)
            buf_ref[s, 0:HDR, :] = qkv_ref[s, tbs - 2 * HDR:tbs, :].astype(F32)[HDR:]
    else:
        buf_ref[0, HDR:HDR + tb, :] = qkv_ref[...].astype(F32)
        acc = None
        for j in range(CONV_W):
            term = tap_from_history(0, j, tb)
            acc = term if acc is None else acc + term
        cv_ref[...] = _silu(acc)

    ba = ba_ref[...].astype(F32).reshape(tb, LANE)
    beta_all = _sigmoid(ba)
    g_all = -jnp.exp(arow_ref[...]) * _softplus(ba + dtrow_ref[...])
    if per_chunk_state:
        r = lax.broadcasted_iota(jnp.int32, (tb, LANE), 0) & (c - 1)
        valid = (r >= valid_lo) & (r < valid_hi)
        beta_all = jnp.where(valid, beta_all, 0.0)
        g_all = jnp.where(valid, g_all, 0.0)
    gb_ref[0] = beta_all
    gb_ref[1] = g_all

    row = lax.broadcasted_iota(jnp.int32, (c, c), 0)
    col = lax.broadcasted_iota(jnp.int32, (c, c), 1)
    causal = row >= col
    strict = row > col
    tril = jnp.where(causal, 1.0, 0.0).astype(F32)
    scale_q = GDN_DK ** -0.5

    heads = range(GDN_H)
    qcol = lambda h: slice(h * GDN_DK, (h + 1) * GDN_DK)
    kcol = lambda h: slice(GDN_QK + h * GDN_DK, GDN_QK + (h + 1) * GDN_DK)
    vcol = lambda h: slice(2 * GDN_QK + h * GDN_DV, 2 * GDN_QK + (h + 1) * GDN_DV)
    ucol = lambda h: slice(h * (GDN_DV + GDN_DK), h * (GDN_DV + GDN_DK) + GDN_DV)
    wcol = lambda h: slice(h * (GDN_DV + GDN_DK) + GDN_DV, (h + 1) * (GDN_DV + GDN_DK))
    if per_chunk_state:
        rv = lax.broadcasted_iota(jnp.int32, (c, 1), 0)
        rvalid = (rv >= valid_lo) & (rv < valid_hi)

    def chunk_rows(ci):
        return slice(ci * c, (ci + 1) * c)

    def prep(chunks):
        probs = []
        for ci in chunks:
            rows = chunk_rows(ci)
            beta_c = gb_ref[0, rows, :]
            gc_all = _dot_f32(tril, gb_ref[1, rows, :])
            gc_t = gc_all.T
            gl_ref[ci] = gc_all[c - 1:c, :]
            for h in heads:
                q = cv_ref[rows, qcol(h)]
                k = cv_ref[rows, kcol(h)]
                v = cv_ref[rows, vcol(h)]
                q = q * lax.rsqrt(jnp.sum(q * q, axis=-1, keepdims=True) + NORM_EPS) * scale_q
                k = k * lax.rsqrt(jnp.sum(k * k, axis=-1, keepdims=True) + NORM_EPS)
                if per_chunk_state:
                    q = jnp.where(rvalid, q, 0.0)
                    k = jnp.where(rvalid, k, 0.0)
                    v = jnp.where(rvalid, v, 0.0)
                beta = beta_c[:, h:h + 1]
                gc_col = gc_all[:, GDN_H + h:GDN_H + h + 1]
                gc_row = gc_t[GDN_H + h:GDN_H + h + 1, :]
                decay = jnp.where(causal, jnp.exp(jnp.minimum(gc_col - gc_row, 0.0)), 0.0)
                e_gc = jnp.exp(gc_col)
                kb = k * beta
                rhs = jnp.concatenate([v * beta, kb * e_gc], axis=1)
                cv_ref[rows, qcol(h)] = q * e_gc
                cv_ref[rows, kcol(h)] = k * jnp.exp(gc_col[c - 1:c, :] - gc_col)
                probs.append((rows, h, q, k, kb, rhs, decay))
        kq = [_dot_nt(jnp.concatenate([kb, q], axis=0), k) for (_, _, q, k, kb, _, _) in probs]
        a_list = [jnp.where(strict, kq_i[:c] * p[6], 0.0) for kq_i, p in zip(kq, probs)]
        t_inv = _tri_inv_many(a_list, c)
        for t_i, kq_i, (rows, h, _, _, _, rhs, decay) in zip(t_inv, kq, probs):
            sol_ref[rows, h * (GDN_DV + GDN_DK):(h + 1) * (GDN_DV + GDN_DK)] = _dot(t_i, rhs)
            att_ref[h, rows, :] = kq_i[c:] * decay

    def scan(chunks):
        probs = [(ci, h) for ci in chunks for h in heads]
        rows = chunk_rows
        si = lambda ci: ci // cps
        g_tot = {ci: jnp.exp(gl_ref[ci]) for ci in chunks}
        s_old = [s_ref[si(ci), h] for ci, h in probs]
        wq_s = [_dot(jnp.concatenate([sol_ref[rows(ci), wcol(h)], cv_ref[rows(ci), qcol(h)]], axis=0), s)
                for (ci, h), s in zip(probs, s_old)]
        v_new = [sol_ref[rows(ci), ucol(h)] - w[:c] for (ci, h), w in zip(probs, wq_s)]
        for (ci, h), s, v in zip(probs, s_old, v_new):
            s_ref[si(ci), h] = (s * g_tot[ci][:, GDN_H + h:GDN_H + h + 1]
                                + _dot_tn(cv_ref[rows(ci), kcol(h)], v))
        o_att = [_dot(att_ref[h, rows(ci), :], v) for (ci, h), v in zip(probs, v_new)]
        for (ci, h), w, oa in zip(probs, wq_s, o_att):
            o = w[c:] + oa
            o = o * lax.rsqrt(jnp.mean(o * o, axis=-1, keepdims=True) + NORM_EPS) * nw_ref[...]
            oacc_ref[rows(ci), h * GDN_DV:(h + 1) * GDN_DV] = o

    if per_chunk_state:
        s_ref[...] = s0_ref[...]
        prep(range(n_chunks))
        scan(range(n_chunks))
    else:
        for s in range(n_seqs):
            prep(range(s * cps, (s + 1) * cps))
        for j in range(cps):
            scan([s * cps + j for s in range(n_seqs)])
    gated = oacc_ref[...] * _silu(z_ref[...].astype(F32).reshape(tb, GDN_W))
    o_ref[...] = gated.astype(o_ref.dtype).reshape(o_ref.shape)


def _gdn(qkv, z, ba, hdr, s0, conv_w, a_row, dt_row, norm_w, *, chunk, n_chunks, chunks_per_seq,
         per_chunk_state, valid_lo, valid_hi, out_dtype, z_colblock=0, ba_colblock=0):
    tb = chunk * n_chunks
    if per_chunk_state:
        rows = qkv.shape[0]
        grid = (rows // tb,)
        row_blk = lambda width, cb: pl.BlockSpec((tb, width), lambda i: (i, cb))
        hdr_spec = pl.BlockSpec((1, HDR, GDN_CONV_CH), lambda i: (0, 0, 0))
        state_spec = pl.BlockSpec((n_chunks, GDN_H, GDN_DK, GDN_DV), lambda i: (i, 0, 0, 0))
        out_o = jax.ShapeDtypeStruct((rows, GDN_W), out_dtype)
        n_hist, hist_rows = 1, HDR + tb
    else:
        n_seq, seq_len = qkv.shape[0], qkv.shape[1]
        assert n_chunks == n_seq * chunks_per_seq and qkv.dtype == BF16
        tbs = chunk * chunks_per_seq
        grid = (seq_len // tbs,)
        row_blk = lambda width, cb: pl.BlockSpec((n_seq, tbs, width), lambda n: (0, n, cb))
        hdr_spec = pl.BlockSpec((n_seq, HDR, GDN_CONV_CH), lambda n: (0, 0, 0))
        state_spec = pl.BlockSpec((n_seq, GDN_H, GDN_DK, GDN_DV), lambda n: (0, 0, 0, 0))
        out_o = jax.ShapeDtypeStruct((n_seq, seq_len, GDN_W), out_dtype)
        n_hist, hist_rows = n_seq, 2 * HDR
    const2 = lambda *_: (0, 0)
    kern = functools.partial(_gdn_kernel, chunk=chunk, n_chunks=n_chunks, chunks_per_seq=chunks_per_seq,
                             per_chunk_state=per_chunk_state, valid_lo=valid_lo, valid_hi=valid_hi)
    return pl.pallas_call(
        kern,
        out_shape=(out_o, jax.ShapeDtypeStruct(s0.shape, F32)),
        grid=grid,
        in_specs=[row_blk(GDN_CONV_CH, 0),
                  row_blk(GDN_W, z_colblock),
                  row_blk(LANE, ba_colblock),
                  hdr_spec,
                  state_spec,
                  pl.BlockSpec((CONV_W, GDN_CONV_CH), const2),
                  pl.BlockSpec((1, LANE), const2),
                  pl.BlockSpec((1, LANE), const2),
                  pl.BlockSpec((1, GDN_DV), const2)],
        out_specs=(row_blk(GDN_W, 0), state_spec),
        scratch_shapes=[pltpu.VMEM((n_hist, hist_rows, GDN_CONV_CH), F32),
                        pltpu.VMEM((tb, GDN_CONV_CH), F32),
                        pltpu.VMEM((2, tb, LANE), F32),
                        pltpu.VMEM((tb, GDN_W), F32),
                        pltpu.VMEM((tb, GDN_H * (GDN_DV + GDN_DK)), F32),
                        pltpu.VMEM((GDN_H, tb, chunk), F32),
                        pltpu.VMEM((n_chunks, 1, LANE), F32)],
        compiler_params=_cparams(("arbitrary",)),
        name="gdn",
    )(qkv, z, ba, hdr, s0, conv_w, a_row, dt_row, norm_w)


def _t5_bucket_np(dist):
    n = np.maximum(dist, 0)
    max_exact = N_BUCKETS // 2
    nf = np.maximum(n, 1).astype(np.float32)
    large = max_exact + (np.log(nf / np.float32(max_exact)) / np.float32(math.log(MAX_DISTANCE / max_exact))
                         * np.float32(N_BUCKETS - max_exact)).astype(np.int32)
    large = np.minimum(large, N_BUCKETS - 1)
    return np.where(n < max_exact, n, large).astype(np.int32)


def _bias_prompt_kernel(code_ref, tab_ref, o_ref):
    code = code_ref[0]
    for h in range(SWA_H):
        acc = jnp.full(code.shape, -jnp.inf, F32)
        for b in range(N_BUCKETS):
            acc = jnp.where(code == b, tab_ref[b, h], acc)
        o_ref[0, h] = acc


def _bias_prompt(table):
    qi = np.arange(WINDOW)[None, :]
    sj = np.arange(WINDOW)[:, None]
    own = sj <= qi
    bucket = _t5_bucket_np(np.where(own, qi - sj, qi + WINDOW - sj))
    code_first = np.where(own, bucket, -1)
    code = jnp.asarray(np.stack([code_first, bucket]).astype(np.int32))
    return pl.pallas_call(
        _bias_prompt_kernel,
        out_shape=jax.ShapeDtypeStruct((2, SWA_H, WINDOW, WINDOW), F32),
        grid=(2,),
        in_specs=[pl.BlockSpec((1, WINDOW, WINDOW), lambda v: (v, 0, 0)),
                  pl.BlockSpec(memory_space=pltpu.SMEM)],
        out_specs=pl.BlockSpec((1, SWA_H, WINDOW, WINDOW), lambda v: (v, 0, 0, 0)),
        compiler_params=_cparams(("arbitrary",)),
        name="swa_bias_prompt",
    )(code, table)


def _bias_sample_kernel(code_ref, tab_ref, o_ref):
    kv = pl.program_id(0)
    code = code_ref[...]
    acc = jnp.full(code.shape, -jnp.inf, F32)
    for g in range(SWA_G):
        for b in range(N_BUCKETS):
            acc = jnp.where(code == b + N_BUCKETS * g, tab_ref[b, kv * SWA_G + g], acc)
    o_ref[0] = acc


def _bias_sample(table, n_tok, n_cache, n_keys_pad):
    dist = (n_cache + np.arange(n_tok))[:, None] - np.arange(n_keys_pad)[None, :]
    valid = (dist >= 0) & (dist < WINDOW) & (np.arange(n_keys_pad)[None, :] < n_cache + n_tok)
    bucket = _t5_bucket_np(dist)
    code_t = np.where(valid, bucket, -1)
    g = np.arange(SWA_G)[:, None, None]
    code = np.where(code_t[None] >= 0, code_t[None] + N_BUCKETS * g, -1)
    code = jnp.asarray(code.reshape(SWA_G * n_tok, n_keys_pad).astype(np.int32))
    return pl.pallas_call(
        _bias_sample_kernel,
        out_shape=jax.ShapeDtypeStruct((SWA_KV, SWA_G * n_tok, n_keys_pad), F32),
        grid=(SWA_KV,),
        in_specs=[pl.BlockSpec((SWA_G * n_tok, n_keys_pad), lambda k: (0, 0)),
                  pl.BlockSpec(memory_space=pltpu.SMEM)],
        out_specs=pl.BlockSpec((1, SWA_G * n_tok, n_keys_pad), lambda k: (k, 0, 0)),
        compiler_params=_cparams(("arbitrary",)),
        name="swa_bias_sample",
    )(code, table)


def _sink_softmax_pv(logits, sink, v):
    m = jnp.maximum(jnp.max(logits, axis=-1, keepdims=True), sink)
    p = jnp.exp(logits - m)
    den = jnp.sum(p, axis=-1, keepdims=True) + jnp.exp(sink - m)
    return _dot(p, v) / den


def _swa_prompt_kernel(q_ref, z_ref, kc_ref, kp_ref, vc_ref, vp_ref, bias_ref, sink_ref, o_ref, *, n_qblk):
    kall = jnp.concatenate([kp_ref[...], kc_ref[...]], axis=0)
    vall = jnp.concatenate([vp_ref[...], vc_ref[...]], axis=0)
    scale = SWA_DH ** -0.5
    key = lax.broadcasted_iota(jnp.int32, (WINDOW, WINDOW), 0)
    qry = lax.broadcasted_iota(jnp.int32, (WINDOW, WINDOW), 1)
    from_prev = key > qry
    first_variant = jnp.where(pl.program_id(1) == 0, 0, 1)
    cs = lambda h: slice(h * SWA_DH, (h + 1) * SWA_DH)
    rows = lambda qb: slice(qb * WINDOW, (qb + 1) * WINDOW)
    keys = lambda qb: slice(qb * WINDOW, (qb + 2) * WINDOW)
    for kv in range(SWA_KV):
        ks = slice(kv * SWA_DH, (kv + 1) * SWA_DH)
        k_kv = kall[:, ks].astype(BF16)
        v_t = vall[:, ks].astype(F32).T.astype(BF16)
        probs = [(qb, h) for qb in range(n_qblk) for h in range(kv * SWA_G, (kv + 1) * SWA_G)]
        lg = [_dot_nt(k_kv[keys(qb)], q_ref[rows(qb), cs(h)] * scale) for qb, h in probs]
        ps, dens = [], []
        for (qb, h), l in zip(probs, lg):
            bias = bias_ref[first_variant if qb == 0 else 1, h]
            l = jnp.where(from_prev, l[:WINDOW], l[WINDOW:]) + bias
            sink = sink_ref[h]
            m = jnp.maximum(jnp.max(l, axis=0, keepdims=True), sink)
            p = jnp.exp(l - m)
            dens.append(jnp.sum(p, axis=0, keepdims=True) + jnp.exp(sink - m))
            ps.append(jnp.concatenate([jnp.where(from_prev, p, 0.0), jnp.where(from_prev, 0.0, p)],
                                      axis=0))
        outs = [_dot(v_t[:, keys(qb)], p) / den for p, den, (qb, _) in zip(ps, dens, probs)]
        for j in range(0, len(probs), 2):
            qb, h = probs[j]
            two = slice(h * SWA_DH, (h + 2) * SWA_DH)
            o2 = jnp.concatenate([outs[j], outs[j + 1]], axis=0).T
            o_ref[rows(qb), two] = (o2 * _silu(z_ref[rows(qb), two].astype(F32))).astype(o_ref.dtype)


def _swa_prompt(proj, kv, bias, sinks, n_seq, seq_len, q_colblock, z_colblock, k_colblock, v_colblock,
                n_qblk=8):
    n_qblk = math.gcd(n_qblk, seq_len // WINDOW)
    tq = n_qblk * WINDOW
    steps = seq_len // tq
    cur = lambda b, n: b * steps + n
    prev = lambda b, n: (b * steps + n) * n_qblk - jnp.where(n == 0, 0, 1)
    return pl.pallas_call(
        functools.partial(_swa_prompt_kernel, n_qblk=n_qblk),
        out_shape=jax.ShapeDtypeStruct((n_seq * seq_len, SWA_QW), BF16),
        grid=(n_seq, steps),
        in_specs=[pl.BlockSpec((tq, SWA_QW), lambda b, n: (cur(b, n), q_colblock)),
                  pl.BlockSpec((tq, SWA_QW), lambda b, n: (cur(b, n), z_colblock)),
                  pl.BlockSpec((tq, SWA_KVW), lambda b, n: (cur(b, n), k_colblock)),
                  pl.BlockSpec((WINDOW, SWA_KVW), lambda b, n: (prev(b, n), k_colblock)),
                  pl.BlockSpec((tq, SWA_KVW), lambda b, n: (cur(b, n), v_colblock)),
                  pl.BlockSpec((WINDOW, SWA_KVW), lambda b, n: (prev(b, n), v_colblock)),
                  pl.BlockSpec((2, SWA_H, WINDOW, WINDOW), lambda b, n: (0, 0, 0, 0)),
                  pl.BlockSpec(memory_space=pltpu.SMEM)],
        out_specs=pl.BlockSpec((tq, SWA_QW), lambda b, n: (cur(b, n), 0)),
        compiler_params=_cparams(("parallel", "arbitrary")),
        name="swa_prompt",
    )(proj, proj, kv, kv, kv, kv, bias, sinks)


def _swa_sample_kernel(q_ref, z_ref, kn_ref, vn_ref, ck_ref, cv_ref, bias_ref, sink_ref, o_ref, nk_ref, nv_ref,
                       *, n_seq_blk, n_new):
    scale = SWA_DH ** -0.5
    wb = ck_ref.shape[3]
    probs = [(s, kv) for s in range(n_seq_blk) for kv in range(SWA_KV)]
    ks = lambda kv: slice(kv * SWA_DH, (kv + 1) * SWA_DH)
    slot = lax.broadcasted_iota(jnp.int32, (SWA_DH, wb), 1)
    tok = lax.broadcasted_iota(jnp.int32, (kn_ref.shape[1], wb), 0)
    tok_slot = lax.broadcasted_iota(jnp.int32, (kn_ref.shape[1], wb), 1)
    place = jnp.where((tok_slot == tok + (wb - n_new)) & (tok < n_new), 1.0, 0.0).astype(F32)
    for s in range(n_seq_blk):
        for cache_ref, new_ref, out_ref in ((ck_ref, kn_ref, nk_ref), (cv_ref, vn_ref, nv_ref)):
            new_t = lax.dot_general(new_ref[s], place, (((0,), (0,)), ((), ())),
                                    preferred_element_type=F32, precision=lax.Precision.HIGHEST)
            for kv in range(SWA_KV):
                out_ref[s, kv] = jnp.where(slot >= wb - n_new, new_t[ks(kv), :],
                                           pltpu.roll(cache_ref[s, kv], wb - n_new, axis=1))
    lc = [_dot(q_ref[s, kv] * scale, ck_ref[s, kv]) + bias_ref[kv, :, 0:wb] for s, kv in probs]
    ln = [_dot_nt(q_ref[s, kv] * scale, kn_ref[s, :, ks(kv)]) + bias_ref[kv, :, wb:] for s, kv in probs]
    pcs, pns, dens = [], [], []
    for (s, kv), c, n in zip(probs, lc, ln):
        sink = sink_ref[kv][:, 0:1]
        m = jnp.maximum(jnp.maximum(jnp.max(c, axis=-1, keepdims=True), jnp.max(n, axis=-1, keepdims=True)),
                        sink)
        pc = jnp.exp(c - m)
        pn = jnp.exp(n - m)
        pcs.append(pc)
        pns.append(pn)
        dens.append(jnp.sum(pc, axis=-1, keepdims=True) + jnp.sum(pn, axis=-1, keepdims=True)
                    + jnp.exp(sink - m))
    outs = [(_dot_nt(pc, cv_ref[s, kv]) + _dot(pn, vn_ref[s, :, ks(kv)])) / den
            for (s, kv), pc, pn, den in zip(probs, pcs, pns, dens)]
    for oh, (s, kv) in zip(outs, probs):
        o_ref[s, kv] = (oh * _silu(z_ref[s, kv].astype(F32))).astype(o_ref.dtype)


def _swa_sample(q, z, k_new, v_new, cache_kt, cache_vt, bias, sink_rows, n_new, n_seq_blk=8):
    bd, _, rows, _ = q.shape
    wb = cache_kt.shape[3]
    npad = k_new.shape[1]
    assert wb == LANE
    blk4 = lambda i: (i, 0, 0, 0)
    blk3 = lambda i: (i, 0, 0)
    cache_spec = pl.BlockSpec((n_seq_blk, SWA_KV, SWA_DH, wb), blk4)
    return pl.pallas_call(
        functools.partial(_swa_sample_kernel, n_seq_blk=n_seq_blk, n_new=n_new),
        out_shape=(jax.ShapeDtypeStruct(q.shape, BF16),
                   jax.ShapeDtypeStruct(cache_kt.shape, F32), jax.ShapeDtypeStruct(cache_vt.shape, F32)),
        grid=(bd // n_seq_blk,),
        in_specs=[pl.BlockSpec((n_seq_blk, SWA_KV, rows, SWA_DH), blk4),
                  pl.BlockSpec((n_seq_blk, SWA_KV, rows, SWA_DH), blk4),
                  pl.BlockSpec((n_seq_blk, npad, SWA_KVW), blk3),
                  pl.BlockSpec((n_seq_blk, npad, SWA_KVW), blk3),
                  pl.BlockSpec((n_seq_blk, SWA_KV, SWA_DH, wb), blk4),
                  pl.BlockSpec((n_seq_blk, SWA_KV, SWA_DH, wb), blk4),
                  pl.BlockSpec((SWA_KV, rows, wb + npad), lambda i: (0, 0, 0)),
                  pl.BlockSpec((SWA_KV, rows, LANE), lambda i: (0, 0, 0))],
        out_specs=(pl.BlockSpec((n_seq_blk, SWA_KV, rows, SWA_DH), blk4), cache_spec, cache_spec),
        compiler_params=_cparams(("arbitrary",)),
        name="swa_sample",
    )(q, z, k_new, v_new, cache_kt, cache_vt, bias, sink_rows)


def _mem_attend(probs, q_of, z_of, k_of, v_of, store):
    scale = MEM_DH ** -0.5
    logits = [_dot_nt(q_of(p) * scale, k_of(p)) for p in probs]
    ps, dens = [], []
    for l in logits:
        m = jnp.max(l, axis=-1, keepdims=True)
        e = jnp.exp(l - m)
        dens.append(jnp.sum(e, axis=-1, keepdims=True))
        ps.append(e)
    outs = [_dot(e, v_of(p)) / den for e, den, p in zip(ps, dens, probs)]
    for p, oh in zip(probs, outs):
        store(p, oh * _silu(z_of(p).astype(F32)))


def _mem_cols(h):
    return slice(h * MEM_DH, (h + 1) * MEM_DH)


def _mem_prompt_kernel(q_ref, z_ref, k_ref, v_ref, o_ref):
    def store(h, val):
        o_ref[:, _mem_cols(h)] = val.astype(o_ref.dtype)
    _mem_attend(range(MEM_H), lambda h: q_ref[:, _mem_cols(h)], lambda h: z_ref[:, _mem_cols(h)],
                lambda h: k_ref[:, _mem_cols(h)], lambda h: v_ref[:, _mem_cols(h)], store)


def _mem_prompt(proj, mkv, n_seq, seq_len, q_colblock, z_colblock, tq=2048):
    tq = math.gcd(tq, seq_len)
    steps = seq_len // tq
    return pl.pallas_call(
        _mem_prompt_kernel,
        out_shape=jax.ShapeDtypeStruct((n_seq * seq_len, MEM_W), BF16),
        grid=(n_seq, steps),
        in_specs=[pl.BlockSpec((tq, MEM_W), lambda b, n: (b * steps + n, q_colblock)),
                  pl.BlockSpec((tq, MEM_W), lambda b, n: (b * steps + n, z_colblock)),
                  pl.BlockSpec((N_MEM, MEM_W), lambda b, n: (b, 0)),
                  pl.BlockSpec((N_MEM, MEM_W), lambda b, n: (b, 1))],
        out_specs=pl.BlockSpec((tq, MEM_W), lambda b, n: (b * steps + n, 0)),
        compiler_params=_cparams(("parallel", "arbitrary")),
        name="mem_prompt",
    )(proj, proj, mkv, mkv)


def _mem_sample_kernel(q_ref, z_ref, k_hbm, v_hbm, o_ref, kbuf, vbuf, sem, *, n_seq_blk):
    i = pl.program_id(0)
    n_steps = pl.num_programs(0)
    slot = i % 2

    def copies(step, slot_):
        seqs = pl.ds(step * n_seq_blk, n_seq_blk)
        out = []
        for h in range(MEM_H):
            out.append(pltpu.make_async_copy(k_hbm.at[seqs, :, h, :], kbuf.at[slot_, h], sem.at[0, slot_, h]))
            out.append(pltpu.make_async_copy(v_hbm.at[seqs, :, h, :], vbuf.at[slot_, h], sem.at[1, slot_, h]))
        return out

    @pl.when(i == 0)
    def _():
        for cp in copies(0, 0):
            cp.start()

    @pl.when(i + 1 < n_steps)
    def _():
        for cp in copies(i + 1, 1 - slot):
            cp.start()

    for cp in copies(i, slot):
        cp.wait()

    def store(p, val):
        o_ref[p[0], :, _mem_cols(p[1])] = val.astype(o_ref.dtype)
    probs = [(s, h) for s in range(n_seq_blk) for h in range(MEM_H)]
    _mem_attend(probs, lambda p: q_ref[p[0], :, _mem_cols(p[1])], lambda p: z_ref[p[0], :, _mem_cols(p[1])],
                lambda p: kbuf[slot, p[1], p[0]], lambda p: vbuf[slot, p[1], p[0]], store)


def _mem_sample(q, z, cache_k, cache_v, n_seq_blk=8):
    bd, rows, _ = q.shape
    blk = lambda i: (i, 0, 0)
    buf = pltpu.VMEM((2, MEM_H, n_seq_blk, N_MEM, MEM_DH), cache_k.dtype)
    return pl.pallas_call(
        functools.partial(_mem_sample_kernel, n_seq_blk=n_seq_blk),
        out_shape=jax.ShapeDtypeStruct(q.shape, BF16),
        grid=(bd // n_seq_blk,),
        in_specs=[pl.BlockSpec((n_seq_blk, rows, MEM_W), blk),
                  pl.BlockSpec((n_seq_blk, rows, MEM_W), blk),
                  pl.BlockSpec(memory_space=pl.ANY),
                  pl.BlockSpec(memory_space=pl.ANY)],
        out_specs=pl.BlockSpec((n_seq_blk, rows, MEM_W), blk),
        scratch_shapes=[buf, buf, pltpu.SemaphoreType.DMA((2, 2, MEM_H))],
        compiler_params=_cparams(("arbitrary",)),
        name="mem_sample",
    )(q, z, cache_k, cache_v)


def _merge_kernel(og_ref, os_ref, om_ref, gate_ref, x_ref, wb_ref, wo_ref, nf_ref, y_ref):
    merged = None
    for b, o_ref in enumerate((og_ref, os_ref, om_ref)):
        t = jnp.dot(o_ref[...], wb_ref[b], preferred_element_type=F32)
        t = t * gate_ref[:, b * D_MODEL:(b + 1) * D_MODEL].astype(F32)
        merged = t if merged is None else merged + t
    h = x_ref[...] + jnp.dot(merged.astype(BF16), wo_ref[...], preferred_element_type=F32)
    ms = jnp.mean(h * h, axis=-1, keepdims=True)
    y_ref[...] = h * lax.rsqrt(ms + NORM_EPS) * nf_ref[...]


def _merge(o_gdn, o_swa, o_mem, gates, x, w_branch, w_out, norm_f, tm=256):
    m = x.shape[0]
    tm = min(tm, m)
    row = lambda i: (i, 0)
    const2 = lambda i: (0, 0)
    return pl.pallas_call(
        _merge_kernel,
        out_shape=jax.ShapeDtypeStruct((m, D_MODEL), F32),
        grid=(m // tm,),
        in_specs=[pl.BlockSpec((tm, BR_W), row),
                  pl.BlockSpec((tm, BR_W), row),
                  pl.BlockSpec((tm, BR_W), row),
                  pl.BlockSpec((tm, N_BRANCH * D_MODEL), row),
                  pl.BlockSpec((tm, D_MODEL), row),
                  pl.BlockSpec((N_BRANCH, BR_W, D_MODEL), lambda i: (0, 0, 0),
                               pipeline_mode=pl.Buffered(1)),
                  pl.BlockSpec((D_MODEL, D_MODEL), const2, pipeline_mode=pl.Buffered(1)),
                  pl.BlockSpec((1, D_MODEL), const2)],
        out_specs=pl.BlockSpec((tm, D_MODEL), row),
        compiler_params=_cparams(("parallel",)),
        name="merge",
    )(o_gdn, o_swa, o_mem, gates, x, w_branch, w_out, norm_f)


_IN_SIZES = (GDN_QK, GDN_QK, GDN_W, GDN_W, GDN_H, GDN_H, SWA_QW, SWA_KVW, SWA_KVW, SWA_QW,
             MEM_W, MEM_W, N_BRANCH * D_MODEL)
_IN_NAMES = ("gq", "gk", "gv", "gz", "gb", "ga", "sq", "sk", "sv", "sz", "mq", "mz", "mg")
_IN_SPAN = {name: (int(off), int(off + size)) for name, off, size in
            zip(_IN_NAMES, np.cumsum((0,) + _IN_SIZES[:-1]), _IN_SIZES)}


def kernel(x_prompt, x_sample, state_gdn, state_gdn_conv, cache_swa_k, cache_swa_v, cache_mem_k,
           cache_mem_v, mem_prompt, norm_in, w_in, gdn_conv_w, gdn_a_log, gdn_dt_bias, gdn_norm,
           swa_sinks, rel_bias, norm_mem, w_mem_kv, w_branch, w_out, norm_f):
    n_layers = norm_in.shape[0]
    assert n_layers == 1
    b, seq, _ = x_prompt.shape
    bd, ns, _ = x_sample.shape
    wb = cache_swa_k.shape[2]
    assert seq % WINDOW == 0 and seq % GDN_CHUNK == 0 and ns + CONV_W <= SUBLANE and wb == WINDOW
    lyr = 0

    w = w_in[lyr]
    main_names = ("gq", "gk", "gv", "gz", "sq", "sz", "mq", "mz")
    n_main = len(main_names) * W_BLK
    n_gate = N_BRANCH * D_MODEL
    n_small = 3 * LANE
    small_lead = -(n_main + n_gate) % n_small
    w_all = _wprep(w.T, [[(_IN_SPAN[a][0], W_BLK)] for a in main_names]
                   + [[(_IN_SPAN["mg"][0] + W_BLK * k, W_BLK)] for k in range(n_gate // W_BLK)]
                   + [[(None, small_lead), (_IN_SPAN["sk"][0], 2 * SWA_KVW), (_IN_SPAN["gb"][0], 2 * GDN_H)]])
    col_small = n_main + n_gate + small_lead
    cb_gz, cb_sq, cb_sz, cb_mq, cb_mz = 3, 4, 5, 6, 7
    cb_sk, cb_sv, cb_ba = 0, 1, 2
    w_mkv = w_mem_kv[lyr]
    w_br = w_branch[lyr].astype(BF16)
    w_o = w_out[lyr].astype(BF16)
    nw_in = norm_in[lyr].reshape(1, D_MODEL)
    nw_mem = norm_mem[lyr].reshape(1, D_MODEL)
    nw_f = norm_f.reshape(1, D_MODEL)
    conv_w = gdn_conv_w[lyr]
    a_row = jnp.pad(gdn_a_log[lyr].reshape(1, GDN_H), ((0, 0), (GDN_H, LANE - 2 * GDN_H)))
    dt_row = jnp.pad(gdn_dt_bias[lyr].reshape(1, GDN_H), ((0, 0), (GDN_H, LANE - 2 * GDN_H)))
    gnw = gdn_norm[lyr].reshape(1, GDN_DV)
    sinks = swa_sinks[lyr]
    bias_p = _bias_prompt(rel_bias)
    npad = SUBLANE
    bias_s = _bias_sample(rel_bias, ns, wb, wb + npad)
    sink_rows = jnp.broadcast_to(jnp.repeat(sinks.reshape(SWA_KV, SWA_G), ns, axis=1)[:, :, None],
                                 (SWA_KV, SWA_G * ns, LANE))

    t = b * seq
    xp = x_prompt.reshape(t, D_MODEL)
    xn_p, p_small = _rmsnorm(xp, nw_in, w_all, col_small, n_small)
    p_main = _proj(xn_p, w_all, BF16, n=n_main)
    g_p = _proj(xn_p, w_all, BF16, act="sigmoid", col0=n_main, n=n_gate)
    xn_tail = xn_p.reshape(b, seq, D_MODEL)[:, seq - SUBLANE:, :].reshape(b * SUBLANE, D_MODEL)
    conv_p = _proj(xn_tail, w_all, F32, n=GDN_CONV_CH).reshape(b, SUBLANE, GDN_CONV_CH)[:, SUBLANE - (CONV_W - 1):]
    kv_tail = p_small.reshape(b, seq, 3 * LANE)[:, seq - WINDOW:]
    swk_p = kv_tail[:, :, cb_sk * LANE:(cb_sk + 1) * LANE].reshape(b, WINDOW, SWA_KV, SWA_DH)
    swv_p = kv_tail[:, :, cb_sv * LANE:(cb_sv + 1) * LANE].reshape(b, WINDOW, SWA_KV, SWA_DH)

    mkv = _proj(_rmsnorm(mem_prompt.reshape(b * N_MEM, D_MODEL), nw_mem), w_mkv, F32)
    mk_p = mkv[:, :MEM_W].reshape(b, N_MEM, MEM_W)
    mv_p = mkv[:, MEM_W:].reshape(b, N_MEM, MEM_W)

    p_main3 = p_main.reshape(b, seq, n_main)
    gdn_cps = 4
    o_gdn_p, s_p = _gdn(p_main3, p_main3, p_small.reshape(b, seq, n_small),
                        jnp.zeros((b, HDR, GDN_CONV_CH), F32),
                        jnp.zeros((b, GDN_H, GDN_DK, GDN_DV), F32), conv_w, a_row, dt_row, gnw,
                        chunk=GDN_CHUNK, n_chunks=gdn_cps * b, chunks_per_seq=gdn_cps,
                        per_chunk_state=False, valid_lo=0, valid_hi=GDN_CHUNK, out_dtype=BF16,
                        z_colblock=cb_gz, ba_colblock=cb_ba)
    o_gdn_p = o_gdn_p.reshape(t, GDN_W)
    o_swa_p = _swa_prompt(p_main, p_small, bias_p, sinks, b, seq, cb_sq, cb_sz, cb_sk, cb_sv)
    o_mem_p = _mem_prompt(p_main, mkv, b, seq, cb_mq, cb_mz)
    y_p = _merge(o_gdn_p, o_swa_p, o_mem_p, g_p, xp, w_br, w_o, nw_f).reshape(b, seq, D_MODEL)

    ts = bd * ns
    xs = x_sample.reshape(ts, D_MODEL)
    xn_s, s_small = _rmsnorm(xs, nw_in, w_all, col_small, n_small)
    s_main = _proj(xn_s, w_all, F32, n=n_main)
    g_s = _proj(xn_s, w_all, BF16, act="sigmoid", col0=n_main, n=n_gate)
    s_gdn = s_main[:, :GDN_CONV_CH + GDN_W]
    s_ba = s_small[:, cb_ba * LANE:(cb_ba + 1) * LANE]
    s_swa = jnp.concatenate([s_main[:, cb_sq * BR_W:(cb_sz + 1) * BR_W], s_small[:, :2 * LANE]], axis=1)
    s_mem = s_main[:, cb_mq * BR_W:(cb_mz + 1) * BR_W].astype(BF16)

    lo = CONV_W - 1
    hi = lo + ns
    pad_rows = ((0, 0), (lo, SUBLANE - hi), (0, 0))
    e_qkv = jnp.concatenate([state_gdn_conv[lyr], s_gdn[:, :GDN_CONV_CH].reshape(bd, ns, GDN_CONV_CH),
                             jnp.zeros((bd, SUBLANE - hi, GDN_CONV_CH), F32)], axis=1)
    e_z = jnp.pad(s_gdn[:, GDN_CONV_CH:].reshape(bd, ns, GDN_W), pad_rows)
    e_ba = jnp.pad(s_ba.reshape(bd, ns, LANE), pad_rows)
    seq_blk = 16
    o_gdn_s8, s_s = _gdn(e_qkv.reshape(bd * SUBLANE, GDN_CONV_CH), e_z.reshape(bd * SUBLANE, GDN_W),
                         e_ba.reshape(bd * SUBLANE, LANE), jnp.zeros((1, HDR, GDN_CONV_CH), F32),
                         state_gdn[lyr], conv_w, a_row, dt_row, gnw,
                         chunk=SUBLANE, n_chunks=seq_blk, chunks_per_seq=1, per_chunk_state=True,
                         valid_lo=lo, valid_hi=hi, out_dtype=BF16)
    o_gdn_s = o_gdn_s8.reshape(bd, SUBLANE, GDN_W)[:, lo:hi].reshape(ts, GDN_W)
    conv_s = e_qkv[:, hi - (CONV_W - 1):hi]

    def to_heads(a):
        return a.reshape(bd, ns, SWA_KV, SWA_G, SWA_DH).transpose(0, 2, 3, 1, 4).reshape(
            bd, SWA_KV, SWA_G * ns, SWA_DH)

    k_new = s_swa[:, 2 * SWA_QW:2 * SWA_QW + SWA_KVW].reshape(bd, ns, SWA_KVW)
    v_new = s_swa[:, 2 * SWA_QW + SWA_KVW:].reshape(bd, ns, SWA_KVW)
    tok_pad = ((0, 0), (0, npad - ns), (0, 0))
    ck_t = cache_swa_k[lyr].transpose(0, 2, 3, 1)
    cv_t = cache_swa_v[lyr].transpose(0, 2, 3, 1)
    o_swa_h, nk_t, nv_t = _swa_sample(
        to_heads(s_swa[:, :SWA_QW]).astype(BF16), to_heads(s_swa[:, SWA_QW:2 * SWA_QW]),
        jnp.pad(k_new, tok_pad), jnp.pad(v_new, tok_pad), ck_t, cv_t, bias_s, sink_rows, n_new=ns)
    o_swa_s = o_swa_h.reshape(bd, SWA_KV, SWA_G, ns, SWA_DH).transpose(0, 3, 1, 2, 4).reshape(ts, SWA_QW)
    swk_s = nk_t.transpose(0, 3, 1, 2)
    swv_s = nv_t.transpose(0, 3, 1, 2)

    mq = jnp.pad(s_mem[:, :MEM_W].reshape(bd, ns, MEM_W), tok_pad)
    mz = jnp.pad(s_mem[:, MEM_W:].reshape(bd, ns, MEM_W), tok_pad)
    o_mem_s = _mem_sample(mq, mz, cache_mem_k[lyr], cache_mem_v[lyr])[:, :ns].reshape(ts, MEM_W)
    y_s = _merge(o_gdn_s, o_swa_s, o_mem_s, g_s, xs, w_br, w_o, nw_f).reshape(bd, ns, D_MODEL)

    return (y_p, y_s,
            s_p[None], conv_p[None], swk_p[None], swv_p[None],
            mk_p.reshape(b, N_MEM, MEM_H, MEM_DH)[None], mv_p.reshape(b, N_MEM, MEM_H, MEM_DH)[None],
            s_s[None], conv_s[None], swk_s[None], swv_s[None])
```

```python
import functools
import math

import numpy as np
import jax
import jax.numpy as jnp
from jax import lax
from jax.experimental import pallas as pl
from jax.experimental.pallas import tpu as pltpu

F32 = jnp.float32
BF16 = jnp.bfloat16

D_MODEL = 2048
N_BRANCH = 3
BR_W = 1024
GDN_H = 8
GDN_DK = 128
GDN_DV = 128
GDN_QK = GDN_H * GDN_DK
GDN_W = GDN_H * GDN_DV
GDN_CONV_CH = 2 * GDN_QK + GDN_W
CONV_W = 4
GDN_CHUNK = 64
SWA_H = 16
SWA_KV = 2
SWA_G = SWA_H // SWA_KV
SWA_DH = 64
SWA_QW = SWA_H * SWA_DH
SWA_KVW = SWA_KV * SWA_DH
WINDOW = 128
N_BUCKETS = 32
MAX_DISTANCE = 128
N_MEM = 256
MEM_H = 4
MEM_DH = 256
MEM_W = MEM_H * MEM_DH
NORM_EPS = 1e-6

LANE = 128
SUBLANE = 8
VMEM_LIMIT = 52 * 1024 * 1024


def _cparams(sem):
    return pltpu.CompilerParams(dimension_semantics=sem, vmem_limit_bytes=VMEM_LIMIT)


def _sigmoid(x):
    return 0.5 * jnp.tanh(0.5 * x) + 0.5


def _silu(x):
    h = 0.5 * x
    return h * jnp.tanh(h) + h


def _softplus(x):
    return jnp.maximum(x, 0.0) + jnp.log(1.0 + jnp.exp(-jnp.abs(x)))


def _dot(a, b):
    return jnp.dot(a.astype(BF16), b.astype(BF16), preferred_element_type=F32)


def _dot_nt(a, b):
    return lax.dot_general(a.astype(BF16), b.astype(BF16), (((1,), (1,)), ((), ())),
                           preferred_element_type=F32)


def _dot_tn(a, b):
    return lax.dot_general(a.astype(BF16), b.astype(BF16), (((0,), (0,)), ((), ())),
                           preferred_element_type=F32)


def _dot_f32(a, b):
    return jnp.dot(a, b, preferred_element_type=F32, precision=lax.Precision.HIGHEST)


def _rmsnorm_kernel(x_ref, nw_ref, *rest):
    x = x_ref[...].astype(F32)
    ms = jnp.mean(x * x, axis=-1, keepdims=True)
    xn = (x * lax.rsqrt(ms + NORM_EPS) * nw_ref[...]).astype(BF16)
    if len(rest) == 1:
        rest[0][...] = xn
    else:
        w_ref, o_ref, p_ref = rest
        o_ref[...] = xn
        p_ref[...] = jnp.dot(xn, w_ref[...], preferred_element_type=F32)


def _rmsnorm(x, norm_w, w=None, col0=0, n=None, tm_pref=1024):
    m, d = x.shape
    tm = min(m, tm_pref)
    assert m % tm == 0
    row = lambda i: (i, 0)
    in_specs = [pl.BlockSpec((tm, d), row), pl.BlockSpec((1, d), lambda i: (0, 0))]
    out_shape = jax.ShapeDtypeStruct((m, d), BF16)
    out_specs = pl.BlockSpec((tm, d), row)
    args = (x, norm_w)
    if w is not None:
        assert col0 % n == 0
        in_specs.append(pl.BlockSpec((d, n), lambda i: (0, col0 // n)))
        out_shape = (out_shape, jax.ShapeDtypeStruct((m, n), F32))
        out_specs = (out_specs, pl.BlockSpec((tm, n), row))
        args = args + (w,)
    return pl.pallas_call(
        _rmsnorm_kernel,
        out_shape=out_shape,
        grid=(m // tm,),
        in_specs=in_specs,
        out_specs=out_specs,
        compiler_params=_cparams(("parallel",)),
        name="rmsnorm",
    )(*args)


def _proj_kernel(x_ref, w_ref, o_ref, *, act):
    y = jnp.dot(x_ref[...], w_ref[...].astype(BF16), preferred_element_type=F32)
    if act == "sigmoid":
        y = _sigmoid(y)
    o_ref[...] = y.astype(o_ref.dtype)


def _pick_tile(n, pref):
    t = min(n, pref)
    while n % t:
        t -= LANE
    return t


def _proj(xn, w, out_dtype, act=None, col0=0, n=None, tm_pref=1024, tn_pref=2048):
    m, d = xn.shape
    n = w.shape[1] if n is None else n
    tm = min(m, tm_pref)
    assert m % tm == 0
    tn = _pick_tile(n, tn_pref if out_dtype == BF16 else tn_pref // 2)
    while col0 % tn:
        tn = _pick_tile(n, tn - LANE)
    jb = col0 // tn
    return pl.pallas_call(
        functools.partial(_proj_kernel, act=act),
        out_shape=jax.ShapeDtypeStruct((m, n), out_dtype),
        grid=(m // tm, n // tn),
        in_specs=[pl.BlockSpec((tm, d), lambda i, j: (i, 0)),
                  pl.BlockSpec((d, tn), lambda i, j: (0, jb + j))],
        out_specs=pl.BlockSpec((tm, tn), lambda i, j: (i, j)),
        compiler_params=_cparams(("parallel", "arbitrary")),
        name="proj",
    )(xn, w)


W_BLK = 1024
W_PIECE_ROWS = (W_BLK, 2 * SUBLANE)


def _wprep_kernel(wt_hbm, o_ref, inbuf0, inbuf1, sem, *, blocks):
    j = pl.program_id(0)
    inbuf = (inbuf0, inbuf1)

    def sources(b):
        return [(start, rows) for start, rows in blocks[b] if start is not None]

    def pieces(b):
        return [pltpu.make_async_copy(wt_hbm.at[pl.ds(start, rows), :],
                                      inbuf[k].at[b % 2, pl.ds(0, rows), :], sem.at[b % 2, k])
                for k, (start, rows) in enumerate(sources(b))]

    for b, ranges in enumerate(blocks):
        @pl.when(j == b)
        def _(b=b, ranges=ranges):
            if b == 0:
                for cp in pieces(0):
                    cp.start()
            if b + 1 < len(blocks):
                for cp in pieces(b + 1):
                    cp.start()
            for cp in pieces(b):
                cp.wait()
            n_have = sum(rows for _, rows in ranges)
            parts, k = [], 0
            for start, rows in tuple(ranges) + ((None, W_BLK - n_have),):
                if start is None:
                    if rows:
                        parts.append(jnp.zeros((rows, inbuf0.shape[2]), inbuf0.dtype))
                else:
                    parts.append(inbuf[k][b % 2, 0:rows, :])
                    k += 1
            val = parts[0] if len(parts) == 1 else jnp.concatenate(parts, axis=0)
            o_ref[...] = val.T.astype(BF16)


def _wprep(wt, block_srcs):
    ncol, d = wt.shape
    for ranges in block_srcs:
        srcs = [(s, rows) for s, rows in ranges if s is not None]
        assert len(srcs) <= len(W_PIECE_ROWS) and all(rows % SUBLANE == 0 for _, rows in ranges)
        for (s, rows), cap in zip(srcs, W_PIECE_ROWS):
            assert s % SUBLANE == 0 and rows <= cap and s + rows <= ncol
    return pl.pallas_call(
        functools.partial(_wprep_kernel, blocks=tuple(tuple(r) for r in block_srcs)),
        out_shape=jax.ShapeDtypeStruct((d, W_BLK * len(block_srcs)), BF16),
        grid=(len(block_srcs),),
        in_specs=[pl.BlockSpec(memory_space=pl.ANY)],
        out_specs=pl.BlockSpec((d, W_BLK), lambda j: (0, j)),
        scratch_shapes=[pltpu.VMEM((2, cap, d), wt.dtype) for cap in W_PIECE_ROWS]
                       + [pltpu.SemaphoreType.DMA((2, len(W_PIECE_ROWS)))],
        compiler_params=_cparams(("arbitrary",)),
        name="wprep",
    )(wt)


HDR = SUBLANE


def _tri_inv_many(a_list, c):
    row = lax.broadcasted_iota(jnp.int32, (c, c), 0)
    col = lax.broadcasted_iota(jnp.int32, (c, c), 1)
    eye = jnp.where(row == col, 1.0, 0.0).astype(F32)
    xs = [eye - a for a in a_list]
    bs = [_dot(a, a) for a in a_list]
    n = 2
    while n < c:
        xs = [x + _dot(x, b) for x, b in zip(xs, bs)]
        n *= 2
        if n < c:
            bs = [_dot(b, b) for b in bs]
    return xs


def _gdn_kernel(qkv_ref, z_ref, ba_ref, hdr_ref, s0_ref, cw_ref, arow_ref, dtrow_ref, nw_ref,
                o_ref, s_ref, buf_ref, cv_ref, gb_ref, oacc_ref, sol_ref, att_ref, gl_ref,
                *, chunk, n_chunks, chunks_per_seq, per_chunk_state, valid_lo, valid_hi):
    c = chunk
    tb = c * n_chunks
    cps = chunks_per_seq
    n_seqs = n_chunks // cps
    tbs = cps * c

    if per_chunk_state:
        buf_ref[0, 0:2 * HDR, :] = jnp.zeros((2 * HDR, GDN_CONV_CH), F32)
    else:
        @pl.when(pl.program_id(0) == 0)
        def _():
            for s in range(n_seqs):
                buf_ref[s, 0:HDR, :] = hdr_ref[s]
                buf_ref[s, HDR:2 * HDR, :] = jnp.zeros((HDR, GDN_CONV_CH), F32)
            s_ref[...] = s0_ref[...]

    def tap_from_history(s, j, n_rows):
        off = HDR - (CONV_W - 1) + j
        return buf_ref[s, off:off + n_rows, :] * cw_ref[j:j + 1, :]

    if qkv_ref.dtype == BF16:
        r = lax.broadcasted_iota(jnp.int32, (tbs, tbs), 0)
        cc = lax.broadcasted_iota(jnp.int32, (tbs, tbs), 1)
        for s in range(n_seqs):
            xb = qkv_ref[s]
            acc = xb.astype(F32) * cw_ref[CONV_W - 1:CONV_W, :]
            for j in range(CONV_W - 1):
                shift = jnp.where(r - cc == CONV_W - 1 - j, 1.0, 0.0).astype(BF16)
                acc = acc + jnp.dot(shift, xb, preferred_element_type=F32) * cw_ref[j:j + 1, :]
            cv_ref[s * tbs:(s + 1) * tbs, :] = _silu(acc)
            top = acc[0:HDR]
            for j in range(CONV_W - 1):
                top = top + tap_from_history(s, j, HDR)
            cv_ref[s * tbs:s * tbs + HDR, :] = _silu(top)
            buf_ref[s, 0:HDR, :] = qkv_ref[s, tbs - 2 * HDR:tbs, :].astype(F32)[HDR:]
    else:
        buf_ref[0, HDR:HDR + tb, :] = qkv_ref[...].astype(F32)
        acc = None
        for j in range(CONV_W):
            term = tap_from_history(0, j, tb)
            acc = term if acc is None else acc + term
        cv_ref[...] = _silu(acc)

    ba = ba_ref[...].astype(F32).reshape(tb, LANE)
    beta_all = _sigmoid(ba)
    g_all = -jnp.exp(arow_ref[...]) * _softplus(ba + dtrow_ref[...])
    if per_chunk_state:
        r = lax.broadcasted_iota(jnp.int32, (tb, LANE), 0) & (c - 1)
        valid = (r >= valid_lo) & (r < valid_hi)
        beta_all = jnp.where(valid, beta_all, 0.0)
        g_all = jnp.where(valid, g_all, 0.0)
    gb_ref[0] = beta_all
    gb_ref[1] = g_all

    row = lax.broadcasted_iota(jnp.int32, (c, c), 0)
    col = lax.broadcasted_iota(jnp.int32, (c, c), 1)
    causal = row >= col
    strict = row > col
    tril = jnp.where(causal, 1.0, 0.0).astype(F32)
    scale_q = GDN_DK ** -0.5

    heads = range(GDN_H)
    qcol = lambda h: slice(h * GDN_DK, (h + 1) * GDN_DK)
    kcol = lambda h: slice(GDN_QK + h * GDN_DK, GDN_QK + (h + 1) * GDN_DK)
    vcol = lambda h: slice(2 * GDN_QK + h * GDN_DV, 2 * GDN_QK + (h + 1) * GDN_DV)
    ucol = lambda h: slice(h * (GDN_DV + GDN_DK), h * (GDN_DV + GDN_DK) + GDN_DV)
    wcol = lambda h: slice(h * (GDN_DV + GDN_DK) + GDN_DV, (h + 1) * (GDN_DV + GDN_DK))
    if per_chunk_state:
        rv = lax.broadcasted_iota(jnp.int32, (c, 1), 0)
        rvalid = (rv >= valid_lo) & (rv < valid_hi)

    def chunk_rows(ci):
        return slice(ci * c, (ci + 1) * c)

    def prep(chunks):
        probs = []
        for ci in chunks:
            rows = chunk_rows(ci)
            beta_c = gb_ref[0, rows, :]
            gc_all = _dot_f32(tril, gb_ref[1, rows, :])
            gc_t = gc_all.T
            gl_ref[ci] = gc_all[c - 1:c, :]
            for h in heads:
                q = cv_ref[rows, qcol(h)]
                k = cv_ref[rows, kcol(h)]
                v = cv_ref[rows, vcol(h)]
                q = q * lax.rsqrt(jnp.sum(q * q, axis=-1, keepdims=True) + NORM_EPS) * scale_q
                k = k * lax.rsqrt(jnp.sum(k * k, axis=-1, keepdims=True) + NORM_EPS)
                if per_chunk_state:
                    q = jnp.where(rvalid, q, 0.0)
                    k = jnp.where(rvalid, k, 0.0)
                    v = jnp.where(rvalid, v, 0.0)
                beta = beta_c[:, h:h + 1]
                gc_col = gc_all[:, GDN_H + h:GDN_H + h + 1]
                gc_row = gc_t[GDN_H + h:GDN_H + h + 1, :]
                decay = jnp.where(causal, jnp.exp(jnp.minimum(gc_col - gc_row, 0.0)), 0.0)
                e_gc = jnp.exp(gc_col)
                kb = k * beta
                rhs = jnp.concatenate([v * beta, kb * e_gc], axis=1)
                cv_ref[rows, qcol(h)] = q * e_gc
                cv_ref[rows, kcol(h)] = k * jnp.exp(gc_col[c - 1:c, :] - gc_col)
                probs.append((rows, h, q, k, kb, rhs, decay))
        kq = [_dot_nt(jnp.concatenate([kb, q], axis=0), k) for (_, _, q, k, kb, _, _) in probs]
        a_list = [jnp.where(strict, kq_i[:c] * p[6], 0.0) for kq_i, p in zip(kq, probs)]
        t_inv = _tri_inv_many(a_list, c)
        for t_i, kq_i, (rows, h, _, _, _, rhs, decay) in zip(t_inv, kq, probs):
            sol_ref[rows, h * (GDN_DV + GDN_DK):(h + 1) * (GDN_DV + GDN_DK)] = _dot(t_i, rhs)
            att_ref[h, rows, :] = kq_i[c:] * decay

    def scan(chunks):
        probs = [(ci, h) for ci in chunks for h in heads]
        rows = chunk_rows
        si = lambda ci: ci // cps
        g_tot = {ci: jnp.exp(gl_ref[ci]) for ci in chunks}
        s_old = [s_ref[si(ci), h] for ci, h in probs]
        wq_s = [_dot(jnp.concatenate([sol_ref[rows(ci), wcol(h)], cv_ref[rows(ci), qcol(h)]], axis=0), s)
                for (ci, h), s in zip(probs, s_old)]
        v_new = [sol_ref[rows(ci), ucol(h)] - w[:c] for (ci, h), w in zip(probs, wq_s)]
        for (ci, h), s, v in zip(probs, s_old, v_new):
            s_ref[si(ci), h] = (s * g_tot[ci][:, GDN_H + h:GDN_H + h + 1]
                                + _dot_tn(cv_ref[rows(ci), kcol(h)], v))
        o_att = [_dot(att_ref[h, rows(ci), :], v) for (ci, h), v in zip(probs, v_new)]
        for (ci, h), w, oa in zip(probs, wq_s, o_att):
            o = w[c:] + oa
            o = o * lax.rsqrt(jnp.mean(o * o, axis=-1, keepdims=True) + NORM_EPS) * nw_ref[...]
            oacc_ref[rows(ci), h * GDN_DV:(h + 1) * GDN_DV] = o

    if per_chunk_state:
        s_ref[...] = s0_ref[...]
        prep(range(n_chunks))
        scan(range(n_chunks))
    else:
        for s in range(n_seqs):
            prep(range(s * cps, (s + 1) * cps))
        for j in range(cps):
            scan([s * cps + j for s in range(n_seqs)])
    gated = oacc_ref[...] * _silu(z_ref[...].astype(F32).reshape(tb, GDN_W))
    o_ref[...] = gated.astype(o_ref.dtype).reshape(o_ref.shape)


def _gdn(qkv, z, ba, hdr, s0, conv_w, a_row, dt_row, norm_w, *, chunk, n_chunks, chunks_per_seq,
         per_chunk_state, valid_lo, valid_hi, out_dtype, z_colblock=0, ba_colblock=0):
    tb = chunk * n_chunks
    if per_chunk_state:
        rows = qkv.shape[0]
        grid = (rows // tb,)
        row_blk = lambda width, cb: pl.BlockSpec((tb, width), lambda i: (i, cb))
        hdr_spec = pl.BlockSpec((1, HDR, GDN_CONV_CH), lambda i: (0, 0, 0))
        state_spec = pl.BlockSpec((n_chunks, GDN_H, GDN_DK, GDN_DV), lambda i: (i, 0, 0, 0))
        out_o = jax.ShapeDtypeStruct((rows, GDN_W), out_dtype)
        n_hist, hist_rows = 1, HDR + tb
    else:
        n_seq, seq_len = qkv.shape[0], qkv.shape[1]
        assert n_chunks == n_seq * chunks_per_seq and qkv.dtype == BF16
        tbs = chunk * chunks_per_seq
        grid = (seq_len // tbs,)
        row_blk = lambda width, cb: pl.BlockSpec((n_seq, tbs, width), lambda n: (0, n, cb))
        hdr_spec = pl.BlockSpec((n_seq, HDR, GDN_CONV_CH), lambda n: (0, 0, 0))
        state_spec = pl.BlockSpec((n_seq, GDN_H, GDN_DK, GDN_DV), lambda n: (0, 0, 0, 0))
        out_o = jax.ShapeDtypeStruct((n_seq, seq_len, GDN_W), out_dtype)
        n_hist, hist_rows = n_seq, 2 * HDR
    const2 = lambda *_: (0, 0)
    kern = functools.partial(_gdn_kernel, chunk=chunk, n_chunks=n_chunks, chunks_per_seq=chunks_per_seq,
                             per_chunk_state=per_chunk_state, valid_lo=valid_lo, valid_hi=valid_hi)
    return pl.pallas_call(
        kern,
        out_shape=(out_o, jax.ShapeDtypeStruct(s0.shape, F32)),
        grid=grid,
        in_specs=[row_blk(GDN_CONV_CH, 0),
                  row_blk(GDN_W, z_colblock),
                  row_blk(LANE, ba_colblock),
                  hdr_spec,
                  state_spec,
                  pl.BlockSpec((CONV_W, GDN_CONV_CH), const2),
                  pl.BlockSpec((1, LANE), const2),
                  pl.BlockSpec((1, LANE), const2),
                  pl.BlockSpec((1, GDN_DV), const2)],
        out_specs=(row_blk(GDN_W, 0), state_spec),
        scratch_shapes=[pltpu.VMEM((n_hist, hist_rows, GDN_CONV_CH), F32),
                        pltpu.VMEM((tb, GDN_CONV_CH), F32),
                        pltpu.VMEM((2, tb, LANE), F32),
                        pltpu.VMEM((tb, GDN_W), F32),
                        pltpu.VMEM((tb, GDN_H * (GDN_DV + GDN_DK)), F32),
                        pltpu.VMEM((GDN_H, tb, chunk), F32),
                        pltpu.VMEM((n_chunks, 1, LANE), F32)],
        compiler_params=_cparams(("arbitrary",)),
        name="gdn",
    )(qkv, z, ba, hdr, s0, conv_w, a_row, dt_row, norm_w)


def _t5_bucket_np(dist):
    n = np.maximum(dist, 0)
    max_exact = N_BUCKETS // 2
    nf = np.maximum(n, 1).astype(np.float32)
    large = max_exact + (np.log(nf / np.float32(max_exact)) / np.float32(math.log(MAX_DISTANCE / max_exact))
                         * np.float32(N_BUCKETS - max_exact)).astype(np.int32)
    large = np.minimum(large, N_BUCKETS - 1)
    return np.where(n < max_exact, n, large).astype(np.int32)


def _bias_prompt_kernel(code_ref, tab_ref, o_ref):
    code = code_ref[0]
    for h in range(SWA_H):
        acc = jnp.full(code.shape, -jnp.inf, F32)
        for b in range(N_BUCKETS):
            acc = jnp.where(code == b, tab_ref[b, h], acc)
        o_ref[0, h] = acc


def _bias_prompt(table):
    qi = np.arange(WINDOW)[None, :]
    sj = np.arange(WINDOW)[:, None]
    own = sj <= qi
    bucket = _t5_bucket_np(np.where(own, qi - sj, qi + WINDOW - sj))
    code_first = np.where(own, bucket, -1)
    code = jnp.asarray(np.stack([code_first, bucket]).astype(np.int32))
    return pl.pallas_call(
        _bias_prompt_kernel,
        out_shape=jax.ShapeDtypeStruct((2, SWA_H, WINDOW, WINDOW), F32),
        grid=(2,),
        in_specs=[pl.BlockSpec((1, WINDOW, WINDOW), lambda v: (v, 0, 0)),
                  pl.BlockSpec(memory_space=pltpu.SMEM)],
        out_specs=pl.BlockSpec((1, SWA_H, WINDOW, WINDOW), lambda v: (v, 0, 0, 0)),
        compiler_params=_cparams(("arbitrary",)),
        name="swa_bias_prompt",
    )(code, table)


def _bias_sample_kernel(code_ref, tab_ref, o_ref):
    kv = pl.program_id(0)
    code = code_ref[...]
    acc = jnp.full(code.shape, -jnp.inf, F32)
    for g in range(SWA_G):
        for b in range(N_BUCKETS):
            acc = jnp.where(code == b + N_BUCKETS * g, tab_ref[b, kv * SWA_G + g], acc)
    o_ref[0] = acc


def _bias_sample(table, n_tok, n_cache, n_keys_pad):
    dist = (n_cache + np.arange(n_tok))[:, None] - np.arange(n_keys_pad)[None, :]
    valid = (dist >= 0) & (dist < WINDOW) & (np.arange(n_keys_pad)[None, :] < n_cache + n_tok)
    bucket = _t5_bucket_np(dist)
    code_t = np.where(valid, bucket, -1)
    g = np.arange(SWA_G)[:, None, None]
    code = np.where(code_t[None] >= 0, code_t[None] + N_BUCKETS * g, -1)
    code = jnp.asarray(code.reshape(SWA_G * n_tok, n_keys_pad).astype(np.int32))
    return pl.pallas_call(
        _bias_sample_kernel,
        out_shape=jax.ShapeDtypeStruct((SWA_KV, SWA_G * n_tok, n_keys_pad), F32),
        grid=(SWA_KV,),
        in_specs=[pl.BlockSpec((SWA_G * n_tok, n_keys_pad), lambda k: (0, 0)),
                  pl.BlockSpec(memory_space=pltpu.SMEM)],
        out_specs=pl.BlockSpec((1, SWA_G * n_tok, n_keys_pad), lambda k: (k, 0, 0)),
        compiler_params=_cparams(("arbitrary",)),
        name="swa_bias_sample",
    )(code, table)


def _swa_prompt_kernel(q_ref, z_ref, kc_ref, kp_ref, vc_ref, vp_ref, bias_ref, sink_ref, o_ref, *, n_qblk):
    kall = jnp.concatenate([kp_ref[...], kc_ref[...]], axis=0)
    vall = jnp.concatenate([vp_ref[...], vc_ref[...]], axis=0)
    scale = SWA_DH ** -0.5
    key = lax.broadcasted_iota(jnp.int32, (WINDOW, WINDOW), 0)
    qry = lax.broadcasted_iota(jnp.int32, (WINDOW, WINDOW), 1)
    from_prev = key > qry
    first_variant = jnp.where(pl.program_id(1) == 0, 0, 1)
    cs = lambda h: slice(h * SWA_DH, (h + 1) * SWA_DH)
    rows = lambda qb: slice(qb * WINDOW, (qb + 1) * WINDOW)
    keys = lambda qb: slice(qb * WINDOW, (qb + 2) * WINDOW)
    for kv in range(SWA_KV):
        ks = slice(kv * SWA_DH, (kv + 1) * SWA_DH)
        k_kv = kall[:, ks].astype(BF16)
        v_t = vall[:, ks].astype(F32).T.astype(BF16)
        probs = [(qb, h) for qb in range(n_qblk) for h in range(kv * SWA_G, (kv + 1) * SWA_G)]
        lg = [_dot_nt(k_kv[keys(qb)], q_ref[rows(qb), cs(h)] * scale) for qb, h in probs]
        ps, dens = [], []
        for (qb, h), l in zip(probs, lg):
            bias = bias_ref[first_variant if qb == 0 else 1, h]
            l = jnp.where(from_prev, l[:WINDOW], l[WINDOW:]) + bias
            sink = sink_ref[h]
            m = jnp.maximum(jnp.max(l, axis=0, keepdims=True), sink)
            p = jnp.exp(l - m)
            dens.append(jnp.sum(p, axis=0, keepdims=True) + jnp.exp(sink - m))
            ps.append(jnp.concatenate([jnp.where(from_prev, p, 0.0), jnp.where(from_prev, 0.0, p)],
                                      axis=0))
        outs = [_dot(v_t[:, keys(qb)], p) / den for p, den, (qb, _) in zip(ps, dens, probs)]
        for j in range(0, len(probs), 2):
            qb, h = probs[j]
            two = slice(h * SWA_DH, (h + 2) * SWA_DH)
            o2 = jnp.concatenate([outs[j], outs[j + 1]], axis=0).T
            o_ref[rows(qb), two] = (o2 * _silu(z_ref[rows(qb), two].astype(F32))).astype(o_ref.dtype)


def _swa_prompt(proj, kv, bias, sinks, n_seq, seq_len, q_colblock, z_colblock, k_colblock, v_colblock,
                n_qblk=8):
    n_qblk = math.gcd(n_qblk, seq_len // WINDOW)
    tq = n_qblk * WINDOW
    steps = seq_len // tq
    cur = lambda b, n: b * steps + n
    prev = lambda b, n: (b * steps + n) * n_qblk - jnp.where(n == 0, 0, 1)
    return pl.pallas_call(
        functools.partial(_swa_prompt_kernel, n_qblk=n_qblk),
        out_shape=jax.ShapeDtypeStruct((n_seq * seq_len, SWA_QW), BF16),
        grid=(n_seq, steps),
        in_specs=[pl.BlockSpec((tq, SWA_QW), lambda b, n: (cur(b, n), q_colblock)),
                  pl.BlockSpec((tq, SWA_QW), lambda b, n: (cur(b, n), z_colblock)),
                  pl.BlockSpec((tq, SWA_KVW), lambda b, n: (cur(b, n), k_colblock)),
                  pl.BlockSpec((WINDOW, SWA_KVW), lambda b, n: (prev(b, n), k_colblock)),
                  pl.BlockSpec((tq, SWA_KVW), lambda b, n: (cur(b, n), v_colblock)),
                  pl.BlockSpec((WINDOW, SWA_KVW), lambda b, n: (prev(b, n), v_colblock)),
                  pl.BlockSpec((2, SWA_H, WINDOW, WINDOW), lambda b, n: (0, 0, 0, 0)),
                  pl.BlockSpec(memory_space=pltpu.SMEM)],
        out_specs=pl.BlockSpec((tq, SWA_QW), lambda b, n: (cur(b, n), 0)),
        compiler_params=_cparams(("parallel", "arbitrary")),
        name="swa_prompt",
    )(proj, proj, kv, kv, kv, kv, bias, sinks)


def _swa_sample_kernel(q_ref, z_ref, kn_ref, vn_ref, ck_ref, cv_ref, bias_ref, sink_ref, o_ref, nk_ref, nv_ref,
                       *, n_seq_blk, n_new):
    scale = SWA_DH ** -0.5
    wb = ck_ref.shape[3]
    probs = [(s, kv) for s in range(n_seq_blk) for kv in range(SWA_KV)]
    ks = lambda kv: slice(kv * SWA_DH, (kv + 1) * SWA_DH)
    slot = lax.broadcasted_iota(jnp.int32, (SWA_DH, wb), 1)
    tok = lax.broadcasted_iota(jnp.int32, (kn_ref.shape[1], wb), 0)
    tok_slot = lax.broadcasted_iota(jnp.int32, (kn_ref.shape[1], wb), 1)
    place = jnp.where((tok_slot == tok + (wb - n_new)) & (tok < n_new), 1.0, 0.0).astype(F32)
    for s in range(n_seq_blk):
        for cache_ref, new_ref, out_ref in ((ck_ref, kn_ref, nk_ref), (cv_ref, vn_ref, nv_ref)):
            new_t = lax.dot_general(new_ref[s], place, (((0,), (0,)), ((), ())),
                                    preferred_element_type=F32, precision=lax.Precision.HIGHEST)
            for kv in range(SWA_KV):
                out_ref[s, kv] = jnp.where(slot >= wb - n_new, new_t[ks(kv), :],
                                           pltpu.roll(cache_ref[s, kv], wb - n_new, axis=1))
    lc = [_dot(q_ref[s, kv] * scale, ck_ref[s, kv]) + bias_ref[kv, :, 0:wb] for s, kv in probs]
    ln = [_dot_nt(q_ref[s, kv] * scale, kn_ref[s, :, ks(kv)]) + bias_ref[kv, :, wb:] for s, kv in probs]
    pcs, pns, dens = [], [], []
    for (s, kv), c, n in zip(probs, lc, ln):
        sink = sink_ref[kv][:, 0:1]
        m = jnp.maximum(jnp.maximum(jnp.max(c, axis=-1, keepdims=True), jnp.max(n, axis=-1, keepdims=True)),
                        sink)
        pc = jnp.exp(c - m)
        pn = jnp.exp(n - m)
        pcs.append(pc)
        pns.append(pn)
        dens.append(jnp.sum(pc, axis=-1, keepdims=True) + jnp.sum(pn, axis=-1, keepdims=True)
                    + jnp.exp(sink - m))
    outs = [(_dot_nt(pc, cv_ref[s, kv]) + _dot(pn, vn_ref[s, :, ks(kv)])) / den
            for (s, kv), pc, pn, den in zip(probs, pcs, pns, dens)]
    for oh, (s, kv) in zip(outs, probs):
        o_ref[s, kv] = (oh * _silu(z_ref[s, kv].astype(F32))).astype(o_ref.dtype)


def _swa_sample(q, z, k_new, v_new, cache_kt, cache_vt, bias, sink_rows, n_new, n_seq_blk=8):
    bd, _, rows, _ = q.shape
    wb = cache_kt.shape[3]
    npad = k_new.shape[1]
    assert wb == LANE
    blk4 = lambda i: (i, 0, 0, 0)
    blk3 = lambda i: (i, 0, 0)
    cache_spec = pl.BlockSpec((n_seq_blk, SWA_KV, SWA_DH, wb), blk4)
    return pl.pallas_call(
        functools.partial(_swa_sample_kernel, n_seq_blk=n_seq_blk, n_new=n_new),
        out_shape=(jax.ShapeDtypeStruct(q.shape, BF16),
                   jax.ShapeDtypeStruct(cache_kt.shape, F32), jax.ShapeDtypeStruct(cache_vt.shape, F32)),
        grid=(bd // n_seq_blk,),
        in_specs=[pl.BlockSpec((n_seq_blk, SWA_KV, rows, SWA_DH), blk4),
                  pl.BlockSpec((n_seq_blk, SWA_KV, rows, SWA_DH), blk4),
                  pl.BlockSpec((n_seq_blk, npad, SWA_KVW), blk3),
                  pl.BlockSpec((n_seq_blk, npad, SWA_KVW), blk3),
                  pl.BlockSpec((n_seq_blk, SWA_KV, SWA_DH, wb), blk4),
                  pl.BlockSpec((n_seq_blk, SWA_KV, SWA_DH, wb), blk4),
                  pl.BlockSpec((SWA_KV, rows, wb + npad), lambda i: (0, 0, 0)),
                  pl.BlockSpec((SWA_KV, rows, LANE), lambda i: (0, 0, 0))],
        out_specs=(pl.BlockSpec((n_seq_blk, SWA_KV, rows, SWA_DH), blk4), cache_spec, cache_spec),
        compiler_params=_cparams(("arbitrary",)),
        name="swa_sample",
    )(q, z, k_new, v_new, cache_kt, cache_vt, bias, sink_rows)


def _mem_attend(probs, q_of, z_of, k_of, v_of, store):
    scale = MEM_DH ** -0.5
    logits = [_dot_nt(q_of(p) * scale, k_of(p)) for p in probs]
    ps, dens = [], []
    for l in logits:
        m = jnp.max(l, axis=-1, keepdims=True)
        e = jnp.exp(l - m)
        dens.append(jnp.sum(e, axis=-1, keepdims=True))
        ps.append(e)
    outs = [_dot(e, v_of(p)) / den for e, den, p in zip(ps, dens, probs)]
    for p, oh in zip(probs, outs):
        store(p, oh * _silu(z_of(p).astype(F32)))


def _mem_cols(h):
    return slice(h * MEM_DH, (h + 1) * MEM_DH)


def _mem_prompt_kernel(q_ref, z_ref, k_ref, v_ref, o_ref):
    def store(h, val):
        o_ref[:, _mem_cols(h)] = val.astype(o_ref.dtype)
    _mem_attend(range(MEM_H), lambda h: q_ref[:, _mem_cols(h)], lambda h: z_ref[:, _mem_cols(h)],
                lambda h: k_ref[:, _mem_cols(h)], lambda h: v_ref[:, _mem_cols(h)], store)


def _mem_prompt(proj, mkv, n_seq, seq_len, q_colblock, z_colblock, tq=2048):
    tq = math.gcd(tq, seq_len)
    steps = seq_len // tq
    return pl.pallas_call(
        _mem_prompt_kernel,
        out_shape=jax.ShapeDtypeStruct((n_seq * seq_len, MEM_W), BF16),
        grid=(n_seq, steps),
        in_specs=[pl.BlockSpec((tq, MEM_W), lambda b, n: (b * steps + n, q_colblock)),
                  pl.BlockSpec((tq, MEM_W), lambda b, n: (b * steps + n, z_colblock)),
                  pl.BlockSpec((N_MEM, MEM_W), lambda b, n: (b, 0)),
                  pl.BlockSpec((N_MEM, MEM_W), lambda b, n: (b, 1))],
        out_specs=pl.BlockSpec((tq, MEM_W), lambda b, n: (b * steps + n, 0)),
        compiler_params=_cparams(("parallel", "arbitrary")),
        name="mem_prompt",
    )(proj, proj, mkv, mkv)


def _mem_sample_kernel(q_ref, z_ref, k_hbm, v_hbm, o_ref, kbuf, vbuf, sem, *, n_seq_blk):
    i = pl.program_id(0)
    n_steps = pl.num_programs(0)
    slot = i % 2

    def copies(step, slot_):
        seqs = pl.ds(step * n_seq_blk, n_seq_blk)
        out = []
        for h in range(MEM_H):
            out.append(pltpu.make_async_copy(k_hbm.at[seqs, :, h, :], kbuf.at[slot_, h], sem.at[0, slot_, h]))
            out.append(pltpu.make_async_copy(v_hbm.at[seqs, :, h, :], vbuf.at[slot_, h], sem.at[1, slot_, h]))
        return out

    @pl.when(i == 0)
    def _():
        for cp in copies(0, 0):
            cp.start()

    @pl.when(i + 1 < n_steps)
    def _():
        for cp in copies(i + 1, 1 - slot):
            cp.start()

    for cp in copies(i, slot):
        cp.wait()

    def store(p, val):
        o_ref[p[0], :, _mem_cols(p[1])] = val.astype(o_ref.dtype)
    probs = [(s, h) for s in range(n_seq_blk) for h in range(MEM_H)]
    _mem_attend(probs, lambda p: q_ref[p[0], :, _mem_cols(p[1])], lambda p: z_ref[p[0], :, _mem_cols(p[1])],
                lambda p: kbuf[slot, p[1], p[0]], lambda p: vbuf[slot, p[1], p[0]], store)


def _mem_sample(q, z, cache_k, cache_v, n_seq_blk=8):
    bd, rows, _ = q.shape
    blk = lambda i: (i, 0, 0)
    buf = pltpu.VMEM((2, MEM_H, n_seq_blk, N_MEM, MEM_DH), cache_k.dtype)
    return pl.pallas_call(
        functools.partial(_mem_sample_kernel, n_seq_blk=n_seq_blk),
        out_shape=jax.ShapeDtypeStruct(q.shape, BF16),
        grid=(bd // n_seq_blk,),
        in_specs=[pl.BlockSpec((n_seq_blk, rows, MEM_W), blk),
                  pl.BlockSpec((n_seq_blk, rows, MEM_W), blk),
                  pl.BlockSpec(memory_space=pl.ANY),
                  pl.BlockSpec(memory_space=pl.ANY)],
        out_specs=pl.BlockSpec((n_seq_blk, rows, MEM_W), blk),
        scratch_shapes=[buf, buf, pltpu.SemaphoreType.DMA((2, 2, MEM_H))],
        compiler_params=_cparams(("arbitrary",)),
        name="mem_sample",
    )(q, z, cache_k, cache_v)


def _merge_kernel(og_ref, os_ref, om_ref, gate_ref, x_ref, wb_ref, wo_ref, nf_ref, y_ref):
    merged = None
    for b, o_ref in enumerate((og_ref, os_ref, om_ref)):
        t = jnp.dot(o_ref[...], wb_ref[b], preferred_element_type=F32)
        t = t * gate_ref[:, b * D_MODEL:(b + 1) * D_MODEL].astype(F32)
        merged = t if merged is None else merged + t
    h = x_ref[...] + jnp.dot(merged.astype(BF16), wo_ref[...], preferred_element_type=F32)
    ms = jnp.mean(h * h, axis=-1, keepdims=True)
    y_ref[...] = h * lax.rsqrt(ms + NORM_EPS) * nf_ref[...]


def _merge(o_gdn, o_swa, o_mem, gates, x, w_branch, w_out, norm_f, tm=256):
    m = x.shape[0]
    tm = min(tm, m)
    row = lambda i: (i, 0)
    const2 = lambda i: (0, 0)
    return pl.pallas_call(
        _merge_kernel,
        out_shape=jax.ShapeDtypeStruct((m, D_MODEL), F32),
        grid=(m // tm,),
        in_specs=[pl.BlockSpec((tm, BR_W), row),
                  pl.BlockSpec((tm, BR_W), row),
                  pl.BlockSpec((tm, BR_W), row),
                  pl.BlockSpec((tm, N_BRANCH * D_MODEL), row),
                  pl.BlockSpec((tm, D_MODEL), row),
                  pl.BlockSpec((N_BRANCH, BR_W, D_MODEL), lambda i: (0, 0, 0),
                               pipeline_mode=pl.Buffered(1)),
                  pl.BlockSpec((D_MODEL, D_MODEL), const2, pipeline_mode=pl.Buffered(1)),
                  pl.BlockSpec((1, D_MODEL), const2)],
        out_specs=pl.BlockSpec((tm, D_MODEL), row),
        compiler_params=_cparams(("parallel",)),
        name="merge",
    )(o_gdn, o_swa, o_mem, gates, x, w_branch, w_out, norm_f)


_IN_SIZES = (GDN_QK, GDN_QK, GDN_W, GDN_W, GDN_H, GDN_H, SWA_QW, SWA_KVW, SWA_KVW, SWA_QW,
             MEM_W, MEM_W, N_BRANCH * D_MODEL)
_IN_NAMES = ("gq", "gk", "gv", "gz", "gb", "ga", "sq", "sk", "sv", "sz", "mq", "mz", "mg")
_IN_SPAN = {name: (int(off), int(off + size)) for name, off, size in
            zip(_IN_NAMES, np.cumsum((0,) + _IN_SIZES[:-1]), _IN_SIZES)}


def kernel(x_prompt, x_sample, state_gdn, state_gdn_conv, cache_swa_k, cache_swa_v, cache_mem_k,
           cache_mem_v, mem_prompt, norm_in, w_in, gdn_conv_w, gdn_a_log, gdn_dt_bias, gdn_norm,
           swa_sinks, rel_bias, norm_mem, w_mem_kv, w_branch, w_out, norm_f):
    n_layers = norm_in.shape[0]
    assert n_layers == 1
    b, seq, _ = x_prompt.shape
    bd, ns, _ = x_sample.shape
    wb = cache_swa_k.shape[2]
    assert seq % WINDOW == 0 and seq % GDN_CHUNK == 0 and ns + CONV_W <= SUBLANE and wb == WINDOW
    lyr = 0

    w = w_in[lyr]
    main_names = ("gq", "gk", "gv", "gz", "sq", "sz", "mq", "mz")
    n_main = len(main_names) * W_BLK
    n_gate = N_BRANCH * D_MODEL
    n_small = 3 * LANE
    small_lead = -(n_main + n_gate) % n_small
    w_all = _wprep(w.T, [[(_IN_SPAN[a][0], W_BLK)] for a in main_names]
                   + [[(_IN_SPAN["mg"][0] + W_BLK * k, W_BLK)] for k in range(n_gate // W_BLK)]
                   + [[(None, small_lead), (_IN_SPAN["sk"][0], 2 * SWA_KVW), (_IN_SPAN["gb"][0], 2 * GDN_H)]])
    col_small = n_main + n_gate + small_lead
    cb_gz, cb_sq, cb_sz, cb_mq, cb_mz = 3, 4, 5, 6, 7
    cb_sk, cb_sv, cb_ba = 0, 1, 2
    w_mkv = w_mem_kv[lyr]
    w_br = w_branch[lyr].astype(BF16)
    w_o = w_out[lyr].astype(BF16)
    nw_in = norm_in[lyr].reshape(1, D_MODEL)
    nw_mem = norm_mem[lyr].reshape(1, D_MODEL)
    nw_f = norm_f.reshape(1, D_MODEL)
    conv_w = gdn_conv_w[lyr]
    a_row = jnp.pad(gdn_a_log[lyr].reshape(1, GDN_H), ((0, 0), (GDN_H, LANE - 2 * GDN_H)))
    dt_row = jnp.pad(gdn_dt_bias[lyr].reshape(1, GDN_H), ((0, 0), (GDN_H, LANE - 2 * GDN_H)))
    gnw = gdn_norm[lyr].reshape(1, GDN_DV)
    sinks = swa_sinks[lyr]
    bias_p = _bias_prompt(rel_bias)
    npad = SUBLANE
    bias_s = _bias_sample(rel_bias, ns, wb, wb + npad)
    sink_rows = jnp.broadcast_to(jnp.repeat(sinks.reshape(SWA_KV, SWA_G), ns, axis=1)[:, :, None],
                                 (SWA_KV, SWA_G * ns, LANE))

    t = b * seq
    xp = x_prompt.reshape(t, D_MODEL)
    xn_p, p_small = _rmsnorm(xp, nw_in, w_all, col_small, n_small)
    p_main = _proj(xn_p, w_all, BF16, n=n_main)
    g_p = _proj(xn_p, w_all, BF16, act="sigmoid", col0=n_main, n=n_gate)
    xn_tail = xn_p.reshape(b, seq, D_MODEL)[:, seq - SUBLANE:, :].reshape(b * SUBLANE, D_MODEL)
    conv_p = _proj(xn_tail, w_all, F32, n=GDN_CONV_CH).reshape(b, SUBLANE, GDN_CONV_CH)[:, SUBLANE - (CONV_W - 1):]
    kv_tail = p_small.reshape(b, seq, 3 * LANE)[:, seq - WINDOW:]
    swk_p = kv_tail[:, :, cb_sk * LANE:(cb_sk + 1) * LANE].reshape(b, WINDOW, SWA_KV, SWA_DH)
    swv_p = kv_tail[:, :, cb_sv * LANE:(cb_sv + 1) * LANE].reshape(b, WINDOW, SWA_KV, SWA_DH)

    mkv = _proj(_rmsnorm(mem_prompt.reshape(b * N_MEM, D_MODEL), nw_mem), w_mkv, F32)
    mk_p = mkv[:, :MEM_W].reshape(b, N_MEM, MEM_W)
    mv_p = mkv[:, MEM_W:].reshape(b, N_MEM, MEM_W)

    p_main3 = p_main.reshape(b, seq, n_main)
    gdn_cps = 4
    o_gdn_p, s_p = _gdn(p_main3, p_main3, p_small.reshape(b, seq, n_small),
                        jnp.zeros((b, HDR, GDN_CONV_CH), F32),
                        jnp.zeros((b, GDN_H, GDN_DK, GDN_DV), F32), conv_w, a_row, dt_row, gnw,
                        chunk=GDN_CHUNK, n_chunks=gdn_cps * b, chunks_per_seq=gdn_cps,
                        per_chunk_state=False, valid_lo=0, valid_hi=GDN_CHUNK, out_dtype=BF16,
                        z_colblock=cb_gz, ba_colblock=cb_ba)
    o_gdn_p = o_gdn_p.reshape(t, GDN_W)
    o_swa_p = _swa_prompt(p_main, p_small, bias_p, sinks, b, seq, cb_sq, cb_sz, cb_sk, cb_sv)
    o_mem_p = _mem_prompt(p_main, mkv, b, seq, cb_mq, cb_mz)
    y_p = _merge(o_gdn_p, o_swa_p, o_mem_p, g_p, xp, w_br, w_o, nw_f).reshape(b, seq, D_MODEL)

    ts = bd * ns
    xs = x_sample.reshape(ts, D_MODEL)
    xn_s, s_small = _rmsnorm(xs, nw_in, w_all, col_small, n_small)
    s_main = _proj(xn_s, w_all, F32, n=n_main)
    g_s = _proj(xn_s, w_all, BF16, act="sigmoid", col0=n_main, n=n_gate)
    s_gdn = s_main[:, :GDN_CONV_CH + GDN_W]
    s_ba = s_small[:, cb_ba * LANE:(cb_ba + 1) * LANE]
    s_swa = jnp.concatenate([s_main[:, cb_sq * BR_W:(cb_sz + 1) * BR_W], s_small[:, :2 * LANE]], axis=1)
    s_mem = s_main[:, cb_mq * BR_W:(cb_mz + 1) * BR_W].astype(BF16)

    lo = CONV_W - 1
    hi = lo + ns
    pad_rows = ((0, 0), (lo, SUBLANE - hi), (0, 0))
    e_qkv = jnp.concatenate([state_gdn_conv[lyr], s_gdn[:, :GDN_CONV_CH].reshape(bd, ns, GDN_CONV_CH),
                             jnp.zeros((bd, SUBLANE - hi, GDN_CONV_CH), F32)], axis=1)
    e_z = jnp.pad(s_gdn[:, GDN_CONV_CH:].reshape(bd, ns, GDN_W), pad_rows)
    e_ba = jnp.pad(s_ba.reshape(bd, ns, LANE), pad_rows)
    seq_blk = 16
    o_gdn_s8, s_s = _gdn(e_qkv.reshape(bd * SUBLANE, GDN_CONV_CH), e_z.reshape(bd * SUBLANE, GDN_W),
                         e_ba.reshape(bd * SUBLANE, LANE), jnp.zeros((1, HDR, GDN_CONV_CH), F32),
                         state_gdn[lyr], conv_w, a_row, dt_row, gnw,
                         chunk=SUBLANE, n_chunks=seq_blk, chunks_per_seq=1, per_chunk_state=True,
                         valid_lo=lo, valid_hi=hi, out_dtype=BF16)
    o_gdn_s = o_gdn_s8.reshape(bd, SUBLANE, GDN_W)[:, lo:hi].reshape(ts, GDN_W)
    conv_s = e_qkv[:, hi - (CONV_W - 1):hi]

    def to_heads(a):
        return a.reshape(bd, ns, SWA_KV, SWA_G, SWA_DH).transpose(0, 2, 3, 1, 4).reshape(
            bd, SWA_KV, SWA_G * ns, SWA_DH)

    k_new = s_swa[:, 2 * SWA_QW:2 * SWA_QW + SWA_KVW].reshape(bd, ns, SWA_KVW)
    v_new = s_swa[:, 2 * SWA_QW + SWA_KVW:].reshape(bd, ns, SWA_KVW)
    tok_pad = ((0, 0), (0, npad - ns), (0, 0))
    ck_t = cache_swa_k[lyr].transpose(0, 2, 3, 1)
    cv_t = cache_swa_v[lyr].transpose(0, 2, 3, 1)
    o_swa_h, nk_t, nv_t = _swa_sample(
        to_heads(s_swa[:, :SWA_QW]).astype(BF16), to_heads(s_swa[:, SWA_QW:2 * SWA_QW]),
        jnp.pad(k_new, tok_pad), jnp.pad(v_new, tok_pad), ck_t, cv_t, bias_s, sink_rows, n_new=ns)
    o_swa_s = o_swa_h.reshape(bd, SWA_KV, SWA_G, ns, SWA_DH).transpose(0, 3, 1, 2, 4).reshape(ts, SWA_QW)
    swk_s = nk_t.transpose(0, 3, 1, 2)
    swv_s = nv_t.transpose(0, 3, 1, 2)

    mq = jnp.pad(s_mem[:, :MEM_W].reshape(bd, ns, MEM_W), tok_pad)
    mz = jnp.pad(s_mem[:, MEM_W:].reshape(bd, ns, MEM_W), tok_pad)
    o_mem_s = _mem_sample(mq, mz, cache_mem_k[lyr], cache_mem_v[lyr])[:, :ns].reshape(ts, MEM_W)
    y_s = _merge(o_gdn_s, o_swa_s, o_mem_s, g_s, xs, w_br, w_o, nw_f).reshape(bd, ns, D_MODEL)

    return (y_p, y_s,
            s_p[None], conv_p[None], swk_p[None], swv_p[None],
            mk_p.reshape(b, N_MEM, MEM_H, MEM_DH)[None], mv_p.reshape(b, N_MEM, MEM_H, MEM_DH)[None],
            s_s[None], conv_s[None], swk_s[None], swv_s[None])
```

```python
import functools
import math

import numpy as np
import jax
import jax.numpy as jnp
from jax import lax
from jax.experimental import pallas as pl
from jax.experimental.pallas import tpu as pltpu

F32 = jnp.float32
BF16 = jnp.bfloat16

D_MODEL = 2048
N_BRANCH = 3
BR_W = 1024
GDN_H = 8
GDN_DK = 128
GDN_DV = 128
GDN_QK = GDN_H * GDN_DK
GDN_W = GDN_H * GDN_DV
GDN_CONV_CH = 2 * GDN_QK + GDN_W
CONV_W = 4
GDN_CHUNK = 64
SWA_H = 16
SWA_KV = 2
SWA_G = SWA_H // SWA_KV
SWA_DH = 64
SWA_QW = SWA_H * SWA_DH
SWA_KVW = SWA_KV * SWA_DH
WINDOW = 128
N_BUCKETS = 32
MAX_DISTANCE = 128
N_MEM = 256
MEM_H = 4
MEM_DH = 256
MEM_W = MEM_H * MEM_DH
NORM_EPS = 1e-6

LANE = 128
SUBLANE = 8
VMEM_LIMIT = 52 * 1024 * 1024


def _cparams(sem):
    return pltpu.CompilerParams(dimension_semantics=sem, vmem_limit_bytes=VMEM_LIMIT)


def _sigmoid(x):
    return 0.5 * jnp.tanh(0.5 * x) + 0.5


def _silu(x):
    h = 0.5 * x
    return h * jnp.tanh(h) + h


def _softplus(x):
    return jnp.maximum(x, 0.0) + jnp.log(1.0 + jnp.exp(-jnp.abs(x)))


def _dot(a, b):
    return jnp.dot(a.astype(BF16), b.astype(BF16), preferred_element_type=F32)


def _dot_nt(a, b):
    return lax.dot_general(a.astype(BF16), b.astype(BF16), (((1,), (1,)), ((), ())),
                           preferred_element_type=F32)


def _dot_tn(a, b):
    return lax.dot_general(a.astype(BF16), b.astype(BF16), (((0,), (0,)), ((), ())),
                           preferred_element_type=F32)


def _dot_f32(a, b):
    return jnp.dot(a, b, preferred_element_type=F32, precision=lax.Precision.HIGHEST)


def _rmsnorm_kernel(x_ref, nw_ref, *rest):
    x = x_ref[...].astype(F32)
    ms = jnp.mean(x * x, axis=-1, keepdims=True)
    xn = (x * lax.rsqrt(ms + NORM_EPS) * nw_ref[...]).astype(BF16)
    if len(rest) == 1:
        rest[0][...] = xn
    else:
        w_ref, o_ref, p_ref = rest
        o_ref[...] = xn
        p_ref[...] = jnp.dot(xn, w_ref[...], preferred_element_type=F32)


def _rmsnorm(x, norm_w, w=None, col0=0, n=None, tm_pref=1024):
    m, d = x.shape
    tm = min(m, tm_pref)
    assert m % tm == 0
    row = lambda i: (i, 0)
    in_specs = [pl.BlockSpec((tm, d), row), pl.BlockSpec((1, d), lambda i: (0, 0))]
    out_shape = jax.ShapeDtypeStruct((m, d), BF16)
    out_specs = pl.BlockSpec((tm, d), row)
    args = (x, norm_w)
    if w is not None:
        assert col0 % n == 0
        in_specs.append(pl.BlockSpec((d, n), lambda i: (0, col0 // n)))
        out_shape = (out_shape, jax.ShapeDtypeStruct((m, n), F32))
        out_specs = (out_specs, pl.BlockSpec((tm, n), row))
        args = args + (w,)
    return pl.pallas_call(
        _rmsnorm_kernel,
        out_shape=out_shape,
        grid=(m // tm,),
        in_specs=in_specs,
        out_specs=out_specs,
        compiler_params=_cparams(("parallel",)),
        name="rmsnorm",
    )(*args)


def _proj_kernel(x_ref, w_ref, o_ref, *, act):
    y = jnp.dot(x_ref[...], w_ref[...].astype(BF16), preferred_element_type=F32)
    if act == "sigmoid":
        y = _sigmoid(y)
    o_ref[...] = y.astype(o_ref.dtype)


def _pick_tile(n, pref):
    t = min(n, pref)
    while n % t:
        t -= LANE
    return t


def _proj(xn, w, out_dtype, act=None, col0=0, n=None, tm_pref=1024, tn_pref=2048):
    m, d = xn.shape
    n = w.shape[1] if n is None else n
    tm = min(m, tm_pref)
    assert m % tm == 0
    tn = _pick_tile(n, tn_pref if out_dtype == BF16 else tn_pref // 2)
    while col0 % tn:
        tn = _pick_tile(n, tn - LANE)
    jb = col0 // tn
    return pl.pallas_call(
        functools.partial(_proj_kernel, act=act),
        out_shape=jax.ShapeDtypeStruct((m, n), out_dtype),
        grid=(m // tm, n // tn),
        in_specs=[pl.BlockSpec((tm, d), lambda i, j: (i, 0)),
                  pl.BlockSpec((d, tn), lambda i, j: (0, jb + j))],
        out_specs=pl.BlockSpec((tm, tn), lambda i, j: (i, j)),
        compiler_params=_cparams(("parallel", "arbitrary")),
        name="proj",
    )(xn, w)


W_BLK = 1024
W_PIECE_ROWS = (W_BLK, 2 * SUBLANE)


def _wprep_kernel(wt_hbm, o_ref, inbuf0, inbuf1, sem, *, blocks):
    j = pl.program_id(0)
    inbuf = (inbuf0, inbuf1)

    def sources(b):
        return [(start, rows) for start, rows in blocks[b] if start is not None]

    def pieces(b):
        return [pltpu.make_async_copy(wt_hbm.at[pl.ds(start, rows), :],
                                      inbuf[k].at[b % 2, pl.ds(0, rows), :], sem.at[b % 2, k])
                for k, (start, rows) in enumerate(sources(b))]

    for b, ranges in enumerate(blocks):
        @pl.when(j == b)
        def _(b=b, ranges=ranges):
            if b == 0:
                for cp in pieces(0):
                    cp.start()
            if b + 1 < len(blocks):
                for cp in pieces(b + 1):
                    cp.start()
            for cp in pieces(b):
                cp.wait()
            n_have = sum(rows for _, rows in ranges)
            parts, k = [], 0
            for start, rows in tuple(ranges) + ((None, W_BLK - n_have),):
                if start is None:
                    if rows:
                        parts.append(jnp.zeros((rows, inbuf0.shape[2]), inbuf0.dtype))
                else:
                    parts.append(inbuf[k][b % 2, 0:rows, :])
                    k += 1
            val = parts[0] if len(parts) == 1 else jnp.concatenate(parts, axis=0)
            o_ref[...] = val.T.astype(BF16)


def _wprep(wt, block_srcs):
    ncol, d = wt.shape
    for ranges in block_srcs:
        srcs = [(s, rows) for s, rows in ranges if s is not None]
        assert len(srcs) <= len(W_PIECE_ROWS) and all(rows % SUBLANE == 0 for _, rows in ranges)
        for (s, rows), cap in zip(srcs, W_PIECE_ROWS):
            assert s % SUBLANE == 0 and rows <= cap and s + rows <= ncol
    return pl.pallas_call(
        functools.partial(_wprep_kernel, blocks=tuple(tuple(r) for r in block_srcs)),
        out_shape=jax.ShapeDtypeStruct((d, W_BLK * len(block_srcs)), BF16),
        grid=(len(block_srcs),),
        in_specs=[pl.BlockSpec(memory_space=pl.ANY)],
        out_specs=pl.BlockSpec((d, W_BLK), lambda j: (0, j)),
        scratch_shapes=[pltpu.VMEM((2, cap, d), wt.dtype) for cap in W_PIECE_ROWS]
                       + [pltpu.SemaphoreType.DMA((2, len(W_PIECE_ROWS)))],
        compiler_params=_cparams(("arbitrary",)),
        name="wprep",
    )(wt)


HDR = SUBLANE


def _tri_inv_many(a_list, c):
    row = lax.broadcasted_iota(jnp.int32, (c, c), 0)
    col = lax.broadcasted_iota(jnp.int32, (c, c), 1)
    eye = jnp.where(row == col, 1.0, 0.0).astype(F32)
    xs = [eye - a for a in a_list]
    bs = [_dot(a, a) for a in a_list]
    n = 2
    while n < c:
        xs = [x + _dot(x, b) for x, b in zip(xs, bs)]
        n *= 2
        if n < c:
            bs = [_dot(b, b) for b in bs]
    return xs


def _gdn_kernel(qkv_ref, z_ref, ba_ref, hdr_ref, s0_ref, cw_ref, arow_ref, dtrow_ref, nw_ref,
                o_ref, s_ref, buf_ref, cv_ref, gb_ref, oacc_ref, sol_ref, att_ref, gl_ref,
                *, chunk, n_chunks, chunks_per_seq, per_chunk_state, valid_lo, valid_hi):
    c = chunk
    tb = c * n_chunks
    cps = chunks_per_seq
    n_seqs = n_chunks // cps
    tbs = cps * c

    if per_chunk_state:
        buf_ref[0, 0:2 * HDR, :] = jnp.zeros((2 * HDR, GDN_CONV_CH), F32)
    else:
        @pl.when(pl.program_id(0) == 0)
        def _():
            for s in range(n_seqs):
                buf_ref[s, 0:HDR, :] = hdr_ref[s]
                buf_ref[s, HDR:2 * HDR, :] = jnp.zeros((HDR, GDN_CONV_CH), F32)
            s_ref[...] = s0_ref[...]

    def tap_from_history(s, j, n_rows):
        off = HDR - (CONV_W - 1) + j
        return buf_ref[s, off:off + n_rows, :] * cw_ref[j:j + 1, :]

    if qkv_ref.dtype == BF16:
        r = lax.broadcasted_iota(jnp.int32, (tbs, tbs), 0)
        cc = lax.broadcasted_iota(jnp.int32, (tbs, tbs), 1)
        for s in range(n_seqs):
            xb = qkv_ref[s]
            acc = xb.astype(F32) * cw_ref[CONV_W - 1:CONV_W, :]
            for j in range(CONV_W - 1):
                shift = jnp.where(r - cc == CONV_W - 1 - j, 1.0, 0.0).astype(BF16)
                acc = acc + jnp.dot(shift, xb, preferred_element_type=F32) * cw_ref[j:j + 1, :]
            cv_ref[s * tbs:(s + 1) * tbs, :] = _silu(acc)
            top = acc[0:HDR]
            for j in range(CONV_W - 1):
                top = top + tap_from_history(s, j, HDR)
            cv_ref[s * tbs:s * tbs + HDR, :] = _silu(top)
            buf_ref[s, 0:HDR, :] = qkv_ref[s, tbs - 2 * HDR:tbs, :].astype(F32)[HDR:]
    else:
        buf_ref[0, HDR:HDR + tb, :] = qkv_ref[...].astype(F32)
        acc = None
        for j in range(CONV_W):
            term = tap_from_history(0, j, tb)
            acc = term if acc is None else acc + term
        cv_ref[...] = _silu(acc)

    ba = ba_ref[...].astype(F32).reshape(tb, LANE)
    beta_all = _sigmoid(ba)
    g_all = -jnp.exp(arow_ref[...]) * _softplus(ba + dtrow_ref[...])
    if per_chunk_state:
        r = lax.broadcasted_iota(jnp.int32, (tb, LANE), 0) & (c - 1)
        valid = (r >= valid_lo) & (r < valid_hi)
        beta_all = jnp.where(valid, beta_all, 0.0)
        g_all = jnp.where(valid, g_all, 0.0)
    gb_ref[0] = beta_all
    gb_ref[1] = g_all

    row = lax.broadcasted_iota(jnp.int32, (c, c), 0)
    col = lax.broadcasted_iota(jnp.int32, (c, c), 1)
    causal = row >= col
    strict = row > col
    tril = jnp.where(causal, 1.0, 0.0).astype(F32)
    scale_q = GDN_DK ** -0.5

    heads = range(GDN_H)
    qcol = lambda h: slice(h * GDN_DK, (h + 1) * GDN_DK)
    kcol = lambda h: slice(GDN_QK + h * GDN_DK, GDN_QK + (h + 1) * GDN_DK)
    vcol = lambda h: slice(2 * GDN_QK + h * GDN_DV, 2 * GDN_QK + (h + 1) * GDN_DV)
    ucol = lambda h: slice(h * (GDN_DV + GDN_DK), h * (GDN_DV + GDN_DK) + GDN_DV)
    wcol = lambda h: slice(h * (GDN_DV + GDN_DK) + GDN_DV, (h + 1) * (GDN_DV + GDN_DK))
    if per_chunk_state:
        rv = lax.broadcasted_iota(jnp.int32, (c, 1), 0)
        rvalid = (rv >= valid_lo) & (rv < valid_hi)

    def chunk_rows(ci):
        return slice(ci * c, (ci + 1) * c)

    def prep(chunks):
        probs = []
        for ci in chunks:
            rows = chunk_rows(ci)
            beta_c = gb_ref[0, rows, :]
            gc_all = _dot_f32(tril, gb_ref[1, rows, :])
            gc_t = gc_all.T
            gl_ref[ci] = gc_all[c - 1:c, :]
            for h in heads:
                q = cv_ref[rows, qcol(h)]
                k = cv_ref[rows, kcol(h)]
                v = cv_ref[rows, vcol(h)]
                q = q * lax.rsqrt(jnp.sum(q * q, axis=-1, keepdims=True) + NORM_EPS) * scale_q
                k = k * lax.rsqrt(jnp.sum(k * k, axis=-1, keepdims=True) + NORM_EPS)
                if per_chunk_state:
                    q = jnp.where(rvalid, q, 0.0)
                    k = jnp.where(rvalid, k, 0.0)
                    v = jnp.where(rvalid, v, 0.0)
                beta = beta_c[:, h:h + 1]
                gc_col = gc_all[:, GDN_H + h:GDN_H + h + 1]
                gc_row = gc_t[GDN_H + h:GDN_H + h + 1, :]
                decay = jnp.where(causal, jnp.exp(jnp.minimum(gc_col - gc_row, 0.0)), 0.0)
                e_gc = jnp.exp(gc_col)
                kb = k * beta
                rhs = jnp.concatenate([v * beta, kb * e_gc], axis=1)
                cv_ref[rows, qcol(h)] = q * e_gc
                cv_ref[rows, kcol(h)] = k * jnp.exp(gc_col[c - 1:c, :] - gc_col)
                probs.append((rows, h, q, k, kb, rhs, decay))
        kq = [_dot_nt(jnp.concatenate([kb, q], axis=0), k) for (_, _, q, k, kb, _, _) in probs]
        a_list = [jnp.where(strict, kq_i[:c] * p[6], 0.0) for kq_i, p in zip(kq, probs)]
        t_inv = _tri_inv_many(a_list, c)
        for t_i, kq_i, (rows, h, _, _, _, rhs, decay) in zip(t_inv, kq, probs):
            sol_ref[rows, h * (GDN_DV + GDN_DK):(h + 1) * (GDN_DV + GDN_DK)] = _dot(t_i, rhs)
            att_ref[h, rows, :] = kq_i[c:] * decay

    def scan(chunks):
        probs = [(ci, h) for ci in chunks for h in heads]
        rows = chunk_rows
        si = lambda ci: ci // cps
        g_tot = {ci: jnp.exp(gl_ref[ci]) for ci in chunks}
        s_old = [s_ref[si(ci), h] for ci, h in probs]
        wq_s = [_dot(jnp.concatenate([sol_ref[rows(ci), wcol(h)], cv_ref[rows(ci), qcol(h)]], axis=0), s)
                for (ci, h), s in zip(probs, s_old)]
        v_new = [sol_ref[rows(ci), ucol(h)] - w[:c] for (ci, h), w in zip(probs, wq_s)]
        for (ci, h), s, v in zip(probs, s_old, v_new):
            s_ref[si(ci), h] = (s * g_tot[ci][:, GDN_H + h:GDN_H + h + 1]
                                + _dot_tn(cv_ref[rows(ci), kcol(h)], v))
        o_att = [_dot(att_ref[h, rows(ci), :], v) for (ci, h), v in zip(probs, v_new)]
        for (ci, h), w, oa in zip(probs, wq_s, o_att):
            o = w[c:] + oa
            o = o * lax.rsqrt(jnp.mean(o * o, axis=-1, keepdims=True) + NORM_EPS) * nw_ref[...]
            oacc_ref[rows(ci), h * GDN_DV:(h + 1) * GDN_DV] = o

    if per_chunk_state:
        s_ref[...] = s0_ref[...]
        prep(range(n_chunks))
        scan(range(n_chunks))
    else:
        for s in range(n_seqs):
            prep(range(s * cps, (s + 1) * cps))
        for j in range(cps):
            scan([s * cps + j for s in range(n_seqs)])
    gated = oacc_ref[...] * _silu(z_ref[...].astype(F32).reshape(tb, GDN_W))
    o_ref[...] = gated.astype(o_ref.dtype).reshape(o_ref.shape)


def _gdn(qkv, z, ba, hdr, s0, conv_w, a_row, dt_row, norm_w, *, chunk, n_chunks, chunks_per_seq,
         per_chunk_state, valid_lo, valid_hi, out_dtype, z_colblock=0, ba_colblock=0):
    tb = chunk * n_chunks
    if per_chunk_state:
        rows = qkv.shape[0]
        grid = (rows // tb,)
        row_blk = lambda width, cb: pl.BlockSpec((tb, width), lambda i: (i, cb))
        hdr_spec = pl.BlockSpec((1, HDR, GDN_CONV_CH), lambda i: (0, 0, 0))
        state_spec = pl.BlockSpec((n_chunks, GDN_H, GDN_DK, GDN_DV), lambda i: (i, 0, 0, 0))
        out_o = jax.ShapeDtypeStruct((rows, GDN_W), out_dtype)
        n_hist, hist_rows = 1, HDR + tb
    else:
        n_seq, seq_len = qkv.shape[0], qkv.shape[1]
        assert n_chunks == n_seq * chunks_per_seq and qkv.dtype == BF16
        tbs = chunk * chunks_per_seq
        grid = (seq_len // tbs,)
        row_blk = lambda width, cb: pl.BlockSpec((n_seq, tbs, width), lambda n: (0, n, cb))
        hdr_spec = pl.BlockSpec((n_seq, HDR, GDN_CONV_CH), lambda n: (0, 0, 0))
        state_spec = pl.BlockSpec((n_seq, GDN_H, GDN_DK, GDN_DV), lambda n: (0, 0, 0, 0))
        out_o = jax.ShapeDtypeStruct((n_seq, seq_len, GDN_W), out_dtype)
        n_hist, hist_rows = n_seq, 2 * HDR
    const2 = lambda *_: (0, 0)
    kern = functools.partial(_gdn_kernel, chunk=chunk, n_chunks=n_chunks, chunks_per_seq=chunks_per_seq,
                             per_chunk_state=per_chunk_state, valid_lo=valid_lo, valid_hi=valid_hi)
    return pl.pallas_call(
        kern,
        out_shape=(out_o, jax.ShapeDtypeStruct(s0.shape, F32)),
        grid=grid,
        in_specs=[row_blk(GDN_CONV_CH, 0),
                  row_blk(GDN_W, z_colblock),
                  row_blk(LANE, ba_colblock),
                  hdr_spec,
                  state_spec,
                  pl.BlockSpec((CONV_W, GDN_CONV_CH), const2),
                  pl.BlockSpec((1, LANE), const2),
                  pl.BlockSpec((1, LANE), const2),
                  pl.BlockSpec((1, GDN_DV), const2)],
        out_specs=(row_blk(GDN_W, 0), state_spec),
        scratch_shapes=[pltpu.VMEM((n_hist, hist_rows, GDN_CONV_CH), F32),
                        pltpu.VMEM((tb, GDN_CONV_CH), F32),
                        pltpu.VMEM((2, tb, LANE), F32),
                        pltpu.VMEM((tb, GDN_W), F32),
                        pltpu.VMEM((tb, GDN_H * (GDN_DV + GDN_DK)), F32),
                        pltpu.VMEM((GDN_H, tb, chunk), F32),
                        pltpu.VMEM((n_chunks, 1, LANE), F32)],
        compiler_params=_cparams(("arbitrary",)),
        name="gdn",
    )(qkv, z, ba, hdr, s0, conv_w, a_row, dt_row, norm_w)


def _t5_bucket_np(dist):
    n = np.maximum(dist, 0)
    max_exact = N_BUCKETS // 2
    nf = np.maximum(n, 1).astype(np.float32)
    large = max_exact + (np.log(nf / np.float32(max_exact)) / np.float32(math.log(MAX_DISTANCE / max_exact))
                         * np.float32(N_BUCKETS - max_exact)).astype(np.int32)
    large = np.minimum(large, N_BUCKETS - 1)
    return np.where(n < max_exact, n, large).astype(np.int32)


def _bias_prompt_kernel(code_ref, tab_ref, o_ref):
    code = code_ref[0]
    for h in range(SWA_H):
        acc = jnp.full(code.shape, -jnp.inf, F32)
        for b in range(N_BUCKETS):
            acc = jnp.where(code == b, tab_ref[b, h], acc)
        o_ref[0, h] = acc


def _bias_prompt(table):
    qi = np.arange(WINDOW)[None, :]
    sj = np.arange(WINDOW)[:, None]
    own = sj <= qi
    bucket = _t5_bucket_np(np.where(own, qi - sj, qi + WINDOW - sj))
    code_first = np.where(own, bucket, -1)
    code = jnp.asarray(np.stack([code_first, bucket]).astype(np.int32))
    return pl.pallas_call(
        _bias_prompt_kernel,
        out_shape=jax.ShapeDtypeStruct((2, SWA_H, WINDOW, WINDOW), F32),
        grid=(2,),
        in_specs=[pl.BlockSpec((1, WINDOW, WINDOW), lambda v: (v, 0, 0)),
                  pl.BlockSpec(memory_space=pltpu.SMEM)],
        out_specs=pl.BlockSpec((1, SWA_H, WINDOW, WINDOW), lambda v: (v, 0, 0, 0)),
        compiler_params=_cparams(("arbitrary",)),
        name="swa_bias_prompt",
    )(code, table)


def _bias_sample_kernel(code_ref, tab_ref, o_ref):
    kv = pl.program_id(0)
    code = code_ref[...]
    acc = jnp.full(code.shape, -jnp.inf, F32)
    for g in range(SWA_G):
        for b in range(N_BUCKETS):
            acc = jnp.where(code == b + N_BUCKETS * g, tab_ref[b, kv * SWA_G + g], acc)
    o_ref[0] = acc


def _bias_sample(table, n_tok, n_cache, n_keys_pad):
    dist = (n_cache + np.arange(n_tok))[:, None] - np.arange(n_keys_pad)[None, :]
    valid = (dist >= 0) & (dist < WINDOW) & (np.arange(n_keys_pad)[None, :] < n_cache + n_tok)
    bucket = _t5_bucket_np(dist)
    code_t = np.where(valid, bucket, -1)
    g = np.arange(SWA_G)[:, None, None]
    code = np.where(code_t[None] >= 0, code_t[None] + N_BUCKETS * g, -1)
    code = jnp.asarray(code.reshape(SWA_G * n_tok, n_keys_pad).astype(np.int32))
    return pl.pallas_call(
        _bias_sample_kernel,
        out_shape=jax.ShapeDtypeStruct((SWA_KV, SWA_G * n_tok, n_keys_pad), F32),
        grid=(SWA_KV,),
        in_specs=[pl.BlockSpec((SWA_G * n_tok, n_keys_pad), lambda k: (0, 0)),
                  pl.BlockSpec(memory_space=pltpu.SMEM)],
        out_specs=pl.BlockSpec((1, SWA_G * n_tok, n_keys_pad), lambda k: (k, 0, 0)),
        compiler_params=_cparams(("arbitrary",)),
        name="swa_bias_sample",
    )(code, table)


def _swa_prompt_kernel(q_ref, z_ref, kc_ref, kp_ref, vc_ref, vp_ref, bias_ref, sink_ref, o_ref, *, n_qblk):
    kall = jnp.concatenate([kp_ref[...], kc_ref[...]], axis=0)
    vall = jnp.concatenate([vp_ref[...], vc_ref[...]], axis=0)
    scale = SWA_DH ** -0.5
    key = lax.broadcasted_iota(jnp.int32, (WINDOW, WINDOW), 0)
    qry = lax.broadcasted_iota(jnp.int32, (WINDOW, WINDOW), 1)
    from_prev = key > qry
    first_variant = jnp.where(pl.program_id(1) == 0, 0, 1)
    cs = lambda h: slice(h * SWA_DH, (h + 1) * SWA_DH)
    rows = lambda qb: slice(qb * WINDOW, (qb + 1) * WINDOW)
    keys = lambda qb: slice(qb * WINDOW, (qb + 2) * WINDOW)
    for kv in range(SWA_KV):
        ks = slice(kv * SWA_DH, (kv + 1) * SWA_DH)
        k_kv = kall[:, ks].astype(BF16)
        v_t = vall[:, ks].astype(F32).T.astype(BF16)
        probs = [(qb, h) for qb in range(n_qblk) for h in range(kv * SWA_G, (kv + 1) * SWA_G)]
        lg = [_dot_nt(k_kv[keys(qb)], q_ref[rows(qb), cs(h)] * scale) for qb, h in probs]
        ps, dens = [], []
        for (qb, h), l in zip(probs, lg):
            bias = bias_ref[first_variant if qb == 0 else 1, h]
            l = jnp.where(from_prev, l[:WINDOW], l[WINDOW:]) + bias
            sink = sink_ref[h]
            m = jnp.maximum(jnp.max(l, axis=0, keepdims=True), sink)
            p = jnp.exp(l - m)
            dens.append(jnp.sum(p, axis=0, keepdims=True) + jnp.exp(sink - m))
            ps.append(jnp.concatenate([jnp.where(from_prev, p, 0.0), jnp.where(from_prev, 0.0, p)],
                                      axis=0))
        outs = [_dot(v_t[:, keys(qb)], p) / den for p, den, (qb, _) in zip(ps, dens, probs)]
        for j in range(0, len(probs), 2):
            qb, h = probs[j]
            two = slice(h * SWA_DH, (h + 2) * SWA_DH)
            o2 = jnp.concatenate([outs[j], outs[j + 1]], axis=0).T
            o_ref[rows(qb), two] = (o2 * _silu(z_ref[rows(qb), two].astype(F32))).astype(o_ref.dtype)


def _swa_prompt(proj, kv, bias, sinks, n_seq, seq_len, q_colblock, z_colblock, k_colblock, v_colblock,
                n_qblk=8):
    n_qblk = math.gcd(n_qblk, seq_len // WINDOW)
    tq = n_qblk * WINDOW
    steps = seq_len // tq
    cur = lambda b, n: b * steps + n
    prev = lambda b, n: (b * steps + n) * n_qblk - jnp.where(n == 0, 0, 1)
    return pl.pallas_call(
        functools.partial(_swa_prompt_kernel, n_qblk=n_qblk),
        out_shape=jax.ShapeDtypeStruct((n_seq * seq_len, SWA_QW), BF16),
        grid=(n_seq, steps),
        in_specs=[pl.BlockSpec((tq, SWA_QW), lambda b, n: (cur(b, n), q_colblock)),
                  pl.BlockSpec((tq, SWA_QW), lambda b, n: (cur(b, n), z_colblock)),
                  pl.BlockSpec((tq, SWA_KVW), lambda b, n: (cur(b, n), k_colblock)),
                  pl.BlockSpec((WINDOW, SWA_KVW), lambda b, n: (prev(b, n), k_colblock)),
                  pl.BlockSpec((tq, SWA_KVW), lambda b, n: (cur(b, n), v_colblock)),
                  pl.BlockSpec((WINDOW, SWA_KVW), lambda b, n: (prev(b, n), v_colblock)),
                  pl.BlockSpec((2, SWA_H, WINDOW, WINDOW), lambda b, n: (0, 0, 0, 0)),
                  pl.BlockSpec(memory_space=pltpu.SMEM)],
        out_specs=pl.BlockSpec((tq, SWA_QW), lambda b, n: (cur(b, n), 0)),
        compiler_params=_cparams(("parallel", "arbitrary")),
        name="swa_prompt",
    )(proj, proj, kv, kv, kv, kv, bias, sinks)


def _swa_sample_kernel(q_ref, z_ref, kn_ref, vn_ref, ck_ref, cv_ref, bias_ref, sink_ref, o_ref, nk_ref, nv_ref,
                       *, n_seq_blk, n_new):
    scale = SWA_DH ** -0.5
    wb = ck_ref.shape[3]
    probs = [(s, kv) for s in range(n_seq_blk) for kv in range(SWA_KV)]
    ks = lambda kv: slice(kv * SWA_DH, (kv + 1) * SWA_DH)
    slot = lax.broadcasted_iota(jnp.int32, (SWA_DH, wb), 1)
    tok = lax.broadcasted_iota(jnp.int32, (kn_ref.shape[1], wb), 0)
    tok_slot = lax.broadcasted_iota(jnp.int32, (kn_ref.shape[1], wb), 1)
    place = jnp.where((tok_slot == tok + (wb - n_new)) & (tok < n_new), 1.0, 0.0).astype(F32)
    for s in range(n_seq_blk):
        for cache_ref, new_ref, out_ref in ((ck_ref, kn_ref, nk_ref), (cv_ref, vn_ref, nv_ref)):
            new_t = lax.dot_general(new_ref[s], place, (((0,), (0,)), ((), ())),
                                    preferred_element_type=F32, precision=lax.Precision.HIGHEST)
            for kv in range(SWA_KV):
                out_ref[s, kv] = jnp.where(slot >= wb - n_new, new_t[ks(kv), :],
                                           pltpu.roll(cache_ref[s, kv], wb - n_new, axis=1))
    lc = [_dot(q_ref[s, kv] * scale, ck_ref[s, kv]) + bias_ref[kv, :, 0:wb] for s, kv in probs]
    ln = [_dot_nt(q_ref[s, kv] * scale, kn_ref[s, :, ks(kv)]) + bias_ref[kv, :, wb:] for s, kv in probs]
    pcs, pns, dens = [], [], []
    for (s, kv), c, n in zip(probs, lc, ln):
        sink = sink_ref[kv][:, 0:1]
        m = jnp.maximum(jnp.maximum(jnp.max(c, axis=-1, keepdims=True), jnp.max(n, axis=-1, keepdims=True)),
                        sink)
        pc = jnp.exp(c - m)
        pn = jnp.exp(n - m)
        pcs.append(pc)
        pns.append(pn)
        dens.append(jnp.sum(pc, axis=-1, keepdims=True) + jnp.sum(pn, axis=-1, keepdims=True)
                    + jnp.exp(sink - m))
    outs = [(_dot_nt(pc, cv_ref[s, kv]) + _dot(pn, vn_ref[s, :, ks(kv)])) / den
            for (s, kv), pc, pn, den in zip(probs, pcs, pns, dens)]
    for oh, (s, kv) in zip(outs, probs):
        o_ref[s, kv] = (oh * _silu(z_ref[s, kv].astype(F32))).astype(o_ref.dtype)


def _swa_sample(q, z, k_new, v_new, cache_kt, cache_vt, bias, sink_rows, n_new, n_seq_blk=8):
    bd, _, rows, _ = q.shape
    wb = cache_kt.shape[3]
    npad = k_new.shape[1]
    assert wb == LANE
    blk4 = lambda i: (i, 0, 0, 0)
    blk3 = lambda i: (i, 0, 0)
    cache_spec = pl.BlockSpec((n_seq_blk, SWA_KV, SWA_DH, wb), blk4)
    return pl.pallas_call(
        functools.partial(_swa_sample_kernel, n_seq_blk=n_seq_blk, n_new=n_new),
        out_shape=(jax.ShapeDtypeStruct(q.shape, BF16),
                   jax.ShapeDtypeStruct(cache_kt.shape, F32), jax.ShapeDtypeStruct(cache_vt.shape, F32)),
        grid=(bd // n_seq_blk,),
        in_specs=[pl.BlockSpec((n_seq_blk, SWA_KV, rows, SWA_DH), blk4),
                  pl.BlockSpec((n_seq_blk, SWA_KV, rows, SWA_DH), blk4),
                  pl.BlockSpec((n_seq_blk, npad, SWA_KVW), blk3),
                  pl.BlockSpec((n_seq_blk, npad, SWA_KVW), blk3),
                  pl.BlockSpec((n_seq_blk, SWA_KV, SWA_DH, wb), blk4),
                  pl.BlockSpec((n_seq_blk, SWA_KV, SWA_DH, wb), blk4),
                  pl.BlockSpec((SWA_KV, rows, wb + npad), lambda i: (0, 0, 0)),
                  pl.BlockSpec((SWA_KV, rows, LANE), lambda i: (0, 0, 0))],
        out_specs=(pl.BlockSpec((n_seq_blk, SWA_KV, rows, SWA_DH), blk4), cache_spec, cache_spec),
        compiler_params=_cparams(("arbitrary",)),
        name="swa_sample",
    )(q, z, k_new, v_new, cache_kt, cache_vt, bias, sink_rows)


def _mem_attend(probs, q_of, z_of, k_of, v_of, store):
    scale = MEM_DH ** -0.5
    logits = [_dot_nt(q_of(p) * scale, k_of(p)) for p in probs]
    ps, dens = [], []
    for l in logits:
        m = jnp.max(l, axis=-1, keepdims=True)
        e = jnp.exp(l - m)
        dens.append(jnp.sum(e, axis=-1, keepdims=True))
        ps.append(e)
    outs = [_dot(e, v_of(p)) / den for e, den, p in zip(ps, dens, probs)]
    for p, oh in zip(probs, outs):
        store(p, oh * _silu(z_of(p).astype(F32)))


def _mem_cols(h):
    return slice(h * MEM_DH, (h + 1) * MEM_DH)


def _mem_prompt_kernel(q_ref, z_ref, k_ref, v_ref, o_ref):
    def store(h, val):
        o_ref[:, _mem_cols(h)] = val.astype(o_ref.dtype)
    _mem_attend(range(MEM_H), lambda h: q_ref[:, _mem_cols(h)], lambda h: z_ref[:, _mem_cols(h)],
                lambda h: k_ref[:, _mem_cols(h)], lambda h: v_ref[:, _mem_cols(h)], store)


def _mem_prompt(proj, mkv, n_seq, seq_len, q_colblock, z_colblock, tq=2048):
    tq = math.gcd(tq, seq_len)
    steps = seq_len // tq
    return pl.pallas_call(
        _mem_prompt_kernel,
        out_shape=jax.ShapeDtypeStruct((n_seq * seq_len, MEM_W), BF16),
        grid=(n_seq, steps),
        in_specs=[pl.BlockSpec((tq, MEM_W), lambda b, n: (b * steps + n, q_colblock)),
                  pl.BlockSpec((tq, MEM_W), lambda b, n: (b * steps + n, z_colblock)),
                  pl.BlockSpec((N_MEM, MEM_W), lambda b, n: (b, 0)),
                  pl.BlockSpec((N_MEM, MEM_W), lambda b, n: (b, 1))],
        out_specs=pl.BlockSpec((tq, MEM_W), lambda b, n: (b * steps + n, 0)),
        compiler_params=_cparams(("parallel", "arbitrary")),
        name="mem_prompt",
    )(proj, proj, mkv, mkv)


def _mem_sample_kernel(q_ref, z_ref, k_hbm, v_hbm, o_ref, kbuf, vbuf, sem, *, n_seq_blk):
    i = pl.program_id(0)
    n_steps = pl.num_programs(0)
    slot = i % 2

    def copies(step, slot_):
        seqs = pl.ds(step * n_seq_blk, n_seq_blk)
        out = []
        for h in range(MEM_H):
            out.append(pltpu.make_async_copy(k_hbm.at[seqs, :, h, :], kbuf.at[slot_, h], sem.at[0, slot_, h]))
            out.append(pltpu.make_async_copy(v_hbm.at[seqs, :, h, :], vbuf.at[slot_, h], sem.at[1, slot_, h]))
        return out

    def start_all(cps):
        for n, cp in enumerate(cps):
            cp.start(priority=n % 2)

    @pl.when(i == 0)
    def _():
        start_all(copies(0, 0))

    @pl.when(i + 1 < n_steps)
    def _():
        start_all(copies(i + 1, 1 - slot))

    for cp in copies(i, slot):
        cp.wait()

    def store(p, val):
        o_ref[p[0], :, _mem_cols(p[1])] = val.astype(o_ref.dtype)
    probs = [(s, h) for s in range(n_seq_blk) for h in range(MEM_H)]
    _mem_attend(probs, lambda p: q_ref[p[0], :, _mem_cols(p[1])], lambda p: z_ref[p[0], :, _mem_cols(p[1])],
                lambda p: kbuf[slot, p[1], p[0]], lambda p: vbuf[slot, p[1], p[0]], store)


def _mem_sample(q, z, cache_k, cache_v, n_seq_blk=8):
    bd, rows, _ = q.shape
    blk = lambda i: (i, 0, 0)
    buf = pltpu.VMEM((2, MEM_H, n_seq_blk, N_MEM, MEM_DH), cache_k.dtype)
    return pl.pallas_call(
        functools.partial(_mem_sample_kernel, n_seq_blk=n_seq_blk),
        out_shape=jax.ShapeDtypeStruct(q.shape, BF16),
        grid=(bd // n_seq_blk,),
        in_specs=[pl.BlockSpec((n_seq_blk, rows, MEM_W), blk),
                  pl.BlockSpec((n_seq_blk, rows, MEM_W), blk),
                  pl.BlockSpec(memory_space=pl.ANY),
                  pl.BlockSpec(memory_space=pl.ANY)],
        out_specs=pl.BlockSpec((n_seq_blk, rows, MEM_W), blk),
        scratch_shapes=[buf, buf, pltpu.SemaphoreType.DMA((2, 2, MEM_H))],
        compiler_params=_cparams(("arbitrary",)),
        name="mem_sample",
    )(q, z, cache_k, cache_v)


def _merge_kernel(og_ref, os_ref, om_ref, gate_ref, x_ref, wb_ref, wo_ref, nf_ref, y_ref):
    merged = None
    for b, o_ref in enumerate((og_ref, os_ref, om_ref)):
        t = jnp.dot(o_ref[...], wb_ref[b], preferred_element_type=F32)
        t = t * gate_ref[:, b * D_MODEL:(b + 1) * D_MODEL].astype(F32)
        merged = t if merged is None else merged + t
    h = x_ref[...] + jnp.dot(merged.astype(BF16), wo_ref[...], preferred_element_type=F32)
    ms = jnp.mean(h * h, axis=-1, keepdims=True)
    y_ref[...] = h * lax.rsqrt(ms + NORM_EPS) * nf_ref[...]


def _merge(o_gdn, o_swa, o_mem, gates, x, w_branch, w_out, norm_f, tm=256):
    m = x.shape[0]
    tm = min(tm, m)
    row = lambda i: (i, 0)
    const2 = lambda i: (0, 0)
    return pl.pallas_call(
        _merge_kernel,
        out_shape=jax.ShapeDtypeStruct((m, D_MODEL), F32),
        grid=(m // tm,),
        in_specs=[pl.BlockSpec((tm, BR_W), row),
                  pl.BlockSpec((tm, BR_W), row),
                  pl.BlockSpec((tm, BR_W), row),
                  pl.BlockSpec((tm, N_BRANCH * D_MODEL), row),
                  pl.BlockSpec((tm, D_MODEL), row),
                  pl.BlockSpec((N_BRANCH, BR_W, D_MODEL), lambda i: (0, 0, 0),
                               pipeline_mode=pl.Buffered(1)),
                  pl.BlockSpec((D_MODEL, D_MODEL), const2, pipeline_mode=pl.Buffered(1)),
                  pl.BlockSpec((1, D_MODEL), const2)],
        out_specs=pl.BlockSpec((tm, D_MODEL), row),
        compiler_params=_cparams(("parallel",)),
        name="merge",
    )(o_gdn, o_swa, o_mem, gates, x, w_branch, w_out, norm_f)


_IN_SIZES = (GDN_QK, GDN_QK, GDN_W, GDN_W, GDN_H, GDN_H, SWA_QW, SWA_KVW, SWA_KVW, SWA_QW,
             MEM_W, MEM_W, N_BRANCH * D_MODEL)
_IN_NAMES = ("gq", "gk", "gv", "gz", "gb", "ga", "sq", "sk", "sv", "sz", "mq", "mz", "mg")
_IN_SPAN = {name: (int(off), int(off + size)) for name, off, size in
            zip(_IN_NAMES, np.cumsum((0,) + _IN_SIZES[:-1]), _IN_SIZES)}


def kernel(x_prompt, x_sample, state_gdn, state_gdn_conv, cache_swa_k, cache_swa_v, cache_mem_k,
           cache_mem_v, mem_prompt, norm_in, w_in, gdn_conv_w, gdn_a_log, gdn_dt_bias, gdn_norm,
           swa_sinks, rel_bias, norm_mem, w_mem_kv, w_branch, w_out, norm_f):
    n_layers = norm_in.shape[0]
    assert n_layers == 1
    b, seq, _ = x_prompt.shape
    bd, ns, _ = x_sample.shape
    wb = cache_swa_k.shape[2]
    assert seq % WINDOW == 0 and seq % GDN_CHUNK == 0 and ns + CONV_W <= SUBLANE and wb == WINDOW
    lyr = 0

    w = w_in[lyr]
    main_names = ("gq", "gk", "gv", "gz", "sq", "sz", "mq", "mz")
    n_main = len(main_names) * W_BLK
    n_gate = N_BRANCH * D_MODEL
    n_small = 3 * LANE
    small_lead = -(n_main + n_gate) % n_small
    w_all = _wprep(w.T, [[(_IN_SPAN[a][0], W_BLK)] for a in main_names]
                   + [[(_IN_SPAN["mg"][0] + W_BLK * k, W_BLK)] for k in range(n_gate // W_BLK)]
                   + [[(None, small_lead), (_IN_SPAN["sk"][0], 2 * SWA_KVW), (_IN_SPAN["gb"][0], 2 * GDN_H)]])
    col_small = n_main + n_gate + small_lead
    cb_gz, cb_sq, cb_sz, cb_mq, cb_mz = 3, 4, 5, 6, 7
    cb_sk, cb_sv, cb_ba = 0, 1, 2
    w_mkv = w_mem_kv[lyr]
    w_br = w_branch[lyr].astype(BF16)
    w_o = w_out[lyr].astype(BF16)
    nw_in = norm_in[lyr].reshape(1, D_MODEL)
    nw_mem = norm_mem[lyr].reshape(1, D_MODEL)
    nw_f = norm_f.reshape(1, D_MODEL)
    conv_w = gdn_conv_w[lyr]
    a_row = jnp.pad(gdn_a_log[lyr].reshape(1, GDN_H), ((0, 0), (GDN_H, LANE - 2 * GDN_H)))
    dt_row = jnp.pad(gdn_dt_bias[lyr].reshape(1, GDN_H), ((0, 0), (GDN_H, LANE - 2 * GDN_H)))
    gnw = gdn_norm[lyr].reshape(1, GDN_DV)
    sinks = swa_sinks[lyr]
    bias_p = _bias_prompt(rel_bias)
    npad = SUBLANE
    bias_s = _bias_sample(rel_bias, ns, wb, wb + npad)
    sink_rows = jnp.broadcast_to(jnp.repeat(sinks.reshape(SWA_KV, SWA_G), ns, axis=1)[:, :, None],
                                 (SWA_KV, SWA_G * ns, LANE))

    t = b * seq
    xp = x_prompt.reshape(t, D_MODEL)
    xn_p, p_small = _rmsnorm(xp, nw_in, w_all, col_small, n_small)
    p_main = _proj(xn_p, w_all, BF16, n=n_main)
    g_p = _proj(xn_p, w_all, BF16, act="sigmoid", col0=n_main, n=n_gate)
    xn_tail = xn_p.reshape(b, seq, D_MODEL)[:, seq - SUBLANE:, :].reshape(b * SUBLANE, D_MODEL)
    conv_p = _proj(xn_tail, w_all, F32, n=GDN_CONV_CH).reshape(b, SUBLANE, GDN_CONV_CH)[:, SUBLANE - (CONV_W - 1):]
    kv_tail = p_small.reshape(b, seq, 3 * LANE)[:, seq - WINDOW:]
    swk_p = kv_tail[:, :, cb_sk * LANE:(cb_sk + 1) * LANE].reshape(b, WINDOW, SWA_KV, SWA_DH)
    swv_p = kv_tail[:, :, cb_sv * LANE:(cb_sv + 1) * LANE].reshape(b, WINDOW, SWA_KV, SWA_DH)

    mkv = _proj(_rmsnorm(mem_prompt.reshape(b * N_MEM, D_MODEL), nw_mem), w_mkv, F32)
    mk_p = mkv[:, :MEM_W].reshape(b, N_MEM, MEM_W)
    mv_p = mkv[:, MEM_W:].reshape(b, N_MEM, MEM_W)

    p_main3 = p_main.reshape(b, seq, n_main)
    gdn_cps = 4
    o_gdn_p, s_p = _gdn(p_main3, p_main3, p_small.reshape(b, seq, n_small),
                        jnp.zeros((b, HDR, GDN_CONV_CH), F32),
                        jnp.zeros((b, GDN_H, GDN_DK, GDN_DV), F32), conv_w, a_row, dt_row, gnw,
                        chunk=GDN_CHUNK, n_chunks=gdn_cps * b, chunks_per_seq=gdn_cps,
                        per_chunk_state=False, valid_lo=0, valid_hi=GDN_CHUNK, out_dtype=BF16,
                        z_colblock=cb_gz, ba_colblock=cb_ba)
    o_gdn_p = o_gdn_p.reshape(t, GDN_W)
    o_swa_p = _swa_prompt(p_main, p_small, bias_p, sinks, b, seq, cb_sq, cb_sz, cb_sk, cb_sv)
    o_mem_p = _mem_prompt(p_main, mkv, b, seq, cb_mq, cb_mz)
    y_p = _merge(o_gdn_p, o_swa_p, o_mem_p, g_p, xp, w_br, w_o, nw_f).reshape(b, seq, D_MODEL)

    ts = bd * ns
    xs = x_sample.reshape(ts, D_MODEL)
    xn_s, s_small = _rmsnorm(xs, nw_in, w_all, col_small, n_small)
    s_main = _proj(xn_s, w_all, F32, n=n_main)
    g_s = _proj(xn_s, w_all, BF16, act="sigmoid", col0=n_main, n=n_gate)
    s_gdn = s_main[:, :GDN_CONV_CH + GDN_W]
    s_ba = s_small[:, cb_ba * LANE:(cb_ba + 1) * LANE]
    s_swa = jnp.concatenate([s_main[:, cb_sq * BR_W:(cb_sz + 1) * BR_W], s_small[:, :2 * LANE]], axis=1)
    s_mem = s_main[:, cb_mq * BR_W:(cb_mz + 1) * BR_W].astype(BF16)

    lo = CONV_W - 1
    hi = lo + ns
    pad_rows = ((0, 0), (lo, SUBLANE - hi), (0, 0))
    e_qkv = jnp.concatenate([state_gdn_conv[lyr], s_gdn[:, :GDN_CONV_CH].reshape(bd, ns, GDN_CONV_CH),
                             jnp.zeros((bd, SUBLANE - hi, GDN_CONV_CH), F32)], axis=1)
    e_z = jnp.pad(s_gdn[:, GDN_CONV_CH:].reshape(bd, ns, GDN_W), pad_rows)
    e_ba = jnp.pad(s_ba.reshape(bd, ns, LANE), pad_rows)
    seq_blk = 16
    o_gdn_s8, s_s = _gdn(e_qkv.reshape(bd * SUBLANE, GDN_CONV_CH), e_z.reshape(bd * SUBLANE, GDN_W),
                         e_ba.reshape(bd * SUBLANE, LANE), jnp.zeros((1, HDR, GDN_CONV_CH), F32),
                         state_gdn[lyr], conv_w, a_row, dt_row, gnw,
                         chunk=SUBLANE, n_chunks=seq_blk, chunks_per_seq=1, per_chunk_state=True,
                         valid_lo=lo, valid_hi=hi, out_dtype=BF16)
    o_gdn_s = o_gdn_s8.reshape(bd, SUBLANE, GDN_W)[:, lo:hi].reshape(ts, GDN_W)
    conv_s = e_qkv[:, hi - (CONV_W - 1):hi]

    def to_heads(a):
        return a.reshape(bd, ns, SWA_KV, SWA_G, SWA_DH).transpose(0, 2, 3, 1, 4).reshape(
            bd, SWA_KV, SWA_G * ns, SWA_DH)

    k_new = s_swa[:, 2 * SWA_QW:2 * SWA_QW + SWA_KVW].reshape(bd, ns, SWA_KVW)
    v_new = s_swa[:, 2 * SWA_QW + SWA_KVW:].reshape(bd, ns, SWA_KVW)
    tok_pad = ((0, 0), (0, npad - ns), (0, 0))
    ck_t = cache_swa_k[lyr].transpose(0, 2, 3, 1)
    cv_t = cache_swa_v[lyr].transpose(0, 2, 3, 1)
    o_swa_h, nk_t, nv_t = _swa_sample(
        to_heads(s_swa[:, :SWA_QW]).astype(BF16), to_heads(s_swa[:, SWA_QW:2 * SWA_QW]),
        jnp.pad(k_new, tok_pad), jnp.pad(v_new, tok_pad), ck_t, cv_t, bias_s, sink_rows, n_new=ns)
    o_swa_s = o_swa_h.reshape(bd, SWA_KV, SWA_G, ns, SWA_DH).transpose(0, 3, 1, 2, 4).reshape(ts, SWA_QW)
    swk_s = nk_t.transpose(0, 3, 1, 2)
    swv_s = nv_t.transpose(0, 3, 1, 2)

    mq = jnp.pad(s_mem[:, :MEM_W].reshape(bd, ns, MEM_W), tok_pad)
    mz = jnp.pad(s_mem[:, MEM_W:].reshape(bd, ns, MEM_W), tok_pad)
    o_mem_s = _mem_sample(mq, mz, cache_mem_k[lyr], cache_mem_v[lyr])[:, :ns].reshape(ts, MEM_W)
    y_s = _merge(o_gdn_s, o_swa_s, o_mem_s, g_s, xs, w_br, w_o, nw_f).reshape(bd, ns, D_MODEL)

    return (y_p, y_s,
            s_p[None], conv_p[None], swk_p[None], swv_p[None],
            mk_p.reshape(b, N_MEM, MEM_H, MEM_DH)[None], mv_p.reshape(b, N_MEM, MEM_H, MEM_DH)[None],
            s_s[None], conv_s[None], swk_s[None], swv_s[None])
```
